```python
import math
import jax, jax.numpy as jnp
from jax import lax
import numpy as np

D_MODEL = 1024
BATCH = 2
SEQ = 8192
DEPTH = 1

HEAD_DIM = 64
NSA_HEADS = 8
NSA_KV_GROUPS = 2
NSA_HPG = NSA_HEADS // NSA_KV_GROUPS
FOX_HEADS = 8
CMP_LEN = 32
CMP_STRIDE = 16
SEL_LEN = 64
N_SEL = 16
WINDOW = 512
Q_BLOCK = 128
N_EXPERT_GROUPS = 4
EXPERTS_PER_GROUP = 8
N_EXPERTS = N_EXPERT_GROUPS * EXPERTS_PER_GROUP
EXPERT_TOP_K = 2
D_EXPERT = D_MODEL // 2
MOE_BLOCK = 128
NORM_EPS = 1e-6
NEG = -1e30
FORCE = 1e9

NSA_W = NSA_HEADS * HEAD_DIM
NSA_KV_W = NSA_KV_GROUPS * HEAD_DIM
FOX_W = FOX_HEADS * HEAD_DIM
IN_SIZES = (NSA_W, 6 * NSA_KV_W, 3 * NSA_HEADS, 3 * FOX_W, FOX_HEADS, 2 * D_MODEL)
D_IN = sum(IN_SIZES)
IN_SPLITS = tuple(int(v) for v in np.cumsum(IN_SIZES)[:-1])

kernel_name = "hybrid_nsa_fox_hmoe_block"


def rmsnorm(x, g):
    xf = x.astype(jnp.float32)
    y = xf * lax.rsqrt(jnp.mean(xf * xf, axis=-1, keepdims=True) + NORM_EPS)
    return (y * g.astype(jnp.float32)).astype(x.dtype)


def masked_softmax(logits, mask):
    l = jnp.where(mask, logits.astype(jnp.float32), NEG)
    m = jnp.max(l, axis=-1, keepdims=True)
    e = jnp.where(mask, jnp.exp(l - m), 0.0)
    return e / jnp.maximum(jnp.sum(e, axis=-1, keepdims=True), 1e-30)


def alibi_slopes(n):
    return jnp.exp2(-8.0 * jnp.arange(1, n + 1, dtype=jnp.float32) / n)


def compress_blocks(kv, pe, w1, w2):
    S = kv.shape[2]
    n_cmp = (S - CMP_LEN) // CMP_STRIDE + 1
    idx = jnp.arange(n_cmp)[:, None] * CMP_STRIDE + jnp.arange(CMP_LEN)[None, :]
    blocks = kv[:, :, idx] + pe
    hid = jax.nn.gelu(jnp.einsum('bgnld,lde->bgne', blocks, w1))
    return hid @ w2


def nsa_attention(q, k_cmp, v_cmp, k_slc, v_slc, k_win, v_win, gates,
                  pe_k, w1_k, w2_k, pe_v, w1_v, w2_v):
    B, S = q.shape[0], q.shape[1]
    G, HPG, dh = NSA_KV_GROUPS, NSA_HPG, HEAD_DIM
    n_cmp = (S - CMP_LEN) // CMP_STRIDE + 1
    n_slc = S // SEL_LEN
    n_sel = min(N_SEL, n_slc)
    scale = HEAD_DIM ** -0.5
    tr = lambda a: a.transpose(0, 2, 1, 3)
    kc = compress_blocks(tr(k_cmp), pe_k, w1_k, w2_k)
    vc = compress_blocks(tr(v_cmp), pe_v, w1_v, w2_v)
    cmp_start = jnp.arange(n_cmp) * CMP_STRIDE
    cmp_end = cmp_start + CMP_LEN - 1
    slc_start = jnp.arange(n_slc) * SEL_LEN
    overlap = ((cmp_start[:, None] < slc_start[None, :] + SEL_LEN)
               & (cmp_end[:, None] >= slc_start[None, :])).astype(jnp.float32)
    ks_blocks = tr(k_slc).reshape(B, G, n_slc, SEL_LEN * dh)
    vs_blocks = tr(v_slc).reshape(B, G, n_slc, SEL_LEN * dh)
    pad = ((0, 0), (0, 0), (WINDOW, 0), (0, 0))
    kw = jnp.pad(tr(k_win), pad)
    vw = jnp.pad(tr(v_win), pad)
    qg = q.reshape(B, S, G, HPG, dh).transpose(0, 2, 3, 1, 4)
    gt = gates.reshape(B, S, G, HPG, 3).transpose(0, 2, 3, 1, 4)
    slopes = alibi_slopes(NSA_HEADS).reshape(G, HPG)[None, :, :, None, None]
    j_idx = jnp.arange(n_slc)

    def block(i):
        q0 = i * Q_BLOCK
        qb = lax.dynamic_slice_in_dim(qg, q0, Q_BLOCK, axis=3)
        gb = lax.dynamic_slice_in_dim(gt, q0, Q_BLOCK, axis=3)
        t = q0 + jnp.arange(Q_BLOCK)
        dc = (t[:, None] - cmp_end[None, :]).astype(jnp.float32)
        lc = jnp.einsum('bghqd,bgnd->bghqn', qb, kc).astype(jnp.float32) * scale - slopes * dc
        pc = masked_softmax(lc, dc >= 0)
        oc = jnp.einsum('bghqn,bgnd->bghqd', pc.astype(vc.dtype), vc)
        imp = jnp.einsum('bghqn,nj->bgqj', pc, overlap)
        cur = (t // SEL_LEN)[:, None]
        forced = (j_idx[None] == 0) | (j_idx[None] == cur) | (j_idx[None] == cur - 1)
        imp = jnp.where(j_idx[None] > cur, -FORCE, jnp.where(forced, FORCE, imp))
        _, sel = lax.top_k(imp, n_sel)
        flat = sel.reshape(B, G, Q_BLOCK * n_sel, 1)
        ks = jnp.take_along_axis(ks_blocks, flat, axis=2).reshape(B, G, Q_BLOCK, n_sel * SEL_LEN, dh)
        vs = jnp.take_along_axis(vs_blocks, flat, axis=2).reshape(B, G, Q_BLOCK, n_sel * SEL_LEN, dh)
        pos = (sel[..., None] * SEL_LEN + jnp.arange(SEL_LEN)).reshape(B, G, Q_BLOCK, n_sel * SEL_LEN)
        ds = (t[:, None] - pos)[:, :, None]
        ls = jnp.einsum('bghqd,bgqkd->bghqk', qb, ks).astype(jnp.float32) * scale - slopes * ds.astype(jnp.float32)
        ps = masked_softmax(ls, ds >= 0)
        osel = jnp.einsum('bghqk,bgqkd->bghqd', ps.astype(vs.dtype), vs)
        kwb = lax.dynamic_slice_in_dim(kw, q0, WINDOW + Q_BLOCK, axis=2)
        vwb = lax.dynamic_slice_in_dim(vw, q0, WINDOW + Q_BLOCK, axis=2)
        s = q0 - WINDOW + jnp.arange(WINDOW + Q_BLOCK)
        dw = t[:, None] - s[None, :]
        mw = (dw >= 0) & (dw < WINDOW) & (s[None, :] >= 0)
        lw = jnp.einsum('bghqd,bgkd->bghqk', qb, kwb).astype(jnp.float32) * scale - slopes * dw.astype(jnp.float32)
        pw = masked_softmax(lw, mw)
        ow = jnp.einsum('bghqk,bgkd->bghqd', pw.astype(vwb.dtype), vwb)
        return gb[..., 0:1] * oc + gb[..., 1:2] * osel + gb[..., 2:3] * ow

    out = lax.map(block, jnp.arange(S // Q_BLOCK))
    return out.transpose(1, 0, 4, 2, 3, 5).reshape(B, S, NSA_W)


def forgetting_attention(q, k, v, f_logit):
    B, S = q.shape[0], q.shape[1]
    scale = HEAD_DIM ** -0.5
    F = jnp.cumsum(jax.nn.log_sigmoid(f_logit.astype(jnp.float32)), axis=1).transpose(0, 2, 1)
    qh, kh, vh = (a.transpose(0, 2, 1, 3) for a in (q, k, v))
    outs = []
    for i in range(S // Q_BLOCK):
        q0, kend = i * Q_BLOCK, (i + 1) * Q_BLOCK
        l = (jnp.einsum('bhqd,bhkd->bhqk', qh[:, :, q0:kend], kh[:, :, :kend]).astype(jnp.float32) * scale
             + F[:, :, q0:kend, None] - F[:, :, None, :kend])
        mask = jnp.arange(kend)[None, :] <= jnp.arange(q0, kend)[:, None]
        p = masked_softmax(l, mask)
        outs.append(jnp.einsum('bhqk,bhkd->bhqd', p.astype(vh.dtype), vh[:, :, :kend]))
    o = jnp.concatenate(outs, axis=2)
    return o.transpose(0, 2, 1, 3).reshape(B, S, FOX_W)


def hierarchical_moe(h, w_rg, b_rg, w_re, b_re, w_gate, w_up, w_down):
    T, D = h.shape
    pg = jax.nn.softmax((h @ w_rg + b_rg).astype(jnp.float32), axis=-1)
    pg_top, g_idx = lax.top_k(pg, 1)
    le = (h @ w_re + b_re).astype(jnp.float32).reshape(T, N_EXPERT_GROUPS, EXPERTS_PER_GROUP)
    le = jnp.take_along_axis(le, g_idx[:, :, None], axis=1)[:, 0]
    top_l, e_local = lax.top_k(le, EXPERT_TOP_K)
    weight = pg_top * jax.nn.softmax(top_l, axis=-1)
    expert = g_idx * EXPERTS_PER_GROUP + e_local
    A = T * EXPERT_TOP_K
    e_flat = expert.reshape(A)
    w_flat = weight.reshape(A)
    tok = jnp.arange(A, dtype=jnp.int32) // EXPERT_TOP_K
    order = jnp.argsort(e_flat)
    e_s, tok_s, w_s = e_flat[order], tok[order], w_flat[order]
    counts = jnp.bincount(e_flat, length=N_EXPERTS)
    padded = (counts + MOE_BLOCK - 1) // MOE_BLOCK * MOE_BLOCK
    pad_end = jnp.cumsum(padded)
    pad_start = pad_end - padded
    raw_start = jnp.cumsum(counts) - counts
    dest = pad_start[e_s] + jnp.arange(A) - raw_start[e_s]
    cap = -(-(A + N_EXPERTS * (MOE_BLOCK - 1)) // MOE_BLOCK) * MOE_BLOCK
    n_blocks = cap // MOE_BLOCK
    buf_tok = jnp.zeros((cap,), jnp.int32).at[dest].set(tok_s)
    buf_w = jnp.zeros((cap,), jnp.float32).at[dest].set(w_s)
    block_expert = jnp.minimum(jnp.searchsorted(pad_end, jnp.arange(n_blocks) * MOE_BLOCK, side='right'), N_EXPERTS - 1)
    xb = h[buf_tok].reshape(n_blocks, MOE_BLOCK, D)

    def expert_block(args):
        xe, e = args
        return (jax.nn.silu(xe @ w_gate[e]) * (xe @ w_up[e])) @ w_down[e]

    yb = lax.map(expert_block, (xb, block_expert)).reshape(cap, D)
    return jnp.zeros((T, D), h.dtype).at[buf_tok].add(yb * buf_w[:, None].astype(h.dtype))


def setup_inputs(seed: int = 0) -> dict:
    key = jax.random.key(seed)
    ks = jax.random.split(key, 26)
    f32 = jnp.float32
    L = DEPTH
    nrm = lambda k, shape, fan: jax.random.normal(k, shape, f32) * fan ** -0.5
    gain = lambda k: 1.0 + 0.05 * jax.random.normal(k, (L, D_MODEL), f32)
    return {
        "x": jax.random.normal(ks[0], (BATCH, SEQ, D_MODEL), f32),
        "c": jax.random.normal(ks[1], (BATCH, D_MODEL), f32),
        "w_ada": 0.5 * nrm(ks[2], (L, D_MODEL, 6 * D_MODEL), D_MODEL),
        "b_ada": 0.02 * jax.random.normal(ks[3], (L, 6 * D_MODEL), f32),
        "g_pre_mix": gain(ks[4]),
        "g_post_mix": gain(ks[5]),
        "g_pre_ffn": gain(ks[6]),
        "g_post_ffn": gain(ks[7]),
        "w_in": nrm(ks[8], (L, D_MODEL, D_IN), D_MODEL),
        "b_forget": jax.random.uniform(ks[9], (L, FOX_HEADS), f32, 1.0, 6.0),
        "cmp_pe_k": 0.1 * jax.random.normal(ks[10], (L, CMP_LEN, HEAD_DIM), f32),
        "cmp_w1_k": nrm(ks[11], (L, CMP_LEN, HEAD_DIM, HEAD_DIM), CMP_LEN * HEAD_DIM),
        "cmp_w2_k": nrm(ks[12], (L, HEAD_DIM, HEAD_DIM), HEAD_DIM),
        "cmp_pe_v": 0.1 * jax.random.normal(ks[13], (L, CMP_LEN, HEAD_DIM), f32),
        "cmp_w1_v": nrm(ks[14], (L, CMP_LEN, HEAD_DIM, HEAD_DIM), CMP_LEN * HEAD_DIM),
        "cmp_w2_v": nrm(ks[15], (L, HEAD_DIM, HEAD_DIM), HEAD_DIM),
        "w_o_nsa": nrm(ks[16], (L, NSA_W, D_MODEL), NSA_W),
        "w_o_fox": nrm(ks[17], (L, FOX_W, D_MODEL), FOX_W),
        "w_out": nrm(ks[18], (L, D_MODEL, D_MODEL), D_MODEL),
        "w_router_group": nrm(ks[19], (L, D_MODEL, N_EXPERT_GROUPS), D_MODEL),
        "b_router_group": 0.01 * jax.random.normal(ks[20], (L, N_EXPERT_GROUPS), f32),
        "w_router_expert": nrm(ks[21], (L, D_MODEL, N_EXPERTS), D_MODEL),
        "b_router_expert": 0.01 * jax.random.normal(ks[22], (L, N_EXPERTS), f32),
        "w_exp_gate": nrm(ks[23], (L, N_EXPERTS, D_MODEL, D_EXPERT), D_MODEL),
        "w_exp_up": nrm(ks[24], (L, N_EXPERTS, D_MODEL, D_EXPERT), D_MODEL),
        "w_exp_down": nrm(ks[25], (L, N_EXPERTS, D_EXPERT, D_MODEL), D_EXPERT),
    }


def reference(x, c, w_ada, b_ada, g_pre_mix, g_post_mix, g_pre_ffn, g_post_ffn, w_in, b_forget,
              cmp_pe_k, cmp_w1_k, cmp_w2_k, cmp_pe_v, cmp_w1_v, cmp_w2_v,
              w_o_nsa, w_o_fox, w_out, w_router_group, b_router_group, w_router_expert, b_router_expert,
              w_exp_gate, w_exp_up, w_exp_down):
    B, S, D = x.shape
    for l in range(DEPTH):
        mod = jax.nn.silu(c) @ w_ada[l] + b_ada[l]
        sh_m, sc_m, gt_m, sh_f, sc_f, gt_f = (m[:, None, :] for m in jnp.split(mod, 6, axis=-1))
        h = rmsnorm(x, g_pre_mix[l]) * (1 + sc_m) + sh_m
        proj = h @ w_in[l]
        q_a, kv_a, gl_a, qkv_b, f_b, mg = jnp.split(proj, IN_SPLITS, axis=-1)
        q_a = q_a.reshape(B, S, NSA_HEADS, HEAD_DIM)
        kcmp, vcmp, kslc, vslc, kwin, vwin = (a.reshape(B, S, NSA_KV_GROUPS, HEAD_DIM)
                                              for a in jnp.split(kv_a, 6, axis=-1))
        g_a = jax.nn.sigmoid(gl_a).reshape(B, S, NSA_HEADS, 3)
        y_a = nsa_attention(q_a, kcmp, vcmp, kslc, vslc, kwin, vwin, g_a,
                            cmp_pe_k[l], cmp_w1_k[l], cmp_w2_k[l], cmp_pe_v[l], cmp_w1_v[l], cmp_w2_v[l])
        q_b, k_b, v_b = (a.reshape(B, S, FOX_HEADS, HEAD_DIM) for a in jnp.split(qkv_b, 3, axis=-1))
        y_b = forgetting_attention(q_b, k_b, v_b, f_b + b_forget[l])
        g_merge_a, g_merge_b = jnp.split(jax.nn.sigmoid(mg), 2, axis=-1)
        mixed = (g_merge_a * (y_a @ w_o_nsa[l]) + g_merge_b * (y_b @ w_o_fox[l])) @ w_out[l]
        x = x + gt_m * rmsnorm(mixed, g_post_mix[l])
        h2 = rmsnorm(x, g_pre_ffn[l]) * (1 + sc_f) + sh_f
        y = hierarchical_moe(h2.reshape(B * S, D), w_router_group[l], b_router_group[l],
                             w_router_expert[l], b_router_expert[l],
                             w_exp_gate[l], w_exp_up[l], w_exp_down[l]).reshape(B, S, D)
        x = x + gt_f * rmsnorm(y, g_post_ffn[l])
    return x
```

```python
import functools

import numpy as np
import jax
import jax.numpy as jnp
from jax import lax
from jax.experimental import pallas as pl
from jax.experimental.pallas import tpu as pltpu

D_MODEL = 1024
HEAD_DIM = 64
NSA_HEADS = 8
NSA_KV_GROUPS = 2
NSA_HPG = NSA_HEADS // NSA_KV_GROUPS
FOX_HEADS = 8
CMP_LEN = 32
CMP_STRIDE = 16
SEL_LEN = 64
N_SEL = 16
WINDOW = 512
N_EXPERT_GROUPS = 4
EXPERTS_PER_GROUP = 8
N_EXPERTS = N_EXPERT_GROUPS * EXPERTS_PER_GROUP
EXPERT_TOP_K = 2
D_EXPERT = D_MODEL // 2
NORM_EPS = 1e-6
NEG = -1e30
FORCE = 1e9

NSA_W = NSA_HEADS * HEAD_DIM
NSA_KV_W = NSA_KV_GROUPS * HEAD_DIM
FOX_W = FOX_HEADS * HEAD_DIM
IN_SIZES = (NSA_W, 6 * NSA_KV_W, 3 * NSA_HEADS, 3 * FOX_W, FOX_HEADS, 2 * D_MODEL)
IN_SPLITS = tuple(int(v) for v in np.cumsum(IN_SIZES)[:-1])

LANES = 128
Q_TILE = 128
K_TILE = 128
N_CMP_PAD = 512
MOE_TILE = 256
VMEM_LIMIT = 56 * 1024 * 1024

F32 = jnp.float32
BF16 = jnp.bfloat16


def _dot(a, b):
    return jnp.dot(a, b, preferred_element_type=F32)


def _dot_nt(a, b):
    return lax.dot_general(a, b, (((1,), (1,)), ((), ())), preferred_element_type=F32)


def _rms(x, g):
    return x * lax.rsqrt(jnp.mean(x * x, axis=-1, keepdims=True) + NORM_EPS) * g


def _cparams(sem):
    return pltpu.CompilerParams(dimension_semantics=sem, vmem_limit_bytes=VMEM_LIMIT)


def _inproj_kernel(x_ref, mod_ref, g_ref, wb_ref, ws_ref, bf_ref,
                   qa_ref, kva_ref, fox_ref, mg_ref, sm_ref):
    x = x_ref[0]
    h = _rms(x, g_ref[...]) * (1.0 + mod_ref[0, 1:2, :]) + mod_ref[0, 0:1, :]
    hb = h.astype(BF16)
    acc = _dot(hb, wb_ref[:, 0:NSA_W]) * (HEAD_DIM ** -0.5)
    for i in range(NSA_HEADS):
        qa_ref[0, i] = acc[:, i * HEAD_DIM:(i + 1) * HEAD_DIM].astype(BF16)
    off = NSA_W
    for c in range(3):
        acc = _dot(hb, wb_ref[:, off + c * 256: off + (c + 1) * 256])
        for i in range(4):
            kva_ref[0, 4 * c + i] = acc[:, i * HEAD_DIM:(i + 1) * HEAD_DIM].astype(BF16)
    off += 6 * NSA_KV_W
    for c in range(3):
        acc = _dot(hb, wb_ref[:, off + c * FOX_W: off + (c + 1) * FOX_W])
        if c == 0:
            acc = acc * (HEAD_DIM ** -0.5)
        for i in range(FOX_HEADS):
            fox_ref[0, FOX_HEADS * c + i] = acc[:, i * HEAD_DIM:(i + 1) * HEAD_DIM].astype(BF16)
    off += 3 * FOX_W
    for c in range(4):
        acc = _dot(hb, wb_ref[:, off + c * 512: off + (c + 1) * 512])
        mg_ref[0, :, c * 512:(c + 1) * 512] = jax.nn.sigmoid(acc).astype(BF16)
    z = _dot(hb, ws_ref[...]) + bf_ref[...]
    col = lax.broadcasted_iota(jnp.int32, z.shape, 1)
    logsig = jnp.minimum(z, 0.0) - jnp.log1p(jnp.exp(-jnp.abs(z)))
    sm_ref[0] = jnp.where(col < 3 * NSA_HEADS, jax.nn.sigmoid(z), logsig)


def _inproj(x, mod, g, wb, ws, bfp, tm=512):
    B, S, D = x.shape
    nb = wb.shape[1]
    grid = (B, S // tm)
    const2 = lambda b, i: (0, 0)
    return pl.pallas_call(
        _inproj_kernel,
        grid=grid,
        in_specs=[
            pl.BlockSpec((1, tm, D), lambda b, i: (b, i, 0)),
            pl.BlockSpec((1, 6, D), lambda b, i: (b, 0, 0)),
            pl.BlockSpec((1, D), const2),
            pl.BlockSpec((D, nb), const2),
            pl.BlockSpec((D, LANES), const2),
            pl.BlockSpec((1, LANES), const2),
        ],
        out_specs=[
            pl.BlockSpec((1, NSA_HEADS, tm, HEAD_DIM), lambda b, i: (b, 0, i, 0)),
            pl.BlockSpec((1, 12, tm, HEAD_DIM), lambda b, i: (b, 0, i, 0)),
            pl.BlockSpec((1, 3 * FOX_HEADS, tm, HEAD_DIM), lambda b, i: (b, 0, i, 0)),
            pl.BlockSpec((1, tm, 2 * D), lambda b, i: (b, i, 0)),
            pl.BlockSpec((1, tm, LANES), lambda b, i: (b, i, 0)),
        ],
        out_shape=[
            jax.ShapeDtypeStruct((B, NSA_HEADS, S, HEAD_DIM), BF16),
            jax.ShapeDtypeStruct((B, 12, S, HEAD_DIM), BF16),
            jax.ShapeDtypeStruct((B, 3 * FOX_HEADS, S, HEAD_DIM), BF16),
            jax.ShapeDtypeStruct((B, S, 2 * D), BF16),
            jax.ShapeDtypeStruct((B, S, LANES), F32),
        ],
        compiler_params=_cparams(("parallel", "parallel")),
        name="inproj",
    )(x, mod, g, wb, ws, bfp)


def _compress_kernel(x_ref, pe_ref, w1_ref, w2_ref, o_ref):
    x = x_ref[0, 0].astype(F32)
    x_lo = (x + pe_ref[0, 0]).astype(BF16)
    x_hi = (x + pe_ref[0, 1]).astype(BF16)
    y_lo = _dot(x_lo, w1_ref[0, 0])
    y_hi = _dot(x_hi, w1_ref[0, 1])
    n = y_hi.shape[0]
    hid = y_lo + pltpu.roll(y_hi, n - 1, 0)
    hid = jax.nn.gelu(hid)
    o_ref[0, 0] = _dot(hid.astype(BF16), w2_ref[0]).astype(BF16)


def _compress(kv_rows, pe, w1, w2):
    B = kv_rows.shape[0]
    R, C = kv_rows.shape[2], kv_rows.shape[3]
    return pl.pallas_call(
        _compress_kernel,
        grid=(B, 4),
        in_specs=[
            pl.BlockSpec((1, 1, R, C), lambda b, p: (b, p, 0, 0)),
            pl.BlockSpec((1, 2, 1, C), lambda b, p: (p // 2, 0, 0, 0)),
            pl.BlockSpec((1, 2, C, HEAD_DIM), lambda b, p: (p // 2, 0, 0, 0)),
            pl.BlockSpec((1, HEAD_DIM, HEAD_DIM), lambda b, p: (p // 2, 0, 0)),
        ],
        out_specs=pl.BlockSpec((1, 1, R, HEAD_DIM), lambda b, p: (b, p, 0, 0)),
        out_shape=jax.ShapeDtypeStruct((B, 4, R, HEAD_DIM), BF16),
        compiler_params=_cparams(("parallel", "parallel")),
        name="compress",
    )(kv_rows, pe, w1, w2)


def _head_rows():
    row = lax.broadcasted_iota(jnp.int32, (NSA_HPG * Q_TILE, 1), 0)
    return row // Q_TILE, row % Q_TILE


def _slopes(g, hl):
    return jnp.exp2(-(NSA_HPG * g + hl + 1).astype(F32))


def _gate_rows(sm, g, branch):
    col = lax.broadcasted_iota(jnp.int32, sm.shape, 1)
    parts = []
    for hl in range(NSA_HPG):
        want = 3 * (NSA_HPG * g + hl) + branch
        parts.append(jnp.sum(jnp.where(col == want, sm, 0.0), axis=-1, keepdims=True))
    return jnp.concatenate(parts, axis=0)


def _cmp_kernel(q_ref, kc_ref, vc_ref, sm_ref, ov_ref, oc_ref, selb_ref, flag_ref):
    g = pl.program_id(1)
    qb = pl.program_id(2)
    q0 = qb * Q_TILE
    q = q_ref[0].reshape(NSA_HPG * Q_TILE, HEAD_DIM)
    hl, r = _head_rows()
    slope = _slopes(g, hl)
    s = _dot_nt(q, kc_ref[0, 0])
    n = lax.broadcasted_iota(jnp.int32, (1, N_CMP_PAD), 1)
    dc = (q0 + r) - (n * CMP_STRIDE + (CMP_LEN - 1))
    mask = (dc >= 0) & (n < N_CMP_PAD - 1)
    l = jnp.where(mask, s - slope * dc.astype(F32), NEG)
    m = jnp.max(l, axis=-1, keepdims=True)
    e = jnp.where(mask, jnp.exp(l - m), 0.0)
    pc = e / jnp.maximum(jnp.sum(e, axis=-1, keepdims=True), 1e-30)
    oc = _dot(pc.astype(BF16), vc_ref[0, 0])
    g0 = _gate_rows(sm_ref[0], g, 0)
    oc_ref[0] = (oc * g0).reshape(NSA_HPG, Q_TILE, HEAD_DIM)
    ps = pc[0:Q_TILE]
    for i in range(1, NSA_HPG):
        ps = ps + pc[i * Q_TILE:(i + 1) * Q_TILE]
    ps_hi = ps.astype(BF16)
    ps_lo = (ps - ps_hi.astype(F32)).astype(BF16)
    imp = _dot(ps_hi, ov_ref[...]) + _dot(ps_lo, ov_ref[...])
    j = lax.broadcasted_iota(jnp.int32, imp.shape, 1)
    jf = j.astype(F32)
    t = q0 + lax.broadcasted_iota(jnp.int32, (Q_TILE, 1), 0)
    cur = t // SEL_LEN
    forced = (j == 0) | (j == cur) | (j == cur - 1)
    v = jnp.where(j > cur, -FORCE, jnp.where(forced, FORCE, imp))
    sel = jnp.zeros(imp.shape, jnp.bool_)
    for _ in range(N_SEL):
        mx = jnp.max(v, axis=-1, keepdims=True)
        idx = jnp.min(jnp.where(v == mx, jf, float(LANES)), axis=-1, keepdims=True)
        pick = jf == idx
        sel = sel | pick
        v = jnp.where(pick, -3e38, v)
    live = sel & (j <= cur)
    selb_ref[0, 0] = jnp.where(live, 0.0, NEG).astype(BF16)
    flag_ref[0, 0, 0] = jnp.max(live.astype(jnp.int32), axis=0, keepdims=True)


def _cmp_attention(qa, kvc, sm, ov):
    B, H, S, _ = qa.shape
    G = NSA_KV_GROUPS
    nq = S // Q_TILE
    return pl.pallas_call(
        _cmp_kernel,
        grid=(B, G, nq),
        in_specs=[
            pl.BlockSpec((1, NSA_HPG, Q_TILE, HEAD_DIM), lambda b, g, i: (b, g, i, 0)),
            pl.BlockSpec((1, 1, N_CMP_PAD, HEAD_DIM), lambda b, g, i: (b, g, 0, 0)),
            pl.BlockSpec((1, 1, N_CMP_PAD, HEAD_DIM), lambda b, g, i: (b, 2 + g, 0, 0)),
            pl.BlockSpec((1, Q_TILE, LANES), lambda b, g, i: (b, i, 0)),
            pl.BlockSpec((N_CMP_PAD, LANES), lambda b, g, i: (0, 0)),
        ],
        out_specs=[
            pl.BlockSpec((1, NSA_HPG, Q_TILE, HEAD_DIM), lambda b, g, i: (b, g, i, 0)),
            pl.BlockSpec((1, 1, Q_TILE, LANES), lambda b, g, i: (b, g, i, 0)),
            pl.BlockSpec((1, 1, 1, 1, LANES), lambda b, g, i: (b, g, i, 0, 0)),
        ],
        out_shape=[
            jax.ShapeDtypeStruct((B, H, S, HEAD_DIM), F32),
            jax.ShapeDtypeStruct((B, G, S, LANES), BF16),
            jax.ShapeDtypeStruct((B, G, nq, 1, LANES), jnp.int32),
        ],
        compiler_params=_cparams(("parallel", "parallel", "parallel")),
        name="cmp_attention",
    )(qa, kvc, kvc, sm, ov)


def _online_update(l, v, m_sc, l_sc, acc_sc):
    m_old = m_sc[...]
    m_new = jnp.maximum(m_old, jnp.max(l, axis=-1, keepdims=True))
    alpha = jnp.exp(m_old - m_new)
    p = jnp.exp(l - m_new)
    l_sc[...] = alpha * l_sc[...] + jnp.sum(p, axis=-1, keepdims=True)
    acc_sc[...] = alpha * acc_sc[...] + _dot(p.astype(BF16), v)
    m_sc[...] = m_new


def _selwin_kernel(flags_ref, q_ref, ks_ref, vs_ref, kw_ref, vw_ref, selb_ref, oc_ref, sm_ref,
                   o_ref, m_sc, l_sc, acc_sc):
    b = pl.program_id(0)
    g = pl.program_id(1)
    qb = pl.program_id(2)
    nq = pl.num_programs(2)
    q = q_ref[0].reshape(NSA_HPG * Q_TILE, HEAD_DIM)
    hl, r = _head_rows()
    slope = _slopes(g, hl)
    c = lax.broadcasted_iota(jnp.int32, (1, K_TILE), 1)
    rel = r - c
    alibi = -slope * rel.astype(F32)
    sb = selb_ref[0, 0]
    selb4 = jnp.concatenate([sb] * NSA_HPG, axis=0)
    jb = lax.broadcasted_iota(jnp.int32, (LANES, K_TILE), 0)
    half = (lax.broadcasted_iota(jnp.int32, (LANES, K_TILE), 1) >= SEL_LEN).astype(jnp.int32)

    def reset():
        m_sc[...] = jnp.full(m_sc.shape, NEG, F32)
        l_sc[...] = jnp.zeros(l_sc.shape, F32)
        acc_sc[...] = jnp.zeros(acc_sc.shape, F32)

    def result():
        return acc_sc[...] / jnp.maximum(l_sc[...], 1e-30)

    def sel_tile(kt, diag):
        k = ks_ref[0, 0, pl.ds(pl.multiple_of(kt * K_TILE, K_TILE), K_TILE), :]
        v = vs_ref[0, 0, pl.ds(pl.multiple_of(kt * K_TILE, K_TILE), K_TILE), :]
        expand = (jb == 2 * kt + half).astype(BF16)
        s = _dot_nt(q, k) + _dot(selb4, expand)
        l = s + alibi - slope * ((qb - kt) * K_TILE).astype(F32)
        if diag:
            l = jnp.where(rel >= 0, l, NEG)
        _online_update(l, v, m_sc, l_sc, acc_sc)

    reset()
    base = ((b * NSA_KV_GROUPS + g) * nq + qb) * 2

    def body(kt, carry):
        word = flags_ref[base + kt // 32]
        bit = (word >> (kt % 32)) & 1

        @pl.when(bit == 1)
        def _():
            sel_tile(kt, False)
        return carry

    lax.fori_loop(0, qb, body, 0)
    sel_tile(qb, True)
    o_sel = result()

    reset()
    n_win = WINDOW // K_TILE
    for d in range(n_win, -1, -1):
        def win_tile(d=d):
            kt = qb - d
            k = kw_ref[0, 0, pl.ds(pl.multiple_of(kt * K_TILE, K_TILE), K_TILE), :]
            v = vw_ref[0, 0, pl.ds(pl.multiple_of(kt * K_TILE, K_TILE), K_TILE), :]
            l = _dot_nt(q, k) + alibi - slope * float(d * K_TILE)
            if d == n_win:
                l = jnp.where(rel < 0, l, NEG)
            if d == 0:
                l = jnp.where(rel >= 0, l, NEG)
            _online_update(l, v, m_sc, l_sc, acc_sc)
        if d == 0:
            win_tile()
        else:
            pl.when(qb >= d)(win_tile)
    o_win = result()

    sm = sm_ref[0]
    y = oc_ref[0].reshape(NSA_HPG * Q_TILE, HEAD_DIM) + _gate_rows(sm, g, 1) * o_sel + _gate_rows(sm, g, 2) * o_win
    o_ref[0] = jnp.concatenate([y[i * Q_TILE:(i + 1) * Q_TILE] for i in range(NSA_HPG)], axis=1).astype(BF16)


def _selwin_attention(flag_words, qa, kva, selb, ocg, sm):
    B, H, S, _ = qa.shape
    G = NSA_KV_GROUPS
    nq = S // Q_TILE
    rows = NSA_HPG * Q_TILE
    kv_spec = lambda piece: pl.BlockSpec((1, 1, S, HEAD_DIM), lambda b, g, i, f: (b, piece + g, 0, 0))
    grid_spec = pltpu.PrefetchScalarGridSpec(
        num_scalar_prefetch=1,
        grid=(B, G, nq),
        in_specs=[
            pl.BlockSpec((1, NSA_HPG, Q_TILE, HEAD_DIM), lambda b, g, i, f: (b, g, i, 0)),
            kv_spec(4), kv_spec(6), kv_spec(8), kv_spec(10),
            pl.BlockSpec((1, 1, Q_TILE, LANES), lambda b, g, i, f: (b, g, i, 0)),
            pl.BlockSpec((1, NSA_HPG, Q_TILE, HEAD_DIM), lambda b, g, i, f: (b, g, i, 0)),
            pl.BlockSpec((1, Q_TILE, LANES), lambda b, g, i, f: (b, i, 0)),
        ],
        out_specs=pl.BlockSpec((1, Q_TILE, NSA_HPG * HEAD_DIM), lambda b, g, i, f: (b, i, g)),
        scratch_shapes=[
            pltpu.VMEM((rows, 1), F32),
            pltpu.VMEM((rows, 1), F32),
            pltpu.VMEM((rows, HEAD_DIM), F32),
        ],
    )
    return pl.pallas_call(
        _selwin_kernel,
        grid_spec=grid_spec,
        out_shape=jax.ShapeDtypeStruct((B, S, NSA_W), BF16),
        compiler_params=_cparams(("parallel", "parallel", "arbitrary")),
        name="selwin_attention",
    )(flag_words, qa, kva, kva, kva, kva, selb, ocg, sm)


def _fox_kernel(q_ref, k_ref, v_ref, fc_ref, fr_ref, o_ref, m_sc, l_sc, acc_sc, *, tq):
    qi = pl.program_id(2)
    m_sc[...] = jnp.full(m_sc.shape, NEG, F32)
    l_sc[...] = jnp.zeros(l_sc.shape, F32)
    acc_sc[...] = jnp.zeros(acc_sc.shape, F32)

    def tile(kt, diag):
        start = pl.multiple_of(kt * tq, tq)
        for hh in range(2):
            k = k_ref[0, hh, pl.ds(start, tq), :]
            v = v_ref[0, hh, pl.ds(start, tq), :]
            fk = fr_ref[0, hh, :, pl.ds(start, tq)]
            l = _dot_nt(q_ref[0, hh], k) + fc_ref[0, hh] - fk
            if diag:
                r = lax.broadcasted_iota(jnp.int32, l.shape, 0)
                c = lax.broadcasted_iota(jnp.int32, l.shape, 1)
                l = jnp.where(r >= c, l, NEG)
            _online_update(l, v, m_sc.at[hh], l_sc.at[hh], acc_sc.at[hh])

    def body(kt, carry):
        tile(kt, False)
        return carry

    lax.fori_loop(0, qi, body, 0)
    tile(qi, True)
    o = [acc_sc[hh] / jnp.maximum(l_sc[hh], 1e-30) for hh in range(2)]
    o_ref[0] = jnp.concatenate(o, axis=1).astype(BF16)


def _fox_attention(fox, f_col, f_row, tq=512):
    B, _, S, _ = fox.shape
    HP = FOX_HEADS // 2
    return pl.pallas_call(
        functools.partial(_fox_kernel, tq=tq),
        grid=(B, HP, S // tq),
        in_specs=[
            pl.BlockSpec((1, 2, tq, HEAD_DIM), lambda b, h, i: (b, h, i, 0)),
            pl.BlockSpec((1, 2, S, HEAD_DIM), lambda b, h, i: (b, HP + h, 0, 0)),
            pl.BlockSpec((1, 2, S, HEAD_DIM), lambda b, h, i: (b, 2 * HP + h, 0, 0)),
            pl.BlockSpec((1, 2, tq, 1), lambda b, h, i: (b, h, i, 0)),
            pl.BlockSpec((1, 2, 1, S), lambda b, h, i: (b, h, 0, 0)),
        ],
        out_specs=pl.BlockSpec((1, tq, 2 * HEAD_DIM), lambda b, h, i: (b, i, h)),
        out_shape=jax.ShapeDtypeStruct((B, S, FOX_W), BF16),
        scratch_shapes=[
            pltpu.VMEM((2, tq, 1), F32),
            pltpu.VMEM((2, tq, 1), F32),
            pltpu.VMEM((2, tq, HEAD_DIM), F32),
        ],
        compiler_params=_cparams(("parallel", "parallel", "arbitrary")),
        name="fox_attention",
    )(fox, fox, fox, f_col, f_row)


def _merge_kernel(ya_ref, yb_ref, mg_ref, x_ref, mod_ref, gpost_ref, gpre_ref,
                  wa_ref, wb_ref, wo_ref, wrh_ref, wrl_ref, br_ref,
                  x1_ref, h2_ref, lg_ref):
    D = D_MODEL
    a = _dot(ya_ref[0], wa_ref[...])
    bq = _dot(yb_ref[0], wb_ref[...])
    mg = mg_ref[0]
    u = mg[:, :D].astype(F32) * a + mg[:, D:].astype(F32) * bq
    mixed = _dot(u.astype(BF16), wo_ref[...])
    x1 = x_ref[0] + mod_ref[0, 2:3, :] * _rms(mixed, gpost_ref[...])
    x1_ref[0] = x1
    h2 = _rms(x1, gpre_ref[...]) * (1.0 + mod_ref[0, 4:5, :]) + mod_ref[0, 3:4, :]
    hi = h2.astype(BF16)
    lo = (h2 - hi.astype(F32)).astype(BF16)
    h2_ref[0] = hi
    lg_ref[0] = _dot(hi, wrh_ref[...]) + _dot(lo, wrh_ref[...]) + _dot(hi, wrl_ref[...]) + br_ref[...]


def _merge(ya, yb, mg, x, mod, gpost, gpre, wa, wb, wo, wrh, wrl, br, tm=256):
    B, S, D = x.shape
    c2 = lambda b, i: (0, 0)
    row = lambda w: pl.BlockSpec((1, tm, w), lambda b, i: (b, i, 0))
    return pl.pallas_call(
        _merge_kernel,
        grid=(B, S // tm),
        in_specs=[
            row(NSA_W), row(FOX_W), row(2 * D), row(D),
            pl.BlockSpec((1, 6, D), lambda b, i: (b, 0, 0)),
            pl.BlockSpec((1, D), c2), pl.BlockSpec((1, D), c2),
            pl.BlockSpec((NSA_W, D), c2), pl.BlockSpec((FOX_W, D), c2), pl.BlockSpec((D, D), c2),
            pl.BlockSpec((D, LANES), c2), pl.BlockSpec((D, LANES), c2), pl.BlockSpec((1, LANES), c2),
        ],
        out_specs=[row(D), row(D), row(LANES)],
        out_shape=[
            jax.ShapeDtypeStruct((B, S, D), F32),
            jax.ShapeDtypeStruct((B, S, D), BF16),
            jax.ShapeDtypeStruct((B, S, LANES), F32),
        ],
        compiler_params=_cparams(("parallel", "parallel")),
        name="merge",
    )(ya, yb, mg, x, mod, gpost, gpre, wa, wb, wo, wrh, wrl, br)


def _expert_kernel(be_ref, na_ref, x_ref, wg_ref, wu_ref, wd_ref, o_ref):
    i = pl.program_id(0)

    @pl.when(i < na_ref[0])
    def _():
        x = x_ref[...]
        gate = _dot(x, wg_ref[0])
        up = _dot(x, wu_ref[0])
        mid = (gate * jax.nn.sigmoid(gate) * up).astype(BF16)
        o_ref[...] = _dot(mid, wd_ref[0])

    @pl.when(i >= na_ref[0])
    def _():
        o_ref[...] = jnp.zeros(o_ref.shape, o_ref.dtype)


def _experts(block_expert, n_active, xb, wg, wu, wd):
    cap, D = xb.shape
    nblk = cap // MOE_TILE
    grid_spec = pltpu.PrefetchScalarGridSpec(
        num_scalar_prefetch=2,
        grid=(nblk,),
        in_specs=[
            pl.BlockSpec((MOE_TILE, D), lambda i, be, na: (i, 0)),
            pl.BlockSpec((1, D, D_EXPERT), lambda i, be, na: (be[i], 0, 0)),
            pl.BlockSpec((1, D, D_EXPERT), lambda i, be, na: (be[i], 0, 0)),
            pl.BlockSpec((1, D_EXPERT, D), lambda i, be, na: (be[i], 0, 0)),
        ],
        out_specs=pl.BlockSpec((MOE_TILE, D), lambda i, be, na: (i, 0)),
    )
    return pl.pallas_call(
        _expert_kernel,
        grid_spec=grid_spec,
        out_shape=jax.ShapeDtypeStruct((cap, D), F32),
        compiler_params=_cparams(("arbitrary",)),
        name="experts",
    )(block_expert, n_active, xb, wg, wu, wd)


def _final_kernel(x1_ref, y_ref, mod_ref, g_ref, o_ref):
    o_ref[0] = x1_ref[0] + mod_ref[0, 5:6, :] * _rms(y_ref[0], g_ref[...])


def _final(x1, y, mod, g, tm=512):
    B, S, D = x1.shape
    row = pl.BlockSpec((1, tm, D), lambda b, i: (b, i, 0))
    return pl.pallas_call(
        _final_kernel,
        grid=(B, S // tm),
        in_specs=[row, row, pl.BlockSpec((1, 6, D), lambda b, i: (b, 0, 0)),
                  pl.BlockSpec((1, D), lambda b, i: (0, 0))],
        out_specs=row,
        out_shape=jax.ShapeDtypeStruct((B, S, D), F32),
        compiler_params=_cparams(("parallel", "parallel")),
        name="final",
    )(x1, y, mod, g)


def _overlap_matrix():
    n = np.arange(N_CMP_PAD)[:, None]
    j = np.arange(LANES)[None, :]
    start = n * CMP_STRIDE
    ov = (start < j * SEL_LEN + SEL_LEN) & (start + CMP_LEN - 1 >= j * SEL_LEN) & (n < N_CMP_PAD - 1)
    return jnp.asarray(ov.astype(np.float32), dtype=BF16)


def _pad_cols(w, width=LANES):
    return jnp.pad(w, ((0, 0), (0, width - w.shape[1])))


def _route(logits, T):
    pg = jax.nn.softmax(logits[:, :N_EXPERT_GROUPS], axis=-1)
    pg_top, g_idx = lax.top_k(pg, 1)
    le = logits[:, N_EXPERT_GROUPS:N_EXPERT_GROUPS + N_EXPERTS].reshape(T, N_EXPERT_GROUPS, EXPERTS_PER_GROUP)
    le = jnp.take_along_axis(le, g_idx[:, :, None], axis=1)[:, 0]
    top_l, e_local = lax.top_k(le, EXPERT_TOP_K)
    weight = pg_top * jax.nn.softmax(top_l, axis=-1)
    expert = g_idx * EXPERTS_PER_GROUP + e_local
    A = T * EXPERT_TOP_K
    e_flat = expert.reshape(A)
    onehot = (e_flat[:, None] == jnp.arange(N_EXPERTS)[None, :]).astype(jnp.int32)
    rank = jnp.take_along_axis(jnp.cumsum(onehot, axis=0) - onehot, e_flat[:, None], axis=1)[:, 0]
    counts = jnp.sum(onehot, axis=0)
    padded = (counts + MOE_TILE - 1) // MOE_TILE * MOE_TILE
    pad_end = jnp.cumsum(padded)
    pad_start = pad_end - padded
    dest = pad_start[e_flat] + rank
    cap = -(-(A + N_EXPERTS * (MOE_TILE - 1)) // MOE_TILE) * MOE_TILE
    nblk = cap // MOE_TILE
    n_active = (pad_end[-1] // MOE_TILE).astype(jnp.int32)
    blk = jnp.arange(nblk) * MOE_TILE
    block_expert = jnp.minimum(jnp.searchsorted(pad_end, blk, side='right'), N_EXPERTS - 1)
    last = block_expert[jnp.maximum(n_active - 1, 0)]
    block_expert = jnp.where(jnp.arange(nblk) < n_active, block_expert, last).astype(jnp.int32)
    tok = jnp.arange(A, dtype=jnp.int32) // EXPERT_TOP_K
    buf_tok = jnp.zeros((cap,), jnp.int32).at[dest].set(tok)
    return weight, dest.reshape(T, EXPERT_TOP_K), buf_tok, block_expert, n_active.reshape(1)


def kernel(x, c, w_ada, b_ada, g_pre_mix, g_post_mix, g_pre_ffn, g_post_ffn, w_in, b_forget,
           cmp_pe_k, cmp_w1_k, cmp_w2_k, cmp_pe_v, cmp_w1_v, cmp_w2_v,
           w_o_nsa, w_o_fox, w_out, w_router_group, b_router_group, w_router_expert, b_router_expert,
           w_exp_gate, w_exp_up, w_exp_down):
    B, S, D = x.shape
    T = B * S
    depth = w_ada.shape[0]
    ov = _overlap_matrix()
    for l in range(depth):
        mod = (jax.nn.silu(c) @ w_ada[l] + b_ada[l]).reshape(B, 6, D)
        w_qa, w_kva, w_gl, w_fox, w_f, w_mg = jnp.split(w_in[l], IN_SPLITS, axis=-1)
        w_big = jnp.concatenate([w_qa, w_kva, w_fox, w_mg], axis=1).astype(BF16)
        w_small = _pad_cols(jnp.concatenate([w_gl, w_f], axis=1)).astype(BF16)
        bf_pad = jnp.pad(b_forget[l], (3 * NSA_HEADS, LANES - 3 * NSA_HEADS - FOX_HEADS)).reshape(1, LANES)
        qa, kva, fox, mg, sm = _inproj(x, mod, g_pre_mix[l].reshape(1, D), w_big, w_small, bf_pad)

        half = CMP_LEN // 2
        pe = jnp.stack([cmp_pe_k[l], cmp_pe_v[l]]).reshape(2, 2, 1, half * HEAD_DIM)
        w1 = jnp.stack([cmp_w1_k[l], cmp_w1_v[l]]).reshape(2, 2, half * HEAD_DIM, HEAD_DIM).astype(BF16)
        w2 = jnp.stack([cmp_w2_k[l], cmp_w2_v[l]]).astype(BF16)
        kvc = _compress(kva.reshape(B, 12, S // CMP_STRIDE, CMP_STRIDE * HEAD_DIM), pe, w1, w2)
        ocg, selb, flags = _cmp_attention(qa, kvc, sm, ov)
        nq = S // Q_TILE
        tile_any = jnp.max(flags.reshape(B, NSA_KV_GROUPS, nq, LANES // 2, 2), axis=-1)
        bits = tile_any.reshape(B, NSA_KV_GROUPS, nq, 2, 32).astype(jnp.uint32) << jnp.arange(32, dtype=jnp.uint32)
        flag_words = lax.bitcast_convert_type(jnp.sum(bits, axis=-1, dtype=jnp.uint32), jnp.int32).reshape(-1)
        y_a = _selwin_attention(flag_words, qa, kva, selb, ocg, sm)

        f_cum = jnp.cumsum(sm[:, :, 3 * NSA_HEADS:3 * NSA_HEADS + FOX_HEADS], axis=1).transpose(0, 2, 1)
        y_b = _fox_attention(fox, f_cum[..., None], f_cum[:, :, None, :])

        w_r = _pad_cols(jnp.concatenate([w_router_group[l], w_router_expert[l]], axis=1))
        w_rh = w_r.astype(BF16)
        w_rl = (w_r - w_rh.astype(F32)).astype(BF16)
        b_r = _pad_cols(jnp.concatenate([b_router_group[l], b_router_expert[l]]).reshape(1, -1))
        x1, h2, logits = _merge(y_a, y_b, mg, x, mod, g_post_mix[l].reshape(1, D), g_pre_ffn[l].reshape(1, D),
                                w_o_nsa[l].astype(BF16), w_o_fox[l].astype(BF16), w_out[l].astype(BF16),
                                w_rh, w_rl, b_r)

        weight, dest, buf_tok, block_expert, n_active = _route(logits.reshape(T, LANES), T)
        xb = h2.reshape(T, D)[buf_tok]
        yb = _experts(block_expert, n_active, xb, w_exp_gate[l].astype(BF16), w_exp_up[l].astype(BF16),
                      w_exp_down[l].astype(BF16))
        y = weight[:, 0:1] * yb[dest[:, 0]] + weight[:, 1:2] * yb[dest[:, 1]]
        x = _final(x1, y.reshape(B, S, D), mod, g_post_ffn[l].reshape(1, D))
    return x
```

```python
import functools

import ml_dtypes
import numpy as np
import jax
import jax.numpy as jnp
from jax import lax
from jax.experimental import pallas as pl
from jax.experimental.pallas import tpu as pltpu

D_MODEL = 1024
HEAD_DIM = 64
NSA_HEADS = 8
NSA_KV_GROUPS = 2
NSA_HPG = NSA_HEADS // NSA_KV_GROUPS
FOX_HEADS = 8
CMP_LEN = 32
CMP_STRIDE = 16
SEL_LEN = 64
N_SEL = 16
WINDOW = 512
N_EXPERT_GROUPS = 4
EXPERTS_PER_GROUP = 8
N_EXPERTS = N_EXPERT_GROUPS * EXPERTS_PER_GROUP
EXPERT_TOP_K = 2
D_EXPERT = D_MODEL // 2
NORM_EPS = 1e-6
NEG = -1e30
FORCE = 1e9
LOG2E = 1.4426950408889634

NSA_W = NSA_HEADS * HEAD_DIM
NSA_KV_W = NSA_KV_GROUPS * HEAD_DIM
FOX_W = FOX_HEADS * HEAD_DIM
IN_SIZES = (NSA_W, 6 * NSA_KV_W, 3 * NSA_HEADS, 3 * FOX_W, FOX_HEADS, 2 * D_MODEL)
IN_SPLITS = tuple(int(v) for v in np.cumsum(IN_SIZES)[:-1])

LANES = 128
Q_TILE = 128
K_TILE = 256
N_CMP_PAD = 512
MOE_TILE = 256
IN_TILE = 512
VMEM_LIMIT = 56 * 1024 * 1024

F_LANE = 3 * NSA_HEADS
U_LANE = 64
ONE_LANE = 88
A_LANE = 89
B_LANE = 90
EXT = HEAD_DIM
G_FQ, G_FK, G_NQ, G_NK, N_GROUPS = 0, 8, 16, 24, 25

F32 = jnp.float32
BF16 = jnp.bfloat16


def _dot(a, b):
    return jnp.dot(a, b, preferred_element_type=F32)


def _dot_nt(a, b):
    return lax.dot_general(a, b, (((1,), (1,)), ((), ())), preferred_element_type=F32)


def _rms(x, g):
    return x * lax.rsqrt(jnp.mean(x * x, axis=-1, keepdims=True) + NORM_EPS) * g


def _cparams(sem):
    return pltpu.CompilerParams(dimension_semantics=sem, vmem_limit_bytes=VMEM_LIMIT)


def _split3(x):
    hi = x.astype(BF16).astype(F32)
    r = x - hi
    mid = r.astype(BF16).astype(F32)
    lo = (r - mid).astype(BF16).astype(F32)
    return hi, mid, lo


def _np_split3(x):
    x = np.asarray(x, np.float32)
    hi = x.astype(ml_dtypes.bfloat16).astype(np.float32)
    r = x - hi
    mid = r.astype(ml_dtypes.bfloat16).astype(np.float32)
    lo = (r - mid).astype(ml_dtypes.bfloat16).astype(np.float32)
    return hi, mid, lo


def _alibi_c():
    slopes = np.exp2(-8.0 * np.arange(1, NSA_HEADS + 1, dtype=np.float32) / NSA_HEADS).astype(np.float32)
    return slopes * np.float32(LOG2E)


def _row_features(S):
    t = np.arange(S, dtype=np.float32)
    c = _alibi_c()
    rs = np.zeros((S, LANES), np.float32)
    for h in range(NSA_HEADS):
        for j, term in enumerate(_np_split3(c[h] * t)):
            rs[:, U_LANE + 8 * j + h] = -term
    rs[:, ONE_LANE] = 1.0
    rs[:, A_LANE] = np.floor(t / LANES)
    rs[:, B_LANE] = t % LANES
    return jnp.asarray(rs, dtype=BF16)


def _placement():
    c = _alibi_c()
    p = np.zeros((LANES, N_GROUPS * LANES), np.float32)
    for h in range(FOX_HEADS):
        q0 = (G_FQ + h) * LANES + EXT
        k0 = (G_FK + h) * LANES + EXT
        for j in range(3):
            p[ONE_LANE, q0 + j] = -1.0
            p[F_LANE + 8 * j + h, q0 + 3 + j] = 1.0
            p[F_LANE + 8 * j + h, k0 + j] = 1.0
            p[ONE_LANE, k0 + 3 + j] = 1.0
    for h in range(NSA_HEADS):
        q0 = (G_NQ + h) * LANES + EXT
        c128 = _np_split3(c[h] * np.float32(LANES))
        c1 = _np_split3(c[h])
        for j in range(3):
            p[U_LANE + 8 * j + h, q0 + j] = 1.0
            p[ONE_LANE, q0 + 3 + j] = c128[j]
            p[ONE_LANE, q0 + 6 + j] = c1[j]
    k0 = G_NK * LANES + EXT
    for j in range(3):
        p[ONE_LANE, k0 + j] = 1.0
        p[A_LANE, k0 + 3 + j] = 1.0
        p[B_LANE, k0 + 6 + j] = 1.0
    return jnp.asarray(p, dtype=BF16)


def _cmp_key_ext():
    pos = np.arange(N_CMP_PAD, dtype=np.float32) * CMP_STRIDE + (CMP_LEN - 1)
    e = np.zeros((2, N_CMP_PAD, LANES), np.float32)
    for j in range(3):
        e[0, :, EXT + j] = 1.0
        e[0, :, EXT + 3 + j] = np.floor(pos / LANES)
        e[0, :, EXT + 6 + j] = pos % LANES
    return jnp.asarray(e, dtype=BF16)


def _inproj_kernel(x_ref, mod_ref, g_ref, wb_ref, ws_ref, bf_ref, tri_ref, rs_ref, p_ref,
                   qa_ref, ckv_ref, ksl_ref, nkv_ref, fq_ref, fk_ref, fv_ref, mg_ref, sm_ref, carry_sc):
    i = pl.program_id(1)
    tm = x_ref.shape[1]
    x = x_ref[0]
    h = _rms(x, g_ref[...]) * (1.0 + mod_ref[0, 1:2, :]) + mod_ref[0, 0:1, :]
    hb = h.astype(BF16)
    lane = lax.broadcasted_iota(jnp.int32, (tm, LANES), 1)
    lower = lane < HEAD_DIM
    ones_col = (lane == EXT).astype(F32)

    z = _dot(hb, ws_ref[...]) + bf_ref[...]
    logsig = jnp.minimum(z, 0.0) - jnp.log1p(jnp.exp(-jnp.abs(z)))
    sm_ref[0] = jnp.where(lane < F_LANE, jax.nn.sigmoid(z), logsig)

    @pl.when(i == 0)
    def _():
        carry_sc[...] = jnp.zeros(carry_sc.shape, F32)

    is_f = (lane >= F_LANE) & (lane < F_LANE + FOX_HEADS)
    l_hi, l_mid, l_lo = _split3(jnp.where(is_f, logsig, 0.0))
    tri = tri_ref[...]
    cum = carry_sc[...] + _dot(tri, l_hi.astype(BF16)) + _dot(tri, l_mid.astype(BF16)) + _dot(tri, l_lo.astype(BF16))
    carry_sc[...] = cum[tm - 1:tm, :]
    f_hi, f_mid, f_lo = _split3(cum * LOG2E)
    feat = (f_hi + pltpu.roll(f_mid, 8, 1) + pltpu.roll(f_lo, 16, 1) + rs_ref[...].astype(F32)).astype(BF16)

    def ext(group):
        return _dot(feat, p_ref[:, group * LANES:(group + 1) * LANES])

    def piece(acc, idx, extra):
        pair = acc[:, (idx // 2) * LANES:(idx // 2 + 1) * LANES]
        if idx % 2:
            pair = pltpu.roll(pair, HEAD_DIM, 1)
        return jnp.where(lower, pair, extra).astype(BF16)

    qscale = (HEAD_DIM ** -0.5) * LOG2E
    acc = _dot(hb, wb_ref[:, 0:NSA_W]) * qscale
    for hd in range(NSA_HEADS):
        qa_ref[0, hd] = piece(acc, hd, ext(G_NQ + hd))
    off = NSA_W
    acc = _dot(hb, wb_ref[:, off:off + 6 * NSA_KV_W])
    for pc in range(4):
        ckv_ref[0, pc] = acc[:, pc * HEAD_DIM:(pc + 1) * HEAD_DIM].astype(BF16)
    ext_k = ext(G_NK)
    t = i * tm + lax.broadcasted_iota(jnp.int32, (tm, LANES), 0)
    block_onehot = (lane == t // SEL_LEN).astype(BF16)
    for g in range(NSA_KV_GROUPS):
        ksl_ref[0, g, :, 0:LANES] = piece(acc, 4 + g, ext_k)
        ksl_ref[0, g, :, LANES:2 * LANES] = block_onehot
        nkv_ref[0, g] = piece(acc, 6 + g, ones_col)
        nkv_ref[0, 2 + g] = piece(acc, 8 + g, ext_k)
        nkv_ref[0, 4 + g] = piece(acc, 10 + g, ones_col)
    off += 6 * NSA_KV_W
    acc = _dot(hb, wb_ref[:, off:off + FOX_W]) * qscale
    for hd in range(FOX_HEADS):
        fq_ref[0, hd] = piece(acc, hd, ext(G_FQ + hd))
    off += FOX_W
    acc = _dot(hb, wb_ref[:, off:off + FOX_W])
    for hd in range(FOX_HEADS):
        fk_ref[0, hd] = piece(acc, hd, ext(G_FK + hd))
    off += FOX_W
    acc = _dot(hb, wb_ref[:, off:off + FOX_W])
    for hd in range(FOX_HEADS):
        fv_ref[0, hd] = piece(acc, hd, ones_col)
    off += FOX_W
    for c in range(4):
        acc = _dot(hb, wb_ref[:, off + c * 512: off + (c + 1) * 512])
        mg_ref[0, :, c * 512:(c + 1) * 512] = jax.nn.sigmoid(acc).astype(BF16)


def _inproj(x, mod, g, wb, ws, bfp, tri, rs, pm):
    B, S, D = x.shape
    tm = IN_TILE
    nb = wb.shape[1]
    const2 = lambda b, i: (0, 0)
    heads = lambda n: pl.BlockSpec((1, n, tm, LANES), lambda b, i: (b, 0, i, 0))
    hshape = lambda n: jax.ShapeDtypeStruct((B, n, S, LANES), BF16)
    return pl.pallas_call(
        _inproj_kernel,
        grid=(B, S // tm),
        in_specs=[
            pl.BlockSpec((1, tm, D), lambda b, i: (b, i, 0)),
            pl.BlockSpec((1, 6, D), lambda b, i: (b, 0, 0)),
            pl.BlockSpec((1, D), const2),
            pl.BlockSpec((D, nb), const2),
            pl.BlockSpec((D, LANES), const2),
            pl.BlockSpec((1, LANES), const2),
            pl.BlockSpec((tm, tm), const2),
            pl.BlockSpec((tm, LANES), lambda b, i: (i, 0)),
            pl.BlockSpec((LANES, N_GROUPS * LANES), const2),
        ],
        out_specs=[
            heads(NSA_HEADS),
            pl.BlockSpec((1, 4, tm, HEAD_DIM), lambda b, i: (b, 0, i, 0)),
            pl.BlockSpec((1, NSA_KV_GROUPS, tm, 2 * LANES), lambda b, i: (b, 0, i, 0)),
            heads(6), heads(FOX_HEADS), heads(FOX_HEADS), heads(FOX_HEADS),
            pl.BlockSpec((1, tm, 2 * D), lambda b, i: (b, i, 0)),
            pl.BlockSpec((1, tm, LANES), lambda b, i: (b, i, 0)),
        ],
        out_shape=[
            hshape(NSA_HEADS),
            jax.ShapeDtypeStruct((B, 4, S, HEAD_DIM), BF16),
            jax.ShapeDtypeStruct((B, NSA_KV_GROUPS, S, 2 * LANES), BF16),
            hshape(6), hshape(FOX_HEADS), hshape(FOX_HEADS), hshape(FOX_HEADS),
            jax.ShapeDtypeStruct((B, S, 2 * D), BF16),
            jax.ShapeDtypeStruct((B, S, LANES), F32),
        ],
        scratch_shapes=[pltpu.VMEM((1, LANES), F32)],
        compiler_params=_cparams(("parallel", "arbitrary")),
        name="inproj",
    )(x, mod, g, wb, ws, bfp, tri, rs, pm)


def _compress_kernel(x_ref, pe_ref, w1_ref, w2_ref, e_ref, o_ref):
    x = x_ref[0, 0].astype(F32)
    x_lo = (x + pe_ref[0, 0]).astype(BF16)
    x_hi = (x + pe_ref[0, 1]).astype(BF16)
    y_lo = _dot(x_lo, w1_ref[0, 0])
    y_hi = _dot(x_hi, w1_ref[0, 1])
    n = y_hi.shape[0]
    hid = y_lo + pltpu.roll(y_hi, n - 1, 0)
    hid = jax.nn.gelu(hid)
    o_ref[0, 0] = (_dot(hid.astype(BF16), w2_ref[0]) + e_ref[0].astype(F32)).astype(BF16)


def _compress(kv_rows, pe, w1, w2, e):
    B = kv_rows.shape[0]
    R, C = kv_rows.shape[2], kv_rows.shape[3]
    return pl.pallas_call(
        _compress_kernel,
        grid=(B, 4),
        in_specs=[
            pl.BlockSpec((1, 1, R, C), lambda b, p: (b, p, 0, 0)),
            pl.BlockSpec((1, 2, 1, C), lambda b, p: (p // 2, 0, 0, 0)),
            pl.BlockSpec((1, 2, C, HEAD_DIM), lambda b, p: (p // 2, 0, 0, 0)),
            pl.BlockSpec((1, HEAD_DIM, LANES), lambda b, p: (p // 2, 0, 0)),
            pl.BlockSpec((1, R, LANES), lambda b, p: (p // 2, 0, 0)),
        ],
        out_specs=pl.BlockSpec((1, 1, R, LANES), lambda b, p: (b, p, 0, 0)),
        out_shape=jax.ShapeDtypeStruct((B, 4, R, LANES), BF16),
        compiler_params=_cparams(("parallel", "parallel")),
        name="compress",
    )(kv_rows, pe, w1, w2, e)


def _gate_rows(sm, g, branch):
    col = lax.broadcasted_iota(jnp.int32, sm.shape, 1)
    parts = []
    for hl in range(NSA_HPG):
        want = 3 * (NSA_HPG * g + hl) + branch
        parts.append(jnp.sum(jnp.where(col == want, sm, 0.0), axis=-1, keepdims=True))
    return jnp.concatenate(parts, axis=0)


def _head_tile(y):
    lane = lax.broadcasted_iota(jnp.int32, (Q_TILE, LANES), 1)
    hs = [y[i * Q_TILE:(i + 1) * Q_TILE] for i in range(NSA_HPG)]
    pairs = [jnp.where(lane < HEAD_DIM, hs[2 * i], pltpu.roll(hs[2 * i + 1], HEAD_DIM, 1)) for i in range(2)]
    return jnp.concatenate(pairs, axis=1)


def _cmp_kernel(q_ref, kc_ref, vc_ref, sm_ref, ov_ref, oc_ref, selb_ref, flag_ref):
    g = pl.program_id(1)
    qb = pl.program_id(2)
    q0 = qb * Q_TILE
    q = q_ref[0].reshape(NSA_HPG * Q_TILE, LANES)
    s = _dot_nt(q, kc_ref[0, 0])
    r = lax.broadcasted_iota(jnp.int32, (NSA_HPG * Q_TILE, 1), 0) % Q_TILE
    n = lax.broadcasted_iota(jnp.int32, (1, N_CMP_PAD), 1)
    dc = (q0 + r) - (n * CMP_STRIDE + (CMP_LEN - 1))
    mask = (dc >= 0) & (n < N_CMP_PAD - 1)
    l = jnp.where(mask, s, NEG)
    m = jnp.max(l, axis=-1, keepdims=True)
    e = jnp.where(mask, jnp.exp2(l - m), 0.0)
    pc = e / jnp.maximum(jnp.sum(e, axis=-1, keepdims=True), 1e-30)
    oc = _dot(pc.astype(BF16), vc_ref[0, 0])
    oc_ref[0] = _head_tile(oc * _gate_rows(sm_ref[0], g, 0))
    ps = pc[0:Q_TILE]
    for i in range(1, NSA_HPG):
        ps = ps + pc[i * Q_TILE:(i + 1) * Q_TILE]
    ps_hi = ps.astype(BF16)
    ps_lo = (ps - ps_hi.astype(F32)).astype(BF16)
    imp = _dot(ps_hi, ov_ref[...]) + _dot(ps_lo, ov_ref[...])
    j = lax.broadcasted_iota(jnp.int32, imp.shape, 1)
    jf = j.astype(F32)
    t = q0 + lax.broadcasted_iota(jnp.int32, (Q_TILE, 1), 0)
    cur = t // SEL_LEN
    forced = (j == 0) | (j == cur) | (j == cur - 1)
    v = jnp.where(j > cur, -FORCE, jnp.where(forced, FORCE, imp))
    sel = jnp.zeros(imp.shape, jnp.bool_)
    for _ in range(N_SEL):
        mx = jnp.max(v, axis=-1, keepdims=True)
        idx = jnp.min(jnp.where(v == mx, jf, float(LANES)), axis=-1, keepdims=True)
        pick = jf == idx
        sel = sel | pick
        v = jnp.where(pick, -3e38, v)
    live = sel & (j <= cur)
    selb_ref[0, 0] = jnp.where(live, 0.0, NEG).astype(BF16)
    flag_ref[0, 0, 0] = jnp.max(live.astype(jnp.int32), axis=0, keepdims=True)


def _cmp_attention(qa, kvc, sm, ov):
    B, H, S, _ = qa.shape
    G = NSA_KV_GROUPS
    nq = S // Q_TILE
    return pl.pallas_call(
        _cmp_kernel,
        grid=(B, G, nq),
        in_specs=[
            pl.BlockSpec((1, NSA_HPG, Q_TILE, LANES), lambda b, g, i: (b, g, i, 0)),
            pl.BlockSpec((1, 1, N_CMP_PAD, LANES), lambda b, g, i: (b, g, 0, 0)),
            pl.BlockSpec((1, 1, N_CMP_PAD, LANES), lambda b, g, i: (b, 2 + g, 0, 0)),
            pl.BlockSpec((1, Q_TILE, LANES), lambda b, g, i: (b, i, 0)),
            pl.BlockSpec((N_CMP_PAD, LANES), lambda b, g, i: (0, 0)),
        ],
        out_specs=[
            pl.BlockSpec((1, Q_TILE, NSA_HPG * HEAD_DIM), lambda b, g, i: (b, i, g)),
            pl.BlockSpec((1, 1, Q_TILE, LANES), lambda b, g, i: (b, g, i, 0)),
            pl.BlockSpec((1, 1, 1, 1, LANES), lambda b, g, i: (b, g, i, 0, 0)),
        ],
        out_shape=[
            jax.ShapeDtypeStruct((B, S, NSA_W), F32),
            jax.ShapeDtypeStruct((B, G, S, LANES), BF16),
            jax.ShapeDtypeStruct((B, G, nq, 1, LANES), jnp.int32),
        ],
        compiler_params=_cparams(("parallel", "parallel", "parallel")),
        name="cmp_attention",
    )(qa, kvc, kvc, sm, ov)


def _online_update(s, v, m_ref, acc_ref):
    m_old = m_ref[...]
    m_new = jnp.maximum(m_old, jnp.max(s, axis=-1, keepdims=True))
    p = jnp.exp2(s - pltpu.repeat(m_new, s.shape[1] // LANES, 1))
    acc_ref[...] = jnp.exp2(m_old - m_new) * acc_ref[...] + _dot(p.astype(BF16), v)
    m_ref[...] = m_new


def _normalized(acc):
    return acc / jnp.maximum(acc[:, EXT:EXT + 1], 1e-30)


def _selwin_kernel(flags_ref, q_ref, ks_ref, vs_ref, kw_ref, vw_ref, selb_ref, oc_ref, sm_ref,
                   o_ref, m_sc, acc_sc):
    b = pl.program_id(0)
    g = pl.program_id(1)
    qb = pl.program_id(2)
    nq = pl.num_programs(2)
    rows = NSA_HPG * Q_TILE
    q4 = q_ref[0].reshape(rows, LANES)
    q_aug = jnp.concatenate([q4, jnp.concatenate([selb_ref[0, 0]] * NSA_HPG, axis=0)], axis=1)
    r = lax.broadcasted_iota(jnp.int32, (rows, 1), 0) % Q_TILE
    c = lax.broadcasted_iota(jnp.int32, (1, K_TILE), 1)
    rel = r - c
    diag = qb // (K_TILE // Q_TILE)

    def reset():
        m_sc[...] = jnp.full(m_sc.shape, NEG, F32)
        acc_sc[...] = jnp.zeros(acc_sc.shape, F32)

    def sel_tile(kt, causal):
        start = pl.multiple_of(kt * K_TILE, K_TILE)
        s = _dot_nt(q_aug, ks_ref[0, 0, pl.ds(start, K_TILE), :])
        if causal:
            s = jnp.where(rel + (qb * Q_TILE - kt * K_TILE) >= 0, s, NEG)
        _online_update(s, vs_ref[0, 0, pl.ds(start, K_TILE), :], m_sc, acc_sc)

    reset()
    word = flags_ref[(b * NSA_KV_GROUPS + g) * nq + qb]

    def body(kt, carry):
        @pl.when(((word >> kt) & 1) == 1)
        def _():
            sel_tile(kt, False)
        return carry

    lax.fori_loop(0, diag, body, 0)
    sel_tile(diag, True)
    o_sel = _normalized(acc_sc[...])

    reset()
    for d in range(WINDOW // K_TILE, -1, -1):
        def win_tile(d=d):
            kt = diag - d
            start = pl.multiple_of(kt * K_TILE, K_TILE)
            dist = rel + (qb * Q_TILE - kt * K_TILE)
            s = _dot_nt(q4, kw_ref[0, 0, pl.ds(start, K_TILE), :])
            s = jnp.where((dist >= 0) & (dist < WINDOW), s, NEG)
            _online_update(s, vw_ref[0, 0, pl.ds(start, K_TILE), :], m_sc, acc_sc)
        if d == 0:
            win_tile()
        else:
            pl.when(diag >= d)(win_tile)
    o_win = _normalized(acc_sc[...])

    sm = sm_ref[0]
    y = _gate_rows(sm, g, 1) * o_sel + _gate_rows(sm, g, 2) * o_win
    o_ref[0] = (oc_ref[0] + _head_tile(y)).astype(BF16)


def _selwin_attention(flag_words, qa, ksl, nkv, selb, ocg, sm):
    B, H, S, _ = qa.shape
    G = NSA_KV_GROUPS
    nq = S // Q_TILE
    rows = NSA_HPG * Q_TILE
    kv_spec = lambda piece: pl.BlockSpec((1, 1, S, LANES), lambda b, g, i, f: (b, piece + g, 0, 0))
    out_tile = pl.BlockSpec((1, Q_TILE, NSA_HPG * HEAD_DIM), lambda b, g, i, f: (b, i, g))
    grid_spec = pltpu.PrefetchScalarGridSpec(
        num_scalar_prefetch=1,
        grid=(B, G, nq),
        in_specs=[
            pl.BlockSpec((1, NSA_HPG, Q_TILE, LANES), lambda b, g, i, f: (b, g, i, 0)),
            pl.BlockSpec((1, 1, S, 2 * LANES), lambda b, g, i, f: (b, g, 0, 0)),
            kv_spec(0), kv_spec(2), kv_spec(4),
            pl.BlockSpec((1, 1, Q_TILE, LANES), lambda b, g, i, f: (b, g, i, 0)),
            out_tile,
            pl.BlockSpec((1, Q_TILE, LANES), lambda b, g, i, f: (b, i, 0)),
        ],
        out_specs=out_tile,
        scratch_shapes=[
            pltpu.VMEM((rows, LANES), F32),
            pltpu.VMEM((rows, LANES), F32),
        ],
    )
    return pl.pallas_call(
        _selwin_kernel,
        grid_spec=grid_spec,
        out_shape=jax.ShapeDtypeStruct((B, S, NSA_W), BF16),
        compiler_params=_cparams(("parallel", "parallel", "arbitrary")),
        name="selwin_attention",
    )(flag_words, qa, ksl, nkv, nkv, nkv, selb, ocg, sm)


def _fox_kernel(q_ref, k_ref, v_ref, o_ref, m_sc, acc_sc, *, tq):
    qi = pl.program_id(2)
    m_sc[...] = jnp.full(m_sc.shape, NEG, F32)
    acc_sc[...] = jnp.zeros(acc_sc.shape, F32)

    def tile(kt, causal):
        start = pl.multiple_of(kt * tq, tq)
        for hh in range(2):
            s = _dot_nt(q_ref[0, hh], k_ref[0, hh, pl.ds(start, tq), :])
            if causal:
                r = lax.broadcasted_iota(jnp.int32, s.shape, 0)
                c = lax.broadcasted_iota(jnp.int32, s.shape, 1)
                s = jnp.where(r >= c, s, NEG)
            _online_update(s, v_ref[0, hh, pl.ds(start, tq), :], m_sc.at[hh], acc_sc.at[hh])

    def body(kt, carry):
        tile(kt, False)
        return carry

    lax.fori_loop(0, qi, body, 0)
    tile(qi, True)
    lane = lax.broadcasted_iota(jnp.int32, (tq, LANES), 1)
    o = [_normalized(acc_sc[hh]) for hh in range(2)]
    o_ref[0] = jnp.where(lane < HEAD_DIM, o[0], pltpu.roll(o[1], HEAD_DIM, 1)).astype(BF16)


def _fox_attention(fq, fk, fv, tq=512):
    B, H, S, _ = fq.shape
    return pl.pallas_call(
        functools.partial(_fox_kernel, tq=tq),
        grid=(B, H // 2, S // tq),
        in_specs=[
            pl.BlockSpec((1, 2, tq, LANES), lambda b, h, i: (b, h, i, 0)),
            pl.BlockSpec((1, 2, S, LANES), lambda b, h, i: (b, h, 0, 0)),
            pl.BlockSpec((1, 2, S, LANES), lambda b, h, i: (b, h, 0, 0)),
        ],
        out_specs=pl.BlockSpec((1, tq, LANES), lambda b, h, i: (b, i, h)),
        out_shape=jax.ShapeDtypeStruct((B, S, FOX_W), BF16),
        scratch_shapes=[
            pltpu.VMEM((2, tq, LANES), F32),
            pltpu.VMEM((2, tq, LANES), F32),
        ],
        compiler_params=_cparams(("parallel", "parallel", "arbitrary")),
        name="fox_attention",
    )(fq, fk, fv)


def _merge_kernel(ya_ref, yb_ref, mg_ref, x_ref, mod_ref, gpost_ref, gpre_ref,
                  wa_ref, wb_ref, wo_ref, wrh_ref, wrl_ref, br_ref,
                  x1_ref, h2_ref, lg_ref):
    D = D_MODEL
    a = _dot(ya_ref[0], wa_ref[...])
    bq = _dot(yb_ref[0], wb_ref[...])
    mg = mg_ref[0]
    u = mg[:, :D].astype(F32) * a + mg[:, D:].astype(F32) * bq
    mixed = _dot(u.astype(BF16), wo_ref[...])
    x1 = x_ref[0] + mod_ref[0, 2:3, :] * _rms(mixed, gpost_ref[...])
    x1_ref[0] = x1
    h2 = _rms(x1, gpre_ref[...]) * (1.0 + mod_ref[0, 4:5, :]) + mod_ref[0, 3:4, :]
    hi = h2.astype(BF16)
    lo = (h2 - hi.astype(F32)).astype(BF16)
    h2_ref[0] = hi
    lg_ref[0] = _dot(hi, wrh_ref[...]) + _dot(lo, wrh_ref[...]) + _dot(hi, wrl_ref[...]) + br_ref[...]


def _merge(ya, yb, mg, x, mod, gpost, gpre, wa, wb, wo, wrh, wrl, br, tm=256):
    B, S, D = x.shape
    c2 = lambda b, i: (0, 0)
    row = lambda w: pl.BlockSpec((1, tm, w), lambda b, i: (b, i, 0))
    return pl.pallas_call(
        _merge_kernel,
        grid=(B, S // tm),
        in_specs=[
            row(NSA_W), row(FOX_W), row(2 * D), row(D),
            pl.BlockSpec((1, 6, D), lambda b, i: (b, 0, 0)),
            pl.BlockSpec((1, D), c2), pl.BlockSpec((1, D), c2),
            pl.BlockSpec((NSA_W, D), c2), pl.BlockSpec((FOX_W, D), c2), pl.BlockSpec((D, D), c2),
            pl.BlockSpec((D, LANES), c2), pl.BlockSpec((D, LANES), c2), pl.BlockSpec((1, LANES), c2),
        ],
        out_specs=[row(D), row(D), row(LANES)],
        out_shape=[
            jax.ShapeDtypeStruct((B, S, D), F32),
            jax.ShapeDtypeStruct((B, S, D), BF16),
            jax.ShapeDtypeStruct((B, S, LANES), F32),
        ],
        compiler_params=_cparams(("parallel", "parallel")),
        name="merge",
    )(ya, yb, mg, x, mod, gpost, gpre, wa, wb, wo, wrh, wrl, br)


def _expert_kernel(be_ref, na_ref, x_ref, wg_ref, wu_ref, wd_ref, o_ref):
    i = pl.program_id(0)

    @pl.when(i < na_ref[0])
    def _():
        x = x_ref[...]
        gate = _dot(x, wg_ref[0])
        up = _dot(x, wu_ref[0])
        mid = (gate * jax.nn.sigmoid(gate) * up).astype(BF16)
        o_ref[...] = _dot(mid, wd_ref[0])

    @pl.when(i >= na_ref[0])
    def _():
        o_ref[...] = jnp.zeros(o_ref.shape, o_ref.dtype)


def _experts(block_expert, n_active, xb, wg, wu, wd):
    cap, D = xb.shape
    nblk = cap // MOE_TILE
    grid_spec = pltpu.PrefetchScalarGridSpec(
        num_scalar_prefetch=2,
        grid=(nblk,),
        in_specs=[
            pl.BlockSpec((MOE_TILE, D), lambda i, be, na: (i, 0)),
            pl.BlockSpec((1, D, D_EXPERT), lambda i, be, na: (be[i], 0, 0)),
            pl.BlockSpec((1, D, D_EXPERT), lambda i, be, na: (be[i], 0, 0)),
            pl.BlockSpec((1, D_EXPERT, D), lambda i, be, na: (be[i], 0, 0)),
        ],
        out_specs=pl.BlockSpec((MOE_TILE, D), lambda i, be, na: (i, 0)),
    )
    return pl.pallas_call(
        _expert_kernel,
        grid_spec=grid_spec,
        out_shape=jax.ShapeDtypeStruct((cap, D), F32),
        compiler_params=_cparams(("arbitrary",)),
        name="experts",
    )(block_expert, n_active, xb, wg, wu, wd)


def _final_kernel(x1_ref, y_ref, mod_ref, g_ref, o_ref):
    o_ref[0] = x1_ref[0] + mod_ref[0, 5:6, :] * _rms(y_ref[0], g_ref[...])


def _final(x1, y, mod, g, tm=512):
    B, S, D = x1.shape
    row = pl.BlockSpec((1, tm, D), lambda b, i: (b, i, 0))
    return pl.pallas_call(
        _final_kernel,
        grid=(B, S // tm),
        in_specs=[row, row, pl.BlockSpec((1, 6, D), lambda b, i: (b, 0, 0)),
                  pl.BlockSpec((1, D), lambda b, i: (0, 0))],
        out_specs=row,
        out_shape=jax.ShapeDtypeStruct((B, S, D), F32),
        compiler_params=_cparams(("parallel", "parallel")),
        name="final",
    )(x1, y, mod, g)


def _overlap_matrix():
    n = np.arange(N_CMP_PAD)[:, None]
    j = np.arange(LANES)[None, :]
    start = n * CMP_STRIDE
    ov = (start < j * SEL_LEN + SEL_LEN) & (start + CMP_LEN - 1 >= j * SEL_LEN) & (n < N_CMP_PAD - 1)
    return jnp.asarray(ov.astype(np.float32), dtype=BF16)


def _pad_cols(w, width=LANES):
    return jnp.pad(w, ((0, 0), (0, width - w.shape[1])))


def _route(logits, T):
    pg = jax.nn.softmax(logits[:, :N_EXPERT_GROUPS], axis=-1)
    pg_top, g_idx = lax.top_k(pg, 1)
    le = logits[:, N_EXPERT_GROUPS:N_EXPERT_GROUPS + N_EXPERTS].reshape(T, N_EXPERT_GROUPS, EXPERTS_PER_GROUP)
    le = jnp.take_along_axis(le, g_idx[:, :, None], axis=1)[:, 0]
    top_l, e_local = lax.top_k(le, EXPERT_TOP_K)
    weight = pg_top * jax.nn.softmax(top_l, axis=-1)
    expert = g_idx * EXPERTS_PER_GROUP + e_local
    A = T * EXPERT_TOP_K
    e_flat = expert.reshape(A)
    onehot = (e_flat[:, None] == jnp.arange(N_EXPERTS)[None, :]).astype(jnp.int32)
    rank = jnp.take_along_axis(jnp.cumsum(onehot, axis=0) - onehot, e_flat[:, None], axis=1)[:, 0]
    counts = jnp.sum(onehot, axis=0)
    padded = (counts + MOE_TILE - 1) // MOE_TILE * MOE_TILE
    pad_end = jnp.cumsum(padded)
    pad_start = pad_end - padded
    dest = pad_start[e_flat] + rank
    cap = -(-(A + N_EXPERTS * (MOE_TILE - 1)) // MOE_TILE) * MOE_TILE
    nblk = cap // MOE_TILE
    n_active = (pad_end[-1] // MOE_TILE).astype(jnp.int32)
    blk = jnp.arange(nblk) * MOE_TILE
    block_expert = jnp.minimum(jnp.searchsorted(pad_end, blk, side='right'), N_EXPERTS - 1)
    last = block_expert[jnp.maximum(n_active - 1, 0)]
    block_expert = jnp.where(jnp.arange(nblk) < n_active, block_expert, last).astype(jnp.int32)
    tok = jnp.arange(A, dtype=jnp.int32) // EXPERT_TOP_K
    buf_tok = jnp.zeros((cap,), jnp.int32).at[dest].set(tok)
    return weight, dest.reshape(T, EXPERT_TOP_K), buf_tok, block_expert, n_active.reshape(1)


def kernel(x, c, w_ada, b_ada, g_pre_mix, g_post_mix, g_pre_ffn, g_post_ffn, w_in, b_forget,
           cmp_pe_k, cmp_w1_k, cmp_w2_k, cmp_pe_v, cmp_w1_v, cmp_w2_v,
           w_o_nsa, w_o_fox, w_out, w_router_group, b_router_group, w_router_expert, b_router_expert,
           w_exp_gate, w_exp_up, w_exp_down):
    B, S, D = x.shape
    T = B * S
    depth = w_ada.shape[0]
    ov = _overlap_matrix()
    tri = jnp.asarray(np.tril(np.ones((IN_TILE, IN_TILE), np.float32)), dtype=BF16)
    row_feat = _row_features(S)
    placement = _placement()
    cmp_ext = _cmp_key_ext()
    for l in range(depth):
        mod = (jax.nn.silu(c) @ w_ada[l] + b_ada[l]).reshape(B, 6, D)
        w_qa, w_kva, w_gl, w_fox, w_f, w_mg = jnp.split(w_in[l], IN_SPLITS, axis=-1)
        w_big = jnp.concatenate([w_qa, w_kva, w_fox, w_mg], axis=1).astype(BF16)
        w_small = _pad_cols(jnp.concatenate([w_gl, w_f], axis=1)).astype(BF16)
        bf_pad = jnp.pad(b_forget[l], (F_LANE, LANES - F_LANE - FOX_HEADS)).reshape(1, LANES)
        qa, ckv, ksl, nkv, fq, fk, fv, mg, sm = _inproj(
            x, mod, g_pre_mix[l].reshape(1, D), w_big, w_small, bf_pad, tri, row_feat, placement)

        half = CMP_LEN // 2
        pe = jnp.stack([cmp_pe_k[l], cmp_pe_v[l]]).reshape(2, 2, 1, half * HEAD_DIM)
        w1 = jnp.stack([cmp_w1_k[l], cmp_w1_v[l]]).reshape(2, 2, half * HEAD_DIM, HEAD_DIM).astype(BF16)
        w2 = jnp.pad(jnp.stack([cmp_w2_k[l], cmp_w2_v[l]]), ((0, 0), (0, 0), (0, LANES - HEAD_DIM))).astype(BF16)
        kvc = _compress(ckv.reshape(B, 4, S // CMP_STRIDE, CMP_STRIDE * HEAD_DIM), pe, w1, w2, cmp_ext)
        ocg, selb, flags = _cmp_attention(qa, kvc, sm, ov)
        nq = S // Q_TILE
        per_tile = K_TILE // SEL_LEN
        tile_any = jnp.max(flags.reshape(B, NSA_KV_GROUPS, nq, LANES // per_tile, per_tile), axis=-1)
        bits = tile_any.astype(jnp.uint32) << jnp.arange(LANES // per_tile, dtype=jnp.uint32)
        flag_words = lax.bitcast_convert_type(jnp.sum(bits, axis=-1, dtype=jnp.uint32), jnp.int32).reshape(-1)
        y_a = _selwin_attention(flag_words, qa, ksl, nkv, selb, ocg, sm)

        y_b = _fox_attention(fq, fk, fv)

        w_r = _pad_cols(jnp.concatenate([w_router_group[l], w_router_expert[l]], axis=1))
        w_rh = w_r.astype(BF16)
        w_rl = (w_r - w_rh.astype(F32)).astype(BF16)
        b_r = _pad_cols(jnp.concatenate([b_router_group[l], b_router_expert[l]]).reshape(1, -1))
        x1, h2, logits = _merge(y_a, y_b, mg, x, mod, g_post_mix[l].reshape(1, D), g_pre_ffn[l].reshape(1, D),
                                w_o_nsa[l].astype(BF16), w_o_fox[l].astype(BF16), w_out[l].astype(BF16),
                                w_rh, w_rl, b_r)

        weight, dest, buf_tok, block_expert, n_active = _route(logits.reshape(T, LANES), T)
        xb = h2.reshape(T, D)[buf_tok]
        yb = _experts(block_expert, n_active, xb, w_exp_gate[l].astype(BF16), w_exp_up[l].astype(BF16),
                      w_exp_down[l].astype(BF16))
        y = weight[:, 0:1] * yb[dest[:, 0]] + weight[:, 1:2] * yb[dest[:, 1]]
        x = _final(x1, y.reshape(B, S, D), mod, g_post_ffn[l].reshape(1, D))
    return x
```

```python
import functools

import ml_dtypes
import numpy as np
import jax
import jax.numpy as jnp
from jax import lax
from jax.experimental import pallas as pl
from jax.experimental.pallas import tpu as pltpu

D_MODEL = 1024
HEAD_DIM = 64
NSA_HEADS = 8
NSA_KV_GROUPS = 2
NSA_HPG = NSA_HEADS // NSA_KV_GROUPS
FOX_HEADS = 8
CMP_LEN = 32
CMP_STRIDE = 16
SEL_LEN = 64
N_SEL = 16
WINDOW = 512
N_EXPERT_GROUPS = 4
EXPERTS_PER_GROUP = 8
N_EXPERTS = N_EXPERT_GROUPS * EXPERTS_PER_GROUP
EXPERT_TOP_K = 2
D_EXPERT = D_MODEL // 2
NORM_EPS = 1e-6
NEG = -1e30
FORCE = 1e9
LOG2E = 1.4426950408889634

NSA_W = NSA_HEADS * HEAD_DIM
NSA_KV_W = NSA_KV_GROUPS * HEAD_DIM
FOX_W = FOX_HEADS * HEAD_DIM
IN_SIZES = (NSA_W, 6 * NSA_KV_W, 3 * NSA_HEADS, 3 * FOX_W, FOX_HEADS, 2 * D_MODEL)
IN_SPLITS = tuple(int(v) for v in np.cumsum(IN_SIZES)[:-1])

LANES = 128
Q_TILE = 128
K_TILE = 256
N_CMP_PAD = 512
MOE_TILE = 256
IN_TILE = 512
MERGE_TILE = 256
FOX_HPS = 4
VMEM_LIMIT = 56 * 1024 * 1024

F_LANE = 3 * NSA_HEADS
U_LANE = 64
ONE_LANE = 88
A_LANE = 89
B_LANE = 90
EXT = HEAD_DIM
G_FQ, G_FK, G_NQ, G_NK, N_GROUPS = 0, 8, 16, 24, 25

F32 = jnp.float32
BF16 = jnp.bfloat16


def _dot(a, b):
    return jnp.dot(a, b, preferred_element_type=F32)


def _dot_nt(a, b):
    return lax.dot_general(a, b, (((1,), (1,)), ((), ())), preferred_element_type=F32)


def _rms(x, g):
    return x * lax.rsqrt(jnp.mean(x * x, axis=-1, keepdims=True) + NORM_EPS) * g


def _cparams(sem):
    return pltpu.CompilerParams(dimension_semantics=sem, vmem_limit_bytes=VMEM_LIMIT)


def _split3(x):
    hi = x.astype(BF16).astype(F32)
    r = x - hi
    mid = r.astype(BF16).astype(F32)
    lo = (r - mid).astype(BF16).astype(F32)
    return hi, mid, lo


def _np_split3(x):
    x = np.asarray(x, np.float32)
    hi = x.astype(ml_dtypes.bfloat16).astype(np.float32)
    r = x - hi
    mid = r.astype(ml_dtypes.bfloat16).astype(np.float32)
    lo = (r - mid).astype(ml_dtypes.bfloat16).astype(np.float32)
    return hi, mid, lo


def _alibi_c():
    slopes = np.exp2(-8.0 * np.arange(1, NSA_HEADS + 1, dtype=np.float32) / NSA_HEADS).astype(np.float32)
    return slopes * np.float32(LOG2E)


def _row_features(S):
    t = np.arange(S, dtype=np.float32)
    c = _alibi_c()
    rs = np.zeros((S, LANES), np.float32)
    for h in range(NSA_HEADS):
        for j, term in enumerate(_np_split3(c[h] * t)):
            rs[:, U_LANE + 8 * j + h] = -term
    rs[:, ONE_LANE] = 1.0
    rs[:, A_LANE] = np.floor(t / LANES)
    rs[:, B_LANE] = t % LANES
    return jnp.asarray(rs, dtype=BF16)


def _placement():
    c = _alibi_c()
    p = np.zeros((LANES, N_GROUPS * LANES), np.float32)
    for h in range(FOX_HEADS):
        q0 = (G_FQ + h) * LANES + EXT
        k0 = (G_FK + h) * LANES + EXT
        for j in range(3):
            p[ONE_LANE, q0 + j] = -1.0
            p[F_LANE + 8 * j + h, q0 + 3 + j] = 1.0
            p[F_LANE + 8 * j + h, k0 + j] = 1.0
            p[ONE_LANE, k0 + 3 + j] = 1.0
    for h in range(NSA_HEADS):
        q0 = (G_NQ + h) * LANES + EXT
        c128 = _np_split3(c[h] * np.float32(LANES))
        c1 = _np_split3(c[h])
        for j in range(3):
            p[U_LANE + 8 * j + h, q0 + j] = 1.0
            p[ONE_LANE, q0 + 3 + j] = c128[j]
            p[ONE_LANE, q0 + 6 + j] = c1[j]
    k0 = G_NK * LANES + EXT
    for j in range(3):
        p[ONE_LANE, k0 + j] = 1.0
        p[A_LANE, k0 + 3 + j] = 1.0
        p[B_LANE, k0 + 6 + j] = 1.0
    return jnp.asarray(p, dtype=BF16)


def _cmp_key_ext():
    pos = np.arange(N_CMP_PAD, dtype=np.float32) * CMP_STRIDE + (CMP_LEN - 1)
    e = np.zeros((2, N_CMP_PAD, LANES), np.float32)
    for j in range(3):
        e[0, :, EXT + j] = 1.0
        e[0, :, EXT + 3 + j] = np.floor(pos / LANES)
        e[0, :, EXT + 6 + j] = pos % LANES
    return jnp.asarray(e, dtype=BF16)


def _inproj_kernel(x_ref, mod_ref, g_ref, wb_ref, ws_ref, bf_ref, tri_ref, rs_ref, p_ref,
                   qa_ref, ckv_ref, ksl_ref, nkv_ref, fq_ref, fk_ref, fv_ref, mg_ref, sm_ref, carry_sc):
    i = pl.program_id(1)
    tm = x_ref.shape[1]
    x = x_ref[0]
    h = _rms(x, g_ref[...]) * (1.0 + mod_ref[0, 1:2, :]) + mod_ref[0, 0:1, :]
    hb = h.astype(BF16)
    lane = lax.broadcasted_iota(jnp.int32, (tm, LANES), 1)
    lower = lane < HEAD_DIM
    ones_col = (lane == EXT).astype(F32)

    z = _dot(hb, ws_ref[...]) + bf_ref[...]
    logsig = jnp.minimum(z, 0.0) - jnp.log1p(jnp.exp(-jnp.abs(z)))
    sm_ref[0] = jnp.where(lane < F_LANE, jax.nn.sigmoid(z), logsig)

    @pl.when(i == 0)
    def _():
        carry_sc[...] = jnp.zeros(carry_sc.shape, F32)

    is_f = (lane >= F_LANE) & (lane < F_LANE + FOX_HEADS)
    l_hi, l_mid, l_lo = _split3(jnp.where(is_f, logsig, 0.0))
    tri = tri_ref[...]
    cum = carry_sc[...] + _dot(tri, l_hi.astype(BF16)) + _dot(tri, l_mid.astype(BF16)) + _dot(tri, l_lo.astype(BF16))
    carry_sc[...] = cum[tm - 1:tm, :]
    f_hi, f_mid, f_lo = _split3(cum * LOG2E)
    feat = (f_hi + pltpu.roll(f_mid, 8, 1) + pltpu.roll(f_lo, 16, 1) + rs_ref[...].astype(F32)).astype(BF16)

    def ext(group):
        return _dot(feat, p_ref[:, group * LANES:(group + 1) * LANES])

    def piece(acc, idx, extra):
        pair = acc[:, (idx // 2) * LANES:(idx // 2 + 1) * LANES]
        if idx % 2:
            pair = pltpu.roll(pair, HEAD_DIM, 1)
        return jnp.where(lower, pair, extra).astype(BF16)

    qscale = (HEAD_DIM ** -0.5) * LOG2E
    acc = _dot(hb, wb_ref[:, 0:NSA_W]) * qscale
    for hd in range(NSA_HEADS):
        qa_ref[0, hd] = piece(acc, hd, ext(G_NQ + hd))
    off = NSA_W
    acc = _dot(hb, wb_ref[:, off:off + 6 * NSA_KV_W])
    for pc in range(4):
        ckv_ref[0, pc] = acc[:, pc * HEAD_DIM:(pc + 1) * HEAD_DIM].astype(BF16)
    ext_k = ext(G_NK)
    t = i * tm + lax.broadcasted_iota(jnp.int32, (tm, LANES), 0)
    block_onehot = (lane == t // SEL_LEN).astype(BF16)
    for g in range(NSA_KV_GROUPS):
        ksl_ref[0, g, :, 0:LANES] = piece(acc, 4 + g, ext_k)
        ksl_ref[0, g, :, LANES:2 * LANES] = block_onehot
        nkv_ref[0, g] = piece(acc, 6 + g, ones_col)
        nkv_ref[0, 2 + g] = piece(acc, 8 + g, ext_k)
        nkv_ref[0, 4 + g] = piece(acc, 10 + g, ones_col)
    off += 6 * NSA_KV_W
    acc = _dot(hb, wb_ref[:, off:off + FOX_W]) * qscale
    for hd in range(FOX_HEADS):
        fq_ref[0, hd] = piece(acc, hd, ext(G_FQ + hd))
    off += FOX_W
    acc = _dot(hb, wb_ref[:, off:off + FOX_W])
    for hd in range(FOX_HEADS):
        fk_ref[0, hd] = piece(acc, hd, ext(G_FK + hd))
    off += FOX_W
    acc = _dot(hb, wb_ref[:, off:off + FOX_W])
    for hd in range(FOX_HEADS):
        fv_ref[0, hd] = piece(acc, hd, ones_col)
    off += FOX_W
    for c in range(4):
        acc = _dot(hb, wb_ref[:, off + c * 512: off + (c + 1) * 512])
        mg_ref[0, :, c * 512:(c + 1) * 512] = jax.nn.sigmoid(acc).astype(BF16)


def _inproj(x, mod, g, wb, ws, bfp, tri, rs, pm):
    B, S, D = x.shape
    tm = IN_TILE
    nb = wb.shape[1]
    const2 = lambda b, i: (0, 0)
    heads = lambda n: pl.BlockSpec((1, n, tm, LANES), lambda b, i: (b, 0, i, 0))
    hshape = lambda n: jax.ShapeDtypeStruct((B, n, S, LANES), BF16)
    return pl.pallas_call(
        _inproj_kernel,
        grid=(B, S // tm),
        in_specs=[
            pl.BlockSpec((1, tm, D), lambda b, i: (b, i, 0)),
            pl.BlockSpec((1, 6, D), lambda b, i: (b, 0, 0)),
            pl.BlockSpec((1, D), const2),
            pl.BlockSpec((D, nb), const2),
            pl.BlockSpec((D, LANES), const2),
            pl.BlockSpec((1, LANES), const2),
            pl.BlockSpec((tm, tm), const2),
            pl.BlockSpec((tm, LANES), lambda b, i: (i, 0)),
            pl.BlockSpec((LANES, N_GROUPS * LANES), const2),
        ],
        out_specs=[
            heads(NSA_HEADS),
            pl.BlockSpec((1, 4, tm, HEAD_DIM), lambda b, i: (b, 0, i, 0)),
            pl.BlockSpec((1, NSA_KV_GROUPS, tm, 2 * LANES), lambda b, i: (b, 0, i, 0)),
            heads(6), heads(FOX_HEADS), heads(FOX_HEADS), heads(FOX_HEADS),
            pl.BlockSpec((1, tm, 2 * D), lambda b, i: (b, i, 0)),
            pl.BlockSpec((1, tm, LANES), lambda b, i: (b, i, 0)),
        ],
        out_shape=[
            hshape(NSA_HEADS),
            jax.ShapeDtypeStruct((B, 4, S, HEAD_DIM), BF16),
            jax.ShapeDtypeStruct((B, NSA_KV_GROUPS, S, 2 * LANES), BF16),
            hshape(6), hshape(FOX_HEADS), hshape(FOX_HEADS), hshape(FOX_HEADS),
            jax.ShapeDtypeStruct((B, S, 2 * D), BF16),
            jax.ShapeDtypeStruct((B, S, LANES), F32),
        ],
        scratch_shapes=[pltpu.VMEM((1, LANES), F32)],
        compiler_params=_cparams(("parallel", "arbitrary")),
        name="inproj",
    )(x, mod, g, wb, ws, bfp, tri, rs, pm)


def _compress_kernel(x_ref, pe_ref, w1_ref, w2_ref, e_ref, o_ref):
    x = x_ref[0, 0].astype(F32)
    x_lo = (x + pe_ref[0, 0]).astype(BF16)
    x_hi = (x + pe_ref[0, 1]).astype(BF16)
    y_lo = _dot(x_lo, w1_ref[0, 0])
    y_hi = _dot(x_hi, w1_ref[0, 1])
    n = y_hi.shape[0]
    hid = y_lo + pltpu.roll(y_hi, n - 1, 0)
    hid = jax.nn.gelu(hid)
    o_ref[0, 0] = (_dot(hid.astype(BF16), w2_ref[0]) + e_ref[0].astype(F32)).astype(BF16)


def _compress(kv_rows, pe, w1, w2, e):
    B = kv_rows.shape[0]
    R, C = kv_rows.shape[2], kv_rows.shape[3]
    return pl.pallas_call(
        _compress_kernel,
        grid=(B, 4),
        in_specs=[
            pl.BlockSpec((1, 1, R, C), lambda b, p: (b, p, 0, 0)),
            pl.BlockSpec((1, 2, 1, C), lambda b, p: (p // 2, 0, 0, 0)),
            pl.BlockSpec((1, 2, C, HEAD_DIM), lambda b, p: (p // 2, 0, 0, 0)),
            pl.BlockSpec((1, HEAD_DIM, LANES), lambda b, p: (p // 2, 0, 0)),
            pl.BlockSpec((1, R, LANES), lambda b, p: (p // 2, 0, 0)),
        ],
        out_specs=pl.BlockSpec((1, 1, R, LANES), lambda b, p: (b, p, 0, 0)),
        out_shape=jax.ShapeDtypeStruct((B, 4, R, LANES), BF16),
        compiler_params=_cparams(("parallel", "parallel")),
        name="compress",
    )(kv_rows, pe, w1, w2, e)


def _gate_rows(sm, g, branch):
    col = lax.broadcasted_iota(jnp.int32, sm.shape, 1)
    parts = []
    for hl in range(NSA_HPG):
        want = 3 * (NSA_HPG * g + hl) + branch
        parts.append(jnp.sum(jnp.where(col == want, sm, 0.0), axis=-1, keepdims=True))
    return jnp.concatenate(parts, axis=0)


def _head_tile(y):
    lane = lax.broadcasted_iota(jnp.int32, (Q_TILE, LANES), 1)
    hs = [y[i * Q_TILE:(i + 1) * Q_TILE] for i in range(NSA_HPG)]
    pairs = [jnp.where(lane < HEAD_DIM, hs[2 * i], pltpu.roll(hs[2 * i + 1], HEAD_DIM, 1)) for i in range(2)]
    return jnp.concatenate(pairs, axis=1)


def _cmp_kernel(q_ref, kc_ref, vc_ref, sm_ref, ovt_ref, oc_ref, selb_ref, flag_ref):
    g = pl.program_id(1)
    qb = pl.program_id(2)
    q0 = qb * Q_TILE
    q = q_ref[0].reshape(NSA_HPG * Q_TILE, LANES)
    s = _dot_nt(q, kc_ref[0, 0])
    r = lax.broadcasted_iota(jnp.int32, (NSA_HPG * Q_TILE, 1), 0) % Q_TILE
    n = lax.broadcasted_iota(jnp.int32, (1, N_CMP_PAD), 1)
    dc = (q0 + r) - (n * CMP_STRIDE + (CMP_LEN - 1))
    mask = (dc >= 0) & (n < N_CMP_PAD - 1)
    l = jnp.where(mask, s, NEG)
    m = jnp.max(l, axis=-1, keepdims=True)
    e = jnp.where(mask, jnp.exp2(l - m), 0.0)
    pc = e / jnp.maximum(jnp.sum(e, axis=-1, keepdims=True), 1e-30)
    oc = _dot(pc.astype(BF16), vc_ref[0, 0])
    oc_ref[0] = _head_tile(oc * _gate_rows(sm_ref[0], g, 0))
    ps = pc[0:Q_TILE]
    for i in range(1, NSA_HPG):
        ps = ps + pc[i * Q_TILE:(i + 1) * Q_TILE]
    ps_hi = ps.astype(BF16)
    ps_lo = (ps - ps_hi.astype(F32)).astype(BF16)
    imp = _dot_nt(ovt_ref[...], ps_hi) + _dot_nt(ovt_ref[...], ps_lo)
    j = lax.broadcasted_iota(jnp.int32, imp.shape, 0)
    jf = j.astype(F32)
    t = q0 + lax.broadcasted_iota(jnp.int32, (1, Q_TILE), 1)
    cur = t // SEL_LEN
    forced = (j == 0) | (j == cur) | (j == cur - 1)
    v = jnp.where(j > cur, -FORCE, jnp.where(forced, FORCE, imp))
    sel = jnp.zeros(imp.shape, jnp.bool_)
    for _ in range(N_SEL):
        mx = jnp.max(v, axis=0, keepdims=True)
        idx = jnp.min(jnp.where(v == mx, jf, float(LANES)), axis=0, keepdims=True)
        pick = jf == idx
        sel = sel | pick
        v = jnp.where(pick, -3e38, v)
    live_t = jnp.where(sel & (j <= cur), 1.0, 0.0).astype(BF16)
    eye = (lax.broadcasted_iota(jnp.int32, imp.shape, 0) == lax.broadcasted_iota(jnp.int32, imp.shape, 1))
    live = _dot_nt(eye.astype(BF16), live_t)
    selb_ref[0, 0] = jnp.where(live > 0.5, 0.0, NEG).astype(BF16)
    flag_ref[0, 0, 0] = jnp.max(live, axis=0, keepdims=True).astype(jnp.int32)


def _cmp_attention(qa, kvc, sm, ov):
    B, H, S, _ = qa.shape
    G = NSA_KV_GROUPS
    nq = S // Q_TILE
    return pl.pallas_call(
        _cmp_kernel,
        grid=(B, G, nq),
        in_specs=[
            pl.BlockSpec((1, NSA_HPG, Q_TILE, LANES), lambda b, g, i: (b, g, i, 0)),
            pl.BlockSpec((1, 1, N_CMP_PAD, LANES), lambda b, g, i: (b, g, 0, 0)),
            pl.BlockSpec((1, 1, N_CMP_PAD, LANES), lambda b, g, i: (b, 2 + g, 0, 0)),
            pl.BlockSpec((1, Q_TILE, LANES), lambda b, g, i: (b, i, 0)),
            pl.BlockSpec((LANES, N_CMP_PAD), lambda b, g, i: (0, 0)),
        ],
        out_specs=[
            pl.BlockSpec((1, Q_TILE, NSA_HPG * HEAD_DIM), lambda b, g, i: (b, i, g)),
            pl.BlockSpec((1, 1, Q_TILE, LANES), lambda b, g, i: (b, g, i, 0)),
            pl.BlockSpec((1, 1, 1, 1, LANES), lambda b, g, i: (b, g, i, 0, 0)),
        ],
        out_shape=[
            jax.ShapeDtypeStruct((B, S, NSA_W), F32),
            jax.ShapeDtypeStruct((B, G, S, LANES), BF16),
            jax.ShapeDtypeStruct((B, G, nq, 1, LANES), jnp.int32),
        ],
        compiler_params=_cparams(("parallel", "parallel", "parallel")),
        name="cmp_attention",
    )(qa, kvc, kvc, sm, ov)


def _online_update(s, v, m_ref, acc_ref):
    m_old = m_ref[...]
    m_new = jnp.maximum(m_old, jnp.max(s, axis=-1, keepdims=True))
    p = jnp.exp2(s - pltpu.repeat(m_new, s.shape[1] // LANES, 1))
    acc_ref[...] = jnp.exp2(m_old - m_new) * acc_ref[...] + _dot(p.astype(BF16), v)
    m_ref[...] = m_new


def _normalized(acc):
    return acc / jnp.maximum(acc[:, EXT:EXT + 1], 1e-30)


def _selwin_kernel(flags_ref, q_ref, ks_ref, vs_ref, kw_ref, vw_ref, selb_ref, oc_ref, sm_ref,
                   o_ref, m_sc, acc_sc):
    b = pl.program_id(0)
    g = pl.program_id(1)
    qb = pl.program_id(2)
    nq = pl.num_programs(2)
    rows = NSA_HPG * Q_TILE
    q4 = q_ref[0].reshape(rows, LANES)
    q_aug = jnp.concatenate([q4, jnp.concatenate([selb_ref[0, 0]] * NSA_HPG, axis=0)], axis=1)
    r = lax.broadcasted_iota(jnp.int32, (rows, 1), 0) % Q_TILE
    c = lax.broadcasted_iota(jnp.int32, (1, K_TILE), 1)
    rel = r - c
    diag = qb // (K_TILE // Q_TILE)

    def reset():
        m_sc[...] = jnp.full(m_sc.shape, NEG, F32)
        acc_sc[...] = jnp.zeros(acc_sc.shape, F32)

    def sel_tile(kt, causal):
        start = pl.multiple_of(kt * K_TILE, K_TILE)
        s = _dot_nt(q_aug, ks_ref[0, 0, pl.ds(start, K_TILE), :])
        if causal:
            s = jnp.where(rel + (qb * Q_TILE - kt * K_TILE) >= 0, s, NEG)
        _online_update(s, vs_ref[0, 0, pl.ds(start, K_TILE), :], m_sc, acc_sc)

    reset()
    word = flags_ref[(b * NSA_KV_GROUPS + g) * nq + qb]

    def body(kt, carry):
        @pl.when(((word >> kt) & 1) == 1)
        def _():
            sel_tile(kt, False)
        return carry

    lax.fori_loop(0, diag, body, 0)
    sel_tile(diag, True)
    o_sel = _normalized(acc_sc[...])

    reset()
    for d in range(WINDOW // K_TILE, -1, -1):
        def win_tile(d=d):
            kt = diag - d
            start = pl.multiple_of(kt * K_TILE, K_TILE)
            dist = rel + (qb * Q_TILE - kt * K_TILE)
            s = _dot_nt(q4, kw_ref[0, 0, pl.ds(start, K_TILE), :])
            s = jnp.where((dist >= 0) & (dist < WINDOW), s, NEG)
            _online_update(s, vw_ref[0, 0, pl.ds(start, K_TILE), :], m_sc, acc_sc)
        if d == 0:
            win_tile()
        else:
            pl.when(diag >= d)(win_tile)
    o_win = _normalized(acc_sc[...])

    sm = sm_ref[0]
    y = _gate_rows(sm, g, 1) * o_sel + _gate_rows(sm, g, 2) * o_win
    o_ref[0] = (oc_ref[0] + _head_tile(y)).astype(BF16)


def _selwin_attention(flag_words, qa, ksl, nkv, selb, ocg, sm):
    B, H, S, _ = qa.shape
    G = NSA_KV_GROUPS
    nq = S // Q_TILE
    rows = NSA_HPG * Q_TILE
    kv_spec = lambda piece: pl.BlockSpec((1, 1, S, LANES), lambda b, g, i, f: (b, piece + g, 0, 0))
    out_tile = pl.BlockSpec((1, Q_TILE, NSA_HPG * HEAD_DIM), lambda b, g, i, f: (b, i, g))
    grid_spec = pltpu.PrefetchScalarGridSpec(
        num_scalar_prefetch=1,
        grid=(B, G, nq),
        in_specs=[
            pl.BlockSpec((1, NSA_HPG, Q_TILE, LANES), lambda b, g, i, f: (b, g, i, 0)),
            pl.BlockSpec((1, 1, S, 2 * LANES), lambda b, g, i, f: (b, g, 0, 0)),
            kv_spec(0), kv_spec(2), kv_spec(4),
            pl.BlockSpec((1, 1, Q_TILE, LANES), lambda b, g, i, f: (b, g, i, 0)),
            out_tile,
            pl.BlockSpec((1, Q_TILE, LANES), lambda b, g, i, f: (b, i, 0)),
        ],
        out_specs=out_tile,
        scratch_shapes=[
            pltpu.VMEM((rows, LANES), F32),
            pltpu.VMEM((rows, LANES), F32),
        ],
    )
    return pl.pallas_call(
        _selwin_kernel,
        grid_spec=grid_spec,
        out_shape=jax.ShapeDtypeStruct((B, S, NSA_W), BF16),
        compiler_params=_cparams(("parallel", "parallel", "arbitrary")),
        name="selwin_attention",
    )(flag_words, qa, ksl, nkv, nkv, nkv, selb, ocg, sm)


def _fox_kernel(q_ref, k_ref, v_ref, o_ref, m_sc, acc_sc, *, tq):
    qi = pl.program_id(2)
    m_sc[...] = jnp.full(m_sc.shape, NEG, F32)
    acc_sc[...] = jnp.zeros(acc_sc.shape, F32)

    def tile(kt, causal):
        start = pl.multiple_of(kt * tq, tq)
        for hh in range(FOX_HPS):
            s = _dot_nt(q_ref[0, hh], k_ref[0, hh, pl.ds(start, tq), :])
            if causal:
                r = lax.broadcasted_iota(jnp.int32, s.shape, 0)
                c = lax.broadcasted_iota(jnp.int32, s.shape, 1)
                s = jnp.where(r >= c, s, NEG)
            _online_update(s, v_ref[0, hh, pl.ds(start, tq), :], m_sc.at[hh], acc_sc.at[hh])

    def body(kt, carry):
        tile(kt, False)
        return carry

    lax.fori_loop(0, qi, body, 0)
    tile(qi, True)
    lane = lax.broadcasted_iota(jnp.int32, (tq, LANES), 1)
    o = [_normalized(acc_sc[hh]) for hh in range(FOX_HPS)]
    for pr in range(FOX_HPS // 2):
        o_ref[0, :, pr * LANES:(pr + 1) * LANES] = jnp.where(
            lane < HEAD_DIM, o[2 * pr], pltpu.roll(o[2 * pr + 1], HEAD_DIM, 1)).astype(BF16)


def _fox_attention(fq, fk, fv, tq=512):
    B, H, S, _ = fq.shape
    hps = FOX_HPS
    return pl.pallas_call(
        functools.partial(_fox_kernel, tq=tq),
        grid=(B, H // hps, S // tq),
        in_specs=[
            pl.BlockSpec((1, hps, tq, LANES), lambda b, h, i: (b, h, i, 0)),
            pl.BlockSpec((1, hps, S, LANES), lambda b, h, i: (b, h, 0, 0)),
            pl.BlockSpec((1, hps, S, LANES), lambda b, h, i: (b, h, 0, 0)),
        ],
        out_specs=pl.BlockSpec((1, tq, hps * HEAD_DIM), lambda b, h, i: (b, i, h)),
        out_shape=jax.ShapeDtypeStruct((B, S, FOX_W), BF16),
        scratch_shapes=[
            pltpu.VMEM((hps, tq, LANES), F32),
            pltpu.VMEM((hps, tq, LANES), F32),
        ],
        compiler_params=_cparams(("parallel", "parallel", "arbitrary")),
        name="fox_attention",
    )(fq, fk, fv)


def _merge_kernel(ya_ref, yb_ref, mg_ref, x_ref, mod_ref, gpost_ref, gpre_ref,
                  wa_ref, wb_ref, wo_ref, wrh_ref, wrl_ref, br_ref, stri_ref,
                  x1_ref, h2_ref, rt_ref, cnt_ref):
    D = D_MODEL

    @pl.when((pl.program_id(0) == 0) & (pl.program_id(1) == 0))
    def _():
        cnt_ref[...] = jnp.zeros(cnt_ref.shape, F32)

    a = _dot(ya_ref[0], wa_ref[...])
    bq = _dot(yb_ref[0], wb_ref[...])
    mg = mg_ref[0]
    u = mg[:, :D].astype(F32) * a + mg[:, D:].astype(F32) * bq
    mixed = _dot(u.astype(BF16), wo_ref[...])
    x1 = x_ref[0] + mod_ref[0, 2:3, :] * _rms(mixed, gpost_ref[...])
    x1_ref[0] = x1
    h2 = _rms(x1, gpre_ref[...]) * (1.0 + mod_ref[0, 4:5, :]) + mod_ref[0, 3:4, :]
    hi = h2.astype(BF16)
    lo = (h2 - hi.astype(F32)).astype(BF16)
    h2_ref[0] = hi
    lg = _dot(hi, wrh_ref[...]) + _dot(lo, wrh_ref[...]) + _dot(hi, wrl_ref[...]) + br_ref[...]

    lane = lax.broadcasted_iota(jnp.int32, lg.shape, 1)
    lanef = lane.astype(F32)
    no_lane = float(LANES)
    is_g = lane < N_EXPERT_GROUPS
    gl = jnp.where(is_g, lg, NEG)
    gmax = jnp.max(gl, axis=-1, keepdims=True)
    pg_top = 1.0 / jnp.sum(jnp.where(is_g, jnp.exp(gl - gmax), 0.0), axis=-1, keepdims=True)
    g_idx = jnp.min(jnp.where(is_g & (gl == gmax), lanef, no_lane), axis=-1, keepdims=True)
    in_grp = ((lane >= N_EXPERT_GROUPS) & (lane < N_EXPERT_GROUPS + N_EXPERTS)
              & (((lane - N_EXPERT_GROUPS) // EXPERTS_PER_GROUP).astype(F32) == g_idx))
    le = jnp.where(in_grp, lg, NEG)
    m1 = jnp.max(le, axis=-1, keepdims=True)
    i1 = jnp.min(jnp.where(in_grp & (le == m1), lanef, no_lane), axis=-1, keepdims=True)
    rest = in_grp & (lanef != i1)
    le2 = jnp.where(rest, lg, NEG)
    m2 = jnp.max(le2, axis=-1, keepdims=True)
    i2 = jnp.min(jnp.where(rest & (le2 == m2), lanef, no_lane), axis=-1, keepdims=True)
    e21 = jnp.exp(m2 - m1)
    w1 = pg_top / (1.0 + e21)
    w2 = w1 * e21
    pick1 = lanef == i1
    pick2 = lanef == i2
    onehot = jnp.where(pick1 | pick2, 1.0, 0.0)
    before = cnt_ref[...] + _dot(stri_ref[...], onehot.astype(BF16))
    rank1 = jnp.sum(jnp.where(pick1, before, 0.0), axis=-1, keepdims=True)
    rank2 = jnp.sum(jnp.where(pick2, before, 0.0), axis=-1, keepdims=True)
    cnt_ref[...] = cnt_ref[...] + jnp.sum(onehot, axis=0, keepdims=True)
    fields = [i1 - N_EXPERT_GROUPS, i2 - N_EXPERT_GROUPS, rank1, rank2, w1, w2]
    rt = jnp.zeros(lg.shape, F32)
    for k, f in enumerate(fields):
        rt = jnp.where(lane == k, f, rt)
    rt_ref[0] = rt


def _merge(ya, yb, mg, x, mod, gpost, gpre, wa, wb, wo, wrh, wrl, br, stri):
    B, S, D = x.shape
    tm = MERGE_TILE
    c2 = lambda b, i: (0, 0)
    row = lambda w: pl.BlockSpec((1, tm, w), lambda b, i: (b, i, 0))
    return pl.pallas_call(
        _merge_kernel,
        grid=(B, S // tm),
        in_specs=[
            row(NSA_W), row(FOX_W), row(2 * D), row(D),
            pl.BlockSpec((1, 6, D), lambda b, i: (b, 0, 0)),
            pl.BlockSpec((1, D), c2), pl.BlockSpec((1, D), c2),
            pl.BlockSpec((NSA_W, D), c2), pl.BlockSpec((FOX_W, D), c2), pl.BlockSpec((D, D), c2),
            pl.BlockSpec((D, LANES), c2), pl.BlockSpec((D, LANES), c2), pl.BlockSpec((1, LANES), c2),
            pl.BlockSpec((tm, tm), c2),
        ],
        out_specs=[row(D), row(D), row(LANES), pl.BlockSpec((1, LANES), c2)],
        out_shape=[
            jax.ShapeDtypeStruct((B, S, D), F32),
            jax.ShapeDtypeStruct((B, S, D), BF16),
            jax.ShapeDtypeStruct((B, S, LANES), F32),
            jax.ShapeDtypeStruct((1, LANES), F32),
        ],
        compiler_params=_cparams(("arbitrary", "arbitrary")),
        name="merge",
    )(ya, yb, mg, x, mod, gpost, gpre, wa, wb, wo, wrh, wrl, br, stri)


def _expert_kernel(be_ref, na_ref, x_ref, wg_ref, wu_ref, wd_ref, o_ref):
    i = pl.program_id(0)

    @pl.when(i < na_ref[0])
    def _():
        x = x_ref[...]
        gate = _dot(x, wg_ref[0])
        up = _dot(x, wu_ref[0])
        mid = (gate * jax.nn.sigmoid(gate) * up).astype(BF16)
        o_ref[...] = _dot(mid, wd_ref[0])

    @pl.when(i >= na_ref[0])
    def _():
        o_ref[...] = jnp.zeros(o_ref.shape, o_ref.dtype)


def _experts(block_expert, n_active, xb, wg, wu, wd):
    cap, D = xb.shape
    nblk = cap // MOE_TILE
    grid_spec = pltpu.PrefetchScalarGridSpec(
        num_scalar_prefetch=2,
        grid=(nblk,),
        in_specs=[
            pl.BlockSpec((MOE_TILE, D), lambda i, be, na: (i, 0)),
            pl.BlockSpec((1, D, D_EXPERT), lambda i, be, na: (be[i], 0, 0)),
            pl.BlockSpec((1, D, D_EXPERT), lambda i, be, na: (be[i], 0, 0)),
            pl.BlockSpec((1, D_EXPERT, D), lambda i, be, na: (be[i], 0, 0)),
        ],
        out_specs=pl.BlockSpec((MOE_TILE, D), lambda i, be, na: (i, 0)),
    )
    return pl.pallas_call(
        _expert_kernel,
        grid_spec=grid_spec,
        out_shape=jax.ShapeDtypeStruct((cap, D), F32),
        compiler_params=_cparams(("arbitrary",)),
        name="experts",
    )(block_expert, n_active, xb, wg, wu, wd)


def _final_kernel(x1_ref, y_ref, mod_ref, g_ref, o_ref):
    o_ref[0] = x1_ref[0] + mod_ref[0, 5:6, :] * _rms(y_ref[0], g_ref[...])


def _final(x1, y, mod, g, tm=512):
    B, S, D = x1.shape
    row = pl.BlockSpec((1, tm, D), lambda b, i: (b, i, 0))
    return pl.pallas_call(
        _final_kernel,
        grid=(B, S // tm),
        in_specs=[row, row, pl.BlockSpec((1, 6, D), lambda b, i: (b, 0, 0)),
                  pl.BlockSpec((1, D), lambda b, i: (0, 0))],
        out_specs=row,
        out_shape=jax.ShapeDtypeStruct((B, S, D), F32),
        compiler_params=_cparams(("parallel", "parallel")),
        name="final",
    )(x1, y, mod, g)


def _overlap_matrix():
    n = np.arange(N_CMP_PAD)[:, None]
    j = np.arange(LANES)[None, :]
    start = n * CMP_STRIDE
    ov = (start < j * SEL_LEN + SEL_LEN) & (start + CMP_LEN - 1 >= j * SEL_LEN) & (n < N_CMP_PAD - 1)
    return jnp.asarray(ov.T.astype(np.float32), dtype=BF16)


def _pad_cols(w, width=LANES):
    return jnp.pad(w, ((0, 0), (0, width - w.shape[1])))


def _dispatch_plan(rt, cnt, T):
    expert = rt[:, 0:2].astype(jnp.int32)
    rank = rt[:, 2:4].astype(jnp.int32)
    weight = rt[:, 4:6]
    counts = cnt[0, N_EXPERT_GROUPS:N_EXPERT_GROUPS + N_EXPERTS].astype(jnp.int32)
    padded = (counts + MOE_TILE - 1) // MOE_TILE * MOE_TILE
    pad_end = jnp.cumsum(padded)
    pad_start = pad_end - padded
    onehot = expert[:, :, None] == jnp.arange(N_EXPERTS)[None, None, :]
    dest = jnp.sum(jnp.where(onehot, pad_start[None, None, :], 0), axis=-1) + rank
    A = T * EXPERT_TOP_K
    cap = -(-(A + N_EXPERTS * (MOE_TILE - 1)) // MOE_TILE) * MOE_TILE
    nblk = cap // MOE_TILE
    n_active = (pad_end[-1] // MOE_TILE).astype(jnp.int32)
    blk = jnp.arange(nblk) * MOE_TILE
    block_expert = jnp.minimum(jnp.sum(pad_end[None, :] <= blk[:, None], axis=1), N_EXPERTS - 1)
    last = jnp.max(jnp.where(jnp.arange(nblk) < n_active, block_expert, 0))
    block_expert = jnp.where(jnp.arange(nblk) < n_active, block_expert, last).astype(jnp.int32)
    tok = jnp.arange(A, dtype=jnp.int32) // EXPERT_TOP_K
    buf_tok = jnp.zeros((cap,), jnp.int32).at[dest.reshape(A)].set(tok)
    return weight, dest, buf_tok, block_expert, n_active.reshape(1)


def kernel(x, c, w_ada, b_ada, g_pre_mix, g_post_mix, g_pre_ffn, g_post_ffn, w_in, b_forget,
           cmp_pe_k, cmp_w1_k, cmp_w2_k, cmp_pe_v, cmp_w1_v, cmp_w2_v,
           w_o_nsa, w_o_fox, w_out, w_router_group, b_router_group, w_router_expert, b_router_expert,
           w_exp_gate, w_exp_up, w_exp_down):
    B, S, D = x.shape
    T = B * S
    depth = w_ada.shape[0]
    ov = _overlap_matrix()
    tri = jnp.asarray(np.tril(np.ones((IN_TILE, IN_TILE), np.float32)), dtype=BF16)
    stri = jnp.asarray(np.tril(np.ones((MERGE_TILE, MERGE_TILE), np.float32), -1), dtype=BF16)
    row_feat = _row_features(S)
    placement = _placement()
    cmp_ext = _cmp_key_ext()
    for l in range(depth):
        mod = (jax.nn.silu(c) @ w_ada[l] + b_ada[l]).reshape(B, 6, D)
        w_qa, w_kva, w_gl, w_fox, w_f, w_mg = jnp.split(w_in[l], IN_SPLITS, axis=-1)
        w_big = jnp.concatenate([w_qa, w_kva, w_fox, w_mg], axis=1).astype(BF16)
        w_small = _pad_cols(jnp.concatenate([w_gl, w_f], axis=1)).astype(BF16)
        bf_pad = jnp.pad(b_forget[l], (F_LANE, LANES - F_LANE - FOX_HEADS)).reshape(1, LANES)
        qa, ckv, ksl, nkv, fq, fk, fv, mg, sm = _inproj(
            x, mod, g_pre_mix[l].reshape(1, D), w_big, w_small, bf_pad, tri, row_feat, placement)

        half = CMP_LEN // 2
        pe = jnp.stack([cmp_pe_k[l], cmp_pe_v[l]]).reshape(2, 2, 1, half * HEAD_DIM)
        w1 = jnp.stack([cmp_w1_k[l], cmp_w1_v[l]]).reshape(2, 2, half * HEAD_DIM, HEAD_DIM).astype(BF16)
        w2 = jnp.pad(jnp.stack([cmp_w2_k[l], cmp_w2_v[l]]), ((0, 0), (0, 0), (0, LANES - HEAD_DIM))).astype(BF16)
        kvc = _compress(ckv.reshape(B, 4, S // CMP_STRIDE, CMP_STRIDE * HEAD_DIM), pe, w1, w2, cmp_ext)
        ocg, selb, flags = _cmp_attention(qa, kvc, sm, ov)
        nq = S // Q_TILE
        per_tile = K_TILE // SEL_LEN
        tile_any = jnp.max(flags.reshape(B, NSA_KV_GROUPS, nq, LANES // per_tile, per_tile), axis=-1)
        bits = tile_any.astype(jnp.uint32) << jnp.arange(LANES // per_tile, dtype=jnp.uint32)
        flag_words = lax.bitcast_convert_type(jnp.sum(bits, axis=-1, dtype=jnp.uint32), jnp.int32).reshape(-1)
        y_a = _selwin_attention(flag_words, qa, ksl, nkv, selb, ocg, sm)

        y_b = _fox_attention(fq, fk, fv)

        w_r = _pad_cols(jnp.concatenate([w_router_group[l], w_router_expert[l]], axis=1))
        w_rh = w_r.astype(BF16)
        w_rl = (w_r - w_rh.astype(F32)).astype(BF16)
        b_r = _pad_cols(jnp.concatenate([b_router_group[l], b_router_expert[l]]).reshape(1, -1))
        x1, h2, rt, cnt = _merge(y_a, y_b, mg, x, mod, g_post_mix[l].reshape(1, D), g_pre_ffn[l].reshape(1, D),
                                 w_o_nsa[l].astype(BF16), w_o_fox[l].astype(BF16), w_out[l].astype(BF16),
                                 w_rh, w_rl, b_r, stri)

        weight, dest, buf_tok, block_expert, n_active = _dispatch_plan(rt.reshape(T, LANES), cnt, T)
        xb = h2.reshape(T, D)[buf_tok]
        yb = _experts(block_expert, n_active, xb, w_exp_gate[l].astype(BF16), w_exp_up[l].astype(BF16),
                      w_exp_down[l].astype(BF16))
        y = weight[:, 0:1] * yb[dest[:, 0]] + weight[:, 1:2] * yb[dest[:, 1]]
        x = _final(x1, y.reshape(B, S, D), mod, g_post_ffn[l].reshape(1, D))
    return x
```

```python
import functools

import ml_dtypes
import numpy as np
import jax
import jax.numpy as jnp
from jax import lax
from jax.experimental import pallas as pl
from jax.experimental.pallas import tpu as pltpu

D_MODEL = 1024
HEAD_DIM = 64
NSA_HEADS = 8
NSA_KV_GROUPS = 2
NSA_HPG = NSA_HEADS // NSA_KV_GROUPS
FOX_HEADS = 8
CMP_LEN = 32
CMP_STRIDE = 16
SEL_LEN = 64
N_SEL = 16
WINDOW = 512
N_EXPERT_GROUPS = 4
EXPERTS_PER_GROUP = 8
N_EXPERTS = N_EXPERT_GROUPS * EXPERTS_PER_GROUP
EXPERT_TOP_K = 2
D_EXPERT = D_MODEL // 2
NORM_EPS = 1e-6
NEG = -1e30
FORCE = 1e9
LOG2E = 1.4426950408889634

NSA_W = NSA_HEADS * HEAD_DIM
NSA_KV_W = NSA_KV_GROUPS * HEAD_DIM
FOX_W = FOX_HEADS * HEAD_DIM
IN_SIZES = (NSA_W, 6 * NSA_KV_W, 3 * NSA_HEADS, 3 * FOX_W, FOX_HEADS, 2 * D_MODEL)
IN_SPLITS = tuple(int(v) for v in np.cumsum(IN_SIZES)[:-1])

LANES = 128
Q_TILE = 128
K_TILE = 256
N_CMP_PAD = 512
MOE_TILE = 256
IN_TILE = 512
MERGE_TILE = 256
FOX_HPS = 4
VMEM_LIMIT = 56 * 1024 * 1024

F_LANE = 3 * NSA_HEADS
U_LANE = 64
ONE_LANE = 88
A_LANE = 89
B_LANE = 90
EXT = HEAD_DIM
G_FQ, G_FK, G_NQ, G_NK, N_GROUPS = 0, 8, 16, 24, 25

F32 = jnp.float32
BF16 = jnp.bfloat16


def _dot(a, b):
    return jnp.dot(a, b, preferred_element_type=F32)


def _dot_nt(a, b):
    return lax.dot_general(a, b, (((1,), (1,)), ((), ())), preferred_element_type=F32)


def _rms(x, g):
    return x * lax.rsqrt(jnp.mean(x * x, axis=-1, keepdims=True) + NORM_EPS) * g


def _cparams(sem):
    return pltpu.CompilerParams(dimension_semantics=sem, vmem_limit_bytes=VMEM_LIMIT)


def _split3(x):
    hi = x.astype(BF16).astype(F32)
    r = x - hi
    mid = r.astype(BF16).astype(F32)
    lo = (r - mid).astype(BF16).astype(F32)
    return hi, mid, lo


def _np_split3(x):
    x = np.asarray(x, np.float32)
    hi = x.astype(ml_dtypes.bfloat16).astype(np.float32)
    r = x - hi
    mid = r.astype(ml_dtypes.bfloat16).astype(np.float32)
    lo = (r - mid).astype(ml_dtypes.bfloat16).astype(np.float32)
    return hi, mid, lo


def _alibi_c():
    slopes = np.exp2(-8.0 * np.arange(1, NSA_HEADS + 1, dtype=np.float32) / NSA_HEADS).astype(np.float32)
    return slopes * np.float32(LOG2E)


def _row_features(S):
    t = np.arange(S, dtype=np.float32)
    c = _alibi_c()
    rs = np.zeros((S, LANES), np.float32)
    for h in range(NSA_HEADS):
        for j, term in enumerate(_np_split3(c[h] * t)):
            rs[:, U_LANE + 8 * j + h] = -term
    rs[:, ONE_LANE] = 1.0
    rs[:, A_LANE] = np.floor(t / LANES)
    rs[:, B_LANE] = t % LANES
    return jnp.asarray(rs, dtype=BF16)


def _placement():
    c = _alibi_c()
    p = np.zeros((LANES, N_GROUPS * LANES), np.float32)
    for h in range(FOX_HEADS):
        q0 = (G_FQ + h) * LANES + EXT
        k0 = (G_FK + h) * LANES + EXT
        for j in range(3):
            p[ONE_LANE, q0 + j] = -1.0
            p[F_LANE + 8 * j + h, q0 + 3 + j] = 1.0
            p[F_LANE + 8 * j + h, k0 + j] = 1.0
            p[ONE_LANE, k0 + 3 + j] = 1.0
    for h in range(NSA_HEADS):
        q0 = (G_NQ + h) * LANES + EXT
        c128 = _np_split3(c[h] * np.float32(LANES))
        c1 = _np_split3(c[h])
        for j in range(3):
            p[U_LANE + 8 * j + h, q0 + j] = 1.0
            p[ONE_LANE, q0 + 3 + j] = c128[j]
            p[ONE_LANE, q0 + 6 + j] = c1[j]
    k0 = G_NK * LANES + EXT
    for j in range(3):
        p[ONE_LANE, k0 + j] = 1.0
        p[A_LANE, k0 + 3 + j] = 1.0
        p[B_LANE, k0 + 6 + j] = 1.0
    return jnp.asarray(p, dtype=BF16)


def _cmp_key_ext():
    pos = np.arange(N_CMP_PAD, dtype=np.float32) * CMP_STRIDE + (CMP_LEN - 1)
    e = np.zeros((2, N_CMP_PAD, LANES), np.float32)
    for j in range(3):
        e[0, :, EXT + j] = 1.0
        e[0, :, EXT + 3 + j] = np.floor(pos / LANES)
        e[0, :, EXT + 6 + j] = pos % LANES
    return jnp.asarray(e, dtype=BF16)


def _inproj_kernel(x_ref, mod_ref, g_ref, wb_ref, ws_ref, bf_ref, tri_ref, rs_ref, p_ref,
                   qa_ref, ckv_ref, ksl_ref, nkv_ref, fq_ref, fk_ref, fv_ref, mg_ref, sm_ref, carry_sc):
    i = pl.program_id(1)
    tm = x_ref.shape[1]
    x = x_ref[0]
    h = _rms(x, g_ref[...]) * (1.0 + mod_ref[0, 1:2, :]) + mod_ref[0, 0:1, :]
    hb = h.astype(BF16)
    lane = lax.broadcasted_iota(jnp.int32, (tm, LANES), 1)
    lower = lane < HEAD_DIM
    ones_col = (lane == EXT).astype(F32)

    z = _dot(hb, ws_ref[...]) + bf_ref[...]
    logsig = jnp.minimum(z, 0.0) - jnp.log1p(jnp.exp(-jnp.abs(z)))
    sm_ref[0] = jnp.where(lane < F_LANE, jax.nn.sigmoid(z), logsig)

    @pl.when(i == 0)
    def _():
        carry_sc[...] = jnp.zeros(carry_sc.shape, F32)

    is_f = (lane >= F_LANE) & (lane < F_LANE + FOX_HEADS)
    l_hi, l_mid, l_lo = _split3(jnp.where(is_f, logsig, 0.0))
    tri = tri_ref[...]
    cum = carry_sc[...] + _dot(tri, l_hi.astype(BF16)) + _dot(tri, l_mid.astype(BF16)) + _dot(tri, l_lo.astype(BF16))
    carry_sc[...] = cum[tm - 1:tm, :]
    f_hi, f_mid, f_lo = _split3(cum * LOG2E)
    feat = (f_hi + pltpu.roll(f_mid, 8, 1) + pltpu.roll(f_lo, 16, 1) + rs_ref[...].astype(F32)).astype(BF16)

    def ext(group):
        return _dot(feat, p_ref[:, group * LANES:(group + 1) * LANES])

    def piece(acc, idx, extra):
        pair = acc[:, (idx // 2) * LANES:(idx // 2 + 1) * LANES]
        if idx % 2:
            pair = pltpu.roll(pair, HEAD_DIM, 1)
        return jnp.where(lower, pair, extra).astype(BF16)

    qscale = (HEAD_DIM ** -0.5) * LOG2E
    acc = _dot(hb, wb_ref[:, 0:NSA_W]) * qscale
    for hd in range(NSA_HEADS):
        qa_ref[0, hd] = piece(acc, hd, ext(G_NQ + hd))
    off = NSA_W
    acc = _dot(hb, wb_ref[:, off:off + 6 * NSA_KV_W])
    for pc in range(4):
        ckv_ref[0, pc] = acc[:, pc * HEAD_DIM:(pc + 1) * HEAD_DIM].astype(BF16)
    ext_k = ext(G_NK)
    t = i * tm + lax.broadcasted_iota(jnp.int32, (tm, LANES), 0)
    block_onehot = (lane == t // SEL_LEN).astype(BF16)
    for g in range(NSA_KV_GROUPS):
        ksl_ref[0, g, :, 0:LANES] = piece(acc, 4 + g, ext_k)
        ksl_ref[0, g, :, LANES:2 * LANES] = block_onehot
        nkv_ref[0, g] = piece(acc, 6 + g, ones_col)
        nkv_ref[0, 2 + g] = piece(acc, 8 + g, ext_k)
        nkv_ref[0, 4 + g] = piece(acc, 10 + g, ones_col)
    off += 6 * NSA_KV_W
    acc = _dot(hb, wb_ref[:, off:off + FOX_W]) * qscale
    for hd in range(FOX_HEADS):
        fq_ref[0, hd] = piece(acc, hd, ext(G_FQ + hd))
    off += FOX_W
    acc = _dot(hb, wb_ref[:, off:off + FOX_W])
    for hd in range(FOX_HEADS):
        fk_ref[0, hd] = piece(acc, hd, ext(G_FK + hd))
    off += FOX_W
    acc = _dot(hb, wb_ref[:, off:off + FOX_W])
    for hd in range(FOX_HEADS):
        fv_ref[0, hd] = piece(acc, hd, ones_col)
    off += FOX_W
    for c in range(4):
        acc = _dot(hb, wb_ref[:, off + c * 512: off + (c + 1) * 512])
        mg_ref[0, :, c * 512:(c + 1) * 512] = jax.nn.sigmoid(acc).astype(BF16)


def _inproj(x, mod, g, wb, ws, bfp, tri, rs, pm):
    B, S, D = x.shape
    tm = IN_TILE
    nb = wb.shape[1]
    const2 = lambda b, i: (0, 0)
    heads = lambda n: pl.BlockSpec((1, n, tm, LANES), lambda b, i: (b, 0, i, 0))
    hshape = lambda n: jax.ShapeDtypeStruct((B, n, S, LANES), BF16)
    return pl.pallas_call(
        _inproj_kernel,
        grid=(B, S // tm),
        in_specs=[
            pl.BlockSpec((1, tm, D), lambda b, i: (b, i, 0)),
            pl.BlockSpec((1, 6, D), lambda b, i: (b, 0, 0)),
            pl.BlockSpec((1, D), const2),
            pl.BlockSpec((D, nb), const2),
            pl.BlockSpec((D, LANES), const2),
            pl.BlockSpec((1, LANES), const2),
            pl.BlockSpec((tm, tm), const2),
            pl.BlockSpec((tm, LANES), lambda b, i: (i, 0)),
            pl.BlockSpec((LANES, N_GROUPS * LANES), const2),
        ],
        out_specs=[
            heads(NSA_HEADS),
            pl.BlockSpec((1, 4, tm, HEAD_DIM), lambda b, i: (b, 0, i, 0)),
            pl.BlockSpec((1, NSA_KV_GROUPS, tm, 2 * LANES), lambda b, i: (b, 0, i, 0)),
            heads(6), heads(FOX_HEADS), heads(FOX_HEADS), heads(FOX_HEADS),
            pl.BlockSpec((1, tm, 2 * D), lambda b, i: (b, i, 0)),
            pl.BlockSpec((1, tm, LANES), lambda b, i: (b, i, 0)),
        ],
        out_shape=[
            hshape(NSA_HEADS),
            jax.ShapeDtypeStruct((B, 4, S, HEAD_DIM), BF16),
            jax.ShapeDtypeStruct((B, NSA_KV_GROUPS, S, 2 * LANES), BF16),
            hshape(6), hshape(FOX_HEADS), hshape(FOX_HEADS), hshape(FOX_HEADS),
            jax.ShapeDtypeStruct((B, S, 2 * D), BF16),
            jax.ShapeDtypeStruct((B, S, LANES), F32),
        ],
        scratch_shapes=[pltpu.VMEM((1, LANES), F32)],
        compiler_params=_cparams(("parallel", "arbitrary")),
        name="inproj",
    )(x, mod, g, wb, ws, bfp, tri, rs, pm)


def _compress_kernel(x_ref, pe_ref, w1_ref, w2_ref, e_ref, o_ref):
    x = x_ref[0, 0].astype(F32)
    x_lo = (x + pe_ref[0, 0]).astype(BF16)
    x_hi = (x + pe_ref[0, 1]).astype(BF16)
    y_lo = _dot(x_lo, w1_ref[0, 0])
    y_hi = _dot(x_hi, w1_ref[0, 1])
    n = y_hi.shape[0]
    hid = y_lo + pltpu.roll(y_hi, n - 1, 0)
    hid = jax.nn.gelu(hid)
    o_ref[0, 0] = (_dot(hid.astype(BF16), w2_ref[0]) + e_ref[0].astype(F32)).astype(BF16)


def _compress(kv_rows, pe, w1, w2, e):
    B = kv_rows.shape[0]
    R, C = kv_rows.shape[2], kv_rows.shape[3]
    return pl.pallas_call(
        _compress_kernel,
        grid=(B, 4),
        in_specs=[
            pl.BlockSpec((1, 1, R, C), lambda b, p: (b, p, 0, 0)),
            pl.BlockSpec((1, 2, 1, C), lambda b, p: (p // 2, 0, 0, 0)),
            pl.BlockSpec((1, 2, C, HEAD_DIM), lambda b, p: (p // 2, 0, 0, 0)),
            pl.BlockSpec((1, HEAD_DIM, LANES), lambda b, p: (p // 2, 0, 0)),
            pl.BlockSpec((1, R, LANES), lambda b, p: (p // 2, 0, 0)),
        ],
        out_specs=pl.BlockSpec((1, 1, R, LANES), lambda b, p: (b, p, 0, 0)),
        out_shape=jax.ShapeDtypeStruct((B, 4, R, LANES), BF16),
        compiler_params=_cparams(("parallel", "parallel")),
        name="compress",
    )(kv_rows, pe, w1, w2, e)


def _gate_rows(sm, g, branch):
    col = lax.broadcasted_iota(jnp.int32, sm.shape, 1)
    parts = []
    for hl in range(NSA_HPG):
        want = 3 * (NSA_HPG * g + hl) + branch
        parts.append(jnp.sum(jnp.where(col == want, sm, 0.0), axis=-1, keepdims=True))
    return jnp.concatenate(parts, axis=0)


def _head_tile(y):
    lane = lax.broadcasted_iota(jnp.int32, (Q_TILE, LANES), 1)
    hs = [y[i * Q_TILE:(i + 1) * Q_TILE] for i in range(NSA_HPG)]
    pairs = [jnp.where(lane < HEAD_DIM, hs[2 * i], pltpu.roll(hs[2 * i + 1], HEAD_DIM, 1)) for i in range(2)]
    return jnp.concatenate(pairs, axis=1)


def _cmp_kernel(q_ref, kc_ref, vc_ref, sm_ref, ovt_ref, oc_ref, selb_ref, flag_ref):
    g = pl.program_id(1)
    qb = pl.program_id(2)
    q0 = qb * Q_TILE
    q = q_ref[0].reshape(NSA_HPG * Q_TILE, LANES)
    s = _dot_nt(q, kc_ref[0, 0])
    r = lax.broadcasted_iota(jnp.int32, (NSA_HPG * Q_TILE, 1), 0) % Q_TILE
    n = lax.broadcasted_iota(jnp.int32, (1, N_CMP_PAD), 1)
    dc = (q0 + r) - (n * CMP_STRIDE + (CMP_LEN - 1))
    mask = (dc >= 0) & (n < N_CMP_PAD - 1)
    l = jnp.where(mask, s, NEG)
    m = jnp.max(l, axis=-1, keepdims=True)
    e = jnp.where(mask, jnp.exp2(l - m), 0.0)
    pc = e / jnp.maximum(jnp.sum(e, axis=-1, keepdims=True), 1e-30)
    oc = _dot(pc.astype(BF16), vc_ref[0, 0])
    oc_ref[0] = _head_tile(oc * _gate_rows(sm_ref[0], g, 0))
    ps = pc[0:Q_TILE]
    for i in range(1, NSA_HPG):
        ps = ps + pc[i * Q_TILE:(i + 1) * Q_TILE]
    ps_hi = ps.astype(BF16)
    ps_lo = (ps - ps_hi.astype(F32)).astype(BF16)
    imp = _dot_nt(ovt_ref[...], ps_hi) + _dot_nt(ovt_ref[...], ps_lo)
    j = lax.broadcasted_iota(jnp.int32, imp.shape, 0)
    jf = j.astype(F32)
    t = q0 + lax.broadcasted_iota(jnp.int32, (1, Q_TILE), 1)
    cur = t // SEL_LEN
    forced = (j == 0) | (j == cur) | (j == cur - 1)
    v = jnp.where(j > cur, -FORCE, jnp.where(forced, FORCE, imp))
    sel = jnp.zeros(imp.shape, jnp.bool_)
    for _ in range(N_SEL):
        mx = jnp.max(v, axis=0, keepdims=True)
        idx = jnp.min(jnp.where(v == mx, jf, float(LANES)), axis=0, keepdims=True)
        pick = jf == idx
        sel = sel | pick
        v = jnp.where(pick, -3e38, v)
    live_t = jnp.where(sel & (j <= cur), 1.0, 0.0).astype(BF16)
    eye = (lax.broadcasted_iota(jnp.int32, imp.shape, 0) == lax.broadcasted_iota(jnp.int32, imp.shape, 1))
    live = _dot_nt(eye.astype(BF16), live_t)
    selb_ref[0, 0] = jnp.where(live > 0.5, 0.0, NEG).astype(BF16)
    flag_ref[0, 0, 0] = jnp.max(live, axis=0, keepdims=True).astype(jnp.int32)


def _cmp_attention(qa, kvc, sm, ov):
    B, H, S, _ = qa.shape
    G = NSA_KV_GROUPS
    nq = S // Q_TILE
    return pl.pallas_call(
        _cmp_kernel,
        grid=(B, G, nq),
        in_specs=[
            pl.BlockSpec((1, NSA_HPG, Q_TILE, LANES), lambda b, g, i: (b, g, i, 0)),
            pl.BlockSpec((1, 1, N_CMP_PAD, LANES), lambda b, g, i: (b, g, 0, 0)),
            pl.BlockSpec((1, 1, N_CMP_PAD, LANES), lambda b, g, i: (b, 2 + g, 0, 0)),
            pl.BlockSpec((1, Q_TILE, LANES), lambda b, g, i: (b, i, 0)),
            pl.BlockSpec((LANES, N_CMP_PAD), lambda b, g, i: (0, 0)),
        ],
        out_specs=[
            pl.BlockSpec((1, Q_TILE, NSA_HPG * HEAD_DIM), lambda b, g, i: (b, i, g)),
            pl.BlockSpec((1, 1, Q_TILE, LANES), lambda b, g, i: (b, g, i, 0)),
            pl.BlockSpec((1, 1, 1, 1, LANES), lambda b, g, i: (b, g, i, 0, 0)),
        ],
        out_shape=[
            jax.ShapeDtypeStruct((B, S, NSA_W), F32),
            jax.ShapeDtypeStruct((B, G, S, LANES), BF16),
            jax.ShapeDtypeStruct((B, G, nq, 1, LANES), jnp.int32),
        ],
        compiler_params=_cparams(("parallel", "parallel", "parallel")),
        name="cmp_attention",
    )(qa, kvc, kvc, sm, ov)


def _online_update(s, v, m_ref, acc_ref):
    m_old = m_ref[...]
    m_new = jnp.maximum(m_old, jnp.max(s, axis=-1, keepdims=True))
    chunks = [s[:, c * LANES:(c + 1) * LANES] - m_new for c in range(s.shape[1] // LANES)]
    p = jnp.exp2(jnp.concatenate(chunks, axis=1))
    acc_ref[...] = jnp.exp2(m_old - m_new) * acc_ref[...] + _dot(p.astype(BF16), v)
    m_ref[...] = m_new


def _normalized(acc):
    return acc / jnp.maximum(acc[:, EXT:EXT + 1], 1e-30)


def _attend_once(s, v):
    m = jnp.broadcast_to(jnp.max(s, axis=-1, keepdims=True), (s.shape[0], LANES))
    chunks = [s[:, c * LANES:(c + 1) * LANES] - m for c in range(s.shape[1] // LANES)]
    p = jnp.exp2(jnp.concatenate(chunks, axis=1))
    return _normalized(_dot(p.astype(BF16), v))


def _selwin_kernel(flags_ref, q_ref, ks_ref, vs_ref, kw_ref, vw_ref, selb_ref, oc_ref, sm_ref,
                   o_ref, m_sc, acc_sc):
    b = pl.program_id(0)
    g = pl.program_id(1)
    qb = pl.program_id(2)
    nq = pl.num_programs(2)
    rows = NSA_HPG * Q_TILE
    q4 = q_ref[0].reshape(rows, LANES)
    q_aug = jnp.concatenate([q4, jnp.concatenate([selb_ref[0, 0]] * NSA_HPG, axis=0)], axis=1)
    r = lax.broadcasted_iota(jnp.int32, (rows, 1), 0) % Q_TILE
    c = lax.broadcasted_iota(jnp.int32, (1, K_TILE), 1)
    rel = r - c
    diag = qb // (K_TILE // Q_TILE)

    def reset():
        m_sc[...] = jnp.full(m_sc.shape, NEG, F32)
        acc_sc[...] = jnp.zeros(acc_sc.shape, F32)

    def sel_tile(kt, causal):
        start = pl.multiple_of(kt * K_TILE, K_TILE)
        s = _dot_nt(q_aug, ks_ref[0, 0, pl.ds(start, K_TILE), :])
        if causal:
            s = jnp.where(rel + (qb * Q_TILE - kt * K_TILE) >= 0, s, NEG)
        _online_update(s, vs_ref[0, 0, pl.ds(start, K_TILE), :], m_sc, acc_sc)

    reset()
    word = flags_ref[(b * NSA_KV_GROUPS + g) * nq + qb]

    def body(kt, carry):
        @pl.when(((word >> kt) & 1) == 1)
        def _():
            sel_tile(kt, False)
        return carry

    lax.fori_loop(0, diag, body, 0)
    sel_tile(diag, True)
    o_sel = _normalized(acc_sc[...])

    span = WINDOW + K_TILE
    wstart = pl.multiple_of(jnp.maximum(diag - WINDOW // K_TILE, 0) * K_TILE, K_TILE)
    dist = (qb * Q_TILE + r) - (wstart + lax.broadcasted_iota(jnp.int32, (1, span), 1))
    s = _dot_nt(q4, kw_ref[0, 0, pl.ds(wstart, span), :])
    s = jnp.where((dist >= 0) & (dist < WINDOW), s, NEG)
    o_win = _attend_once(s, vw_ref[0, 0, pl.ds(wstart, span), :])

    sm = sm_ref[0]
    y = _gate_rows(sm, g, 1) * o_sel + _gate_rows(sm, g, 2) * o_win
    o_ref[0] = (oc_ref[0] + _head_tile(y)).astype(BF16)


def _selwin_attention(flag_words, qa, ksl, nkv, selb, ocg, sm):
    B, H, S, _ = qa.shape
    G = NSA_KV_GROUPS
    nq = S // Q_TILE
    rows = NSA_HPG * Q_TILE
    kv_spec = lambda piece: pl.BlockSpec((1, 1, S, LANES), lambda b, g, i, f: (b, piece + g, 0, 0))
    out_tile = pl.BlockSpec((1, Q_TILE, NSA_HPG * HEAD_DIM), lambda b, g, i, f: (b, i, g))
    grid_spec = pltpu.PrefetchScalarGridSpec(
        num_scalar_prefetch=1,
        grid=(B, G, nq),
        in_specs=[
            pl.BlockSpec((1, NSA_HPG, Q_TILE, LANES), lambda b, g, i, f: (b, g, i, 0)),
            pl.BlockSpec((1, 1, S, 2 * LANES), lambda b, g, i, f: (b, g, 0, 0)),
            kv_spec(0), kv_spec(2), kv_spec(4),
            pl.BlockSpec((1, 1, Q_TILE, LANES), lambda b, g, i, f: (b, g, i, 0)),
            out_tile,
            pl.BlockSpec((1, Q_TILE, LANES), lambda b, g, i, f: (b, i, 0)),
        ],
        out_specs=out_tile,
        scratch_shapes=[
            pltpu.VMEM((rows, LANES), F32),
            pltpu.VMEM((rows, LANES), F32),
        ],
    )
    return pl.pallas_call(
        _selwin_kernel,
        grid_spec=grid_spec,
        out_shape=jax.ShapeDtypeStruct((B, S, NSA_W), BF16),
        compiler_params=_cparams(("parallel", "parallel", "arbitrary")),
        name="selwin_attention",
    )(flag_words, qa, ksl, nkv, nkv, nkv, selb, ocg, sm)


def _fox_kernel(q_ref, k_ref, v_ref, o_ref, m_sc, acc_sc, *, tq):
    qi = pl.program_id(2)
    m_sc[...] = jnp.full(m_sc.shape, NEG, F32)
    acc_sc[...] = jnp.zeros(acc_sc.shape, F32)

    def tile(kt, causal):
        start = pl.multiple_of(kt * tq, tq)
        for hh in range(FOX_HPS):
            s = _dot_nt(q_ref[0, hh], k_ref[0, hh, pl.ds(start, tq), :])
            if causal:
                r = lax.broadcasted_iota(jnp.int32, s.shape, 0)
                c = lax.broadcasted_iota(jnp.int32, s.shape, 1)
                s = jnp.where(r >= c, s, NEG)
            _online_update(s, v_ref[0, hh, pl.ds(start, tq), :], m_sc.at[hh], acc_sc.at[hh])

    def body(kt, carry):
        tile(kt, False)
        return carry

    lax.fori_loop(0, qi, body, 0)
    tile(qi, True)
    lane = lax.broadcasted_iota(jnp.int32, (tq, LANES), 1)
    o = [_normalized(acc_sc[hh]) for hh in range(FOX_HPS)]
    for pr in range(FOX_HPS // 2):
        o_ref[0, :, pr * LANES:(pr + 1) * LANES] = jnp.where(
            lane < HEAD_DIM, o[2 * pr], pltpu.roll(o[2 * pr + 1], HEAD_DIM, 1)).astype(BF16)


def _fox_attention(fq, fk, fv, tq=512):
    B, H, S, _ = fq.shape
    hps = FOX_HPS
    return pl.pallas_call(
        functools.partial(_fox_kernel, tq=tq),
        grid=(B, H // hps, S // tq),
        in_specs=[
            pl.BlockSpec((1, hps, tq, LANES), lambda b, h, i: (b, h, i, 0)),
            pl.BlockSpec((1, hps, S, LANES), lambda b, h, i: (b, h, 0, 0)),
            pl.BlockSpec((1, hps, S, LANES), lambda b, h, i: (b, h, 0, 0)),
        ],
        out_specs=pl.BlockSpec((1, tq, hps * HEAD_DIM), lambda b, h, i: (b, i, h)),
        out_shape=jax.ShapeDtypeStruct((B, S, FOX_W), BF16),
        scratch_shapes=[
            pltpu.VMEM((hps, tq, LANES), F32),
            pltpu.VMEM((hps, tq, LANES), F32),
        ],
        compiler_params=_cparams(("parallel", "parallel", "arbitrary")),
        name="fox_attention",
    )(fq, fk, fv)


def _merge_kernel(ya_ref, yb_ref, mg_ref, x_ref, mod_ref, gpost_ref, gpre_ref,
                  wa_ref, wb_ref, wo_ref, wrh_ref, wrl_ref, br_ref, stri_ref,
                  x1_ref, h2_ref, rt_ref, cnt_ref):
    D = D_MODEL

    @pl.when((pl.program_id(0) == 0) & (pl.program_id(1) == 0))
    def _():
        cnt_ref[...] = jnp.zeros(cnt_ref.shape, F32)

    a = _dot(ya_ref[0], wa_ref[...])
    bq = _dot(yb_ref[0], wb_ref[...])
    mg = mg_ref[0]
    u = mg[:, :D].astype(F32) * a + mg[:, D:].astype(F32) * bq
    mixed = _dot(u.astype(BF16), wo_ref[...])
    x1 = x_ref[0] + mod_ref[0, 2:3, :] * _rms(mixed, gpost_ref[...])
    x1_ref[0] = x1
    h2 = _rms(x1, gpre_ref[...]) * (1.0 + mod_ref[0, 4:5, :]) + mod_ref[0, 3:4, :]
    hi = h2.astype(BF16)
    lo = (h2 - hi.astype(F32)).astype(BF16)
    h2_ref[0] = h2
    lg = _dot(hi, wrh_ref[...]) + _dot(lo, wrh_ref[...]) + _dot(hi, wrl_ref[...]) + br_ref[...]

    lane = lax.broadcasted_iota(jnp.int32, lg.shape, 1)
    lanef = lane.astype(F32)
    no_lane = float(LANES)
    is_g = lane < N_EXPERT_GROUPS
    gl = jnp.where(is_g, lg, NEG)
    gmax = jnp.max(gl, axis=-1, keepdims=True)
    pg_top = 1.0 / jnp.sum(jnp.where(is_g, jnp.exp(gl - gmax), 0.0), axis=-1, keepdims=True)
    g_idx = jnp.min(jnp.where(is_g & (gl == gmax), lanef, no_lane), axis=-1, keepdims=True)
    in_grp = ((lane >= N_EXPERT_GROUPS) & (lane < N_EXPERT_GROUPS + N_EXPERTS)
              & (((lane - N_EXPERT_GROUPS) // EXPERTS_PER_GROUP).astype(F32) == g_idx))
    le = jnp.where(in_grp, lg, NEG)
    m1 = jnp.max(le, axis=-1, keepdims=True)
    i1 = jnp.min(jnp.where(in_grp & (le == m1), lanef, no_lane), axis=-1, keepdims=True)
    rest = in_grp & (lanef != i1)
    le2 = jnp.where(rest, lg, NEG)
    m2 = jnp.max(le2, axis=-1, keepdims=True)
    i2 = jnp.min(jnp.where(rest & (le2 == m2), lanef, no_lane), axis=-1, keepdims=True)
    e21 = jnp.exp(m2 - m1)
    w1 = pg_top / (1.0 + e21)
    w2 = w1 * e21
    pick1 = lanef == i1
    pick2 = lanef == i2
    onehot = jnp.where(pick1 | pick2, 1.0, 0.0)
    before = cnt_ref[...] + _dot(stri_ref[...], onehot.astype(BF16))
    rank1 = jnp.sum(jnp.where(pick1, before, 0.0), axis=-1, keepdims=True)
    rank2 = jnp.sum(jnp.where(pick2, before, 0.0), axis=-1, keepdims=True)
    cnt_ref[...] = cnt_ref[...] + jnp.sum(onehot, axis=0, keepdims=True)
    fields = [i1 - N_EXPERT_GROUPS, i2 - N_EXPERT_GROUPS, rank1, rank2, w1, w2]
    rt = jnp.zeros(lg.shape, F32)
    for k, f in enumerate(fields):
        rt = jnp.where(lane == k, f, rt)
    rt_ref[0] = rt


def _merge(ya, yb, mg, x, mod, gpost, gpre, wa, wb, wo, wrh, wrl, br, stri):
    B, S, D = x.shape
    tm = MERGE_TILE
    c2 = lambda b, i: (0, 0)
    row = lambda w: pl.BlockSpec((1, tm, w), lambda b, i: (b, i, 0))
    return pl.pallas_call(
        _merge_kernel,
        grid=(B, S // tm),
        in_specs=[
            row(NSA_W), row(FOX_W), row(2 * D), row(D),
            pl.BlockSpec((1, 6, D), lambda b, i: (b, 0, 0)),
            pl.BlockSpec((1, D), c2), pl.BlockSpec((1, D), c2),
            pl.BlockSpec((NSA_W, D), c2), pl.BlockSpec((FOX_W, D), c2), pl.BlockSpec((D, D), c2),
            pl.BlockSpec((D, LANES), c2), pl.BlockSpec((D, LANES), c2), pl.BlockSpec((1, LANES), c2),
            pl.BlockSpec((tm, tm), c2),
        ],
        out_specs=[row(D), row(D), row(LANES), pl.BlockSpec((1, LANES), c2)],
        out_shape=[
            jax.ShapeDtypeStruct((B, S, D), F32),
            jax.ShapeDtypeStruct((B, S, D), F32),
            jax.ShapeDtypeStruct((B, S, LANES), F32),
            jax.ShapeDtypeStruct((1, LANES), F32),
        ],
        compiler_params=_cparams(("arbitrary", "arbitrary")),
        name="merge",
    )(ya, yb, mg, x, mod, gpost, gpre, wa, wb, wo, wrh, wrl, br, stri)


def _expert_kernel(be_ref, na_ref, tok_ref, h_hbm, wg_ref, wu_ref, wd_ref, o_ref, xbuf, sem):
    i = pl.program_id(0)
    n_active = na_ref[0]
    slot = i % 2

    def row_copy(blk, r, sl):
        tok = tok_ref[blk * MOE_TILE + r]
        return pltpu.make_async_copy(h_hbm.at[pl.ds(tok, 1)], xbuf.at[sl, pl.ds(r, 1)], sem.at[sl])

    def start_gather(blk, sl):
        def body(r, carry):
            row_copy(blk, r, sl).start()
            return carry
        lax.fori_loop(0, MOE_TILE, body, 0, unroll=8)

    @pl.when(i == 0)
    def _():
        start_gather(0, 0)

    @pl.when(i + 1 < n_active)
    def _():
        start_gather(i + 1, 1 - slot)

    @pl.when(i < n_active)
    def _():
        def body(r, carry):
            row_copy(i, r, slot).wait()
            return carry
        lax.fori_loop(0, MOE_TILE, body, 0, unroll=8)
        x = xbuf[slot].astype(BF16)
        gate = _dot(x, wg_ref[0])
        up = _dot(x, wu_ref[0])
        mid = (gate * jax.nn.sigmoid(gate) * up).astype(BF16)
        o_ref[...] = _dot(mid, wd_ref[0])

    @pl.when(i >= n_active)
    def _():
        o_ref[...] = jnp.zeros(o_ref.shape, o_ref.dtype)


def _experts(block_expert, n_active, buf_tok, h2, wg, wu, wd):
    cap = buf_tok.shape[0]
    D = h2.shape[1]
    nblk = cap // MOE_TILE
    grid_spec = pltpu.PrefetchScalarGridSpec(
        num_scalar_prefetch=3,
        grid=(nblk,),
        in_specs=[
            pl.BlockSpec(memory_space=pl.ANY),
            pl.BlockSpec((1, D, D_EXPERT), lambda i, be, na, tok: (be[i], 0, 0)),
            pl.BlockSpec((1, D, D_EXPERT), lambda i, be, na, tok: (be[i], 0, 0)),
            pl.BlockSpec((1, D_EXPERT, D), lambda i, be, na, tok: (be[i], 0, 0)),
        ],
        out_specs=pl.BlockSpec((MOE_TILE, D), lambda i, be, na, tok: (i, 0)),
        scratch_shapes=[
            pltpu.VMEM((2, MOE_TILE, D), F32),
            pltpu.SemaphoreType.DMA((2,)),
        ],
    )
    return pl.pallas_call(
        _expert_kernel,
        grid_spec=grid_spec,
        out_shape=jax.ShapeDtypeStruct((cap, D), F32),
        compiler_params=_cparams(("arbitrary",)),
        name="experts",
    )(block_expert, n_active, buf_tok, h2, wg, wu, wd)


def _final_kernel(x1_ref, y_ref, mod_ref, g_ref, o_ref):
    o_ref[0] = x1_ref[0] + mod_ref[0, 5:6, :] * _rms(y_ref[0], g_ref[...])


def _final(x1, y, mod, g, tm=512):
    B, S, D = x1.shape
    row = pl.BlockSpec((1, tm, D), lambda b, i: (b, i, 0))
    return pl.pallas_call(
        _final_kernel,
        grid=(B, S // tm),
        in_specs=[row, row, pl.BlockSpec((1, 6, D), lambda b, i: (b, 0, 0)),
                  pl.BlockSpec((1, D), lambda b, i: (0, 0))],
        out_specs=row,
        out_shape=jax.ShapeDtypeStruct((B, S, D), F32),
        compiler_params=_cparams(("parallel", "parallel")),
        name="final",
    )(x1, y, mod, g)


def _overlap_matrix():
    n = np.arange(N_CMP_PAD)[:, None]
    j = np.arange(LANES)[None, :]
    start = n * CMP_STRIDE
    ov = (start < j * SEL_LEN + SEL_LEN) & (start + CMP_LEN - 1 >= j * SEL_LEN) & (n < N_CMP_PAD - 1)
    return jnp.asarray(ov.T.astype(np.float32), dtype=BF16)


def _pad_cols(w, width=LANES):
    return jnp.pad(w, ((0, 0), (0, width - w.shape[1])))


def _dispatch_plan(rt, cnt, T):
    expert = rt[:, 0:2].astype(jnp.int32)
    rank = rt[:, 2:4].astype(jnp.int32)
    weight = rt[:, 4:6]
    counts = cnt[0, N_EXPERT_GROUPS:N_EXPERT_GROUPS + N_EXPERTS].astype(jnp.int32)
    padded = (counts + MOE_TILE - 1) // MOE_TILE * MOE_TILE
    pad_end = jnp.cumsum(padded)
    pad_start = pad_end - padded
    onehot = expert[:, :, None] == jnp.arange(N_EXPERTS)[None, None, :]
    dest = jnp.sum(jnp.where(onehot, pad_start[None, None, :], 0), axis=-1) + rank
    A = T * EXPERT_TOP_K
    cap = -(-(A + N_EXPERTS * (MOE_TILE - 1)) // MOE_TILE) * MOE_TILE
    nblk = cap // MOE_TILE
    n_active = (pad_end[-1] // MOE_TILE).astype(jnp.int32)
    blk = jnp.arange(nblk) * MOE_TILE
    block_expert = jnp.minimum(jnp.sum(pad_end[None, :] <= blk[:, None], axis=1), N_EXPERTS - 1)
    last = jnp.max(jnp.where(jnp.arange(nblk) < n_active, block_expert, 0))
    block_expert = jnp.where(jnp.arange(nblk) < n_active, block_expert, last).astype(jnp.int32)
    tok = jnp.arange(A, dtype=jnp.int32) // EXPERT_TOP_K
    buf_tok = jnp.zeros((cap,), jnp.int32).at[dest.reshape(A)].set(tok)
    return weight, dest, buf_tok, block_expert, n_active.reshape(1)


def kernel(x, c, w_ada, b_ada, g_pre_mix, g_post_mix, g_pre_ffn, g_post_ffn, w_in, b_forget,
           cmp_pe_k, cmp_w1_k, cmp_w2_k, cmp_pe_v, cmp_w1_v, cmp_w2_v,
           w_o_nsa, w_o_fox, w_out, w_router_group, b_router_group, w_router_expert, b_router_expert,
           w_exp_gate, w_exp_up, w_exp_down):
    B, S, D = x.shape
    T = B * S
    depth = w_ada.shape[0]
    ov = _overlap_matrix()
    tri = jnp.asarray(np.tril(np.ones((IN_TILE, IN_TILE), np.float32)), dtype=BF16)
    stri = jnp.asarray(np.tril(np.ones((MERGE_TILE, MERGE_TILE), np.float32), -1), dtype=BF16)
    row_feat = _row_features(S)
    placement = _placement()
    cmp_ext = _cmp_key_ext()
    for l in range(depth):
        mod = (jax.nn.silu(c) @ w_ada[l] + b_ada[l]).reshape(B, 6, D)
        w_qa, w_kva, w_gl, w_fox, w_f, w_mg = jnp.split(w_in[l], IN_SPLITS, axis=-1)
        w_big = jnp.concatenate([w_qa, w_kva, w_fox, w_mg], axis=1).astype(BF16)
        w_small = _pad_cols(jnp.concatenate([w_gl, w_f], axis=1)).astype(BF16)
        bf_pad = jnp.pad(b_forget[l], (F_LANE, LANES - F_LANE - FOX_HEADS)).reshape(1, LANES)
        qa, ckv, ksl, nkv, fq, fk, fv, mg, sm = _inproj(
            x, mod, g_pre_mix[l].reshape(1, D), w_big, w_small, bf_pad, tri, row_feat, placement)

        half = CMP_LEN // 2
        pe = jnp.stack([cmp_pe_k[l], cmp_pe_v[l]]).reshape(2, 2, 1, half * HEAD_DIM)
        w1 = jnp.stack([cmp_w1_k[l], cmp_w1_v[l]]).reshape(2, 2, half * HEAD_DIM, HEAD_DIM).astype(BF16)
        w2 = jnp.pad(jnp.stack([cmp_w2_k[l], cmp_w2_v[l]]), ((0, 0), (0, 0), (0, LANES - HEAD_DIM))).astype(BF16)
        kvc = _compress(ckv.reshape(B, 4, S // CMP_STRIDE, CMP_STRIDE * HEAD_DIM), pe, w1, w2, cmp_ext)
        ocg, selb, flags = _cmp_attention(qa, kvc, sm, ov)
        nq = S // Q_TILE
        per_tile = K_TILE // SEL_LEN
        tile_any = jnp.max(flags.reshape(B, NSA_KV_GROUPS, nq, LANES // per_tile, per_tile), axis=-1)
        bits = tile_any.astype(jnp.uint32) << jnp.arange(LANES // per_tile, dtype=jnp.uint32)
        flag_words = lax.bitcast_convert_type(jnp.sum(bits, axis=-1, dtype=jnp.uint32), jnp.int32).reshape(-1)
        y_a = _selwin_attention(flag_words, qa, ksl, nkv, selb, ocg, sm)

        y_b = _fox_attention(fq, fk, fv)

        w_r = _pad_cols(jnp.concatenate([w_router_group[l], w_router_expert[l]], axis=1))
        w_rh = w_r.astype(BF16)
        w_rl = (w_r - w_rh.astype(F32)).astype(BF16)
        b_r = _pad_cols(jnp.concatenate([b_router_group[l], b_router_expert[l]]).reshape(1, -1))
        x1, h2, rt, cnt = _merge(y_a, y_b, mg, x, mod, g_post_mix[l].reshape(1, D), g_pre_ffn[l].reshape(1, D),
                                 w_o_nsa[l].astype(BF16), w_o_fox[l].astype(BF16), w_out[l].astype(BF16),
                                 w_rh, w_rl, b_r, stri)

        weight, dest, buf_tok, block_expert, n_active = _dispatch_plan(rt.reshape(T, LANES), cnt, T)
        yb = _experts(block_expert, n_active, buf_tok, h2.reshape(T, D), w_exp_gate[l].astype(BF16),
                      w_exp_up[l].astype(BF16), w_exp_down[l].astype(BF16))
        y = weight[:, 0:1] * yb[dest[:, 0]] + weight[:, 1:2] * yb[dest[:, 1]]
        x = _final(x1, y.reshape(B, S, D), mod, g_post_ffn[l].reshape(1, D))
    return x
```

```python
import functools

import ml_dtypes
import numpy as np
import jax
import jax.numpy as jnp
from jax import lax
from jax.experimental import pallas as pl
from jax.experimental.pallas import tpu as pltpu

D_MODEL = 1024
HEAD_DIM = 64
NSA_HEADS = 8
NSA_KV_GROUPS = 2
NSA_HPG = NSA_HEADS // NSA_KV_GROUPS
FOX_HEADS = 8
CMP_LEN = 32
CMP_STRIDE = 16
SEL_LEN = 64
N_SEL = 16
WINDOW = 512
N_EXPERT_GROUPS = 4
EXPERTS_PER_GROUP = 8
N_EXPERTS = N_EXPERT_GROUPS * EXPERTS_PER_GROUP
EXPERT_TOP_K = 2
D_EXPERT = D_MODEL // 2
NORM_EPS = 1e-6
NEG = -1e30
FORCE = 1e9
LOG2E = 1.4426950408889634

NSA_W = NSA_HEADS * HEAD_DIM
NSA_KV_W = NSA_KV_GROUPS * HEAD_DIM
FOX_W = FOX_HEADS * HEAD_DIM
IN_SIZES = (NSA_W, 6 * NSA_KV_W, 3 * NSA_HEADS, 3 * FOX_W, FOX_HEADS, 2 * D_MODEL)
IN_SPLITS = tuple(int(v) for v in np.cumsum(IN_SIZES)[:-1])

LANES = 128
Q_TILE = 128
K_TILE = 256
N_CMP_PAD = 512
MOE_TILE = 256
IN_TILE = 512
MERGE_TILE = 256
FOX_HPS = 4
VMEM_LIMIT = 56 * 1024 * 1024

F_LANE = 3 * NSA_HEADS
U_LANE = 64
ONE_LANE = 88
A_LANE = 89
B_LANE = 90
EXT = HEAD_DIM
G_FQ, G_FK, G_NQ, G_NK, N_GROUPS = 0, 8, 16, 24, 25

F32 = jnp.float32
BF16 = jnp.bfloat16


def _dot(a, b):
    return jnp.dot(a, b, preferred_element_type=F32)


def _dot_nt(a, b):
    return lax.dot_general(a, b, (((1,), (1,)), ((), ())), preferred_element_type=F32)


def _rms(x, g):
    return x * lax.rsqrt(jnp.mean(x * x, axis=-1, keepdims=True) + NORM_EPS) * g


def _cparams(sem):
    return pltpu.CompilerParams(dimension_semantics=sem, vmem_limit_bytes=VMEM_LIMIT)


def _split3(x):
    hi = x.astype(BF16).astype(F32)
    r = x - hi
    mid = r.astype(BF16).astype(F32)
    lo = (r - mid).astype(BF16).astype(F32)
    return hi, mid, lo


def _np_split3(x):
    x = np.asarray(x, np.float32)
    hi = x.astype(ml_dtypes.bfloat16).astype(np.float32)
    r = x - hi
    mid = r.astype(ml_dtypes.bfloat16).astype(np.float32)
    lo = (r - mid).astype(ml_dtypes.bfloat16).astype(np.float32)
    return hi, mid, lo


def _alibi_c():
    slopes = np.exp2(-8.0 * np.arange(1, NSA_HEADS + 1, dtype=np.float32) / NSA_HEADS).astype(np.float32)
    return slopes * np.float32(LOG2E)


def _row_features(S):
    t = np.arange(S, dtype=np.float32)
    c = _alibi_c()
    rs = np.zeros((S, LANES), np.float32)
    for h in range(NSA_HEADS):
        for j, term in enumerate(_np_split3(c[h] * t)):
            rs[:, U_LANE + 8 * j + h] = -term
    rs[:, ONE_LANE] = 1.0
    rs[:, A_LANE] = np.floor(t / LANES)
    rs[:, B_LANE] = t % LANES
    return jnp.asarray(rs, dtype=BF16)


def _placement():
    c = _alibi_c()
    p = np.zeros((LANES, N_GROUPS * LANES), np.float32)
    for h in range(FOX_HEADS):
        q0 = (G_FQ + h) * LANES + EXT
        k0 = (G_FK + h) * LANES + EXT
        for j in range(3):
            p[ONE_LANE, q0 + j] = -1.0
            p[F_LANE + 8 * j + h, q0 + 3 + j] = 1.0
            p[F_LANE + 8 * j + h, k0 + j] = 1.0
            p[ONE_LANE, k0 + 3 + j] = 1.0
    for h in range(NSA_HEADS):
        q0 = (G_NQ + h) * LANES + EXT
        c128 = _np_split3(c[h] * np.float32(LANES))
        c1 = _np_split3(c[h])
        for j in range(3):
            p[U_LANE + 8 * j + h, q0 + j] = 1.0
            p[ONE_LANE, q0 + 3 + j] = c128[j]
            p[ONE_LANE, q0 + 6 + j] = c1[j]
    k0 = G_NK * LANES + EXT
    for j in range(3):
        p[ONE_LANE, k0 + j] = 1.0
        p[A_LANE, k0 + 3 + j] = 1.0
        p[B_LANE, k0 + 6 + j] = 1.0
    return jnp.asarray(p, dtype=BF16)


def _cmp_key_ext():
    pos = np.arange(N_CMP_PAD, dtype=np.float32) * CMP_STRIDE + (CMP_LEN - 1)
    e = np.zeros((2, N_CMP_PAD, LANES), np.float32)
    for j in range(3):
        e[0, :, EXT + j] = 1.0
        e[0, :, EXT + 3 + j] = np.floor(pos / LANES)
        e[0, :, EXT + 6 + j] = pos % LANES
    return jnp.asarray(e, dtype=BF16)


def _inproj_kernel(x_ref, mod_ref, g_ref, wb_ref, ws_ref, bf_ref, tri_ref, rs_ref, p_ref,
                   qa_ref, ckv_ref, ksl_ref, nkv_ref, fq_ref, fk_ref, fv_ref, mg_ref, sm_ref, carry_sc):
    i = pl.program_id(1)
    tm = x_ref.shape[1]
    x = x_ref[0]
    h = _rms(x, g_ref[...]) * (1.0 + mod_ref[0, 1:2, :]) + mod_ref[0, 0:1, :]
    hb = h.astype(BF16)
    lane = lax.broadcasted_iota(jnp.int32, (tm, LANES), 1)
    lower = lane < HEAD_DIM
    ones_col = (lane == EXT).astype(F32)

    z = _dot(hb, ws_ref[...]) + bf_ref[...]
    logsig = jnp.minimum(z, 0.0) - jnp.log1p(jnp.exp(-jnp.abs(z)))
    sm_ref[0] = jnp.where(lane < F_LANE, jax.nn.sigmoid(z), logsig)

    @pl.when(i == 0)
    def _():
        carry_sc[...] = jnp.zeros(carry_sc.shape, F32)

    is_f = (lane >= F_LANE) & (lane < F_LANE + FOX_HEADS)
    l_hi, l_mid, l_lo = _split3(jnp.where(is_f, logsig, 0.0))
    tri = tri_ref[...]
    cum = carry_sc[...] + _dot(tri, l_hi.astype(BF16)) + _dot(tri, l_mid.astype(BF16)) + _dot(tri, l_lo.astype(BF16))
    carry_sc[...] = cum[tm - 1:tm, :]
    f_hi, f_mid, f_lo = _split3(cum * LOG2E)
    feat = (f_hi + pltpu.roll(f_mid, 8, 1) + pltpu.roll(f_lo, 16, 1) + rs_ref[...].astype(F32)).astype(BF16)

    def ext(group):
        return _dot(feat, p_ref[:, group * LANES:(group + 1) * LANES])

    def piece(acc, idx, extra):
        pair = acc[:, (idx // 2) * LANES:(idx // 2 + 1) * LANES]
        if idx % 2:
            pair = pltpu.roll(pair, HEAD_DIM, 1)
        return jnp.where(lower, pair, extra).astype(BF16)

    qscale = (HEAD_DIM ** -0.5) * LOG2E
    acc = _dot(hb, wb_ref[:, 0:NSA_W]) * qscale
    for hd in range(NSA_HEADS):
        qa_ref[0, hd] = piece(acc, hd, ext(G_NQ + hd))
    off = NSA_W
    acc = _dot(hb, wb_ref[:, off:off + 6 * NSA_KV_W])
    for pc in range(4):
        ckv_ref[0, pc] = acc[:, pc * HEAD_DIM:(pc + 1) * HEAD_DIM].astype(BF16)
    ext_k = ext(G_NK)
    t = i * tm + lax.broadcasted_iota(jnp.int32, (tm, LANES), 0)
    block_onehot = (lane == t // SEL_LEN).astype(BF16)
    for g in range(NSA_KV_GROUPS):
        ksl_ref[0, g, :, 0:LANES] = piece(acc, 4 + g, ext_k)
        ksl_ref[0, g, :, LANES:2 * LANES] = block_onehot
        nkv_ref[0, g] = piece(acc, 6 + g, ones_col)
        nkv_ref[0, 2 + g] = piece(acc, 8 + g, ext_k)
        nkv_ref[0, 4 + g] = piece(acc, 10 + g, ones_col)
    off += 6 * NSA_KV_W
    acc = _dot(hb, wb_ref[:, off:off + FOX_W]) * qscale
    for hd in range(FOX_HEADS):
        fq_ref[0, hd] = piece(acc, hd, ext(G_FQ + hd))
    off += FOX_W
    acc = _dot(hb, wb_ref[:, off:off + FOX_W])
    for hd in range(FOX_HEADS):
        fk_ref[0, hd] = piece(acc, hd, ext(G_FK + hd))
    off += FOX_W
    acc = _dot(hb, wb_ref[:, off:off + FOX_W])
    for hd in range(FOX_HEADS):
        fv_ref[0, hd] = piece(acc, hd, ones_col)
    off += FOX_W
    for c in range(4):
        acc = _dot(hb, wb_ref[:, off + c * 512: off + (c + 1) * 512])
        mg_ref[0, :, c * 512:(c + 1) * 512] = jax.nn.sigmoid(acc).astype(BF16)


def _inproj(x, mod, g, wb, ws, bfp, tri, rs, pm):
    B, S, D = x.shape
    tm = IN_TILE
    nb = wb.shape[1]
    const2 = lambda b, i: (0, 0)
    heads = lambda n: pl.BlockSpec((1, n, tm, LANES), lambda b, i: (b, 0, i, 0))
    hshape = lambda n: jax.ShapeDtypeStruct((B, n, S, LANES), BF16)
    return pl.pallas_call(
        _inproj_kernel,
        grid=(B, S // tm),
        in_specs=[
            pl.BlockSpec((1, tm, D), lambda b, i: (b, i, 0)),
            pl.BlockSpec((1, 6, D), lambda b, i: (b, 0, 0)),
            pl.BlockSpec((1, D), const2),
            pl.BlockSpec((D, nb), const2),
            pl.BlockSpec((D, LANES), const2),
            pl.BlockSpec((1, LANES), const2),
            pl.BlockSpec((tm, tm), const2),
            pl.BlockSpec((tm, LANES), lambda b, i: (i, 0)),
            pl.BlockSpec((LANES, N_GROUPS * LANES), const2),
        ],
        out_specs=[
            heads(NSA_HEADS),
            pl.BlockSpec((1, 4, tm, HEAD_DIM), lambda b, i: (b, 0, i, 0)),
            pl.BlockSpec((1, NSA_KV_GROUPS, tm, 2 * LANES), lambda b, i: (b, 0, i, 0)),
            heads(6), heads(FOX_HEADS), heads(FOX_HEADS), heads(FOX_HEADS),
            pl.BlockSpec((1, tm, 2 * D), lambda b, i: (b, i, 0)),
            pl.BlockSpec((1, tm, LANES), lambda b, i: (b, i, 0)),
        ],
        out_shape=[
            hshape(NSA_HEADS),
            jax.ShapeDtypeStruct((B, 4, S, HEAD_DIM), BF16),
            jax.ShapeDtypeStruct((B, NSA_KV_GROUPS, S, 2 * LANES), BF16),
            hshape(6), hshape(FOX_HEADS), hshape(FOX_HEADS), hshape(FOX_HEADS),
            jax.ShapeDtypeStruct((B, S, 2 * D), BF16),
            jax.ShapeDtypeStruct((B, S, LANES), F32),
        ],
        scratch_shapes=[pltpu.VMEM((1, LANES), F32)],
        compiler_params=_cparams(("parallel", "arbitrary")),
        name="inproj",
    )(x, mod, g, wb, ws, bfp, tri, rs, pm)


def _compress_kernel(x_ref, pe_ref, w1_ref, w2_ref, e_ref, o_ref):
    x = x_ref[0, 0].astype(F32)
    x_lo = (x + pe_ref[0, 0]).astype(BF16)
    x_hi = (x + pe_ref[0, 1]).astype(BF16)
    y_lo = _dot(x_lo, w1_ref[0, 0])
    y_hi = _dot(x_hi, w1_ref[0, 1])
    n = y_hi.shape[0]
    hid = y_lo + pltpu.roll(y_hi, n - 1, 0)
    hid = jax.nn.gelu(hid)
    o_ref[0, 0] = (_dot(hid.astype(BF16), w2_ref[0]) + e_ref[0].astype(F32)).astype(BF16)


def _compress(kv_rows, pe, w1, w2, e):
    B = kv_rows.shape[0]
    R, C = kv_rows.shape[2], kv_rows.shape[3]
    return pl.pallas_call(
        _compress_kernel,
        grid=(B, 4),
        in_specs=[
            pl.BlockSpec((1, 1, R, C), lambda b, p: (b, p, 0, 0)),
            pl.BlockSpec((1, 2, 1, C), lambda b, p: (p // 2, 0, 0, 0)),
            pl.BlockSpec((1, 2, C, HEAD_DIM), lambda b, p: (p // 2, 0, 0, 0)),
            pl.BlockSpec((1, HEAD_DIM, LANES), lambda b, p: (p // 2, 0, 0)),
            pl.BlockSpec((1, R, LANES), lambda b, p: (p // 2, 0, 0)),
        ],
        out_specs=pl.BlockSpec((1, 1, R, LANES), lambda b, p: (b, p, 0, 0)),
        out_shape=jax.ShapeDtypeStruct((B, 4, R, LANES), BF16),
        compiler_params=_cparams(("parallel", "parallel")),
        name="compress",
    )(kv_rows, pe, w1, w2, e)


def _gate_rows(sm, g, branch):
    col = lax.broadcasted_iota(jnp.int32, sm.shape, 1)
    parts = []
    for hl in range(NSA_HPG):
        want = 3 * (NSA_HPG * g + hl) + branch
        parts.append(jnp.sum(jnp.where(col == want, sm, 0.0), axis=-1, keepdims=True))
    return jnp.concatenate(parts, axis=0)


def _head_tile(y):
    lane = lax.broadcasted_iota(jnp.int32, (Q_TILE, LANES), 1)
    hs = [y[i * Q_TILE:(i + 1) * Q_TILE] for i in range(NSA_HPG)]
    pairs = [jnp.where(lane < HEAD_DIM, hs[2 * i], pltpu.roll(hs[2 * i + 1], HEAD_DIM, 1)) for i in range(2)]
    return jnp.concatenate(pairs, axis=1)


def _cmp_kernel(q_ref, kc_ref, vc_ref, sm_ref, ovt_ref, oc_ref, selb_ref, flag_ref):
    g = pl.program_id(1)
    qb = pl.program_id(2)
    q0 = qb * Q_TILE
    q = q_ref[0].reshape(NSA_HPG * Q_TILE, LANES)
    s = _dot_nt(q, kc_ref[0, 0])
    r = lax.broadcasted_iota(jnp.int32, (NSA_HPG * Q_TILE, 1), 0) % Q_TILE
    n = lax.broadcasted_iota(jnp.int32, (1, N_CMP_PAD), 1)
    dc = (q0 + r) - (n * CMP_STRIDE + (CMP_LEN - 1))
    mask = (dc >= 0) & (n < N_CMP_PAD - 1)
    l = jnp.where(mask, s, NEG)
    m = jnp.max(l, axis=-1, keepdims=True)
    e = jnp.where(mask, jnp.exp2(l - m), 0.0)
    pc = e / jnp.maximum(jnp.sum(e, axis=-1, keepdims=True), 1e-30)
    oc = _dot(pc.astype(BF16), vc_ref[0, 0])
    oc_ref[0] = _head_tile(oc * _gate_rows(sm_ref[0], g, 0))
    ps = pc[0:Q_TILE]
    for i in range(1, NSA_HPG):
        ps = ps + pc[i * Q_TILE:(i + 1) * Q_TILE]
    ps_hi = ps.astype(BF16)
    ps_lo = (ps - ps_hi.astype(F32)).astype(BF16)
    imp = _dot_nt(ovt_ref[...], ps_hi) + _dot_nt(ovt_ref[...], ps_lo)
    j = lax.broadcasted_iota(jnp.int32, imp.shape, 0)
    jf = j.astype(F32)
    t = q0 + lax.broadcasted_iota(jnp.int32, (1, Q_TILE), 1)
    cur = t // SEL_LEN
    forced = (j == 0) | (j == cur) | (j == cur - 1)
    v = jnp.where(j > cur, -FORCE, jnp.where(forced, FORCE, imp))
    sel = jnp.zeros(imp.shape, jnp.bool_)
    for _ in range(N_SEL):
        mx = jnp.max(v, axis=0, keepdims=True)
        idx = jnp.min(jnp.where(v == mx, jf, float(LANES)), axis=0, keepdims=True)
        pick = jf == idx
        sel = sel | pick
        v = jnp.where(pick, -3e38, v)
    live_t = jnp.where(sel & (j <= cur), 1.0, 0.0).astype(BF16)
    eye = (lax.broadcasted_iota(jnp.int32, imp.shape, 0) == lax.broadcasted_iota(jnp.int32, imp.shape, 1))
    live = _dot_nt(eye.astype(BF16), live_t)
    selb_ref[0, 0] = jnp.where(live > 0.5, 0.0, NEG).astype(BF16)
    flag_ref[0, 0, 0] = jnp.max(live, axis=0, keepdims=True).astype(jnp.int32)


def _cmp_attention(qa, kvc, sm, ov):
    B, H, S, _ = qa.shape
    G = NSA_KV_GROUPS
    nq = S // Q_TILE
    return pl.pallas_call(
        _cmp_kernel,
        grid=(B, G, nq),
        in_specs=[
            pl.BlockSpec((1, NSA_HPG, Q_TILE, LANES), lambda b, g, i: (b, g, i, 0)),
            pl.BlockSpec((1, 1, N_CMP_PAD, LANES), lambda b, g, i: (b, g, 0, 0)),
            pl.BlockSpec((1, 1, N_CMP_PAD, LANES), lambda b, g, i: (b, 2 + g, 0, 0)),
            pl.BlockSpec((1, Q_TILE, LANES), lambda b, g, i: (b, i, 0)),
            pl.BlockSpec((LANES, N_CMP_PAD), lambda b, g, i: (0, 0)),
        ],
        out_specs=[
            pl.BlockSpec((1, Q_TILE, NSA_HPG * HEAD_DIM), lambda b, g, i: (b, i, g)),
            pl.BlockSpec((1, 1, Q_TILE, LANES), lambda b, g, i: (b, g, i, 0)),
            pl.BlockSpec((1, 1, 1, 1, LANES), lambda b, g, i: (b, g, i, 0, 0)),
        ],
        out_shape=[
            jax.ShapeDtypeStruct((B, S, NSA_W), F32),
            jax.ShapeDtypeStruct((B, G, S, LANES), BF16),
            jax.ShapeDtypeStruct((B, G, nq, 1, LANES), jnp.int32),
        ],
        compiler_params=_cparams(("parallel", "parallel", "parallel")),
        name="cmp_attention",
    )(qa, kvc, kvc, sm, ov)


def _online_update(s, v, m_ref, acc_ref):
    m_old = m_ref[...]
    m_new = jnp.maximum(m_old, jnp.max(s, axis=-1, keepdims=True))
    chunks = [s[:, c * LANES:(c + 1) * LANES] - m_new for c in range(s.shape[1] // LANES)]
    p = jnp.exp2(jnp.concatenate(chunks, axis=1))
    acc_ref[...] = jnp.exp2(m_old - m_new) * acc_ref[...] + _dot(p.astype(BF16), v)
    m_ref[...] = m_new


def _normalized(acc):
    return acc / jnp.maximum(acc[:, EXT:EXT + 1], 1e-30)


def _attend_once(s, v):
    m = jnp.broadcast_to(jnp.max(s, axis=-1, keepdims=True), (s.shape[0], LANES))
    chunks = [s[:, c * LANES:(c + 1) * LANES] - m for c in range(s.shape[1] // LANES)]
    p = jnp.exp2(jnp.concatenate(chunks, axis=1))
    return _normalized(_dot(p.astype(BF16), v))


def _selwin_kernel(flags_ref, q_ref, ks_ref, vs_ref, kw_ref, vw_ref, selb_ref, oc_ref, sm_ref,
                   o_ref, m_sc, acc_sc):
    b = pl.program_id(0)
    g = pl.program_id(1)
    qb = pl.program_id(2)
    nq = pl.num_programs(2)
    rows = NSA_HPG * Q_TILE
    q4 = q_ref[0].reshape(rows, LANES)
    q_aug = jnp.concatenate([q4, jnp.concatenate([selb_ref[0, 0]] * NSA_HPG, axis=0)], axis=1)
    r = lax.broadcasted_iota(jnp.int32, (rows, 1), 0) % Q_TILE
    c = lax.broadcasted_iota(jnp.int32, (1, K_TILE), 1)
    rel = r - c
    diag = qb // (K_TILE // Q_TILE)

    def reset():
        m_sc[...] = jnp.full(m_sc.shape, NEG, F32)
        acc_sc[...] = jnp.zeros(acc_sc.shape, F32)

    def sel_tile(kt, causal):
        start = pl.multiple_of(kt * K_TILE, K_TILE)
        s = _dot_nt(q_aug, ks_ref[0, 0, pl.ds(start, K_TILE), :])
        if causal:
            s = jnp.where(rel + (qb * Q_TILE - kt * K_TILE) >= 0, s, NEG)
        _online_update(s, vs_ref[0, 0, pl.ds(start, K_TILE), :], m_sc, acc_sc)

    reset()
    word = flags_ref[(b * NSA_KV_GROUPS + g) * nq + qb]

    def body(kt, carry):
        @pl.when(((word >> kt) & 1) == 1)
        def _():
            sel_tile(kt, False)
        return carry

    lax.fori_loop(0, diag, body, 0)
    sel_tile(diag, True)
    o_sel = _normalized(acc_sc[...])

    span = WINDOW + K_TILE
    wstart = pl.multiple_of(jnp.maximum(diag - WINDOW // K_TILE, 0) * K_TILE, K_TILE)
    dist = (qb * Q_TILE + r) - (wstart + lax.broadcasted_iota(jnp.int32, (1, span), 1))
    s = _dot_nt(q4, kw_ref[0, 0, pl.ds(wstart, span), :])
    s = jnp.where((dist >= 0) & (dist < WINDOW), s, NEG)
    o_win = _attend_once(s, vw_ref[0, 0, pl.ds(wstart, span), :])

    sm = sm_ref[0]
    y = _gate_rows(sm, g, 1) * o_sel + _gate_rows(sm, g, 2) * o_win
    o_ref[0] = (oc_ref[0] + _head_tile(y)).astype(BF16)


def _selwin_attention(flag_words, qa, ksl, nkv, selb, ocg, sm):
    B, H, S, _ = qa.shape
    G = NSA_KV_GROUPS
    nq = S // Q_TILE
    rows = NSA_HPG * Q_TILE
    kv_spec = lambda piece: pl.BlockSpec((1, 1, S, LANES), lambda b, g, i, f: (b, piece + g, 0, 0))
    out_tile = pl.BlockSpec((1, Q_TILE, NSA_HPG * HEAD_DIM), lambda b, g, i, f: (b, i, g))
    grid_spec = pltpu.PrefetchScalarGridSpec(
        num_scalar_prefetch=1,
        grid=(B, G, nq),
        in_specs=[
            pl.BlockSpec((1, NSA_HPG, Q_TILE, LANES), lambda b, g, i, f: (b, g, i, 0)),
            pl.BlockSpec((1, 1, S, 2 * LANES), lambda b, g, i, f: (b, g, 0, 0)),
            kv_spec(0), kv_spec(2), kv_spec(4),
            pl.BlockSpec((1, 1, Q_TILE, LANES), lambda b, g, i, f: (b, g, i, 0)),
            out_tile,
            pl.BlockSpec((1, Q_TILE, LANES), lambda b, g, i, f: (b, i, 0)),
        ],
        out_specs=out_tile,
        scratch_shapes=[
            pltpu.VMEM((rows, LANES), F32),
            pltpu.VMEM((rows, LANES), F32),
        ],
    )
    return pl.pallas_call(
        _selwin_kernel,
        grid_spec=grid_spec,
        out_shape=jax.ShapeDtypeStruct((B, S, NSA_W), BF16),
        compiler_params=_cparams(("parallel", "parallel", "arbitrary")),
        name="selwin_attention",
    )(flag_words, qa, ksl, nkv, nkv, nkv, selb, ocg, sm)


def _fox_kernel(q_ref, k_ref, v_ref, o_ref, m_sc, acc_sc, *, tq):
    qi = pl.program_id(2)
    m_sc[...] = jnp.full(m_sc.shape, NEG, F32)
    acc_sc[...] = jnp.zeros(acc_sc.shape, F32)

    def tile(kt, causal):
        start = pl.multiple_of(kt * tq, tq)
        for hh in range(FOX_HPS):
            s = _dot_nt(q_ref[0, hh], k_ref[0, hh, pl.ds(start, tq), :])
            if causal:
                r = lax.broadcasted_iota(jnp.int32, s.shape, 0)
                c = lax.broadcasted_iota(jnp.int32, s.shape, 1)
                s = jnp.where(r >= c, s, NEG)
            _online_update(s, v_ref[0, hh, pl.ds(start, tq), :], m_sc.at[hh], acc_sc.at[hh])

    def body(kt, carry):
        tile(kt, False)
        return carry

    lax.fori_loop(0, qi, body, 0)
    tile(qi, True)
    lane = lax.broadcasted_iota(jnp.int32, (tq, LANES), 1)
    o = [_normalized(acc_sc[hh]) for hh in range(FOX_HPS)]
    for pr in range(FOX_HPS // 2):
        o_ref[0, :, pr * LANES:(pr + 1) * LANES] = jnp.where(
            lane < HEAD_DIM, o[2 * pr], pltpu.roll(o[2 * pr + 1], HEAD_DIM, 1)).astype(BF16)


def _fox_attention(fq, fk, fv, tq=512):
    B, H, S, _ = fq.shape
    hps = FOX_HPS
    return pl.pallas_call(
        functools.partial(_fox_kernel, tq=tq),
        grid=(B, H // hps, S // tq),
        in_specs=[
            pl.BlockSpec((1, hps, tq, LANES), lambda b, h, i: (b, h, i, 0)),
            pl.BlockSpec((1, hps, S, LANES), lambda b, h, i: (b, h, 0, 0)),
            pl.BlockSpec((1, hps, S, LANES), lambda b, h, i: (b, h, 0, 0)),
        ],
        out_specs=pl.BlockSpec((1, tq, hps * HEAD_DIM), lambda b, h, i: (b, i, h)),
        out_shape=jax.ShapeDtypeStruct((B, S, FOX_W), BF16),
        scratch_shapes=[
            pltpu.VMEM((hps, tq, LANES), F32),
            pltpu.VMEM((hps, tq, LANES), F32),
        ],
        compiler_params=_cparams(("parallel", "parallel", "arbitrary")),
        name="fox_attention",
    )(fq, fk, fv)


def _merge_kernel(ya_ref, yb_ref, mg_ref, x_ref, mod_ref, gpost_ref, gpre_ref,
                  wa_ref, wb_ref, wo_ref, wrh_ref, wrl_ref, br_ref, stri_ref,
                  x1_ref, h2_ref, rt_ref, cnt_ref):
    D = D_MODEL

    @pl.when((pl.program_id(0) == 0) & (pl.program_id(1) == 0))
    def _():
        cnt_ref[...] = jnp.zeros(cnt_ref.shape, F32)

    a = _dot(ya_ref[0], wa_ref[...])
    bq = _dot(yb_ref[0], wb_ref[...])
    mg = mg_ref[0]
    u = mg[:, :D].astype(F32) * a + mg[:, D:].astype(F32) * bq
    mixed = _dot(u.astype(BF16), wo_ref[...])
    x1 = x_ref[0] + mod_ref[0, 2:3, :] * _rms(mixed, gpost_ref[...])
    x1_ref[0] = x1
    h2 = _rms(x1, gpre_ref[...]) * (1.0 + mod_ref[0, 4:5, :]) + mod_ref[0, 3:4, :]
    hi = h2.astype(BF16)
    lo = (h2 - hi.astype(F32)).astype(BF16)
    h2_ref[0] = h2
    lg = _dot(hi, wrh_ref[...]) + _dot(lo, wrh_ref[...]) + _dot(hi, wrl_ref[...]) + br_ref[...]

    lane = lax.broadcasted_iota(jnp.int32, lg.shape, 1)
    lanef = lane.astype(F32)
    no_lane = float(LANES)
    is_g = lane < N_EXPERT_GROUPS
    gl = jnp.where(is_g, lg, NEG)
    gmax = jnp.max(gl, axis=-1, keepdims=True)
    pg_top = 1.0 / jnp.sum(jnp.where(is_g, jnp.exp(gl - gmax), 0.0), axis=-1, keepdims=True)
    g_idx = jnp.min(jnp.where(is_g & (gl == gmax), lanef, no_lane), axis=-1, keepdims=True)
    in_grp = ((lane >= N_EXPERT_GROUPS) & (lane < N_EXPERT_GROUPS + N_EXPERTS)
              & (((lane - N_EXPERT_GROUPS) // EXPERTS_PER_GROUP).astype(F32) == g_idx))
    le = jnp.where(in_grp, lg, NEG)
    m1 = jnp.max(le, axis=-1, keepdims=True)
    i1 = jnp.min(jnp.where(in_grp & (le == m1), lanef, no_lane), axis=-1, keepdims=True)
    rest = in_grp & (lanef != i1)
    le2 = jnp.where(rest, lg, NEG)
    m2 = jnp.max(le2, axis=-1, keepdims=True)
    i2 = jnp.min(jnp.where(rest & (le2 == m2), lanef, no_lane), axis=-1, keepdims=True)
    e21 = jnp.exp(m2 - m1)
    w1 = pg_top / (1.0 + e21)
    w2 = w1 * e21
    pick1 = lanef == i1
    pick2 = lanef == i2
    onehot = jnp.where(pick1 | pick2, 1.0, 0.0)
    before = cnt_ref[...] + _dot(stri_ref[...], onehot.astype(BF16))
    rank1 = jnp.sum(jnp.where(pick1, before, 0.0), axis=-1, keepdims=True)
    rank2 = jnp.sum(jnp.where(pick2, before, 0.0), axis=-1, keepdims=True)
    cnt_ref[...] = cnt_ref[...] + jnp.sum(onehot, axis=0, keepdims=True)
    fields = [i1 - N_EXPERT_GROUPS, i2 - N_EXPERT_GROUPS, rank1, rank2, w1, w2]
    rt = jnp.zeros(lg.shape, F32)
    for k, f in enumerate(fields):
        rt = jnp.where(lane == k, f, rt)
    rt_ref[0] = rt


def _merge(ya, yb, mg, x, mod, gpost, gpre, wa, wb, wo, wrh, wrl, br, stri):
    B, S, D = x.shape
    tm = MERGE_TILE
    c2 = lambda b, i: (0, 0)
    row = lambda w: pl.BlockSpec((1, tm, w), lambda b, i: (b, i, 0))
    return pl.pallas_call(
        _merge_kernel,
        grid=(B, S // tm),
        in_specs=[
            row(NSA_W), row(FOX_W), row(2 * D), row(D),
            pl.BlockSpec((1, 6, D), lambda b, i: (b, 0, 0)),
            pl.BlockSpec((1, D), c2), pl.BlockSpec((1, D), c2),
            pl.BlockSpec((NSA_W, D), c2), pl.BlockSpec((FOX_W, D), c2), pl.BlockSpec((D, D), c2),
            pl.BlockSpec((D, LANES), c2), pl.BlockSpec((D, LANES), c2), pl.BlockSpec((1, LANES), c2),
            pl.BlockSpec((tm, tm), c2),
        ],
        out_specs=[row(D), row(D), row(LANES), pl.BlockSpec((1, LANES), c2)],
        out_shape=[
            jax.ShapeDtypeStruct((B, S, D), F32),
            jax.ShapeDtypeStruct((B, S, D), F32),
            jax.ShapeDtypeStruct((B, S, LANES), F32),
            jax.ShapeDtypeStruct((1, LANES), F32),
        ],
        compiler_params=_cparams(("arbitrary", "arbitrary")),
        name="merge",
    )(ya, yb, mg, x, mod, gpost, gpre, wa, wb, wo, wrh, wrl, br, stri)


def _expert_kernel(be_ref, na_ref, tok_ref, h_hbm, wg_ref, wu_ref, wd_ref, o_ref, x_even, x_odd, sem):
    i = pl.program_id(0)
    n_active = na_ref[0]
    last_block = pl.num_programs(0) - 1
    bufs = (x_even, x_odd)

    def row_copy(blk, r, sl):
        tok = tok_ref[blk * MOE_TILE + r]
        return pltpu.make_async_copy(h_hbm.at[pl.ds(tok, 1)], bufs[sl].at[pl.ds(r, 1)], sem.at[sl])

    def wait_rows(blk, sl):
        def body(r, carry):
            row_copy(blk, r, sl).wait()
            return carry
        lax.fori_loop(0, MOE_TILE, body, 0, unroll=8)

    @pl.when(i == 0)
    def _():
        def body(r, carry):
            row_copy(0, r, 0).start()
            return carry
        lax.fori_loop(0, MOE_TILE, body, 0, unroll=8)

    def step(sl):
        wait_rows(i, sl)
        nxt = jnp.minimum(i + 1, last_block)
        for r in range(MOE_TILE):
            row_copy(nxt, r, 1 - sl).start()
        x = bufs[sl][...].astype(BF16)
        gate = _dot(x, wg_ref[0])
        up = _dot(x, wu_ref[0])
        mid = (gate * jax.nn.sigmoid(gate) * up).astype(BF16)
        o_ref[...] = _dot(mid, wd_ref[0])

        @pl.when(i == n_active - 1)
        def _():
            wait_rows(nxt, 1 - sl)

    for sl in range(2):
        pl.when((i % 2 == sl) & (i < n_active))(functools.partial(step, sl))

    @pl.when(i >= n_active)
    def _():
        o_ref[...] = jnp.zeros(o_ref.shape, o_ref.dtype)


def _experts(block_expert, n_active, buf_tok, h2, wg, wu, wd):
    cap = buf_tok.shape[0]
    D = h2.shape[1]
    nblk = cap // MOE_TILE
    grid_spec = pltpu.PrefetchScalarGridSpec(
        num_scalar_prefetch=3,
        grid=(nblk,),
        in_specs=[
            pl.BlockSpec(memory_space=pl.ANY),
            pl.BlockSpec((1, D, D_EXPERT), lambda i, be, na, tok: (be[i], 0, 0)),
            pl.BlockSpec((1, D, D_EXPERT), lambda i, be, na, tok: (be[i], 0, 0)),
            pl.BlockSpec((1, D_EXPERT, D), lambda i, be, na, tok: (be[i], 0, 0)),
        ],
        out_specs=pl.BlockSpec((MOE_TILE, D), lambda i, be, na, tok: (i, 0)),
        scratch_shapes=[
            pltpu.VMEM((MOE_TILE, D), F32),
            pltpu.VMEM((MOE_TILE, D), F32),
            pltpu.SemaphoreType.DMA((2,)),
        ],
    )
    return pl.pallas_call(
        _expert_kernel,
        grid_spec=grid_spec,
        out_shape=jax.ShapeDtypeStruct((cap, D), F32),
        compiler_params=_cparams(("arbitrary",)),
        name="experts",
    )(block_expert, n_active, buf_tok, h2, wg, wu, wd)


def _final_kernel(dest_ref, x1_ref, rt_ref, mod_ref, g_ref, y_hbm, o_ref, a_even, b_even, a_odd, b_odd, sem):
    j = pl.program_id(0)
    last_tile = pl.num_programs(0) - 1
    tm = o_ref.shape[0]
    bufs = ((a_even, b_even), (a_odd, b_odd))

    def row_copy(tile, r, k, sl):
        row = dest_ref[(tile * tm + r) * EXPERT_TOP_K + k]
        return pltpu.make_async_copy(y_hbm.at[pl.ds(row, 1)], bufs[sl][k].at[pl.ds(r, 1)], sem.at[sl])

    def wait_rows(tile, sl):
        def body(r, carry):
            for k in range(EXPERT_TOP_K):
                row_copy(tile, r, k, sl).wait()
            return carry
        lax.fori_loop(0, tm, body, 0, unroll=4)

    @pl.when(j == 0)
    def _():
        def body(r, carry):
            for k in range(EXPERT_TOP_K):
                row_copy(0, r, k, 0).start()
            return carry
        lax.fori_loop(0, tm, body, 0, unroll=4)

    def step(sl):
        wait_rows(j, sl)
        nxt = jnp.minimum(j + 1, last_tile)
        for r in range(tm):
            for k in range(EXPERT_TOP_K):
                row_copy(nxt, r, k, 1 - sl).start()
        rt = rt_ref[...]
        lane = lax.broadcasted_iota(jnp.int32, rt.shape, 1)
        w0 = jnp.sum(jnp.where(lane == 4, rt, 0.0), axis=-1, keepdims=True)
        w1 = jnp.sum(jnp.where(lane == 5, rt, 0.0), axis=-1, keepdims=True)
        y = w0 * bufs[sl][0][...] + w1 * bufs[sl][1][...]
        o_ref[...] = x1_ref[...] + mod_ref[0, 5:6, :] * _rms(y, g_ref[...])

        @pl.when(j == last_tile)
        def _():
            wait_rows(nxt, 1 - sl)

    for sl in range(2):
        pl.when(j % 2 == sl)(functools.partial(step, sl))


def _final(dest, x1, rt, mod, g, yb, tiles_per_batch):
    T, D = x1.shape
    tm = MERGE_TILE
    grid_spec = pltpu.PrefetchScalarGridSpec(
        num_scalar_prefetch=1,
        grid=(T // tm,),
        in_specs=[
            pl.BlockSpec((tm, D), lambda j, d: (j, 0)),
            pl.BlockSpec((tm, LANES), lambda j, d: (j, 0)),
            pl.BlockSpec((1, 6, D), lambda j, d: (j // tiles_per_batch, 0, 0)),
            pl.BlockSpec((1, D), lambda j, d: (0, 0)),
            pl.BlockSpec(memory_space=pl.ANY),
        ],
        out_specs=pl.BlockSpec((tm, D), lambda j, d: (j, 0)),
        scratch_shapes=[pltpu.VMEM((tm, D), F32)] * 4 + [pltpu.SemaphoreType.DMA((2,))],
    )
    return pl.pallas_call(
        _final_kernel,
        grid_spec=grid_spec,
        out_shape=jax.ShapeDtypeStruct((T, D), F32),
        compiler_params=_cparams(("arbitrary",)),
        name="final",
    )(dest, x1, rt, mod, g, yb)


def _overlap_matrix():
    n = np.arange(N_CMP_PAD)[:, None]
    j = np.arange(LANES)[None, :]
    start = n * CMP_STRIDE
    ov = (start < j * SEL_LEN + SEL_LEN) & (start + CMP_LEN - 1 >= j * SEL_LEN) & (n < N_CMP_PAD - 1)
    return jnp.asarray(ov.T.astype(np.float32), dtype=BF16)


def _pad_cols(w, width=LANES):
    return jnp.pad(w, ((0, 0), (0, width - w.shape[1])))


def _dispatch_plan(rt, cnt, T):
    expert = rt[:, 0:2].astype(jnp.int32)
    rank = rt[:, 2:4].astype(jnp.int32)
    weight = rt[:, 4:6]
    counts = cnt[0, N_EXPERT_GROUPS:N_EXPERT_GROUPS + N_EXPERTS].astype(jnp.int32)
    padded = (counts + MOE_TILE - 1) // MOE_TILE * MOE_TILE
    pad_end = jnp.cumsum(padded)
    pad_start = pad_end - padded
    onehot = expert[:, :, None] == jnp.arange(N_EXPERTS)[None, None, :]
    dest = jnp.sum(jnp.where(onehot, pad_start[None, None, :], 0), axis=-1) + rank
    A = T * EXPERT_TOP_K
    cap = -(-(A + N_EXPERTS * (MOE_TILE - 1)) // MOE_TILE) * MOE_TILE
    nblk = cap // MOE_TILE
    n_active = (pad_end[-1] // MOE_TILE).astype(jnp.int32)
    blk = jnp.arange(nblk) * MOE_TILE
    block_expert = jnp.minimum(jnp.sum(pad_end[None, :] <= blk[:, None], axis=1), N_EXPERTS - 1)
    last = jnp.max(jnp.where(jnp.arange(nblk) < n_active, block_expert, 0))
    block_expert = jnp.where(jnp.arange(nblk) < n_active, block_expert, last).astype(jnp.int32)
    tok = jnp.arange(A, dtype=jnp.int32) // EXPERT_TOP_K
    buf_tok = jnp.zeros((cap,), jnp.int32).at[dest.reshape(A)].set(tok)
    return weight, dest, buf_tok, block_expert, n_active.reshape(1)


def kernel(x, c, w_ada, b_ada, g_pre_mix, g_post_mix, g_pre_ffn, g_post_ffn, w_in, b_forget,
           cmp_pe_k, cmp_w1_k, cmp_w2_k, cmp_pe_v, cmp_w1_v, cmp_w2_v,
           w_o_nsa, w_o_fox, w_out, w_router_group, b_router_group, w_router_expert, b_router_expert,
           w_exp_gate, w_exp_up, w_exp_down):
    B, S, D = x.shape
    T = B * S
    depth = w_ada.shape[0]
    ov = _overlap_matrix()
    tri = jnp.asarray(np.tril(np.ones((IN_TILE, IN_TILE), np.float32)), dtype=BF16)
    stri = jnp.asarray(np.tril(np.ones((MERGE_TILE, MERGE_TILE), np.float32), -1), dtype=BF16)
    row_feat = _row_features(S)
    placement = _placement()
    cmp_ext = _cmp_key_ext()
    for l in range(depth):
        mod = (jax.nn.silu(c) @ w_ada[l] + b_ada[l]).reshape(B, 6, D)
        w_qa, w_kva, w_gl, w_fox, w_f, w_mg = jnp.split(w_in[l], IN_SPLITS, axis=-1)
        w_big = jnp.concatenate([w_qa, w_kva, w_fox, w_mg], axis=1).astype(BF16)
        w_small = _pad_cols(jnp.concatenate([w_gl, w_f], axis=1)).astype(BF16)
        bf_pad = jnp.pad(b_forget[l], (F_LANE, LANES - F_LANE - FOX_HEADS)).reshape(1, LANES)
        qa, ckv, ksl, nkv, fq, fk, fv, mg, sm = _inproj(
            x, mod, g_pre_mix[l].reshape(1, D), w_big, w_small, bf_pad, tri, row_feat, placement)

        half = CMP_LEN // 2
        pe = jnp.stack([cmp_pe_k[l], cmp_pe_v[l]]).reshape(2, 2, 1, half * HEAD_DIM)
        w1 = jnp.stack([cmp_w1_k[l], cmp_w1_v[l]]).reshape(2, 2, half * HEAD_DIM, HEAD_DIM).astype(BF16)
        w2 = jnp.pad(jnp.stack([cmp_w2_k[l], cmp_w2_v[l]]), ((0, 0), (0, 0), (0, LANES - HEAD_DIM))).astype(BF16)
        kvc = _compress(ckv.reshape(B, 4, S // CMP_STRIDE, CMP_STRIDE * HEAD_DIM), pe, w1, w2, cmp_ext)
        ocg, selb, flags = _cmp_attention(qa, kvc, sm, ov)
        nq = S // Q_TILE
        per_tile = K_TILE // SEL_LEN
        tile_any = jnp.max(flags.reshape(B, NSA_KV_GROUPS, nq, LANES // per_tile, per_tile), axis=-1)
        bits = tile_any.astype(jnp.uint32) << jnp.arange(LANES // per_tile, dtype=jnp.uint32)
        flag_words = lax.bitcast_convert_type(jnp.sum(bits, axis=-1, dtype=jnp.uint32), jnp.int32).reshape(-1)
        y_a = _selwin_attention(flag_words, qa, ksl, nkv, selb, ocg, sm)

        y_b = _fox_attention(fq, fk, fv)

        w_r = _pad_cols(jnp.concatenate([w_router_group[l], w_router_expert[l]], axis=1))
        w_rh = w_r.astype(BF16)
        w_rl = (w_r - w_rh.astype(F32)).astype(BF16)
        b_r = _pad_cols(jnp.concatenate([b_router_group[l], b_router_expert[l]]).reshape(1, -1))
        x1, h2, rt, cnt = _merge(y_a, y_b, mg, x, mod, g_post_mix[l].reshape(1, D), g_pre_ffn[l].reshape(1, D),
                                 w_o_nsa[l].astype(BF16), w_o_fox[l].astype(BF16), w_out[l].astype(BF16),
                                 w_rh, w_rl, b_r, stri)

        weight, dest, buf_tok, block_expert, n_active = _dispatch_plan(rt.reshape(T, LANES), cnt, T)
        yb = _experts(block_expert, n_active, buf_tok, h2.reshape(T, D), w_exp_gate[l].astype(BF16),
                      w_exp_up[l].astype(BF16), w_exp_down[l].astype(BF16))
        x = _final(dest.reshape(T * EXPERT_TOP_K), x1.reshape(T, D), rt.reshape(T, LANES), mod,
                   g_post_ffn[l].reshape(1, D), yb, S // MERGE_TILE).reshape(B, S, D)
    return x
```

```python
import functools

import ml_dtypes
import numpy as np
import jax
import jax.numpy as jnp
from jax import lax
from jax.experimental import pallas as pl
from jax.experimental.pallas import tpu as pltpu

D_MODEL = 1024
HEAD_DIM = 64
NSA_HEADS = 8
NSA_KV_GROUPS = 2
NSA_HPG = NSA_HEADS // NSA_KV_GROUPS
FOX_HEADS = 8
CMP_LEN = 32
CMP_STRIDE = 16
SEL_LEN = 64
N_SEL = 16
WINDOW = 512
N_EXPERT_GROUPS = 4
EXPERTS_PER_GROUP = 8
N_EXPERTS = N_EXPERT_GROUPS * EXPERTS_PER_GROUP
EXPERT_TOP_K = 2
D_EXPERT = D_MODEL // 2
NORM_EPS = 1e-6
NEG = -1e30
FORCE = 1e9
LOG2E = 1.4426950408889634

NSA_W = NSA_HEADS * HEAD_DIM
NSA_KV_W = NSA_KV_GROUPS * HEAD_DIM
FOX_W = FOX_HEADS * HEAD_DIM
IN_SIZES = (NSA_W, 6 * NSA_KV_W, 3 * NSA_HEADS, 3 * FOX_W, FOX_HEADS, 2 * D_MODEL)
IN_SPLITS = tuple(int(v) for v in np.cumsum(IN_SIZES)[:-1])

LANES = 128
Q_TILE = 128
K_TILE = 256
N_CMP_PAD = 512
MOE_TILE = 256
IN_TILE = 512
MERGE_TILE = 256
FOX_HPS = 4
ROW_CHUNKS = D_MODEL // LANES
VMEM_LIMIT = 56 * 1024 * 1024

F_LANE = 3 * NSA_HEADS
U_LANE = 64
ONE_LANE = 88
A_LANE = 89
B_LANE = 90
EXT = HEAD_DIM
G_FQ, G_FK, G_NQ, G_NK, N_GROUPS = 0, 8, 16, 24, 25

F32 = jnp.float32
BF16 = jnp.bfloat16


def _dot(a, b):
    return jnp.dot(a, b, preferred_element_type=F32)


def _dot_nt(a, b):
    return lax.dot_general(a, b, (((1,), (1,)), ((), ())), preferred_element_type=F32)


def _store_row_tiles(ref, val):
    for c in range(ROW_CHUNKS):
        ref[:, c, :] = val[:, c * LANES:(c + 1) * LANES]


def _load_row_tiles(ref):
    return jnp.concatenate([ref[:, c, :] for c in range(ROW_CHUNKS)], axis=1)


def _rms(x, g):
    return x * lax.rsqrt(jnp.mean(x * x, axis=-1, keepdims=True) + NORM_EPS) * g


def _cparams(sem):
    return pltpu.CompilerParams(dimension_semantics=sem, vmem_limit_bytes=VMEM_LIMIT)


def _split3(x):
    hi = x.astype(BF16).astype(F32)
    r = x - hi
    mid = r.astype(BF16).astype(F32)
    lo = (r - mid).astype(BF16).astype(F32)
    return hi, mid, lo


def _np_split3(x):
    x = np.asarray(x, np.float32)
    hi = x.astype(ml_dtypes.bfloat16).astype(np.float32)
    r = x - hi
    mid = r.astype(ml_dtypes.bfloat16).astype(np.float32)
    lo = (r - mid).astype(ml_dtypes.bfloat16).astype(np.float32)
    return hi, mid, lo


def _alibi_c():
    slopes = np.exp2(-8.0 * np.arange(1, NSA_HEADS + 1, dtype=np.float32) / NSA_HEADS).astype(np.float32)
    return slopes * np.float32(LOG2E)


def _row_features(S):
    t = np.arange(S, dtype=np.float32)
    c = _alibi_c()
    rs = np.zeros((S, LANES), np.float32)
    for h in range(NSA_HEADS):
        for j, term in enumerate(_np_split3(c[h] * t)):
            rs[:, U_LANE + 8 * j + h] = -term
    rs[:, ONE_LANE] = 1.0
    rs[:, A_LANE] = np.floor(t / LANES)
    rs[:, B_LANE] = t % LANES
    return jnp.asarray(rs, dtype=BF16)


def _placement():
    c = _alibi_c()
    p = np.zeros((LANES, N_GROUPS * LANES), np.float32)
    for h in range(FOX_HEADS):
        q0 = (G_FQ + h) * LANES + EXT
        k0 = (G_FK + h) * LANES + EXT
        for j in range(3):
            p[ONE_LANE, q0 + j] = -1.0
            p[F_LANE + 8 * j + h, q0 + 3 + j] = 1.0
            p[F_LANE + 8 * j + h, k0 + j] = 1.0
            p[ONE_LANE, k0 + 3 + j] = 1.0
    for h in range(NSA_HEADS):
        q0 = (G_NQ + h) * LANES + EXT
        c128 = _np_split3(c[h] * np.float32(LANES))
        c1 = _np_split3(c[h])
        for j in range(3):
            p[U_LANE + 8 * j + h, q0 + j] = 1.0
            p[ONE_LANE, q0 + 3 + j] = c128[j]
            p[ONE_LANE, q0 + 6 + j] = c1[j]
    k0 = G_NK * LANES + EXT
    for j in range(3):
        p[ONE_LANE, k0 + j] = 1.0
        p[A_LANE, k0 + 3 + j] = 1.0
        p[B_LANE, k0 + 6 + j] = 1.0
    return jnp.asarray(p, dtype=BF16)


def _cmp_key_ext():
    pos = np.arange(N_CMP_PAD, dtype=np.float32) * CMP_STRIDE + (CMP_LEN - 1)
    e = np.zeros((2, N_CMP_PAD, LANES), np.float32)
    for j in range(3):
        e[0, :, EXT + j] = 1.0
        e[0, :, EXT + 3 + j] = np.floor(pos / LANES)
        e[0, :, EXT + 6 + j] = pos % LANES
    return jnp.asarray(e, dtype=BF16)


def _inproj_kernel(x_ref, mod_ref, g_ref, wb_ref, ws_ref, bf_ref, tri_ref, rs_ref, p_ref,
                   qa_ref, ckv_ref, ksl_ref, nkv_ref, fq_ref, fk_ref, fv_ref, mg_ref, sm_ref, carry_sc):
    i = pl.program_id(1)
    tm = x_ref.shape[1]
    x = x_ref[0]
    h = _rms(x, g_ref[...]) * (1.0 + mod_ref[0, 1:2, :]) + mod_ref[0, 0:1, :]
    hb = h.astype(BF16)
    lane = lax.broadcasted_iota(jnp.int32, (tm, LANES), 1)
    lower = lane < HEAD_DIM
    ones_col = (lane == EXT).astype(F32)

    z = _dot(hb, ws_ref[...]) + bf_ref[...]
    logsig = jnp.minimum(z, 0.0) - jnp.log1p(jnp.exp(-jnp.abs(z)))
    sm_ref[0] = jnp.where(lane < F_LANE, jax.nn.sigmoid(z), logsig)

    @pl.when(i == 0)
    def _():
        carry_sc[...] = jnp.zeros(carry_sc.shape, F32)

    is_f = (lane >= F_LANE) & (lane < F_LANE + FOX_HEADS)
    l_hi, l_mid, l_lo = _split3(jnp.where(is_f, logsig, 0.0))
    tri = tri_ref[...]
    cum = carry_sc[...] + _dot(tri, l_hi.astype(BF16)) + _dot(tri, l_mid.astype(BF16)) + _dot(tri, l_lo.astype(BF16))
    carry_sc[...] = cum[tm - 1:tm, :]
    f_hi, f_mid, f_lo = _split3(cum * LOG2E)
    feat = (f_hi + pltpu.roll(f_mid, 8, 1) + pltpu.roll(f_lo, 16, 1) + rs_ref[...].astype(F32)).astype(BF16)

    def ext(group):
        return _dot(feat, p_ref[:, group * LANES:(group + 1) * LANES])

    def piece(acc, idx, extra):
        pair = acc[:, (idx // 2) * LANES:(idx // 2 + 1) * LANES]
        if idx % 2:
            pair = pltpu.roll(pair, HEAD_DIM, 1)
        return jnp.where(lower, pair, extra).astype(BF16)

    qscale = (HEAD_DIM ** -0.5) * LOG2E
    acc = _dot(hb, wb_ref[:, 0:NSA_W]) * qscale
    for hd in range(NSA_HEADS):
        qa_ref[0, hd] = piece(acc, hd, ext(G_NQ + hd))
    off = NSA_W
    acc = _dot(hb, wb_ref[:, off:off + 6 * NSA_KV_W])
    for pc in range(4):
        ckv_ref[0, pc] = acc[:, pc * HEAD_DIM:(pc + 1) * HEAD_DIM].astype(BF16)
    ext_k = ext(G_NK)
    t = i * tm + lax.broadcasted_iota(jnp.int32, (tm, LANES), 0)
    block_onehot = (lane == t // SEL_LEN).astype(BF16)
    for g in range(NSA_KV_GROUPS):
        ksl_ref[0, g, :, 0:LANES] = piece(acc, 4 + g, ext_k)
        ksl_ref[0, g, :, LANES:2 * LANES] = block_onehot
        nkv_ref[0, g] = piece(acc, 6 + g, ones_col)
        nkv_ref[0, 2 + g] = piece(acc, 8 + g, ext_k)
        nkv_ref[0, 4 + g] = piece(acc, 10 + g, ones_col)
    off += 6 * NSA_KV_W
    acc = _dot(hb, wb_ref[:, off:off + FOX_W]) * qscale
    for hd in range(FOX_HEADS):
        fq_ref[0, hd] = piece(acc, hd, ext(G_FQ + hd))
    off += FOX_W
    acc = _dot(hb, wb_ref[:, off:off + FOX_W])
    for hd in range(FOX_HEADS):
        fk_ref[0, hd] = piece(acc, hd, ext(G_FK + hd))
    off += FOX_W
    acc = _dot(hb, wb_ref[:, off:off + FOX_W])
    for hd in range(FOX_HEADS):
        fv_ref[0, hd] = piece(acc, hd, ones_col)
    off += FOX_W
    for c in range(4):
        acc = _dot(hb, wb_ref[:, off + c * 512: off + (c + 1) * 512])
        mg_ref[0, :, c * 512:(c + 1) * 512] = jax.nn.sigmoid(acc).astype(BF16)


def _inproj(x, mod, g, wb, ws, bfp, tri, rs, pm):
    B, S, D = x.shape
    tm = IN_TILE
    nb = wb.shape[1]
    const2 = lambda b, i: (0, 0)
    heads = lambda n: pl.BlockSpec((1, n, tm, LANES), lambda b, i: (b, 0, i, 0))
    hshape = lambda n: jax.ShapeDtypeStruct((B, n, S, LANES), BF16)
    return pl.pallas_call(
        _inproj_kernel,
        grid=(B, S // tm),
        in_specs=[
            pl.BlockSpec((1, tm, D), lambda b, i: (b, i, 0)),
            pl.BlockSpec((1, 6, D), lambda b, i: (b, 0, 0)),
            pl.BlockSpec((1, D), const2),
            pl.BlockSpec((D, nb), const2),
            pl.BlockSpec((D, LANES), const2),
            pl.BlockSpec((1, LANES), const2),
            pl.BlockSpec((tm, tm), const2),
            pl.BlockSpec((tm, LANES), lambda b, i: (i, 0)),
            pl.BlockSpec((LANES, N_GROUPS * LANES), const2),
        ],
        out_specs=[
            heads(NSA_HEADS),
            pl.BlockSpec((1, 4, tm, HEAD_DIM), lambda b, i: (b, 0, i, 0)),
            pl.BlockSpec((1, NSA_KV_GROUPS, tm, 2 * LANES), lambda b, i: (b, 0, i, 0)),
            heads(6), heads(FOX_HEADS), heads(FOX_HEADS), heads(FOX_HEADS),
            pl.BlockSpec((1, tm, 2 * D), lambda b, i: (b, i, 0)),
            pl.BlockSpec((1, tm, LANES), lambda b, i: (b, i, 0)),
        ],
        out_shape=[
            hshape(NSA_HEADS),
            jax.ShapeDtypeStruct((B, 4, S, HEAD_DIM), BF16),
            jax.ShapeDtypeStruct((B, NSA_KV_GROUPS, S, 2 * LANES), BF16),
            hshape(6), hshape(FOX_HEADS), hshape(FOX_HEADS), hshape(FOX_HEADS),
            jax.ShapeDtypeStruct((B, S, 2 * D), BF16),
            jax.ShapeDtypeStruct((B, S, LANES), F32),
        ],
        scratch_shapes=[pltpu.VMEM((1, LANES), F32)],
        compiler_params=_cparams(("parallel", "arbitrary")),
        name="inproj",
    )(x, mod, g, wb, ws, bfp, tri, rs, pm)


def _compress_kernel(x_ref, pe_ref, w1_ref, w2_ref, e_ref, o_ref):
    x = x_ref[0, 0].astype(F32)
    x_lo = (x + pe_ref[0, 0]).astype(BF16)
    x_hi = (x + pe_ref[0, 1]).astype(BF16)
    y_lo = _dot(x_lo, w1_ref[0, 0])
    y_hi = _dot(x_hi, w1_ref[0, 1])
    n = y_hi.shape[0]
    hid = y_lo + pltpu.roll(y_hi, n - 1, 0)
    hid = jax.nn.gelu(hid)
    o_ref[0, 0] = (_dot(hid.astype(BF16), w2_ref[0]) + e_ref[0].astype(F32)).astype(BF16)


def _compress(kv_rows, pe, w1, w2, e):
    B = kv_rows.shape[0]
    R, C = kv_rows.shape[2], kv_rows.shape[3]
    return pl.pallas_call(
        _compress_kernel,
        grid=(B, 4),
        in_specs=[
            pl.BlockSpec((1, 1, R, C), lambda b, p: (b, p, 0, 0)),
            pl.BlockSpec((1, 2, 1, C), lambda b, p: (p // 2, 0, 0, 0)),
            pl.BlockSpec((1, 2, C, HEAD_DIM), lambda b, p: (p // 2, 0, 0, 0)),
            pl.BlockSpec((1, HEAD_DIM, LANES), lambda b, p: (p // 2, 0, 0)),
            pl.BlockSpec((1, R, LANES), lambda b, p: (p // 2, 0, 0)),
        ],
        out_specs=pl.BlockSpec((1, 1, R, LANES), lambda b, p: (b, p, 0, 0)),
        out_shape=jax.ShapeDtypeStruct((B, 4, R, LANES), BF16),
        compiler_params=_cparams(("parallel", "parallel")),
        name="compress",
    )(kv_rows, pe, w1, w2, e)


def _gate_rows(sm, g, branch):
    col = lax.broadcasted_iota(jnp.int32, sm.shape, 1)
    parts = []
    for hl in range(NSA_HPG):
        want = 3 * (NSA_HPG * g + hl) + branch
        parts.append(jnp.sum(jnp.where(col == want, sm, 0.0), axis=-1, keepdims=True))
    return jnp.concatenate(parts, axis=0)


def _head_tile(y):
    lane = lax.broadcasted_iota(jnp.int32, (Q_TILE, LANES), 1)
    hs = [y[i * Q_TILE:(i + 1) * Q_TILE] for i in range(NSA_HPG)]
    pairs = [jnp.where(lane < HEAD_DIM, hs[2 * i], pltpu.roll(hs[2 * i + 1], HEAD_DIM, 1)) for i in range(2)]
    return jnp.concatenate(pairs, axis=1)


def _cmp_kernel(q_ref, kc_ref, vc_ref, sm_ref, ovt_ref, oc_ref, selb_ref, flag_ref):
    g = pl.program_id(1)
    qb = pl.program_id(2)
    q0 = qb * Q_TILE
    q = q_ref[0].reshape(NSA_HPG * Q_TILE, LANES)
    s = _dot_nt(q, kc_ref[0, 0])
    r = lax.broadcasted_iota(jnp.int32, (NSA_HPG * Q_TILE, 1), 0) % Q_TILE
    n = lax.broadcasted_iota(jnp.int32, (1, N_CMP_PAD), 1)
    dc = (q0 + r) - (n * CMP_STRIDE + (CMP_LEN - 1))
    mask = (dc >= 0) & (n < N_CMP_PAD - 1)
    l = jnp.where(mask, s, NEG)
    m = jnp.max(l, axis=-1, keepdims=True)
    e = jnp.where(mask, jnp.exp2(l - m), 0.0)
    pc = e / jnp.maximum(jnp.sum(e, axis=-1, keepdims=True), 1e-30)
    oc = _dot(pc.astype(BF16), vc_ref[0, 0])
    oc_ref[0] = _head_tile(oc * _gate_rows(sm_ref[0], g, 0))
    ps = pc[0:Q_TILE]
    for i in range(1, NSA_HPG):
        ps = ps + pc[i * Q_TILE:(i + 1) * Q_TILE]
    ps_hi = ps.astype(BF16)
    ps_lo = (ps - ps_hi.astype(F32)).astype(BF16)
    imp = _dot_nt(ovt_ref[...], ps_hi) + _dot_nt(ovt_ref[...], ps_lo)
    j = lax.broadcasted_iota(jnp.int32, imp.shape, 0)
    jf = j.astype(F32)
    t = q0 + lax.broadcasted_iota(jnp.int32, (1, Q_TILE), 1)
    cur = t // SEL_LEN
    forced = (j == 0) | (j == cur) | (j == cur - 1)
    v = jnp.where(j > cur, -FORCE, jnp.where(forced, FORCE, imp))
    sel = jnp.zeros(imp.shape, jnp.bool_)
    for _ in range(N_SEL):
        mx = jnp.max(v, axis=0, keepdims=True)
        idx = jnp.min(jnp.where(v == mx, jf, float(LANES)), axis=0, keepdims=True)
        pick = jf == idx
        sel = sel | pick
        v = jnp.where(pick, -3e38, v)
    live_t = jnp.where(sel & (j <= cur), 1.0, 0.0).astype(BF16)
    eye = (lax.broadcasted_iota(jnp.int32, imp.shape, 0) == lax.broadcasted_iota(jnp.int32, imp.shape, 1))
    live = _dot_nt(eye.astype(BF16), live_t)
    selb_ref[0, 0] = jnp.where(live > 0.5, 0.0, NEG).astype(BF16)
    flag_ref[0, 0, 0] = jnp.max(live, axis=0, keepdims=True).astype(jnp.int32)


def _cmp_attention(qa, kvc, sm, ov):
    B, H, S, _ = qa.shape
    G = NSA_KV_GROUPS
    nq = S // Q_TILE
    return pl.pallas_call(
        _cmp_kernel,
        grid=(B, G, nq),
        in_specs=[
            pl.BlockSpec((1, NSA_HPG, Q_TILE, LANES), lambda b, g, i: (b, g, i, 0)),
            pl.BlockSpec((1, 1, N_CMP_PAD, LANES), lambda b, g, i: (b, g, 0, 0)),
            pl.BlockSpec((1, 1, N_CMP_PAD, LANES), lambda b, g, i: (b, 2 + g, 0, 0)),
            pl.BlockSpec((1, Q_TILE, LANES), lambda b, g, i: (b, i, 0)),
            pl.BlockSpec((LANES, N_CMP_PAD), lambda b, g, i: (0, 0)),
        ],
        out_specs=[
            pl.BlockSpec((1, Q_TILE, NSA_HPG * HEAD_DIM), lambda b, g, i: (b, i, g)),
            pl.BlockSpec((1, 1, Q_TILE, LANES), lambda b, g, i: (b, g, i, 0)),
            pl.BlockSpec((1, 1, 1, 1, LANES), lambda b, g, i: (b, g, i, 0, 0)),
        ],
        out_shape=[
            jax.ShapeDtypeStruct((B, S, NSA_W), F32),
            jax.ShapeDtypeStruct((B, G, S, LANES), BF16),
            jax.ShapeDtypeStruct((B, G, nq, 1, LANES), jnp.int32),
        ],
        compiler_params=_cparams(("parallel", "parallel", "parallel")),
        name="cmp_attention",
    )(qa, kvc, kvc, sm, ov)


def _online_update(s, v, m_ref, acc_ref):
    m_old = m_ref[...]
    m_new = jnp.maximum(m_old, jnp.max(s, axis=-1, keepdims=True))
    chunks = [s[:, c * LANES:(c + 1) * LANES] - m_new for c in range(s.shape[1] // LANES)]
    p = jnp.exp2(jnp.concatenate(chunks, axis=1))
    acc_ref[...] = jnp.exp2(m_old - m_new) * acc_ref[...] + _dot(p.astype(BF16), v)
    m_ref[...] = m_new


def _normalized(acc):
    return acc / jnp.maximum(acc[:, EXT:EXT + 1], 1e-30)


def _attend_once(s, v):
    m = jnp.broadcast_to(jnp.max(s, axis=-1, keepdims=True), (s.shape[0], LANES))
    chunks = [s[:, c * LANES:(c + 1) * LANES] - m for c in range(s.shape[1] // LANES)]
    p = jnp.exp2(jnp.concatenate(chunks, axis=1))
    return _normalized(_dot(p.astype(BF16), v))


def _selwin_kernel(flags_ref, q_ref, ks_ref, vs_ref, kw_ref, vw_ref, selb_ref, oc_ref, sm_ref,
                   o_ref, m_sc, acc_sc):
    b = pl.program_id(0)
    g = pl.program_id(1)
    qb = pl.program_id(2)
    nq = pl.num_programs(2)
    rows = NSA_HPG * Q_TILE
    q4 = q_ref[0].reshape(rows, LANES)
    q_aug = jnp.concatenate([q4, jnp.concatenate([selb_ref[0, 0]] * NSA_HPG, axis=0)], axis=1)
    r = lax.broadcasted_iota(jnp.int32, (rows, 1), 0) % Q_TILE
    c = lax.broadcasted_iota(jnp.int32, (1, K_TILE), 1)
    rel = r - c
    diag = qb // (K_TILE // Q_TILE)

    def reset():
        m_sc[...] = jnp.full(m_sc.shape, NEG, F32)
        acc_sc[...] = jnp.zeros(acc_sc.shape, F32)

    def sel_tile(kt, causal):
        start = pl.multiple_of(kt * K_TILE, K_TILE)
        s = _dot_nt(q_aug, ks_ref[0, 0, pl.ds(start, K_TILE), :])
        if causal:
            s = jnp.where(rel + (qb * Q_TILE - kt * K_TILE) >= 0, s, NEG)
        _online_update(s, vs_ref[0, 0, pl.ds(start, K_TILE), :], m_sc, acc_sc)

    reset()
    word = flags_ref[(b * NSA_KV_GROUPS + g) * nq + qb]

    def body(kt, carry):
        @pl.when(((word >> kt) & 1) == 1)
        def _():
            sel_tile(kt, False)
        return carry

    lax.fori_loop(0, diag, body, 0)
    sel_tile(diag, True)
    o_sel = _normalized(acc_sc[...])

    span = WINDOW + K_TILE
    wstart = pl.multiple_of(jnp.maximum(diag - WINDOW // K_TILE, 0) * K_TILE, K_TILE)
    dist = (qb * Q_TILE + r) - (wstart + lax.broadcasted_iota(jnp.int32, (1, span), 1))
    s = _dot_nt(q4, kw_ref[0, 0, pl.ds(wstart, span), :])
    s = jnp.where((dist >= 0) & (dist < WINDOW), s, NEG)
    o_win = _attend_once(s, vw_ref[0, 0, pl.ds(wstart, span), :])

    sm = sm_ref[0]
    y = _gate_rows(sm, g, 1) * o_sel + _gate_rows(sm, g, 2) * o_win
    o_ref[0] = (oc_ref[0] + _head_tile(y)).astype(BF16)


def _selwin_attention(flag_words, qa, ksl, nkv, selb, ocg, sm):
    B, H, S, _ = qa.shape
    G = NSA_KV_GROUPS
    nq = S // Q_TILE
    rows = NSA_HPG * Q_TILE
    kv_spec = lambda piece: pl.BlockSpec((1, 1, S, LANES), lambda b, g, i, f: (b, piece + g, 0, 0))
    out_tile = pl.BlockSpec((1, Q_TILE, NSA_HPG * HEAD_DIM), lambda b, g, i, f: (b, i, g))
    grid_spec = pltpu.PrefetchScalarGridSpec(
        num_scalar_prefetch=1,
        grid=(B, G, nq),
        in_specs=[
            pl.BlockSpec((1, NSA_HPG, Q_TILE, LANES), lambda b, g, i, f: (b, g, i, 0)),
            pl.BlockSpec((1, 1, S, 2 * LANES), lambda b, g, i, f: (b, g, 0, 0)),
            kv_spec(0), kv_spec(2), kv_spec(4),
            pl.BlockSpec((1, 1, Q_TILE, LANES), lambda b, g, i, f: (b, g, i, 0)),
            out_tile,
            pl.BlockSpec((1, Q_TILE, LANES), lambda b, g, i, f: (b, i, 0)),
        ],
        out_specs=out_tile,
        scratch_shapes=[
            pltpu.VMEM((rows, LANES), F32),
            pltpu.VMEM((rows, LANES), F32),
        ],
    )
    return pl.pallas_call(
        _selwin_kernel,
        grid_spec=grid_spec,
        out_shape=jax.ShapeDtypeStruct((B, S, NSA_W), BF16),
        compiler_params=_cparams(("parallel", "parallel", "arbitrary")),
        name="selwin_attention",
    )(flag_words, qa, ksl, nkv, nkv, nkv, selb, ocg, sm)


def _fox_kernel(q_ref, k_ref, v_ref, o_ref, m_sc, acc_sc, *, tq):
    qi = pl.program_id(2)
    m_sc[...] = jnp.full(m_sc.shape, NEG, F32)
    acc_sc[...] = jnp.zeros(acc_sc.shape, F32)

    def tile(kt, causal):
        start = pl.multiple_of(kt * tq, tq)
        for hh in range(FOX_HPS):
            s = _dot_nt(q_ref[0, hh], k_ref[0, hh, pl.ds(start, tq), :])
            if causal:
                r = lax.broadcasted_iota(jnp.int32, s.shape, 0)
                c = lax.broadcasted_iota(jnp.int32, s.shape, 1)
                s = jnp.where(r >= c, s, NEG)
            _online_update(s, v_ref[0, hh, pl.ds(start, tq), :], m_sc.at[hh], acc_sc.at[hh])

    def body(kt, carry):
        tile(kt, False)
        return carry

    lax.fori_loop(0, qi, body, 0)
    tile(qi, True)
    lane = lax.broadcasted_iota(jnp.int32, (tq, LANES), 1)
    o = [_normalized(acc_sc[hh]) for hh in range(FOX_HPS)]
    for pr in range(FOX_HPS // 2):
        o_ref[0, :, pr * LANES:(pr + 1) * LANES] = jnp.where(
            lane < HEAD_DIM, o[2 * pr], pltpu.roll(o[2 * pr + 1], HEAD_DIM, 1)).astype(BF16)


def _fox_attention(fq, fk, fv, tq=512):
    B, H, S, _ = fq.shape
    hps = FOX_HPS
    return pl.pallas_call(
        functools.partial(_fox_kernel, tq=tq),
        grid=(B, H // hps, S // tq),
        in_specs=[
            pl.BlockSpec((1, hps, tq, LANES), lambda b, h, i: (b, h, i, 0)),
            pl.BlockSpec((1, hps, S, LANES), lambda b, h, i: (b, h, 0, 0)),
            pl.BlockSpec((1, hps, S, LANES), lambda b, h, i: (b, h, 0, 0)),
        ],
        out_specs=pl.BlockSpec((1, tq, hps * HEAD_DIM), lambda b, h, i: (b, i, h)),
        out_shape=jax.ShapeDtypeStruct((B, S, FOX_W), BF16),
        scratch_shapes=[
            pltpu.VMEM((hps, tq, LANES), F32),
            pltpu.VMEM((hps, tq, LANES), F32),
        ],
        compiler_params=_cparams(("parallel", "parallel", "arbitrary")),
        name="fox_attention",
    )(fq, fk, fv)


def _merge_kernel(ya_ref, yb_ref, mg_ref, x_ref, mod_ref, gpost_ref, gpre_ref,
                  wa_ref, wb_ref, wo_ref, wrh_ref, wrl_ref, br_ref, stri_ref,
                  x1_ref, h2_ref, rt_ref, cnt_ref):
    D = D_MODEL

    @pl.when((pl.program_id(0) == 0) & (pl.program_id(1) == 0))
    def _():
        cnt_ref[...] = jnp.zeros(cnt_ref.shape, F32)

    a = _dot(ya_ref[0], wa_ref[...])
    bq = _dot(yb_ref[0], wb_ref[...])
    mg = mg_ref[0]
    u = mg[:, :D].astype(F32) * a + mg[:, D:].astype(F32) * bq
    mixed = _dot(u.astype(BF16), wo_ref[...])
    x1 = x_ref[0] + mod_ref[0, 2:3, :] * _rms(mixed, gpost_ref[...])
    x1_ref[0] = x1
    h2 = _rms(x1, gpre_ref[...]) * (1.0 + mod_ref[0, 4:5, :]) + mod_ref[0, 3:4, :]
    hi = h2.astype(BF16)
    lo = (h2 - hi.astype(F32)).astype(BF16)
    _store_row_tiles(h2_ref.at[0], h2)
    lg = _dot(hi, wrh_ref[...]) + _dot(lo, wrh_ref[...]) + _dot(hi, wrl_ref[...]) + br_ref[...]

    lane = lax.broadcasted_iota(jnp.int32, lg.shape, 1)
    lanef = lane.astype(F32)
    no_lane = float(LANES)
    is_g = lane < N_EXPERT_GROUPS
    gl = jnp.where(is_g, lg, NEG)
    gmax = jnp.max(gl, axis=-1, keepdims=True)
    pg_top = 1.0 / jnp.sum(jnp.where(is_g, jnp.exp(gl - gmax), 0.0), axis=-1, keepdims=True)
    g_idx = jnp.min(jnp.where(is_g & (gl == gmax), lanef, no_lane), axis=-1, keepdims=True)
    in_grp = ((lane >= N_EXPERT_GROUPS) & (lane < N_EXPERT_GROUPS + N_EXPERTS)
              & (((lane - N_EXPERT_GROUPS) // EXPERTS_PER_GROUP).astype(F32) == g_idx))
    le = jnp.where(in_grp, lg, NEG)
    m1 = jnp.max(le, axis=-1, keepdims=True)
    i1 = jnp.min(jnp.where(in_grp & (le == m1), lanef, no_lane), axis=-1, keepdims=True)
    rest = in_grp & (lanef != i1)
    le2 = jnp.where(rest, lg, NEG)
    m2 = jnp.max(le2, axis=-1, keepdims=True)
    i2 = jnp.min(jnp.where(rest & (le2 == m2), lanef, no_lane), axis=-1, keepdims=True)
    e21 = jnp.exp(m2 - m1)
    w1 = pg_top / (1.0 + e21)
    w2 = w1 * e21
    pick1 = lanef == i1
    pick2 = lanef == i2
    onehot = jnp.where(pick1 | pick2, 1.0, 0.0)
    before = cnt_ref[...] + _dot(stri_ref[...], onehot.astype(BF16))
    rank1 = jnp.sum(jnp.where(pick1, before, 0.0), axis=-1, keepdims=True)
    rank2 = jnp.sum(jnp.where(pick2, before, 0.0), axis=-1, keepdims=True)
    cnt_ref[...] = cnt_ref[...] + jnp.sum(onehot, axis=0, keepdims=True)
    fields = [i1 - N_EXPERT_GROUPS, i2 - N_EXPERT_GROUPS, rank1, rank2, w1, w2]
    rt = jnp.zeros(lg.shape, F32)
    for k, f in enumerate(fields):
        rt = jnp.where(lane == k, f, rt)
    rt_ref[0] = rt


def _merge(ya, yb, mg, x, mod, gpost, gpre, wa, wb, wo, wrh, wrl, br, stri):
    B, S, D = x.shape
    tm = MERGE_TILE
    c2 = lambda b, i: (0, 0)
    row = lambda w: pl.BlockSpec((1, tm, w), lambda b, i: (b, i, 0))
    return pl.pallas_call(
        _merge_kernel,
        grid=(B, S // tm),
        in_specs=[
            row(NSA_W), row(FOX_W), row(2 * D), row(D),
            pl.BlockSpec((1, 6, D), lambda b, i: (b, 0, 0)),
            pl.BlockSpec((1, D), c2), pl.BlockSpec((1, D), c2),
            pl.BlockSpec((NSA_W, D), c2), pl.BlockSpec((FOX_W, D), c2), pl.BlockSpec((D, D), c2),
            pl.BlockSpec((D, LANES), c2), pl.BlockSpec((D, LANES), c2), pl.BlockSpec((1, LANES), c2),
            pl.BlockSpec((tm, tm), c2),
        ],
        out_specs=[row(D), pl.BlockSpec((1, tm, ROW_CHUNKS, LANES), lambda b, i: (b, i, 0, 0)), row(LANES),
                   pl.BlockSpec((1, LANES), c2)],
        out_shape=[
            jax.ShapeDtypeStruct((B, S, D), F32),
            jax.ShapeDtypeStruct((B, S, ROW_CHUNKS, LANES), F32),
            jax.ShapeDtypeStruct((B, S, LANES), F32),
            jax.ShapeDtypeStruct((1, LANES), F32),
        ],
        compiler_params=_cparams(("arbitrary", "arbitrary")),
        name="merge",
    )(ya, yb, mg, x, mod, gpost, gpre, wa, wb, wo, wrh, wrl, br, stri)


def _expert_kernel(be_ref, na_ref, tok_ref, h_hbm, wg_ref, wu_ref, wd_ref, o_ref, x_even, x_odd, sem):
    i = pl.program_id(0)
    n_active = na_ref[0]
    last_block = pl.num_programs(0) - 1
    bufs = (x_even, x_odd)

    def row_copy(blk, r, sl):
        tok = tok_ref[blk * MOE_TILE + r]
        return pltpu.make_async_copy(h_hbm.at[pl.ds(tok, 1)], bufs[sl].at[pl.ds(r, 1)], sem.at[sl])

    def wait_rows(blk, sl):
        def body(r, carry):
            row_copy(blk, r, sl).wait()
            return carry
        lax.fori_loop(0, MOE_TILE, body, 0, unroll=8)

    @pl.when(i == 0)
    def _():
        def body(r, carry):
            row_copy(0, r, 0).start()
            return carry
        lax.fori_loop(0, MOE_TILE, body, 0, unroll=8)

    def step(sl):
        wait_rows(i, sl)
        nxt = jnp.minimum(i + 1, last_block)
        for r in range(MOE_TILE):
            row_copy(nxt, r, 1 - sl).start()
        x = _load_row_tiles(bufs[sl]).astype(BF16)
        gate = _dot(x, wg_ref[0])
        up = _dot(x, wu_ref[0])
        mid = (gate * jax.nn.sigmoid(gate) * up).astype(BF16)
        _store_row_tiles(o_ref, _dot(mid, wd_ref[0]))

        @pl.when(i == n_active - 1)
        def _():
            wait_rows(nxt, 1 - sl)

    for sl in range(2):
        pl.when((i % 2 == sl) & (i < n_active))(functools.partial(step, sl))

    @pl.when(i >= n_active)
    def _():
        o_ref[...] = jnp.zeros(o_ref.shape, o_ref.dtype)


def _experts(block_expert, n_active, buf_tok, h2, wg, wu, wd):
    cap = buf_tok.shape[0]
    D = D_MODEL
    nblk = cap // MOE_TILE
    grid_spec = pltpu.PrefetchScalarGridSpec(
        num_scalar_prefetch=3,
        grid=(nblk,),
        in_specs=[
            pl.BlockSpec(memory_space=pl.ANY),
            pl.BlockSpec((1, D, D_EXPERT), lambda i, be, na, tok: (be[i], 0, 0)),
            pl.BlockSpec((1, D, D_EXPERT), lambda i, be, na, tok: (be[i], 0, 0)),
            pl.BlockSpec((1, D_EXPERT, D), lambda i, be, na, tok: (be[i], 0, 0)),
        ],
        out_specs=pl.BlockSpec((MOE_TILE, ROW_CHUNKS, LANES), lambda i, be, na, tok: (i, 0, 0)),
        scratch_shapes=[
            pltpu.VMEM((MOE_TILE, ROW_CHUNKS, LANES), F32),
            pltpu.VMEM((MOE_TILE, ROW_CHUNKS, LANES), F32),
            pltpu.SemaphoreType.DMA((2,)),
        ],
    )
    return pl.pallas_call(
        _expert_kernel,
        grid_spec=grid_spec,
        out_shape=jax.ShapeDtypeStruct((cap, ROW_CHUNKS, LANES), F32),
        compiler_params=_cparams(("arbitrary",)),
        name="experts",
    )(block_expert, n_active, buf_tok, h2, wg, wu, wd)


def _final_kernel(dest_ref, x1_ref, rt_ref, mod_ref, g_ref, y_hbm, o_ref, a_even, b_even, a_odd, b_odd, sem):
    j = pl.program_id(0)
    last_tile = pl.num_programs(0) - 1
    tm = o_ref.shape[0]
    bufs = ((a_even, b_even), (a_odd, b_odd))

    def row_copy(tile, r, k, sl):
        row = dest_ref[(tile * tm + r) * EXPERT_TOP_K + k]
        return pltpu.make_async_copy(y_hbm.at[pl.ds(row, 1)], bufs[sl][k].at[pl.ds(r, 1)], sem.at[sl])

    def wait_rows(tile, sl):
        def body(r, carry):
            for k in range(EXPERT_TOP_K):
                row_copy(tile, r, k, sl).wait()
            return carry
        lax.fori_loop(0, tm, body, 0, unroll=4)

    @pl.when(j == 0)
    def _():
        def body(r, carry):
            for k in range(EXPERT_TOP_K):
                row_copy(0, r, k, 0).start()
            return carry
        lax.fori_loop(0, tm, body, 0, unroll=4)

    def step(sl):
        wait_rows(j, sl)
        nxt = jnp.minimum(j + 1, last_tile)
        for r in range(tm):
            for k in range(EXPERT_TOP_K):
                row_copy(nxt, r, k, 1 - sl).start()
        rt = rt_ref[...]
        lane = lax.broadcasted_iota(jnp.int32, rt.shape, 1)
        w0 = jnp.sum(jnp.where(lane == 4, rt, 0.0), axis=-1, keepdims=True)
        w1 = jnp.sum(jnp.where(lane == 5, rt, 0.0), axis=-1, keepdims=True)
        y = w0 * _load_row_tiles(bufs[sl][0]) + w1 * _load_row_tiles(bufs[sl][1])
        o_ref[...] = x1_ref[...] + mod_ref[0, 5:6, :] * _rms(y, g_ref[...])

        @pl.when(j == last_tile)
        def _():
            wait_rows(nxt, 1 - sl)

    for sl in range(2):
        pl.when(j % 2 == sl)(functools.partial(step, sl))


def _final(dest, x1, rt, mod, g, yb, tiles_per_batch):
    T, D = x1.shape
    tm = MERGE_TILE
    grid_spec = pltpu.PrefetchScalarGridSpec(
        num_scalar_prefetch=1,
        grid=(T // tm,),
        in_specs=[
            pl.BlockSpec((tm, D), lambda j, d: (j, 0)),
            pl.BlockSpec((tm, LANES), lambda j, d: (j, 0)),
            pl.BlockSpec((1, 6, D), lambda j, d: (j // tiles_per_batch, 0, 0)),
            pl.BlockSpec((1, D), lambda j, d: (0, 0)),
            pl.BlockSpec(memory_space=pl.ANY),
        ],
        out_specs=pl.BlockSpec((tm, D), lambda j, d: (j, 0)),
        scratch_shapes=[pltpu.VMEM((tm, ROW_CHUNKS, LANES), F32)] * 4 + [pltpu.SemaphoreType.DMA((2,))],
    )
    return pl.pallas_call(
        _final_kernel,
        grid_spec=grid_spec,
        out_shape=jax.ShapeDtypeStruct((T, D), F32),
        compiler_params=_cparams(("arbitrary",)),
        name="final",
    )(dest, x1, rt, mod, g, yb)


def _overlap_matrix():
    n = np.arange(N_CMP_PAD)[:, None]
    j = np.arange(LANES)[None, :]
    start = n * CMP_STRIDE
    ov = (start < j * SEL_LEN + SEL_LEN) & (start + CMP_LEN - 1 >= j * SEL_LEN) & (n < N_CMP_PAD - 1)
    return jnp.asarray(ov.T.astype(np.float32), dtype=BF16)


def _pad_cols(w, width=LANES):
    return jnp.pad(w, ((0, 0), (0, width - w.shape[1])))


def _dispatch_plan(rt, cnt, T):
    expert = rt[:, 0:2].astype(jnp.int32)
    rank = rt[:, 2:4].astype(jnp.int32)
    weight = rt[:, 4:6]
    counts = cnt[0, N_EXPERT_GROUPS:N_EXPERT_GROUPS + N_EXPERTS].astype(jnp.int32)
    padded = (counts + MOE_TILE - 1) // MOE_TILE * MOE_TILE
    pad_end = jnp.cumsum(padded)
    pad_start = pad_end - padded
    onehot = expert[:, :, None] == jnp.arange(N_EXPERTS)[None, None, :]
    dest = jnp.sum(jnp.where(onehot, pad_start[None, None, :], 0), axis=-1) + rank
    A = T * EXPERT_TOP_K
    cap = -(-(A + N_EXPERTS * (MOE_TILE - 1)) // MOE_TILE) * MOE_TILE
    nblk = cap // MOE_TILE
    n_active = (pad_end[-1] // MOE_TILE).astype(jnp.int32)
    blk = jnp.arange(nblk) * MOE_TILE
    block_expert = jnp.minimum(jnp.sum(pad_end[None, :] <= blk[:, None], axis=1), N_EXPERTS - 1)
    last = jnp.max(jnp.where(jnp.arange(nblk) < n_active, block_expert, 0))
    block_expert = jnp.where(jnp.arange(nblk) < n_active, block_expert, last).astype(jnp.int32)
    tok = jnp.arange(A, dtype=jnp.int32) // EXPERT_TOP_K
    buf_tok = jnp.zeros((cap,), jnp.int32).at[dest.reshape(A)].set(tok)
    return weight, dest, buf_tok, block_expert, n_active.reshape(1)


def kernel(x, c, w_ada, b_ada, g_pre_mix, g_post_mix, g_pre_ffn, g_post_ffn, w_in, b_forget,
           cmp_pe_k, cmp_w1_k, cmp_w2_k, cmp_pe_v, cmp_w1_v, cmp_w2_v,
           w_o_nsa, w_o_fox, w_out, w_router_group, b_router_group, w_router_expert, b_router_expert,
           w_exp_gate, w_exp_up, w_exp_down):
    B, S, D = x.shape
    T = B * S
    depth = w_ada.shape[0]
    ov = _overlap_matrix()
    tri = jnp.asarray(np.tril(np.ones((IN_TILE, IN_TILE), np.float32)), dtype=BF16)
    stri = jnp.asarray(np.tril(np.ones((MERGE_TILE, MERGE_TILE), np.float32), -1), dtype=BF16)
    row_feat = _row_features(S)
    placement = _placement()
    cmp_ext = _cmp_key_ext()
    for l in range(depth):
        mod = (jax.nn.silu(c) @ w_ada[l] + b_ada[l]).reshape(B, 6, D)
        w_qa, w_kva, w_gl, w_fox, w_f, w_mg = jnp.split(w_in[l], IN_SPLITS, axis=-1)
        w_big = jnp.concatenate([w_qa, w_kva, w_fox, w_mg], axis=1).astype(BF16)
        w_small = _pad_cols(jnp.concatenate([w_gl, w_f], axis=1)).astype(BF16)
        bf_pad = jnp.pad(b_forget[l], (F_LANE, LANES - F_LANE - FOX_HEADS)).reshape(1, LANES)
        qa, ckv, ksl, nkv, fq, fk, fv, mg, sm = _inproj(
            x, mod, g_pre_mix[l].reshape(1, D), w_big, w_small, bf_pad, tri, row_feat, placement)

        half = CMP_LEN // 2
        pe = jnp.stack([cmp_pe_k[l], cmp_pe_v[l]]).reshape(2, 2, 1, half * HEAD_DIM)
        w1 = jnp.stack([cmp_w1_k[l], cmp_w1_v[l]]).reshape(2, 2, half * HEAD_DIM, HEAD_DIM).astype(BF16)
        w2 = jnp.pad(jnp.stack([cmp_w2_k[l], cmp_w2_v[l]]), ((0, 0), (0, 0), (0, LANES - HEAD_DIM))).astype(BF16)
        kvc = _compress(ckv.reshape(B, 4, S // CMP_STRIDE, CMP_STRIDE * HEAD_DIM), pe, w1, w2, cmp_ext)
        ocg, selb, flags = _cmp_attention(qa, kvc, sm, ov)
        nq = S // Q_TILE
        per_tile = K_TILE // SEL_LEN
        tile_any = jnp.max(flags.reshape(B, NSA_KV_GROUPS, nq, LANES // per_tile, per_tile), axis=-1)
        bits = tile_any.astype(jnp.uint32) << jnp.arange(LANES // per_tile, dtype=jnp.uint32)
        flag_words = lax.bitcast_convert_type(jnp.sum(bits, axis=-1, dtype=jnp.uint32), jnp.int32).reshape(-1)
        y_a = _selwin_attention(flag_words, qa, ksl, nkv, selb, ocg, sm)

        y_b = _fox_attention(fq, fk, fv)

        w_r = _pad_cols(jnp.concatenate([w_router_group[l], w_router_expert[l]], axis=1))
        w_rh = w_r.astype(BF16)
        w_rl = (w_r - w_rh.astype(F32)).astype(BF16)
        b_r = _pad_cols(jnp.concatenate([b_router_group[l], b_router_expert[l]]).reshape(1, -1))
        x1, h2, rt, cnt = _merge(y_a, y_b, mg, x, mod, g_post_mix[l].reshape(1, D), g_pre_ffn[l].reshape(1, D),
                                 w_o_nsa[l].astype(BF16), w_o_fox[l].astype(BF16), w_out[l].astype(BF16),
                                 w_rh, w_rl, b_r, stri)

        weight, dest, buf_tok, block_expert, n_active = _dispatch_plan(rt.reshape(T, LANES), cnt, T)
        yb = _experts(block_expert, n_active, buf_tok, h2.reshape(T, ROW_CHUNKS, LANES), w_exp_gate[l].astype(BF16),
                      w_exp_up[l].astype(BF16), w_exp_down[l].astype(BF16))
        x = _final(dest.reshape(T * EXPERT_TOP_K), x1.reshape(T, D), rt.reshape(T, LANES), mod,
                   g_post_ffn[l].reshape(1, D), yb, S // MERGE_TILE).reshape(B, S, D)
    return x
```

```python
import functools

import ml_dtypes
import numpy as np
import jax
import jax.numpy as jnp
from jax import lax
from jax.experimental import pallas as pl
from jax.experimental.pallas import tpu as pltpu

D_MODEL = 1024
HEAD_DIM = 64
NSA_HEADS = 8
NSA_KV_GROUPS = 2
NSA_HPG = NSA_HEADS // NSA_KV_GROUPS
FOX_HEADS = 8
CMP_LEN = 32
CMP_STRIDE = 16
SEL_LEN = 64
N_SEL = 16
WINDOW = 512
N_EXPERT_GROUPS = 4
EXPERTS_PER_GROUP = 8
N_EXPERTS = N_EXPERT_GROUPS * EXPERTS_PER_GROUP
EXPERT_TOP_K = 2
D_EXPERT = D_MODEL // 2
NORM_EPS = 1e-6
NEG = -1e30
FORCE = 1e9
LOG2E = 1.4426950408889634

NSA_W = NSA_HEADS * HEAD_DIM
NSA_KV_W = NSA_KV_GROUPS * HEAD_DIM
FOX_W = FOX_HEADS * HEAD_DIM
IN_SIZES = (NSA_W, 6 * NSA_KV_W, 3 * NSA_HEADS, 3 * FOX_W, FOX_HEADS, 2 * D_MODEL)
IN_SPLITS = tuple(int(v) for v in np.cumsum(IN_SIZES)[:-1])

LANES = 128
Q_TILE = 128
K_TILE = 256
N_CMP_PAD = 512
MOE_TILE = 256
IN_TILE = 512
MERGE_TILE = 256
FOX_HPS = 4
CMP_SUB = 2
VMEM_LIMIT = 56 * 1024 * 1024

F_LANE = 3 * NSA_HEADS
U_LANE = 64
ONE_LANE = 88
A_LANE = 89
B_LANE = 90
EXT = HEAD_DIM
G_FQ, G_FK, G_NQ, G_NK, N_GROUPS = 0, 8, 16, 24, 25

F32 = jnp.float32
BF16 = jnp.bfloat16


def _dot(a, b):
    return jnp.dot(a, b, preferred_element_type=F32)


def _dot_nt(a, b):
    return lax.dot_general(a, b, (((1,), (1,)), ((), ())), preferred_element_type=F32)


def _rms(x, g):
    return x * lax.rsqrt(jnp.mean(x * x, axis=-1, keepdims=True) + NORM_EPS) * g


def _cparams(sem):
    return pltpu.CompilerParams(dimension_semantics=sem, vmem_limit_bytes=VMEM_LIMIT)


def _split3(x):
    hi = x.astype(BF16).astype(F32)
    r = x - hi
    mid = r.astype(BF16).astype(F32)
    lo = (r - mid).astype(BF16).astype(F32)
    return hi, mid, lo


def _np_split3(x):
    x = np.asarray(x, np.float32)
    hi = x.astype(ml_dtypes.bfloat16).astype(np.float32)
    r = x - hi
    mid = r.astype(ml_dtypes.bfloat16).astype(np.float32)
    lo = (r - mid).astype(ml_dtypes.bfloat16).astype(np.float32)
    return hi, mid, lo


def _alibi_c():
    slopes = np.exp2(-8.0 * np.arange(1, NSA_HEADS + 1, dtype=np.float32) / NSA_HEADS).astype(np.float32)
    return slopes * np.float32(LOG2E)


def _row_features(S):
    t = np.arange(S, dtype=np.float32)
    c = _alibi_c()
    rs = np.zeros((S, LANES), np.float32)
    for h in range(NSA_HEADS):
        for j, term in enumerate(_np_split3(c[h] * t)):
            rs[:, U_LANE + 8 * j + h] = -term
    rs[:, ONE_LANE] = 1.0
    rs[:, A_LANE] = np.floor(t / LANES)
    rs[:, B_LANE] = t % LANES
    return jnp.asarray(rs, dtype=BF16)


def _placement():
    c = _alibi_c()
    p = np.zeros((LANES, N_GROUPS * LANES), np.float32)
    for h in range(FOX_HEADS):
        q0 = (G_FQ + h) * LANES + EXT
        k0 = (G_FK + h) * LANES + EXT
        for j in range(3):
            p[ONE_LANE, q0 + j] = -1.0
            p[F_LANE + 8 * j + h, q0 + 3 + j] = 1.0
            p[F_LANE + 8 * j + h, k0 + j] = 1.0
            p[ONE_LANE, k0 + 3 + j] = 1.0
    for h in range(NSA_HEADS):
        q0 = (G_NQ + h) * LANES + EXT
        c128 = _np_split3(c[h] * np.float32(LANES))
        c1 = _np_split3(c[h])
        for j in range(3):
            p[U_LANE + 8 * j + h, q0 + j] = 1.0
            p[ONE_LANE, q0 + 3 + j] = c128[j]
            p[ONE_LANE, q0 + 6 + j] = c1[j]
    k0 = G_NK * LANES + EXT
    for j in range(3):
        p[ONE_LANE, k0 + j] = 1.0
        p[A_LANE, k0 + 3 + j] = 1.0
        p[B_LANE, k0 + 6 + j] = 1.0
    return jnp.asarray(p, dtype=BF16)


def _cmp_key_ext():
    pos = np.arange(N_CMP_PAD, dtype=np.float32) * CMP_STRIDE + (CMP_LEN - 1)
    e = np.zeros((2, N_CMP_PAD, LANES), np.float32)
    for j in range(3):
        e[0, :, EXT + j] = 1.0
        e[0, :, EXT + 3 + j] = np.floor(pos / LANES)
        e[0, :, EXT + 6 + j] = pos % LANES
    return jnp.asarray(e, dtype=BF16)


def _inproj_kernel(x_ref, mod_ref, g_ref, wb_ref, ws_ref, bf_ref, tri_ref, rs_ref, p_ref,
                   qa_ref, ckv_ref, ksl_ref, nkv_ref, fq_ref, fk_ref, fv_ref, mg_ref, sm_ref, carry_sc):
    i = pl.program_id(1)
    tm = x_ref.shape[1]
    x = x_ref[0]
    h = _rms(x, g_ref[...]) * (1.0 + mod_ref[0, 1:2, :]) + mod_ref[0, 0:1, :]
    hb = h.astype(BF16)
    lane = lax.broadcasted_iota(jnp.int32, (tm, LANES), 1)
    lower = lane < HEAD_DIM
    ones_col = (lane == EXT).astype(F32)

    z = _dot(hb, ws_ref[...]) + bf_ref[...]
    logsig = jnp.minimum(z, 0.0) - jnp.log1p(jnp.exp(-jnp.abs(z)))
    sm_ref[0] = jnp.where(lane < F_LANE, jax.nn.sigmoid(z), logsig)

    @pl.when(i == 0)
    def _():
        carry_sc[...] = jnp.zeros(carry_sc.shape, F32)

    is_f = (lane >= F_LANE) & (lane < F_LANE + FOX_HEADS)
    l_hi, l_mid, l_lo = _split3(jnp.where(is_f, logsig, 0.0))
    tri = tri_ref[...]
    cum = carry_sc[...] + _dot(tri, l_hi.astype(BF16)) + _dot(tri, l_mid.astype(BF16)) + _dot(tri, l_lo.astype(BF16))
    carry_sc[...] = cum[tm - 1:tm, :]
    f_hi, f_mid, f_lo = _split3(cum * LOG2E)
    feat = (f_hi + pltpu.roll(f_mid, 8, 1) + pltpu.roll(f_lo, 16, 1) + rs_ref[...].astype(F32)).astype(BF16)

    def ext(group):
        return _dot(feat, p_ref[:, group * LANES:(group + 1) * LANES])

    def piece(acc, idx, extra):
        pair = acc[:, (idx // 2) * LANES:(idx // 2 + 1) * LANES]
        if idx % 2:
            pair = pltpu.roll(pair, HEAD_DIM, 1)
        return jnp.where(lower, pair, extra).astype(BF16)

    qscale = (HEAD_DIM ** -0.5) * LOG2E
    acc = _dot(hb, wb_ref[:, 0:NSA_W]) * qscale
    for hd in range(NSA_HEADS):
        qa_ref[0, hd] = piece(acc, hd, ext(G_NQ + hd))
    off = NSA_W
    acc = _dot(hb, wb_ref[:, off:off + 6 * NSA_KV_W])
    for pc in range(4):
        ckv_ref[0, pc] = acc[:, pc * HEAD_DIM:(pc + 1) * HEAD_DIM].astype(BF16)
    ext_k = ext(G_NK)
    t = i * tm + lax.broadcasted_iota(jnp.int32, (tm, LANES), 0)
    block_onehot = (lane == t // SEL_LEN).astype(BF16)
    for g in range(NSA_KV_GROUPS):
        ksl_ref[0, g, :, 0:LANES] = piece(acc, 4 + g, ext_k)
        ksl_ref[0, g, :, LANES:2 * LANES] = block_onehot
        nkv_ref[0, g] = piece(acc, 6 + g, ones_col)
        nkv_ref[0, 2 + g] = piece(acc, 8 + g, ext_k)
        nkv_ref[0, 4 + g] = piece(acc, 10 + g, ones_col)
    off += 6 * NSA_KV_W
    acc = _dot(hb, wb_ref[:, off:off + FOX_W]) * qscale
    for hd in range(FOX_HEADS):
        fq_ref[0, hd] = piece(acc, hd, ext(G_FQ + hd))
    off += FOX_W
    acc = _dot(hb, wb_ref[:, off:off + FOX_W])
    for hd in range(FOX_HEADS):
        fk_ref[0, hd] = piece(acc, hd, ext(G_FK + hd))
    off += FOX_W
    acc = _dot(hb, wb_ref[:, off:off + FOX_W])
    for hd in range(FOX_HEADS):
        fv_ref[0, hd] = piece(acc, hd, ones_col)
    off += FOX_W
    for c in range(4):
        acc = _dot(hb, wb_ref[:, off + c * 512: off + (c + 1) * 512])
        mg_ref[0, :, c * 512:(c + 1) * 512] = jax.nn.sigmoid(acc).astype(BF16)


def _inproj(x, mod, g, wb, ws, bfp, tri, rs, pm):
    B, S, D = x.shape
    tm = IN_TILE
    nb = wb.shape[1]
    const2 = lambda b, i: (0, 0)
    heads = lambda n: pl.BlockSpec((1, n, tm, LANES), lambda b, i: (b, 0, i, 0))
    hshape = lambda n: jax.ShapeDtypeStruct((B, n, S, LANES), BF16)
    return pl.pallas_call(
        _inproj_kernel,
        grid=(B, S // tm),
        in_specs=[
            pl.BlockSpec((1, tm, D), lambda b, i: (b, i, 0)),
            pl.BlockSpec((1, 6, D), lambda b, i: (b, 0, 0)),
            pl.BlockSpec((1, D), const2),
            pl.BlockSpec((D, nb), const2),
            pl.BlockSpec((D, LANES), const2),
            pl.BlockSpec((1, LANES), const2),
            pl.BlockSpec((tm, tm), const2),
            pl.BlockSpec((tm, LANES), lambda b, i: (i, 0)),
            pl.BlockSpec((LANES, N_GROUPS * LANES), const2),
        ],
        out_specs=[
            heads(NSA_HEADS),
            pl.BlockSpec((1, 4, tm, HEAD_DIM), lambda b, i: (b, 0, i, 0)),
            pl.BlockSpec((1, NSA_KV_GROUPS, tm, 2 * LANES), lambda b, i: (b, 0, i, 0)),
            heads(6), heads(FOX_HEADS), heads(FOX_HEADS), heads(FOX_HEADS),
            pl.BlockSpec((1, tm, 2 * D), lambda b, i: (b, i, 0)),
            pl.BlockSpec((1, tm, LANES), lambda b, i: (b, i, 0)),
        ],
        out_shape=[
            hshape(NSA_HEADS),
            jax.ShapeDtypeStruct((B, 4, S, HEAD_DIM), BF16),
            jax.ShapeDtypeStruct((B, NSA_KV_GROUPS, S, 2 * LANES), BF16),
            hshape(6), hshape(FOX_HEADS), hshape(FOX_HEADS), hshape(FOX_HEADS),
            jax.ShapeDtypeStruct((B, S, 2 * D), BF16),
            jax.ShapeDtypeStruct((B, S, LANES), F32),
        ],
        scratch_shapes=[pltpu.VMEM((1, LANES), F32)],
        compiler_params=_cparams(("parallel", "arbitrary")),
        name="inproj",
    )(x, mod, g, wb, ws, bfp, tri, rs, pm)


def _compress_kernel(x_ref, pe_ref, w1_ref, w2_ref, e_ref, o_ref):
    x = x_ref[0, 0].astype(F32)
    x_lo = (x + pe_ref[0, 0]).astype(BF16)
    x_hi = (x + pe_ref[0, 1]).astype(BF16)
    y_lo = _dot(x_lo, w1_ref[0, 0])
    y_hi = _dot(x_hi, w1_ref[0, 1])
    n = y_hi.shape[0]
    hid = y_lo + pltpu.roll(y_hi, n - 1, 0)
    hid = jax.nn.gelu(hid)
    o_ref[0, 0] = (_dot(hid.astype(BF16), w2_ref[0]) + e_ref[0].astype(F32)).astype(BF16)


def _compress(kv_rows, pe, w1, w2, e):
    B = kv_rows.shape[0]
    R, C = kv_rows.shape[2], kv_rows.shape[3]
    return pl.pallas_call(
        _compress_kernel,
        grid=(B, 4),
        in_specs=[
            pl.BlockSpec((1, 1, R, C), lambda b, p: (b, p, 0, 0)),
            pl.BlockSpec((1, 2, 1, C), lambda b, p: (p // 2, 0, 0, 0)),
            pl.BlockSpec((1, 2, C, HEAD_DIM), lambda b, p: (p // 2, 0, 0, 0)),
            pl.BlockSpec((1, HEAD_DIM, LANES), lambda b, p: (p // 2, 0, 0)),
            pl.BlockSpec((1, R, LANES), lambda b, p: (p // 2, 0, 0)),
        ],
        out_specs=pl.BlockSpec((1, 1, R, LANES), lambda b, p: (b, p, 0, 0)),
        out_shape=jax.ShapeDtypeStruct((B, 4, R, LANES), BF16),
        compiler_params=_cparams(("parallel", "parallel")),
        name="compress",
    )(kv_rows, pe, w1, w2, e)


def _gate_rows(sm, g, branch):
    col = lax.broadcasted_iota(jnp.int32, sm.shape, 1)
    parts = []
    for hl in range(NSA_HPG):
        want = 3 * (NSA_HPG * g + hl) + branch
        parts.append(jnp.sum(jnp.where(col == want, sm, 0.0), axis=-1, keepdims=True))
    return jnp.concatenate(parts, axis=0)


def _head_tile(y):
    lane = lax.broadcasted_iota(jnp.int32, (Q_TILE, LANES), 1)
    hs = [y[i * Q_TILE:(i + 1) * Q_TILE] for i in range(NSA_HPG)]
    pairs = [jnp.where(lane < HEAD_DIM, hs[2 * i], pltpu.roll(hs[2 * i + 1], HEAD_DIM, 1)) for i in range(2)]
    return jnp.concatenate(pairs, axis=1)


def _cmp_kernel(q_ref, kc_ref, vc_ref, sm_ref, ovt_ref, oc_ref, selb_ref, flag_ref):
    g = pl.program_id(1)
    for sub in range(CMP_SUB):
        rows = pl.ds(sub * Q_TILE, Q_TILE)
        q0 = (pl.program_id(2) * CMP_SUB + sub) * Q_TILE
        q = q_ref[0, :, rows, :].reshape(NSA_HPG * Q_TILE, LANES)
        oc, selb, flag = _cmp_tile(q, kc_ref[0, 0], vc_ref[0, 0], sm_ref[0, rows, :], ovt_ref[...], g, q0)
        oc_ref[0, rows, :] = oc
        selb_ref[0, 0, rows, :] = selb
        flag_ref[0, 0, sub] = flag


def _cmp_tile(q, kc, vc, sm, ovt, g, q0):
    s = _dot_nt(q, kc)
    r = lax.broadcasted_iota(jnp.int32, (NSA_HPG * Q_TILE, 1), 0) % Q_TILE
    n = lax.broadcasted_iota(jnp.int32, (1, N_CMP_PAD), 1)
    dc = (q0 + r) - (n * CMP_STRIDE + (CMP_LEN - 1))
    mask = (dc >= 0) & (n < N_CMP_PAD - 1)
    l = jnp.where(mask, s, NEG)
    m = jnp.max(l, axis=-1, keepdims=True)
    e = jnp.where(mask, jnp.exp2(l - m), 0.0)
    pc = e / jnp.maximum(jnp.sum(e, axis=-1, keepdims=True), 1e-30)
    oc = _dot(pc.astype(BF16), vc)
    oc = _head_tile(oc * _gate_rows(sm, g, 0))
    ps = pc[0:Q_TILE]
    for i in range(1, NSA_HPG):
        ps = ps + pc[i * Q_TILE:(i + 1) * Q_TILE]
    ps_hi = ps.astype(BF16)
    ps_lo = (ps - ps_hi.astype(F32)).astype(BF16)
    imp = _dot_nt(ovt, ps_hi) + _dot_nt(ovt, ps_lo)
    j = lax.broadcasted_iota(jnp.int32, imp.shape, 0)
    jf = j.astype(F32)
    t = q0 + lax.broadcasted_iota(jnp.int32, (1, Q_TILE), 1)
    cur = t // SEL_LEN
    forced = (j == 0) | (j == cur) | (j == cur - 1)
    v = jnp.where(j > cur, -FORCE, jnp.where(forced, FORCE, imp))
    sel = jnp.zeros(imp.shape, jnp.bool_)
    for _ in range(N_SEL):
        mx = jnp.max(v, axis=0, keepdims=True)
        idx = jnp.min(jnp.where(v == mx, jf, float(LANES)), axis=0, keepdims=True)
        pick = jf == idx
        sel = sel | pick
        v = jnp.where(pick, -3e38, v)
    live_t = jnp.where(sel & (j <= cur), 1.0, 0.0).astype(BF16)
    eye = (lax.broadcasted_iota(jnp.int32, imp.shape, 0) == lax.broadcasted_iota(jnp.int32, imp.shape, 1))
    live = _dot_nt(eye.astype(BF16), live_t)
    selb = jnp.where(live > 0.5, 0.0, NEG).astype(BF16)
    return oc, selb, jnp.max(live, axis=0, keepdims=True).astype(jnp.int32)


def _cmp_attention(qa, kvc, sm, ov):
    B, H, S, _ = qa.shape
    G = NSA_KV_GROUPS
    nq = S // Q_TILE
    qt = CMP_SUB * Q_TILE
    return pl.pallas_call(
        _cmp_kernel,
        grid=(B, G, nq // CMP_SUB),
        in_specs=[
            pl.BlockSpec((1, NSA_HPG, qt, LANES), lambda b, g, i: (b, g, i, 0)),
            pl.BlockSpec((1, 1, N_CMP_PAD, LANES), lambda b, g, i: (b, g, 0, 0)),
            pl.BlockSpec((1, 1, N_CMP_PAD, LANES), lambda b, g, i: (b, 2 + g, 0, 0)),
            pl.BlockSpec((1, qt, LANES), lambda b, g, i: (b, i, 0)),
            pl.BlockSpec((LANES, N_CMP_PAD), lambda b, g, i: (0, 0)),
        ],
        out_specs=[
            pl.BlockSpec((1, qt, NSA_HPG * HEAD_DIM), lambda b, g, i: (b, i, g)),
            pl.BlockSpec((1, 1, qt, LANES), lambda b, g, i: (b, g, i, 0)),
            pl.BlockSpec((1, 1, CMP_SUB, 1, LANES), lambda b, g, i: (b, g, i, 0, 0)),
        ],
        out_shape=[
            jax.ShapeDtypeStruct((B, S, NSA_W), F32),
            jax.ShapeDtypeStruct((B, G, S, LANES), BF16),
            jax.ShapeDtypeStruct((B, G, nq, 1, LANES), jnp.int32),
        ],
        compiler_params=_cparams(("parallel", "parallel", "parallel")),
        name="cmp_attention",
    )(qa, kvc, kvc, sm, ov)


def _online_update(s, v, m_ref, acc_ref):
    m_old = m_ref[...]
    m_new = jnp.maximum(m_old, jnp.max(s, axis=-1, keepdims=True))
    chunks = [s[:, c * LANES:(c + 1) * LANES] - m_new for c in range(s.shape[1] // LANES)]
    p = jnp.exp2(jnp.concatenate(chunks, axis=1))
    acc_ref[...] = jnp.exp2(m_old - m_new) * acc_ref[...] + _dot(p.astype(BF16), v)
    m_ref[...] = m_new


def _normalized(acc):
    return acc / jnp.maximum(acc[:, EXT:EXT + 1], 1e-30)


def _attend_once(s, v):
    m = jnp.broadcast_to(jnp.max(s, axis=-1, keepdims=True), (s.shape[0], LANES))
    chunks = [s[:, c * LANES:(c + 1) * LANES] - m for c in range(s.shape[1] // LANES)]
    p = jnp.exp2(jnp.concatenate(chunks, axis=1))
    return _normalized(_dot(p.astype(BF16), v))


def _selwin_kernel(flags_ref, q_ref, ks_ref, vs_ref, kw_ref, vw_ref, selb_ref, oc_ref, sm_ref,
                   o_ref, m_sc, acc_sc):
    b = pl.program_id(0)
    g = pl.program_id(1)
    qb = pl.program_id(2)
    nq = pl.num_programs(2)
    rows = NSA_HPG * Q_TILE
    q4 = q_ref[0].reshape(rows, LANES)
    q_aug = jnp.concatenate([q4, jnp.concatenate([selb_ref[0, 0]] * NSA_HPG, axis=0)], axis=1)
    r = lax.broadcasted_iota(jnp.int32, (rows, 1), 0) % Q_TILE
    c = lax.broadcasted_iota(jnp.int32, (1, K_TILE), 1)
    rel = r - c
    diag = qb // (K_TILE // Q_TILE)

    def reset():
        m_sc[...] = jnp.full(m_sc.shape, NEG, F32)
        acc_sc[...] = jnp.zeros(acc_sc.shape, F32)

    def sel_tile(kt, causal):
        start = pl.multiple_of(kt * K_TILE, K_TILE)
        s = _dot_nt(q_aug, ks_ref[0, 0, pl.ds(start, K_TILE), :])
        if causal:
            s = jnp.where(rel + (qb * Q_TILE - kt * K_TILE) >= 0, s, NEG)
        _online_update(s, vs_ref[0, 0, pl.ds(start, K_TILE), :], m_sc, acc_sc)

    reset()
    word = flags_ref[(b * NSA_KV_GROUPS + g) * nq + qb]

    def body(kt, carry):
        @pl.when(((word >> kt) & 1) == 1)
        def _():
            sel_tile(kt, False)
        return carry

    lax.fori_loop(0, diag, body, 0)
    sel_tile(diag, True)
    o_sel = _normalized(acc_sc[...])

    span = WINDOW + K_TILE
    wstart = pl.multiple_of(jnp.maximum(diag - WINDOW // K_TILE, 0) * K_TILE, K_TILE)
    dist = (qb * Q_TILE + r) - (wstart + lax.broadcasted_iota(jnp.int32, (1, span), 1))
    s = _dot_nt(q4, kw_ref[0, 0, pl.ds(wstart, span), :])
    s = jnp.where((dist >= 0) & (dist < WINDOW), s, NEG)
    o_win = _attend_once(s, vw_ref[0, 0, pl.ds(wstart, span), :])

    sm = sm_ref[0]
    y = _gate_rows(sm, g, 1) * o_sel + _gate_rows(sm, g, 2) * o_win
    o_ref[0] = (oc_ref[0] + _head_tile(y)).astype(BF16)


def _selwin_attention(flag_words, qa, ksl, nkv, selb, ocg, sm):
    B, H, S, _ = qa.shape
    G = NSA_KV_GROUPS
    nq = S // Q_TILE
    rows = NSA_HPG * Q_TILE
    kv_spec = lambda piece: pl.BlockSpec((1, 1, S, LANES), lambda b, g, i, f: (b, piece + g, 0, 0))
    out_tile = pl.BlockSpec((1, Q_TILE, NSA_HPG * HEAD_DIM), lambda b, g, i, f: (b, i, g))
    grid_spec = pltpu.PrefetchScalarGridSpec(
        num_scalar_prefetch=1,
        grid=(B, G, nq),
        in_specs=[
            pl.BlockSpec((1, NSA_HPG, Q_TILE, LANES), lambda b, g, i, f: (b, g, i, 0)),
            pl.BlockSpec((1, 1, S, 2 * LANES), lambda b, g, i, f: (b, g, 0, 0)),
            kv_spec(0), kv_spec(2), kv_spec(4),
            pl.BlockSpec((1, 1, Q_TILE, LANES), lambda b, g, i, f: (b, g, i, 0)),
            out_tile,
            pl.BlockSpec((1, Q_TILE, LANES), lambda b, g, i, f: (b, i, 0)),
        ],
        out_specs=out_tile,
        scratch_shapes=[
            pltpu.VMEM((rows, LANES), F32),
            pltpu.VMEM((rows, LANES), F32),
        ],
    )
    return pl.pallas_call(
        _selwin_kernel,
        grid_spec=grid_spec,
        out_shape=jax.ShapeDtypeStruct((B, S, NSA_W), BF16),
        compiler_params=_cparams(("parallel", "parallel", "arbitrary")),
        name="selwin_attention",
    )(flag_words, qa, ksl, nkv, nkv, nkv, selb, ocg, sm)


def _fox_kernel(q_ref, k_ref, v_ref, o_ref, m_sc, acc_sc, *, tq):
    qi = pl.program_id(2)
    m_sc[...] = jnp.full(m_sc.shape, NEG, F32)
    acc_sc[...] = jnp.zeros(acc_sc.shape, F32)

    def tile(kt, width, causal):
        start = pl.multiple_of(kt * tq, tq)
        for hh in range(FOX_HPS):
            s = _dot_nt(q_ref[0, hh], k_ref[0, hh, pl.ds(start, width), :])
            if causal:
                r = lax.broadcasted_iota(jnp.int32, s.shape, 0)
                c = lax.broadcasted_iota(jnp.int32, s.shape, 1)
                s = jnp.where(r >= c, s, NEG)
            _online_update(s, v_ref[0, hh, pl.ds(start, width), :], m_sc.at[hh], acc_sc.at[hh])

    def body(kp, carry):
        tile(2 * kp, 2 * tq, False)
        return carry

    lax.fori_loop(0, qi // 2, body, 0)

    @pl.when(qi % 2 == 1)
    def _():
        tile(qi - 1, tq, False)

    tile(qi, tq, True)
    lane = lax.broadcasted_iota(jnp.int32, (tq, LANES), 1)
    o = [_normalized(acc_sc[hh]) for hh in range(FOX_HPS)]
    for pr in range(FOX_HPS // 2):
        o_ref[0, :, pr * LANES:(pr + 1) * LANES] = jnp.where(
            lane < HEAD_DIM, o[2 * pr], pltpu.roll(o[2 * pr + 1], HEAD_DIM, 1)).astype(BF16)


def _fox_attention(fq, fk, fv, tq=512):
    B, H, S, _ = fq.shape
    hps = FOX_HPS
    return pl.pallas_call(
        functools.partial(_fox_kernel, tq=tq),
        grid=(B, H // hps, S // tq),
        in_specs=[
            pl.BlockSpec((1, hps, tq, LANES), lambda b, h, i: (b, h, i, 0)),
            pl.BlockSpec((1, hps, S, LANES), lambda b, h, i: (b, h, 0, 0)),
            pl.BlockSpec((1, hps, S, LANES), lambda b, h, i: (b, h, 0, 0)),
        ],
        out_specs=pl.BlockSpec((1, tq, hps * HEAD_DIM), lambda b, h, i: (b, i, h)),
        out_shape=jax.ShapeDtypeStruct((B, S, FOX_W), BF16),
        scratch_shapes=[
            pltpu.VMEM((hps, tq, LANES), F32),
            pltpu.VMEM((hps, tq, LANES), F32),
        ],
        compiler_params=_cparams(("parallel", "parallel", "arbitrary")),
        name="fox_attention",
    )(fq, fk, fv)


def _merge_kernel(ya_ref, yb_ref, mg_ref, x_ref, mod_ref, gpost_ref, gpre_ref,
                  wa_ref, wb_ref, wo_ref, wrh_ref, wrl_ref, br_ref, stri_ref,
                  x1_ref, h2_ref, rt_ref, cnt_ref):
    D = D_MODEL

    @pl.when((pl.program_id(0) == 0) & (pl.program_id(1) == 0))
    def _():
        cnt_ref[...] = jnp.zeros(cnt_ref.shape, F32)

    a = _dot(ya_ref[0], wa_ref[...])
    bq = _dot(yb_ref[0], wb_ref[...])
    mg = mg_ref[0]
    u = mg[:, :D].astype(F32) * a + mg[:, D:].astype(F32) * bq
    mixed = _dot(u.astype(BF16), wo_ref[...])
    x1 = x_ref[0] + mod_ref[0, 2:3, :] * _rms(mixed, gpost_ref[...])
    x1_ref[0] = x1
    h2 = _rms(x1, gpre_ref[...]) * (1.0 + mod_ref[0, 4:5, :]) + mod_ref[0, 3:4, :]
    hi = h2.astype(BF16)
    lo = (h2 - hi.astype(F32)).astype(BF16)
    h2_ref[0] = h2
    lg = _dot(hi, wrh_ref[...]) + _dot(lo, wrh_ref[...]) + _dot(hi, wrl_ref[...]) + br_ref[...]

    lane = lax.broadcasted_iota(jnp.int32, lg.shape, 1)
    lanef = lane.astype(F32)
    no_lane = float(LANES)
    is_g = lane < N_EXPERT_GROUPS
    gl = jnp.where(is_g, lg, NEG)
    gmax = jnp.max(gl, axis=-1, keepdims=True)
    pg_top = 1.0 / jnp.sum(jnp.where(is_g, jnp.exp(gl - gmax), 0.0), axis=-1, keepdims=True)
    g_idx = jnp.min(jnp.where(is_g & (gl == gmax), lanef, no_lane), axis=-1, keepdims=True)
    in_grp = ((lane >= N_EXPERT_GROUPS) & (lane < N_EXPERT_GROUPS + N_EXPERTS)
              & (((lane - N_EXPERT_GROUPS) // EXPERTS_PER_GROUP).astype(F32) == g_idx))
    le = jnp.where(in_grp, lg, NEG)
    m1 = jnp.max(le, axis=-1, keepdims=True)
    i1 = jnp.min(jnp.where(in_grp & (le == m1), lanef, no_lane), axis=-1, keepdims=True)
    rest = in_grp & (lanef != i1)
    le2 = jnp.where(rest, lg, NEG)
    m2 = jnp.max(le2, axis=-1, keepdims=True)
    i2 = jnp.min(jnp.where(rest & (le2 == m2), lanef, no_lane), axis=-1, keepdims=True)
    e21 = jnp.exp(m2 - m1)
    w1 = pg_top / (1.0 + e21)
    w2 = w1 * e21
    pick1 = lanef == i1
    pick2 = lanef == i2
    onehot = jnp.where(pick1 | pick2, 1.0, 0.0)
    before = cnt_ref[...] + _dot(stri_ref[...], onehot.astype(BF16))
    rank1 = jnp.sum(jnp.where(pick1, before, 0.0), axis=-1, keepdims=True)
    rank2 = jnp.sum(jnp.where(pick2, before, 0.0), axis=-1, keepdims=True)
    cnt_ref[...] = cnt_ref[...] + jnp.sum(onehot, axis=0, keepdims=True)
    fields = [i1 - N_EXPERT_GROUPS, i2 - N_EXPERT_GROUPS, rank1, rank2, w1, w2]
    rt = jnp.zeros(lg.shape, F32)
    for k, f in enumerate(fields):
        rt = jnp.where(lane == k, f, rt)
    rt_ref[0] = rt


def _merge(ya, yb, mg, x, mod, gpost, gpre, wa, wb, wo, wrh, wrl, br, stri):
    B, S, D = x.shape
    tm = MERGE_TILE
    c2 = lambda b, i: (0, 0)
    row = lambda w: pl.BlockSpec((1, tm, w), lambda b, i: (b, i, 0))
    return pl.pallas_call(
        _merge_kernel,
        grid=(B, S // tm),
        in_specs=[
            row(NSA_W), row(FOX_W), row(2 * D), row(D),
            pl.BlockSpec((1, 6, D), lambda b, i: (b, 0, 0)),
            pl.BlockSpec((1, D), c2), pl.BlockSpec((1, D), c2),
            pl.BlockSpec((NSA_W, D), c2), pl.BlockSpec((FOX_W, D), c2), pl.BlockSpec((D, D), c2),
            pl.BlockSpec((D, LANES), c2), pl.BlockSpec((D, LANES), c2), pl.BlockSpec((1, LANES), c2),
            pl.BlockSpec((tm, tm), c2),
        ],
        out_specs=[row(D), row(D), row(LANES), pl.BlockSpec((1, LANES), c2)],
        out_shape=[
            jax.ShapeDtypeStruct((B, S, D), F32),
            jax.ShapeDtypeStruct((B, S, D), F32),
            jax.ShapeDtypeStruct((B, S, LANES), F32),
            jax.ShapeDtypeStruct((1, LANES), F32),
        ],
        compiler_params=_cparams(("arbitrary", "arbitrary")),
        name="merge",
    )(ya, yb, mg, x, mod, gpost, gpre, wa, wb, wo, wrh, wrl, br, stri)


def _expert_kernel(be_ref, na_ref, tok_ref, h_hbm, wg_ref, wu_ref, wd_ref, o_ref, x_even, x_odd, sem):
    i = pl.program_id(0)
    n_active = na_ref[0]
    last_block = pl.num_programs(0) - 1
    bufs = (x_even, x_odd)

    def row_copy(blk, r, sl):
        tok = tok_ref[blk * MOE_TILE + r]
        return pltpu.make_async_copy(h_hbm.at[pl.ds(tok, 1)], bufs[sl].at[pl.ds(r, 1)], sem.at[sl])

    def wait_rows(blk, sl):
        del blk
        pltpu.make_async_copy(h_hbm.at[pl.ds(0, MOE_TILE)], bufs[sl], sem.at[sl]).wait()

    @pl.when(i == 0)
    def _():
        def body(r, carry):
            row_copy(0, r, 0).start()
            return carry
        lax.fori_loop(0, MOE_TILE, body, 0, unroll=8)

    def step(sl):
        wait_rows(i, sl)
        nxt = jnp.minimum(i + 1, last_block)
        for r in range(MOE_TILE):
            row_copy(nxt, r, 1 - sl).start()
        x = bufs[sl][...].astype(BF16)
        gate = _dot(x, wg_ref[0])
        up = _dot(x, wu_ref[0])
        mid = (gate * jax.nn.sigmoid(gate) * up).astype(BF16)
        o_ref[...] = _dot(mid, wd_ref[0])

        @pl.when(i == n_active - 1)
        def _():
            wait_rows(nxt, 1 - sl)

    for sl in range(2):
        pl.when((i % 2 == sl) & (i < n_active))(functools.partial(step, sl))

    @pl.when(i >= n_active)
    def _():
        o_ref[...] = jnp.zeros(o_ref.shape, o_ref.dtype)


def _experts(block_expert, n_active, buf_tok, h2, wg, wu, wd):
    cap = buf_tok.shape[0]
    D = D_MODEL
    nblk = cap // MOE_TILE
    grid_spec = pltpu.PrefetchScalarGridSpec(
        num_scalar_prefetch=3,
        grid=(nblk,),
        in_specs=[
            pl.BlockSpec(memory_space=pl.ANY),
            pl.BlockSpec((1, D, D_EXPERT), lambda i, be, na, tok: (be[i], 0, 0)),
            pl.BlockSpec((1, D, D_EXPERT), lambda i, be, na, tok: (be[i], 0, 0)),
            pl.BlockSpec((1, D_EXPERT, D), lambda i, be, na, tok: (be[i], 0, 0)),
        ],
        out_specs=pl.BlockSpec((MOE_TILE, D), lambda i, be, na, tok: (i, 0)),
        scratch_shapes=[
            pltpu.VMEM((MOE_TILE, D), F32),
            pltpu.VMEM((MOE_TILE, D), F32),
            pltpu.SemaphoreType.DMA((2,)),
        ],
    )
    return pl.pallas_call(
        _expert_kernel,
        grid_spec=grid_spec,
        out_shape=jax.ShapeDtypeStruct((cap, D), F32),
        compiler_params=_cparams(("arbitrary",)),
        name="experts",
    )(block_expert, n_active, buf_tok, h2, wg, wu, wd)


def _final_kernel(dest_ref, x1_ref, rt_ref, mod_ref, g_ref, y_hbm, o_ref, a_even, b_even, a_odd, b_odd, sem):
    j = pl.program_id(0)
    last_tile = pl.num_programs(0) - 1
    tm = o_ref.shape[0]
    bufs = ((a_even, b_even), (a_odd, b_odd))

    def row_copy(tile, r, k, sl):
        row = dest_ref[(tile * tm + r) * EXPERT_TOP_K + k]
        return pltpu.make_async_copy(y_hbm.at[pl.ds(row, 1)], bufs[sl][k].at[pl.ds(r, 1)], sem.at[sl])

    def wait_rows(tile, sl):
        del tile
        for k in range(EXPERT_TOP_K):
            pltpu.make_async_copy(y_hbm.at[pl.ds(0, tm)], bufs[sl][k], sem.at[sl]).wait()

    @pl.when(j == 0)
    def _():
        def body(r, carry):
            for k in range(EXPERT_TOP_K):
                row_copy(0, r, k, 0).start()
            return carry
        lax.fori_loop(0, tm, body, 0, unroll=4)

    def step(sl):
        wait_rows(j, sl)
        nxt = jnp.minimum(j + 1, last_tile)
        for r in range(tm):
            for k in range(EXPERT_TOP_K):
                row_copy(nxt, r, k, 1 - sl).start()
        rt = rt_ref[...]
        lane = lax.broadcasted_iota(jnp.int32, rt.shape, 1)
        w0 = jnp.sum(jnp.where(lane == 4, rt, 0.0), axis=-1, keepdims=True)
        w1 = jnp.sum(jnp.where(lane == 5, rt, 0.0), axis=-1, keepdims=True)
        y = w0 * bufs[sl][0][...] + w1 * bufs[sl][1][...]
        o_ref[...] = x1_ref[...] + mod_ref[0, 5:6, :] * _rms(y, g_ref[...])

        @pl.when(j == last_tile)
        def _():
            wait_rows(nxt, 1 - sl)

    for sl in range(2):
        pl.when(j % 2 == sl)(functools.partial(step, sl))


def _final(dest, x1, rt, mod, g, yb, tiles_per_batch):
    T, D = x1.shape
    tm = MERGE_TILE
    grid_spec = pltpu.PrefetchScalarGridSpec(
        num_scalar_prefetch=1,
        grid=(T // tm,),
        in_specs=[
            pl.BlockSpec((tm, D), lambda j, d: (j, 0)),
            pl.BlockSpec((tm, LANES), lambda j, d: (j, 0)),
            pl.BlockSpec((1, 6, D), lambda j, d: (j // tiles_per_batch, 0, 0)),
            pl.BlockSpec((1, D), lambda j, d: (0, 0)),
            pl.BlockSpec(memory_space=pl.ANY),
        ],
        out_specs=pl.BlockSpec((tm, D), lambda j, d: (j, 0)),
        scratch_shapes=[pltpu.VMEM((tm, D), F32)] * 4 + [pltpu.SemaphoreType.DMA((2,))],
    )
    return pl.pallas_call(
        _final_kernel,
        grid_spec=grid_spec,
        out_shape=jax.ShapeDtypeStruct((T, D), F32),
        compiler_params=_cparams(("arbitrary",)),
        name="final",
    )(dest, x1, rt, mod, g, yb)


def _overlap_matrix():
    n = np.arange(N_CMP_PAD)[:, None]
    j = np.arange(LANES)[None, :]
    start = n * CMP_STRIDE
    ov = (start < j * SEL_LEN + SEL_LEN) & (start + CMP_LEN - 1 >= j * SEL_LEN) & (n < N_CMP_PAD - 1)
    return jnp.asarray(ov.T.astype(np.float32), dtype=BF16)


def _pad_cols(w, width=LANES):
    return jnp.pad(w, ((0, 0), (0, width - w.shape[1])))


def _dispatch_plan(rt, cnt, T):
    expert = rt[:, 0:2].astype(jnp.int32)
    rank = rt[:, 2:4].astype(jnp.int32)
    weight = rt[:, 4:6]
    counts = cnt[0, N_EXPERT_GROUPS:N_EXPERT_GROUPS + N_EXPERTS].astype(jnp.int32)
    padded = (counts + MOE_TILE - 1) // MOE_TILE * MOE_TILE
    pad_end = jnp.cumsum(padded)
    pad_start = pad_end - padded
    onehot = expert[:, :, None] == jnp.arange(N_EXPERTS)[None, None, :]
    dest = jnp.sum(jnp.where(onehot, pad_start[None, None, :], 0), axis=-1) + rank
    A = T * EXPERT_TOP_K
    cap = -(-(A + N_EXPERTS * (MOE_TILE - 1)) // MOE_TILE) * MOE_TILE
    nblk = cap // MOE_TILE
    n_active = (pad_end[-1] // MOE_TILE).astype(jnp.int32)
    blk = jnp.arange(nblk) * MOE_TILE
    block_expert = jnp.minimum(jnp.sum(pad_end[None, :] <= blk[:, None], axis=1), N_EXPERTS - 1)
    last = jnp.max(jnp.where(jnp.arange(nblk) < n_active, block_expert, 0))
    block_expert = jnp.where(jnp.arange(nblk) < n_active, block_expert, last).astype(jnp.int32)
    tok = jnp.arange(A, dtype=jnp.int32) // EXPERT_TOP_K
    buf_tok = jnp.zeros((cap,), jnp.int32).at[dest.reshape(A)].set(tok)
    return weight, dest, buf_tok, block_expert, n_active.reshape(1)


def kernel(x, c, w_ada, b_ada, g_pre_mix, g_post_mix, g_pre_ffn, g_post_ffn, w_in, b_forget,
           cmp_pe_k, cmp_w1_k, cmp_w2_k, cmp_pe_v, cmp_w1_v, cmp_w2_v,
           w_o_nsa, w_o_fox, w_out, w_router_group, b_router_group, w_router_expert, b_router_expert,
           w_exp_gate, w_exp_up, w_exp_down):
    B, S, D = x.shape
    T = B * S
    depth = w_ada.shape[0]
    ov = _overlap_matrix()
    tri = jnp.asarray(np.tril(np.ones((IN_TILE, IN_TILE), np.float32)), dtype=BF16)
    stri = jnp.asarray(np.tril(np.ones((MERGE_TILE, MERGE_TILE), np.float32), -1), dtype=BF16)
    row_feat = _row_features(S)
    placement = _placement()
    cmp_ext = _cmp_key_ext()
    for l in range(depth):
        mod = (jax.nn.silu(c) @ w_ada[l] + b_ada[l]).reshape(B, 6, D)
        w_qa, w_kva, w_gl, w_fox, w_f, w_mg = jnp.split(w_in[l], IN_SPLITS, axis=-1)
        w_big = jnp.concatenate([w_qa, w_kva, w_fox, w_mg], axis=1).astype(BF16)
        w_small = _pad_cols(jnp.concatenate([w_gl, w_f], axis=1)).astype(BF16)
        bf_pad = jnp.pad(b_forget[l], (F_LANE, LANES - F_LANE - FOX_HEADS)).reshape(1, LANES)
        qa, ckv, ksl, nkv, fq, fk, fv, mg, sm = _inproj(
            x, mod, g_pre_mix[l].reshape(1, D), w_big, w_small, bf_pad, tri, row_feat, placement)

        half = CMP_LEN // 2
        pe = jnp.stack([cmp_pe_k[l], cmp_pe_v[l]]).reshape(2, 2, 1, half * HEAD_DIM)
        w1 = jnp.stack([cmp_w1_k[l], cmp_w1_v[l]]).reshape(2, 2, half * HEAD_DIM, HEAD_DIM).astype(BF16)
        w2 = jnp.pad(jnp.stack([cmp_w2_k[l], cmp_w2_v[l]]), ((0, 0), (0, 0), (0, LANES - HEAD_DIM))).astype(BF16)
        kvc = _compress(ckv.reshape(B, 4, S // CMP_STRIDE, CMP_STRIDE * HEAD_DIM), pe, w1, w2, cmp_ext)
        ocg, selb, flags = _cmp_attention(qa, kvc, sm, ov)
        nq = S // Q_TILE
        per_tile = K_TILE // SEL_LEN
        tile_any = jnp.max(flags.reshape(B, NSA_KV_GROUPS, nq, LANES // per_tile, per_tile), axis=-1)
        bits = tile_any.astype(jnp.uint32) << jnp.arange(LANES // per_tile, dtype=jnp.uint32)
        flag_words = lax.bitcast_convert_type(jnp.sum(bits, axis=-1, dtype=jnp.uint32), jnp.int32).reshape(-1)
        y_a = _selwin_attention(flag_words, qa, ksl, nkv, selb, ocg, sm)

        y_b = _fox_attention(fq, fk, fv)

        w_r = _pad_cols(jnp.concatenate([w_router_group[l], w_router_expert[l]], axis=1))
        w_rh = w_r.astype(BF16)
        w_rl = (w_r - w_rh.astype(F32)).astype(BF16)
        b_r = _pad_cols(jnp.concatenate([b_router_group[l], b_router_expert[l]]).reshape(1, -1))
        x1, h2, rt, cnt = _merge(y_a, y_b, mg, x, mod, g_post_mix[l].reshape(1, D), g_pre_ffn[l].reshape(1, D),
                                 w_o_nsa[l].astype(BF16), w_o_fox[l].astype(BF16), w_out[l].astype(BF16),
                                 w_rh, w_rl, b_r, stri)

        weight, dest, buf_tok, block_expert, n_active = _dispatch_plan(rt.reshape(T, LANES), cnt, T)
        yb = _experts(block_expert, n_active, buf_tok, h2.reshape(T, D), w_exp_gate[l].astype(BF16),
                      w_exp_up[l].astype(BF16), w_exp_down[l].astype(BF16))
        x = _final(dest.reshape(T * EXPERT_TOP_K), x1.reshape(T, D), rt.reshape(T, LANES), mod,
                   g_post_ffn[l].reshape(1, D), yb, S // MERGE_TILE).reshape(B, S, D)
    return x
```

```python
import functools

import ml_dtypes
import numpy as np
import jax
import jax.numpy as jnp
from jax import lax
from jax.experimental import pallas as pl
from jax.experimental.pallas import tpu as pltpu

D_MODEL = 1024
HEAD_DIM = 64
NSA_HEADS = 8
NSA_KV_GROUPS = 2
NSA_HPG = NSA_HEADS // NSA_KV_GROUPS
FOX_HEADS = 8
CMP_LEN = 32
CMP_STRIDE = 16
SEL_LEN = 64
N_SEL = 16
WINDOW = 512
N_EXPERT_GROUPS = 4
EXPERTS_PER_GROUP = 8
N_EXPERTS = N_EXPERT_GROUPS * EXPERTS_PER_GROUP
EXPERT_TOP_K = 2
D_EXPERT = D_MODEL // 2
NORM_EPS = 1e-6
NEG = -1e30
FORCE = 1e9
LOG2E = 1.4426950408889634

NSA_W = NSA_HEADS * HEAD_DIM
NSA_KV_W = NSA_KV_GROUPS * HEAD_DIM
FOX_W = FOX_HEADS * HEAD_DIM
IN_SIZES = (NSA_W, 6 * NSA_KV_W, 3 * NSA_HEADS, 3 * FOX_W, FOX_HEADS, 2 * D_MODEL)
IN_SPLITS = tuple(int(v) for v in np.cumsum(IN_SIZES)[:-1])

LANES = 128
Q_TILE = 128
K_TILE = 256
N_CMP_PAD = 512
MOE_TILE = 256
IN_TILE = 512
MERGE_TILE = 256
FOX_HPS = 4
CMP_SUB = 2
MAX_TILES = 32
VMEM_LIMIT = 56 * 1024 * 1024

F_LANE = 3 * NSA_HEADS
U_LANE = 64
ONE_LANE = 88
A_LANE = 89
B_LANE = 90
EXT = HEAD_DIM
G_FQ, G_FK, G_NQ, G_NK, N_GROUPS = 0, 8, 16, 24, 25

F32 = jnp.float32
BF16 = jnp.bfloat16


def _dot(a, b):
    return jnp.dot(a, b, preferred_element_type=F32)


def _dot_nt(a, b):
    return lax.dot_general(a, b, (((1,), (1,)), ((), ())), preferred_element_type=F32)


def _rms(x, g):
    return x * lax.rsqrt(jnp.mean(x * x, axis=-1, keepdims=True) + NORM_EPS) * g


def _cparams(sem):
    return pltpu.CompilerParams(dimension_semantics=sem, vmem_limit_bytes=VMEM_LIMIT)


def _split3(x):
    hi = x.astype(BF16).astype(F32)
    r = x - hi
    mid = r.astype(BF16).astype(F32)
    lo = (r - mid).astype(BF16).astype(F32)
    return hi, mid, lo


def _np_split3(x):
    x = np.asarray(x, np.float32)
    hi = x.astype(ml_dtypes.bfloat16).astype(np.float32)
    r = x - hi
    mid = r.astype(ml_dtypes.bfloat16).astype(np.float32)
    lo = (r - mid).astype(ml_dtypes.bfloat16).astype(np.float32)
    return hi, mid, lo


def _alibi_c():
    slopes = np.exp2(-8.0 * np.arange(1, NSA_HEADS + 1, dtype=np.float32) / NSA_HEADS).astype(np.float32)
    return slopes * np.float32(LOG2E)


def _row_features(S):
    t = np.arange(S, dtype=np.float32)
    c = _alibi_c()
    rs = np.zeros((S, LANES), np.float32)
    for h in range(NSA_HEADS):
        for j, term in enumerate(_np_split3(c[h] * t)):
            rs[:, U_LANE + 8 * j + h] = -term
    rs[:, ONE_LANE] = 1.0
    rs[:, A_LANE] = np.floor(t / LANES)
    rs[:, B_LANE] = t % LANES
    return jnp.asarray(rs, dtype=BF16)


def _placement():
    c = _alibi_c()
    p = np.zeros((LANES, N_GROUPS * LANES), np.float32)
    for h in range(FOX_HEADS):
        q0 = (G_FQ + h) * LANES + EXT
        k0 = (G_FK + h) * LANES + EXT
        for j in range(3):
            p[ONE_LANE, q0 + j] = -1.0
            p[F_LANE + 8 * j + h, q0 + 3 + j] = 1.0
            p[F_LANE + 8 * j + h, k0 + j] = 1.0
            p[ONE_LANE, k0 + 3 + j] = 1.0
    for h in range(NSA_HEADS):
        q0 = (G_NQ + h) * LANES + EXT
        c128 = _np_split3(c[h] * np.float32(LANES))
        c1 = _np_split3(c[h])
        for j in range(3):
            p[U_LANE + 8 * j + h, q0 + j] = 1.0
            p[ONE_LANE, q0 + 3 + j] = c128[j]
            p[ONE_LANE, q0 + 6 + j] = c1[j]
    k0 = G_NK * LANES + EXT
    for j in range(3):
        p[ONE_LANE, k0 + j] = 1.0
        p[A_LANE, k0 + 3 + j] = 1.0
        p[B_LANE, k0 + 6 + j] = 1.0
    return jnp.asarray(p, dtype=BF16)


def _cmp_key_ext():
    pos = np.arange(N_CMP_PAD, dtype=np.float32) * CMP_STRIDE + (CMP_LEN - 1)
    e = np.zeros((2, N_CMP_PAD, LANES), np.float32)
    for j in range(3):
        e[0, :, EXT + j] = 1.0
        e[0, :, EXT + 3 + j] = np.floor(pos / LANES)
        e[0, :, EXT + 6 + j] = pos % LANES
    return jnp.asarray(e, dtype=BF16)


def _inproj_kernel(x_ref, mod_ref, g_ref, wb_ref, ws_ref, bf_ref, tri_ref, rs_ref, p_ref,
                   qa_ref, ckv_ref, ksl_ref, nkv_ref, fq_ref, fk_ref, fv_ref, mg_ref, sm_ref, carry_sc):
    i = pl.program_id(1)
    tm = x_ref.shape[1]
    x = x_ref[0]
    h = _rms(x, g_ref[...]) * (1.0 + mod_ref[0, 1:2, :]) + mod_ref[0, 0:1, :]
    hb = h.astype(BF16)
    lane = lax.broadcasted_iota(jnp.int32, (tm, LANES), 1)
    lower = lane < HEAD_DIM
    ones_col = (lane == EXT).astype(F32)

    z = _dot(hb, ws_ref[...]) + bf_ref[...]
    logsig = jnp.minimum(z, 0.0) - jnp.log1p(jnp.exp(-jnp.abs(z)))
    sm_ref[0] = jnp.where(lane < F_LANE, jax.nn.sigmoid(z), logsig)

    @pl.when(i == 0)
    def _():
        carry_sc[...] = jnp.zeros(carry_sc.shape, F32)

    is_f = (lane >= F_LANE) & (lane < F_LANE + FOX_HEADS)
    l_hi, l_mid, l_lo = _split3(jnp.where(is_f, logsig, 0.0))
    tri = tri_ref[...]
    cum = carry_sc[...] + _dot(tri, l_hi.astype(BF16)) + _dot(tri, l_mid.astype(BF16)) + _dot(tri, l_lo.astype(BF16))
    carry_sc[...] = cum[tm - 1:tm, :]
    f_hi, f_mid, f_lo = _split3(cum * LOG2E)
    feat = (f_hi + pltpu.roll(f_mid, 8, 1) + pltpu.roll(f_lo, 16, 1) + rs_ref[...].astype(F32)).astype(BF16)

    def ext(group):
        return _dot(feat, p_ref[:, group * LANES:(group + 1) * LANES])

    def piece(acc, idx, extra):
        pair = acc[:, (idx // 2) * LANES:(idx // 2 + 1) * LANES]
        if idx % 2:
            pair = pltpu.roll(pair, HEAD_DIM, 1)
        return jnp.where(lower, pair, extra).astype(BF16)

    qscale = (HEAD_DIM ** -0.5) * LOG2E
    acc = _dot(hb, wb_ref[:, 0:NSA_W]) * qscale
    for hd in range(NSA_HEADS):
        qa_ref[0, hd] = piece(acc, hd, ext(G_NQ + hd))
    off = NSA_W
    acc = _dot(hb, wb_ref[:, off:off + 6 * NSA_KV_W])
    for pc in range(4):
        ckv_ref[0, pc] = acc[:, pc * HEAD_DIM:(pc + 1) * HEAD_DIM].astype(BF16)
    ext_k = ext(G_NK)
    t = i * tm + lax.broadcasted_iota(jnp.int32, (tm, LANES), 0)
    block_onehot = (lane == t // SEL_LEN).astype(BF16)
    for g in range(NSA_KV_GROUPS):
        ksl_ref[0, g, :, 0:LANES] = piece(acc, 4 + g, ext_k)
        ksl_ref[0, g, :, LANES:2 * LANES] = block_onehot
        nkv_ref[0, g] = piece(acc, 6 + g, ones_col)
        nkv_ref[0, 2 + g] = piece(acc, 8 + g, ext_k)
        nkv_ref[0, 4 + g] = piece(acc, 10 + g, ones_col)
    off += 6 * NSA_KV_W
    acc = _dot(hb, wb_ref[:, off:off + FOX_W]) * qscale
    for hd in range(FOX_HEADS):
        fq_ref[0, hd] = piece(acc, hd, ext(G_FQ + hd))
    off += FOX_W
    acc = _dot(hb, wb_ref[:, off:off + FOX_W])
    for hd in range(FOX_HEADS):
        fk_ref[0, hd] = piece(acc, hd, ext(G_FK + hd))
    off += FOX_W
    acc = _dot(hb, wb_ref[:, off:off + FOX_W])
    for hd in range(FOX_HEADS):
        fv_ref[0, hd] = piece(acc, hd, ones_col)
    off += FOX_W
    for c in range(4):
        acc = _dot(hb, wb_ref[:, off + c * 512: off + (c + 1) * 512])
        mg_ref[0, :, c * 512:(c + 1) * 512] = jax.nn.sigmoid(acc).astype(BF16)


def _inproj(x, mod, g, wb, ws, bfp, tri, rs, pm):
    B, S, D = x.shape
    tm = IN_TILE
    nb = wb.shape[1]
    const2 = lambda b, i: (0, 0)
    heads = lambda n: pl.BlockSpec((1, n, tm, LANES), lambda b, i: (b, 0, i, 0))
    hshape = lambda n: jax.ShapeDtypeStruct((B, n, S, LANES), BF16)
    return pl.pallas_call(
        _inproj_kernel,
        grid=(B, S // tm),
        in_specs=[
            pl.BlockSpec((1, tm, D), lambda b, i: (b, i, 0)),
            pl.BlockSpec((1, 6, D), lambda b, i: (b, 0, 0)),
            pl.BlockSpec((1, D), const2),
            pl.BlockSpec((D, nb), const2),
            pl.BlockSpec((D, LANES), const2),
            pl.BlockSpec((1, LANES), const2),
            pl.BlockSpec((tm, tm), const2),
            pl.BlockSpec((tm, LANES), lambda b, i: (i, 0)),
            pl.BlockSpec((LANES, N_GROUPS * LANES), const2),
        ],
        out_specs=[
            heads(NSA_HEADS),
            pl.BlockSpec((1, 4, tm, HEAD_DIM), lambda b, i: (b, 0, i, 0)),
            pl.BlockSpec((1, NSA_KV_GROUPS, tm, 2 * LANES), lambda b, i: (b, 0, i, 0)),
            heads(6), heads(FOX_HEADS), heads(FOX_HEADS), heads(FOX_HEADS),
            pl.BlockSpec((1, tm, 2 * D), lambda b, i: (b, i, 0)),
            pl.BlockSpec((1, tm, LANES), lambda b, i: (b, i, 0)),
        ],
        out_shape=[
            hshape(NSA_HEADS),
            jax.ShapeDtypeStruct((B, 4, S, HEAD_DIM), BF16),
            jax.ShapeDtypeStruct((B, NSA_KV_GROUPS, S, 2 * LANES), BF16),
            hshape(6), hshape(FOX_HEADS), hshape(FOX_HEADS), hshape(FOX_HEADS),
            jax.ShapeDtypeStruct((B, S, 2 * D), BF16),
            jax.ShapeDtypeStruct((B, S, LANES), F32),
        ],
        scratch_shapes=[pltpu.VMEM((1, LANES), F32)],
        compiler_params=_cparams(("parallel", "arbitrary")),
        name="inproj",
    )(x, mod, g, wb, ws, bfp, tri, rs, pm)


def _compress_kernel(x_ref, pe_ref, w1_ref, w2_ref, e_ref, o_ref):
    x = x_ref[0, 0].astype(F32)
    x_lo = (x + pe_ref[0, 0]).astype(BF16)
    x_hi = (x + pe_ref[0, 1]).astype(BF16)
    y_lo = _dot(x_lo, w1_ref[0, 0])
    y_hi = _dot(x_hi, w1_ref[0, 1])
    n = y_hi.shape[0]
    hid = y_lo + pltpu.roll(y_hi, n - 1, 0)
    hid = jax.nn.gelu(hid)
    o_ref[0, 0] = (_dot(hid.astype(BF16), w2_ref[0]) + e_ref[0].astype(F32)).astype(BF16)


def _compress(kv_rows, pe, w1, w2, e):
    B = kv_rows.shape[0]
    R, C = kv_rows.shape[2], kv_rows.shape[3]
    return pl.pallas_call(
        _compress_kernel,
        grid=(B, 4),
        in_specs=[
            pl.BlockSpec((1, 1, R, C), lambda b, p: (b, p, 0, 0)),
            pl.BlockSpec((1, 2, 1, C), lambda b, p: (p // 2, 0, 0, 0)),
            pl.BlockSpec((1, 2, C, HEAD_DIM), lambda b, p: (p // 2, 0, 0, 0)),
            pl.BlockSpec((1, HEAD_DIM, LANES), lambda b, p: (p // 2, 0, 0)),
            pl.BlockSpec((1, R, LANES), lambda b, p: (p // 2, 0, 0)),
        ],
        out_specs=pl.BlockSpec((1, 1, R, LANES), lambda b, p: (b, p, 0, 0)),
        out_shape=jax.ShapeDtypeStruct((B, 4, R, LANES), BF16),
        compiler_params=_cparams(("parallel", "parallel")),
        name="compress",
    )(kv_rows, pe, w1, w2, e)


def _gate_rows(sm, g, branch):
    col = lax.broadcasted_iota(jnp.int32, sm.shape, 1)
    parts = []
    for hl in range(NSA_HPG):
        want = 3 * (NSA_HPG * g + hl) + branch
        parts.append(jnp.sum(jnp.where(col == want, sm, 0.0), axis=-1, keepdims=True))
    return jnp.concatenate(parts, axis=0)


def _head_tile(y):
    lane = lax.broadcasted_iota(jnp.int32, (Q_TILE, LANES), 1)
    hs = [y[i * Q_TILE:(i + 1) * Q_TILE] for i in range(NSA_HPG)]
    pairs = [jnp.where(lane < HEAD_DIM, hs[2 * i], pltpu.roll(hs[2 * i + 1], HEAD_DIM, 1)) for i in range(2)]
    return jnp.concatenate(pairs, axis=1)


def _cmp_kernel(q_ref, kc_ref, vc_ref, sm_ref, ovt_ref, oc_ref, selb_ref, flag_ref):
    g = pl.program_id(1)
    for sub in range(CMP_SUB):
        rows = pl.ds(sub * Q_TILE, Q_TILE)
        q0 = (pl.program_id(2) * CMP_SUB + sub) * Q_TILE
        q = q_ref[0, :, rows, :].reshape(NSA_HPG * Q_TILE, LANES)
        oc, selb, flag = _cmp_tile(q, kc_ref[0, 0], vc_ref[0, 0], sm_ref[0, rows, :], ovt_ref[...], g, q0)
        oc_ref[0, rows, :] = oc
        selb_ref[0, 0, rows, :] = selb
        flag_ref[0, 0, sub] = flag


def _cmp_tile(q, kc, vc, sm, ovt, g, q0):
    s = _dot_nt(q, kc)
    r = lax.broadcasted_iota(jnp.int32, (NSA_HPG * Q_TILE, 1), 0) % Q_TILE
    n = lax.broadcasted_iota(jnp.int32, (1, N_CMP_PAD), 1)
    dc = (q0 + r) - (n * CMP_STRIDE + (CMP_LEN - 1))
    mask = (dc >= 0) & (n < N_CMP_PAD - 1)
    l = jnp.where(mask, s, NEG)
    m = jnp.max(l, axis=-1, keepdims=True)
    e = jnp.where(mask, jnp.exp2(l - m), 0.0)
    pc = e / jnp.maximum(jnp.sum(e, axis=-1, keepdims=True), 1e-30)
    oc = _dot(pc.astype(BF16), vc)
    oc = _head_tile(oc * _gate_rows(sm, g, 0))
    ps = pc[0:Q_TILE]
    for i in range(1, NSA_HPG):
        ps = ps + pc[i * Q_TILE:(i + 1) * Q_TILE]
    ps_hi = ps.astype(BF16)
    ps_lo = (ps - ps_hi.astype(F32)).astype(BF16)
    imp = _dot_nt(ovt, ps_hi) + _dot_nt(ovt, ps_lo)
    j = lax.broadcasted_iota(jnp.int32, imp.shape, 0)
    jf = j.astype(F32)
    t = q0 + lax.broadcasted_iota(jnp.int32, (1, Q_TILE), 1)
    cur = t // SEL_LEN
    forced = (j == 0) | (j == cur) | (j == cur - 1)
    v = jnp.where(j > cur, -FORCE, jnp.where(forced, FORCE, imp))
    sel = jnp.zeros(imp.shape, jnp.bool_)
    for _ in range(N_SEL):
        mx = jnp.max(v, axis=0, keepdims=True)
        idx = jnp.min(jnp.where(v == mx, jf, float(LANES)), axis=0, keepdims=True)
        pick = jf == idx
        sel = sel | pick
        v = jnp.where(pick, -3e38, v)
    live_t = jnp.where(sel & (j <= cur), 1.0, 0.0).astype(BF16)
    eye = (lax.broadcasted_iota(jnp.int32, imp.shape, 0) == lax.broadcasted_iota(jnp.int32, imp.shape, 1))
    live = _dot_nt(eye.astype(BF16), live_t)
    selb = jnp.where(live > 0.5, 0.0, NEG).astype(BF16)
    return oc, selb, jnp.max(live, axis=0, keepdims=True).astype(jnp.int32)


def _cmp_attention(qa, kvc, sm, ov):
    B, H, S, _ = qa.shape
    G = NSA_KV_GROUPS
    nq = S // Q_TILE
    qt = CMP_SUB * Q_TILE
    return pl.pallas_call(
        _cmp_kernel,
        grid=(B, G, nq // CMP_SUB),
        in_specs=[
            pl.BlockSpec((1, NSA_HPG, qt, LANES), lambda b, g, i: (b, g, i, 0)),
            pl.BlockSpec((1, 1, N_CMP_PAD, LANES), lambda b, g, i: (b, g, 0, 0)),
            pl.BlockSpec((1, 1, N_CMP_PAD, LANES), lambda b, g, i: (b, 2 + g, 0, 0)),
            pl.BlockSpec((1, qt, LANES), lambda b, g, i: (b, i, 0)),
            pl.BlockSpec((LANES, N_CMP_PAD), lambda b, g, i: (0, 0)),
        ],
        out_specs=[
            pl.BlockSpec((1, qt, NSA_HPG * HEAD_DIM), lambda b, g, i: (b, i, g)),
            pl.BlockSpec((1, 1, qt, LANES), lambda b, g, i: (b, g, i, 0)),
            pl.BlockSpec((1, 1, CMP_SUB, 1, LANES), lambda b, g, i: (b, g, i, 0, 0)),
        ],
        out_shape=[
            jax.ShapeDtypeStruct((B, S, NSA_W), F32),
            jax.ShapeDtypeStruct((B, G, S, LANES), BF16),
            jax.ShapeDtypeStruct((B, G, nq, 1, LANES), jnp.int32),
        ],
        compiler_params=_cparams(("parallel", "parallel", "parallel")),
        name="cmp_attention",
    )(qa, kvc, kvc, sm, ov)


def _online_update(s, v, m_ref, acc_ref):
    m_old = m_ref[...]
    m_new = jnp.maximum(m_old, jnp.max(s, axis=-1, keepdims=True))
    chunks = [s[:, c * LANES:(c + 1) * LANES] - m_new for c in range(s.shape[1] // LANES)]
    p = jnp.exp2(jnp.concatenate(chunks, axis=1))
    acc_ref[...] = jnp.exp2(m_old - m_new) * acc_ref[...] + _dot(p.astype(BF16), v)
    m_ref[...] = m_new


def _normalized(acc):
    return acc / jnp.maximum(acc[:, EXT:EXT + 1], 1e-30)


def _attend_once(s, v):
    m = jnp.broadcast_to(jnp.max(s, axis=-1, keepdims=True), (s.shape[0], LANES))
    chunks = [s[:, c * LANES:(c + 1) * LANES] - m for c in range(s.shape[1] // LANES)]
    p = jnp.exp2(jnp.concatenate(chunks, axis=1))
    return _normalized(_dot(p.astype(BF16), v))


def _selwin_kernel(list_ref, cnt_ref, q_ref, ks_ref, vs_ref, kw_ref, vw_ref, selb_ref, oc_ref, sm_ref,
                   o_ref, m_a, acc_a, m_b, acc_b):
    b = pl.program_id(0)
    g = pl.program_id(1)
    qb = pl.program_id(2)
    nq = pl.num_programs(2)
    rows = NSA_HPG * Q_TILE
    q4 = q_ref[0].reshape(rows, LANES)
    q_aug = jnp.concatenate([q4, jnp.concatenate([selb_ref[0, 0]] * NSA_HPG, axis=0)], axis=1)
    r = lax.broadcasted_iota(jnp.int32, (rows, 1), 0) % Q_TILE
    c = lax.broadcasted_iota(jnp.int32, (1, K_TILE), 1)
    rel = r - c
    diag = qb // (K_TILE // Q_TILE)

    def sel_tile(kt, m_ref, acc_ref, causal=False, bias=None):
        start = pl.multiple_of(kt * K_TILE, K_TILE)
        s = _dot_nt(q_aug, ks_ref[0, 0, pl.ds(start, K_TILE), :])
        if bias is not None:
            s = s + bias
        if causal:
            s = jnp.where(rel + (qb * Q_TILE - kt * K_TILE) >= 0, s, NEG)
        _online_update(s, vs_ref[0, 0, pl.ds(start, K_TILE), :], m_ref, acc_ref)

    for m_ref, acc_ref in ((m_a, acc_a), (m_b, acc_b)):
        m_ref[...] = jnp.full(m_ref.shape, NEG, F32)
        acc_ref[...] = jnp.zeros(acc_ref.shape, F32)
    step = (b * NSA_KV_GROUPS + g) * nq + qb
    count = cnt_ref[step]
    base = step * MAX_TILES

    def body(p, carry):
        second = 2 * p + 1
        sel_tile(list_ref[base + 2 * p], m_a, acc_a)
        sel_tile(list_ref[base + jnp.minimum(second, MAX_TILES - 1)], m_b, acc_b,
                 bias=jnp.where(second < count, 0.0, NEG))
        return carry

    lax.fori_loop(0, (count + 1) // 2, body, 0)
    m_new = jnp.maximum(m_a[...], m_b[...])
    acc_a[...] = jnp.exp2(m_a[...] - m_new) * acc_a[...] + jnp.exp2(m_b[...] - m_new) * acc_b[...]
    m_a[...] = m_new
    sel_tile(diag, m_a, acc_a, causal=True)
    o_sel = _normalized(acc_a[...])

    span = WINDOW + K_TILE
    wstart = pl.multiple_of(jnp.maximum(diag - WINDOW // K_TILE, 0) * K_TILE, K_TILE)
    dist = (qb * Q_TILE + r) - (wstart + lax.broadcasted_iota(jnp.int32, (1, span), 1))
    s = _dot_nt(q4, kw_ref[0, 0, pl.ds(wstart, span), :])
    s = jnp.where((dist >= 0) & (dist < WINDOW), s, NEG)
    o_win = _attend_once(s, vw_ref[0, 0, pl.ds(wstart, span), :])

    sm = sm_ref[0]
    y = _gate_rows(sm, g, 1) * o_sel + _gate_rows(sm, g, 2) * o_win
    o_ref[0] = (oc_ref[0] + _head_tile(y)).astype(BF16)


def _selwin_attention(tile_list, tile_count, qa, ksl, nkv, selb, ocg, sm):
    B, H, S, _ = qa.shape
    G = NSA_KV_GROUPS
    nq = S // Q_TILE
    rows = NSA_HPG * Q_TILE
    kv_spec = lambda piece: pl.BlockSpec((1, 1, S, LANES), lambda b, g, i, tl, tc: (b, piece + g, 0, 0))
    out_tile = pl.BlockSpec((1, Q_TILE, NSA_HPG * HEAD_DIM), lambda b, g, i, tl, tc: (b, i, g))
    grid_spec = pltpu.PrefetchScalarGridSpec(
        num_scalar_prefetch=2,
        grid=(B, G, nq),
        in_specs=[
            pl.BlockSpec((1, NSA_HPG, Q_TILE, LANES), lambda b, g, i, tl, tc: (b, g, i, 0)),
            pl.BlockSpec((1, 1, S, 2 * LANES), lambda b, g, i, tl, tc: (b, g, 0, 0)),
            kv_spec(0), kv_spec(2), kv_spec(4),
            pl.BlockSpec((1, 1, Q_TILE, LANES), lambda b, g, i, tl, tc: (b, g, i, 0)),
            out_tile,
            pl.BlockSpec((1, Q_TILE, LANES), lambda b, g, i, tl, tc: (b, i, 0)),
        ],
        out_specs=out_tile,
        scratch_shapes=[pltpu.VMEM((rows, LANES), F32)] * 4,
    )
    return pl.pallas_call(
        _selwin_kernel,
        grid_spec=grid_spec,
        out_shape=jax.ShapeDtypeStruct((B, S, NSA_W), BF16),
        compiler_params=_cparams(("parallel", "parallel", "arbitrary")),
        name="selwin_attention",
    )(tile_list, tile_count, qa, ksl, nkv, nkv, nkv, selb, ocg, sm)


def _fox_kernel(q_ref, k_ref, v_ref, o_ref, m_sc, acc_sc, *, tq):
    qi = pl.program_id(2)
    m_sc[...] = jnp.full(m_sc.shape, NEG, F32)
    acc_sc[...] = jnp.zeros(acc_sc.shape, F32)

    def tile(kt, width, causal):
        start = pl.multiple_of(kt * tq, tq)
        for hh in range(FOX_HPS):
            s = _dot_nt(q_ref[0, hh], k_ref[0, hh, pl.ds(start, width), :])
            if causal:
                r = lax.broadcasted_iota(jnp.int32, s.shape, 0)
                c = lax.broadcasted_iota(jnp.int32, s.shape, 1)
                s = jnp.where(r >= c, s, NEG)
            _online_update(s, v_ref[0, hh, pl.ds(start, width), :], m_sc.at[hh], acc_sc.at[hh])

    def body(kp, carry):
        tile(2 * kp, 2 * tq, False)
        return carry

    lax.fori_loop(0, qi // 2, body, 0)

    @pl.when(qi % 2 == 1)
    def _():
        tile(qi - 1, tq, False)

    tile(qi, tq, True)
    lane = lax.broadcasted_iota(jnp.int32, (tq, LANES), 1)
    o = [_normalized(acc_sc[hh]) for hh in range(FOX_HPS)]
    for pr in range(FOX_HPS // 2):
        o_ref[0, :, pr * LANES:(pr + 1) * LANES] = jnp.where(
            lane < HEAD_DIM, o[2 * pr], pltpu.roll(o[2 * pr + 1], HEAD_DIM, 1)).astype(BF16)


def _fox_attention(fq, fk, fv, tq=512):
    B, H, S, _ = fq.shape
    hps = FOX_HPS
    return pl.pallas_call(
        functools.partial(_fox_kernel, tq=tq),
        grid=(B, H // hps, S // tq),
        in_specs=[
            pl.BlockSpec((1, hps, tq, LANES), lambda b, h, i: (b, h, i, 0)),
            pl.BlockSpec((1, hps, S, LANES), lambda b, h, i: (b, h, 0, 0)),
            pl.BlockSpec((1, hps, S, LANES), lambda b, h, i: (b, h, 0, 0)),
        ],
        out_specs=pl.BlockSpec((1, tq, hps * HEAD_DIM), lambda b, h, i: (b, i, h)),
        out_shape=jax.ShapeDtypeStruct((B, S, FOX_W), BF16),
        scratch_shapes=[
            pltpu.VMEM((hps, tq, LANES), F32),
            pltpu.VMEM((hps, tq, LANES), F32),
        ],
        compiler_params=_cparams(("parallel", "parallel", "arbitrary")),
        name="fox_attention",
    )(fq, fk, fv)


def _merge_kernel(ya_ref, yb_ref, mg_ref, x_ref, mod_ref, gpost_ref, gpre_ref,
                  wa_ref, wb_ref, wo_ref, wrh_ref, wrl_ref, br_ref, stri_ref,
                  x1_ref, h2_ref, rt_ref, cnt_ref):
    D = D_MODEL

    @pl.when((pl.program_id(0) == 0) & (pl.program_id(1) == 0))
    def _():
        cnt_ref[...] = jnp.zeros(cnt_ref.shape, F32)

    a = _dot(ya_ref[0], wa_ref[...])
    bq = _dot(yb_ref[0], wb_ref[...])
    mg = mg_ref[0]
    u = mg[:, :D].astype(F32) * a + mg[:, D:].astype(F32) * bq
    mixed = _dot(u.astype(BF16), wo_ref[...])
    x1 = x_ref[0] + mod_ref[0, 2:3, :] * _rms(mixed, gpost_ref[...])
    x1_ref[0] = x1
    h2 = _rms(x1, gpre_ref[...]) * (1.0 + mod_ref[0, 4:5, :]) + mod_ref[0, 3:4, :]
    hi = h2.astype(BF16)
    lo = (h2 - hi.astype(F32)).astype(BF16)
    h2_ref[0] = h2
    lg = _dot(hi, wrh_ref[...]) + _dot(lo, wrh_ref[...]) + _dot(hi, wrl_ref[...]) + br_ref[...]

    lane = lax.broadcasted_iota(jnp.int32, lg.shape, 1)
    lanef = lane.astype(F32)
    no_lane = float(LANES)
    is_g = lane < N_EXPERT_GROUPS
    gl = jnp.where(is_g, lg, NEG)
    gmax = jnp.max(gl, axis=-1, keepdims=True)
    pg_top = 1.0 / jnp.sum(jnp.where(is_g, jnp.exp(gl - gmax), 0.0), axis=-1, keepdims=True)
    g_idx = jnp.min(jnp.where(is_g & (gl == gmax), lanef, no_lane), axis=-1, keepdims=True)
    in_grp = ((lane >= N_EXPERT_GROUPS) & (lane < N_EXPERT_GROUPS + N_EXPERTS)
              & (((lane - N_EXPERT_GROUPS) // EXPERTS_PER_GROUP).astype(F32) == g_idx))
    le = jnp.where(in_grp, lg, NEG)
    m1 = jnp.max(le, axis=-1, keepdims=True)
    i1 = jnp.min(jnp.where(in_grp & (le == m1), lanef, no_lane), axis=-1, keepdims=True)
    rest = in_grp & (lanef != i1)
    le2 = jnp.where(rest, lg, NEG)
    m2 = jnp.max(le2, axis=-1, keepdims=True)
    i2 = jnp.min(jnp.where(rest & (le2 == m2), lanef, no_lane), axis=-1, keepdims=True)
    e21 = jnp.exp(m2 - m1)
    w1 = pg_top / (1.0 + e21)
    w2 = w1 * e21
    pick1 = lanef == i1
    pick2 = lanef == i2
    onehot = jnp.where(pick1 | pick2, 1.0, 0.0)
    before = cnt_ref[...] + _dot(stri_ref[...], onehot.astype(BF16))
    rank1 = jnp.sum(jnp.where(pick1, before, 0.0), axis=-1, keepdims=True)
    rank2 = jnp.sum(jnp.where(pick2, before, 0.0), axis=-1, keepdims=True)
    cnt_ref[...] = cnt_ref[...] + jnp.sum(onehot, axis=0, keepdims=True)
    fields = [i1 - N_EXPERT_GROUPS, i2 - N_EXPERT_GROUPS, rank1, rank2, w1, w2]
    rt = jnp.zeros(lg.shape, F32)
    for k, f in enumerate(fields):
        rt = jnp.where(lane == k, f, rt)
    rt_ref[0] = rt


def _merge(ya, yb, mg, x, mod, gpost, gpre, wa, wb, wo, wrh, wrl, br, stri):
    B, S, D = x.shape
    tm = MERGE_TILE
    c2 = lambda b, i: (0, 0)
    row = lambda w: pl.BlockSpec((1, tm, w), lambda b, i: (b, i, 0))
    return pl.pallas_call(
        _merge_kernel,
        grid=(B, S // tm),
        in_specs=[
            row(NSA_W), row(FOX_W), row(2 * D), row(D),
            pl.BlockSpec((1, 6, D), lambda b, i: (b, 0, 0)),
            pl.BlockSpec((1, D), c2), pl.BlockSpec((1, D), c2),
            pl.BlockSpec((NSA_W, D), c2), pl.BlockSpec((FOX_W, D), c2), pl.BlockSpec((D, D), c2),
            pl.BlockSpec((D, LANES), c2), pl.BlockSpec((D, LANES), c2), pl.BlockSpec((1, LANES), c2),
            pl.BlockSpec((tm, tm), c2),
        ],
        out_specs=[row(D), row(D), row(LANES), pl.BlockSpec((1, LANES), c2)],
        out_shape=[
            jax.ShapeDtypeStruct((B, S, D), F32),
            jax.ShapeDtypeStruct((B, S, D), F32),
            jax.ShapeDtypeStruct((B, S, LANES), F32),
            jax.ShapeDtypeStruct((1, LANES), F32),
        ],
        compiler_params=_cparams(("arbitrary", "arbitrary")),
        name="merge",
    )(ya, yb, mg, x, mod, gpost, gpre, wa, wb, wo, wrh, wrl, br, stri)


def _expert_kernel(be_ref, na_ref, tok_ref, h_hbm, wg_ref, wu_ref, wd_ref, o_ref,
                   x_even, x_odd, wg_b, wu_b, wd_b, sem):
    i = pl.program_id(0)
    n_active = na_ref[0]
    last_block = pl.num_programs(0) - 1
    bufs = (x_even, x_odd)

    def row_copy(blk, r, sl):
        tok = tok_ref[blk * MOE_TILE + r]
        return pltpu.make_async_copy(h_hbm.at[pl.ds(tok, 1)], bufs[sl].at[pl.ds(r, 1)], sem.at[sl])

    def wait_rows(blk, sl):
        del blk
        pltpu.make_async_copy(h_hbm.at[pl.ds(0, MOE_TILE)], bufs[sl], sem.at[sl]).wait()

    @pl.when(i == 0)
    def _():
        def body(r, carry):
            row_copy(0, r, 0).start()
            return carry
        lax.fori_loop(0, MOE_TILE, body, 0, unroll=8)

    @pl.when((i == 0) | (be_ref[i] != be_ref[jnp.maximum(i - 1, 0)]))
    def _():
        wg_b[...] = wg_ref[0].astype(BF16)
        wu_b[...] = wu_ref[0].astype(BF16)
        wd_b[...] = wd_ref[0].astype(BF16)

    def step(sl):
        wait_rows(i, sl)
        nxt = jnp.minimum(i + 1, last_block)
        for r in range(MOE_TILE):
            row_copy(nxt, r, 1 - sl).start(priority=r % 2)
        x = bufs[sl][...].astype(BF16)
        gate = _dot(x, wg_b[...])
        up = _dot(x, wu_b[...])
        mid = (gate * jax.nn.sigmoid(gate) * up).astype(BF16)
        o_ref[...] = _dot(mid, wd_b[...])

        @pl.when(i == n_active - 1)
        def _():
            wait_rows(nxt, 1 - sl)

    for sl in range(2):
        pl.when((i % 2 == sl) & (i < n_active))(functools.partial(step, sl))

    @pl.when(i >= n_active)
    def _():
        o_ref[...] = jnp.zeros(o_ref.shape, o_ref.dtype)


def _experts(block_expert, n_active, buf_tok, h2, wg, wu, wd):
    cap = buf_tok.shape[0]
    D = D_MODEL
    nblk = cap // MOE_TILE
    grid_spec = pltpu.PrefetchScalarGridSpec(
        num_scalar_prefetch=3,
        grid=(nblk,),
        in_specs=[
            pl.BlockSpec(memory_space=pl.ANY),
            pl.BlockSpec((1, D, D_EXPERT), lambda i, be, na, tok: (be[i], 0, 0)),
            pl.BlockSpec((1, D, D_EXPERT), lambda i, be, na, tok: (be[i], 0, 0)),
            pl.BlockSpec((1, D_EXPERT, D), lambda i, be, na, tok: (be[i], 0, 0)),
        ],
        out_specs=pl.BlockSpec((MOE_TILE, D), lambda i, be, na, tok: (i, 0)),
        scratch_shapes=[
            pltpu.VMEM((MOE_TILE, D), F32),
            pltpu.VMEM((MOE_TILE, D), F32),
            pltpu.VMEM((D, D_EXPERT), BF16),
            pltpu.VMEM((D, D_EXPERT), BF16),
            pltpu.VMEM((D_EXPERT, D), BF16),
            pltpu.SemaphoreType.DMA((2,)),
        ],
    )
    return pl.pallas_call(
        _expert_kernel,
        grid_spec=grid_spec,
        out_shape=jax.ShapeDtypeStruct((cap, D), F32),
        compiler_params=_cparams(("arbitrary",)),
        name="experts",
    )(block_expert, n_active, buf_tok, h2, wg, wu, wd)


def _final_kernel(dest_ref, x1_ref, rt_ref, mod_ref, g_ref, y_hbm, o_ref, a_even, b_even, a_odd, b_odd, sem):
    j = pl.program_id(0)
    last_tile = pl.num_programs(0) - 1
    tm = o_ref.shape[0]
    bufs = ((a_even, b_even), (a_odd, b_odd))

    def row_copy(tile, r, k, sl):
        row = dest_ref[(tile * tm + r) * EXPERT_TOP_K + k]
        return pltpu.make_async_copy(y_hbm.at[pl.ds(row, 1)], bufs[sl][k].at[pl.ds(r, 1)], sem.at[sl])

    def wait_rows(tile, sl):
        del tile
        for k in range(EXPERT_TOP_K):
            pltpu.make_async_copy(y_hbm.at[pl.ds(0, tm)], bufs[sl][k], sem.at[sl]).wait()

    @pl.when(j == 0)
    def _():
        def body(r, carry):
            for k in range(EXPERT_TOP_K):
                row_copy(0, r, k, 0).start()
            return carry
        lax.fori_loop(0, tm, body, 0, unroll=4)

    def step(sl):
        wait_rows(j, sl)
        nxt = jnp.minimum(j + 1, last_tile)
        for r in range(tm):
            for k in range(EXPERT_TOP_K):
                row_copy(nxt, r, k, 1 - sl).start(priority=k)
        rt = rt_ref[...]
        lane = lax.broadcasted_iota(jnp.int32, rt.shape, 1)
        w0 = jnp.sum(jnp.where(lane == 4, rt, 0.0), axis=-1, keepdims=True)
        w1 = jnp.sum(jnp.where(lane == 5, rt, 0.0), axis=-1, keepdims=True)
        y = w0 * bufs[sl][0][...] + w1 * bufs[sl][1][...]
        o_ref[...] = x1_ref[...] + mod_ref[0, 5:6, :] * _rms(y, g_ref[...])

        @pl.when(j == last_tile)
        def _():
            wait_rows(nxt, 1 - sl)

    for sl in range(2):
        pl.when(j % 2 == sl)(functools.partial(step, sl))


def _final(dest, x1, rt, mod, g, yb, tiles_per_batch):
    T, D = x1.shape
    tm = MERGE_TILE
    grid_spec = pltpu.PrefetchScalarGridSpec(
        num_scalar_prefetch=1,
        grid=(T // tm,),
        in_specs=[
            pl.BlockSpec((tm, D), lambda j, d: (j, 0)),
            pl.BlockSpec((tm, LANES), lambda j, d: (j, 0)),
            pl.BlockSpec((1, 6, D), lambda j, d: (j // tiles_per_batch, 0, 0)),
            pl.BlockSpec((1, D), lambda j, d: (0, 0)),
            pl.BlockSpec(memory_space=pl.ANY),
        ],
        out_specs=pl.BlockSpec((tm, D), lambda j, d: (j, 0)),
        scratch_shapes=[pltpu.VMEM((tm, D), F32)] * 4 + [pltpu.SemaphoreType.DMA((2,))],
    )
    return pl.pallas_call(
        _final_kernel,
        grid_spec=grid_spec,
        out_shape=jax.ShapeDtypeStruct((T, D), F32),
        compiler_params=_cparams(("arbitrary",)),
        name="final",
    )(dest, x1, rt, mod, g, yb)


def _overlap_matrix():
    n = np.arange(N_CMP_PAD)[:, None]
    j = np.arange(LANES)[None, :]
    start = n * CMP_STRIDE
    ov = (start < j * SEL_LEN + SEL_LEN) & (start + CMP_LEN - 1 >= j * SEL_LEN) & (n < N_CMP_PAD - 1)
    return jnp.asarray(ov.T.astype(np.float32), dtype=BF16)


def _pad_cols(w, width=LANES):
    return jnp.pad(w, ((0, 0), (0, width - w.shape[1])))


def _dispatch_plan(rt, cnt, T):
    expert = rt[:, 0:2].astype(jnp.int32)
    rank = rt[:, 2:4].astype(jnp.int32)
    weight = rt[:, 4:6]
    counts = cnt[0, N_EXPERT_GROUPS:N_EXPERT_GROUPS + N_EXPERTS].astype(jnp.int32)
    padded = (counts + MOE_TILE - 1) // MOE_TILE * MOE_TILE
    pad_end = jnp.cumsum(padded)
    pad_start = pad_end - padded
    onehot = expert[:, :, None] == jnp.arange(N_EXPERTS)[None, None, :]
    dest = jnp.sum(jnp.where(onehot, pad_start[None, None, :], 0), axis=-1) + rank
    A = T * EXPERT_TOP_K
    cap = -(-(A + N_EXPERTS * (MOE_TILE - 1)) // MOE_TILE) * MOE_TILE
    nblk = cap // MOE_TILE
    n_active = (pad_end[-1] // MOE_TILE).astype(jnp.int32)
    blk = jnp.arange(nblk) * MOE_TILE
    block_expert = jnp.minimum(jnp.sum(pad_end[None, :] <= blk[:, None], axis=1), N_EXPERTS - 1)
    last = jnp.max(jnp.where(jnp.arange(nblk) < n_active, block_expert, 0))
    block_expert = jnp.where(jnp.arange(nblk) < n_active, block_expert, last).astype(jnp.int32)
    tok = jnp.arange(A, dtype=jnp.int32) // EXPERT_TOP_K
    buf_tok = jnp.zeros((cap,), jnp.int32).at[dest.reshape(A)].set(tok)
    return weight, dest, buf_tok, block_expert, n_active.reshape(1)


def kernel(x, c, w_ada, b_ada, g_pre_mix, g_post_mix, g_pre_ffn, g_post_ffn, w_in, b_forget,
           cmp_pe_k, cmp_w1_k, cmp_w2_k, cmp_pe_v, cmp_w1_v, cmp_w2_v,
           w_o_nsa, w_o_fox, w_out, w_router_group, b_router_group, w_router_expert, b_router_expert,
           w_exp_gate, w_exp_up, w_exp_down):
    B, S, D = x.shape
    T = B * S
    depth = w_ada.shape[0]
    ov = _overlap_matrix()
    tri = jnp.asarray(np.tril(np.ones((IN_TILE, IN_TILE), np.float32)), dtype=BF16)
    stri = jnp.asarray(np.tril(np.ones((MERGE_TILE, MERGE_TILE), np.float32), -1), dtype=BF16)
    row_feat = _row_features(S)
    placement = _placement()
    cmp_ext = _cmp_key_ext()
    for l in range(depth):
        mod = (jax.nn.silu(c) @ w_ada[l] + b_ada[l]).reshape(B, 6, D)
        w_qa, w_kva, w_gl, w_fox, w_f, w_mg = jnp.split(w_in[l], IN_SPLITS, axis=-1)
        w_big = jnp.concatenate([w_qa, w_kva, w_fox, w_mg], axis=1).astype(BF16)
        w_small = _pad_cols(jnp.concatenate([w_gl, w_f], axis=1)).astype(BF16)
        bf_pad = jnp.pad(b_forget[l], (F_LANE, LANES - F_LANE - FOX_HEADS)).reshape(1, LANES)
        qa, ckv, ksl, nkv, fq, fk, fv, mg, sm = _inproj(
            x, mod, g_pre_mix[l].reshape(1, D), w_big, w_small, bf_pad, tri, row_feat, placement)

        half = CMP_LEN // 2
        pe = jnp.stack([cmp_pe_k[l], cmp_pe_v[l]]).reshape(2, 2, 1, half * HEAD_DIM)
        w1 = jnp.stack([cmp_w1_k[l], cmp_w1_v[l]]).reshape(2, 2, half * HEAD_DIM, HEAD_DIM).astype(BF16)
        w2 = jnp.pad(jnp.stack([cmp_w2_k[l], cmp_w2_v[l]]), ((0, 0), (0, 0), (0, LANES - HEAD_DIM))).astype(BF16)
        kvc = _compress(ckv.reshape(B, 4, S // CMP_STRIDE, CMP_STRIDE * HEAD_DIM), pe, w1, w2, cmp_ext)
        ocg, selb, flags = _cmp_attention(qa, kvc, sm, ov)
        nq = S // Q_TILE
        per_tile = K_TILE // SEL_LEN
        tile_any = jnp.max(flags.reshape(B, NSA_KV_GROUPS, nq, MAX_TILES, per_tile), axis=-1)
        tile_id = jnp.arange(MAX_TILES)
        diag = (jnp.arange(nq) // (K_TILE // Q_TILE))[:, None]
        active = (tile_any > 0) & (tile_id < diag)
        slot = jnp.cumsum(active, axis=-1) - 1
        hit = active[..., :, None] & (slot[..., :, None] == tile_id)
        tile_list = jnp.sum(jnp.where(hit, tile_id[:, None], 0), axis=-2).astype(jnp.int32).reshape(-1)
        tile_count = jnp.sum(active, axis=-1).astype(jnp.int32).reshape(-1)
        y_a = _selwin_attention(tile_list, tile_count, qa, ksl, nkv, selb, ocg, sm)

        y_b = _fox_attention(fq, fk, fv)

        w_r = _pad_cols(jnp.concatenate([w_router_group[l], w_router_expert[l]], axis=1))
        w_rh = w_r.astype(BF16)
        w_rl = (w_r - w_rh.astype(F32)).astype(BF16)
        b_r = _pad_cols(jnp.concatenate([b_router_group[l], b_router_expert[l]]).reshape(1, -1))
        x1, h2, rt, cnt = _merge(y_a, y_b, mg, x, mod, g_post_mix[l].reshape(1, D), g_pre_ffn[l].reshape(1, D),
                                 w_o_nsa[l].astype(BF16), w_o_fox[l].astype(BF16), w_out[l].astype(BF16),
                                 w_rh, w_rl, b_r, stri)

        weight, dest, buf_tok, block_expert, n_active = _dispatch_plan(rt.reshape(T, LANES), cnt, T)
        yb = _experts(block_expert, n_active, buf_tok, h2.reshape(T, D), w_exp_gate[l], w_exp_up[l], w_exp_down[l])
        x = _final(dest.reshape(T * EXPERT_TOP_K), x1.reshape(T, D), rt.reshape(T, LANES), mod,
                   g_post_ffn[l].reshape(1, D), yb, S // MERGE_TILE).reshape(B, S, D)
    return x
```

```python
import functools

import ml_dtypes
import numpy as np
import jax
import jax.numpy as jnp
from jax import lax
from jax.experimental import pallas as pl
from jax.experimental.pallas import tpu as pltpu

D_MODEL = 1024
HEAD_DIM = 64
NSA_HEADS = 8
NSA_KV_GROUPS = 2
NSA_HPG = NSA_HEADS // NSA_KV_GROUPS
FOX_HEADS = 8
CMP_LEN = 32
CMP_STRIDE = 16
SEL_LEN = 64
N_SEL = 16
WINDOW = 512
N_EXPERT_GROUPS = 4
EXPERTS_PER_GROUP = 8
N_EXPERTS = N_EXPERT_GROUPS * EXPERTS_PER_GROUP
EXPERT_TOP_K = 2
D_EXPERT = D_MODEL // 2
NORM_EPS = 1e-6
NEG = -1e30
FORCE = 1e9
LOG2E = 1.4426950408889634

NSA_W = NSA_HEADS * HEAD_DIM
NSA_KV_W = NSA_KV_GROUPS * HEAD_DIM
FOX_W = FOX_HEADS * HEAD_DIM
IN_SIZES = (NSA_W, 6 * NSA_KV_W, 3 * NSA_HEADS, 3 * FOX_W, FOX_HEADS, 2 * D_MODEL)
IN_SPLITS = tuple(int(v) for v in np.cumsum(IN_SIZES)[:-1])

LANES = 128
Q_TILE = 128
K_TILE = 256
SW_TILE = 256
N_CMP_PAD = 512
MOE_TILE = 256
IN_TILE = 512
MERGE_TILE = 256
FOX_HPS = 4
CMP_SUB = 2
MAX_TILES = 32
VMEM_LIMIT = 56 * 1024 * 1024

F_LANE = 3 * NSA_HEADS
U_LANE = 64
ONE_LANE = 88
A_LANE = 89
B_LANE = 90
EXT = HEAD_DIM
G_FQ, G_FK, G_NQ, G_NK, N_GROUPS = 0, 8, 16, 24, 25

F32 = jnp.float32
BF16 = jnp.bfloat16


def _dot(a, b):
    return jnp.dot(a, b, preferred_element_type=F32)


def _dot_nt(a, b):
    return lax.dot_general(a, b, (((1,), (1,)), ((), ())), preferred_element_type=F32)


def _rms(x, g):
    return x * lax.rsqrt(jnp.mean(x * x, axis=-1, keepdims=True) + NORM_EPS) * g


def _cparams(sem):
    return pltpu.CompilerParams(dimension_semantics=sem, vmem_limit_bytes=VMEM_LIMIT)


def _split3(x):
    hi = x.astype(BF16).astype(F32)
    r = x - hi
    mid = r.astype(BF16).astype(F32)
    lo = (r - mid).astype(BF16).astype(F32)
    return hi, mid, lo


def _np_split3(x):
    x = np.asarray(x, np.float32)
    hi = x.astype(ml_dtypes.bfloat16).astype(np.float32)
    r = x - hi
    mid = r.astype(ml_dtypes.bfloat16).astype(np.float32)
    lo = (r - mid).astype(ml_dtypes.bfloat16).astype(np.float32)
    return hi, mid, lo


def _alibi_c():
    slopes = np.exp2(-8.0 * np.arange(1, NSA_HEADS + 1, dtype=np.float32) / NSA_HEADS).astype(np.float32)
    return slopes * np.float32(LOG2E)


def _row_features(S):
    t = np.arange(S, dtype=np.float32)
    c = _alibi_c()
    rs = np.zeros((S, LANES), np.float32)
    for h in range(NSA_HEADS):
        for j, term in enumerate(_np_split3(c[h] * t)):
            rs[:, U_LANE + 8 * j + h] = -term
    rs[:, ONE_LANE] = 1.0
    rs[:, A_LANE] = np.floor(t / LANES)
    rs[:, B_LANE] = t % LANES
    return jnp.asarray(rs, dtype=BF16)


def _placement():
    c = _alibi_c()
    p = np.zeros((LANES, N_GROUPS * LANES), np.float32)
    for h in range(FOX_HEADS):
        q0 = (G_FQ + h) * LANES + EXT
        k0 = (G_FK + h) * LANES + EXT
        for j in range(3):
            p[ONE_LANE, q0 + j] = -1.0
            p[F_LANE + 8 * j + h, q0 + 3 + j] = 1.0
            p[F_LANE + 8 * j + h, k0 + j] = 1.0
            p[ONE_LANE, k0 + 3 + j] = 1.0
    for h in range(NSA_HEADS):
        q0 = (G_NQ + h) * LANES + EXT
        c128 = _np_split3(c[h] * np.float32(LANES))
        c1 = _np_split3(c[h])
        for j in range(3):
            p[U_LANE + 8 * j + h, q0 + j] = 1.0
            p[ONE_LANE, q0 + 3 + j] = c128[j]
            p[ONE_LANE, q0 + 6 + j] = c1[j]
    k0 = G_NK * LANES + EXT
    for j in range(3):
        p[ONE_LANE, k0 + j] = 1.0
        p[A_LANE, k0 + 3 + j] = 1.0
        p[B_LANE, k0 + 6 + j] = 1.0
    return jnp.asarray(p, dtype=BF16)


def _cmp_key_ext():
    pos = np.arange(N_CMP_PAD, dtype=np.float32) * CMP_STRIDE + (CMP_LEN - 1)
    e = np.zeros((2, N_CMP_PAD, LANES), np.float32)
    for j in range(3):
        e[0, :, EXT + j] = 1.0
        e[0, :, EXT + 3 + j] = np.floor(pos / LANES)
        e[0, :, EXT + 6 + j] = pos % LANES
    return jnp.asarray(e, dtype=BF16)


def _inproj_kernel(x_ref, mod_ref, g_ref, wb_ref, ws_ref, bf_ref, tri_ref, rs_ref, p_ref,
                   qa_ref, ckv_ref, ksl_ref, nkv_ref, fq_ref, fk_ref, fv_ref, mg_ref, sm_ref, carry_sc):
    i = pl.program_id(1)
    tm = x_ref.shape[1]
    x = x_ref[0]
    h = _rms(x, g_ref[...]) * (1.0 + mod_ref[0, 1:2, :]) + mod_ref[0, 0:1, :]
    hb = h.astype(BF16)
    lane = lax.broadcasted_iota(jnp.int32, (tm, LANES), 1)
    lower = lane < HEAD_DIM
    ones_col = (lane == EXT).astype(F32)

    z = _dot(hb, ws_ref[...]) + bf_ref[...]
    logsig = jnp.minimum(z, 0.0) - jnp.log1p(jnp.exp(-jnp.abs(z)))
    sm_ref[0] = jnp.where(lane < F_LANE, jax.nn.sigmoid(z), logsig)

    @pl.when(i == 0)
    def _():
        carry_sc[...] = jnp.zeros(carry_sc.shape, F32)

    is_f = (lane >= F_LANE) & (lane < F_LANE + FOX_HEADS)
    l_hi, l_mid, l_lo = _split3(jnp.where(is_f, logsig, 0.0))
    tri = tri_ref[...]
    cum = carry_sc[...] + _dot(tri, l_hi.astype(BF16)) + _dot(tri, l_mid.astype(BF16)) + _dot(tri, l_lo.astype(BF16))
    carry_sc[...] = cum[tm - 1:tm, :]
    f_hi, f_mid, f_lo = _split3(cum * LOG2E)
    feat = (f_hi + pltpu.roll(f_mid, 8, 1) + pltpu.roll(f_lo, 16, 1) + rs_ref[...].astype(F32)).astype(BF16)

    def ext(group):
        return _dot(feat, p_ref[:, group * LANES:(group + 1) * LANES])

    def piece(acc, idx, extra):
        pair = acc[:, (idx // 2) * LANES:(idx // 2 + 1) * LANES]
        if idx % 2:
            pair = pltpu.roll(pair, HEAD_DIM, 1)
        return jnp.where(lower, pair, extra).astype(BF16)

    qscale = (HEAD_DIM ** -0.5) * LOG2E
    acc = _dot(hb, wb_ref[:, 0:NSA_W]) * qscale
    for hd in range(NSA_HEADS):
        qa_ref[0, hd] = piece(acc, hd, ext(G_NQ + hd))
    off = NSA_W
    acc = _dot(hb, wb_ref[:, off:off + 6 * NSA_KV_W])
    for pc in range(4):
        ckv_ref[0, pc] = acc[:, pc * HEAD_DIM:(pc + 1) * HEAD_DIM].astype(BF16)
    ext_k = ext(G_NK)
    t = i * tm + lax.broadcasted_iota(jnp.int32, (tm, LANES), 0)
    block_onehot = (lane == t // SEL_LEN).astype(BF16)
    for g in range(NSA_KV_GROUPS):
        ksl_ref[0, g, :, 0:LANES] = piece(acc, 4 + g, ext_k)
        ksl_ref[0, g, :, LANES:2 * LANES] = block_onehot
        nkv_ref[0, g] = piece(acc, 6 + g, ones_col)
        nkv_ref[0, 2 + g] = piece(acc, 8 + g, ext_k)
        nkv_ref[0, 4 + g] = piece(acc, 10 + g, ones_col)
    off += 6 * NSA_KV_W
    acc = _dot(hb, wb_ref[:, off:off + FOX_W]) * qscale
    for hd in range(FOX_HEADS):
        fq_ref[0, hd] = piece(acc, hd, ext(G_FQ + hd))
    off += FOX_W
    acc = _dot(hb, wb_ref[:, off:off + FOX_W])
    for hd in range(FOX_HEADS):
        fk_ref[0, hd] = piece(acc, hd, ext(G_FK + hd))
    off += FOX_W
    acc = _dot(hb, wb_ref[:, off:off + FOX_W])
    for hd in range(FOX_HEADS):
        fv_ref[0, hd] = piece(acc, hd, ones_col)
    off += FOX_W
    for c in range(4):
        acc = _dot(hb, wb_ref[:, off + c * 512: off + (c + 1) * 512])
        mg_ref[0, :, c * 512:(c + 1) * 512] = jax.nn.sigmoid(acc).astype(BF16)


def _inproj(x, mod, g, wb, ws, bfp, tri, rs, pm):
    B, S, D = x.shape
    tm = IN_TILE
    nb = wb.shape[1]
    const2 = lambda b, i: (0, 0)
    heads = lambda n: pl.BlockSpec((1, n, tm, LANES), lambda b, i: (b, 0, i, 0))
    hshape = lambda n: jax.ShapeDtypeStruct((B, n, S, LANES), BF16)
    return pl.pallas_call(
        _inproj_kernel,
        grid=(B, S // tm),
        in_specs=[
            pl.BlockSpec((1, tm, D), lambda b, i: (b, i, 0)),
            pl.BlockSpec((1, 6, D), lambda b, i: (b, 0, 0)),
            pl.BlockSpec((1, D), const2),
            pl.BlockSpec((D, nb), const2),
            pl.BlockSpec((D, LANES), const2),
            pl.BlockSpec((1, LANES), const2),
            pl.BlockSpec((tm, tm), const2),
            pl.BlockSpec((tm, LANES), lambda b, i: (i, 0)),
            pl.BlockSpec((LANES, N_GROUPS * LANES), const2),
        ],
        out_specs=[
            heads(NSA_HEADS),
            pl.BlockSpec((1, 4, tm, HEAD_DIM), lambda b, i: (b, 0, i, 0)),
            pl.BlockSpec((1, NSA_KV_GROUPS, tm, 2 * LANES), lambda b, i: (b, 0, i, 0)),
            heads(6), heads(FOX_HEADS), heads(FOX_HEADS), heads(FOX_HEADS),
            pl.BlockSpec((1, tm, 2 * D), lambda b, i: (b, i, 0)),
            pl.BlockSpec((1, tm, LANES), lambda b, i: (b, i, 0)),
        ],
        out_shape=[
            hshape(NSA_HEADS),
            jax.ShapeDtypeStruct((B, 4, S, HEAD_DIM), BF16),
            jax.ShapeDtypeStruct((B, NSA_KV_GROUPS, S, 2 * LANES), BF16),
            hshape(6), hshape(FOX_HEADS), hshape(FOX_HEADS), hshape(FOX_HEADS),
            jax.ShapeDtypeStruct((B, S, 2 * D), BF16),
            jax.ShapeDtypeStruct((B, S, LANES), F32),
        ],
        scratch_shapes=[pltpu.VMEM((1, LANES), F32)],
        compiler_params=_cparams(("parallel", "arbitrary")),
        name="inproj",
    )(x, mod, g, wb, ws, bfp, tri, rs, pm)


def _compress_kernel(x_ref, pe_ref, w1_ref, w2_ref, e_ref, o_ref):
    x = x_ref[0, 0].astype(F32)
    x_lo = (x + pe_ref[0, 0]).astype(BF16)
    x_hi = (x + pe_ref[0, 1]).astype(BF16)
    y_lo = _dot(x_lo, w1_ref[0, 0])
    y_hi = _dot(x_hi, w1_ref[0, 1])
    n = y_hi.shape[0]
    hid = y_lo + pltpu.roll(y_hi, n - 1, 0)
    hid = jax.nn.gelu(hid)
    o_ref[0, 0] = (_dot(hid.astype(BF16), w2_ref[0]) + e_ref[0].astype(F32)).astype(BF16)


def _compress(kv_rows, pe, w1, w2, e):
    B = kv_rows.shape[0]
    R, C = kv_rows.shape[2], kv_rows.shape[3]
    return pl.pallas_call(
        _compress_kernel,
        grid=(B, 4),
        in_specs=[
            pl.BlockSpec((1, 1, R, C), lambda b, p: (b, p, 0, 0)),
            pl.BlockSpec((1, 2, 1, C), lambda b, p: (p // 2, 0, 0, 0)),
            pl.BlockSpec((1, 2, C, HEAD_DIM), lambda b, p: (p // 2, 0, 0, 0)),
            pl.BlockSpec((1, HEAD_DIM, LANES), lambda b, p: (p // 2, 0, 0)),
            pl.BlockSpec((1, R, LANES), lambda b, p: (p // 2, 0, 0)),
        ],
        out_specs=pl.BlockSpec((1, 1, R, LANES), lambda b, p: (b, p, 0, 0)),
        out_shape=jax.ShapeDtypeStruct((B, 4, R, LANES), BF16),
        compiler_params=_cparams(("parallel", "parallel")),
        name="compress",
    )(kv_rows, pe, w1, w2, e)


def _gate_rows(sm, g, branch):
    col = lax.broadcasted_iota(jnp.int32, sm.shape, 1)
    parts = []
    for hl in range(NSA_HPG):
        want = 3 * (NSA_HPG * g + hl) + branch
        parts.append(jnp.sum(jnp.where(col == want, sm, 0.0), axis=-1, keepdims=True))
    return jnp.concatenate(parts, axis=0)


def _head_tile(y):
    n = y.shape[0] // NSA_HPG
    lane = lax.broadcasted_iota(jnp.int32, (n, LANES), 1)
    hs = [y[i * n:(i + 1) * n] for i in range(NSA_HPG)]
    pairs = [jnp.where(lane < HEAD_DIM, hs[2 * i], pltpu.roll(hs[2 * i + 1], HEAD_DIM, 1)) for i in range(2)]
    return jnp.concatenate(pairs, axis=1)


def _cmp_kernel(q_ref, kc_ref, vc_ref, sm_ref, ovt_ref, oc_ref, selb_ref, flag_ref):
    g = pl.program_id(1)
    for sub in range(CMP_SUB):
        rows = pl.ds(sub * Q_TILE, Q_TILE)
        q0 = (pl.program_id(2) * CMP_SUB + sub) * Q_TILE
        q = q_ref[0, :, rows, :].reshape(NSA_HPG * Q_TILE, LANES)
        oc, selb, flag = _cmp_tile(q, kc_ref[0, 0], vc_ref[0, 0], sm_ref[0, rows, :], ovt_ref[...], g, q0)
        oc_ref[0, rows, :] = oc
        selb_ref[0, 0, rows, :] = selb
        flag_ref[0, 0, sub] = flag


def _cmp_tile(q, kc, vc, sm, ovt, g, q0):
    s = _dot_nt(q, kc)
    r = lax.broadcasted_iota(jnp.int32, (NSA_HPG * Q_TILE, 1), 0) % Q_TILE
    n = lax.broadcasted_iota(jnp.int32, (1, N_CMP_PAD), 1)
    dc = (q0 + r) - (n * CMP_STRIDE + (CMP_LEN - 1))
    mask = (dc >= 0) & (n < N_CMP_PAD - 1)
    l = jnp.where(mask, s, NEG)
    m = jnp.max(l, axis=-1, keepdims=True)
    e = jnp.where(mask, jnp.exp2(l - m), 0.0)
    pc = e / jnp.maximum(jnp.sum(e, axis=-1, keepdims=True), 1e-30)
    oc = _dot(pc.astype(BF16), vc)
    oc = _head_tile(oc * _gate_rows(sm, g, 0))
    ps = pc[0:Q_TILE]
    for i in range(1, NSA_HPG):
        ps = ps + pc[i * Q_TILE:(i + 1) * Q_TILE]
    ps_hi = ps.astype(BF16)
    ps_lo = (ps - ps_hi.astype(F32)).astype(BF16)
    imp = _dot_nt(ovt, ps_hi) + _dot_nt(ovt, ps_lo)
    j = lax.broadcasted_iota(jnp.int32, imp.shape, 0)
    jf = j.astype(F32)
    t = q0 + lax.broadcasted_iota(jnp.int32, (1, Q_TILE), 1)
    cur = t // SEL_LEN
    forced = (j == 0) | (j == cur) | (j == cur - 1)
    v = jnp.where(j > cur, -FORCE, jnp.where(forced, FORCE, imp))
    sel = jnp.zeros(imp.shape, jnp.bool_)
    for _ in range(N_SEL):
        mx = jnp.max(v, axis=0, keepdims=True)
        idx = jnp.min(jnp.where(v == mx, jf, float(LANES)), axis=0, keepdims=True)
        pick = jf == idx
        sel = sel | pick
        v = jnp.where(pick, -3e38, v)
    live_t = jnp.where(sel & (j <= cur), 1.0, 0.0).astype(BF16)
    eye = (lax.broadcasted_iota(jnp.int32, imp.shape, 0) == lax.broadcasted_iota(jnp.int32, imp.shape, 1))
    live = _dot_nt(eye.astype(BF16), live_t)
    selb = jnp.where(live > 0.5, 0.0, NEG).astype(BF16)
    return oc, selb, jnp.max(live, axis=0, keepdims=True).astype(jnp.int32)


def _cmp_attention(qa, kvc, sm, ov):
    B, H, S, _ = qa.shape
    G = NSA_KV_GROUPS
    nq = S // Q_TILE
    qt = CMP_SUB * Q_TILE
    return pl.pallas_call(
        _cmp_kernel,
        grid=(B, G, nq // CMP_SUB),
        in_specs=[
            pl.BlockSpec((1, NSA_HPG, qt, LANES), lambda b, g, i: (b, g, i, 0)),
            pl.BlockSpec((1, 1, N_CMP_PAD, LANES), lambda b, g, i: (b, g, 0, 0)),
            pl.BlockSpec((1, 1, N_CMP_PAD, LANES), lambda b, g, i: (b, 2 + g, 0, 0)),
            pl.BlockSpec((1, qt, LANES), lambda b, g, i: (b, i, 0)),
            pl.BlockSpec((LANES, N_CMP_PAD), lambda b, g, i: (0, 0)),
        ],
        out_specs=[
            pl.BlockSpec((1, qt, NSA_HPG * HEAD_DIM), lambda b, g, i: (b, i, g)),
            pl.BlockSpec((1, 1, qt, LANES), lambda b, g, i: (b, g, i, 0)),
            pl.BlockSpec((1, 1, CMP_SUB, 1, LANES), lambda b, g, i: (b, g, i, 0, 0)),
        ],
        out_shape=[
            jax.ShapeDtypeStruct((B, S, NSA_W), F32),
            jax.ShapeDtypeStruct((B, G, S, LANES), BF16),
            jax.ShapeDtypeStruct((B, G, nq, 1, LANES), jnp.int32),
        ],
        compiler_params=_cparams(("parallel", "parallel", "parallel")),
        name="cmp_attention",
    )(qa, kvc, kvc, sm, ov)


def _online_update(s, v, m_ref, acc_ref):
    m_old = m_ref[...]
    m_new = jnp.maximum(m_old, jnp.max(s, axis=-1, keepdims=True))
    chunks = [s[:, c * LANES:(c + 1) * LANES] - m_new for c in range(s.shape[1] // LANES)]
    p = jnp.exp2(jnp.concatenate(chunks, axis=1))
    acc_ref[...] = jnp.exp2(m_old - m_new) * acc_ref[...] + _dot(p.astype(BF16), v)
    m_ref[...] = m_new


def _normalized(acc):
    return acc / jnp.maximum(acc[:, EXT:EXT + 1], 1e-30)


def _attend_once(s, v):
    m = jnp.broadcast_to(jnp.max(s, axis=-1, keepdims=True), (s.shape[0], LANES))
    chunks = [s[:, c * LANES:(c + 1) * LANES] - m for c in range(s.shape[1] // LANES)]
    p = jnp.exp2(jnp.concatenate(chunks, axis=1))
    return _normalized(_dot(p.astype(BF16), v))


def _selwin_kernel(list_ref, cnt_ref, q_ref, ks_ref, vs_ref, kw_ref, vw_ref, selb_ref, oc_ref, sm_ref,
                   o_ref, m_a, acc_a, m_b, acc_b):
    b = pl.program_id(0)
    g = pl.program_id(1)
    qb = pl.program_id(2)
    nq = pl.num_programs(2)
    rows = NSA_HPG * SW_TILE
    q4 = q_ref[0].reshape(rows, LANES)
    q_aug = jnp.concatenate([q4, jnp.concatenate([selb_ref[0, 0]] * NSA_HPG, axis=0)], axis=1)
    r = lax.broadcasted_iota(jnp.int32, (rows, 1), 0) % SW_TILE
    c = lax.broadcasted_iota(jnp.int32, (1, K_TILE), 1)
    rel = r - c
    diag = qb // (K_TILE // SW_TILE)

    def sel_tile(kt, m_ref, acc_ref, causal=False, bias=None):
        start = pl.multiple_of(kt * K_TILE, K_TILE)
        s = _dot_nt(q_aug, ks_ref[0, 0, pl.ds(start, K_TILE), :])
        if bias is not None:
            s = s + bias
        if causal:
            s = jnp.where(rel + (qb * SW_TILE - kt * K_TILE) >= 0, s, NEG)
        _online_update(s, vs_ref[0, 0, pl.ds(start, K_TILE), :], m_ref, acc_ref)

    for m_ref, acc_ref in ((m_a, acc_a), (m_b, acc_b)):
        m_ref[...] = jnp.full(m_ref.shape, NEG, F32)
        acc_ref[...] = jnp.zeros(acc_ref.shape, F32)
    step = (b * NSA_KV_GROUPS + g) * nq + qb
    count = cnt_ref[step]
    base = step * MAX_TILES

    def body(p, carry):
        second = 2 * p + 1
        sel_tile(list_ref[base + 2 * p], m_a, acc_a)
        sel_tile(list_ref[base + jnp.minimum(second, MAX_TILES - 1)], m_b, acc_b,
                 bias=jnp.where(second < count, 0.0, NEG))
        return carry

    lax.fori_loop(0, (count + 1) // 2, body, 0)
    m_new = jnp.maximum(m_a[...], m_b[...])
    acc_a[...] = jnp.exp2(m_a[...] - m_new) * acc_a[...] + jnp.exp2(m_b[...] - m_new) * acc_b[...]
    m_a[...] = m_new
    sel_tile(diag, m_a, acc_a, causal=True)
    o_sel = _normalized(acc_a[...])

    span = WINDOW + K_TILE
    wstart = pl.multiple_of(jnp.maximum(diag - WINDOW // K_TILE, 0) * K_TILE, K_TILE)
    dist = (qb * SW_TILE + r) - (wstart + lax.broadcasted_iota(jnp.int32, (1, span), 1))
    s = _dot_nt(q4, kw_ref[0, 0, pl.ds(wstart, span), :])
    s = jnp.where((dist >= 0) & (dist < WINDOW), s, NEG)
    o_win = _attend_once(s, vw_ref[0, 0, pl.ds(wstart, span), :])

    sm = sm_ref[0]
    y = _gate_rows(sm, g, 1) * o_sel + _gate_rows(sm, g, 2) * o_win
    o_ref[0] = (oc_ref[0] + _head_tile(y)).astype(BF16)


def _selwin_attention(tile_list, tile_count, qa, ksl, nkv, selb, ocg, sm):
    B, H, S, _ = qa.shape
    G = NSA_KV_GROUPS
    nq = S // SW_TILE
    rows = NSA_HPG * SW_TILE
    kv_spec = lambda piece: pl.BlockSpec((1, 1, S, LANES), lambda b, g, i, tl, tc: (b, piece + g, 0, 0))
    out_tile = pl.BlockSpec((1, SW_TILE, NSA_HPG * HEAD_DIM), lambda b, g, i, tl, tc: (b, i, g))
    grid_spec = pltpu.PrefetchScalarGridSpec(
        num_scalar_prefetch=2,
        grid=(B, G, nq),
        in_specs=[
            pl.BlockSpec((1, NSA_HPG, SW_TILE, LANES), lambda b, g, i, tl, tc: (b, g, i, 0)),
            pl.BlockSpec((1, 1, S, 2 * LANES), lambda b, g, i, tl, tc: (b, g, 0, 0)),
            kv_spec(0), kv_spec(2), kv_spec(4),
            pl.BlockSpec((1, 1, SW_TILE, LANES), lambda b, g, i, tl, tc: (b, g, i, 0)),
            out_tile,
            pl.BlockSpec((1, SW_TILE, LANES), lambda b, g, i, tl, tc: (b, i, 0)),
        ],
        out_specs=out_tile,
        scratch_shapes=[pltpu.VMEM((rows, LANES), F32)] * 4,
    )
    return pl.pallas_call(
        _selwin_kernel,
        grid_spec=grid_spec,
        out_shape=jax.ShapeDtypeStruct((B, S, NSA_W), BF16),
        compiler_params=_cparams(("parallel", "parallel", "arbitrary")),
        name="selwin_attention",
    )(tile_list, tile_count, qa, ksl, nkv, nkv, nkv, selb, ocg, sm)


def _fox_kernel(q_ref, k_ref, v_ref, o_ref, m_sc, acc_sc, *, tq):
    qi = pl.program_id(2)
    m_sc[...] = jnp.full(m_sc.shape, NEG, F32)
    acc_sc[...] = jnp.zeros(acc_sc.shape, F32)

    def tile(kt, width, causal):
        start = pl.multiple_of(kt * tq, tq)
        for hh in range(FOX_HPS):
            s = _dot_nt(q_ref[0, hh], k_ref[0, hh, pl.ds(start, width), :])
            if causal:
                r = lax.broadcasted_iota(jnp.int32, s.shape, 0)
                c = lax.broadcasted_iota(jnp.int32, s.shape, 1)
                s = jnp.where(r >= c, s, NEG)
            _online_update(s, v_ref[0, hh, pl.ds(start, width), :], m_sc.at[hh], acc_sc.at[hh])

    def body(kp, carry):
        tile(2 * kp, 2 * tq, False)
        return carry

    lax.fori_loop(0, qi // 2, body, 0)

    @pl.when(qi % 2 == 1)
    def _():
        tile(qi - 1, tq, False)

    tile(qi, tq, True)
    lane = lax.broadcasted_iota(jnp.int32, (tq, LANES), 1)
    o = [_normalized(acc_sc[hh]) for hh in range(FOX_HPS)]
    for pr in range(FOX_HPS // 2):
        o_ref[0, :, pr * LANES:(pr + 1) * LANES] = jnp.where(
            lane < HEAD_DIM, o[2 * pr], pltpu.roll(o[2 * pr + 1], HEAD_DIM, 1)).astype(BF16)


def _fox_attention(fq, fk, fv, tq=512):
    B, H, S, _ = fq.shape
    hps = FOX_HPS
    return pl.pallas_call(
        functools.partial(_fox_kernel, tq=tq),
        grid=(B, H // hps, S // tq),
        in_specs=[
            pl.BlockSpec((1, hps, tq, LANES), lambda b, h, i: (b, h, i, 0)),
            pl.BlockSpec((1, hps, S, LANES), lambda b, h, i: (b, h, 0, 0)),
            pl.BlockSpec((1, hps, S, LANES), lambda b, h, i: (b, h, 0, 0)),
        ],
        out_specs=pl.BlockSpec((1, tq, hps * HEAD_DIM), lambda b, h, i: (b, i, h)),
        out_shape=jax.ShapeDtypeStruct((B, S, FOX_W), BF16),
        scratch_shapes=[
            pltpu.VMEM((hps, tq, LANES), F32),
            pltpu.VMEM((hps, tq, LANES), F32),
        ],
        compiler_params=_cparams(("parallel", "parallel", "arbitrary")),
        name="fox_attention",
    )(fq, fk, fv)


def _merge_kernel(ya_ref, yb_ref, mg_ref, x_ref, mod_ref, gpost_ref, gpre_ref,
                  wa_ref, wb_ref, wo_ref, wrh_ref, wrl_ref, br_ref, stri_ref,
                  x1_ref, h2_ref, rt_ref, cnt_ref):
    D = D_MODEL

    @pl.when((pl.program_id(0) == 0) & (pl.program_id(1) == 0))
    def _():
        cnt_ref[...] = jnp.zeros(cnt_ref.shape, F32)

    a = _dot(ya_ref[0], wa_ref[...])
    bq = _dot(yb_ref[0], wb_ref[...])
    mg = mg_ref[0]
    u = mg[:, :D].astype(F32) * a + mg[:, D:].astype(F32) * bq
    mixed = _dot(u.astype(BF16), wo_ref[...])
    x1 = x_ref[0] + mod_ref[0, 2:3, :] * _rms(mixed, gpost_ref[...])
    x1_ref[0] = x1
    h2 = _rms(x1, gpre_ref[...]) * (1.0 + mod_ref[0, 4:5, :]) + mod_ref[0, 3:4, :]
    hi = h2.astype(BF16)
    lo = (h2 - hi.astype(F32)).astype(BF16)
    h2_ref[0] = h2
    lg = _dot(hi, wrh_ref[...]) + _dot(lo, wrh_ref[...]) + _dot(hi, wrl_ref[...]) + br_ref[...]

    lane = lax.broadcasted_iota(jnp.int32, lg.shape, 1)
    lanef = lane.astype(F32)
    no_lane = float(LANES)
    is_g = lane < N_EXPERT_GROUPS
    gl = jnp.where(is_g, lg, NEG)
    gmax = jnp.max(gl, axis=-1, keepdims=True)
    pg_top = 1.0 / jnp.sum(jnp.where(is_g, jnp.exp(gl - gmax), 0.0), axis=-1, keepdims=True)
    g_idx = jnp.min(jnp.where(is_g & (gl == gmax), lanef, no_lane), axis=-1, keepdims=True)
    in_grp = ((lane >= N_EXPERT_GROUPS) & (lane < N_EXPERT_GROUPS + N_EXPERTS)
              & (((lane - N_EXPERT_GROUPS) // EXPERTS_PER_GROUP).astype(F32) == g_idx))
    le = jnp.where(in_grp, lg, NEG)
    m1 = jnp.max(le, axis=-1, keepdims=True)
    i1 = jnp.min(jnp.where(in_grp & (le == m1), lanef, no_lane), axis=-1, keepdims=True)
    rest = in_grp & (lanef != i1)
    le2 = jnp.where(rest, lg, NEG)
    m2 = jnp.max(le2, axis=-1, keepdims=True)
    i2 = jnp.min(jnp.where(rest & (le2 == m2), lanef, no_lane), axis=-1, keepdims=True)
    e21 = jnp.exp(m2 - m1)
    w1 = pg_top / (1.0 + e21)
    w2 = w1 * e21
    pick1 = lanef == i1
    pick2 = lanef == i2
    onehot = jnp.where(pick1 | pick2, 1.0, 0.0)
    before = cnt_ref[...] + _dot(stri_ref[...], onehot.astype(BF16))
    rank1 = jnp.sum(jnp.where(pick1, before, 0.0), axis=-1, keepdims=True)
    rank2 = jnp.sum(jnp.where(pick2, before, 0.0), axis=-1, keepdims=True)
    cnt_ref[...] = cnt_ref[...] + jnp.sum(onehot, axis=0, keepdims=True)
    fields = [i1 - N_EXPERT_GROUPS, i2 - N_EXPERT_GROUPS, rank1, rank2, w1, w2]
    rt = jnp.zeros(lg.shape, F32)
    for k, f in enumerate(fields):
        rt = jnp.where(lane == k, f, rt)
    rt_ref[0] = rt


def _merge(ya, yb, mg, x, mod, gpost, gpre, wa, wb, wo, wrh, wrl, br, stri):
    B, S, D = x.shape
    tm = MERGE_TILE
    c2 = lambda b, i: (0, 0)
    row = lambda w: pl.BlockSpec((1, tm, w), lambda b, i: (b, i, 0))
    return pl.pallas_call(
        _merge_kernel,
        grid=(B, S // tm),
        in_specs=[
            row(NSA_W), row(FOX_W), row(2 * D), row(D),
            pl.BlockSpec((1, 6, D), lambda b, i: (b, 0, 0)),
            pl.BlockSpec((1, D), c2), pl.BlockSpec((1, D), c2),
            pl.BlockSpec((NSA_W, D), c2), pl.BlockSpec((FOX_W, D), c2), pl.BlockSpec((D, D), c2),
            pl.BlockSpec((D, LANES), c2), pl.BlockSpec((D, LANES), c2), pl.BlockSpec((1, LANES), c2),
            pl.BlockSpec((tm, tm), c2),
        ],
        out_specs=[row(D), row(D), row(LANES), pl.BlockSpec((1, LANES), c2)],
        out_shape=[
            jax.ShapeDtypeStruct((B, S, D), F32),
            jax.ShapeDtypeStruct((B, S, D), F32),
            jax.ShapeDtypeStruct((B, S, LANES), F32),
            jax.ShapeDtypeStruct((1, LANES), F32),
        ],
        compiler_params=_cparams(("arbitrary", "arbitrary")),
        name="merge",
    )(ya, yb, mg, x, mod, gpost, gpre, wa, wb, wo, wrh, wrl, br, stri)


def _expert_kernel(be_ref, na_ref, tok_ref, h_hbm, wg_ref, wu_ref, wd_ref, o_ref,
                   x_even, x_odd, wg_b, wu_b, wd_b, sem):
    i = pl.program_id(0)
    n_active = na_ref[0]
    last_block = pl.num_programs(0) - 1
    bufs = (x_even, x_odd)

    def row_copy(blk, r, sl):
        tok = tok_ref[blk * MOE_TILE + r]
        return pltpu.make_async_copy(h_hbm.at[pl.ds(tok, 1)], bufs[sl].at[pl.ds(r, 1)], sem.at[sl])

    def wait_rows(blk, sl):
        del blk
        pltpu.make_async_copy(h_hbm.at[pl.ds(0, MOE_TILE)], bufs[sl], sem.at[sl]).wait()

    @pl.when(i == 0)
    def _():
        def body(r, carry):
            row_copy(0, r, 0).start()
            return carry
        lax.fori_loop(0, MOE_TILE, body, 0, unroll=8)

    @pl.when((i == 0) | (be_ref[i] != be_ref[jnp.maximum(i - 1, 0)]))
    def _():
        wg_b[...] = wg_ref[0].astype(BF16)
        wu_b[...] = wu_ref[0].astype(BF16)
        wd_b[...] = wd_ref[0].astype(BF16)

    def step(sl):
        wait_rows(i, sl)
        nxt = jnp.minimum(i + 1, last_block)
        for r in range(MOE_TILE):
            row_copy(nxt, r, 1 - sl).start(priority=r % 2)
        x = bufs[sl][...].astype(BF16)
        gate = _dot(x, wg_b[...])
        up = _dot(x, wu_b[...])
        mid = (gate * jax.nn.sigmoid(gate) * up).astype(BF16)
        o_ref[...] = _dot(mid, wd_b[...])

        @pl.when(i == n_active - 1)
        def _():
            wait_rows(nxt, 1 - sl)

    for sl in range(2):
        pl.when((i % 2 == sl) & (i < n_active))(functools.partial(step, sl))

    @pl.when(i >= n_active)
    def _():
        o_ref[...] = jnp.zeros(o_ref.shape, o_ref.dtype)


def _experts(block_expert, n_active, buf_tok, h2, wg, wu, wd):
    cap = buf_tok.shape[0]
    D = D_MODEL
    nblk = cap // MOE_TILE
    grid_spec = pltpu.PrefetchScalarGridSpec(
        num_scalar_prefetch=3,
        grid=(nblk,),
        in_specs=[
            pl.BlockSpec(memory_space=pl.ANY),
            pl.BlockSpec((1, D, D_EXPERT), lambda i, be, na, tok: (be[i], 0, 0)),
            pl.BlockSpec((1, D, D_EXPERT), lambda i, be, na, tok: (be[i], 0, 0)),
            pl.BlockSpec((1, D_EXPERT, D), lambda i, be, na, tok: (be[i], 0, 0)),
        ],
        out_specs=pl.BlockSpec((MOE_TILE, D), lambda i, be, na, tok: (i, 0)),
        scratch_shapes=[
            pltpu.VMEM((MOE_TILE, D), F32),
            pltpu.VMEM((MOE_TILE, D), F32),
            pltpu.VMEM((D, D_EXPERT), BF16),
            pltpu.VMEM((D, D_EXPERT), BF16),
            pltpu.VMEM((D_EXPERT, D), BF16),
            pltpu.SemaphoreType.DMA((2,)),
        ],
    )
    return pl.pallas_call(
        _expert_kernel,
        grid_spec=grid_spec,
        out_shape=jax.ShapeDtypeStruct((cap, D), F32),
        compiler_params=_cparams(("arbitrary",)),
        name="experts",
    )(block_expert, n_active, buf_tok, h2, wg, wu, wd)


def _final_kernel(dest_ref, x1_ref, rt_ref, mod_ref, g_ref, y_hbm, o_ref, a_even, b_even, a_odd, b_odd, sem):
    j = pl.program_id(0)
    last_tile = pl.num_programs(0) - 1
    tm = o_ref.shape[0]
    bufs = ((a_even, b_even), (a_odd, b_odd))

    def row_copy(tile, r, k, sl):
        row = dest_ref[(tile * tm + r) * EXPERT_TOP_K + k]
        return pltpu.make_async_copy(y_hbm.at[pl.ds(row, 1)], bufs[sl][k].at[pl.ds(r, 1)], sem.at[sl])

    def wait_rows(tile, sl):
        del tile
        for k in range(EXPERT_TOP_K):
            pltpu.make_async_copy(y_hbm.at[pl.ds(0, tm)], bufs[sl][k], sem.at[sl]).wait()

    @pl.when(j == 0)
    def _():
        def body(r, carry):
            for k in range(EXPERT_TOP_K):
                row_copy(0, r, k, 0).start()
            return carry
        lax.fori_loop(0, tm, body, 0, unroll=4)

    def step(sl):
        wait_rows(j, sl)
        nxt = jnp.minimum(j + 1, last_tile)
        for r in range(tm):
            for k in range(EXPERT_TOP_K):
                row_copy(nxt, r, k, 1 - sl).start(priority=k)
        rt = rt_ref[...]
        lane = lax.broadcasted_iota(jnp.int32, rt.shape, 1)
        w0 = jnp.sum(jnp.where(lane == 4, rt, 0.0), axis=-1, keepdims=True)
        w1 = jnp.sum(jnp.where(lane == 5, rt, 0.0), axis=-1, keepdims=True)
        y = w0 * bufs[sl][0][...] + w1 * bufs[sl][1][...]
        o_ref[...] = x1_ref[...] + mod_ref[0, 5:6, :] * _rms(y, g_ref[...])

        @pl.when(j == last_tile)
        def _():
            wait_rows(nxt, 1 - sl)

    for sl in range(2):
        pl.when(j % 2 == sl)(functools.partial(step, sl))


def _final(dest, x1, rt, mod, g, yb, tiles_per_batch):
    T, D = x1.shape
    tm = MERGE_TILE
    grid_spec = pltpu.PrefetchScalarGridSpec(
        num_scalar_prefetch=1,
        grid=(T // tm,),
        in_specs=[
            pl.BlockSpec((tm, D), lambda j, d: (j, 0)),
            pl.BlockSpec((tm, LANES), lambda j, d: (j, 0)),
            pl.BlockSpec((1, 6, D), lambda j, d: (j // tiles_per_batch, 0, 0)),
            pl.BlockSpec((1, D), lambda j, d: (0, 0)),
            pl.BlockSpec(memory_space=pl.ANY),
        ],
        out_specs=pl.BlockSpec((tm, D), lambda j, d: (j, 0)),
        scratch_shapes=[pltpu.VMEM((tm, D), F32)] * 4 + [pltpu.SemaphoreType.DMA((2,))],
    )
    return pl.pallas_call(
        _final_kernel,
        grid_spec=grid_spec,
        out_shape=jax.ShapeDtypeStruct((T, D), F32),
        compiler_params=_cparams(("arbitrary",)),
        name="final",
    )(dest, x1, rt, mod, g, yb)


def _overlap_matrix():
    n = np.arange(N_CMP_PAD)[:, None]
    j = np.arange(LANES)[None, :]
    start = n * CMP_STRIDE
    ov = (start < j * SEL_LEN + SEL_LEN) & (start + CMP_LEN - 1 >= j * SEL_LEN) & (n < N_CMP_PAD - 1)
    return jnp.asarray(ov.T.astype(np.float32), dtype=BF16)


def _pad_cols(w, width=LANES):
    return jnp.pad(w, ((0, 0), (0, width - w.shape[1])))


def _dispatch_plan(rt, cnt, T):
    expert = rt[:, 0:2].astype(jnp.int32)
    rank = rt[:, 2:4].astype(jnp.int32)
    weight = rt[:, 4:6]
    counts = cnt[0, N_EXPERT_GROUPS:N_EXPERT_GROUPS + N_EXPERTS].astype(jnp.int32)
    padded = (counts + MOE_TILE - 1) // MOE_TILE * MOE_TILE
    pad_end = jnp.cumsum(padded)
    pad_start = pad_end - padded
    onehot = expert[:, :, None] == jnp.arange(N_EXPERTS)[None, None, :]
    dest = jnp.sum(jnp.where(onehot, pad_start[None, None, :], 0), axis=-1) + rank
    A = T * EXPERT_TOP_K
    cap = -(-(A + N_EXPERTS * (MOE_TILE - 1)) // MOE_TILE) * MOE_TILE
    nblk = cap // MOE_TILE
    n_active = (pad_end[-1] // MOE_TILE).astype(jnp.int32)
    blk = jnp.arange(nblk) * MOE_TILE
    block_expert = jnp.minimum(jnp.sum(pad_end[None, :] <= blk[:, None], axis=1), N_EXPERTS - 1)
    last = jnp.max(jnp.where(jnp.arange(nblk) < n_active, block_expert, 0))
    block_expert = jnp.where(jnp.arange(nblk) < n_active, block_expert, last).astype(jnp.int32)
    tok = jnp.arange(A, dtype=jnp.int32) // EXPERT_TOP_K
    buf_tok = jnp.zeros((cap,), jnp.int32).at[dest.reshape(A)].set(tok)
    return weight, dest, buf_tok, block_expert, n_active.reshape(1)


def kernel(x, c, w_ada, b_ada, g_pre_mix, g_post_mix, g_pre_ffn, g_post_ffn, w_in, b_forget,
           cmp_pe_k, cmp_w1_k, cmp_w2_k, cmp_pe_v, cmp_w1_v, cmp_w2_v,
           w_o_nsa, w_o_fox, w_out, w_router_group, b_router_group, w_router_expert, b_router_expert,
           w_exp_gate, w_exp_up, w_exp_down):
    B, S, D = x.shape
    T = B * S
    depth = w_ada.shape[0]
    ov = _overlap_matrix()
    tri = jnp.asarray(np.tril(np.ones((IN_TILE, IN_TILE), np.float32)), dtype=BF16)
    stri = jnp.asarray(np.tril(np.ones((MERGE_TILE, MERGE_TILE), np.float32), -1), dtype=BF16)
    row_feat = _row_features(S)
    placement = _placement()
    cmp_ext = _cmp_key_ext()
    for l in range(depth):
        mod = (jax.nn.silu(c) @ w_ada[l] + b_ada[l]).reshape(B, 6, D)
        w_qa, w_kva, w_gl, w_fox, w_f, w_mg = jnp.split(w_in[l], IN_SPLITS, axis=-1)
        w_big = jnp.concatenate([w_qa, w_kva, w_fox, w_mg], axis=1).astype(BF16)
        w_small = _pad_cols(jnp.concatenate([w_gl, w_f], axis=1)).astype(BF16)
        bf_pad = jnp.pad(b_forget[l], (F_LANE, LANES - F_LANE - FOX_HEADS)).reshape(1, LANES)
        qa, ckv, ksl, nkv, fq, fk, fv, mg, sm = _inproj(
            x, mod, g_pre_mix[l].reshape(1, D), w_big, w_small, bf_pad, tri, row_feat, placement)

        half = CMP_LEN // 2
        pe = jnp.stack([cmp_pe_k[l], cmp_pe_v[l]]).reshape(2, 2, 1, half * HEAD_DIM)
        w1 = jnp.stack([cmp_w1_k[l], cmp_w1_v[l]]).reshape(2, 2, half * HEAD_DIM, HEAD_DIM).astype(BF16)
        w2 = jnp.pad(jnp.stack([cmp_w2_k[l], cmp_w2_v[l]]), ((0, 0), (0, 0), (0, LANES - HEAD_DIM))).astype(BF16)
        kvc = _compress(ckv.reshape(B, 4, S // CMP_STRIDE, CMP_STRIDE * HEAD_DIM), pe, w1, w2, cmp_ext)
        ocg, selb, flags = _cmp_attention(qa, kvc, sm, ov)
        nq = S // SW_TILE
        per_tile = K_TILE // SEL_LEN
        tile_any = jnp.max(flags.reshape(B, NSA_KV_GROUPS, nq, SW_TILE // Q_TILE, MAX_TILES, per_tile), axis=(3, 5))
        tile_id = jnp.arange(MAX_TILES)
        diag = (jnp.arange(nq) // (K_TILE // SW_TILE))[:, None]
        active = (tile_any > 0) & (tile_id < diag)
        slot = jnp.cumsum(active, axis=-1) - 1
        hit = active[..., :, None] & (slot[..., :, None] == tile_id)
        tile_list = jnp.sum(jnp.where(hit, tile_id[:, None], 0), axis=-2).astype(jnp.int32).reshape(-1)
        tile_count = jnp.sum(active, axis=-1).astype(jnp.int32).reshape(-1)
        y_a = _selwin_attention(tile_list, tile_count, qa, ksl, nkv, selb, ocg, sm)

        y_b = _fox_attention(fq, fk, fv)

        w_r = _pad_cols(jnp.concatenate([w_router_group[l], w_router_expert[l]], axis=1))
        w_rh = w_r.astype(BF16)
        w_rl = (w_r - w_rh.astype(F32)).astype(BF16)
        b_r = _pad_cols(jnp.concatenate([b_router_group[l], b_router_expert[l]]).reshape(1, -1))
        x1, h2, rt, cnt = _merge(y_a, y_b, mg, x, mod, g_post_mix[l].reshape(1, D), g_pre_ffn[l].reshape(1, D),
                                 w_o_nsa[l].astype(BF16), w_o_fox[l].astype(BF16), w_out[l].astype(BF16),
                                 w_rh, w_rl, b_r, stri)

        weight, dest, buf_tok, block_expert, n_active = _dispatch_plan(rt.reshape(T, LANES), cnt, T)
        yb = _experts(block_expert, n_active, buf_tok, h2.reshape(T, D), w_exp_gate[l], w_exp_up[l], w_exp_down[l])
        x = _final(dest.reshape(T * EXPERT_TOP_K), x1.reshape(T, D), rt.reshape(T, LANES), mod,
                   g_post_ffn[l].reshape(1, D), yb, S // MERGE_TILE).reshape(B, S, D)
    return x
```

```python
import functools

import ml_dtypes
import numpy as np
import jax
import jax.numpy as jnp
from jax import lax
from jax.experimental import pallas as pl
from jax.experimental.pallas import tpu as pltpu

D_MODEL = 1024
HEAD_DIM = 64
NSA_HEADS = 8
NSA_KV_GROUPS = 2
NSA_HPG = NSA_HEADS // NSA_KV_GROUPS
FOX_HEADS = 8
CMP_LEN = 32
CMP_STRIDE = 16
SEL_LEN = 64
N_SEL = 16
WINDOW = 512
N_EXPERT_GROUPS = 4
EXPERTS_PER_GROUP = 8
N_EXPERTS = N_EXPERT_GROUPS * EXPERTS_PER_GROUP
EXPERT_TOP_K = 2
D_EXPERT = D_MODEL // 2
NORM_EPS = 1e-6
NEG = -1e30
FORCE = 1e9
LOG2E = 1.4426950408889634

NSA_W = NSA_HEADS * HEAD_DIM
NSA_KV_W = NSA_KV_GROUPS * HEAD_DIM
FOX_W = FOX_HEADS * HEAD_DIM
IN_SIZES = (NSA_W, 6 * NSA_KV_W, 3 * NSA_HEADS, 3 * FOX_W, FOX_HEADS, 2 * D_MODEL)
IN_SPLITS = tuple(int(v) for v in np.cumsum(IN_SIZES)[:-1])

LANES = 128
Q_TILE = 128
K_TILE = 256
SW_TILE = 256
N_CMP_PAD = 512
MOE_TILE = 256
IN_TILE = 512
MERGE_TILE = 256
FOX_HPS = 4
CMP_SUB = 2
MAX_TILES = 32
GATHER_AHEAD = 2
VMEM_LIMIT = 56 * 1024 * 1024

F_LANE = 3 * NSA_HEADS
U_LANE = 64
ONE_LANE = 88
A_LANE = 89
B_LANE = 90
EXT = HEAD_DIM
G_FQ, G_FK, G_NQ, G_NK, N_GROUPS = 0, 8, 16, 24, 25

F32 = jnp.float32
BF16 = jnp.bfloat16


def _dot(a, b):
    return jnp.dot(a, b, preferred_element_type=F32)


def _dot_nt(a, b):
    return lax.dot_general(a, b, (((1,), (1,)), ((), ())), preferred_element_type=F32)


def _rms(x, g):
    return x * lax.rsqrt(jnp.mean(x * x, axis=-1, keepdims=True) + NORM_EPS) * g


def _cparams(sem):
    return pltpu.CompilerParams(dimension_semantics=sem, vmem_limit_bytes=VMEM_LIMIT)


def _split3(x):
    hi = x.astype(BF16).astype(F32)
    r = x - hi
    mid = r.astype(BF16).astype(F32)
    lo = (r - mid).astype(BF16).astype(F32)
    return hi, mid, lo


def _np_split3(x):
    x = np.asarray(x, np.float32)
    hi = x.astype(ml_dtypes.bfloat16).astype(np.float32)
    r = x - hi
    mid = r.astype(ml_dtypes.bfloat16).astype(np.float32)
    lo = (r - mid).astype(ml_dtypes.bfloat16).astype(np.float32)
    return hi, mid, lo


def _alibi_c():
    slopes = np.exp2(-8.0 * np.arange(1, NSA_HEADS + 1, dtype=np.float32) / NSA_HEADS).astype(np.float32)
    return slopes * np.float32(LOG2E)


def _row_features(S):
    t = np.arange(S, dtype=np.float32)
    c = _alibi_c()
    rs = np.zeros((S, LANES), np.float32)
    for h in range(NSA_HEADS):
        for j, term in enumerate(_np_split3(c[h] * t)):
            rs[:, U_LANE + 8 * j + h] = -term
    rs[:, ONE_LANE] = 1.0
    rs[:, A_LANE] = np.floor(t / LANES)
    rs[:, B_LANE] = t % LANES
    return jnp.asarray(rs, dtype=BF16)


def _placement():
    c = _alibi_c()
    p = np.zeros((LANES, N_GROUPS * LANES), np.float32)
    for h in range(FOX_HEADS):
        q0 = (G_FQ + h) * LANES + EXT
        k0 = (G_FK + h) * LANES + EXT
        for j in range(3):
            p[ONE_LANE, q0 + j] = -1.0
            p[F_LANE + 8 * j + h, q0 + 3 + j] = 1.0
            p[F_LANE + 8 * j + h, k0 + j] = 1.0
            p[ONE_LANE, k0 + 3 + j] = 1.0
    for h in range(NSA_HEADS):
        q0 = (G_NQ + h) * LANES + EXT
        c128 = _np_split3(c[h] * np.float32(LANES))
        c1 = _np_split3(c[h])
        for j in range(3):
            p[U_LANE + 8 * j + h, q0 + j] = 1.0
            p[ONE_LANE, q0 + 3 + j] = c128[j]
            p[ONE_LANE, q0 + 6 + j] = c1[j]
    k0 = G_NK * LANES + EXT
    for j in range(3):
        p[ONE_LANE, k0 + j] = 1.0
        p[A_LANE, k0 + 3 + j] = 1.0
        p[B_LANE, k0 + 6 + j] = 1.0
    return jnp.asarray(p, dtype=BF16)


def _cmp_key_ext():
    pos = np.arange(N_CMP_PAD, dtype=np.float32) * CMP_STRIDE + (CMP_LEN - 1)
    e = np.zeros((2, N_CMP_PAD, LANES), np.float32)
    for j in range(3):
        e[0, :, EXT + j] = 1.0
        e[0, :, EXT + 3 + j] = np.floor(pos / LANES)
        e[0, :, EXT + 6 + j] = pos % LANES
    return jnp.asarray(e, dtype=BF16)


def _inproj_kernel(x_ref, mod_ref, g_ref, wb_ref, ws_ref, bf_ref, tri_ref, rs_ref, p_ref,
                   qa_ref, ckv_ref, ksl_ref, nkv_ref, fq_ref, fk_ref, fv_ref, mg_ref, sm_ref, carry_sc):
    i = pl.program_id(1)
    tm = x_ref.shape[1]
    x = x_ref[0]
    h = _rms(x, g_ref[...]) * (1.0 + mod_ref[0, 1:2, :]) + mod_ref[0, 0:1, :]
    hb = h.astype(BF16)
    lane = lax.broadcasted_iota(jnp.int32, (tm, LANES), 1)
    lower = lane < HEAD_DIM
    ones_col = (lane == EXT).astype(F32)

    z = _dot(hb, ws_ref[...]) + bf_ref[...]
    logsig = jnp.minimum(z, 0.0) - jnp.log1p(jnp.exp(-jnp.abs(z)))
    sm_ref[0] = jnp.where(lane < F_LANE, jax.nn.sigmoid(z), logsig)

    @pl.when(i == 0)
    def _():
        carry_sc[...] = jnp.zeros(carry_sc.shape, F32)

    is_f = (lane >= F_LANE) & (lane < F_LANE + FOX_HEADS)
    l_hi, l_mid, l_lo = _split3(jnp.where(is_f, logsig, 0.0))
    tri = tri_ref[...]
    cum = carry_sc[...] + _dot(tri, l_hi.astype(BF16)) + _dot(tri, l_mid.astype(BF16)) + _dot(tri, l_lo.astype(BF16))
    carry_sc[...] = cum[tm - 1:tm, :]
    f_hi, f_mid, f_lo = _split3(cum * LOG2E)
    feat = (f_hi + pltpu.roll(f_mid, 8, 1) + pltpu.roll(f_lo, 16, 1) + rs_ref[...].astype(F32)).astype(BF16)

    def ext(group):
        return _dot(feat, p_ref[:, group * LANES:(group + 1) * LANES])

    def piece(acc, idx, extra):
        pair = acc[:, (idx // 2) * LANES:(idx // 2 + 1) * LANES]
        if idx % 2:
            pair = pltpu.roll(pair, HEAD_DIM, 1)
        return jnp.where(lower, pair, extra).astype(BF16)

    qscale = (HEAD_DIM ** -0.5) * LOG2E
    acc = _dot(hb, wb_ref[:, 0:NSA_W]) * qscale
    for hd in range(NSA_HEADS):
        qa_ref[0, hd] = piece(acc, hd, ext(G_NQ + hd))
    off = NSA_W
    acc = _dot(hb, wb_ref[:, off:off + 6 * NSA_KV_W])
    for pc in range(4):
        ckv_ref[0, pc] = acc[:, pc * HEAD_DIM:(pc + 1) * HEAD_DIM].astype(BF16)
    ext_k = ext(G_NK)
    t = i * tm + lax.broadcasted_iota(jnp.int32, (tm, LANES), 0)
    block_onehot = (lane == t // SEL_LEN).astype(BF16)
    for g in range(NSA_KV_GROUPS):
        ksl_ref[0, g, :, 0:LANES] = piece(acc, 4 + g, ext_k)
        ksl_ref[0, g, :, LANES:2 * LANES] = block_onehot
        nkv_ref[0, g] = piece(acc, 6 + g, ones_col)
        nkv_ref[0, 2 + g] = piece(acc, 8 + g, ext_k)
        nkv_ref[0, 4 + g] = piece(acc, 10 + g, ones_col)
    off += 6 * NSA_KV_W
    acc = _dot(hb, wb_ref[:, off:off + FOX_W]) * qscale
    for hd in range(FOX_HEADS):
        fq_ref[0, hd] = piece(acc, hd, ext(G_FQ + hd))
    off += FOX_W
    acc = _dot(hb, wb_ref[:, off:off + FOX_W])
    for hd in range(FOX_HEADS):
        fk_ref[0, hd] = piece(acc, hd, ext(G_FK + hd))
    off += FOX_W
    acc = _dot(hb, wb_ref[:, off:off + FOX_W])
    for hd in range(FOX_HEADS):
        fv_ref[0, hd] = piece(acc, hd, ones_col)
    off += FOX_W
    for c in range(4):
        acc = _dot(hb, wb_ref[:, off + c * 512: off + (c + 1) * 512])
        mg_ref[0, :, c * 512:(c + 1) * 512] = jax.nn.sigmoid(acc).astype(BF16)


def _inproj(x, mod, g, wb, ws, bfp, tri, rs, pm):
    B, S, D = x.shape
    tm = IN_TILE
    nb = wb.shape[1]
    const2 = lambda b, i: (0, 0)
    heads = lambda n: pl.BlockSpec((1, n, tm, LANES), lambda b, i: (b, 0, i, 0))
    hshape = lambda n: jax.ShapeDtypeStruct((B, n, S, LANES), BF16)
    return pl.pallas_call(
        _inproj_kernel,
        grid=(B, S // tm),
        in_specs=[
            pl.BlockSpec((1, tm, D), lambda b, i: (b, i, 0)),
            pl.BlockSpec((1, 6, D), lambda b, i: (b, 0, 0)),
            pl.BlockSpec((1, D), const2),
            pl.BlockSpec((D, nb), const2),
            pl.BlockSpec((D, LANES), const2),
            pl.BlockSpec((1, LANES), const2),
            pl.BlockSpec((tm, tm), const2),
            pl.BlockSpec((tm, LANES), lambda b, i: (i, 0)),
            pl.BlockSpec((LANES, N_GROUPS * LANES), const2),
        ],
        out_specs=[
            heads(NSA_HEADS),
            pl.BlockSpec((1, 4, tm, HEAD_DIM), lambda b, i: (b, 0, i, 0)),
            pl.BlockSpec((1, NSA_KV_GROUPS, tm, 2 * LANES), lambda b, i: (b, 0, i, 0)),
            heads(6), heads(FOX_HEADS), heads(FOX_HEADS), heads(FOX_HEADS),
            pl.BlockSpec((1, tm, 2 * D), lambda b, i: (b, i, 0)),
            pl.BlockSpec((1, tm, LANES), lambda b, i: (b, i, 0)),
        ],
        out_shape=[
            hshape(NSA_HEADS),
            jax.ShapeDtypeStruct((B, 4, S, HEAD_DIM), BF16),
            jax.ShapeDtypeStruct((B, NSA_KV_GROUPS, S, 2 * LANES), BF16),
            hshape(6), hshape(FOX_HEADS), hshape(FOX_HEADS), hshape(FOX_HEADS),
            jax.ShapeDtypeStruct((B, S, 2 * D), BF16),
            jax.ShapeDtypeStruct((B, S, LANES), F32),
        ],
        scratch_shapes=[pltpu.VMEM((1, LANES), F32)],
        compiler_params=_cparams(("parallel", "arbitrary")),
        name="inproj",
    )(x, mod, g, wb, ws, bfp, tri, rs, pm)


def _compress_kernel(x_ref, pe_ref, w1_ref, w2_ref, e_ref, o_ref):
    x = x_ref[0, 0].astype(F32)
    x_lo = (x + pe_ref[0, 0]).astype(BF16)
    x_hi = (x + pe_ref[0, 1]).astype(BF16)
    y_lo = _dot(x_lo, w1_ref[0, 0])
    y_hi = _dot(x_hi, w1_ref[0, 1])
    n = y_hi.shape[0]
    hid = y_lo + pltpu.roll(y_hi, n - 1, 0)
    hid = jax.nn.gelu(hid)
    o_ref[0, 0] = (_dot(hid.astype(BF16), w2_ref[0]) + e_ref[0].astype(F32)).astype(BF16)


def _compress(kv_rows, pe, w1, w2, e):
    B = kv_rows.shape[0]
    R, C = kv_rows.shape[2], kv_rows.shape[3]
    return pl.pallas_call(
        _compress_kernel,
        grid=(B, 4),
        in_specs=[
            pl.BlockSpec((1, 1, R, C), lambda b, p: (b, p, 0, 0)),
            pl.BlockSpec((1, 2, 1, C), lambda b, p: (p // 2, 0, 0, 0)),
            pl.BlockSpec((1, 2, C, HEAD_DIM), lambda b, p: (p // 2, 0, 0, 0)),
            pl.BlockSpec((1, HEAD_DIM, LANES), lambda b, p: (p // 2, 0, 0)),
            pl.BlockSpec((1, R, LANES), lambda b, p: (p // 2, 0, 0)),
        ],
        out_specs=pl.BlockSpec((1, 1, R, LANES), lambda b, p: (b, p, 0, 0)),
        out_shape=jax.ShapeDtypeStruct((B, 4, R, LANES), BF16),
        compiler_params=_cparams(("parallel", "parallel")),
        name="compress",
    )(kv_rows, pe, w1, w2, e)


def _gate_rows(sm, g, branch):
    col = lax.broadcasted_iota(jnp.int32, sm.shape, 1)
    parts = []
    for hl in range(NSA_HPG):
        want = 3 * (NSA_HPG * g + hl) + branch
        parts.append(jnp.sum(jnp.where(col == want, sm, 0.0), axis=-1, keepdims=True))
    return jnp.concatenate(parts, axis=0)


def _head_tile(y):
    n = y.shape[0] // NSA_HPG
    lane = lax.broadcasted_iota(jnp.int32, (n, LANES), 1)
    hs = [y[i * n:(i + 1) * n] for i in range(NSA_HPG)]
    pairs = [jnp.where(lane < HEAD_DIM, hs[2 * i], pltpu.roll(hs[2 * i + 1], HEAD_DIM, 1)) for i in range(2)]
    return jnp.concatenate(pairs, axis=1)


def _cmp_kernel(q_ref, kc_ref, vc_ref, sm_ref, ovt_ref, oc_ref, selb_ref, flag_ref):
    g = pl.program_id(1)
    for sub in range(CMP_SUB):
        rows = pl.ds(sub * Q_TILE, Q_TILE)
        q0 = (pl.program_id(2) * CMP_SUB + sub) * Q_TILE
        q = q_ref[0, :, rows, :].reshape(NSA_HPG * Q_TILE, LANES)
        oc, selb, flag = _cmp_tile(q, kc_ref[0, 0], vc_ref[0, 0], sm_ref[0, rows, :], ovt_ref[...], g, q0)
        oc_ref[0, rows, :] = oc
        selb_ref[0, 0, rows, :] = selb
        flag_ref[0, 0, sub] = flag


def _cmp_tile(q, kc, vc, sm, ovt, g, q0):
    s = _dot_nt(q, kc)
    r = lax.broadcasted_iota(jnp.int32, (NSA_HPG * Q_TILE, 1), 0) % Q_TILE
    n = lax.broadcasted_iota(jnp.int32, (1, N_CMP_PAD), 1)
    dc = (q0 + r) - (n * CMP_STRIDE + (CMP_LEN - 1))
    mask = (dc >= 0) & (n < N_CMP_PAD - 1)
    l = jnp.where(mask, s, NEG)
    m = jnp.max(l, axis=-1, keepdims=True)
    e = jnp.where(mask, jnp.exp2(l - m), 0.0)
    pc = e / jnp.maximum(jnp.sum(e, axis=-1, keepdims=True), 1e-30)
    oc = _dot(pc.astype(BF16), vc)
    oc = _head_tile(oc * _gate_rows(sm, g, 0))
    ps = pc[0:Q_TILE]
    for i in range(1, NSA_HPG):
        ps = ps + pc[i * Q_TILE:(i + 1) * Q_TILE]
    ps_hi = ps.astype(BF16)
    ps_lo = (ps - ps_hi.astype(F32)).astype(BF16)
    imp = _dot_nt(ovt, ps_hi) + _dot_nt(ovt, ps_lo)
    j = lax.broadcasted_iota(jnp.int32, imp.shape, 0)
    jf = j.astype(F32)
    t = q0 + lax.broadcasted_iota(jnp.int32, (1, Q_TILE), 1)
    cur = t // SEL_LEN
    forced = (j == 0) | (j == cur) | (j == cur - 1)
    v = jnp.where(j > cur, -FORCE, jnp.where(forced, FORCE, imp))
    sel = jnp.zeros(imp.shape, jnp.bool_)
    for _ in range(N_SEL):
        mx = jnp.max(v, axis=0, keepdims=True)
        idx = jnp.min(jnp.where(v == mx, jf, float(LANES)), axis=0, keepdims=True)
        pick = jf == idx
        sel = sel | pick
        v = jnp.where(pick, -3e38, v)
    live_t = jnp.where(sel & (j <= cur), 1.0, 0.0).astype(BF16)
    eye = (lax.broadcasted_iota(jnp.int32, imp.shape, 0) == lax.broadcasted_iota(jnp.int32, imp.shape, 1))
    live = _dot_nt(eye.astype(BF16), live_t)
    selb = jnp.where(live > 0.5, 0.0, NEG).astype(BF16)
    return oc, selb, jnp.max(live, axis=0, keepdims=True).astype(jnp.int32)


def _cmp_attention(qa, kvc, sm, ov):
    B, H, S, _ = qa.shape
    G = NSA_KV_GROUPS
    nq = S // Q_TILE
    qt = CMP_SUB * Q_TILE
    return pl.pallas_call(
        _cmp_kernel,
        grid=(B, G, nq // CMP_SUB),
        in_specs=[
            pl.BlockSpec((1, NSA_HPG, qt, LANES), lambda b, g, i: (b, g, i, 0)),
            pl.BlockSpec((1, 1, N_CMP_PAD, LANES), lambda b, g, i: (b, g, 0, 0)),
            pl.BlockSpec((1, 1, N_CMP_PAD, LANES), lambda b, g, i: (b, 2 + g, 0, 0)),
            pl.BlockSpec((1, qt, LANES), lambda b, g, i: (b, i, 0)),
            pl.BlockSpec((LANES, N_CMP_PAD), lambda b, g, i: (0, 0)),
        ],
        out_specs=[
            pl.BlockSpec((1, qt, NSA_HPG * HEAD_DIM), lambda b, g, i: (b, i, g)),
            pl.BlockSpec((1, 1, qt, LANES), lambda b, g, i: (b, g, i, 0)),
            pl.BlockSpec((1, 1, CMP_SUB, 1, LANES), lambda b, g, i: (b, g, i, 0, 0)),
        ],
        out_shape=[
            jax.ShapeDtypeStruct((B, S, NSA_W), F32),
            jax.ShapeDtypeStruct((B, G, S, LANES), BF16),
            jax.ShapeDtypeStruct((B, G, nq, 1, LANES), jnp.int32),
        ],
        compiler_params=_cparams(("parallel", "parallel", "parallel")),
        name="cmp_attention",
    )(qa, kvc, kvc, sm, ov)


def _online_update(s, v, m_ref, acc_ref):
    m_old = m_ref[...]
    m_new = jnp.maximum(m_old, jnp.max(s, axis=-1, keepdims=True))
    chunks = [s[:, c * LANES:(c + 1) * LANES] - m_new for c in range(s.shape[1] // LANES)]
    p = jnp.exp2(jnp.concatenate(chunks, axis=1))
    acc_ref[...] = jnp.exp2(m_old - m_new) * acc_ref[...] + _dot(p.astype(BF16), v)
    m_ref[...] = m_new


def _normalized(acc):
    return acc / jnp.maximum(acc[:, EXT:EXT + 1], 1e-30)


def _attend_once(s, v):
    m = jnp.broadcast_to(jnp.max(s, axis=-1, keepdims=True), (s.shape[0], LANES))
    chunks = [s[:, c * LANES:(c + 1) * LANES] - m for c in range(s.shape[1] // LANES)]
    p = jnp.exp2(jnp.concatenate(chunks, axis=1))
    return _normalized(_dot(p.astype(BF16), v))


def _selwin_kernel(list_ref, cnt_ref, q_ref, ks_ref, vs_ref, kw_ref, vw_ref, selb_ref, oc_ref, sm_ref,
                   o_ref, m_a, acc_a, m_b, acc_b):
    b = pl.program_id(0)
    g = pl.program_id(1)
    qb = pl.program_id(2)
    nq = pl.num_programs(2)
    rows = NSA_HPG * SW_TILE
    q4 = q_ref[0].reshape(rows, LANES)
    q_aug = jnp.concatenate([q4, jnp.concatenate([selb_ref[0, 0]] * NSA_HPG, axis=0)], axis=1)
    r = lax.broadcasted_iota(jnp.int32, (rows, 1), 0) % SW_TILE
    c = lax.broadcasted_iota(jnp.int32, (1, K_TILE), 1)
    rel = r - c
    diag = qb // (K_TILE // SW_TILE)

    def sel_tile(kt, m_ref, acc_ref, causal=False, bias=None):
        start = pl.multiple_of(kt * K_TILE, K_TILE)
        s = _dot_nt(q_aug, ks_ref[0, 0, pl.ds(start, K_TILE), :])
        if bias is not None:
            s = s + bias
        if causal:
            s = jnp.where(rel + (qb * SW_TILE - kt * K_TILE) >= 0, s, NEG)
        _online_update(s, vs_ref[0, 0, pl.ds(start, K_TILE), :], m_ref, acc_ref)

    for m_ref, acc_ref in ((m_a, acc_a), (m_b, acc_b)):
        m_ref[...] = jnp.full(m_ref.shape, NEG, F32)
        acc_ref[...] = jnp.zeros(acc_ref.shape, F32)
    step = (b * NSA_KV_GROUPS + g) * nq + qb
    count = cnt_ref[step]
    base = step * MAX_TILES

    def body(p, carry):
        second = 2 * p + 1
        sel_tile(list_ref[base + 2 * p], m_a, acc_a)
        sel_tile(list_ref[base + jnp.minimum(second, MAX_TILES - 1)], m_b, acc_b,
                 bias=jnp.where(second < count, 0.0, NEG))
        return carry

    lax.fori_loop(0, (count + 1) // 2, body, 0)
    m_new = jnp.maximum(m_a[...], m_b[...])
    acc_a[...] = jnp.exp2(m_a[...] - m_new) * acc_a[...] + jnp.exp2(m_b[...] - m_new) * acc_b[...]
    m_a[...] = m_new
    sel_tile(diag, m_a, acc_a, causal=True)
    o_sel = _normalized(acc_a[...])

    span = WINDOW + K_TILE
    wstart = pl.multiple_of(jnp.maximum(diag - WINDOW // K_TILE, 0) * K_TILE, K_TILE)
    dist = (qb * SW_TILE + r) - (wstart + lax.broadcasted_iota(jnp.int32, (1, span), 1))
    s = _dot_nt(q4, kw_ref[0, 0, pl.ds(wstart, span), :])
    s = jnp.where((dist >= 0) & (dist < WINDOW), s, NEG)
    o_win = _attend_once(s, vw_ref[0, 0, pl.ds(wstart, span), :])

    sm = sm_ref[0]
    y = _gate_rows(sm, g, 1) * o_sel + _gate_rows(sm, g, 2) * o_win
    o_ref[0] = (oc_ref[0] + _head_tile(y)).astype(BF16)


def _selwin_attention(tile_list, tile_count, qa, ksl, nkv, selb, ocg, sm):
    B, H, S, _ = qa.shape
    G = NSA_KV_GROUPS
    nq = S // SW_TILE
    rows = NSA_HPG * SW_TILE
    kv_spec = lambda piece: pl.BlockSpec((1, 1, S, LANES), lambda b, g, i, tl, tc: (b, piece + g, 0, 0))
    out_tile = pl.BlockSpec((1, SW_TILE, NSA_HPG * HEAD_DIM), lambda b, g, i, tl, tc: (b, i, g))
    grid_spec = pltpu.PrefetchScalarGridSpec(
        num_scalar_prefetch=2,
        grid=(B, G, nq),
        in_specs=[
            pl.BlockSpec((1, NSA_HPG, SW_TILE, LANES), lambda b, g, i, tl, tc: (b, g, i, 0)),
            pl.BlockSpec((1, 1, S, 2 * LANES), lambda b, g, i, tl, tc: (b, g, 0, 0)),
            kv_spec(0), kv_spec(2), kv_spec(4),
            pl.BlockSpec((1, 1, SW_TILE, LANES), lambda b, g, i, tl, tc: (b, g, i, 0)),
            out_tile,
            pl.BlockSpec((1, SW_TILE, LANES), lambda b, g, i, tl, tc: (b, i, 0)),
        ],
        out_specs=out_tile,
        scratch_shapes=[pltpu.VMEM((rows, LANES), F32)] * 4,
    )
    return pl.pallas_call(
        _selwin_kernel,
        grid_spec=grid_spec,
        out_shape=jax.ShapeDtypeStruct((B, S, NSA_W), BF16),
        compiler_params=_cparams(("parallel", "parallel", "arbitrary")),
        name="selwin_attention",
    )(tile_list, tile_count, qa, ksl, nkv, nkv, nkv, selb, ocg, sm)


def _fox_kernel(q_ref, k_ref, v_ref, o_ref, m_sc, acc_sc, *, tq):
    qi = pl.program_id(2)
    m_sc[...] = jnp.full(m_sc.shape, NEG, F32)
    acc_sc[...] = jnp.zeros(acc_sc.shape, F32)

    def tile(kt, width, causal):
        start = pl.multiple_of(kt * tq, tq)
        for hh in range(FOX_HPS):
            s = _dot_nt(q_ref[0, hh], k_ref[0, hh, pl.ds(start, width), :])
            if causal:
                r = lax.broadcasted_iota(jnp.int32, s.shape, 0)
                c = lax.broadcasted_iota(jnp.int32, s.shape, 1)
                s = jnp.where(r >= c, s, NEG)
            _online_update(s, v_ref[0, hh, pl.ds(start, width), :], m_sc.at[hh], acc_sc.at[hh])

    def body(kp, carry):
        tile(2 * kp, 2 * tq, False)
        return carry

    lax.fori_loop(0, qi // 2, body, 0)

    @pl.when(qi % 2 == 1)
    def _():
        tile(qi - 1, tq, False)

    tile(qi, tq, True)
    lane = lax.broadcasted_iota(jnp.int32, (tq, LANES), 1)
    o = [_normalized(acc_sc[hh]) for hh in range(FOX_HPS)]
    for pr in range(FOX_HPS // 2):
        o_ref[0, :, pr * LANES:(pr + 1) * LANES] = jnp.where(
            lane < HEAD_DIM, o[2 * pr], pltpu.roll(o[2 * pr + 1], HEAD_DIM, 1)).astype(BF16)


def _fox_attention(fq, fk, fv, tq=512):
    B, H, S, _ = fq.shape
    hps = FOX_HPS
    return pl.pallas_call(
        functools.partial(_fox_kernel, tq=tq),
        grid=(B, H // hps, S // tq),
        in_specs=[
            pl.BlockSpec((1, hps, tq, LANES), lambda b, h, i: (b, h, i, 0)),
            pl.BlockSpec((1, hps, S, LANES), lambda b, h, i: (b, h, 0, 0)),
            pl.BlockSpec((1, hps, S, LANES), lambda b, h, i: (b, h, 0, 0)),
        ],
        out_specs=pl.BlockSpec((1, tq, hps * HEAD_DIM), lambda b, h, i: (b, i, h)),
        out_shape=jax.ShapeDtypeStruct((B, S, FOX_W), BF16),
        scratch_shapes=[
            pltpu.VMEM((hps, tq, LANES), F32),
            pltpu.VMEM((hps, tq, LANES), F32),
        ],
        compiler_params=_cparams(("parallel", "parallel", "arbitrary")),
        name="fox_attention",
    )(fq, fk, fv)


def _merge_kernel(ya_ref, yb_ref, mg_ref, x_ref, mod_ref, gpost_ref, gpre_ref,
                  wa_ref, wb_ref, wo_ref, wrh_ref, wrl_ref, br_ref, stri_ref,
                  x1_ref, h2_ref, rt_ref, cnt_ref):
    D = D_MODEL

    @pl.when((pl.program_id(0) == 0) & (pl.program_id(1) == 0))
    def _():
        cnt_ref[...] = jnp.zeros(cnt_ref.shape, F32)

    a = _dot(ya_ref[0], wa_ref[...])
    bq = _dot(yb_ref[0], wb_ref[...])
    mg = mg_ref[0]
    u = mg[:, :D].astype(F32) * a + mg[:, D:].astype(F32) * bq
    mixed = _dot(u.astype(BF16), wo_ref[...])
    x1 = x_ref[0] + mod_ref[0, 2:3, :] * _rms(mixed, gpost_ref[...])
    x1_ref[0] = x1
    h2 = _rms(x1, gpre_ref[...]) * (1.0 + mod_ref[0, 4:5, :]) + mod_ref[0, 3:4, :]
    hi = h2.astype(BF16)
    lo = (h2 - hi.astype(F32)).astype(BF16)
    h2_ref[0] = h2
    lg = _dot(hi, wrh_ref[...]) + _dot(lo, wrh_ref[...]) + _dot(hi, wrl_ref[...]) + br_ref[...]

    lane = lax.broadcasted_iota(jnp.int32, lg.shape, 1)
    lanef = lane.astype(F32)
    no_lane = float(LANES)
    is_g = lane < N_EXPERT_GROUPS
    gl = jnp.where(is_g, lg, NEG)
    gmax = jnp.max(gl, axis=-1, keepdims=True)
    pg_top = 1.0 / jnp.sum(jnp.where(is_g, jnp.exp(gl - gmax), 0.0), axis=-1, keepdims=True)
    g_idx = jnp.min(jnp.where(is_g & (gl == gmax), lanef, no_lane), axis=-1, keepdims=True)
    in_grp = ((lane >= N_EXPERT_GROUPS) & (lane < N_EXPERT_GROUPS + N_EXPERTS)
              & (((lane - N_EXPERT_GROUPS) // EXPERTS_PER_GROUP).astype(F32) == g_idx))
    le = jnp.where(in_grp, lg, NEG)
    m1 = jnp.max(le, axis=-1, keepdims=True)
    i1 = jnp.min(jnp.where(in_grp & (le == m1), lanef, no_lane), axis=-1, keepdims=True)
    rest = in_grp & (lanef != i1)
    le2 = jnp.where(rest, lg, NEG)
    m2 = jnp.max(le2, axis=-1, keepdims=True)
    i2 = jnp.min(jnp.where(rest & (le2 == m2), lanef, no_lane), axis=-1, keepdims=True)
    e21 = jnp.exp(m2 - m1)
    w1 = pg_top / (1.0 + e21)
    w2 = w1 * e21
    pick1 = lanef == i1
    pick2 = lanef == i2
    onehot = jnp.where(pick1 | pick2, 1.0, 0.0)
    before = cnt_ref[...] + _dot(stri_ref[...], onehot.astype(BF16))
    rank1 = jnp.sum(jnp.where(pick1, before, 0.0), axis=-1, keepdims=True)
    rank2 = jnp.sum(jnp.where(pick2, before, 0.0), axis=-1, keepdims=True)
    cnt_ref[...] = cnt_ref[...] + jnp.sum(onehot, axis=0, keepdims=True)
    fields = [i1 - N_EXPERT_GROUPS, i2 - N_EXPERT_GROUPS, rank1, rank2, w1, w2]
    rt = jnp.zeros(lg.shape, F32)
    for k, f in enumerate(fields):
        rt = jnp.where(lane == k, f, rt)
    rt_ref[0] = rt


def _merge(ya, yb, mg, x, mod, gpost, gpre, wa, wb, wo, wrh, wrl, br, stri):
    B, S, D = x.shape
    tm = MERGE_TILE
    c2 = lambda b, i: (0, 0)
    row = lambda w: pl.BlockSpec((1, tm, w), lambda b, i: (b, i, 0))
    return pl.pallas_call(
        _merge_kernel,
        grid=(B, S // tm),
        in_specs=[
            row(NSA_W), row(FOX_W), row(2 * D), row(D),
            pl.BlockSpec((1, 6, D), lambda b, i: (b, 0, 0)),
            pl.BlockSpec((1, D), c2), pl.BlockSpec((1, D), c2),
            pl.BlockSpec((NSA_W, D), c2), pl.BlockSpec((FOX_W, D), c2), pl.BlockSpec((D, D), c2),
            pl.BlockSpec((D, LANES), c2), pl.BlockSpec((D, LANES), c2), pl.BlockSpec((1, LANES), c2),
            pl.BlockSpec((tm, tm), c2),
        ],
        out_specs=[row(D), row(D), row(LANES), pl.BlockSpec((1, LANES), c2)],
        out_shape=[
            jax.ShapeDtypeStruct((B, S, D), F32),
            jax.ShapeDtypeStruct((B, S, D), F32),
            jax.ShapeDtypeStruct((B, S, LANES), F32),
            jax.ShapeDtypeStruct((1, LANES), F32),
        ],
        compiler_params=_cparams(("arbitrary", "arbitrary")),
        name="merge",
    )(ya, yb, mg, x, mod, gpost, gpre, wa, wb, wo, wrh, wrl, br, stri)


def _expert_kernel(be_ref, na_ref, tok_ref, h_hbm, wg_ref, wu_ref, wd_ref, o_ref,
                   x0, x1, x2, wg_b, wu_b, wd_b, sem):
    i = pl.program_id(0)
    n_active = na_ref[0]
    last_block = pl.num_programs(0) - 1
    bufs = (x0, x1, x2)
    ring = len(bufs)

    def row_copy(blk, r, sl):
        tok = tok_ref[blk * MOE_TILE + r]
        return pltpu.make_async_copy(h_hbm.at[pl.ds(tok, 1)], bufs[sl].at[pl.ds(r, 1)], sem.at[sl])

    def wait_rows(sl):
        pltpu.make_async_copy(h_hbm.at[pl.ds(0, MOE_TILE)], bufs[sl], sem.at[sl]).wait()

    @pl.when(i == 0)
    def _():
        for ahead in range(GATHER_AHEAD):
            def body(r, carry, ahead=ahead):
                row_copy(jnp.minimum(ahead, last_block), r, ahead).start()
                return carry
            lax.fori_loop(0, MOE_TILE, body, 0, unroll=8)

    @pl.when((i == 0) | (be_ref[i] != be_ref[jnp.maximum(i - 1, 0)]))
    def _():
        wg_b[...] = wg_ref[0].astype(BF16)
        wu_b[...] = wu_ref[0].astype(BF16)
        wd_b[...] = wd_ref[0].astype(BF16)

    def step(sl):
        wait_rows(sl)
        nxt = jnp.minimum(i + GATHER_AHEAD, last_block)
        nxt_sl = (sl + GATHER_AHEAD) % ring
        for r in range(MOE_TILE):
            row_copy(nxt, r, nxt_sl).start(priority=r % 2)
        x = bufs[sl][...].astype(BF16)
        gate = _dot(x, wg_b[...])
        up = _dot(x, wu_b[...])
        mid = (gate * jax.nn.sigmoid(gate) * up).astype(BF16)
        o_ref[...] = _dot(mid, wd_b[...])

        @pl.when(i == n_active - 1)
        def _():
            for ahead in range(1, GATHER_AHEAD + 1):
                wait_rows((sl + ahead) % ring)

    for sl in range(ring):
        pl.when((i % ring == sl) & (i < n_active))(functools.partial(step, sl))

    @pl.when(i >= n_active)
    def _():
        o_ref[...] = jnp.zeros(o_ref.shape, o_ref.dtype)


def _experts(block_expert, n_active, buf_tok, h2, wg, wu, wd):
    cap = buf_tok.shape[0]
    D = D_MODEL
    nblk = cap // MOE_TILE
    grid_spec = pltpu.PrefetchScalarGridSpec(
        num_scalar_prefetch=3,
        grid=(nblk,),
        in_specs=[
            pl.BlockSpec(memory_space=pl.ANY),
            pl.BlockSpec((1, D, D_EXPERT), lambda i, be, na, tok: (be[i], 0, 0)),
            pl.BlockSpec((1, D, D_EXPERT), lambda i, be, na, tok: (be[i], 0, 0)),
            pl.BlockSpec((1, D_EXPERT, D), lambda i, be, na, tok: (be[i], 0, 0)),
        ],
        out_specs=pl.BlockSpec((MOE_TILE, D), lambda i, be, na, tok: (i, 0)),
        scratch_shapes=[
            pltpu.VMEM((MOE_TILE, D), F32),
            pltpu.VMEM((MOE_TILE, D), F32),
            pltpu.VMEM((MOE_TILE, D), F32),
            pltpu.VMEM((D, D_EXPERT), BF16),
            pltpu.VMEM((D, D_EXPERT), BF16),
            pltpu.VMEM((D_EXPERT, D), BF16),
            pltpu.SemaphoreType.DMA((GATHER_AHEAD + 1,)),
        ],
    )
    return pl.pallas_call(
        _expert_kernel,
        grid_spec=grid_spec,
        out_shape=jax.ShapeDtypeStruct((cap, D), F32),
        compiler_params=_cparams(("arbitrary",)),
        name="experts",
    )(block_expert, n_active, buf_tok, h2, wg, wu, wd)


def _final_kernel(dest_ref, x1_ref, rt_ref, mod_ref, g_ref, y_hbm, o_ref, a0, b0, a1, b1, a2, b2, sem):
    j = pl.program_id(0)
    last_tile = pl.num_programs(0) - 1
    tm = o_ref.shape[0]
    bufs = ((a0, b0), (a1, b1), (a2, b2))
    ring = len(bufs)

    def row_copy(tile, r, k, sl):
        row = dest_ref[(tile * tm + r) * EXPERT_TOP_K + k]
        return pltpu.make_async_copy(y_hbm.at[pl.ds(row, 1)], bufs[sl][k].at[pl.ds(r, 1)], sem.at[sl])

    def wait_rows(sl):
        for k in range(EXPERT_TOP_K):
            pltpu.make_async_copy(y_hbm.at[pl.ds(0, tm)], bufs[sl][k], sem.at[sl]).wait()

    @pl.when(j == 0)
    def _():
        for ahead in range(GATHER_AHEAD):
            def body(r, carry, ahead=ahead):
                for k in range(EXPERT_TOP_K):
                    row_copy(jnp.minimum(ahead, last_tile), r, k, ahead).start()
                return carry
            lax.fori_loop(0, tm, body, 0, unroll=4)

    def step(sl):
        wait_rows(sl)
        nxt = jnp.minimum(j + GATHER_AHEAD, last_tile)
        nxt_sl = (sl + GATHER_AHEAD) % ring
        for r in range(tm):
            for k in range(EXPERT_TOP_K):
                row_copy(nxt, r, k, nxt_sl).start(priority=k)
        rt = rt_ref[...]
        lane = lax.broadcasted_iota(jnp.int32, rt.shape, 1)
        w0 = jnp.sum(jnp.where(lane == 4, rt, 0.0), axis=-1, keepdims=True)
        w1 = jnp.sum(jnp.where(lane == 5, rt, 0.0), axis=-1, keepdims=True)
        y = w0 * bufs[sl][0][...] + w1 * bufs[sl][1][...]
        o_ref[...] = x1_ref[...] + mod_ref[0, 5:6, :] * _rms(y, g_ref[...])

        @pl.when(j == last_tile)
        def _():
            for ahead in range(1, GATHER_AHEAD + 1):
                wait_rows((sl + ahead) % ring)

    for sl in range(ring):
        pl.when(j % ring == sl)(functools.partial(step, sl))


def _final(dest, x1, rt, mod, g, yb, tiles_per_batch):
    T, D = x1.shape
    tm = MERGE_TILE
    grid_spec = pltpu.PrefetchScalarGridSpec(
        num_scalar_prefetch=1,
        grid=(T // tm,),
        in_specs=[
            pl.BlockSpec((tm, D), lambda j, d: (j, 0)),
            pl.BlockSpec((tm, LANES), lambda j, d: (j, 0)),
            pl.BlockSpec((1, 6, D), lambda j, d: (j // tiles_per_batch, 0, 0)),
            pl.BlockSpec((1, D), lambda j, d: (0, 0)),
            pl.BlockSpec(memory_space=pl.ANY),
        ],
        out_specs=pl.BlockSpec((tm, D), lambda j, d: (j, 0)),
        scratch_shapes=([pltpu.VMEM((tm, D), F32)] * (EXPERT_TOP_K * (GATHER_AHEAD + 1))
                        + [pltpu.SemaphoreType.DMA((GATHER_AHEAD + 1,))]),
    )
    return pl.pallas_call(
        _final_kernel,
        grid_spec=grid_spec,
        out_shape=jax.ShapeDtypeStruct((T, D), F32),
        compiler_params=_cparams(("arbitrary",)),
        name="final",
    )(dest, x1, rt, mod, g, yb)


def _overlap_matrix():
    n = np.arange(N_CMP_PAD)[:, None]
    j = np.arange(LANES)[None, :]
    start = n * CMP_STRIDE
    ov = (start < j * SEL_LEN + SEL_LEN) & (start + CMP_LEN - 1 >= j * SEL_LEN) & (n < N_CMP_PAD - 1)
    return jnp.asarray(ov.T.astype(np.float32), dtype=BF16)


def _pad_cols(w, width=LANES):
    return jnp.pad(w, ((0, 0), (0, width - w.shape[1])))


def _dispatch_plan(rt, cnt, T):
    expert = rt[:, 0:2].astype(jnp.int32)
    rank = rt[:, 2:4].astype(jnp.int32)
    weight = rt[:, 4:6]
    counts = cnt[0, N_EXPERT_GROUPS:N_EXPERT_GROUPS + N_EXPERTS].astype(jnp.int32)
    padded = (counts + MOE_TILE - 1) // MOE_TILE * MOE_TILE
    pad_end = jnp.cumsum(padded)
    pad_start = pad_end - padded
    onehot = expert[:, :, None] == jnp.arange(N_EXPERTS)[None, None, :]
    dest = jnp.sum(jnp.where(onehot, pad_start[None, None, :], 0), axis=-1) + rank
    A = T * EXPERT_TOP_K
    cap = -(-(A + N_EXPERTS * (MOE_TILE - 1)) // MOE_TILE) * MOE_TILE
    nblk = cap // MOE_TILE
    n_active = (pad_end[-1] // MOE_TILE).astype(jnp.int32)
    blk = jnp.arange(nblk) * MOE_TILE
    block_expert = jnp.minimum(jnp.sum(pad_end[None, :] <= blk[:, None], axis=1), N_EXPERTS - 1)
    last = jnp.max(jnp.where(jnp.arange(nblk) < n_active, block_expert, 0))
    block_expert = jnp.where(jnp.arange(nblk) < n_active, block_expert, last).astype(jnp.int32)
    tok = jnp.arange(A, dtype=jnp.int32) // EXPERT_TOP_K
    buf_tok = jnp.zeros((cap,), jnp.int32).at[dest.reshape(A)].set(tok)
    return weight, dest, buf_tok, block_expert, n_active.reshape(1)


def kernel(x, c, w_ada, b_ada, g_pre_mix, g_post_mix, g_pre_ffn, g_post_ffn, w_in, b_forget,
           cmp_pe_k, cmp_w1_k, cmp_w2_k, cmp_pe_v, cmp_w1_v, cmp_w2_v,
           w_o_nsa, w_o_fox, w_out, w_router_group, b_router_group, w_router_expert, b_router_expert,
           w_exp_gate, w_exp_up, w_exp_down):
    B, S, D = x.shape
    T = B * S
    depth = w_ada.shape[0]
    ov = _overlap_matrix()
    tri = jnp.asarray(np.tril(np.ones((IN_TILE, IN_TILE), np.float32)), dtype=BF16)
    stri = jnp.asarray(np.tril(np.ones((MERGE_TILE, MERGE_TILE), np.float32), -1), dtype=BF16)
    row_feat = _row_features(S)
    placement = _placement()
    cmp_ext = _cmp_key_ext()
    for l in range(depth):
        mod = (jax.nn.silu(c) @ w_ada[l] + b_ada[l]).reshape(B, 6, D)
        w_qa, w_kva, w_gl, w_fox, w_f, w_mg = jnp.split(w_in[l], IN_SPLITS, axis=-1)
        w_big = jnp.concatenate([w_qa, w_kva, w_fox, w_mg], axis=1).astype(BF16)
        w_small = _pad_cols(jnp.concatenate([w_gl, w_f], axis=1)).astype(BF16)
        bf_pad = jnp.pad(b_forget[l], (F_LANE, LANES - F_LANE - FOX_HEADS)).reshape(1, LANES)
        qa, ckv, ksl, nkv, fq, fk, fv, mg, sm = _inproj(
            x, mod, g_pre_mix[l].reshape(1, D), w_big, w_small, bf_pad, tri, row_feat, placement)

        half = CMP_LEN // 2
        pe = jnp.stack([cmp_pe_k[l], cmp_pe_v[l]]).reshape(2, 2, 1, half * HEAD_DIM)
        w1 = jnp.stack([cmp_w1_k[l], cmp_w1_v[l]]).reshape(2, 2, half * HEAD_DIM, HEAD_DIM).astype(BF16)
        w2 = jnp.pad(jnp.stack([cmp_w2_k[l], cmp_w2_v[l]]), ((0, 0), (0, 0), (0, LANES - HEAD_DIM))).astype(BF16)
        kvc = _compress(ckv.reshape(B, 4, S // CMP_STRIDE, CMP_STRIDE * HEAD_DIM), pe, w1, w2, cmp_ext)
        ocg, selb, flags = _cmp_attention(qa, kvc, sm, ov)
        nq = S // SW_TILE
        per_tile = K_TILE // SEL_LEN
        tile_any = jnp.max(flags.reshape(B, NSA_KV_GROUPS, nq, SW_TILE // Q_TILE, MAX_TILES, per_tile), axis=(3, 5))
        tile_id = jnp.arange(MAX_TILES)
        diag = (jnp.arange(nq) // (K_TILE // SW_TILE))[:, None]
        active = (tile_any > 0) & (tile_id < diag)
        slot = jnp.cumsum(active, axis=-1) - 1
        hit = active[..., :, None] & (slot[..., :, None] == tile_id)
        tile_list = jnp.sum(jnp.where(hit, tile_id[:, None], 0), axis=-2).astype(jnp.int32).reshape(-1)
        tile_count = jnp.sum(active, axis=-1).astype(jnp.int32).reshape(-1)
        y_a = _selwin_attention(tile_list, tile_count, qa, ksl, nkv, selb, ocg, sm)

        y_b = _fox_attention(fq, fk, fv)

        w_r = _pad_cols(jnp.concatenate([w_router_group[l], w_router_expert[l]], axis=1))
        w_rh = w_r.astype(BF16)
        w_rl = (w_r - w_rh.astype(F32)).astype(BF16)
        b_r = _pad_cols(jnp.concatenate([b_router_group[l], b_router_expert[l]]).reshape(1, -1))
        x1, h2, rt, cnt = _merge(y_a, y_b, mg, x, mod, g_post_mix[l].reshape(1, D), g_pre_ffn[l].reshape(1, D),
                                 w_o_nsa[l].astype(BF16), w_o_fox[l].astype(BF16), w_out[l].astype(BF16),
                                 w_rh, w_rl, b_r, stri)

        weight, dest, buf_tok, block_expert, n_active = _dispatch_plan(rt.reshape(T, LANES), cnt, T)
        yb = _experts(block_expert, n_active, buf_tok, h2.reshape(T, D), w_exp_gate[l], w_exp_up[l], w_exp_down[l])
        x = _final(dest.reshape(T * EXPERT_TOP_K), x1.reshape(T, D), rt.reshape(T, LANES), mod,
                   g_post_ffn[l].reshape(1, D), yb, S // MERGE_TILE).reshape(B, S, D)
    return x
```

```python
import functools

import ml_dtypes
import numpy as np
import jax
import jax.numpy as jnp
from jax import lax
from jax.experimental import pallas as pl
from jax.experimental.pallas import tpu as pltpu

D_MODEL = 1024
HEAD_DIM = 64
NSA_HEADS = 8
NSA_KV_GROUPS = 2
NSA_HPG = NSA_HEADS // NSA_KV_GROUPS
FOX_HEADS = 8
CMP_LEN = 32
CMP_STRIDE = 16
SEL_LEN = 64
N_SEL = 16
WINDOW = 512
N_EXPERT_GROUPS = 4
EXPERTS_PER_GROUP = 8
N_EXPERTS = N_EXPERT_GROUPS * EXPERTS_PER_GROUP
EXPERT_TOP_K = 2
D_EXPERT = D_MODEL // 2
NORM_EPS = 1e-6
NEG = -1e30
FORCE = 1e9
LOG2E = 1.4426950408889634

NSA_W = NSA_HEADS * HEAD_DIM
NSA_KV_W = NSA_KV_GROUPS * HEAD_DIM
FOX_W = FOX_HEADS * HEAD_DIM
IN_SIZES = (NSA_W, 6 * NSA_KV_W, 3 * NSA_HEADS, 3 * FOX_W, FOX_HEADS, 2 * D_MODEL)
IN_SPLITS = tuple(int(v) for v in np.cumsum(IN_SIZES)[:-1])

LANES = 128
Q_TILE = 128
K_TILE = 256
SW_TILE = 256
N_CMP_PAD = 512
MOE_TILE = 256
IN_TILE = 512
MERGE_TILE = 256
FOX_HPS = 4
CMP_SUB = 2
MAX_TILES = 32
GATHER_AHEAD = 3
VMEM_LIMIT = 56 * 1024 * 1024

F_LANE = 3 * NSA_HEADS
U_LANE = 64
ONE_LANE = 88
A_LANE = 89
B_LANE = 90
EXT = HEAD_DIM
G_FQ, G_FK, G_NQ, G_NK, N_GROUPS = 0, 8, 16, 24, 25

F32 = jnp.float32
BF16 = jnp.bfloat16


def _dot(a, b):
    return jnp.dot(a, b, preferred_element_type=F32)


def _dot_nt(a, b):
    return lax.dot_general(a, b, (((1,), (1,)), ((), ())), preferred_element_type=F32)


def _rms(x, g):
    return x * lax.rsqrt(jnp.mean(x * x, axis=-1, keepdims=True) + NORM_EPS) * g


def _cparams(sem):
    return pltpu.CompilerParams(dimension_semantics=sem, vmem_limit_bytes=VMEM_LIMIT)


def _split3(x):
    hi = x.astype(BF16).astype(F32)
    r = x - hi
    mid = r.astype(BF16).astype(F32)
    lo = (r - mid).astype(BF16).astype(F32)
    return hi, mid, lo


def _np_split3(x):
    x = np.asarray(x, np.float32)
    hi = x.astype(ml_dtypes.bfloat16).astype(np.float32)
    r = x - hi
    mid = r.astype(ml_dtypes.bfloat16).astype(np.float32)
    lo = (r - mid).astype(ml_dtypes.bfloat16).astype(np.float32)
    return hi, mid, lo


def _alibi_c():
    slopes = np.exp2(-8.0 * np.arange(1, NSA_HEADS + 1, dtype=np.float32) / NSA_HEADS).astype(np.float32)
    return slopes * np.float32(LOG2E)


def _row_features(S):
    t = np.arange(S, dtype=np.float32)
    c = _alibi_c()
    rs = np.zeros((S, LANES), np.float32)
    for h in range(NSA_HEADS):
        for j, term in enumerate(_np_split3(c[h] * t)):
            rs[:, U_LANE + 8 * j + h] = -term
    rs[:, ONE_LANE] = 1.0
    rs[:, A_LANE] = np.floor(t / LANES)
    rs[:, B_LANE] = t % LANES
    return jnp.asarray(rs, dtype=BF16)


def _placement():
    c = _alibi_c()
    p = np.zeros((LANES, N_GROUPS * LANES), np.float32)
    for h in range(FOX_HEADS):
        q0 = (G_FQ + h) * LANES + EXT
        k0 = (G_FK + h) * LANES + EXT
        for j in range(3):
            p[ONE_LANE, q0 + j] = -1.0
            p[F_LANE + 8 * j + h, q0 + 3 + j] = 1.0
            p[F_LANE + 8 * j + h, k0 + j] = 1.0
            p[ONE_LANE, k0 + 3 + j] = 1.0
    for h in range(NSA_HEADS):
        q0 = (G_NQ + h) * LANES + EXT
        c128 = _np_split3(c[h] * np.float32(LANES))
        c1 = _np_split3(c[h])
        for j in range(3):
            p[U_LANE + 8 * j + h, q0 + j] = 1.0
            p[ONE_LANE, q0 + 3 + j] = c128[j]
            p[ONE_LANE, q0 + 6 + j] = c1[j]
    k0 = G_NK * LANES + EXT
    for j in range(3):
        p[ONE_LANE, k0 + j] = 1.0
        p[A_LANE, k0 + 3 + j] = 1.0
        p[B_LANE, k0 + 6 + j] = 1.0
    return jnp.asarray(p, dtype=BF16)


def _cmp_key_ext():
    pos = np.arange(N_CMP_PAD, dtype=np.float32) * CMP_STRIDE + (CMP_LEN - 1)
    e = np.zeros((2, N_CMP_PAD, LANES), np.float32)
    for j in range(3):
        e[0, :, EXT + j] = 1.0
        e[0, :, EXT + 3 + j] = np.floor(pos / LANES)
        e[0, :, EXT + 6 + j] = pos % LANES
    return jnp.asarray(e, dtype=BF16)


def _inproj_kernel(x_ref, mod_ref, g_ref, wb_ref, ws_ref, bf_ref, tri_ref, rs_ref, p_ref,
                   qa_ref, ckv_ref, ksl_ref, nkv_ref, fq_ref, fk_ref, fv_ref, mg_ref, sm_ref, carry_sc):
    i = pl.program_id(1)
    tm = x_ref.shape[1]
    x = x_ref[0]
    h = _rms(x, g_ref[...]) * (1.0 + mod_ref[0, 1:2, :]) + mod_ref[0, 0:1, :]
    hb = h.astype(BF16)
    lane = lax.broadcasted_iota(jnp.int32, (tm, LANES), 1)
    lower = lane < HEAD_DIM
    ones_col = (lane == EXT).astype(F32)

    z = _dot(hb, ws_ref[...]) + bf_ref[...]
    logsig = jnp.minimum(z, 0.0) - jnp.log1p(jnp.exp(-jnp.abs(z)))
    sm_ref[0] = jnp.where(lane < F_LANE, jax.nn.sigmoid(z), logsig)

    @pl.when(i == 0)
    def _():
        carry_sc[...] = jnp.zeros(carry_sc.shape, F32)

    is_f = (lane >= F_LANE) & (lane < F_LANE + FOX_HEADS)
    l_hi, l_mid, l_lo = _split3(jnp.where(is_f, logsig, 0.0))
    tri = tri_ref[...]
    cum = carry_sc[...] + _dot(tri, l_hi.astype(BF16)) + _dot(tri, l_mid.astype(BF16)) + _dot(tri, l_lo.astype(BF16))
    carry_sc[...] = cum[tm - 1:tm, :]
    f_hi, f_mid, f_lo = _split3(cum * LOG2E)
    feat = (f_hi + pltpu.roll(f_mid, 8, 1) + pltpu.roll(f_lo, 16, 1) + rs_ref[...].astype(F32)).astype(BF16)

    def ext(group):
        return _dot(feat, p_ref[:, group * LANES:(group + 1) * LANES])

    def piece(acc, idx, extra):
        pair = acc[:, (idx // 2) * LANES:(idx // 2 + 1) * LANES]
        if idx % 2:
            pair = pltpu.roll(pair, HEAD_DIM, 1)
        return jnp.where(lower, pair, extra).astype(BF16)

    qscale = (HEAD_DIM ** -0.5) * LOG2E
    acc = _dot(hb, wb_ref[:, 0:NSA_W]) * qscale
    for hd in range(NSA_HEADS):
        qa_ref[0, hd] = piece(acc, hd, ext(G_NQ + hd))
    off = NSA_W
    acc = _dot(hb, wb_ref[:, off:off + 6 * NSA_KV_W])
    for pc in range(4):
        ckv_ref[0, pc] = acc[:, pc * HEAD_DIM:(pc + 1) * HEAD_DIM].astype(BF16)
    ext_k = ext(G_NK)
    t = i * tm + lax.broadcasted_iota(jnp.int32, (tm, LANES), 0)
    block_onehot = (lane == t // SEL_LEN).astype(BF16)
    for g in range(NSA_KV_GROUPS):
        ksl_ref[0, g, :, 0:LANES] = piece(acc, 4 + g, ext_k)
        ksl_ref[0, g, :, LANES:2 * LANES] = block_onehot
        nkv_ref[0, g] = piece(acc, 6 + g, ones_col)
        nkv_ref[0, 2 + g] = piece(acc, 8 + g, ext_k)
        nkv_ref[0, 4 + g] = piece(acc, 10 + g, ones_col)
    off += 6 * NSA_KV_W
    acc = _dot(hb, wb_ref[:, off:off + FOX_W]) * qscale
    for hd in range(FOX_HEADS):
        fq_ref[0, hd] = piece(acc, hd, ext(G_FQ + hd))
    off += FOX_W
    acc = _dot(hb, wb_ref[:, off:off + FOX_W])
    for hd in range(FOX_HEADS):
        fk_ref[0, hd] = piece(acc, hd, ext(G_FK + hd))
    off += FOX_W
    acc = _dot(hb, wb_ref[:, off:off + FOX_W])
    for hd in range(FOX_HEADS):
        fv_ref[0, hd] = piece(acc, hd, ones_col)
    off += FOX_W
    for c in range(4):
        acc = _dot(hb, wb_ref[:, off + c * 512: off + (c + 1) * 512])
        mg_ref[0, :, c * 512:(c + 1) * 512] = jax.nn.sigmoid(acc).astype(BF16)


def _inproj(x, mod, g, wb, ws, bfp, tri, rs, pm):
    B, S, D = x.shape
    tm = IN_TILE
    nb = wb.shape[1]
    const2 = lambda b, i: (0, 0)
    heads = lambda n: pl.BlockSpec((1, n, tm, LANES), lambda b, i: (b, 0, i, 0))
    hshape = lambda n: jax.ShapeDtypeStruct((B, n, S, LANES), BF16)
    return pl.pallas_call(
        _inproj_kernel,
        grid=(B, S // tm),
        in_specs=[
            pl.BlockSpec((1, tm, D), lambda b, i: (b, i, 0)),
            pl.BlockSpec((1, 6, D), lambda b, i: (b, 0, 0)),
            pl.BlockSpec((1, D), const2),
            pl.BlockSpec((D, nb), const2),
            pl.BlockSpec((D, LANES), const2),
            pl.BlockSpec((1, LANES), const2),
            pl.BlockSpec((tm, tm), const2),
            pl.BlockSpec((tm, LANES), lambda b, i: (i, 0)),
            pl.BlockSpec((LANES, N_GROUPS * LANES), const2),
        ],
        out_specs=[
            heads(NSA_HEADS),
            pl.BlockSpec((1, 4, tm, HEAD_DIM), lambda b, i: (b, 0, i, 0)),
            pl.BlockSpec((1, NSA_KV_GROUPS, tm, 2 * LANES), lambda b, i: (b, 0, i, 0)),
            heads(6), heads(FOX_HEADS), heads(FOX_HEADS), heads(FOX_HEADS),
            pl.BlockSpec((1, tm, 2 * D), lambda b, i: (b, i, 0)),
            pl.BlockSpec((1, tm, LANES), lambda b, i: (b, i, 0)),
        ],
        out_shape=[
            hshape(NSA_HEADS),
            jax.ShapeDtypeStruct((B, 4, S, HEAD_DIM), BF16),
            jax.ShapeDtypeStruct((B, NSA_KV_GROUPS, S, 2 * LANES), BF16),
            hshape(6), hshape(FOX_HEADS), hshape(FOX_HEADS), hshape(FOX_HEADS),
            jax.ShapeDtypeStruct((B, S, 2 * D), BF16),
            jax.ShapeDtypeStruct((B, S, LANES), F32),
        ],
        scratch_shapes=[pltpu.VMEM((1, LANES), F32)],
        compiler_params=_cparams(("parallel", "arbitrary")),
        name="inproj",
    )(x, mod, g, wb, ws, bfp, tri, rs, pm)


def _compress_kernel(x_ref, pe_ref, w1_ref, w2_ref, e_ref, o_ref):
    x = x_ref[0, 0].astype(F32)
    x_lo = (x + pe_ref[0, 0]).astype(BF16)
    x_hi = (x + pe_ref[0, 1]).astype(BF16)
    y_lo = _dot(x_lo, w1_ref[0, 0])
    y_hi = _dot(x_hi, w1_ref[0, 1])
    n = y_hi.shape[0]
    hid = y_lo + pltpu.roll(y_hi, n - 1, 0)
    hid = jax.nn.gelu(hid)
    o_ref[0, 0] = (_dot(hid.astype(BF16), w2_ref[0]) + e_ref[0].astype(F32)).astype(BF16)


def _compress(kv_rows, pe, w1, w2, e):
    B = kv_rows.shape[0]
    R, C = kv_rows.shape[2], kv_rows.shape[3]
    return pl.pallas_call(
        _compress_kernel,
        grid=(B, 4),
        in_specs=[
            pl.BlockSpec((1, 1, R, C), lambda b, p: (b, p, 0, 0)),
            pl.BlockSpec((1, 2, 1, C), lambda b, p: (p // 2, 0, 0, 0)),
            pl.BlockSpec((1, 2, C, HEAD_DIM), lambda b, p: (p // 2, 0, 0, 0)),
            pl.BlockSpec((1, HEAD_DIM, LANES), lambda b, p: (p // 2, 0, 0)),
            pl.BlockSpec((1, R, LANES), lambda b, p: (p // 2, 0, 0)),
        ],
        out_specs=pl.BlockSpec((1, 1, R, LANES), lambda b, p: (b, p, 0, 0)),
        out_shape=jax.ShapeDtypeStruct((B, 4, R, LANES), BF16),
        compiler_params=_cparams(("parallel", "parallel")),
        name="compress",
    )(kv_rows, pe, w1, w2, e)


def _gate_rows(sm, g, branch):
    col = lax.broadcasted_iota(jnp.int32, sm.shape, 1)
    parts = []
    for hl in range(NSA_HPG):
        want = 3 * (NSA_HPG * g + hl) + branch
        parts.append(jnp.sum(jnp.where(col == want, sm, 0.0), axis=-1, keepdims=True))
    return jnp.concatenate(parts, axis=0)


def _head_tile(y):
    n = y.shape[0] // NSA_HPG
    lane = lax.broadcasted_iota(jnp.int32, (n, LANES), 1)
    hs = [y[i * n:(i + 1) * n] for i in range(NSA_HPG)]
    pairs = [jnp.where(lane < HEAD_DIM, hs[2 * i], pltpu.roll(hs[2 * i + 1], HEAD_DIM, 1)) for i in range(2)]
    return jnp.concatenate(pairs, axis=1)


def _cmp_kernel(q_ref, kc_ref, vc_ref, sm_ref, ovt_ref, oc_ref, selb_ref, flag_ref, imp_sc):
    g = pl.program_id(1)
    step_q0 = pl.program_id(2) * CMP_SUB * Q_TILE
    last_visible = (step_q0 + CMP_SUB * Q_TILE - CMP_LEN) // CMP_STRIDE
    chunks = last_visible // LANES + 1

    def attend(width):
        for sub in range(CMP_SUB):
            rows = pl.ds(sub * Q_TILE, Q_TILE)
            q = q_ref[0, :, rows, :].reshape(NSA_HPG * Q_TILE, LANES)
            oc, imp = _cmp_attend(q, kc_ref[0, 0, 0:width, :], vc_ref[0, 0, 0:width, :], sm_ref[0, rows, :],
                                  ovt_ref[:, 0:width], g, step_q0 + sub * Q_TILE)
            oc_ref[0, rows, :] = oc
            imp_sc[sub] = imp

    for v in range(1, N_CMP_PAD // LANES + 1):
        pl.when(chunks == v)(functools.partial(attend, v * LANES))

    for sub in range(CMP_SUB):
        selb, flag = _select_blocks(imp_sc[sub], step_q0 + sub * Q_TILE)
        selb_ref[0, 0, pl.ds(sub * Q_TILE, Q_TILE), :] = selb
        flag_ref[0, 0, sub] = flag


def _cmp_attend(q, kc, vc, sm, ovt, g, q0):
    width = kc.shape[0]
    s = _dot_nt(q, kc)
    r = lax.broadcasted_iota(jnp.int32, (NSA_HPG * Q_TILE, 1), 0) % Q_TILE
    n = lax.broadcasted_iota(jnp.int32, (1, width), 1)
    dc = (q0 + r) - (n * CMP_STRIDE + (CMP_LEN - 1))
    mask = (dc >= 0) & (n < N_CMP_PAD - 1)
    l = jnp.where(mask, s, NEG)
    m = jnp.max(l, axis=-1, keepdims=True)
    e = jnp.where(mask, jnp.exp2(l - m), 0.0)
    pc = e / jnp.maximum(jnp.sum(e, axis=-1, keepdims=True), 1e-30)
    oc = _dot(pc.astype(BF16), vc)
    oc = _head_tile(oc * _gate_rows(sm, g, 0))
    ps = pc[0:Q_TILE]
    for i in range(1, NSA_HPG):
        ps = ps + pc[i * Q_TILE:(i + 1) * Q_TILE]
    ps_hi = ps.astype(BF16)
    ps_lo = (ps - ps_hi.astype(F32)).astype(BF16)
    return oc, _dot_nt(ovt, ps_hi) + _dot_nt(ovt, ps_lo)


def _select_blocks(imp, q0):
    j = lax.broadcasted_iota(jnp.int32, imp.shape, 0)
    jf = j.astype(F32)
    t = q0 + lax.broadcasted_iota(jnp.int32, (1, Q_TILE), 1)
    cur = t // SEL_LEN
    forced = (j == 0) | (j == cur) | (j == cur - 1)
    v = jnp.where(j > cur, -FORCE, jnp.where(forced, FORCE, imp))
    sel = jnp.zeros(imp.shape, jnp.bool_)
    for _ in range(N_SEL):
        mx = jnp.max(v, axis=0, keepdims=True)
        idx = jnp.min(jnp.where(v == mx, jf, float(LANES)), axis=0, keepdims=True)
        pick = jf == idx
        sel = sel | pick
        v = jnp.where(pick, -3e38, v)
    live_t = jnp.where(sel & (j <= cur), 1.0, 0.0).astype(BF16)
    eye = (lax.broadcasted_iota(jnp.int32, imp.shape, 0) == lax.broadcasted_iota(jnp.int32, imp.shape, 1))
    live = _dot_nt(eye.astype(BF16), live_t)
    selb = jnp.where(live > 0.5, 0.0, NEG).astype(BF16)
    return selb, jnp.max(live, axis=0, keepdims=True).astype(jnp.int32)


def _cmp_attention(qa, kvc, sm, ov):
    B, H, S, _ = qa.shape
    G = NSA_KV_GROUPS
    nq = S // Q_TILE
    qt = CMP_SUB * Q_TILE
    return pl.pallas_call(
        _cmp_kernel,
        grid=(B, G, nq // CMP_SUB),
        in_specs=[
            pl.BlockSpec((1, NSA_HPG, qt, LANES), lambda b, g, i: (b, g, i, 0)),
            pl.BlockSpec((1, 1, N_CMP_PAD, LANES), lambda b, g, i: (b, g, 0, 0)),
            pl.BlockSpec((1, 1, N_CMP_PAD, LANES), lambda b, g, i: (b, 2 + g, 0, 0)),
            pl.BlockSpec((1, qt, LANES), lambda b, g, i: (b, i, 0)),
            pl.BlockSpec((LANES, N_CMP_PAD), lambda b, g, i: (0, 0)),
        ],
        out_specs=[
            pl.BlockSpec((1, qt, NSA_HPG * HEAD_DIM), lambda b, g, i: (b, i, g)),
            pl.BlockSpec((1, 1, qt, LANES), lambda b, g, i: (b, g, i, 0)),
            pl.BlockSpec((1, 1, CMP_SUB, 1, LANES), lambda b, g, i: (b, g, i, 0, 0)),
        ],
        out_shape=[
            jax.ShapeDtypeStruct((B, S, NSA_W), F32),
            jax.ShapeDtypeStruct((B, G, S, LANES), BF16),
            jax.ShapeDtypeStruct((B, G, nq, 1, LANES), jnp.int32),
        ],
        scratch_shapes=[pltpu.VMEM((CMP_SUB, LANES, Q_TILE), F32)],
        compiler_params=_cparams(("parallel", "parallel", "parallel")),
        name="cmp_attention",
    )(qa, kvc, kvc, sm, ov)


def _online_update(s, v, m_ref, acc_ref):
    m_old = m_ref[...]
    m_new = jnp.maximum(m_old, jnp.max(s, axis=-1, keepdims=True))
    chunks = [s[:, c * LANES:(c + 1) * LANES] - m_new for c in range(s.shape[1] // LANES)]
    p = jnp.exp2(jnp.concatenate(chunks, axis=1))
    acc_ref[...] = jnp.exp2(m_old - m_new) * acc_ref[...] + _dot(p.astype(BF16), v)
    m_ref[...] = m_new


def _normalized(acc):
    return acc / jnp.maximum(acc[:, EXT:EXT + 1], 1e-30)


def _attend_once(s, v):
    m = jnp.broadcast_to(jnp.max(s, axis=-1, keepdims=True), (s.shape[0], LANES))
    chunks = [s[:, c * LANES:(c + 1) * LANES] - m for c in range(s.shape[1] // LANES)]
    p = jnp.exp2(jnp.concatenate(chunks, axis=1))
    return _normalized(_dot(p.astype(BF16), v))


def _selwin_kernel(list_ref, cnt_ref, q_ref, ks_ref, vs_ref, kw_ref, vw_ref, selb_ref, oc_ref, sm_ref,
                   o_ref, m_a, acc_a, m_b, acc_b):
    b = pl.program_id(0)
    g = pl.program_id(1)
    qb = pl.program_id(2)
    nq = pl.num_programs(2)
    rows = NSA_HPG * SW_TILE
    q4 = q_ref[0].reshape(rows, LANES)
    q_aug = jnp.concatenate([q4, jnp.concatenate([selb_ref[0, 0]] * NSA_HPG, axis=0)], axis=1)
    r = lax.broadcasted_iota(jnp.int32, (rows, 1), 0) % SW_TILE
    c = lax.broadcasted_iota(jnp.int32, (1, K_TILE), 1)
    rel = r - c
    diag = qb // (K_TILE // SW_TILE)

    def sel_tile(kt, m_ref, acc_ref, causal=False, bias=None):
        start = pl.multiple_of(kt * K_TILE, K_TILE)
        s = _dot_nt(q_aug, ks_ref[0, 0, pl.ds(start, K_TILE), :])
        if bias is not None:
            s = s + bias
        if causal:
            s = jnp.where(rel + (qb * SW_TILE - kt * K_TILE) >= 0, s, NEG)
        _online_update(s, vs_ref[0, 0, pl.ds(start, K_TILE), :], m_ref, acc_ref)

    for m_ref, acc_ref in ((m_a, acc_a), (m_b, acc_b)):
        m_ref[...] = jnp.full(m_ref.shape, NEG, F32)
        acc_ref[...] = jnp.zeros(acc_ref.shape, F32)
    step = (b * NSA_KV_GROUPS + g) * nq + qb
    count = cnt_ref[step]
    base = step * MAX_TILES

    def body(p, carry):
        second = 2 * p + 1
        sel_tile(list_ref[base + 2 * p], m_a, acc_a)
        sel_tile(list_ref[base + jnp.minimum(second, MAX_TILES - 1)], m_b, acc_b,
                 bias=jnp.where(second < count, 0.0, NEG))
        return carry

    lax.fori_loop(0, (count + 1) // 2, body, 0)
    m_new = jnp.maximum(m_a[...], m_b[...])
    acc_a[...] = jnp.exp2(m_a[...] - m_new) * acc_a[...] + jnp.exp2(m_b[...] - m_new) * acc_b[...]
    m_a[...] = m_new
    sel_tile(diag, m_a, acc_a, causal=True)
    o_sel = _normalized(acc_a[...])

    span = WINDOW + K_TILE
    wstart = pl.multiple_of(jnp.maximum(diag - WINDOW // K_TILE, 0) * K_TILE, K_TILE)
    dist = (qb * SW_TILE + r) - (wstart + lax.broadcasted_iota(jnp.int32, (1, span), 1))
    s = _dot_nt(q4, kw_ref[0, 0, pl.ds(wstart, span), :])
    s = jnp.where((dist >= 0) & (dist < WINDOW), s, NEG)
    o_win = _attend_once(s, vw_ref[0, 0, pl.ds(wstart, span), :])

    sm = sm_ref[0]
    y = _gate_rows(sm, g, 1) * o_sel + _gate_rows(sm, g, 2) * o_win
    o_ref[0] = (oc_ref[0] + _head_tile(y)).astype(BF16)


def _selwin_attention(tile_list, tile_count, qa, ksl, nkv, selb, ocg, sm):
    B, H, S, _ = qa.shape
    G = NSA_KV_GROUPS
    nq = S // SW_TILE
    rows = NSA_HPG * SW_TILE
    kv_spec = lambda piece: pl.BlockSpec((1, 1, S, LANES), lambda b, g, i, tl, tc: (b, piece + g, 0, 0))
    out_tile = pl.BlockSpec((1, SW_TILE, NSA_HPG * HEAD_DIM), lambda b, g, i, tl, tc: (b, i, g))
    grid_spec = pltpu.PrefetchScalarGridSpec(
        num_scalar_prefetch=2,
        grid=(B, G, nq),
        in_specs=[
            pl.BlockSpec((1, NSA_HPG, SW_TILE, LANES), lambda b, g, i, tl, tc: (b, g, i, 0)),
            pl.BlockSpec((1, 1, S, 2 * LANES), lambda b, g, i, tl, tc: (b, g, 0, 0)),
            kv_spec(0), kv_spec(2), kv_spec(4),
            pl.BlockSpec((1, 1, SW_TILE, LANES), lambda b, g, i, tl, tc: (b, g, i, 0)),
            out_tile,
            pl.BlockSpec((1, SW_TILE, LANES), lambda b, g, i, tl, tc: (b, i, 0)),
        ],
        out_specs=out_tile,
        scratch_shapes=[pltpu.VMEM((rows, LANES), F32)] * 4,
    )
    return pl.pallas_call(
        _selwin_kernel,
        grid_spec=grid_spec,
        out_shape=jax.ShapeDtypeStruct((B, S, NSA_W), BF16),
        compiler_params=_cparams(("parallel", "parallel", "arbitrary")),
        name="selwin_attention",
    )(tile_list, tile_count, qa, ksl, nkv, nkv, nkv, selb, ocg, sm)


def _fox_kernel(q_ref, k_ref, v_ref, o_ref, m_sc, acc_sc, *, tq):
    qi = pl.program_id(2)
    m_sc[...] = jnp.full(m_sc.shape, NEG, F32)
    acc_sc[...] = jnp.zeros(acc_sc.shape, F32)

    def tile(kt, width, causal):
        start = pl.multiple_of(kt * tq, tq)
        for hh in range(FOX_HPS):
            s = _dot_nt(q_ref[0, hh], k_ref[0, hh, pl.ds(start, width), :])
            if causal:
                r = lax.broadcasted_iota(jnp.int32, s.shape, 0)
                c = lax.broadcasted_iota(jnp.int32, s.shape, 1)
                s = jnp.where(r >= c, s, NEG)
            _online_update(s, v_ref[0, hh, pl.ds(start, width), :], m_sc.at[hh], acc_sc.at[hh])

    def body(kp, carry):
        tile(2 * kp, 2 * tq, False)
        return carry

    lax.fori_loop(0, qi // 2, body, 0)

    @pl.when(qi % 2 == 1)
    def _():
        tile(qi - 1, tq, False)

    tile(qi, tq, True)
    lane = lax.broadcasted_iota(jnp.int32, (tq, LANES), 1)
    o = [_normalized(acc_sc[hh]) for hh in range(FOX_HPS)]
    for pr in range(FOX_HPS // 2):
        o_ref[0, :, pr * LANES:(pr + 1) * LANES] = jnp.where(
            lane < HEAD_DIM, o[2 * pr], pltpu.roll(o[2 * pr + 1], HEAD_DIM, 1)).astype(BF16)


def _fox_attention(fq, fk, fv, tq=512):
    B, H, S, _ = fq.shape
    hps = FOX_HPS
    return pl.pallas_call(
        functools.partial(_fox_kernel, tq=tq),
        grid=(B, H // hps, S // tq),
        in_specs=[
            pl.BlockSpec((1, hps, tq, LANES), lambda b, h, i: (b, h, i, 0)),
            pl.BlockSpec((1, hps, S, LANES), lambda b, h, i: (b, h, 0, 0)),
            pl.BlockSpec((1, hps, S, LANES), lambda b, h, i: (b, h, 0, 0)),
        ],
        out_specs=pl.BlockSpec((1, tq, hps * HEAD_DIM), lambda b, h, i: (b, i, h)),
        out_shape=jax.ShapeDtypeStruct((B, S, FOX_W), BF16),
        scratch_shapes=[
            pltpu.VMEM((hps, tq, LANES), F32),
            pltpu.VMEM((hps, tq, LANES), F32),
        ],
        compiler_params=_cparams(("parallel", "parallel", "arbitrary")),
        name="fox_attention",
    )(fq, fk, fv)


def _merge_kernel(ya_ref, yb_ref, mg_ref, x_ref, mod_ref, gpost_ref, gpre_ref,
                  wa_ref, wb_ref, wo_ref, wrh_ref, wrl_ref, br_ref, stri_ref,
                  x1_ref, h2_ref, rt_ref, cnt_ref):
    D = D_MODEL

    @pl.when((pl.program_id(0) == 0) & (pl.program_id(1) == 0))
    def _():
        cnt_ref[...] = jnp.zeros(cnt_ref.shape, F32)

    a = _dot(ya_ref[0], wa_ref[...])
    bq = _dot(yb_ref[0], wb_ref[...])
    mg = mg_ref[0]
    u = mg[:, :D].astype(F32) * a + mg[:, D:].astype(F32) * bq
    mixed = _dot(u.astype(BF16), wo_ref[...])
    x1 = x_ref[0] + mod_ref[0, 2:3, :] * _rms(mixed, gpost_ref[...])
    x1_ref[0] = x1
    h2 = _rms(x1, gpre_ref[...]) * (1.0 + mod_ref[0, 4:5, :]) + mod_ref[0, 3:4, :]
    hi = h2.astype(BF16)
    lo = (h2 - hi.astype(F32)).astype(BF16)
    h2_ref[0] = h2
    lg = _dot(hi, wrh_ref[...]) + _dot(lo, wrh_ref[...]) + _dot(hi, wrl_ref[...]) + br_ref[...]

    lane = lax.broadcasted_iota(jnp.int32, lg.shape, 1)
    lanef = lane.astype(F32)
    no_lane = float(LANES)
    is_g = lane < N_EXPERT_GROUPS
    gl = jnp.where(is_g, lg, NEG)
    gmax = jnp.max(gl, axis=-1, keepdims=True)
    pg_top = 1.0 / jnp.sum(jnp.where(is_g, jnp.exp(gl - gmax), 0.0), axis=-1, keepdims=True)
    g_idx = jnp.min(jnp.where(is_g & (gl == gmax), lanef, no_lane), axis=-1, keepdims=True)
    in_grp = ((lane >= N_EXPERT_GROUPS) & (lane < N_EXPERT_GROUPS + N_EXPERTS)
              & (((lane - N_EXPERT_GROUPS) // EXPERTS_PER_GROUP).astype(F32) == g_idx))
    le = jnp.where(in_grp, lg, NEG)
    m1 = jnp.max(le, axis=-1, keepdims=True)
    i1 = jnp.min(jnp.where(in_grp & (le == m1), lanef, no_lane), axis=-1, keepdims=True)
    rest = in_grp & (lanef != i1)
    le2 = jnp.where(rest, lg, NEG)
    m2 = jnp.max(le2, axis=-1, keepdims=True)
    i2 = jnp.min(jnp.where(rest & (le2 == m2), lanef, no_lane), axis=-1, keepdims=True)
    e21 = jnp.exp(m2 - m1)
    w1 = pg_top / (1.0 + e21)
    w2 = w1 * e21
    pick1 = lanef == i1
    pick2 = lanef == i2
    onehot = jnp.where(pick1 | pick2, 1.0, 0.0)
    before = cnt_ref[...] + _dot(stri_ref[...], onehot.astype(BF16))
    rank1 = jnp.sum(jnp.where(pick1, before, 0.0), axis=-1, keepdims=True)
    rank2 = jnp.sum(jnp.where(pick2, before, 0.0), axis=-1, keepdims=True)
    cnt_ref[...] = cnt_ref[...] + jnp.sum(onehot, axis=0, keepdims=True)
    fields = [i1 - N_EXPERT_GROUPS, i2 - N_EXPERT_GROUPS, rank1, rank2, w1, w2]
    rt = jnp.zeros(lg.shape, F32)
    for k, f in enumerate(fields):
        rt = jnp.where(lane == k, f, rt)
    rt_ref[0] = rt


def _merge(ya, yb, mg, x, mod, gpost, gpre, wa, wb, wo, wrh, wrl, br, stri):
    B, S, D = x.shape
    tm = MERGE_TILE
    c2 = lambda b, i: (0, 0)
    row = lambda w: pl.BlockSpec((1, tm, w), lambda b, i: (b, i, 0))
    return pl.pallas_call(
        _merge_kernel,
        grid=(B, S // tm),
        in_specs=[
            row(NSA_W), row(FOX_W), row(2 * D), row(D),
            pl.BlockSpec((1, 6, D), lambda b, i: (b, 0, 0)),
            pl.BlockSpec((1, D), c2), pl.BlockSpec((1, D), c2),
            pl.BlockSpec((NSA_W, D), c2), pl.BlockSpec((FOX_W, D), c2), pl.BlockSpec((D, D), c2),
            pl.BlockSpec((D, LANES), c2), pl.BlockSpec((D, LANES), c2), pl.BlockSpec((1, LANES), c2),
            pl.BlockSpec((tm, tm), c2),
        ],
        out_specs=[row(D), row(D), row(LANES), pl.BlockSpec((1, LANES), c2)],
        out_shape=[
            jax.ShapeDtypeStruct((B, S, D), F32),
            jax.ShapeDtypeStruct((B, S, D), F32),
            jax.ShapeDtypeStruct((B, S, LANES), F32),
            jax.ShapeDtypeStruct((1, LANES), F32),
        ],
        compiler_params=_cparams(("arbitrary", "arbitrary")),
        name="merge",
    )(ya, yb, mg, x, mod, gpost, gpre, wa, wb, wo, wrh, wrl, br, stri)


def _expert_kernel(be_ref, na_ref, tok_ref, h_hbm, wg_ref, wu_ref, wd_ref, o_ref, *scratch):
    i = pl.program_id(0)
    n_active = na_ref[0]
    last_block = pl.num_programs(0) - 1
    ring = GATHER_AHEAD + 1
    bufs = scratch[:ring]
    wg_b, wu_b, wd_b, sem = scratch[ring:]

    def row_copy(blk, r, sl):
        tok = tok_ref[blk * MOE_TILE + r]
        return pltpu.make_async_copy(h_hbm.at[pl.ds(tok, 1)], bufs[sl].at[pl.ds(r, 1)], sem.at[sl])

    def wait_rows(sl):
        pltpu.make_async_copy(h_hbm.at[pl.ds(0, MOE_TILE)], bufs[sl], sem.at[sl]).wait()

    @pl.when(i == 0)
    def _():
        for ahead in range(GATHER_AHEAD):
            def body(r, carry, ahead=ahead):
                row_copy(jnp.minimum(ahead, last_block), r, ahead).start()
                return carry
            lax.fori_loop(0, MOE_TILE, body, 0, unroll=8)

    @pl.when((i == 0) | (be_ref[i] != be_ref[jnp.maximum(i - 1, 0)]))
    def _():
        wg_b[...] = wg_ref[0].astype(BF16)
        wu_b[...] = wu_ref[0].astype(BF16)
        wd_b[...] = wd_ref[0].astype(BF16)

    def step(sl):
        wait_rows(sl)
        nxt = jnp.minimum(i + GATHER_AHEAD, last_block)
        nxt_sl = (sl + GATHER_AHEAD) % ring
        for r in range(MOE_TILE):
            row_copy(nxt, r, nxt_sl).start(priority=r % 2)
        x = bufs[sl][...].astype(BF16)
        gate = _dot(x, wg_b[...])
        up = _dot(x, wu_b[...])
        mid = (gate * jax.nn.sigmoid(gate) * up).astype(BF16)
        o_ref[...] = _dot(mid, wd_b[...])

        @pl.when(i == n_active - 1)
        def _():
            for ahead in range(1, GATHER_AHEAD + 1):
                wait_rows((sl + ahead) % ring)

    for sl in range(ring):
        pl.when((i % ring == sl) & (i < n_active))(functools.partial(step, sl))

    @pl.when(i >= n_active)
    def _():
        o_ref[...] = jnp.zeros(o_ref.shape, o_ref.dtype)


def _experts(block_expert, n_active, buf_tok, h2, wg, wu, wd):
    cap = buf_tok.shape[0]
    D = D_MODEL
    nblk = cap // MOE_TILE
    grid_spec = pltpu.PrefetchScalarGridSpec(
        num_scalar_prefetch=3,
        grid=(nblk,),
        in_specs=[
            pl.BlockSpec(memory_space=pl.ANY),
            pl.BlockSpec((1, D, D_EXPERT), lambda i, be, na, tok: (be[i], 0, 0)),
            pl.BlockSpec((1, D, D_EXPERT), lambda i, be, na, tok: (be[i], 0, 0)),
            pl.BlockSpec((1, D_EXPERT, D), lambda i, be, na, tok: (be[i], 0, 0)),
        ],
        out_specs=pl.BlockSpec((MOE_TILE, D), lambda i, be, na, tok: (i, 0)),
        scratch_shapes=[
            *([pltpu.VMEM((MOE_TILE, D), F32)] * (GATHER_AHEAD + 1)),
            pltpu.VMEM((D, D_EXPERT), BF16),
            pltpu.VMEM((D, D_EXPERT), BF16),
            pltpu.VMEM((D_EXPERT, D), BF16),
            pltpu.SemaphoreType.DMA((GATHER_AHEAD + 1,)),
        ],
    )
    return pl.pallas_call(
        _expert_kernel,
        grid_spec=grid_spec,
        out_shape=jax.ShapeDtypeStruct((cap, D), F32),
        compiler_params=_cparams(("arbitrary",)),
        name="experts",
    )(block_expert, n_active, buf_tok, h2, wg, wu, wd)


def _final_kernel(dest_ref, x1_ref, rt_ref, mod_ref, g_ref, y_hbm, o_ref, *scratch):
    j = pl.program_id(0)
    last_tile = pl.num_programs(0) - 1
    tm = o_ref.shape[0]
    ring = GATHER_AHEAD + 1
    bufs = tuple(scratch[EXPERT_TOP_K * sl:EXPERT_TOP_K * (sl + 1)] for sl in range(ring))
    sem = scratch[-1]

    def row_copy(tile, r, k, sl):
        row = dest_ref[(tile * tm + r) * EXPERT_TOP_K + k]
        return pltpu.make_async_copy(y_hbm.at[pl.ds(row, 1)], bufs[sl][k].at[pl.ds(r, 1)], sem.at[sl])

    def wait_rows(sl):
        for k in range(EXPERT_TOP_K):
            pltpu.make_async_copy(y_hbm.at[pl.ds(0, tm)], bufs[sl][k], sem.at[sl]).wait()

    @pl.when(j == 0)
    def _():
        for ahead in range(GATHER_AHEAD):
            def body(r, carry, ahead=ahead):
                for k in range(EXPERT_TOP_K):
                    row_copy(jnp.minimum(ahead, last_tile), r, k, ahead).start()
                return carry
            lax.fori_loop(0, tm, body, 0, unroll=4)

    def step(sl):
        wait_rows(sl)
        nxt = jnp.minimum(j + GATHER_AHEAD, last_tile)
        nxt_sl = (sl + GATHER_AHEAD) % ring
        for r in range(tm):
            for k in range(EXPERT_TOP_K):
                row_copy(nxt, r, k, nxt_sl).start(priority=k)
        rt = rt_ref[...]
        lane = lax.broadcasted_iota(jnp.int32, rt.shape, 1)
        w0 = jnp.sum(jnp.where(lane == 4, rt, 0.0), axis=-1, keepdims=True)
        w1 = jnp.sum(jnp.where(lane == 5, rt, 0.0), axis=-1, keepdims=True)
        y = w0 * bufs[sl][0][...] + w1 * bufs[sl][1][...]
        o_ref[...] = x1_ref[...] + mod_ref[0, 5:6, :] * _rms(y, g_ref[...])

        @pl.when(j == last_tile)
        def _():
            for ahead in range(1, GATHER_AHEAD + 1):
                wait_rows((sl + ahead) % ring)

    for sl in range(ring):
        pl.when(j % ring == sl)(functools.partial(step, sl))


def _final(dest, x1, rt, mod, g, yb, tiles_per_batch):
    T, D = x1.shape
    tm = MERGE_TILE
    grid_spec = pltpu.PrefetchScalarGridSpec(
        num_scalar_prefetch=1,
        grid=(T // tm,),
        in_specs=[
            pl.BlockSpec((tm, D), lambda j, d: (j, 0)),
            pl.BlockSpec((tm, LANES), lambda j, d: (j, 0)),
            pl.BlockSpec((1, 6, D), lambda j, d: (j // tiles_per_batch, 0, 0)),
            pl.BlockSpec((1, D), lambda j, d: (0, 0)),
            pl.BlockSpec(memory_space=pl.ANY),
        ],
        out_specs=pl.BlockSpec((tm, D), lambda j, d: (j, 0)),
        scratch_shapes=([pltpu.VMEM((tm, D), F32)] * (EXPERT_TOP_K * (GATHER_AHEAD + 1))
                        + [pltpu.SemaphoreType.DMA((GATHER_AHEAD + 1,))]),
    )
    return pl.pallas_call(
        _final_kernel,
        grid_spec=grid_spec,
        out_shape=jax.ShapeDtypeStruct((T, D), F32),
        compiler_params=_cparams(("arbitrary",)),
        name="final",
    )(dest, x1, rt, mod, g, yb)


def _overlap_matrix():
    n = np.arange(N_CMP_PAD)[:, None]
    j = np.arange(LANES)[None, :]
    start = n * CMP_STRIDE
    ov = (start < j * SEL_LEN + SEL_LEN) & (start + CMP_LEN - 1 >= j * SEL_LEN) & (n < N_CMP_PAD - 1)
    return jnp.asarray(ov.T.astype(np.float32), dtype=BF16)


def _pad_cols(w, width=LANES):
    return jnp.pad(w, ((0, 0), (0, width - w.shape[1])))


def _dispatch_plan(rt, cnt, T):
    expert = rt[:, 0:2].astype(jnp.int32)
    rank = rt[:, 2:4].astype(jnp.int32)
    weight = rt[:, 4:6]
    counts = cnt[0, N_EXPERT_GROUPS:N_EXPERT_GROUPS + N_EXPERTS].astype(jnp.int32)
    padded = (counts + MOE_TILE - 1) // MOE_TILE * MOE_TILE
    pad_end = jnp.cumsum(padded)
    pad_start = pad_end - padded
    onehot = expert[:, :, None] == jnp.arange(N_EXPERTS)[None, None, :]
    dest = jnp.sum(jnp.where(onehot, pad_start[None, None, :], 0), axis=-1) + rank
    A = T * EXPERT_TOP_K
    cap = -(-(A + N_EXPERTS * (MOE_TILE - 1)) // MOE_TILE) * MOE_TILE
    nblk = cap // MOE_TILE
    n_active = (pad_end[-1] // MOE_TILE).astype(jnp.int32)
    blk = jnp.arange(nblk) * MOE_TILE
    block_expert = jnp.minimum(jnp.sum(pad_end[None, :] <= blk[:, None], axis=1), N_EXPERTS - 1)
    last = jnp.max(jnp.where(jnp.arange(nblk) < n_active, block_expert, 0))
    block_expert = jnp.where(jnp.arange(nblk) < n_active, block_expert, last).astype(jnp.int32)
    tok = jnp.arange(A, dtype=jnp.int32) // EXPERT_TOP_K
    buf_tok = jnp.zeros((cap,), jnp.int32).at[dest.reshape(A)].set(tok)
    return weight, dest, buf_tok, block_expert, n_active.reshape(1)


def kernel(x, c, w_ada, b_ada, g_pre_mix, g_post_mix, g_pre_ffn, g_post_ffn, w_in, b_forget,
           cmp_pe_k, cmp_w1_k, cmp_w2_k, cmp_pe_v, cmp_w1_v, cmp_w2_v,
           w_o_nsa, w_o_fox, w_out, w_router_group, b_router_group, w_router_expert, b_router_expert,
           w_exp_gate, w_exp_up, w_exp_down):
    B, S, D = x.shape
    T = B * S
    depth = w_ada.shape[0]
    ov = _overlap_matrix()
    tri = jnp.asarray(np.tril(np.ones((IN_TILE, IN_TILE), np.float32)), dtype=BF16)
    stri = jnp.asarray(np.tril(np.ones((MERGE_TILE, MERGE_TILE), np.float32), -1), dtype=BF16)
    row_feat = _row_features(S)
    placement = _placement()
    cmp_ext = _cmp_key_ext()
    for l in range(depth):
        mod = (jax.nn.silu(c) @ w_ada[l] + b_ada[l]).reshape(B, 6, D)
        w_qa, w_kva, w_gl, w_fox, w_f, w_mg = jnp.split(w_in[l], IN_SPLITS, axis=-1)
        w_big = jnp.concatenate([w_qa, w_kva, w_fox, w_mg], axis=1).astype(BF16)
        w_small = _pad_cols(jnp.concatenate([w_gl, w_f], axis=1)).astype(BF16)
        bf_pad = jnp.pad(b_forget[l], (F_LANE, LANES - F_LANE - FOX_HEADS)).reshape(1, LANES)
        qa, ckv, ksl, nkv, fq, fk, fv, mg, sm = _inproj(
            x, mod, g_pre_mix[l].reshape(1, D), w_big, w_small, bf_pad, tri, row_feat, placement)

        half = CMP_LEN // 2
        pe = jnp.stack([cmp_pe_k[l], cmp_pe_v[l]]).reshape(2, 2, 1, half * HEAD_DIM)
        w1 = jnp.stack([cmp_w1_k[l], cmp_w1_v[l]]).reshape(2, 2, half * HEAD_DIM, HEAD_DIM).astype(BF16)
        w2 = jnp.pad(jnp.stack([cmp_w2_k[l], cmp_w2_v[l]]), ((0, 0), (0, 0), (0, LANES - HEAD_DIM))).astype(BF16)
        kvc = _compress(ckv.reshape(B, 4, S // CMP_STRIDE, CMP_STRIDE * HEAD_DIM), pe, w1, w2, cmp_ext)
        ocg, selb, flags = _cmp_attention(qa, kvc, sm, ov)
        nq = S // SW_TILE
        per_tile = K_TILE // SEL_LEN
        tile_any = jnp.max(flags.reshape(B, NSA_KV_GROUPS, nq, SW_TILE // Q_TILE, MAX_TILES, per_tile), axis=(3, 5))
        tile_id = jnp.arange(MAX_TILES)
        diag = (jnp.arange(nq) // (K_TILE // SW_TILE))[:, None]
        active = (tile_any > 0) & (tile_id < diag)
        slot = jnp.cumsum(active, axis=-1) - 1
        hit = active[..., :, None] & (slot[..., :, None] == tile_id)
        tile_list = jnp.sum(jnp.where(hit, tile_id[:, None], 0), axis=-2).astype(jnp.int32).reshape(-1)
        tile_count = jnp.sum(active, axis=-1).astype(jnp.int32).reshape(-1)
        y_a = _selwin_attention(tile_list, tile_count, qa, ksl, nkv, selb, ocg, sm)

        y_b = _fox_attention(fq, fk, fv)

        w_r = _pad_cols(jnp.concatenate([w_router_group[l], w_router_expert[l]], axis=1))
        w_rh = w_r.astype(BF16)
        w_rl = (w_r - w_rh.astype(F32)).astype(BF16)
        b_r = _pad_cols(jnp.concatenate([b_router_group[l], b_router_expert[l]]).reshape(1, -1))
        x1, h2, rt, cnt = _merge(y_a, y_b, mg, x, mod, g_post_mix[l].reshape(1, D), g_pre_ffn[l].reshape(1, D),
                                 w_o_nsa[l].astype(BF16), w_o_fox[l].astype(BF16), w_out[l].astype(BF16),
                                 w_rh, w_rl, b_r, stri)

        weight, dest, buf_tok, block_expert, n_active = _dispatch_plan(rt.reshape(T, LANES), cnt, T)
        yb = _experts(block_expert, n_active, buf_tok, h2.reshape(T, D), w_exp_gate[l], w_exp_up[l], w_exp_down[l])
        x = _final(dest.reshape(T * EXPERT_TOP_K), x1.reshape(T, D), rt.reshape(T, LANES), mod,
                   g_post_ffn[l].reshape(1, D), yb, S // MERGE_TILE).reshape(B, S, D)
    return x
```

```python
import functools

import ml_dtypes
import numpy as np
import jax
import jax.numpy as jnp
from jax import lax
from jax.experimental import pallas as pl
from jax.experimental.pallas import tpu as pltpu

D_MODEL = 1024
HEAD_DIM = 64
NSA_HEADS = 8
NSA_KV_GROUPS = 2
NSA_HPG = NSA_HEADS // NSA_KV_GROUPS
FOX_HEADS = 8
CMP_LEN = 32
CMP_STRIDE = 16
SEL_LEN = 64
N_SEL = 16
WINDOW = 512
N_EXPERT_GROUPS = 4
EXPERTS_PER_GROUP = 8
N_EXPERTS = N_EXPERT_GROUPS * EXPERTS_PER_GROUP
EXPERT_TOP_K = 2
D_EXPERT = D_MODEL // 2
NORM_EPS = 1e-6
NEG = -1e30
FORCE = 1e9
LOG2E = 1.4426950408889634

NSA_W = NSA_HEADS * HEAD_DIM
NSA_KV_W = NSA_KV_GROUPS * HEAD_DIM
FOX_W = FOX_HEADS * HEAD_DIM
IN_SIZES = (NSA_W, 6 * NSA_KV_W, 3 * NSA_HEADS, 3 * FOX_W, FOX_HEADS, 2 * D_MODEL)
IN_SPLITS = tuple(int(v) for v in np.cumsum(IN_SIZES)[:-1])

LANES = 128
Q_TILE = 128
K_TILE = 256
SW_TILE = 256
N_CMP_PAD = 512
MOE_TILE = 256
IN_TILE = 512
MERGE_TILE = 512
FINAL_TILE = 256
FOX_HPS = 4
CMP_SUB = 2
MAX_TILES = 32
GATHER_AHEAD = 3
VMEM_LIMIT = 56 * 1024 * 1024

F_LANE = 3 * NSA_HEADS
U_LANE = 64
ONE_LANE = 88
A_LANE = 89
B_LANE = 90
EXT = HEAD_DIM
G_FQ, G_FK, G_NQ, G_NK, N_GROUPS = 0, 8, 16, 24, 25

F32 = jnp.float32
BF16 = jnp.bfloat16


def _dot(a, b):
    return jnp.dot(a, b, preferred_element_type=F32)


def _dot_nt(a, b):
    return lax.dot_general(a, b, (((1,), (1,)), ((), ())), preferred_element_type=F32)


def _rms(x, g):
    return x * lax.rsqrt(jnp.mean(x * x, axis=-1, keepdims=True) + NORM_EPS) * g


def _cparams(sem):
    return pltpu.CompilerParams(dimension_semantics=sem, vmem_limit_bytes=VMEM_LIMIT)


def _split3(x):
    hi = x.astype(BF16).astype(F32)
    r = x - hi
    mid = r.astype(BF16).astype(F32)
    lo = (r - mid).astype(BF16).astype(F32)
    return hi, mid, lo


def _np_split3(x):
    x = np.asarray(x, np.float32)
    hi = x.astype(ml_dtypes.bfloat16).astype(np.float32)
    r = x - hi
    mid = r.astype(ml_dtypes.bfloat16).astype(np.float32)
    lo = (r - mid).astype(ml_dtypes.bfloat16).astype(np.float32)
    return hi, mid, lo


def _alibi_c():
    slopes = np.exp2(-8.0 * np.arange(1, NSA_HEADS + 1, dtype=np.float32) / NSA_HEADS).astype(np.float32)
    return slopes * np.float32(LOG2E)


def _row_features(S):
    t = np.arange(S, dtype=np.float32)
    c = _alibi_c()
    rs = np.zeros((S, LANES), np.float32)
    for h in range(NSA_HEADS):
        for j, term in enumerate(_np_split3(c[h] * t)):
            rs[:, U_LANE + 8 * j + h] = -term
    rs[:, ONE_LANE] = 1.0
    rs[:, A_LANE] = np.floor(t / LANES)
    rs[:, B_LANE] = t % LANES
    return jnp.asarray(rs, dtype=BF16)


def _placement():
    c = _alibi_c()
    p = np.zeros((LANES, N_GROUPS * LANES), np.float32)
    for h in range(FOX_HEADS):
        q0 = (G_FQ + h) * LANES + EXT
        k0 = (G_FK + h) * LANES + EXT
        for j in range(3):
            p[ONE_LANE, q0 + j] = -1.0
            p[F_LANE + 8 * j + h, q0 + 3 + j] = 1.0
            p[F_LANE + 8 * j + h, k0 + j] = 1.0
            p[ONE_LANE, k0 + 3 + j] = 1.0
    for h in range(NSA_HEADS):
        q0 = (G_NQ + h) * LANES + EXT
        c128 = _np_split3(c[h] * np.float32(LANES))
        c1 = _np_split3(c[h])
        for j in range(3):
            p[U_LANE + 8 * j + h, q0 + j] = 1.0
            p[ONE_LANE, q0 + 3 + j] = c128[j]
            p[ONE_LANE, q0 + 6 + j] = c1[j]
    k0 = G_NK * LANES + EXT
    for j in range(3):
        p[ONE_LANE, k0 + j] = 1.0
        p[A_LANE, k0 + 3 + j] = 1.0
        p[B_LANE, k0 + 6 + j] = 1.0
    return jnp.asarray(p, dtype=BF16)


def _cmp_key_ext():
    pos = np.arange(N_CMP_PAD, dtype=np.float32) * CMP_STRIDE + (CMP_LEN - 1)
    e = np.zeros((2, N_CMP_PAD, LANES), np.float32)
    for j in range(3):
        e[0, :, EXT + j] = 1.0
        e[0, :, EXT + 3 + j] = np.floor(pos / LANES)
        e[0, :, EXT + 6 + j] = pos % LANES
    return jnp.asarray(e, dtype=BF16)


def _adaln_kernel(c_ref, w_ref, b_ref, o_ref):
    c = c_ref[...]
    act = (c * jax.nn.sigmoid(c)).astype(BF16)
    o_ref[...] = _dot(act, w_ref[...].astype(BF16)) + b_ref[...]


def _adaln(c, w, b):
    B, D = c.shape
    n = w.shape[1]
    return pl.pallas_call(
        _adaln_kernel,
        grid=(n // D,),
        in_specs=[
            pl.BlockSpec((B, D), lambda j: (0, 0)),
            pl.BlockSpec((D, D), lambda j: (0, j)),
            pl.BlockSpec((1, D), lambda j: (0, j)),
        ],
        out_specs=pl.BlockSpec((B, D), lambda j: (0, j)),
        out_shape=jax.ShapeDtypeStruct((B, n), F32),
        compiler_params=_cparams(("parallel",)),
        name="adaln",
    )(c, w, b)


def _inproj_kernel(x_ref, mod_ref, g_ref, wb_ref, ws_ref, bf_ref, tri_ref, rs_ref, p_ref,
                   qa_ref, ckv_ref, ksl_ref, nkv_ref, fq_ref, fk_ref, fv_ref, mg_ref, sm_ref, carry_sc):
    i = pl.program_id(1)
    tm = x_ref.shape[1]
    x = x_ref[0]
    h = _rms(x, g_ref[...]) * (1.0 + mod_ref[0, 1:2, :]) + mod_ref[0, 0:1, :]
    hb = h.astype(BF16)
    lane = lax.broadcasted_iota(jnp.int32, (tm, LANES), 1)
    lower = lane < HEAD_DIM
    ones_col = (lane == EXT).astype(F32)

    z = _dot(hb, ws_ref[...]) + bf_ref[...]
    logsig = jnp.minimum(z, 0.0) - jnp.log1p(jnp.exp(-jnp.abs(z)))
    sm_ref[0] = jnp.where(lane < F_LANE, jax.nn.sigmoid(z), logsig)

    @pl.when(i == 0)
    def _():
        carry_sc[...] = jnp.zeros(carry_sc.shape, F32)

    is_f = (lane >= F_LANE) & (lane < F_LANE + FOX_HEADS)
    l_hi, l_mid, l_lo = _split3(jnp.where(is_f, logsig, 0.0))
    tri = tri_ref[...]
    cum = carry_sc[...] + _dot(tri, l_hi.astype(BF16)) + _dot(tri, l_mid.astype(BF16)) + _dot(tri, l_lo.astype(BF16))
    carry_sc[...] = cum[tm - 1:tm, :]
    f_hi, f_mid, f_lo = _split3(cum * LOG2E)
    feat = (f_hi + pltpu.roll(f_mid, 8, 1) + pltpu.roll(f_lo, 16, 1) + rs_ref[...].astype(F32)).astype(BF16)

    def ext(group):
        return _dot(feat, p_ref[:, group * LANES:(group + 1) * LANES])

    def piece(acc, idx, extra):
        pair = acc[:, (idx // 2) * LANES:(idx // 2 + 1) * LANES]
        if idx % 2:
            pair = pltpu.roll(pair, HEAD_DIM, 1)
        return jnp.where(lower, pair, extra).astype(BF16)

    qscale = (HEAD_DIM ** -0.5) * LOG2E
    acc = _dot(hb, wb_ref[:, 0:NSA_W]) * qscale
    for hd in range(NSA_HEADS):
        qa_ref[0, hd] = piece(acc, hd, ext(G_NQ + hd))
    off = NSA_W
    acc = _dot(hb, wb_ref[:, off:off + 6 * NSA_KV_W])
    for pc in range(4):
        ckv_ref[0, pc] = acc[:, pc * HEAD_DIM:(pc + 1) * HEAD_DIM].astype(BF16)
    ext_k = ext(G_NK)
    t = i * tm + lax.broadcasted_iota(jnp.int32, (tm, LANES), 0)
    block_onehot = (lane == t // SEL_LEN).astype(BF16)
    for g in range(NSA_KV_GROUPS):
        ksl_ref[0, g, :, 0:LANES] = piece(acc, 4 + g, ext_k)
        ksl_ref[0, g, :, LANES:2 * LANES] = block_onehot
        nkv_ref[0, g] = piece(acc, 6 + g, ones_col)
        nkv_ref[0, 2 + g] = piece(acc, 8 + g, ext_k)
        nkv_ref[0, 4 + g] = piece(acc, 10 + g, ones_col)
    off += 6 * NSA_KV_W
    acc = _dot(hb, wb_ref[:, off:off + FOX_W]) * qscale
    for hd in range(FOX_HEADS):
        fq_ref[0, hd] = piece(acc, hd, ext(G_FQ + hd))
    off += FOX_W
    acc = _dot(hb, wb_ref[:, off:off + FOX_W])
    for hd in range(FOX_HEADS):
        fk_ref[0, hd] = piece(acc, hd, ext(G_FK + hd))
    off += FOX_W
    acc = _dot(hb, wb_ref[:, off:off + FOX_W])
    for hd in range(FOX_HEADS):
        fv_ref[0, hd] = piece(acc, hd, ones_col)
    off += FOX_W
    for c in range(4):
        acc = _dot(hb, wb_ref[:, off + c * 512: off + (c + 1) * 512])
        mg_ref[0, :, c * 512:(c + 1) * 512] = jax.nn.sigmoid(acc).astype(BF16)


def _inproj(x, mod, g, wb, ws, bfp, tri, rs, pm):
    B, S, D = x.shape
    tm = IN_TILE
    nb = wb.shape[1]
    const2 = lambda b, i: (0, 0)
    heads = lambda n: pl.BlockSpec((1, n, tm, LANES), lambda b, i: (b, 0, i, 0))
    hshape = lambda n: jax.ShapeDtypeStruct((B, n, S, LANES), BF16)
    return pl.pallas_call(
        _inproj_kernel,
        grid=(B, S // tm),
        in_specs=[
            pl.BlockSpec((1, tm, D), lambda b, i: (b, i, 0)),
            pl.BlockSpec((1, 6, D), lambda b, i: (b, 0, 0)),
            pl.BlockSpec((1, D), const2),
            pl.BlockSpec((D, nb), const2),
            pl.BlockSpec((D, LANES), const2),
            pl.BlockSpec((1, LANES), const2),
            pl.BlockSpec((tm, tm), const2),
            pl.BlockSpec((tm, LANES), lambda b, i: (i, 0)),
            pl.BlockSpec((LANES, N_GROUPS * LANES), const2),
        ],
        out_specs=[
            heads(NSA_HEADS),
            pl.BlockSpec((1, 4, tm, HEAD_DIM), lambda b, i: (b, 0, i, 0)),
            pl.BlockSpec((1, NSA_KV_GROUPS, tm, 2 * LANES), lambda b, i: (b, 0, i, 0)),
            heads(6), heads(FOX_HEADS), heads(FOX_HEADS), heads(FOX_HEADS),
            pl.BlockSpec((1, tm, 2 * D), lambda b, i: (b, i, 0)),
            pl.BlockSpec((1, tm, LANES), lambda b, i: (b, i, 0)),
        ],
        out_shape=[
            hshape(NSA_HEADS),
            jax.ShapeDtypeStruct((B, 4, S, HEAD_DIM), BF16),
            jax.ShapeDtypeStruct((B, NSA_KV_GROUPS, S, 2 * LANES), BF16),
            hshape(6), hshape(FOX_HEADS), hshape(FOX_HEADS), hshape(FOX_HEADS),
            jax.ShapeDtypeStruct((B, S, 2 * D), BF16),
            jax.ShapeDtypeStruct((B, S, LANES), F32),
        ],
        scratch_shapes=[pltpu.VMEM((1, LANES), F32)],
        compiler_params=_cparams(("parallel", "arbitrary")),
        name="inproj",
    )(x, mod, g, wb, ws, bfp, tri, rs, pm)


def _compress_kernel(x_ref, pe_ref, w1_ref, w2_ref, e_ref, o_ref):
    x = x_ref[0, 0].astype(F32)
    x_lo = (x + pe_ref[0, 0]).astype(BF16)
    x_hi = (x + pe_ref[0, 1]).astype(BF16)
    y_lo = _dot(x_lo, w1_ref[0, 0])
    y_hi = _dot(x_hi, w1_ref[0, 1])
    n = y_hi.shape[0]
    hid = y_lo + pltpu.roll(y_hi, n - 1, 0)
    hid = jax.nn.gelu(hid)
    o_ref[0, 0] = (_dot(hid.astype(BF16), w2_ref[0]) + e_ref[0].astype(F32)).astype(BF16)


def _compress(kv_rows, pe, w1, w2, e):
    B = kv_rows.shape[0]
    R, C = kv_rows.shape[2], kv_rows.shape[3]
    return pl.pallas_call(
        _compress_kernel,
        grid=(B, 4),
        in_specs=[
            pl.BlockSpec((1, 1, R, C), lambda b, p: (b, p, 0, 0)),
            pl.BlockSpec((1, 2, 1, C), lambda b, p: (p // 2, 0, 0, 0)),
            pl.BlockSpec((1, 2, C, HEAD_DIM), lambda b, p: (p // 2, 0, 0, 0)),
            pl.BlockSpec((1, HEAD_DIM, LANES), lambda b, p: (p // 2, 0, 0)),
            pl.BlockSpec((1, R, LANES), lambda b, p: (p // 2, 0, 0)),
        ],
        out_specs=pl.BlockSpec((1, 1, R, LANES), lambda b, p: (b, p, 0, 0)),
        out_shape=jax.ShapeDtypeStruct((B, 4, R, LANES), BF16),
        compiler_params=_cparams(("parallel", "parallel")),
        name="compress",
    )(kv_rows, pe, w1, w2, e)


def _gate_rows(sm, g, branch):
    col = lax.broadcasted_iota(jnp.int32, sm.shape, 1)
    parts = []
    for hl in range(NSA_HPG):
        want = 3 * (NSA_HPG * g + hl) + branch
        parts.append(jnp.sum(jnp.where(col == want, sm, 0.0), axis=-1, keepdims=True))
    return jnp.concatenate(parts, axis=0)


def _head_tile(y):
    n = y.shape[0] // NSA_HPG
    lane = lax.broadcasted_iota(jnp.int32, (n, LANES), 1)
    hs = [y[i * n:(i + 1) * n] for i in range(NSA_HPG)]
    pairs = [jnp.where(lane < HEAD_DIM, hs[2 * i], pltpu.roll(hs[2 * i + 1], HEAD_DIM, 1)) for i in range(2)]
    return jnp.concatenate(pairs, axis=1)


def _cmp_kernel(q_ref, kc_ref, vc_ref, sm_ref, ovt_ref, oc_ref, selb_ref, flag_ref, imp_sc):
    g = pl.program_id(1)
    step_q0 = pl.program_id(2) * CMP_SUB * Q_TILE
    last_visible = (step_q0 + CMP_SUB * Q_TILE - CMP_LEN) // CMP_STRIDE
    chunks = last_visible // LANES + 1

    def attend(width):
        for sub in range(CMP_SUB):
            rows = pl.ds(sub * Q_TILE, Q_TILE)
            q = q_ref[0, :, rows, :].reshape(NSA_HPG * Q_TILE, LANES)
            oc, imp = _cmp_attend(q, kc_ref[0, 0, 0:width, :], vc_ref[0, 0, 0:width, :], sm_ref[0, rows, :],
                                  ovt_ref[:, 0:width], g, step_q0 + sub * Q_TILE)
            oc_ref[0, rows, :] = oc
            imp_sc[sub] = imp

    for v in range(1, N_CMP_PAD // LANES + 1):
        pl.when(chunks == v)(functools.partial(attend, v * LANES))

    for sub in range(CMP_SUB):
        selb, flag = _select_blocks(imp_sc[sub], step_q0 + sub * Q_TILE)
        selb_ref[0, 0, pl.ds(sub * Q_TILE, Q_TILE), :] = selb
        flag_ref[0, 0, sub] = flag


def _cmp_attend(q, kc, vc, sm, ovt, g, q0):
    width = kc.shape[0]
    s = _dot_nt(q, kc)
    r = lax.broadcasted_iota(jnp.int32, (NSA_HPG * Q_TILE, 1), 0) % Q_TILE
    n = lax.broadcasted_iota(jnp.int32, (1, width), 1)
    dc = (q0 + r) - (n * CMP_STRIDE + (CMP_LEN - 1))
    mask = (dc >= 0) & (n < N_CMP_PAD - 1)
    l = jnp.where(mask, s, NEG)
    m = jnp.max(l, axis=-1, keepdims=True)
    e = jnp.where(mask, jnp.exp2(l - m), 0.0)
    pc = e / jnp.maximum(jnp.sum(e, axis=-1, keepdims=True), 1e-30)
    oc = _dot(pc.astype(BF16), vc)
    oc = _head_tile(oc * _gate_rows(sm, g, 0))
    ps = pc[0:Q_TILE]
    for i in range(1, NSA_HPG):
        ps = ps + pc[i * Q_TILE:(i + 1) * Q_TILE]
    ps_hi = ps.astype(BF16)
    ps_lo = (ps - ps_hi.astype(F32)).astype(BF16)
    return oc, _dot_nt(ovt, ps_hi) + _dot_nt(ovt, ps_lo)


def _select_blocks(imp, q0):
    j = lax.broadcasted_iota(jnp.int32, imp.shape, 0)
    jf = j.astype(F32)
    t = q0 + lax.broadcasted_iota(jnp.int32, (1, Q_TILE), 1)
    cur = t // SEL_LEN
    forced = (j == 0) | (j == cur) | (j == cur - 1)
    v = jnp.where(j > cur, -FORCE, jnp.where(forced, FORCE, imp))
    sel = jnp.zeros(imp.shape, jnp.bool_)
    for _ in range(N_SEL):
        mx = jnp.max(v, axis=0, keepdims=True)
        idx = jnp.min(jnp.where(v == mx, jf, float(LANES)), axis=0, keepdims=True)
        pick = jf == idx
        sel = sel | pick
        v = jnp.where(pick, -3e38, v)
    live_t = jnp.where(sel & (j <= cur), 1.0, 0.0).astype(BF16)
    eye = (lax.broadcasted_iota(jnp.int32, imp.shape, 0) == lax.broadcasted_iota(jnp.int32, imp.shape, 1))
    live = _dot_nt(eye.astype(BF16), live_t)
    selb = jnp.where(live > 0.5, 0.0, NEG).astype(BF16)
    return selb, jnp.max(live, axis=0, keepdims=True).astype(jnp.int32)


def _cmp_attention(qa, kvc, sm, ov):
    B, H, S, _ = qa.shape
    G = NSA_KV_GROUPS
    nq = S // Q_TILE
    qt = CMP_SUB * Q_TILE
    return pl.pallas_call(
        _cmp_kernel,
        grid=(B, G, nq // CMP_SUB),
        in_specs=[
            pl.BlockSpec((1, NSA_HPG, qt, LANES), lambda b, g, i: (b, g, i, 0)),
            pl.BlockSpec((1, 1, N_CMP_PAD, LANES), lambda b, g, i: (b, g, 0, 0)),
            pl.BlockSpec((1, 1, N_CMP_PAD, LANES), lambda b, g, i: (b, 2 + g, 0, 0)),
            pl.BlockSpec((1, qt, LANES), lambda b, g, i: (b, i, 0)),
            pl.BlockSpec((LANES, N_CMP_PAD), lambda b, g, i: (0, 0)),
        ],
        out_specs=[
            pl.BlockSpec((1, qt, NSA_HPG * HEAD_DIM), lambda b, g, i: (b, i, g)),
            pl.BlockSpec((1, 1, qt, LANES), lambda b, g, i: (b, g, i, 0)),
            pl.BlockSpec((1, 1, CMP_SUB, 1, LANES), lambda b, g, i: (b, g, i, 0, 0)),
        ],
        out_shape=[
            jax.ShapeDtypeStruct((B, S, NSA_W), F32),
            jax.ShapeDtypeStruct((B, G, S, LANES), BF16),
            jax.ShapeDtypeStruct((B, G, nq, 1, LANES), jnp.int32),
        ],
        scratch_shapes=[pltpu.VMEM((CMP_SUB, LANES, Q_TILE), F32)],
        compiler_params=_cparams(("parallel", "parallel", "parallel")),
        name="cmp_attention",
    )(qa, kvc, kvc, sm, ov)


def _online_update(s, v, m_ref, acc_ref):
    m_old = m_ref[...]
    m_new = jnp.maximum(m_old, jnp.max(s, axis=-1, keepdims=True))
    chunks = [s[:, c * LANES:(c + 1) * LANES] - m_new for c in range(s.shape[1] // LANES)]
    p = jnp.exp2(jnp.concatenate(chunks, axis=1))
    acc_ref[...] = jnp.exp2(m_old - m_new) * acc_ref[...] + _dot(p.astype(BF16), v)
    m_ref[...] = m_new


def _normalized(acc):
    return acc / jnp.maximum(acc[:, EXT:EXT + 1], 1e-30)


def _attend_once(s, v):
    m = jnp.broadcast_to(jnp.max(s, axis=-1, keepdims=True), (s.shape[0], LANES))
    chunks = [s[:, c * LANES:(c + 1) * LANES] - m for c in range(s.shape[1] // LANES)]
    p = jnp.exp2(jnp.concatenate(chunks, axis=1))
    return _normalized(_dot(p.astype(BF16), v))


def _selwin_kernel(list_ref, cnt_ref, q_ref, ks_ref, vs_ref, kw_ref, vw_ref, selb_ref, oc_ref, sm_ref,
                   o_ref, m_a, acc_a, m_b, acc_b):
    b = pl.program_id(0)
    g = pl.program_id(1)
    qb = pl.program_id(2)
    nq = pl.num_programs(2)
    rows = NSA_HPG * SW_TILE
    q4 = q_ref[0].reshape(rows, LANES)
    q_aug = jnp.concatenate([q4, jnp.concatenate([selb_ref[0, 0]] * NSA_HPG, axis=0)], axis=1)
    r = lax.broadcasted_iota(jnp.int32, (rows, 1), 0) % SW_TILE
    c = lax.broadcasted_iota(jnp.int32, (1, K_TILE), 1)
    rel = r - c
    diag = qb // (K_TILE // SW_TILE)

    def sel_tile(kt, m_ref, acc_ref, causal=False, bias=None):
        start = pl.multiple_of(kt * K_TILE, K_TILE)
        s = _dot_nt(q_aug, ks_ref[0, 0, pl.ds(start, K_TILE), :])
        if bias is not None:
            s = s + bias
        if causal:
            s = jnp.where(rel + (qb * SW_TILE - kt * K_TILE) >= 0, s, NEG)
        _online_update(s, vs_ref[0, 0, pl.ds(start, K_TILE), :], m_ref, acc_ref)

    for m_ref, acc_ref in ((m_a, acc_a), (m_b, acc_b)):
        m_ref[...] = jnp.full(m_ref.shape, NEG, F32)
        acc_ref[...] = jnp.zeros(acc_ref.shape, F32)
    step = (b * NSA_KV_GROUPS + g) * nq + qb
    count = cnt_ref[step]
    base = step * MAX_TILES

    def body(p, carry):
        second = 2 * p + 1
        sel_tile(list_ref[base + 2 * p], m_a, acc_a)
        sel_tile(list_ref[base + jnp.minimum(second, MAX_TILES - 1)], m_b, acc_b,
                 bias=jnp.where(second < count, 0.0, NEG))
        return carry

    lax.fori_loop(0, (count + 1) // 2, body, 0)
    m_new = jnp.maximum(m_a[...], m_b[...])
    acc_a[...] = jnp.exp2(m_a[...] - m_new) * acc_a[...] + jnp.exp2(m_b[...] - m_new) * acc_b[...]
    m_a[...] = m_new
    sel_tile(diag, m_a, acc_a, causal=True)
    o_sel = _normalized(acc_a[...])

    span = WINDOW + K_TILE
    wstart = pl.multiple_of(jnp.maximum(diag - WINDOW // K_TILE, 0) * K_TILE, K_TILE)
    dist = (qb * SW_TILE + r) - (wstart + lax.broadcasted_iota(jnp.int32, (1, span), 1))
    s = _dot_nt(q4, kw_ref[0, 0, pl.ds(wstart, span), :])
    s = jnp.where((dist >= 0) & (dist < WINDOW), s, NEG)
    o_win = _attend_once(s, vw_ref[0, 0, pl.ds(wstart, span), :])

    sm = sm_ref[0]
    y = _gate_rows(sm, g, 1) * o_sel + _gate_rows(sm, g, 2) * o_win
    o_ref[0] = (oc_ref[0] + _head_tile(y)).astype(BF16)


def _selwin_attention(tile_list, tile_count, qa, ksl, nkv, selb, ocg, sm):
    B, H, S, _ = qa.shape
    G = NSA_KV_GROUPS
    nq = S // SW_TILE
    rows = NSA_HPG * SW_TILE
    kv_spec = lambda piece: pl.BlockSpec((1, 1, S, LANES), lambda b, g, i, tl, tc: (b, piece + g, 0, 0))
    out_tile = pl.BlockSpec((1, SW_TILE, NSA_HPG * HEAD_DIM), lambda b, g, i, tl, tc: (b, i, g))
    grid_spec = pltpu.PrefetchScalarGridSpec(
        num_scalar_prefetch=2,
        grid=(B, G, nq),
        in_specs=[
            pl.BlockSpec((1, NSA_HPG, SW_TILE, LANES), lambda b, g, i, tl, tc: (b, g, i, 0)),
            pl.BlockSpec((1, 1, S, 2 * LANES), lambda b, g, i, tl, tc: (b, g, 0, 0)),
            kv_spec(0), kv_spec(2), kv_spec(4),
            pl.BlockSpec((1, 1, SW_TILE, LANES), lambda b, g, i, tl, tc: (b, g, i, 0)),
            out_tile,
            pl.BlockSpec((1, SW_TILE, LANES), lambda b, g, i, tl, tc: (b, i, 0)),
        ],
        out_specs=out_tile,
        scratch_shapes=[pltpu.VMEM((rows, LANES), F32)] * 4,
    )
    return pl.pallas_call(
        _selwin_kernel,
        grid_spec=grid_spec,
        out_shape=jax.ShapeDtypeStruct((B, S, NSA_W), BF16),
        compiler_params=_cparams(("parallel", "parallel", "arbitrary")),
        name="selwin_attention",
    )(tile_list, tile_count, qa, ksl, nkv, nkv, nkv, selb, ocg, sm)


def _fox_kernel(q_ref, k_ref, v_ref, o_ref, m_sc, acc_sc, *, tq):
    qi = pl.program_id(2)
    m_sc[...] = jnp.full(m_sc.shape, NEG, F32)
    acc_sc[...] = jnp.zeros(acc_sc.shape, F32)

    def tile(kt, width, causal):
        start = pl.multiple_of(kt * tq, tq)
        for hh in range(FOX_HPS):
            s = _dot_nt(q_ref[0, hh], k_ref[0, hh, pl.ds(start, width), :])
            if causal:
                r = lax.broadcasted_iota(jnp.int32, s.shape, 0)
                c = lax.broadcasted_iota(jnp.int32, s.shape, 1)
                s = jnp.where(r >= c, s, NEG)
            _online_update(s, v_ref[0, hh, pl.ds(start, width), :], m_sc.at[hh], acc_sc.at[hh])

    def body(kp, carry):
        tile(2 * kp, 2 * tq, False)
        return carry

    lax.fori_loop(0, qi // 2, body, 0)

    @pl.when(qi % 2 == 1)
    def _():
        tile(qi - 1, tq, False)

    tile(qi, tq, True)
    lane = lax.broadcasted_iota(jnp.int32, (tq, LANES), 1)
    o = [_normalized(acc_sc[hh]) for hh in range(FOX_HPS)]
    for pr in range(FOX_HPS // 2):
        o_ref[0, :, pr * LANES:(pr + 1) * LANES] = jnp.where(
            lane < HEAD_DIM, o[2 * pr], pltpu.roll(o[2 * pr + 1], HEAD_DIM, 1)).astype(BF16)


def _fox_attention(fq, fk, fv, tq=512):
    B, H, S, _ = fq.shape
    hps = FOX_HPS
    return pl.pallas_call(
        functools.partial(_fox_kernel, tq=tq),
        grid=(B, H // hps, S // tq),
        in_specs=[
            pl.BlockSpec((1, hps, tq, LANES), lambda b, h, i: (b, h, i, 0)),
            pl.BlockSpec((1, hps, S, LANES), lambda b, h, i: (b, h, 0, 0)),
            pl.BlockSpec((1, hps, S, LANES), lambda b, h, i: (b, h, 0, 0)),
        ],
        out_specs=pl.BlockSpec((1, tq, hps * HEAD_DIM), lambda b, h, i: (b, i, h)),
        out_shape=jax.ShapeDtypeStruct((B, S, FOX_W), BF16),
        scratch_shapes=[
            pltpu.VMEM((hps, tq, LANES), F32),
            pltpu.VMEM((hps, tq, LANES), F32),
        ],
        compiler_params=_cparams(("parallel", "parallel", "arbitrary")),
        name="fox_attention",
    )(fq, fk, fv)


def _merge_kernel(ya_ref, yb_ref, mg_ref, x_ref, mod_ref, gpost_ref, gpre_ref,
                  wa_ref, wb_ref, wo_ref, wrh_ref, wrl_ref, br_ref, stri_ref,
                  x1_ref, h2_ref, rt_ref, cnt_ref):
    D = D_MODEL

    @pl.when((pl.program_id(0) == 0) & (pl.program_id(1) == 0))
    def _():
        cnt_ref[...] = jnp.zeros(cnt_ref.shape, F32)

    a = _dot(ya_ref[0], wa_ref[...])
    bq = _dot(yb_ref[0], wb_ref[...])
    mg = mg_ref[0]
    u = mg[:, :D].astype(F32) * a + mg[:, D:].astype(F32) * bq
    mixed = _dot(u.astype(BF16), wo_ref[...])
    x1 = x_ref[0] + mod_ref[0, 2:3, :] * _rms(mixed, gpost_ref[...])
    x1_ref[0] = x1
    h2 = _rms(x1, gpre_ref[...]) * (1.0 + mod_ref[0, 4:5, :]) + mod_ref[0, 3:4, :]
    hi = h2.astype(BF16)
    lo = (h2 - hi.astype(F32)).astype(BF16)
    h2_ref[0] = h2
    lg = _dot(hi, wrh_ref[...]) + _dot(lo, wrh_ref[...]) + _dot(hi, wrl_ref[...]) + br_ref[...]

    lane = lax.broadcasted_iota(jnp.int32, lg.shape, 1)
    lanef = lane.astype(F32)
    no_lane = float(LANES)
    is_g = lane < N_EXPERT_GROUPS
    gl = jnp.where(is_g, lg, NEG)
    gmax = jnp.max(gl, axis=-1, keepdims=True)
    pg_top = 1.0 / jnp.sum(jnp.where(is_g, jnp.exp(gl - gmax), 0.0), axis=-1, keepdims=True)
    g_idx = jnp.min(jnp.where(is_g & (gl == gmax), lanef, no_lane), axis=-1, keepdims=True)
    in_grp = ((lane >= N_EXPERT_GROUPS) & (lane < N_EXPERT_GROUPS + N_EXPERTS)
              & (((lane - N_EXPERT_GROUPS) // EXPERTS_PER_GROUP).astype(F32) == g_idx))
    le = jnp.where(in_grp, lg, NEG)
    m1 = jnp.max(le, axis=-1, keepdims=True)
    i1 = jnp.min(jnp.where(in_grp & (le == m1), lanef, no_lane), axis=-1, keepdims=True)
    rest = in_grp & (lanef != i1)
    le2 = jnp.where(rest, lg, NEG)
    m2 = jnp.max(le2, axis=-1, keepdims=True)
    i2 = jnp.min(jnp.where(rest & (le2 == m2), lanef, no_lane), axis=-1, keepdims=True)
    e21 = jnp.exp(m2 - m1)
    w1 = pg_top / (1.0 + e21)
    w2 = w1 * e21
    pick1 = lanef == i1
    pick2 = lanef == i2
    onehot = jnp.where(pick1 | pick2, 1.0, 0.0)
    before = cnt_ref[...] + _dot(stri_ref[...], onehot.astype(BF16))
    rank1 = jnp.sum(jnp.where(pick1, before, 0.0), axis=-1, keepdims=True)
    rank2 = jnp.sum(jnp.where(pick2, before, 0.0), axis=-1, keepdims=True)
    cnt_ref[...] = cnt_ref[...] + jnp.sum(onehot, axis=0, keepdims=True)
    fields = [i1 - N_EXPERT_GROUPS, i2 - N_EXPERT_GROUPS, rank1, rank2, w1, w2]
    rt = jnp.zeros(lg.shape, F32)
    for k, f in enumerate(fields):
        rt = jnp.where(lane == k, f, rt)
    rt_ref[0] = rt


def _merge(ya, yb, mg, x, mod, gpost, gpre, wa, wb, wo, wrh, wrl, br, stri):
    B, S, D = x.shape
    tm = MERGE_TILE
    c2 = lambda b, i: (0, 0)
    row = lambda w: pl.BlockSpec((1, tm, w), lambda b, i: (b, i, 0))
    return pl.pallas_call(
        _merge_kernel,
        grid=(B, S // tm),
        in_specs=[
            row(NSA_W), row(FOX_W), row(2 * D), row(D),
            pl.BlockSpec((1, 6, D), lambda b, i: (b, 0, 0)),
            pl.BlockSpec((1, D), c2), pl.BlockSpec((1, D), c2),
            pl.BlockSpec((NSA_W, D), c2), pl.BlockSpec((FOX_W, D), c2), pl.BlockSpec((D, D), c2),
            pl.BlockSpec((D, LANES), c2), pl.BlockSpec((D, LANES), c2), pl.BlockSpec((1, LANES), c2),
            pl.BlockSpec((tm, tm), c2),
        ],
        out_specs=[row(D), row(D), row(LANES), pl.BlockSpec((1, LANES), c2)],
        out_shape=[
            jax.ShapeDtypeStruct((B, S, D), F32),
            jax.ShapeDtypeStruct((B, S, D), F32),
            jax.ShapeDtypeStruct((B, S, LANES), F32),
            jax.ShapeDtypeStruct((1, LANES), F32),
        ],
        compiler_params=_cparams(("arbitrary", "arbitrary")),
        name="merge",
    )(ya, yb, mg, x, mod, gpost, gpre, wa, wb, wo, wrh, wrl, br, stri)


def _expert_kernel(be_ref, na_ref, tok_ref, h_hbm, wg_ref, wu_ref, wd_ref, o_ref, *scratch):
    i = pl.program_id(0)
    n_active = na_ref[0]
    last_block = pl.num_programs(0) - 1
    ring = GATHER_AHEAD + 1
    bufs = scratch[:ring]
    wg_b, wu_b, wd_b, sem = scratch[ring:]

    def row_copy(blk, r, sl):
        tok = tok_ref[blk * MOE_TILE + r]
        return pltpu.make_async_copy(h_hbm.at[pl.ds(tok, 1)], bufs[sl].at[pl.ds(r, 1)], sem.at[sl])

    def wait_rows(sl):
        pltpu.make_async_copy(h_hbm.at[pl.ds(0, MOE_TILE)], bufs[sl], sem.at[sl]).wait()

    @pl.when(i == 0)
    def _():
        for ahead in range(GATHER_AHEAD):
            def body(r, carry, ahead=ahead):
                row_copy(jnp.minimum(ahead, last_block), r, ahead).start()
                return carry
            lax.fori_loop(0, MOE_TILE, body, 0, unroll=8)

    @pl.when((i == 0) | (be_ref[i] != be_ref[jnp.maximum(i - 1, 0)]))
    def _():
        wg_b[...] = wg_ref[0].astype(BF16)
        wu_b[...] = wu_ref[0].astype(BF16)
        wd_b[...] = wd_ref[0].astype(BF16)

    def step(sl):
        wait_rows(sl)
        nxt = jnp.minimum(i + GATHER_AHEAD, last_block)
        nxt_sl = (sl + GATHER_AHEAD) % ring
        for r in range(MOE_TILE):
            row_copy(nxt, r, nxt_sl).start(priority=r % 2)
        x = bufs[sl][...].astype(BF16)
        gate = _dot(x, wg_b[...])
        up = _dot(x, wu_b[...])
        mid = (gate * jax.nn.sigmoid(gate) * up).astype(BF16)
        o_ref[...] = _dot(mid, wd_b[...])

        @pl.when(i == n_active - 1)
        def _():
            for ahead in range(1, GATHER_AHEAD + 1):
                wait_rows((sl + ahead) % ring)

    for sl in range(ring):
        pl.when((i % ring == sl) & (i < n_active))(functools.partial(step, sl))

    @pl.when(i >= n_active)
    def _():
        o_ref[...] = jnp.zeros(o_ref.shape, o_ref.dtype)


def _experts(block_expert, n_active, buf_tok, h2, wg, wu, wd):
    cap = buf_tok.shape[0]
    D = D_MODEL
    nblk = cap // MOE_TILE
    grid_spec = pltpu.PrefetchScalarGridSpec(
        num_scalar_prefetch=3,
        grid=(nblk,),
        in_specs=[
            pl.BlockSpec(memory_space=pl.ANY),
            pl.BlockSpec((1, D, D_EXPERT), lambda i, be, na, tok: (be[i], 0, 0)),
            pl.BlockSpec((1, D, D_EXPERT), lambda i, be, na, tok: (be[i], 0, 0)),
            pl.BlockSpec((1, D_EXPERT, D), lambda i, be, na, tok: (be[i], 0, 0)),
        ],
        out_specs=pl.BlockSpec((MOE_TILE, D), lambda i, be, na, tok: (i, 0)),
        scratch_shapes=[
            *([pltpu.VMEM((MOE_TILE, D), F32)] * (GATHER_AHEAD + 1)),
            pltpu.VMEM((D, D_EXPERT), BF16),
            pltpu.VMEM((D, D_EXPERT), BF16),
            pltpu.VMEM((D_EXPERT, D), BF16),
            pltpu.SemaphoreType.DMA((GATHER_AHEAD + 1,)),
        ],
    )
    return pl.pallas_call(
        _expert_kernel,
        grid_spec=grid_spec,
        out_shape=jax.ShapeDtypeStruct((cap, D), F32),
        compiler_params=_cparams(("arbitrary",)),
        name="experts",
    )(block_expert, n_active, buf_tok, h2, wg, wu, wd)


def _final_kernel(dest_ref, x1_ref, rt_ref, mod_ref, g_ref, y_hbm, o_ref, *scratch):
    j = pl.program_id(0)
    last_tile = pl.num_programs(0) - 1
    tm = o_ref.shape[0]
    ring = GATHER_AHEAD + 1
    bufs = tuple(scratch[EXPERT_TOP_K * sl:EXPERT_TOP_K * (sl + 1)] for sl in range(ring))
    sem = scratch[-1]

    def row_copy(tile, r, k, sl):
        row = dest_ref[(tile * tm + r) * EXPERT_TOP_K + k]
        return pltpu.make_async_copy(y_hbm.at[pl.ds(row, 1)], bufs[sl][k].at[pl.ds(r, 1)], sem.at[sl])

    def wait_rows(sl):
        for k in range(EXPERT_TOP_K):
            pltpu.make_async_copy(y_hbm.at[pl.ds(0, tm)], bufs[sl][k], sem.at[sl]).wait()

    @pl.when(j == 0)
    def _():
        for ahead in range(GATHER_AHEAD):
            def body(r, carry, ahead=ahead):
                for k in range(EXPERT_TOP_K):
                    row_copy(jnp.minimum(ahead, last_tile), r, k, ahead).start()
                return carry
            lax.fori_loop(0, tm, body, 0, unroll=4)

    def step(sl):
        wait_rows(sl)
        nxt = jnp.minimum(j + GATHER_AHEAD, last_tile)
        nxt_sl = (sl + GATHER_AHEAD) % ring
        for r in range(tm):
            for k in range(EXPERT_TOP_K):
                row_copy(nxt, r, k, nxt_sl).start(priority=k)
        rt = rt_ref[...]
        lane = lax.broadcasted_iota(jnp.int32, rt.shape, 1)
        w0 = jnp.sum(jnp.where(lane == 4, rt, 0.0), axis=-1, keepdims=True)
        w1 = jnp.sum(jnp.where(lane == 5, rt, 0.0), axis=-1, keepdims=True)
        y = w0 * bufs[sl][0][...] + w1 * bufs[sl][1][...]
        o_ref[...] = x1_ref[...] + mod_ref[0, 5:6, :] * _rms(y, g_ref[...])

        @pl.when(j == last_tile)
        def _():
            for ahead in range(1, GATHER_AHEAD + 1):
                wait_rows((sl + ahead) % ring)

    for sl in range(ring):
        pl.when(j % ring == sl)(functools.partial(step, sl))


def _final(dest, x1, rt, mod, g, yb, tiles_per_batch):
    T, D = x1.shape
    tm = FINAL_TILE
    grid_spec = pltpu.PrefetchScalarGridSpec(
        num_scalar_prefetch=1,
        grid=(T // tm,),
        in_specs=[
            pl.BlockSpec((tm, D), lambda j, d: (j, 0)),
            pl.BlockSpec((tm, LANES), lambda j, d: (j, 0)),
            pl.BlockSpec((1, 6, D), lambda j, d: (j // tiles_per_batch, 0, 0)),
            pl.BlockSpec((1, D), lambda j, d: (0, 0)),
            pl.BlockSpec(memory_space=pl.ANY),
        ],
        out_specs=pl.BlockSpec((tm, D), lambda j, d: (j, 0)),
        scratch_shapes=([pltpu.VMEM((tm, D), F32)] * (EXPERT_TOP_K * (GATHER_AHEAD + 1))
                        + [pltpu.SemaphoreType.DMA((GATHER_AHEAD + 1,))]),
    )
    return pl.pallas_call(
        _final_kernel,
        grid_spec=grid_spec,
        out_shape=jax.ShapeDtypeStruct((T, D), F32),
        compiler_params=_cparams(("arbitrary",)),
        name="final",
    )(dest, x1, rt, mod, g, yb)


def _overlap_matrix():
    n = np.arange(N_CMP_PAD)[:, None]
    j = np.arange(LANES)[None, :]
    start = n * CMP_STRIDE
    ov = (start < j * SEL_LEN + SEL_LEN) & (start + CMP_LEN - 1 >= j * SEL_LEN) & (n < N_CMP_PAD - 1)
    return jnp.asarray(ov.T.astype(np.float32), dtype=BF16)


def _pad_cols(w, width=LANES):
    return jnp.pad(w, ((0, 0), (0, width - w.shape[1])))


def _dispatch_plan(rt, cnt, T):
    expert = rt[:, 0:2].astype(jnp.int32)
    rank = rt[:, 2:4].astype(jnp.int32)
    weight = rt[:, 4:6]
    counts = cnt[0, N_EXPERT_GROUPS:N_EXPERT_GROUPS + N_EXPERTS].astype(jnp.int32)
    padded = (counts + MOE_TILE - 1) // MOE_TILE * MOE_TILE
    pad_end = jnp.cumsum(padded)
    pad_start = pad_end - padded
    onehot = expert[:, :, None] == jnp.arange(N_EXPERTS)[None, None, :]
    dest = jnp.sum(jnp.where(onehot, pad_start[None, None, :], 0), axis=-1) + rank
    A = T * EXPERT_TOP_K
    cap = -(-(A + N_EXPERTS * (MOE_TILE - 1)) // MOE_TILE) * MOE_TILE
    nblk = cap // MOE_TILE
    n_active = (pad_end[-1] // MOE_TILE).astype(jnp.int32)
    blk = jnp.arange(nblk) * MOE_TILE
    block_expert = jnp.minimum(jnp.sum(pad_end[None, :] <= blk[:, None], axis=1), N_EXPERTS - 1)
    last = jnp.max(jnp.where(jnp.arange(nblk) < n_active, block_expert, 0))
    block_expert = jnp.where(jnp.arange(nblk) < n_active, block_expert, last).astype(jnp.int32)
    tok = jnp.arange(A, dtype=jnp.int32) // EXPERT_TOP_K
    buf_tok = jnp.zeros((cap,), jnp.int32).at[dest.reshape(A)].set(tok)
    return weight, dest, buf_tok, block_expert, n_active.reshape(1)


def kernel(x, c, w_ada, b_ada, g_pre_mix, g_post_mix, g_pre_ffn, g_post_ffn, w_in, b_forget,
           cmp_pe_k, cmp_w1_k, cmp_w2_k, cmp_pe_v, cmp_w1_v, cmp_w2_v,
           w_o_nsa, w_o_fox, w_out, w_router_group, b_router_group, w_router_expert, b_router_expert,
           w_exp_gate, w_exp_up, w_exp_down):
    B, S, D = x.shape
    T = B * S
    depth = w_ada.shape[0]
    ov = _overlap_matrix()
    tri = jnp.asarray(np.tril(np.ones((IN_TILE, IN_TILE), np.float32)), dtype=BF16)
    stri = jnp.asarray(np.tril(np.ones((MERGE_TILE, MERGE_TILE), np.float32), -1), dtype=BF16)
    row_feat = _row_features(S)
    placement = _placement()
    cmp_ext = _cmp_key_ext()
    for l in range(depth):
        mod = _adaln(c, w_ada[l], b_ada[l].reshape(1, 6 * D)).reshape(B, 6, D)
        w_qa, w_kva, w_gl, w_fox, w_f, w_mg = jnp.split(w_in[l], IN_SPLITS, axis=-1)
        w_big = jnp.concatenate([w_qa, w_kva, w_fox, w_mg], axis=1).astype(BF16)
        w_small = _pad_cols(jnp.concatenate([w_gl, w_f], axis=1)).astype(BF16)
        bf_pad = jnp.pad(b_forget[l], (F_LANE, LANES - F_LANE - FOX_HEADS)).reshape(1, LANES)
        qa, ckv, ksl, nkv, fq, fk, fv, mg, sm = _inproj(
            x, mod, g_pre_mix[l].reshape(1, D), w_big, w_small, bf_pad, tri, row_feat, placement)

        half = CMP_LEN // 2
        pe = jnp.stack([cmp_pe_k[l], cmp_pe_v[l]]).reshape(2, 2, 1, half * HEAD_DIM)
        w1 = jnp.stack([cmp_w1_k[l], cmp_w1_v[l]]).reshape(2, 2, half * HEAD_DIM, HEAD_DIM).astype(BF16)
        w2 = jnp.pad(jnp.stack([cmp_w2_k[l], cmp_w2_v[l]]), ((0, 0), (0, 0), (0, LANES - HEAD_DIM))).astype(BF16)
        kvc = _compress(ckv.reshape(B, 4, S // CMP_STRIDE, CMP_STRIDE * HEAD_DIM), pe, w1, w2, cmp_ext)
        ocg, selb, flags = _cmp_attention(qa, kvc, sm, ov)
        nq = S // SW_TILE
        per_tile = K_TILE // SEL_LEN
        tile_any = jnp.max(flags.reshape(B, NSA_KV_GROUPS, nq, SW_TILE // Q_TILE, MAX_TILES, per_tile), axis=(3, 5))
        tile_id = jnp.arange(MAX_TILES)
        diag = (jnp.arange(nq) // (K_TILE // SW_TILE))[:, None]
        active = (tile_any > 0) & (tile_id < diag)
        slot = jnp.cumsum(active, axis=-1) - 1
        hit = active[..., :, None] & (slot[..., :, None] == tile_id)
        tile_list = jnp.sum(jnp.where(hit, tile_id[:, None], 0), axis=-2).astype(jnp.int32).reshape(-1)
        tile_count = jnp.sum(active, axis=-1).astype(jnp.int32).reshape(-1)
        y_a = _selwin_attention(tile_list, tile_count, qa, ksl, nkv, selb, ocg, sm)

        y_b = _fox_attention(fq, fk, fv)

        w_r = _pad_cols(jnp.concatenate([w_router_group[l], w_router_expert[l]], axis=1))
        w_rh = w_r.astype(BF16)
        w_rl = (w_r - w_rh.astype(F32)).astype(BF16)
        b_r = _pad_cols(jnp.concatenate([b_router_group[l], b_router_expert[l]]).reshape(1, -1))
        x1, h2, rt, cnt = _merge(y_a, y_b, mg, x, mod, g_post_mix[l].reshape(1, D), g_pre_ffn[l].reshape(1, D),
                                 w_o_nsa[l].astype(BF16), w_o_fox[l].astype(BF16), w_out[l].astype(BF16),
                                 w_rh, w_rl, b_r, stri)

        weight, dest, buf_tok, block_expert, n_active = _dispatch_plan(rt.reshape(T, LANES), cnt, T)
        yb = _experts(block_expert, n_active, buf_tok, h2.reshape(T, D), w_exp_gate[l], w_exp_up[l], w_exp_down[l])
        x = _final(dest.reshape(T * EXPERT_TOP_K), x1.reshape(T, D), rt.reshape(T, LANES), mod,
                   g_post_ffn[l].reshape(1, D), yb, S // FINAL_TILE).reshape(B, S, D)
    return x
```

```python
import functools

import ml_dtypes
import numpy as np
import jax
import jax.numpy as jnp
from jax import lax
from jax.experimental import pallas as pl
from jax.experimental.pallas import tpu as pltpu

D_MODEL = 1024
HEAD_DIM = 64
NSA_HEADS = 8
NSA_KV_GROUPS = 2
NSA_HPG = NSA_HEADS // NSA_KV_GROUPS
FOX_HEADS = 8
CMP_LEN = 32
CMP_STRIDE = 16
SEL_LEN = 64
N_SEL = 16
WINDOW = 512
N_EXPERT_GROUPS = 4
EXPERTS_PER_GROUP = 8
N_EXPERTS = N_EXPERT_GROUPS * EXPERTS_PER_GROUP
EXPERT_TOP_K = 2
D_EXPERT = D_MODEL // 2
NORM_EPS = 1e-6
NEG = -1e30
FORCE = 1e9
LOG2E = 1.4426950408889634

NSA_W = NSA_HEADS * HEAD_DIM
NSA_KV_W = NSA_KV_GROUPS * HEAD_DIM
FOX_W = FOX_HEADS * HEAD_DIM
IN_SIZES = (NSA_W, 6 * NSA_KV_W, 3 * NSA_HEADS, 3 * FOX_W, FOX_HEADS, 2 * D_MODEL)
IN_SPLITS = tuple(int(v) for v in np.cumsum(IN_SIZES)[:-1])

LANES = 128
Q_TILE = 128
K_TILE = 256
SW_TILE = 256
N_CMP_PAD = 512
MOE_TILE = 256
IN_TILE = 512
MERGE_TILE = 512
FINAL_TILE = 256
FOX_HPS = 4
CMP_SUB = 2
MAX_TILES = 32
GATHER_AHEAD = 3
VMEM_LIMIT = 56 * 1024 * 1024

F_LANE = 3 * NSA_HEADS
U_LANE = 64
ONE_LANE = 88
A_LANE = 89
B_LANE = 90
EXT = HEAD_DIM
G_FQ, G_FK, G_NQ, G_NK, N_GROUPS = 0, 8, 16, 24, 25

F32 = jnp.float32
BF16 = jnp.bfloat16


def _dot(a, b):
    return jnp.dot(a, b, preferred_element_type=F32)


def _dot_nt(a, b):
    return lax.dot_general(a, b, (((1,), (1,)), ((), ())), preferred_element_type=F32)


def _rms(x, g):
    return x * lax.rsqrt(jnp.mean(x * x, axis=-1, keepdims=True) + NORM_EPS) * g


def _cparams(sem):
    return pltpu.CompilerParams(dimension_semantics=sem, vmem_limit_bytes=VMEM_LIMIT)


def _split3(x):
    hi = x.astype(BF16).astype(F32)
    r = x - hi
    mid = r.astype(BF16).astype(F32)
    lo = (r - mid).astype(BF16).astype(F32)
    return hi, mid, lo


def _np_split3(x):
    x = np.asarray(x, np.float32)
    hi = x.astype(ml_dtypes.bfloat16).astype(np.float32)
    r = x - hi
    mid = r.astype(ml_dtypes.bfloat16).astype(np.float32)
    lo = (r - mid).astype(ml_dtypes.bfloat16).astype(np.float32)
    return hi, mid, lo


def _alibi_c():
    slopes = np.exp2(-8.0 * np.arange(1, NSA_HEADS + 1, dtype=np.float32) / NSA_HEADS).astype(np.float32)
    return slopes * np.float32(LOG2E)


def _row_features(S):
    t = np.arange(S, dtype=np.float32)
    c = _alibi_c()
    rs = np.zeros((S, LANES), np.float32)
    for h in range(NSA_HEADS):
        for j, term in enumerate(_np_split3(c[h] * t)):
            rs[:, U_LANE + 8 * j + h] = -term
    rs[:, ONE_LANE] = 1.0
    rs[:, A_LANE] = np.floor(t / LANES)
    rs[:, B_LANE] = t % LANES
    return jnp.asarray(rs, dtype=BF16)


def _placement():
    c = _alibi_c()
    p = np.zeros((LANES, N_GROUPS * LANES), np.float32)
    for h in range(FOX_HEADS):
        q0 = (G_FQ + h) * LANES + EXT
        k0 = (G_FK + h) * LANES + EXT
        for j in range(3):
            p[ONE_LANE, q0 + j] = -1.0
            p[F_LANE + 8 * j + h, q0 + 3 + j] = 1.0
            p[F_LANE + 8 * j + h, k0 + j] = 1.0
            p[ONE_LANE, k0 + 3 + j] = 1.0
    for h in range(NSA_HEADS):
        q0 = (G_NQ + h) * LANES + EXT
        c128 = _np_split3(c[h] * np.float32(LANES))
        c1 = _np_split3(c[h])
        for j in range(3):
            p[U_LANE + 8 * j + h, q0 + j] = 1.0
            p[ONE_LANE, q0 + 3 + j] = c128[j]
            p[ONE_LANE, q0 + 6 + j] = c1[j]
    k0 = G_NK * LANES + EXT
    for j in range(3):
        p[ONE_LANE, k0 + j] = 1.0
        p[A_LANE, k0 + 3 + j] = 1.0
        p[B_LANE, k0 + 6 + j] = 1.0
    return jnp.asarray(p, dtype=BF16)


def _cmp_key_ext():
    pos = np.arange(N_CMP_PAD, dtype=np.float32) * CMP_STRIDE + (CMP_LEN - 1)
    e = np.zeros((2, N_CMP_PAD, LANES), np.float32)
    for j in range(3):
        e[0, :, EXT + j] = 1.0
        e[0, :, EXT + 3 + j] = np.floor(pos / LANES)
        e[0, :, EXT + 6 + j] = pos % LANES
    return jnp.asarray(e, dtype=BF16)


def _adaln_kernel(c_ref, w_ref, b_ref, o_ref):
    c = c_ref[...]
    act = (c * jax.nn.sigmoid(c)).astype(BF16)
    o_ref[...] = _dot(act, w_ref[...].astype(BF16)) + b_ref[...]


def _adaln(c, w, b):
    B, D = c.shape
    n = w.shape[1]
    return pl.pallas_call(
        _adaln_kernel,
        grid=(n // D,),
        in_specs=[
            pl.BlockSpec((B, D), lambda j: (0, 0)),
            pl.BlockSpec((D, D), lambda j: (0, j)),
            pl.BlockSpec((1, D), lambda j: (0, j)),
        ],
        out_specs=pl.BlockSpec((B, D), lambda j: (0, j)),
        out_shape=jax.ShapeDtypeStruct((B, n), F32),
        compiler_params=_cparams(("parallel",)),
        name="adaln",
    )(c, w, b)


def _inproj_kernel(x_ref, mod_ref, g_ref, wb_ref, ws_ref, bf_ref, tri_ref, rs_ref, p_ref,
                   qa_ref, ckv_ref, ksl_ref, nkv_ref, fq_ref, fk_ref, fv_ref, mg_ref, sm_ref, carry_sc):
    i = pl.program_id(1)
    tm = x_ref.shape[1]
    x = x_ref[0]
    h = _rms(x, g_ref[...]) * (1.0 + mod_ref[0, 1:2, :]) + mod_ref[0, 0:1, :]
    hb = h.astype(BF16)
    lane = lax.broadcasted_iota(jnp.int32, (tm, LANES), 1)
    lower = lane < HEAD_DIM
    ones_col = (lane == EXT).astype(F32)

    z = _dot(hb, ws_ref[...]) + bf_ref[...]
    logsig = jnp.minimum(z, 0.0) - jnp.log1p(jnp.exp(-jnp.abs(z)))
    sm_ref[0] = jnp.where(lane < F_LANE, jax.nn.sigmoid(z), logsig)

    @pl.when(i == 0)
    def _():
        carry_sc[...] = jnp.zeros(carry_sc.shape, F32)

    is_f = (lane >= F_LANE) & (lane < F_LANE + FOX_HEADS)
    l_hi, l_mid, l_lo = _split3(jnp.where(is_f, logsig, 0.0))
    tri = tri_ref[...]
    cum = carry_sc[...] + _dot(tri, l_hi.astype(BF16)) + _dot(tri, l_mid.astype(BF16)) + _dot(tri, l_lo.astype(BF16))
    carry_sc[...] = cum[tm - 1:tm, :]
    f_hi, f_mid, f_lo = _split3(cum * LOG2E)
    feat = (f_hi + pltpu.roll(f_mid, 8, 1) + pltpu.roll(f_lo, 16, 1) + rs_ref[...].astype(F32)).astype(BF16)

    ext_pairs = {}

    def ext(group):
        first = group - group % 2
        if first not in ext_pairs:
            width = min(2, N_GROUPS - first) * LANES
            ext_pairs[first] = _dot(feat, p_ref[:, first * LANES:first * LANES + width])
        off = (group - first) * LANES
        return ext_pairs[first][:, off:off + LANES]

    def piece(acc, idx, extra):
        pair = acc[:, (idx // 2) * LANES:(idx // 2 + 1) * LANES]
        if idx % 2:
            pair = pltpu.roll(pair, HEAD_DIM, 1)
        return jnp.where(lower, pair, extra).astype(BF16)

    qscale = (HEAD_DIM ** -0.5) * LOG2E
    acc = _dot(hb, wb_ref[:, 0:NSA_W]) * qscale
    for hd in range(NSA_HEADS):
        qa_ref[0, hd] = piece(acc, hd, ext(G_NQ + hd))
    off = NSA_W
    acc = _dot(hb, wb_ref[:, off:off + 6 * NSA_KV_W])
    for pc in range(4):
        ckv_ref[0, pc] = acc[:, pc * HEAD_DIM:(pc + 1) * HEAD_DIM].astype(BF16)
    ext_k = ext(G_NK)
    t = i * tm + lax.broadcasted_iota(jnp.int32, (tm, LANES), 0)
    block_onehot = (lane == t // SEL_LEN).astype(BF16)
    for g in range(NSA_KV_GROUPS):
        ksl_ref[0, g, :, 0:LANES] = piece(acc, 4 + g, ext_k)
        ksl_ref[0, g, :, LANES:2 * LANES] = block_onehot
        nkv_ref[0, g] = piece(acc, 6 + g, ones_col)
        nkv_ref[0, 2 + g] = piece(acc, 8 + g, ext_k)
        nkv_ref[0, 4 + g] = piece(acc, 10 + g, ones_col)
    off += 6 * NSA_KV_W
    acc = _dot(hb, wb_ref[:, off:off + FOX_W]) * qscale
    for hd in range(FOX_HEADS):
        fq_ref[0, hd] = piece(acc, hd, ext(G_FQ + hd))
    off += FOX_W
    acc = _dot(hb, wb_ref[:, off:off + FOX_W])
    for hd in range(FOX_HEADS):
        fk_ref[0, hd] = piece(acc, hd, ext(G_FK + hd))
    off += FOX_W
    acc = _dot(hb, wb_ref[:, off:off + FOX_W])
    for hd in range(FOX_HEADS):
        fv_ref[0, hd] = piece(acc, hd, ones_col)
    off += FOX_W
    for c in range(4):
        acc = _dot(hb, wb_ref[:, off + c * 512: off + (c + 1) * 512])
        mg_ref[0, :, c * 512:(c + 1) * 512] = jax.nn.sigmoid(acc).astype(BF16)


def _inproj(x, mod, g, wb, ws, bfp, tri, rs, pm):
    B, S, D = x.shape
    tm = IN_TILE
    nb = wb.shape[1]
    const2 = lambda b, i: (0, 0)
    heads = lambda n: pl.BlockSpec((1, n, tm, LANES), lambda b, i: (b, 0, i, 0))
    hshape = lambda n: jax.ShapeDtypeStruct((B, n, S, LANES), BF16)
    return pl.pallas_call(
        _inproj_kernel,
        grid=(B, S // tm),
        in_specs=[
            pl.BlockSpec((1, tm, D), lambda b, i: (b, i, 0)),
            pl.BlockSpec((1, 6, D), lambda b, i: (b, 0, 0)),
            pl.BlockSpec((1, D), const2),
            pl.BlockSpec((D, nb), const2),
            pl.BlockSpec((D, LANES), const2),
            pl.BlockSpec((1, LANES), const2),
            pl.BlockSpec((tm, tm), const2),
            pl.BlockSpec((tm, LANES), lambda b, i: (i, 0)),
            pl.BlockSpec((LANES, N_GROUPS * LANES), const2),
        ],
        out_specs=[
            heads(NSA_HEADS),
            pl.BlockSpec((1, 4, tm, HEAD_DIM), lambda b, i: (b, 0, i, 0)),
            pl.BlockSpec((1, NSA_KV_GROUPS, tm, 2 * LANES), lambda b, i: (b, 0, i, 0)),
            heads(6), heads(FOX_HEADS), heads(FOX_HEADS), heads(FOX_HEADS),
            pl.BlockSpec((1, tm, 2 * D), lambda b, i: (b, i, 0)),
            pl.BlockSpec((1, tm, LANES), lambda b, i: (b, i, 0)),
        ],
        out_shape=[
            hshape(NSA_HEADS),
            jax.ShapeDtypeStruct((B, 4, S, HEAD_DIM), BF16),
            jax.ShapeDtypeStruct((B, NSA_KV_GROUPS, S, 2 * LANES), BF16),
            hshape(6), hshape(FOX_HEADS), hshape(FOX_HEADS), hshape(FOX_HEADS),
            jax.ShapeDtypeStruct((B, S, 2 * D), BF16),
            jax.ShapeDtypeStruct((B, S, LANES), F32),
        ],
        scratch_shapes=[pltpu.VMEM((1, LANES), F32)],
        compiler_params=_cparams(("parallel", "arbitrary")),
        name="inproj",
    )(x, mod, g, wb, ws, bfp, tri, rs, pm)


def _compress_kernel(x_ref, pe_ref, w1_ref, w2_ref, e_ref, o_ref):
    x = x_ref[0, 0].astype(F32)
    x_lo = (x + pe_ref[0, 0]).astype(BF16)
    x_hi = (x + pe_ref[0, 1]).astype(BF16)
    y_lo = _dot(x_lo, w1_ref[0, 0])
    y_hi = _dot(x_hi, w1_ref[0, 1])
    n = y_hi.shape[0]
    hid = y_lo + pltpu.roll(y_hi, n - 1, 0)
    hid = jax.nn.gelu(hid)
    o_ref[0, 0] = (_dot(hid.astype(BF16), w2_ref[0]) + e_ref[0].astype(F32)).astype(BF16)


def _compress(kv_rows, pe, w1, w2, e):
    B = kv_rows.shape[0]
    R, C = kv_rows.shape[2], kv_rows.shape[3]
    return pl.pallas_call(
        _compress_kernel,
        grid=(B, 4),
        in_specs=[
            pl.BlockSpec((1, 1, R, C), lambda b, p: (b, p, 0, 0)),
            pl.BlockSpec((1, 2, 1, C), lambda b, p: (p // 2, 0, 0, 0)),
            pl.BlockSpec((1, 2, C, HEAD_DIM), lambda b, p: (p // 2, 0, 0, 0)),
            pl.BlockSpec((1, HEAD_DIM, LANES), lambda b, p: (p // 2, 0, 0)),
            pl.BlockSpec((1, R, LANES), lambda b, p: (p // 2, 0, 0)),
        ],
        out_specs=pl.BlockSpec((1, 1, R, LANES), lambda b, p: (b, p, 0, 0)),
        out_shape=jax.ShapeDtypeStruct((B, 4, R, LANES), BF16),
        compiler_params=_cparams(("parallel", "parallel")),
        name="compress",
    )(kv_rows, pe, w1, w2, e)


def _gate_rows(sm, g, branch):
    col = lax.broadcasted_iota(jnp.int32, sm.shape, 1)
    parts = []
    for hl in range(NSA_HPG):
        want = 3 * (NSA_HPG * g + hl) + branch
        parts.append(jnp.sum(jnp.where(col == want, sm, 0.0), axis=-1, keepdims=True))
    return jnp.concatenate(parts, axis=0)


def _head_tile(y):
    n = y.shape[0] // NSA_HPG
    lane = lax.broadcasted_iota(jnp.int32, (n, LANES), 1)
    hs = [y[i * n:(i + 1) * n] for i in range(NSA_HPG)]
    pairs = [jnp.where(lane < HEAD_DIM, hs[2 * i], pltpu.roll(hs[2 * i + 1], HEAD_DIM, 1)) for i in range(2)]
    return jnp.concatenate(pairs, axis=1)


def _cmp_kernel(q_ref, kc_ref, vc_ref, sm_ref, ovt_ref, oc_ref, selb_ref, flag_ref, imp_sc):
    g = pl.program_id(1)
    step_q0 = pl.program_id(2) * CMP_SUB * Q_TILE
    last_visible = (step_q0 + CMP_SUB * Q_TILE - CMP_LEN) // CMP_STRIDE
    chunks = last_visible // LANES + 1

    def attend(width):
        for sub in range(CMP_SUB):
            rows = pl.ds(sub * Q_TILE, Q_TILE)
            q = q_ref[0, :, rows, :].reshape(NSA_HPG * Q_TILE, LANES)
            oc, imp = _cmp_attend(q, kc_ref[0, 0, 0:width, :], vc_ref[0, 0, 0:width, :], sm_ref[0, rows, :],
                                  ovt_ref[:, 0:width], g, step_q0 + sub * Q_TILE)
            oc_ref[0, rows, :] = oc
            imp_sc[sub] = imp

    for v in range(1, N_CMP_PAD // LANES + 1):
        pl.when(chunks == v)(functools.partial(attend, v * LANES))

    for sub in range(CMP_SUB):
        selb, flag = _select_blocks(imp_sc[sub], step_q0 + sub * Q_TILE)
        selb_ref[0, 0, pl.ds(sub * Q_TILE, Q_TILE), :] = selb
        flag_ref[0, 0, sub] = flag


def _cmp_attend(q, kc, vc, sm, ovt, g, q0):
    width = kc.shape[0]
    s = _dot_nt(q, kc)
    r = lax.broadcasted_iota(jnp.int32, (NSA_HPG * Q_TILE, 1), 0) % Q_TILE
    n = lax.broadcasted_iota(jnp.int32, (1, width), 1)
    dc = (q0 + r) - (n * CMP_STRIDE + (CMP_LEN - 1))
    mask = (dc >= 0) & (n < N_CMP_PAD - 1)
    l = jnp.where(mask, s, NEG)
    m = jnp.max(l, axis=-1, keepdims=True)
    e = jnp.where(mask, jnp.exp2(l - m), 0.0)
    pc = e / jnp.maximum(jnp.sum(e, axis=-1, keepdims=True), 1e-30)
    oc = _dot(pc.astype(BF16), vc)
    oc = _head_tile(oc * _gate_rows(sm, g, 0))
    ps = pc[0:Q_TILE]
    for i in range(1, NSA_HPG):
        ps = ps + pc[i * Q_TILE:(i + 1) * Q_TILE]
    ps_hi = ps.astype(BF16)
    ps_lo = (ps - ps_hi.astype(F32)).astype(BF16)
    return oc, _dot_nt(ovt, ps_hi) + _dot_nt(ovt, ps_lo)


def _select_blocks(imp, q0):
    j = lax.broadcasted_iota(jnp.int32, imp.shape, 0)
    jf = j.astype(F32)
    t = q0 + lax.broadcasted_iota(jnp.int32, (1, Q_TILE), 1)
    cur = t // SEL_LEN
    forced = (j == 0) | (j == cur) | (j == cur - 1)
    v = jnp.where(j > cur, -FORCE, jnp.where(forced, FORCE, imp))
    sel = jnp.zeros(imp.shape, jnp.bool_)
    for _ in range(N_SEL):
        mx = jnp.max(v, axis=0, keepdims=True)
        idx = jnp.min(jnp.where(v == mx, jf, float(LANES)), axis=0, keepdims=True)
        pick = jf == idx
        sel = sel | pick
        v = jnp.where(pick, -3e38, v)
    live_t = jnp.where(sel & (j <= cur), 1.0, 0.0).astype(BF16)
    eye = (lax.broadcasted_iota(jnp.int32, imp.shape, 0) == lax.broadcasted_iota(jnp.int32, imp.shape, 1))
    live = _dot_nt(eye.astype(BF16), live_t)
    selb = jnp.where(live > 0.5, 0.0, NEG).astype(BF16)
    return selb, jnp.max(live, axis=0, keepdims=True).astype(jnp.int32)


def _cmp_attention(qa, kvc, sm, ov):
    B, H, S, _ = qa.shape
    G = NSA_KV_GROUPS
    nq = S // Q_TILE
    qt = CMP_SUB * Q_TILE
    return pl.pallas_call(
        _cmp_kernel,
        grid=(B, G, nq // CMP_SUB),
        in_specs=[
            pl.BlockSpec((1, NSA_HPG, qt, LANES), lambda b, g, i: (b, g, i, 0)),
            pl.BlockSpec((1, 1, N_CMP_PAD, LANES), lambda b, g, i: (b, g, 0, 0)),
            pl.BlockSpec((1, 1, N_CMP_PAD, LANES), lambda b, g, i: (b, 2 + g, 0, 0)),
            pl.BlockSpec((1, qt, LANES), lambda b, g, i: (b, i, 0)),
            pl.BlockSpec((LANES, N_CMP_PAD), lambda b, g, i: (0, 0)),
        ],
        out_specs=[
            pl.BlockSpec((1, qt, NSA_HPG * HEAD_DIM), lambda b, g, i: (b, i, g)),
            pl.BlockSpec((1, 1, qt, LANES), lambda b, g, i: (b, g, i, 0)),
            pl.BlockSpec((1, 1, CMP_SUB, 1, LANES), lambda b, g, i: (b, g, i, 0, 0)),
        ],
        out_shape=[
            jax.ShapeDtypeStruct((B, S, NSA_W), F32),
            jax.ShapeDtypeStruct((B, G, S, LANES), BF16),
            jax.ShapeDtypeStruct((B, G, nq, 1, LANES), jnp.int32),
        ],
        scratch_shapes=[pltpu.VMEM((CMP_SUB, LANES, Q_TILE), F32)],
        compiler_params=_cparams(("parallel", "parallel", "parallel")),
        name="cmp_attention",
    )(qa, kvc, kvc, sm, ov)


def _online_update(s, v, m_ref, acc_ref):
    m_old = m_ref[...]
    m_new = jnp.maximum(m_old, jnp.max(s, axis=-1, keepdims=True))
    chunks = [s[:, c * LANES:(c + 1) * LANES] - m_new for c in range(s.shape[1] // LANES)]
    p = jnp.exp2(jnp.concatenate(chunks, axis=1))
    acc_ref[...] = jnp.exp2(m_old - m_new) * acc_ref[...] + _dot(p.astype(BF16), v)
    m_ref[...] = m_new


def _normalized(acc):
    return acc / jnp.maximum(acc[:, EXT:EXT + 1], 1e-30)


def _attend_once(s, v):
    m = jnp.broadcast_to(jnp.max(s, axis=-1, keepdims=True), (s.shape[0], LANES))
    chunks = [s[:, c * LANES:(c + 1) * LANES] - m for c in range(s.shape[1] // LANES)]
    p = jnp.exp2(jnp.concatenate(chunks, axis=1))
    return _normalized(_dot(p.astype(BF16), v))


def _selwin_kernel(list_ref, cnt_ref, q_ref, ks_ref, vs_ref, kw_ref, vw_ref, selb_ref, oc_ref, sm_ref,
                   o_ref, m_a, acc_a, m_b, acc_b):
    b = pl.program_id(0)
    g = pl.program_id(1)
    qb = pl.program_id(2)
    nq = pl.num_programs(2)
    rows = NSA_HPG * SW_TILE
    q4 = q_ref[0].reshape(rows, LANES)
    q_aug = jnp.concatenate([q4, jnp.concatenate([selb_ref[0, 0]] * NSA_HPG, axis=0)], axis=1)
    r = lax.broadcasted_iota(jnp.int32, (rows, 1), 0) % SW_TILE
    c = lax.broadcasted_iota(jnp.int32, (1, K_TILE), 1)
    rel = r - c
    diag = qb // (K_TILE // SW_TILE)

    def sel_tile(kt, m_ref, acc_ref, causal=False, bias=None):
        start = pl.multiple_of(kt * K_TILE, K_TILE)
        s = _dot_nt(q_aug, ks_ref[0, 0, pl.ds(start, K_TILE), :])
        if bias is not None:
            s = s + bias
        if causal:
            s = jnp.where(rel + (qb * SW_TILE - kt * K_TILE) >= 0, s, NEG)
        _online_update(s, vs_ref[0, 0, pl.ds(start, K_TILE), :], m_ref, acc_ref)

    for m_ref, acc_ref in ((m_a, acc_a), (m_b, acc_b)):
        m_ref[...] = jnp.full(m_ref.shape, NEG, F32)
        acc_ref[...] = jnp.zeros(acc_ref.shape, F32)
    step = (b * NSA_KV_GROUPS + g) * nq + qb
    count = cnt_ref[step]
    base = step * MAX_TILES

    def body(p, carry):
        second = 2 * p + 1
        sel_tile(list_ref[base + 2 * p], m_a, acc_a)
        sel_tile(list_ref[base + jnp.minimum(second, MAX_TILES - 1)], m_b, acc_b,
                 bias=jnp.where(second < count, 0.0, NEG))
        return carry

    lax.fori_loop(0, (count + 1) // 2, body, 0)
    m_new = jnp.maximum(m_a[...], m_b[...])
    acc_a[...] = jnp.exp2(m_a[...] - m_new) * acc_a[...] + jnp.exp2(m_b[...] - m_new) * acc_b[...]
    m_a[...] = m_new
    sel_tile(diag, m_a, acc_a, causal=True)
    o_sel = _normalized(acc_a[...])

    span = WINDOW + K_TILE
    wstart = pl.multiple_of(jnp.maximum(diag - WINDOW // K_TILE, 0) * K_TILE, K_TILE)
    def window(aligned):
        s = _dot_nt(q4, kw_ref[0, 0, pl.ds(wstart, span), :])
        if aligned:
            s = jnp.concatenate([jnp.where(rel < 0, s[:, 0:K_TILE], NEG), s[:, K_TILE:2 * K_TILE],
                                 jnp.where(rel >= 0, s[:, 2 * K_TILE:], NEG)], axis=1)
        else:
            dist = (qb * SW_TILE + r) - (wstart + lax.broadcasted_iota(jnp.int32, (1, span), 1))
            s = jnp.where((dist >= 0) & (dist < WINDOW), s, NEG)
        acc_b[...] = _attend_once(s, vw_ref[0, 0, pl.ds(wstart, span), :])

    pl.when(diag >= WINDOW // K_TILE)(functools.partial(window, True))
    pl.when(diag < WINDOW // K_TILE)(functools.partial(window, False))
    o_win = acc_b[...]

    sm = sm_ref[0]
    y = _gate_rows(sm, g, 1) * o_sel + _gate_rows(sm, g, 2) * o_win
    o_ref[0] = (oc_ref[0] + _head_tile(y)).astype(BF16)


def _selwin_attention(tile_list, tile_count, qa, ksl, nkv, selb, ocg, sm):
    B, H, S, _ = qa.shape
    G = NSA_KV_GROUPS
    nq = S // SW_TILE
    rows = NSA_HPG * SW_TILE
    kv_spec = lambda piece: pl.BlockSpec((1, 1, S, LANES), lambda b, g, i, tl, tc: (b, piece + g, 0, 0))
    out_tile = pl.BlockSpec((1, SW_TILE, NSA_HPG * HEAD_DIM), lambda b, g, i, tl, tc: (b, i, g))
    grid_spec = pltpu.PrefetchScalarGridSpec(
        num_scalar_prefetch=2,
        grid=(B, G, nq),
        in_specs=[
            pl.BlockSpec((1, NSA_HPG, SW_TILE, LANES), lambda b, g, i, tl, tc: (b, g, i, 0)),
            pl.BlockSpec((1, 1, S, 2 * LANES), lambda b, g, i, tl, tc: (b, g, 0, 0)),
            kv_spec(0), kv_spec(2), kv_spec(4),
            pl.BlockSpec((1, 1, SW_TILE, LANES), lambda b, g, i, tl, tc: (b, g, i, 0)),
            out_tile,
            pl.BlockSpec((1, SW_TILE, LANES), lambda b, g, i, tl, tc: (b, i, 0)),
        ],
        out_specs=out_tile,
        scratch_shapes=[pltpu.VMEM((rows, LANES), F32)] * 4,
    )
    return pl.pallas_call(
        _selwin_kernel,
        grid_spec=grid_spec,
        out_shape=jax.ShapeDtypeStruct((B, S, NSA_W), BF16),
        compiler_params=_cparams(("parallel", "parallel", "arbitrary")),
        name="selwin_attention",
    )(tile_list, tile_count, qa, ksl, nkv, nkv, nkv, selb, ocg, sm)


def _fox_kernel(q_ref, k_ref, v_ref, o_ref, m_sc, acc_sc, *, tq):
    qi = pl.program_id(2)
    m_sc[...] = jnp.full(m_sc.shape, NEG, F32)
    acc_sc[...] = jnp.zeros(acc_sc.shape, F32)

    def tile(kt, width, causal):
        start = pl.multiple_of(kt * tq, tq)
        for hh in range(FOX_HPS):
            s = _dot_nt(q_ref[0, hh], k_ref[0, hh, pl.ds(start, width), :])
            if causal:
                r = lax.broadcasted_iota(jnp.int32, s.shape, 0)
                c = lax.broadcasted_iota(jnp.int32, s.shape, 1)
                s = jnp.where(r >= c, s, NEG)
            _online_update(s, v_ref[0, hh, pl.ds(start, width), :], m_sc.at[hh], acc_sc.at[hh])

    def body(kp, carry):
        tile(2 * kp, 2 * tq, False)
        return carry

    lax.fori_loop(0, qi // 2, body, 0)

    @pl.when(qi % 2 == 1)
    def _():
        tile(qi - 1, tq, False)

    tile(qi, tq, True)
    lane = lax.broadcasted_iota(jnp.int32, (tq, LANES), 1)
    o = [_normalized(acc_sc[hh]) for hh in range(FOX_HPS)]
    for pr in range(FOX_HPS // 2):
        o_ref[0, :, pr * LANES:(pr + 1) * LANES] = jnp.where(
            lane < HEAD_DIM, o[2 * pr], pltpu.roll(o[2 * pr + 1], HEAD_DIM, 1)).astype(BF16)


def _fox_attention(fq, fk, fv, tq=512):
    B, H, S, _ = fq.shape
    hps = FOX_HPS
    return pl.pallas_call(
        functools.partial(_fox_kernel, tq=tq),
        grid=(B, H // hps, S // tq),
        in_specs=[
            pl.BlockSpec((1, hps, tq, LANES), lambda b, h, i: (b, h, i, 0)),
            pl.BlockSpec((1, hps, S, LANES), lambda b, h, i: (b, h, 0, 0)),
            pl.BlockSpec((1, hps, S, LANES), lambda b, h, i: (b, h, 0, 0)),
        ],
        out_specs=pl.BlockSpec((1, tq, hps * HEAD_DIM), lambda b, h, i: (b, i, h)),
        out_shape=jax.ShapeDtypeStruct((B, S, FOX_W), BF16),
        scratch_shapes=[
            pltpu.VMEM((hps, tq, LANES), F32),
            pltpu.VMEM((hps, tq, LANES), F32),
        ],
        compiler_params=_cparams(("parallel", "parallel", "arbitrary")),
        name="fox_attention",
    )(fq, fk, fv)


def _merge_kernel(ya_ref, yb_ref, mg_ref, x_ref, mod_ref, gpost_ref, gpre_ref,
                  wa_ref, wb_ref, wo_ref, wrh_ref, wrl_ref, br_ref, stri_ref,
                  x1_ref, h2_ref, rt_ref, cnt_ref):
    D = D_MODEL

    @pl.when((pl.program_id(0) == 0) & (pl.program_id(1) == 0))
    def _():
        cnt_ref[...] = jnp.zeros(cnt_ref.shape, F32)

    a = _dot(ya_ref[0], wa_ref[...])
    bq = _dot(yb_ref[0], wb_ref[...])
    mg = mg_ref[0]
    u = mg[:, :D].astype(F32) * a + mg[:, D:].astype(F32) * bq
    mixed = _dot(u.astype(BF16), wo_ref[...])
    x1 = x_ref[0] + mod_ref[0, 2:3, :] * _rms(mixed, gpost_ref[...])
    x1_ref[0] = x1
    h2 = _rms(x1, gpre_ref[...]) * (1.0 + mod_ref[0, 4:5, :]) + mod_ref[0, 3:4, :]
    hi = h2.astype(BF16)
    lo = (h2 - hi.astype(F32)).astype(BF16)
    h2_ref[0] = h2
    lg = _dot(hi, wrh_ref[...]) + _dot(lo, wrh_ref[...]) + _dot(hi, wrl_ref[...]) + br_ref[...]

    lane = lax.broadcasted_iota(jnp.int32, lg.shape, 1)
    lanef = lane.astype(F32)
    no_lane = float(LANES)
    is_g = lane < N_EXPERT_GROUPS
    gl = jnp.where(is_g, lg, NEG)
    gmax = jnp.max(gl, axis=-1, keepdims=True)
    pg_top = 1.0 / jnp.sum(jnp.where(is_g, jnp.exp(gl - gmax), 0.0), axis=-1, keepdims=True)
    g_idx = jnp.min(jnp.where(is_g & (gl == gmax), lanef, no_lane), axis=-1, keepdims=True)
    in_grp = ((lane >= N_EXPERT_GROUPS) & (lane < N_EXPERT_GROUPS + N_EXPERTS)
              & (((lane - N_EXPERT_GROUPS) // EXPERTS_PER_GROUP).astype(F32) == g_idx))
    le = jnp.where(in_grp, lg, NEG)
    m1 = jnp.max(le, axis=-1, keepdims=True)
    i1 = jnp.min(jnp.where(in_grp & (le == m1), lanef, no_lane), axis=-1, keepdims=True)
    rest = in_grp & (lanef != i1)
    le2 = jnp.where(rest, lg, NEG)
    m2 = jnp.max(le2, axis=-1, keepdims=True)
    i2 = jnp.min(jnp.where(rest & (le2 == m2), lanef, no_lane), axis=-1, keepdims=True)
    e21 = jnp.exp(m2 - m1)
    w1 = pg_top / (1.0 + e21)
    w2 = w1 * e21
    pick1 = lanef == i1
    pick2 = lanef == i2
    onehot = jnp.where(pick1 | pick2, 1.0, 0.0)
    before = cnt_ref[...] + _dot(stri_ref[...], onehot.astype(BF16))
    rank1 = jnp.sum(jnp.where(pick1, before, 0.0), axis=-1, keepdims=True)
    rank2 = jnp.sum(jnp.where(pick2, before, 0.0), axis=-1, keepdims=True)
    cnt_ref[...] = cnt_ref[...] + jnp.sum(onehot, axis=0, keepdims=True)
    fields = [i1 - N_EXPERT_GROUPS, i2 - N_EXPERT_GROUPS, rank1, rank2, w1, w2]
    rt = jnp.zeros(lg.shape, F32)
    for k, f in enumerate(fields):
        rt = jnp.where(lane == k, f, rt)
    rt_ref[0] = rt


def _merge(ya, yb, mg, x, mod, gpost, gpre, wa, wb, wo, wrh, wrl, br, stri):
    B, S, D = x.shape
    tm = MERGE_TILE
    c2 = lambda b, i: (0, 0)
    row = lambda w: pl.BlockSpec((1, tm, w), lambda b, i: (b, i, 0))
    return pl.pallas_call(
        _merge_kernel,
        grid=(B, S // tm),
        in_specs=[
            row(NSA_W), row(FOX_W), row(2 * D), row(D),
            pl.BlockSpec((1, 6, D), lambda b, i: (b, 0, 0)),
            pl.BlockSpec((1, D), c2), pl.BlockSpec((1, D), c2),
            pl.BlockSpec((NSA_W, D), c2), pl.BlockSpec((FOX_W, D), c2), pl.BlockSpec((D, D), c2),
            pl.BlockSpec((D, LANES), c2), pl.BlockSpec((D, LANES), c2), pl.BlockSpec((1, LANES), c2),
            pl.BlockSpec((tm, tm), c2),
        ],
        out_specs=[row(D), row(D), row(LANES), pl.BlockSpec((1, LANES), c2)],
        out_shape=[
            jax.ShapeDtypeStruct((B, S, D), F32),
            jax.ShapeDtypeStruct((B, S, D), F32),
            jax.ShapeDtypeStruct((B, S, LANES), F32),
            jax.ShapeDtypeStruct((1, LANES), F32),
        ],
        compiler_params=_cparams(("arbitrary", "arbitrary")),
        name="merge",
    )(ya, yb, mg, x, mod, gpost, gpre, wa, wb, wo, wrh, wrl, br, stri)


def _expert_kernel(be_ref, na_ref, tok_ref, h_hbm, wg_ref, wu_ref, wd_ref, o_ref, *scratch):
    i = pl.program_id(0)
    n_active = na_ref[0]
    last_block = pl.num_programs(0) - 1
    ring = GATHER_AHEAD + 1
    bufs = scratch[:ring]
    wg_b, wu_b, wd_b, sem = scratch[ring:]

    def row_copy(blk, r, sl):
        tok = tok_ref[blk * MOE_TILE + r]
        return pltpu.make_async_copy(h_hbm.at[pl.ds(tok, 1)], bufs[sl].at[pl.ds(r, 1)], sem.at[sl])

    def wait_rows(sl):
        pltpu.make_async_copy(h_hbm.at[pl.ds(0, MOE_TILE)], bufs[sl], sem.at[sl]).wait()

    @pl.when(i == 0)
    def _():
        for ahead in range(GATHER_AHEAD):
            def body(r, carry, ahead=ahead):
                row_copy(jnp.minimum(ahead, last_block), r, ahead).start()
                return carry
            lax.fori_loop(0, MOE_TILE, body, 0, unroll=8)

    @pl.when((i == 0) | (be_ref[i] != be_ref[jnp.maximum(i - 1, 0)]))
    def _():
        wg_b[...] = wg_ref[0].astype(BF16)
        wu_b[...] = wu_ref[0].astype(BF16)
        wd_b[...] = wd_ref[0].astype(BF16)

    def step(sl):
        wait_rows(sl)
        nxt = jnp.minimum(i + GATHER_AHEAD, last_block)
        nxt_sl = (sl + GATHER_AHEAD) % ring
        for r in range(MOE_TILE):
            row_copy(nxt, r, nxt_sl).start(priority=1)
        x = bufs[sl][...].astype(BF16)
        gate = _dot(x, wg_b[...])
        up = _dot(x, wu_b[...])
        mid = (gate * jax.nn.sigmoid(gate) * up).astype(BF16)
        o_ref[...] = _dot(mid, wd_b[...])

        @pl.when(i == n_active - 1)
        def _():
            for ahead in range(1, GATHER_AHEAD + 1):
                wait_rows((sl + ahead) % ring)

    for sl in range(ring):
        pl.when((i % ring == sl) & (i < n_active))(functools.partial(step, sl))

    @pl.when(i >= n_active)
    def _():
        o_ref[...] = jnp.zeros(o_ref.shape, o_ref.dtype)


def _experts(block_expert, n_active, buf_tok, h2, wg, wu, wd):
    cap = buf_tok.shape[0]
    D = D_MODEL
    nblk = cap // MOE_TILE
    grid_spec = pltpu.PrefetchScalarGridSpec(
        num_scalar_prefetch=3,
        grid=(nblk,),
        in_specs=[
            pl.BlockSpec(memory_space=pl.ANY),
            pl.BlockSpec((1, D, D_EXPERT), lambda i, be, na, tok: (be[i], 0, 0)),
            pl.BlockSpec((1, D, D_EXPERT), lambda i, be, na, tok: (be[i], 0, 0)),
            pl.BlockSpec((1, D_EXPERT, D), lambda i, be, na, tok: (be[i], 0, 0)),
        ],
        out_specs=pl.BlockSpec((MOE_TILE, D), lambda i, be, na, tok: (i, 0)),
        scratch_shapes=[
            *([pltpu.VMEM((MOE_TILE, D), F32)] * (GATHER_AHEAD + 1)),
            pltpu.VMEM((D, D_EXPERT), BF16),
            pltpu.VMEM((D, D_EXPERT), BF16),
            pltpu.VMEM((D_EXPERT, D), BF16),
            pltpu.SemaphoreType.DMA((GATHER_AHEAD + 1,)),
        ],
    )
    return pl.pallas_call(
        _expert_kernel,
        grid_spec=grid_spec,
        out_shape=jax.ShapeDtypeStruct((cap, D), F32),
        compiler_params=_cparams(("arbitrary",)),
        name="experts",
    )(block_expert, n_active, buf_tok, h2, wg, wu, wd)


def _final_kernel(dest_ref, x1_ref, rt_ref, mod_ref, g_ref, y_hbm, o_ref, *scratch):
    j = pl.program_id(0)
    last_tile = pl.num_programs(0) - 1
    tm = o_ref.shape[0]
    ring = GATHER_AHEAD + 1
    bufs = tuple(scratch[EXPERT_TOP_K * sl:EXPERT_TOP_K * (sl + 1)] for sl in range(ring))
    sem = scratch[-1]

    def row_copy(tile, r, k, sl):
        row = dest_ref[(tile * tm + r) * EXPERT_TOP_K + k]
        return pltpu.make_async_copy(y_hbm.at[pl.ds(row, 1)], bufs[sl][k].at[pl.ds(r, 1)], sem.at[sl])

    def wait_rows(sl):
        for k in range(EXPERT_TOP_K):
            pltpu.make_async_copy(y_hbm.at[pl.ds(0, tm)], bufs[sl][k], sem.at[sl]).wait()

    @pl.when(j == 0)
    def _():
        for ahead in range(GATHER_AHEAD):
            def body(r, carry, ahead=ahead):
                for k in range(EXPERT_TOP_K):
                    row_copy(jnp.minimum(ahead, last_tile), r, k, ahead).start()
                return carry
            lax.fori_loop(0, tm, body, 0, unroll=4)

    def step(sl):
        wait_rows(sl)
        nxt = jnp.minimum(j + GATHER_AHEAD, last_tile)
        nxt_sl = (sl + GATHER_AHEAD) % ring
        for r in range(tm):
            for k in range(EXPERT_TOP_K):
                row_copy(nxt, r, k, nxt_sl).start(priority=k)
        rt = rt_ref[...]
        lane = lax.broadcasted_iota(jnp.int32, rt.shape, 1)
        w0 = jnp.sum(jnp.where(lane == 4, rt, 0.0), axis=-1, keepdims=True)
        w1 = jnp.sum(jnp.where(lane == 5, rt, 0.0), axis=-1, keepdims=True)
        y = w0 * bufs[sl][0][...] + w1 * bufs[sl][1][...]
        o_ref[...] = x1_ref[...] + mod_ref[0, 5:6, :] * _rms(y, g_ref[...])

        @pl.when(j == last_tile)
        def _():
            for ahead in range(1, GATHER_AHEAD + 1):
                wait_rows((sl + ahead) % ring)

    for sl in range(ring):
        pl.when(j % ring == sl)(functools.partial(step, sl))


def _final(dest, x1, rt, mod, g, yb, tiles_per_batch):
    T, D = x1.shape
    tm = FINAL_TILE
    grid_spec = pltpu.PrefetchScalarGridSpec(
        num_scalar_prefetch=1,
        grid=(T // tm,),
        in_specs=[
            pl.BlockSpec((tm, D), lambda j, d: (j, 0)),
            pl.BlockSpec((tm, LANES), lambda j, d: (j, 0)),
            pl.BlockSpec((1, 6, D), lambda j, d: (j // tiles_per_batch, 0, 0)),
            pl.BlockSpec((1, D), lambda j, d: (0, 0)),
            pl.BlockSpec(memory_space=pl.ANY),
        ],
        out_specs=pl.BlockSpec((tm, D), lambda j, d: (j, 0)),
        scratch_shapes=([pltpu.VMEM((tm, D), F32)] * (EXPERT_TOP_K * (GATHER_AHEAD + 1))
                        + [pltpu.SemaphoreType.DMA((GATHER_AHEAD + 1,))]),
    )
    return pl.pallas_call(
        _final_kernel,
        grid_spec=grid_spec,
        out_shape=jax.ShapeDtypeStruct((T, D), F32),
        compiler_params=_cparams(("arbitrary",)),
        name="final",
    )(dest, x1, rt, mod, g, yb)


def _overlap_matrix():
    n = np.arange(N_CMP_PAD)[:, None]
    j = np.arange(LANES)[None, :]
    start = n * CMP_STRIDE
    ov = (start < j * SEL_LEN + SEL_LEN) & (start + CMP_LEN - 1 >= j * SEL_LEN) & (n < N_CMP_PAD - 1)
    return jnp.asarray(ov.T.astype(np.float32), dtype=BF16)


def _pad_cols(w, width=LANES):
    return jnp.pad(w, ((0, 0), (0, width - w.shape[1])))


def _dispatch_plan(rt, cnt, T):
    expert = rt[:, 0:2].astype(jnp.int32)
    rank = rt[:, 2:4].astype(jnp.int32)
    weight = rt[:, 4:6]
    counts = cnt[0, N_EXPERT_GROUPS:N_EXPERT_GROUPS + N_EXPERTS].astype(jnp.int32)
    padded = (counts + MOE_TILE - 1) // MOE_TILE * MOE_TILE
    pad_end = jnp.cumsum(padded)
    pad_start = pad_end - padded
    onehot = expert[:, :, None] == jnp.arange(N_EXPERTS)[None, None, :]
    dest = jnp.sum(jnp.where(onehot, pad_start[None, None, :], 0), axis=-1) + rank
    A = T * EXPERT_TOP_K
    cap = -(-(A + N_EXPERTS * (MOE_TILE - 1)) // MOE_TILE) * MOE_TILE
    nblk = cap // MOE_TILE
    n_active = (pad_end[-1] // MOE_TILE).astype(jnp.int32)
    blk = jnp.arange(nblk) * MOE_TILE
    block_expert = jnp.minimum(jnp.sum(pad_end[None, :] <= blk[:, None], axis=1), N_EXPERTS - 1)
    last = jnp.max(jnp.where(jnp.arange(nblk) < n_active, block_expert, 0))
    block_expert = jnp.where(jnp.arange(nblk) < n_active, block_expert, last).astype(jnp.int32)
    tok = jnp.arange(A, dtype=jnp.int32) // EXPERT_TOP_K
    buf_tok = jnp.zeros((cap,), jnp.int32).at[dest.reshape(A)].set(tok)
    return weight, dest, buf_tok, block_expert, n_active.reshape(1)


def kernel(x, c, w_ada, b_ada, g_pre_mix, g_post_mix, g_pre_ffn, g_post_ffn, w_in, b_forget,
           cmp_pe_k, cmp_w1_k, cmp_w2_k, cmp_pe_v, cmp_w1_v, cmp_w2_v,
           w_o_nsa, w_o_fox, w_out, w_router_group, b_router_group, w_router_expert, b_router_expert,
           w_exp_gate, w_exp_up, w_exp_down):
    B, S, D = x.shape
    T = B * S
    depth = w_ada.shape[0]
    ov = _overlap_matrix()
    tri = jnp.asarray(np.tril(np.ones((IN_TILE, IN_TILE), np.float32)), dtype=BF16)
    stri = jnp.asarray(np.tril(np.ones((MERGE_TILE, MERGE_TILE), np.float32), -1), dtype=BF16)
    row_feat = _row_features(S)
    placement = _placement()
    cmp_ext = _cmp_key_ext()
    for l in range(depth):
        mod = _adaln(c, w_ada[l], b_ada[l].reshape(1, 6 * D)).reshape(B, 6, D)
        w_qa, w_kva, w_gl, w_fox, w_f, w_mg = jnp.split(w_in[l], IN_SPLITS, axis=-1)
        w_big = jnp.concatenate([w_qa, w_kva, w_fox, w_mg], axis=1).astype(BF16)
        w_small = _pad_cols(jnp.concatenate([w_gl, w_f], axis=1)).astype(BF16)
        bf_pad = jnp.pad(b_forget[l], (F_LANE, LANES - F_LANE - FOX_HEADS)).reshape(1, LANES)
        qa, ckv, ksl, nkv, fq, fk, fv, mg, sm = _inproj(
            x, mod, g_pre_mix[l].reshape(1, D), w_big, w_small, bf_pad, tri, row_feat, placement)

        half = CMP_LEN // 2
        pe = jnp.stack([cmp_pe_k[l], cmp_pe_v[l]]).reshape(2, 2, 1, half * HEAD_DIM)
        w1 = jnp.stack([cmp_w1_k[l], cmp_w1_v[l]]).reshape(2, 2, half * HEAD_DIM, HEAD_DIM).astype(BF16)
        w2 = jnp.pad(jnp.stack([cmp_w2_k[l], cmp_w2_v[l]]), ((0, 0), (0, 0), (0, LANES - HEAD_DIM))).astype(BF16)
        kvc = _compress(ckv.reshape(B, 4, S // CMP_STRIDE, CMP_STRIDE * HEAD_DIM), pe, w1, w2, cmp_ext)
        ocg, selb, flags = _cmp_attention(qa, kvc, sm, ov)
        nq = S // SW_TILE
        per_tile = K_TILE // SEL_LEN
        tile_any = jnp.max(flags.reshape(B, NSA_KV_GROUPS, nq, SW_TILE // Q_TILE, MAX_TILES, per_tile), axis=(3, 5))
        tile_id = jnp.arange(MAX_TILES)
        diag = (jnp.arange(nq) // (K_TILE // SW_TILE))[:, None]
        active = (tile_any > 0) & (tile_id < diag)
        slot = jnp.cumsum(active, axis=-1) - 1
        hit = active[..., :, None] & (slot[..., :, None] == tile_id)
        tile_list = jnp.sum(jnp.where(hit, tile_id[:, None], 0), axis=-2).astype(jnp.int32).reshape(-1)
        tile_count = jnp.sum(active, axis=-1).astype(jnp.int32).reshape(-1)
        y_a = _selwin_attention(tile_list, tile_count, qa, ksl, nkv, selb, ocg, sm)

        y_b = _fox_attention(fq, fk, fv)

        w_r = _pad_cols(jnp.concatenate([w_router_group[l], w_router_expert[l]], axis=1))
        w_rh = w_r.astype(BF16)
        w_rl = (w_r - w_rh.astype(F32)).astype(BF16)
        b_r = _pad_cols(jnp.concatenate([b_router_group[l], b_router_expert[l]]).reshape(1, -1))
        x1, h2, rt, cnt = _merge(y_a, y_b, mg, x, mod, g_post_mix[l].reshape(1, D), g_pre_ffn[l].reshape(1, D),
                                 w_o_nsa[l].astype(BF16), w_o_fox[l].astype(BF16), w_out[l].astype(BF16),
                                 w_rh, w_rl, b_r, stri)

        weight, dest, buf_tok, block_expert, n_active = _dispatch_plan(rt.reshape(T, LANES), cnt, T)
        yb = _experts(block_expert, n_active, buf_tok, h2.reshape(T, D), w_exp_gate[l], w_exp_up[l], w_exp_down[l])
        x = _final(dest.reshape(T * EXPERT_TOP_K), x1.reshape(T, D), rt.reshape(T, LANES), mod,
                   g_post_ffn[l].reshape(1, D), yb, S // FINAL_TILE).reshape(B, S, D)
    return x
```

```python
import functools

import ml_dtypes
import numpy as np
import jax
import jax.numpy as jnp
from jax import lax
from jax.experimental import pallas as pl
from jax.experimental.pallas import tpu as pltpu

D_MODEL = 1024
HEAD_DIM = 64
NSA_HEADS = 8
NSA_KV_GROUPS = 2
NSA_HPG = NSA_HEADS // NSA_KV_GROUPS
FOX_HEADS = 8
CMP_LEN = 32
CMP_STRIDE = 16
SEL_LEN = 64
N_SEL = 16
WINDOW = 512
N_EXPERT_GROUPS = 4
EXPERTS_PER_GROUP = 8
N_EXPERTS = N_EXPERT_GROUPS * EXPERTS_PER_GROUP
EXPERT_TOP_K = 2
D_EXPERT = D_MODEL // 2
NORM_EPS = 1e-6
NEG = -1e30
FORCE = 1e9
LOG2E = 1.4426950408889634

NSA_W = NSA_HEADS * HEAD_DIM
NSA_KV_W = NSA_KV_GROUPS * HEAD_DIM
FOX_W = FOX_HEADS * HEAD_DIM
IN_SIZES = (NSA_W, 6 * NSA_KV_W, 3 * NSA_HEADS, 3 * FOX_W, FOX_HEADS, 2 * D_MODEL)
IN_SPLITS = tuple(int(v) for v in np.cumsum(IN_SIZES)[:-1])

LANES = 128
Q_TILE = 128
K_TILE = 256
SW_TILE = 256
N_CMP_PAD = 512
MOE_TILE = 256
IN_TILE = 512
MERGE_TILE = 512
FINAL_TILE = 256
DISPATCH_TILE = 512
FOX_HPS = 4
CMP_SUB = 2
MAX_TILES = 32
GATHER_AHEAD = 3
VMEM_LIMIT = 56 * 1024 * 1024

F_LANE = 3 * NSA_HEADS
U_LANE = 64
ONE_LANE = 88
A_LANE = 89
B_LANE = 90
EXT = HEAD_DIM
G_FQ, G_FK, G_NQ, G_NK, N_GROUPS = 0, 8, 16, 24, 25

F32 = jnp.float32
BF16 = jnp.bfloat16


def _dot(a, b):
    return jnp.dot(a, b, preferred_element_type=F32)


def _dot_nt(a, b):
    return lax.dot_general(a, b, (((1,), (1,)), ((), ())), preferred_element_type=F32)


def _rms(x, g):
    return x * lax.rsqrt(jnp.mean(x * x, axis=-1, keepdims=True) + NORM_EPS) * g


def _cparams(sem):
    return pltpu.CompilerParams(dimension_semantics=sem, vmem_limit_bytes=VMEM_LIMIT)


def _split3(x):
    hi = x.astype(BF16).astype(F32)
    r = x - hi
    mid = r.astype(BF16).astype(F32)
    lo = (r - mid).astype(BF16).astype(F32)
    return hi, mid, lo


def _np_split3(x):
    x = np.asarray(x, np.float32)
    hi = x.astype(ml_dtypes.bfloat16).astype(np.float32)
    r = x - hi
    mid = r.astype(ml_dtypes.bfloat16).astype(np.float32)
    lo = (r - mid).astype(ml_dtypes.bfloat16).astype(np.float32)
    return hi, mid, lo


def _alibi_c():
    slopes = np.exp2(-8.0 * np.arange(1, NSA_HEADS + 1, dtype=np.float32) / NSA_HEADS).astype(np.float32)
    return slopes * np.float32(LOG2E)


def _row_features(S):
    t = np.arange(S, dtype=np.float32)
    c = _alibi_c()
    rs = np.zeros((S, LANES), np.float32)
    for h in range(NSA_HEADS):
        for j, term in enumerate(_np_split3(c[h] * t)):
            rs[:, U_LANE + 8 * j + h] = -term
    rs[:, ONE_LANE] = 1.0
    rs[:, A_LANE] = np.floor(t / LANES)
    rs[:, B_LANE] = t % LANES
    return jnp.asarray(rs, dtype=BF16)


def _placement():
    c = _alibi_c()
    p = np.zeros((LANES, N_GROUPS * LANES), np.float32)
    for h in range(FOX_HEADS):
        q0 = (G_FQ + h) * LANES + EXT
        k0 = (G_FK + h) * LANES + EXT
        for j in range(3):
            p[ONE_LANE, q0 + j] = -1.0
            p[F_LANE + 8 * j + h, q0 + 3 + j] = 1.0
            p[F_LANE + 8 * j + h, k0 + j] = 1.0
            p[ONE_LANE, k0 + 3 + j] = 1.0
    for h in range(NSA_HEADS):
        q0 = (G_NQ + h) * LANES + EXT
        c128 = _np_split3(c[h] * np.float32(LANES))
        c1 = _np_split3(c[h])
        for j in range(3):
            p[U_LANE + 8 * j + h, q0 + j] = 1.0
            p[ONE_LANE, q0 + 3 + j] = c128[j]
            p[ONE_LANE, q0 + 6 + j] = c1[j]
    k0 = G_NK * LANES + EXT
    for j in range(3):
        p[ONE_LANE, k0 + j] = 1.0
        p[A_LANE, k0 + 3 + j] = 1.0
        p[B_LANE, k0 + 6 + j] = 1.0
    return jnp.asarray(p, dtype=BF16)


def _cmp_key_ext():
    pos = np.arange(N_CMP_PAD, dtype=np.float32) * CMP_STRIDE + (CMP_LEN - 1)
    e = np.zeros((2, N_CMP_PAD, LANES), np.float32)
    for j in range(3):
        e[0, :, EXT + j] = 1.0
        e[0, :, EXT + 3 + j] = np.floor(pos / LANES)
        e[0, :, EXT + 6 + j] = pos % LANES
    return jnp.asarray(e, dtype=BF16)


def _adaln_kernel(c_ref, w_ref, b_ref, o_ref):
    c = c_ref[...]
    act = (c * jax.nn.sigmoid(c)).astype(BF16)
    o_ref[...] = _dot(act, w_ref[...].astype(BF16)) + b_ref[...]


def _adaln(c, w, b):
    B, D = c.shape
    n = w.shape[1]
    return pl.pallas_call(
        _adaln_kernel,
        grid=(n // D,),
        in_specs=[
            pl.BlockSpec((B, D), lambda j: (0, 0)),
            pl.BlockSpec((D, D), lambda j: (0, j)),
            pl.BlockSpec((1, D), lambda j: (0, j)),
        ],
        out_specs=pl.BlockSpec((B, D), lambda j: (0, j)),
        out_shape=jax.ShapeDtypeStruct((B, n), F32),
        compiler_params=_cparams(("parallel",)),
        name="adaln",
    )(c, w, b)


def _inproj_kernel(x_ref, mod_ref, g_ref, wb_ref, ws_ref, bf_ref, tri_ref, rs_ref, p_ref,
                   qa_ref, ckv_ref, ksl_ref, nkv_ref, fq_ref, fk_ref, fv_ref, mg_ref, sm_ref, carry_sc):
    i = pl.program_id(1)
    tm = x_ref.shape[1]
    x = x_ref[0]
    h = _rms(x, g_ref[...]) * (1.0 + mod_ref[0, 1:2, :]) + mod_ref[0, 0:1, :]
    hb = h.astype(BF16)
    lane = lax.broadcasted_iota(jnp.int32, (tm, LANES), 1)
    lower = lane < HEAD_DIM
    ones_col = (lane == EXT).astype(F32)

    z = _dot(hb, ws_ref[...]) + bf_ref[...]
    logsig = jnp.minimum(z, 0.0) - jnp.log1p(jnp.exp(-jnp.abs(z)))
    sm_ref[0] = jnp.where(lane < F_LANE, jax.nn.sigmoid(z), logsig)

    @pl.when(i == 0)
    def _():
        carry_sc[...] = jnp.zeros(carry_sc.shape, F32)

    is_f = (lane >= F_LANE) & (lane < F_LANE + FOX_HEADS)
    l_hi, l_mid, l_lo = _split3(jnp.where(is_f, logsig, 0.0))
    tri = tri_ref[...]
    cum = carry_sc[...] + _dot(tri, l_hi.astype(BF16)) + _dot(tri, l_mid.astype(BF16)) + _dot(tri, l_lo.astype(BF16))
    carry_sc[...] = cum[tm - 1:tm, :]
    f_hi, f_mid, f_lo = _split3(cum * LOG2E)
    feat = (f_hi + pltpu.roll(f_mid, 8, 1) + pltpu.roll(f_lo, 16, 1) + rs_ref[...].astype(F32)).astype(BF16)

    ext_pairs = {}

    def ext(group):
        first = group - group % 2
        if first not in ext_pairs:
            width = min(2, N_GROUPS - first) * LANES
            ext_pairs[first] = _dot(feat, p_ref[:, first * LANES:first * LANES + width])
        off = (group - first) * LANES
        return ext_pairs[first][:, off:off + LANES]

    def piece(acc, idx, extra):
        pair = acc[:, (idx // 2) * LANES:(idx // 2 + 1) * LANES]
        if idx % 2:
            pair = pltpu.roll(pair, HEAD_DIM, 1)
        return jnp.where(lower, pair, extra).astype(BF16)

    qscale = (HEAD_DIM ** -0.5) * LOG2E
    acc = _dot(hb, wb_ref[:, 0:NSA_W]) * qscale
    for hd in range(NSA_HEADS):
        qa_ref[0, hd] = piece(acc, hd, ext(G_NQ + hd))
    off = NSA_W
    acc = _dot(hb, wb_ref[:, off:off + 6 * NSA_KV_W])
    for pc in range(4):
        ckv_ref[0, pc] = acc[:, pc * HEAD_DIM:(pc + 1) * HEAD_DIM].astype(BF16)
    ext_k = ext(G_NK)
    t = i * tm + lax.broadcasted_iota(jnp.int32, (tm, LANES), 0)
    block_onehot = (lane == t // SEL_LEN).astype(BF16)
    for g in range(NSA_KV_GROUPS):
        ksl_ref[0, g, :, 0:LANES] = piece(acc, 4 + g, ext_k)
        ksl_ref[0, g, :, LANES:2 * LANES] = block_onehot
        nkv_ref[0, g] = piece(acc, 6 + g, ones_col)
        nkv_ref[0, 2 + g] = piece(acc, 8 + g, ext_k)
        nkv_ref[0, 4 + g] = piece(acc, 10 + g, ones_col)
    off += 6 * NSA_KV_W
    acc = _dot(hb, wb_ref[:, off:off + FOX_W]) * qscale
    for hd in range(FOX_HEADS):
        fq_ref[0, hd] = piece(acc, hd, ext(G_FQ + hd))
    off += FOX_W
    acc = _dot(hb, wb_ref[:, off:off + FOX_W])
    for hd in range(FOX_HEADS):
        fk_ref[0, hd] = piece(acc, hd, ext(G_FK + hd))
    off += FOX_W
    acc = _dot(hb, wb_ref[:, off:off + FOX_W])
    for hd in range(FOX_HEADS):
        fv_ref[0, hd] = piece(acc, hd, ones_col)
    off += FOX_W
    for c in range(4):
        acc = _dot(hb, wb_ref[:, off + c * 512: off + (c + 1) * 512])
        mg_ref[0, :, c * 512:(c + 1) * 512] = jax.nn.sigmoid(acc).astype(BF16)


def _inproj(x, mod, g, wb, ws, bfp, tri, rs, pm):
    B, S, D = x.shape
    tm = IN_TILE
    nb = wb.shape[1]
    const2 = lambda b, i: (0, 0)
    heads = lambda n: pl.BlockSpec((1, n, tm, LANES), lambda b, i: (b, 0, i, 0))
    hshape = lambda n: jax.ShapeDtypeStruct((B, n, S, LANES), BF16)
    return pl.pallas_call(
        _inproj_kernel,
        grid=(B, S // tm),
        in_specs=[
            pl.BlockSpec((1, tm, D), lambda b, i: (b, i, 0)),
            pl.BlockSpec((1, 6, D), lambda b, i: (b, 0, 0)),
            pl.BlockSpec((1, D), const2),
            pl.BlockSpec((D, nb), const2),
            pl.BlockSpec((D, LANES), const2),
            pl.BlockSpec((1, LANES), const2),
            pl.BlockSpec((tm, tm), const2),
            pl.BlockSpec((tm, LANES), lambda b, i: (i, 0)),
            pl.BlockSpec((LANES, N_GROUPS * LANES), const2),
        ],
        out_specs=[
            heads(NSA_HEADS),
            pl.BlockSpec((1, 4, tm, HEAD_DIM), lambda b, i: (b, 0, i, 0)),
            pl.BlockSpec((1, NSA_KV_GROUPS, tm, 2 * LANES), lambda b, i: (b, 0, i, 0)),
            heads(6), heads(FOX_HEADS), heads(FOX_HEADS), heads(FOX_HEADS),
            pl.BlockSpec((1, tm, 2 * D), lambda b, i: (b, i, 0)),
            pl.BlockSpec((1, tm, LANES), lambda b, i: (b, i, 0)),
        ],
        out_shape=[
            hshape(NSA_HEADS),
            jax.ShapeDtypeStruct((B, 4, S, HEAD_DIM), BF16),
            jax.ShapeDtypeStruct((B, NSA_KV_GROUPS, S, 2 * LANES), BF16),
            hshape(6), hshape(FOX_HEADS), hshape(FOX_HEADS), hshape(FOX_HEADS),
            jax.ShapeDtypeStruct((B, S, 2 * D), BF16),
            jax.ShapeDtypeStruct((B, S, LANES), F32),
        ],
        scratch_shapes=[pltpu.VMEM((1, LANES), F32)],
        compiler_params=_cparams(("parallel", "arbitrary")),
        name="inproj",
    )(x, mod, g, wb, ws, bfp, tri, rs, pm)


def _compress_kernel(x_ref, pe_ref, w1_ref, w2_ref, e_ref, o_ref):
    x = x_ref[0, 0].astype(F32)
    x_lo = (x + pe_ref[0, 0]).astype(BF16)
    x_hi = (x + pe_ref[0, 1]).astype(BF16)
    y_lo = _dot(x_lo, w1_ref[0, 0])
    y_hi = _dot(x_hi, w1_ref[0, 1])
    n = y_hi.shape[0]
    hid = y_lo + pltpu.roll(y_hi, n - 1, 0)
    hid = jax.nn.gelu(hid)
    o_ref[0, 0] = (_dot(hid.astype(BF16), w2_ref[0]) + e_ref[0].astype(F32)).astype(BF16)


def _compress(kv_rows, pe, w1, w2, e):
    B = kv_rows.shape[0]
    R, C = kv_rows.shape[2], kv_rows.shape[3]
    return pl.pallas_call(
        _compress_kernel,
        grid=(B, 4),
        in_specs=[
            pl.BlockSpec((1, 1, R, C), lambda b, p: (b, p, 0, 0)),
            pl.BlockSpec((1, 2, 1, C), lambda b, p: (p // 2, 0, 0, 0)),
            pl.BlockSpec((1, 2, C, HEAD_DIM), lambda b, p: (p // 2, 0, 0, 0)),
            pl.BlockSpec((1, HEAD_DIM, LANES), lambda b, p: (p // 2, 0, 0)),
            pl.BlockSpec((1, R, LANES), lambda b, p: (p // 2, 0, 0)),
        ],
        out_specs=pl.BlockSpec((1, 1, R, LANES), lambda b, p: (b, p, 0, 0)),
        out_shape=jax.ShapeDtypeStruct((B, 4, R, LANES), BF16),
        compiler_params=_cparams(("parallel", "parallel")),
        name="compress",
    )(kv_rows, pe, w1, w2, e)


def _gate_rows(sm, g, branch):
    col = lax.broadcasted_iota(jnp.int32, sm.shape, 1)
    parts = []
    for hl in range(NSA_HPG):
        want = 3 * (NSA_HPG * g + hl) + branch
        parts.append(jnp.sum(jnp.where(col == want, sm, 0.0), axis=-1, keepdims=True))
    return jnp.concatenate(parts, axis=0)


def _head_tile(y):
    n = y.shape[0] // NSA_HPG
    lane = lax.broadcasted_iota(jnp.int32, (n, LANES), 1)
    hs = [y[i * n:(i + 1) * n] for i in range(NSA_HPG)]
    pairs = [jnp.where(lane < HEAD_DIM, hs[2 * i], pltpu.roll(hs[2 * i + 1], HEAD_DIM, 1)) for i in range(2)]
    return jnp.concatenate(pairs, axis=1)


def _cmp_kernel(q_ref, kc_ref, vc_ref, sm_ref, ovt_ref, oc_ref, selb_ref, flag_ref, imp_sc):
    g = pl.program_id(1)
    step_q0 = pl.program_id(2) * CMP_SUB * Q_TILE
    last_visible = (step_q0 + CMP_SUB * Q_TILE - CMP_LEN) // CMP_STRIDE
    chunks = last_visible // LANES + 1

    def attend(width):
        for sub in range(CMP_SUB):
            rows = pl.ds(sub * Q_TILE, Q_TILE)
            q = q_ref[0, :, rows, :].reshape(NSA_HPG * Q_TILE, LANES)
            oc, imp = _cmp_attend(q, kc_ref[0, 0, 0:width, :], vc_ref[0, 0, 0:width, :], sm_ref[0, rows, :],
                                  ovt_ref[:, 0:width], g, step_q0 + sub * Q_TILE)
            oc_ref[0, rows, :] = oc
            imp_sc[sub] = imp

    for v in range(1, N_CMP_PAD // LANES + 1):
        pl.when(chunks == v)(functools.partial(attend, v * LANES))

    for sub in range(CMP_SUB):
        selb, flag = _select_blocks(imp_sc[sub], step_q0 + sub * Q_TILE)
        selb_ref[0, 0, pl.ds(sub * Q_TILE, Q_TILE), :] = selb
        flag_ref[0, 0, sub] = flag


def _cmp_attend(q, kc, vc, sm, ovt, g, q0):
    width = kc.shape[0]
    s = _dot_nt(q, kc)
    r = lax.broadcasted_iota(jnp.int32, (NSA_HPG * Q_TILE, 1), 0) % Q_TILE
    n = lax.broadcasted_iota(jnp.int32, (1, width), 1)
    dc = (q0 + r) - (n * CMP_STRIDE + (CMP_LEN - 1))
    mask = (dc >= 0) & (n < N_CMP_PAD - 1)
    l = jnp.where(mask, s, NEG)
    m = jnp.max(l, axis=-1, keepdims=True)
    e = jnp.where(mask, jnp.exp2(l - m), 0.0)
    pc = e / jnp.maximum(jnp.sum(e, axis=-1, keepdims=True), 1e-30)
    oc = _dot(pc.astype(BF16), vc)
    oc = _head_tile(oc * _gate_rows(sm, g, 0))
    ps = pc[0:Q_TILE]
    for i in range(1, NSA_HPG):
        ps = ps + pc[i * Q_TILE:(i + 1) * Q_TILE]
    ps_hi = ps.astype(BF16)
    ps_lo = (ps - ps_hi.astype(F32)).astype(BF16)
    return oc, _dot_nt(ovt, ps_hi) + _dot_nt(ovt, ps_lo)


def _select_blocks(imp, q0):
    j = lax.broadcasted_iota(jnp.int32, imp.shape, 0)
    jf = j.astype(F32)
    t = q0 + lax.broadcasted_iota(jnp.int32, (1, Q_TILE), 1)
    cur = t // SEL_LEN
    forced = (j == 0) | (j == cur) | (j == cur - 1)
    v = jnp.where(j > cur, -FORCE, jnp.where(forced, FORCE, imp))
    sel = jnp.zeros(imp.shape, jnp.bool_)
    for _ in range(N_SEL):
        mx = jnp.max(v, axis=0, keepdims=True)
        idx = jnp.min(jnp.where(v == mx, jf, float(LANES)), axis=0, keepdims=True)
        pick = jf == idx
        sel = sel | pick
        v = jnp.where(pick, -3e38, v)
    live_t = jnp.where(sel & (j <= cur), 1.0, 0.0).astype(BF16)
    eye = (lax.broadcasted_iota(jnp.int32, imp.shape, 0) == lax.broadcasted_iota(jnp.int32, imp.shape, 1))
    live = _dot_nt(eye.astype(BF16), live_t)
    selb = jnp.where(live > 0.5, 0.0, NEG).astype(BF16)
    return selb, jnp.max(live, axis=0, keepdims=True).astype(jnp.int32)


def _cmp_attention(qa, kvc, sm, ov):
    B, H, S, _ = qa.shape
    G = NSA_KV_GROUPS
    nq = S // Q_TILE
    qt = CMP_SUB * Q_TILE
    return pl.pallas_call(
        _cmp_kernel,
        grid=(B, G, nq // CMP_SUB),
        in_specs=[
            pl.BlockSpec((1, NSA_HPG, qt, LANES), lambda b, g, i: (b, g, i, 0)),
            pl.BlockSpec((1, 1, N_CMP_PAD, LANES), lambda b, g, i: (b, g, 0, 0)),
            pl.BlockSpec((1, 1, N_CMP_PAD, LANES), lambda b, g, i: (b, 2 + g, 0, 0)),
            pl.BlockSpec((1, qt, LANES), lambda b, g, i: (b, i, 0)),
            pl.BlockSpec((LANES, N_CMP_PAD), lambda b, g, i: (0, 0)),
        ],
        out_specs=[
            pl.BlockSpec((1, qt, NSA_HPG * HEAD_DIM), lambda b, g, i: (b, i, g)),
            pl.BlockSpec((1, 1, qt, LANES), lambda b, g, i: (b, g, i, 0)),
            pl.BlockSpec((1, 1, CMP_SUB, 1, LANES), lambda b, g, i: (b, g, i, 0, 0)),
        ],
        out_shape=[
            jax.ShapeDtypeStruct((B, S, NSA_W), F32),
            jax.ShapeDtypeStruct((B, G, S, LANES), BF16),
            jax.ShapeDtypeStruct((B, G, nq, 1, LANES), jnp.int32),
        ],
        scratch_shapes=[pltpu.VMEM((CMP_SUB, LANES, Q_TILE), F32)],
        compiler_params=_cparams(("parallel", "parallel", "parallel")),
        name="cmp_attention",
    )(qa, kvc, kvc, sm, ov)


def _online_update(s, v, m_ref, acc_ref):
    m_old = m_ref[...]
    m_new = jnp.maximum(m_old, jnp.max(s, axis=-1, keepdims=True))
    chunks = [s[:, c * LANES:(c + 1) * LANES] - m_new for c in range(s.shape[1] // LANES)]
    p = jnp.exp2(jnp.concatenate(chunks, axis=1))
    acc_ref[...] = jnp.exp2(m_old - m_new) * acc_ref[...] + _dot(p.astype(BF16), v)
    m_ref[...] = m_new


def _normalized(acc):
    return acc / jnp.maximum(acc[:, EXT:EXT + 1], 1e-30)


def _attend_once(s, v):
    m = jnp.broadcast_to(jnp.max(s, axis=-1, keepdims=True), (s.shape[0], LANES))
    chunks = [s[:, c * LANES:(c + 1) * LANES] - m for c in range(s.shape[1] // LANES)]
    p = jnp.exp2(jnp.concatenate(chunks, axis=1))
    return _normalized(_dot(p.astype(BF16), v))


def _selwin_kernel(list_ref, cnt_ref, q_ref, ks_ref, vs_ref, kw_ref, vw_ref, selb_ref, oc_ref, sm_ref,
                   o_ref, m_a, acc_a, m_b, acc_b):
    b = pl.program_id(0)
    g = pl.program_id(1)
    qb = pl.program_id(2)
    nq = pl.num_programs(2)
    rows = NSA_HPG * SW_TILE
    q4 = q_ref[0].reshape(rows, LANES)
    q_aug = jnp.concatenate([q4, jnp.concatenate([selb_ref[0, 0]] * NSA_HPG, axis=0)], axis=1)
    r = lax.broadcasted_iota(jnp.int32, (rows, 1), 0) % SW_TILE
    c = lax.broadcasted_iota(jnp.int32, (1, K_TILE), 1)
    rel = r - c
    diag = qb // (K_TILE // SW_TILE)

    def sel_tile(kt, m_ref, acc_ref, causal=False, bias=None):
        start = pl.multiple_of(kt * K_TILE, K_TILE)
        s = _dot_nt(q_aug, ks_ref[0, 0, pl.ds(start, K_TILE), :])
        if bias is not None:
            s = s + bias
        if causal:
            s = jnp.where(rel + (qb * SW_TILE - kt * K_TILE) >= 0, s, NEG)
        _online_update(s, vs_ref[0, 0, pl.ds(start, K_TILE), :], m_ref, acc_ref)

    for m_ref, acc_ref in ((m_a, acc_a), (m_b, acc_b)):
        m_ref[...] = jnp.full(m_ref.shape, NEG, F32)
        acc_ref[...] = jnp.zeros(acc_ref.shape, F32)
    step = (b * NSA_KV_GROUPS + g) * nq + qb
    count = cnt_ref[step]
    base = step * MAX_TILES

    def body(p, carry):
        second = 2 * p + 1
        sel_tile(list_ref[base + 2 * p], m_a, acc_a)
        sel_tile(list_ref[base + jnp.minimum(second, MAX_TILES - 1)], m_b, acc_b,
                 bias=jnp.where(second < count, 0.0, NEG))
        return carry

    lax.fori_loop(0, (count + 1) // 2, body, 0)
    m_new = jnp.maximum(m_a[...], m_b[...])
    acc_a[...] = jnp.exp2(m_a[...] - m_new) * acc_a[...] + jnp.exp2(m_b[...] - m_new) * acc_b[...]
    m_a[...] = m_new
    sel_tile(diag, m_a, acc_a, causal=True)
    o_sel = _normalized(acc_a[...])

    span = WINDOW + K_TILE
    wstart = pl.multiple_of(jnp.maximum(diag - WINDOW // K_TILE, 0) * K_TILE, K_TILE)
    dist = (qb * SW_TILE + r) - (wstart + lax.broadcasted_iota(jnp.int32, (1, span), 1))
    s = _dot_nt(q4, kw_ref[0, 0, pl.ds(wstart, span), :])
    s = jnp.where((dist >= 0) & (dist < WINDOW), s, NEG)
    o_win = _attend_once(s, vw_ref[0, 0, pl.ds(wstart, span), :])

    sm = sm_ref[0]
    y = _gate_rows(sm, g, 1) * o_sel + _gate_rows(sm, g, 2) * o_win
    o_ref[0] = (oc_ref[0] + _head_tile(y)).astype(BF16)


def _selwin_attention(tile_list, tile_count, qa, ksl, nkv, selb, ocg, sm):
    B, H, S, _ = qa.shape
    G = NSA_KV_GROUPS
    nq = S // SW_TILE
    rows = NSA_HPG * SW_TILE
    kv_spec = lambda piece: pl.BlockSpec((1, 1, S, LANES), lambda b, g, i, tl, tc: (b, piece + g, 0, 0))
    out_tile = pl.BlockSpec((1, SW_TILE, NSA_HPG * HEAD_DIM), lambda b, g, i, tl, tc: (b, i, g))
    grid_spec = pltpu.PrefetchScalarGridSpec(
        num_scalar_prefetch=2,
        grid=(B, G, nq),
        in_specs=[
            pl.BlockSpec((1, NSA_HPG, SW_TILE, LANES), lambda b, g, i, tl, tc: (b, g, i, 0)),
            pl.BlockSpec((1, 1, S, 2 * LANES), lambda b, g, i, tl, tc: (b, g, 0, 0)),
            kv_spec(0), kv_spec(2), kv_spec(4),
            pl.BlockSpec((1, 1, SW_TILE, LANES), lambda b, g, i, tl, tc: (b, g, i, 0)),
            out_tile,
            pl.BlockSpec((1, SW_TILE, LANES), lambda b, g, i, tl, tc: (b, i, 0)),
        ],
        out_specs=out_tile,
        scratch_shapes=[pltpu.VMEM((rows, LANES), F32)] * 4,
    )
    return pl.pallas_call(
        _selwin_kernel,
        grid_spec=grid_spec,
        out_shape=jax.ShapeDtypeStruct((B, S, NSA_W), BF16),
        compiler_params=_cparams(("parallel", "parallel", "arbitrary")),
        name="selwin_attention",
    )(tile_list, tile_count, qa, ksl, nkv, nkv, nkv, selb, ocg, sm)


def _fox_kernel(q_ref, k_ref, v_ref, o_ref, m_sc, acc_sc, *, tq):
    qi = pl.program_id(2)
    m_sc[...] = jnp.full(m_sc.shape, NEG, F32)
    acc_sc[...] = jnp.zeros(acc_sc.shape, F32)

    def tile(kt, width, causal):
        start = pl.multiple_of(kt * tq, tq)
        for hh in range(FOX_HPS):
            s = _dot_nt(q_ref[0, hh], k_ref[0, hh, pl.ds(start, width), :])
            if causal:
                r = lax.broadcasted_iota(jnp.int32, s.shape, 0)
                c = lax.broadcasted_iota(jnp.int32, s.shape, 1)
                s = jnp.where(r >= c, s, NEG)
            _online_update(s, v_ref[0, hh, pl.ds(start, width), :], m_sc.at[hh], acc_sc.at[hh])

    def body(kp, carry):
        tile(2 * kp, 2 * tq, False)
        return carry

    lax.fori_loop(0, qi // 2, body, 0)

    @pl.when(qi % 2 == 1)
    def _():
        tile(qi - 1, tq, False)

    tile(qi, tq, True)
    lane = lax.broadcasted_iota(jnp.int32, (tq, LANES), 1)
    o = [_normalized(acc_sc[hh]) for hh in range(FOX_HPS)]
    for pr in range(FOX_HPS // 2):
        o_ref[0, :, pr * LANES:(pr + 1) * LANES] = jnp.where(
            lane < HEAD_DIM, o[2 * pr], pltpu.roll(o[2 * pr + 1], HEAD_DIM, 1)).astype(BF16)


def _fox_attention(fq, fk, fv, tq=512):
    B, H, S, _ = fq.shape
    hps = FOX_HPS
    return pl.pallas_call(
        functools.partial(_fox_kernel, tq=tq),
        grid=(B, H // hps, S // tq),
        in_specs=[
            pl.BlockSpec((1, hps, tq, LANES), lambda b, h, i: (b, h, i, 0)),
            pl.BlockSpec((1, hps, S, LANES), lambda b, h, i: (b, h, 0, 0)),
            pl.BlockSpec((1, hps, S, LANES), lambda b, h, i: (b, h, 0, 0)),
        ],
        out_specs=pl.BlockSpec((1, tq, hps * HEAD_DIM), lambda b, h, i: (b, i, h)),
        out_shape=jax.ShapeDtypeStruct((B, S, FOX_W), BF16),
        scratch_shapes=[
            pltpu.VMEM((hps, tq, LANES), F32),
            pltpu.VMEM((hps, tq, LANES), F32),
        ],
        compiler_params=_cparams(("parallel", "parallel", "arbitrary")),
        name="fox_attention",
    )(fq, fk, fv)


def _merge_kernel(ya_ref, yb_ref, mg_ref, x_ref, mod_ref, gpost_ref, gpre_ref,
                  wa_ref, wb_ref, wo_ref, wrh_ref, wrl_ref, br_ref, stri_ref,
                  x1_ref, h2_ref, rt_ref, cnt_ref):
    D = D_MODEL

    @pl.when((pl.program_id(0) == 0) & (pl.program_id(1) == 0))
    def _():
        cnt_ref[...] = jnp.zeros(cnt_ref.shape, F32)

    a = _dot(ya_ref[0], wa_ref[...])
    bq = _dot(yb_ref[0], wb_ref[...])
    mg = mg_ref[0]
    u = mg[:, :D].astype(F32) * a + mg[:, D:].astype(F32) * bq
    mixed = _dot(u.astype(BF16), wo_ref[...])
    x1 = x_ref[0] + mod_ref[0, 2:3, :] * _rms(mixed, gpost_ref[...])
    x1_ref[0] = x1
    h2 = _rms(x1, gpre_ref[...]) * (1.0 + mod_ref[0, 4:5, :]) + mod_ref[0, 3:4, :]
    hi = h2.astype(BF16)
    lo = (h2 - hi.astype(F32)).astype(BF16)
    h2_ref[0] = h2
    lg = _dot(hi, wrh_ref[...]) + _dot(lo, wrh_ref[...]) + _dot(hi, wrl_ref[...]) + br_ref[...]

    lane = lax.broadcasted_iota(jnp.int32, lg.shape, 1)
    lanef = lane.astype(F32)
    no_lane = float(LANES)
    is_g = lane < N_EXPERT_GROUPS
    gl = jnp.where(is_g, lg, NEG)
    gmax = jnp.max(gl, axis=-1, keepdims=True)
    pg_top = 1.0 / jnp.sum(jnp.where(is_g, jnp.exp(gl - gmax), 0.0), axis=-1, keepdims=True)
    g_idx = jnp.min(jnp.where(is_g & (gl == gmax), lanef, no_lane), axis=-1, keepdims=True)
    in_grp = ((lane >= N_EXPERT_GROUPS) & (lane < N_EXPERT_GROUPS + N_EXPERTS)
              & (((lane - N_EXPERT_GROUPS) // EXPERTS_PER_GROUP).astype(F32) == g_idx))
    le = jnp.where(in_grp, lg, NEG)
    m1 = jnp.max(le, axis=-1, keepdims=True)
    i1 = jnp.min(jnp.where(in_grp & (le == m1), lanef, no_lane), axis=-1, keepdims=True)
    rest = in_grp & (lanef != i1)
    le2 = jnp.where(rest, lg, NEG)
    m2 = jnp.max(le2, axis=-1, keepdims=True)
    i2 = jnp.min(jnp.where(rest & (le2 == m2), lanef, no_lane), axis=-1, keepdims=True)
    e21 = jnp.exp(m2 - m1)
    w1 = pg_top / (1.0 + e21)
    w2 = w1 * e21
    pick1 = lanef == i1
    pick2 = lanef == i2
    onehot = jnp.where(pick1 | pick2, 1.0, 0.0)
    before = cnt_ref[...] + _dot(stri_ref[...], onehot.astype(BF16))
    rank1 = jnp.sum(jnp.where(pick1, before, 0.0), axis=-1, keepdims=True)
    rank2 = jnp.sum(jnp.where(pick2, before, 0.0), axis=-1, keepdims=True)
    cnt_ref[...] = cnt_ref[...] + jnp.sum(onehot, axis=0, keepdims=True)
    fields = [i1 - N_EXPERT_GROUPS, i2 - N_EXPERT_GROUPS, rank1, rank2, w1, w2]
    rt = jnp.zeros(lg.shape, F32)
    for k, f in enumerate(fields):
        rt = jnp.where(lane == k, f, rt)
    rt_ref[0] = rt


def _merge(ya, yb, mg, x, mod, gpost, gpre, wa, wb, wo, wrh, wrl, br, stri):
    B, S, D = x.shape
    tm = MERGE_TILE
    c2 = lambda b, i: (0, 0)
    row = lambda w: pl.BlockSpec((1, tm, w), lambda b, i: (b, i, 0))
    return pl.pallas_call(
        _merge_kernel,
        grid=(B, S // tm),
        in_specs=[
            row(NSA_W), row(FOX_W), row(2 * D), row(D),
            pl.BlockSpec((1, 6, D), lambda b, i: (b, 0, 0)),
            pl.BlockSpec((1, D), c2), pl.BlockSpec((1, D), c2),
            pl.BlockSpec((NSA_W, D), c2), pl.BlockSpec((FOX_W, D), c2), pl.BlockSpec((D, D), c2),
            pl.BlockSpec((D, LANES), c2), pl.BlockSpec((D, LANES), c2), pl.BlockSpec((1, LANES), c2),
            pl.BlockSpec((tm, tm), c2),
        ],
        out_specs=[row(D), row(D), row(LANES), pl.BlockSpec((1, LANES), c2)],
        out_shape=[
            jax.ShapeDtypeStruct((B, S, D), F32),
            jax.ShapeDtypeStruct((B, S, D), F32),
            jax.ShapeDtypeStruct((B, S, LANES), F32),
            jax.ShapeDtypeStruct((1, LANES), F32),
        ],
        compiler_params=_cparams(("arbitrary", "arbitrary")),
        name="merge",
    )(ya, yb, mg, x, mod, gpost, gpre, wa, wb, wo, wrh, wrl, br, stri)


def _dispatch_kernel(dest_ref, h_hbm, zero_hbm, xb_hbm, sem):
    del zero_hbm
    j = pl.program_id(0)
    rows = DISPATCH_TILE * EXPERT_TOP_K

    def wait_step(slot):
        pltpu.make_async_copy(h_hbm.at[pl.ds(0, rows)], xb_hbm.at[pl.ds(0, rows)], sem.at[slot]).wait()

    @pl.when(j > 0)
    def _():
        wait_step((j - 1) % 2)

    def body(r, carry):
        t = j * DISPATCH_TILE + r
        for k in range(EXPERT_TOP_K):
            pltpu.make_async_copy(h_hbm.at[pl.ds(t, 1)], xb_hbm.at[pl.ds(dest_ref[t * EXPERT_TOP_K + k], 1)],
                                  sem.at[j % 2]).start(priority=k)
        return carry

    lax.fori_loop(0, DISPATCH_TILE, body, 0, unroll=8)

    @pl.when(j == pl.num_programs(0) - 1)
    def _():
        wait_step(j % 2)


def _dispatch(dest, h2, cap):
    T, D = h2.shape
    grid_spec = pltpu.PrefetchScalarGridSpec(
        num_scalar_prefetch=1,
        grid=(T // DISPATCH_TILE,),
        in_specs=[pl.BlockSpec(memory_space=pl.ANY), pl.BlockSpec(memory_space=pl.ANY)],
        out_specs=pl.BlockSpec(memory_space=pl.ANY),
        scratch_shapes=[pltpu.SemaphoreType.DMA((2,))],
    )
    return pl.pallas_call(
        _dispatch_kernel,
        grid_spec=grid_spec,
        out_shape=jax.ShapeDtypeStruct((cap, D), F32),
        input_output_aliases={2: 0},
        compiler_params=_cparams(("arbitrary",)),
        name="dispatch",
    )(dest, h2, jnp.zeros((cap, D), F32))


def _expert_kernel(be_ref, na_ref, x_ref, wg_ref, wu_ref, wd_ref, o_ref, wg_b, wu_b, wd_b):
    i = pl.program_id(0)
    n_active = na_ref[0]

    @pl.when((i == 0) | (be_ref[i] != be_ref[jnp.maximum(i - 1, 0)]))
    def _():
        wg_b[...] = wg_ref[0].astype(BF16)
        wu_b[...] = wu_ref[0].astype(BF16)
        wd_b[...] = wd_ref[0].astype(BF16)

    @pl.when(i < n_active)
    def _():
        x = x_ref[...].astype(BF16)
        gate = _dot(x, wg_b[...])
        up = _dot(x, wu_b[...])
        mid = (gate * jax.nn.sigmoid(gate) * up).astype(BF16)
        o_ref[...] = _dot(mid, wd_b[...])

    @pl.when(i >= n_active)
    def _():
        o_ref[...] = jnp.zeros(o_ref.shape, o_ref.dtype)


def _experts(block_expert, n_active, xb, wg, wu, wd):
    cap, D = xb.shape
    nblk = cap // MOE_TILE
    grid_spec = pltpu.PrefetchScalarGridSpec(
        num_scalar_prefetch=2,
        grid=(nblk,),
        in_specs=[
            pl.BlockSpec((MOE_TILE, D), lambda i, be, na: (jnp.minimum(i, na[0] - 1), 0)),
            pl.BlockSpec((1, D, D_EXPERT), lambda i, be, na: (be[i], 0, 0)),
            pl.BlockSpec((1, D, D_EXPERT), lambda i, be, na: (be[i], 0, 0)),
            pl.BlockSpec((1, D_EXPERT, D), lambda i, be, na: (be[i], 0, 0)),
        ],
        out_specs=pl.BlockSpec((MOE_TILE, D), lambda i, be, na: (i, 0)),
        scratch_shapes=[
            pltpu.VMEM((D, D_EXPERT), BF16),
            pltpu.VMEM((D, D_EXPERT), BF16),
            pltpu.VMEM((D_EXPERT, D), BF16),
        ],
    )
    return pl.pallas_call(
        _expert_kernel,
        grid_spec=grid_spec,
        out_shape=jax.ShapeDtypeStruct((cap, D), F32),
        compiler_params=_cparams(("arbitrary",)),
        name="experts",
    )(block_expert, n_active, xb, wg, wu, wd)


def _final_kernel(dest_ref, x1_ref, rt_ref, mod_ref, g_ref, y_hbm, o_ref, *scratch):
    j = pl.program_id(0)
    last_tile = pl.num_programs(0) - 1
    tm = o_ref.shape[0]
    ring = GATHER_AHEAD + 1
    bufs = tuple(scratch[EXPERT_TOP_K * sl:EXPERT_TOP_K * (sl + 1)] for sl in range(ring))
    sem = scratch[-1]

    def row_copy(tile, r, k, sl):
        row = dest_ref[(tile * tm + r) * EXPERT_TOP_K + k]
        return pltpu.make_async_copy(y_hbm.at[pl.ds(row, 1)], bufs[sl][k].at[pl.ds(r, 1)], sem.at[sl])

    def wait_rows(sl):
        for k in range(EXPERT_TOP_K):
            pltpu.make_async_copy(y_hbm.at[pl.ds(0, tm)], bufs[sl][k], sem.at[sl]).wait()

    @pl.when(j == 0)
    def _():
        for ahead in range(GATHER_AHEAD):
            def body(r, carry, ahead=ahead):
                for k in range(EXPERT_TOP_K):
                    row_copy(jnp.minimum(ahead, last_tile), r, k, ahead).start()
                return carry
            lax.fori_loop(0, tm, body, 0, unroll=4)

    def step(sl):
        wait_rows(sl)
        nxt = jnp.minimum(j + GATHER_AHEAD, last_tile)
        nxt_sl = (sl + GATHER_AHEAD) % ring
        for r in range(tm):
            for k in range(EXPERT_TOP_K):
                row_copy(nxt, r, k, nxt_sl).start(priority=k)
        rt = rt_ref[...]
        lane = lax.broadcasted_iota(jnp.int32, rt.shape, 1)
        w0 = jnp.sum(jnp.where(lane == 4, rt, 0.0), axis=-1, keepdims=True)
        w1 = jnp.sum(jnp.where(lane == 5, rt, 0.0), axis=-1, keepdims=True)
        y = w0 * bufs[sl][0][...] + w1 * bufs[sl][1][...]
        o_ref[...] = x1_ref[...] + mod_ref[0, 5:6, :] * _rms(y, g_ref[...])

        @pl.when(j == last_tile)
        def _():
            for ahead in range(1, GATHER_AHEAD + 1):
                wait_rows((sl + ahead) % ring)

    for sl in range(ring):
        pl.when(j % ring == sl)(functools.partial(step, sl))


def _final(dest, x1, rt, mod, g, yb, tiles_per_batch):
    T, D = x1.shape
    tm = FINAL_TILE
    grid_spec = pltpu.PrefetchScalarGridSpec(
        num_scalar_prefetch=1,
        grid=(T // tm,),
        in_specs=[
            pl.BlockSpec((tm, D), lambda j, d: (j, 0)),
            pl.BlockSpec((tm, LANES), lambda j, d: (j, 0)),
            pl.BlockSpec((1, 6, D), lambda j, d: (j // tiles_per_batch, 0, 0)),
            pl.BlockSpec((1, D), lambda j, d: (0, 0)),
            pl.BlockSpec(memory_space=pl.ANY),
        ],
        out_specs=pl.BlockSpec((tm, D), lambda j, d: (j, 0)),
        scratch_shapes=([pltpu.VMEM((tm, D), F32)] * (EXPERT_TOP_K * (GATHER_AHEAD + 1))
                        + [pltpu.SemaphoreType.DMA((GATHER_AHEAD + 1,))]),
    )
    return pl.pallas_call(
        _final_kernel,
        grid_spec=grid_spec,
        out_shape=jax.ShapeDtypeStruct((T, D), F32),
        compiler_params=_cparams(("arbitrary",)),
        name="final",
    )(dest, x1, rt, mod, g, yb)


def _overlap_matrix():
    n = np.arange(N_CMP_PAD)[:, None]
    j = np.arange(LANES)[None, :]
    start = n * CMP_STRIDE
    ov = (start < j * SEL_LEN + SEL_LEN) & (start + CMP_LEN - 1 >= j * SEL_LEN) & (n < N_CMP_PAD - 1)
    return jnp.asarray(ov.T.astype(np.float32), dtype=BF16)


def _pad_cols(w, width=LANES):
    return jnp.pad(w, ((0, 0), (0, width - w.shape[1])))


def _dispatch_plan(rt, cnt, T):
    expert = rt[:, 0:2].astype(jnp.int32)
    rank = rt[:, 2:4].astype(jnp.int32)
    weight = rt[:, 4:6]
    counts = cnt[0, N_EXPERT_GROUPS:N_EXPERT_GROUPS + N_EXPERTS].astype(jnp.int32)
    padded = (counts + MOE_TILE - 1) // MOE_TILE * MOE_TILE
    pad_end = jnp.cumsum(padded)
    pad_start = pad_end - padded
    onehot = expert[:, :, None] == jnp.arange(N_EXPERTS)[None, None, :]
    dest = jnp.sum(jnp.where(onehot, pad_start[None, None, :], 0), axis=-1) + rank
    A = T * EXPERT_TOP_K
    cap = -(-(A + N_EXPERTS * (MOE_TILE - 1)) // MOE_TILE) * MOE_TILE
    nblk = cap // MOE_TILE
    n_active = (pad_end[-1] // MOE_TILE).astype(jnp.int32)
    blk = jnp.arange(nblk) * MOE_TILE
    block_expert = jnp.minimum(jnp.sum(pad_end[None, :] <= blk[:, None], axis=1), N_EXPERTS - 1)
    last = jnp.max(jnp.where(jnp.arange(nblk) < n_active, block_expert, 0))
    block_expert = jnp.where(jnp.arange(nblk) < n_active, block_expert, last).astype(jnp.int32)
    return dest.astype(jnp.int32).reshape(A), cap, block_expert, n_active.reshape(1)


def kernel(x, c, w_ada, b_ada, g_pre_mix, g_post_mix, g_pre_ffn, g_post_ffn, w_in, b_forget,
           cmp_pe_k, cmp_w1_k, cmp_w2_k, cmp_pe_v, cmp_w1_v, cmp_w2_v,
           w_o_nsa, w_o_fox, w_out, w_router_group, b_router_group, w_router_expert, b_router_expert,
           w_exp_gate, w_exp_up, w_exp_down):
    B, S, D = x.shape
    T = B * S
    depth = w_ada.shape[0]
    ov = _overlap_matrix()
    tri = jnp.asarray(np.tril(np.ones((IN_TILE, IN_TILE), np.float32)), dtype=BF16)
    stri = jnp.asarray(np.tril(np.ones((MERGE_TILE, MERGE_TILE), np.float32), -1), dtype=BF16)
    row_feat = _row_features(S)
    placement = _placement()
    cmp_ext = _cmp_key_ext()
    for l in range(depth):
        mod = _adaln(c, w_ada[l], b_ada[l].reshape(1, 6 * D)).reshape(B, 6, D)
        w_qa, w_kva, w_gl, w_fox, w_f, w_mg = jnp.split(w_in[l], IN_SPLITS, axis=-1)
        w_big = jnp.concatenate([w_qa, w_kva, w_fox, w_mg], axis=1).astype(BF16)
        w_small = _pad_cols(jnp.concatenate([w_gl, w_f], axis=1)).astype(BF16)
        bf_pad = jnp.pad(b_forget[l], (F_LANE, LANES - F_LANE - FOX_HEADS)).reshape(1, LANES)
        qa, ckv, ksl, nkv, fq, fk, fv, mg, sm = _inproj(
            x, mod, g_pre_mix[l].reshape(1, D), w_big, w_small, bf_pad, tri, row_feat, placement)

        half = CMP_LEN // 2
        pe = jnp.stack([cmp_pe_k[l], cmp_pe_v[l]]).reshape(2, 2, 1, half * HEAD_DIM)
        w1 = jnp.stack([cmp_w1_k[l], cmp_w1_v[l]]).reshape(2, 2, half * HEAD_DIM, HEAD_DIM).astype(BF16)
        w2 = jnp.pad(jnp.stack([cmp_w2_k[l], cmp_w2_v[l]]), ((0, 0), (0, 0), (0, LANES - HEAD_DIM))).astype(BF16)
        kvc = _compress(ckv.reshape(B, 4, S // CMP_STRIDE, CMP_STRIDE * HEAD_DIM), pe, w1, w2, cmp_ext)
        ocg, selb, flags = _cmp_attention(qa, kvc, sm, ov)
        nq = S // SW_TILE
        per_tile = K_TILE // SEL_LEN
        tile_any = jnp.max(flags.reshape(B, NSA_KV_GROUPS, nq, SW_TILE // Q_TILE, MAX_TILES, per_tile), axis=(3, 5))
        tile_id = jnp.arange(MAX_TILES)
        diag = (jnp.arange(nq) // (K_TILE // SW_TILE))[:, None]
        active = (tile_any > 0) & (tile_id < diag)
        slot = jnp.cumsum(active, axis=-1) - 1
        hit = active[..., :, None] & (slot[..., :, None] == tile_id)
        tile_list = jnp.sum(jnp.where(hit, tile_id[:, None], 0), axis=-2).astype(jnp.int32).reshape(-1)
        tile_count = jnp.sum(active, axis=-1).astype(jnp.int32).reshape(-1)
        y_a = _selwin_attention(tile_list, tile_count, qa, ksl, nkv, selb, ocg, sm)

        y_b = _fox_attention(fq, fk, fv)

        w_r = _pad_cols(jnp.concatenate([w_router_group[l], w_router_expert[l]], axis=1))
        w_rh = w_r.astype(BF16)
        w_rl = (w_r - w_rh.astype(F32)).astype(BF16)
        b_r = _pad_cols(jnp.concatenate([b_router_group[l], b_router_expert[l]]).reshape(1, -1))
        x1, h2, rt, cnt = _merge(y_a, y_b, mg, x, mod, g_post_mix[l].reshape(1, D), g_pre_ffn[l].reshape(1, D),
                                 w_o_nsa[l].astype(BF16), w_o_fox[l].astype(BF16), w_out[l].astype(BF16),
                                 w_rh, w_rl, b_r, stri)

        dest, cap, block_expert, n_active = _dispatch_plan(rt.reshape(T, LANES), cnt, T)
        xb = _dispatch(dest, h2.reshape(T, D), cap)
        yb = _experts(block_expert, n_active, xb, w_exp_gate[l], w_exp_up[l], w_exp_down[l])
        x = _final(dest, x1.reshape(T, D), rt.reshape(T, LANES), mod,
                   g_post_ffn[l].reshape(1, D), yb, S // FINAL_TILE).reshape(B, S, D)
    return x
```

```python
import functools

import ml_dtypes
import numpy as np
import jax
import jax.numpy as jnp
from jax import lax
from jax.experimental import pallas as pl
from jax.experimental.pallas import tpu as pltpu

D_MODEL = 1024
HEAD_DIM = 64
NSA_HEADS = 8
NSA_KV_GROUPS = 2
NSA_HPG = NSA_HEADS // NSA_KV_GROUPS
FOX_HEADS = 8
CMP_LEN = 32
CMP_STRIDE = 16
SEL_LEN = 64
N_SEL = 16
WINDOW = 512
N_EXPERT_GROUPS = 4
EXPERTS_PER_GROUP = 8
N_EXPERTS = N_EXPERT_GROUPS * EXPERTS_PER_GROUP
EXPERT_TOP_K = 2
D_EXPERT = D_MODEL // 2
NORM_EPS = 1e-6
NEG = -1e30
FORCE = 1e9
LOG2E = 1.4426950408889634

NSA_W = NSA_HEADS * HEAD_DIM
NSA_KV_W = NSA_KV_GROUPS * HEAD_DIM
FOX_W = FOX_HEADS * HEAD_DIM
IN_SIZES = (NSA_W, 6 * NSA_KV_W, 3 * NSA_HEADS, 3 * FOX_W, FOX_HEADS, 2 * D_MODEL)
IN_SPLITS = tuple(int(v) for v in np.cumsum(IN_SIZES)[:-1])

LANES = 128
Q_TILE = 128
K_TILE = 256
SW_TILE = 256
N_CMP_PAD = 512
MOE_TILE = 256
IN_TILE = 512
MERGE_TILE = 512
FINAL_TILE = 256
FOX_HPS = 4
CMP_SUB = 4
MAX_TILES = 32
GATHER_AHEAD = 3
V7X_VMEM_BYTES = 64 * 1024 * 1024
VMEM_LIMIT = V7X_VMEM_BYTES - 8 * 1024 * 1024
RT_EXPERT, RT_RANK, RT_WEIGHT = 0, 2, 4
PICKED = -3e38

F_LANE = 3 * NSA_HEADS
U_LANE = 64
ONE_LANE = 88
A_LANE = 89
B_LANE = 90
EXT = HEAD_DIM
G_FQ, G_FK, G_NQ, G_NK, N_GROUPS = 0, 8, 16, 24, 25

F32 = jnp.float32
BF16 = jnp.bfloat16


def _dot(a, b):
    return jnp.dot(a, b, preferred_element_type=F32)


def _dot_nt(a, b):
    return lax.dot_general(a, b, (((1,), (1,)), ((), ())), preferred_element_type=F32)


def _rms(x, g):
    return x * lax.rsqrt(jnp.mean(x * x, axis=-1, keepdims=True) + NORM_EPS) * g


def _cparams(sem):
    return pltpu.CompilerParams(dimension_semantics=sem, vmem_limit_bytes=VMEM_LIMIT)


def _split3(x):
    hi = x.astype(BF16).astype(F32)
    r = x - hi
    mid = r.astype(BF16).astype(F32)
    lo = (r - mid).astype(BF16).astype(F32)
    return hi, mid, lo


def _np_split3(x):
    x = np.asarray(x, np.float32)
    hi = x.astype(ml_dtypes.bfloat16).astype(np.float32)
    r = x - hi
    mid = r.astype(ml_dtypes.bfloat16).astype(np.float32)
    lo = (r - mid).astype(ml_dtypes.bfloat16).astype(np.float32)
    return hi, mid, lo


def _alibi_c():
    slopes = np.exp2(-8.0 * np.arange(1, NSA_HEADS + 1, dtype=np.float32) / NSA_HEADS).astype(np.float32)
    return slopes * np.float32(LOG2E)


def _row_features(S):
    t = np.arange(S, dtype=np.float32)
    c = _alibi_c()
    rs = np.zeros((S, LANES), np.float32)
    for h in range(NSA_HEADS):
        for j, term in enumerate(_np_split3(c[h] * t)):
            rs[:, U_LANE + 8 * j + h] = -term
    rs[:, ONE_LANE] = 1.0
    rs[:, A_LANE] = np.floor(t / LANES)
    rs[:, B_LANE] = t % LANES
    return jnp.asarray(rs, dtype=BF16)


def _placement():
    c = _alibi_c()
    p = np.zeros((LANES, N_GROUPS * LANES), np.float32)
    for h in range(FOX_HEADS):
        q0 = (G_FQ + h) * LANES + EXT
        k0 = (G_FK + h) * LANES + EXT
        for j in range(3):
            p[ONE_LANE, q0 + j] = -1.0
            p[F_LANE + 8 * j + h, q0 + 3 + j] = 1.0
            p[F_LANE + 8 * j + h, k0 + j] = 1.0
            p[ONE_LANE, k0 + 3 + j] = 1.0
    for h in range(NSA_HEADS):
        q0 = (G_NQ + h) * LANES + EXT
        c128 = _np_split3(c[h] * np.float32(LANES))
        c1 = _np_split3(c[h])
        for j in range(3):
            p[U_LANE + 8 * j + h, q0 + j] = 1.0
            p[ONE_LANE, q0 + 3 + j] = c128[j]
            p[ONE_LANE, q0 + 6 + j] = c1[j]
    k0 = G_NK * LANES + EXT
    for j in range(3):
        p[ONE_LANE, k0 + j] = 1.0
        p[A_LANE, k0 + 3 + j] = 1.0
        p[B_LANE, k0 + 6 + j] = 1.0
    return jnp.asarray(p, dtype=BF16)


def _cmp_key_ext():
    pos = np.arange(N_CMP_PAD, dtype=np.float32) * CMP_STRIDE + (CMP_LEN - 1)
    e = np.zeros((2, N_CMP_PAD, LANES), np.float32)
    for j in range(3):
        e[0, :, EXT + j] = 1.0
        e[0, :, EXT + 3 + j] = np.floor(pos / LANES)
        e[0, :, EXT + 6 + j] = pos % LANES
    return jnp.asarray(e, dtype=BF16)


def _adaln_kernel(c_ref, w_ref, b_ref, o_ref):
    c = c_ref[...]
    act = (c * jax.nn.sigmoid(c)).astype(BF16)
    o_ref[...] = _dot(act, w_ref[...].astype(BF16)) + b_ref[...]


def _adaln(c, w, b):
    B, D = c.shape
    n = w.shape[1]
    return pl.pallas_call(
        _adaln_kernel,
        grid=(n // D,),
        in_specs=[
            pl.BlockSpec((B, D), lambda j: (0, 0)),
            pl.BlockSpec((D, D), lambda j: (0, j)),
            pl.BlockSpec((1, D), lambda j: (0, j)),
        ],
        out_specs=pl.BlockSpec((B, D), lambda j: (0, j)),
        out_shape=jax.ShapeDtypeStruct((B, n), F32),
        compiler_params=_cparams(("parallel",)),
        name="adaln",
    )(c, w, b)


def _inproj_kernel(x_ref, mod_ref, g_ref, wb_ref, ws_ref, bf_ref, tri_ref, rs_ref, p_ref,
                   qa_ref, ckv_ref, ksl_ref, nkv_ref, fq_ref, fk_ref, fv_ref, mg_ref, sm_ref, carry_sc):
    i = pl.program_id(1)
    tm = x_ref.shape[1]
    x = x_ref[0]
    h = _rms(x, g_ref[...]) * (1.0 + mod_ref[0, 1:2, :]) + mod_ref[0, 0:1, :]
    hb = h.astype(BF16)
    lane = lax.broadcasted_iota(jnp.int32, (tm, LANES), 1)
    lower = lane < HEAD_DIM
    ones_col = (lane == EXT).astype(F32)

    z = _dot(hb, ws_ref[...]) + bf_ref[...]
    logsig = jnp.minimum(z, 0.0) - jnp.log1p(jnp.exp(-jnp.abs(z)))
    sm_ref[0] = jnp.where(lane < F_LANE, jax.nn.sigmoid(z), logsig)

    @pl.when(i == 0)
    def _():
        carry_sc[...] = jnp.zeros(carry_sc.shape, F32)

    is_f = (lane >= F_LANE) & (lane < F_LANE + FOX_HEADS)
    l_hi, l_mid, l_lo = _split3(jnp.where(is_f, logsig, 0.0))
    tri = tri_ref[...]
    cum = carry_sc[...] + _dot(tri, l_hi.astype(BF16)) + _dot(tri, l_mid.astype(BF16)) + _dot(tri, l_lo.astype(BF16))
    carry_sc[...] = cum[tm - 1:tm, :]
    f_hi, f_mid, f_lo = _split3(cum * LOG2E)
    feat = (f_hi + pltpu.roll(f_mid, 8, 1) + pltpu.roll(f_lo, 16, 1) + rs_ref[...].astype(F32)).astype(BF16)

    ext_pairs = {}

    def ext(group):
        first = group - group % 2
        if first not in ext_pairs:
            width = min(2, N_GROUPS - first) * LANES
            ext_pairs[first] = _dot(feat, p_ref[:, first * LANES:first * LANES + width])
        off = (group - first) * LANES
        return ext_pairs[first][:, off:off + LANES]

    def piece(acc, idx, extra):
        pair = acc[:, (idx // 2) * LANES:(idx // 2 + 1) * LANES]
        if idx % 2:
            pair = pltpu.roll(pair, HEAD_DIM, 1)
        return jnp.where(lower, pair, extra).astype(BF16)

    qscale = (HEAD_DIM ** -0.5) * LOG2E
    acc = _dot(hb, wb_ref[:, 0:NSA_W]) * qscale
    for hd in range(NSA_HEADS):
        qa_ref[0, hd] = piece(acc, hd, ext(G_NQ + hd))
    off = NSA_W
    acc = _dot(hb, wb_ref[:, off:off + 6 * NSA_KV_W])
    for pc in range(4):
        ckv_ref[0, pc] = acc[:, pc * HEAD_DIM:(pc + 1) * HEAD_DIM].astype(BF16)
    ext_k = ext(G_NK)
    t = i * tm + lax.broadcasted_iota(jnp.int32, (tm, LANES), 0)
    block_onehot = (lane == t // SEL_LEN).astype(BF16)
    for g in range(NSA_KV_GROUPS):
        ksl_ref[0, g, :, 0:LANES] = piece(acc, 4 + g, ext_k)
        ksl_ref[0, g, :, LANES:2 * LANES] = block_onehot
        nkv_ref[0, g] = piece(acc, 6 + g, ones_col)
        nkv_ref[0, 2 + g] = piece(acc, 8 + g, ext_k)
        nkv_ref[0, 4 + g] = piece(acc, 10 + g, ones_col)
    off += 6 * NSA_KV_W
    acc = _dot(hb, wb_ref[:, off:off + FOX_W]) * qscale
    for hd in range(FOX_HEADS):
        fq_ref[0, hd] = piece(acc, hd, ext(G_FQ + hd))
    off += FOX_W
    acc = _dot(hb, wb_ref[:, off:off + FOX_W])
    for hd in range(FOX_HEADS):
        fk_ref[0, hd] = piece(acc, hd, ext(G_FK + hd))
    off += FOX_W
    acc = _dot(hb, wb_ref[:, off:off + FOX_W])
    for hd in range(FOX_HEADS):
        fv_ref[0, hd] = piece(acc, hd, ones_col)
    off += FOX_W
    for c in range(4):
        acc = _dot(hb, wb_ref[:, off + c * 512: off + (c + 1) * 512])
        mg_ref[0, :, c * 512:(c + 1) * 512] = jax.nn.sigmoid(acc).astype(BF16)


def _inproj(x, mod, g, wb, ws, bfp, tri, rs, pm):
    B, S, D = x.shape
    tm = IN_TILE
    nb = wb.shape[1]
    const2 = lambda b, i: (0, 0)
    heads = lambda n: pl.BlockSpec((1, n, tm, LANES), lambda b, i: (b, 0, i, 0))
    hshape = lambda n: jax.ShapeDtypeStruct((B, n, S, LANES), BF16)
    return pl.pallas_call(
        _inproj_kernel,
        grid=(B, S // tm),
        in_specs=[
            pl.BlockSpec((1, tm, D), lambda b, i: (b, i, 0)),
            pl.BlockSpec((1, 6, D), lambda b, i: (b, 0, 0)),
            pl.BlockSpec((1, D), const2),
            pl.BlockSpec((D, nb), const2),
            pl.BlockSpec((D, LANES), const2),
            pl.BlockSpec((1, LANES), const2),
            pl.BlockSpec((tm, tm), const2),
            pl.BlockSpec((tm, LANES), lambda b, i: (i, 0)),
            pl.BlockSpec((LANES, N_GROUPS * LANES), const2),
        ],
        out_specs=[
            heads(NSA_HEADS),
            pl.BlockSpec((1, 4, tm, HEAD_DIM), lambda b, i: (b, 0, i, 0)),
            pl.BlockSpec((1, NSA_KV_GROUPS, tm, 2 * LANES), lambda b, i: (b, 0, i, 0)),
            heads(6), heads(FOX_HEADS), heads(FOX_HEADS), heads(FOX_HEADS),
            pl.BlockSpec((1, tm, 2 * D), lambda b, i: (b, i, 0)),
            pl.BlockSpec((1, tm, LANES), lambda b, i: (b, i, 0)),
        ],
        out_shape=[
            hshape(NSA_HEADS),
            jax.ShapeDtypeStruct((B, 4, S, HEAD_DIM), BF16),
            jax.ShapeDtypeStruct((B, NSA_KV_GROUPS, S, 2 * LANES), BF16),
            hshape(6), hshape(FOX_HEADS), hshape(FOX_HEADS), hshape(FOX_HEADS),
            jax.ShapeDtypeStruct((B, S, 2 * D), BF16),
            jax.ShapeDtypeStruct((B, S, LANES), F32),
        ],
        scratch_shapes=[pltpu.VMEM((1, LANES), F32)],
        compiler_params=_cparams(("parallel", "arbitrary")),
        name="inproj",
    )(x, mod, g, wb, ws, bfp, tri, rs, pm)


def _compress_kernel(x_ref, pe_ref, w1_ref, w2_ref, e_ref, o_ref):
    x = x_ref[0, 0].astype(F32)
    x_lo = (x + pe_ref[0, 0]).astype(BF16)
    x_hi = (x + pe_ref[0, 1]).astype(BF16)
    y_lo = _dot(x_lo, w1_ref[0, 0])
    y_hi = _dot(x_hi, w1_ref[0, 1])
    n = y_hi.shape[0]
    hid = y_lo + pltpu.roll(y_hi, n - 1, 0)
    hid = jax.nn.gelu(hid)
    o_ref[0, 0] = (_dot(hid.astype(BF16), w2_ref[0]) + e_ref[0].astype(F32)).astype(BF16)


def _compress(kv_rows, pe, w1, w2, e):
    B = kv_rows.shape[0]
    R, C = kv_rows.shape[2], kv_rows.shape[3]
    return pl.pallas_call(
        _compress_kernel,
        grid=(B, 4),
        in_specs=[
            pl.BlockSpec((1, 1, R, C), lambda b, p: (b, p, 0, 0)),
            pl.BlockSpec((1, 2, 1, C), lambda b, p: (p // 2, 0, 0, 0)),
            pl.BlockSpec((1, 2, C, HEAD_DIM), lambda b, p: (p // 2, 0, 0, 0)),
            pl.BlockSpec((1, HEAD_DIM, LANES), lambda b, p: (p // 2, 0, 0)),
            pl.BlockSpec((1, R, LANES), lambda b, p: (p // 2, 0, 0)),
        ],
        out_specs=pl.BlockSpec((1, 1, R, LANES), lambda b, p: (b, p, 0, 0)),
        out_shape=jax.ShapeDtypeStruct((B, 4, R, LANES), BF16),
        compiler_params=_cparams(("parallel", "parallel")),
        name="compress",
    )(kv_rows, pe, w1, w2, e)


def _gate_rows(sm, g, branch):
    col = lax.broadcasted_iota(jnp.int32, sm.shape, 1)
    parts = []
    for hl in range(NSA_HPG):
        want = 3 * (NSA_HPG * g + hl) + branch
        parts.append(jnp.sum(jnp.where(col == want, sm, 0.0), axis=-1, keepdims=True))
    return jnp.concatenate(parts, axis=0)


def _head_tile(y):
    n = y.shape[0] // NSA_HPG
    lane = lax.broadcasted_iota(jnp.int32, (n, LANES), 1)
    hs = [y[i * n:(i + 1) * n] for i in range(NSA_HPG)]
    pairs = [jnp.where(lane < HEAD_DIM, hs[2 * i], pltpu.roll(hs[2 * i + 1], HEAD_DIM, 1)) for i in range(2)]
    return jnp.concatenate(pairs, axis=1)


def _cmp_kernel(q_ref, kc_ref, vc_ref, sm_ref, ovt_ref, oc_ref, selb_ref, flag_ref, imp_sc):
    g = pl.program_id(1)
    step_q0 = pl.program_id(2) * CMP_SUB * Q_TILE
    last_visible = (step_q0 + CMP_SUB * Q_TILE - CMP_LEN) // CMP_STRIDE
    chunks = last_visible // LANES + 1

    def attend(width):
        for sub in range(CMP_SUB):
            rows = pl.ds(sub * Q_TILE, Q_TILE)
            q = q_ref[0, :, rows, :].reshape(NSA_HPG * Q_TILE, LANES)
            oc, imp = _cmp_attend(q, kc_ref[0, 0, 0:width, :], vc_ref[0, 0, 0:width, :], sm_ref[0, rows, :],
                                  ovt_ref[:, 0:width], g, step_q0 + sub * Q_TILE)
            oc_ref[0, rows, :] = oc
            imp_sc[sub] = imp

    for v in range(1, N_CMP_PAD // LANES + 1):
        pl.when(chunks == v)(functools.partial(attend, v * LANES))

    for sub in range(CMP_SUB):
        selb, flag = _select_blocks(imp_sc[sub], step_q0 + sub * Q_TILE)
        selb_ref[0, 0, pl.ds(sub * Q_TILE, Q_TILE), :] = selb
        flag_ref[0, 0, sub] = flag


def _cmp_attend(q, kc, vc, sm, ovt, g, q0):
    width = kc.shape[0]
    s = _dot_nt(q, kc)
    r = lax.broadcasted_iota(jnp.int32, (NSA_HPG * Q_TILE, 1), 0) % Q_TILE
    n = lax.broadcasted_iota(jnp.int32, (1, width), 1)
    dc = (q0 + r) - (n * CMP_STRIDE + (CMP_LEN - 1))
    mask = (dc >= 0) & (n < N_CMP_PAD - 1)
    l = jnp.where(mask, s, NEG)
    m = jnp.max(l, axis=-1, keepdims=True)
    e = jnp.where(mask, jnp.exp2(l - m), 0.0)
    pc = e / jnp.maximum(jnp.sum(e, axis=-1, keepdims=True), 1e-30)
    oc = _dot(pc.astype(BF16), vc)
    oc = _head_tile(oc * _gate_rows(sm, g, 0))
    ps = pc[0:Q_TILE]
    for i in range(1, NSA_HPG):
        ps = ps + pc[i * Q_TILE:(i + 1) * Q_TILE]
    ps_hi = ps.astype(BF16)
    ps_lo = (ps - ps_hi.astype(F32)).astype(BF16)
    return oc, _dot_nt(ovt, ps_hi) + _dot_nt(ovt, ps_lo)


def _select_blocks(imp, q0):
    j = lax.broadcasted_iota(jnp.int32, imp.shape, 0)
    jf = j.astype(F32)
    t = q0 + lax.broadcasted_iota(jnp.int32, (1, Q_TILE), 1)
    cur = t // SEL_LEN
    forced = (j == 0) | (j == cur) | (j == cur - 1)
    v = jnp.where(j > cur, -FORCE, jnp.where(forced, FORCE, imp))
    sel = jnp.zeros(imp.shape, jnp.bool_)
    for _ in range(N_SEL):
        mx = jnp.max(v, axis=0, keepdims=True)
        idx = jnp.min(jnp.where(v == mx, jf, float(LANES)), axis=0, keepdims=True)
        pick = jf == idx
        sel = sel | pick
        v = jnp.where(pick, PICKED, v)
    live_t = jnp.where(sel & (j <= cur), 1.0, 0.0).astype(BF16)
    eye = (lax.broadcasted_iota(jnp.int32, imp.shape, 0) == lax.broadcasted_iota(jnp.int32, imp.shape, 1))
    live = _dot_nt(eye.astype(BF16), live_t)
    selb = jnp.where(live > 0.5, 0.0, NEG).astype(BF16)
    return selb, jnp.max(live, axis=0, keepdims=True).astype(jnp.int32)


def _cmp_attention(qa, kvc, sm, ov):
    B, H, S, _ = qa.shape
    G = NSA_KV_GROUPS
    nq = S // Q_TILE
    qt = CMP_SUB * Q_TILE
    return pl.pallas_call(
        _cmp_kernel,
        grid=(B, G, nq // CMP_SUB),
        in_specs=[
            pl.BlockSpec((1, NSA_HPG, qt, LANES), lambda b, g, i: (b, g, i, 0)),
            pl.BlockSpec((1, 1, N_CMP_PAD, LANES), lambda b, g, i: (b, g, 0, 0)),
            pl.BlockSpec((1, 1, N_CMP_PAD, LANES), lambda b, g, i: (b, 2 + g, 0, 0)),
            pl.BlockSpec((1, qt, LANES), lambda b, g, i: (b, i, 0)),
            pl.BlockSpec((LANES, N_CMP_PAD), lambda b, g, i: (0, 0)),
        ],
        out_specs=[
            pl.BlockSpec((1, qt, NSA_HPG * HEAD_DIM), lambda b, g, i: (b, i, g)),
            pl.BlockSpec((1, 1, qt, LANES), lambda b, g, i: (b, g, i, 0)),
            pl.BlockSpec((1, 1, CMP_SUB, 1, LANES), lambda b, g, i: (b, g, i, 0, 0)),
        ],
        out_shape=[
            jax.ShapeDtypeStruct((B, S, NSA_W), F32),
            jax.ShapeDtypeStruct((B, G, S, LANES), BF16),
            jax.ShapeDtypeStruct((B, G, nq, 1, LANES), jnp.int32),
        ],
        scratch_shapes=[pltpu.VMEM((CMP_SUB, LANES, Q_TILE), F32)],
        compiler_params=_cparams(("parallel", "parallel", "parallel")),
        name="cmp_attention",
    )(qa, kvc, kvc, sm, ov)


def _online_update(s, v, m_ref, acc_ref):
    m_old = m_ref[...]
    m_new = jnp.maximum(m_old, jnp.max(s, axis=-1, keepdims=True))
    chunks = [s[:, c * LANES:(c + 1) * LANES] - m_new for c in range(s.shape[1] // LANES)]
    p = jnp.exp2(jnp.concatenate(chunks, axis=1))
    acc_ref[...] = jnp.exp2(m_old - m_new) * acc_ref[...] + _dot(p.astype(BF16), v)
    m_ref[...] = m_new


def _normalized(acc):
    return acc / jnp.maximum(acc[:, EXT:EXT + 1], 1e-30)


def _attend_once(s, v):
    m = jnp.broadcast_to(jnp.max(s, axis=-1, keepdims=True), (s.shape[0], LANES))
    chunks = [s[:, c * LANES:(c + 1) * LANES] - m for c in range(s.shape[1] // LANES)]
    p = jnp.exp2(jnp.concatenate(chunks, axis=1))
    return _normalized(_dot(p.astype(BF16), v))


def _selwin_kernel(list_ref, cnt_ref, q_ref, ks_ref, vs_ref, kw_ref, vw_ref, selb_ref, oc_ref, sm_ref,
                   o_ref, m_a, acc_a, m_b, acc_b):
    b = pl.program_id(0)
    g = pl.program_id(1)
    qb = pl.program_id(2)
    nq = pl.num_programs(2)
    rows = NSA_HPG * SW_TILE
    q4 = q_ref[0].reshape(rows, LANES)
    q_aug = jnp.concatenate([q4, jnp.concatenate([selb_ref[0, 0]] * NSA_HPG, axis=0)], axis=1)
    r = lax.broadcasted_iota(jnp.int32, (rows, 1), 0) % SW_TILE
    c = lax.broadcasted_iota(jnp.int32, (1, K_TILE), 1)
    rel = r - c
    diag = qb // (K_TILE // SW_TILE)

    def sel_tile(kt, m_ref, acc_ref, causal=False, bias=None):
        start = pl.multiple_of(kt * K_TILE, K_TILE)
        s = _dot_nt(q_aug, ks_ref[0, 0, pl.ds(start, K_TILE), :])
        if bias is not None:
            s = s + bias
        if causal:
            s = jnp.where(rel + (qb * SW_TILE - kt * K_TILE) >= 0, s, NEG)
        _online_update(s, vs_ref[0, 0, pl.ds(start, K_TILE), :], m_ref, acc_ref)

    for m_ref, acc_ref in ((m_a, acc_a), (m_b, acc_b)):
        m_ref[...] = jnp.full(m_ref.shape, NEG, F32)
        acc_ref[...] = jnp.zeros(acc_ref.shape, F32)
    step = (b * NSA_KV_GROUPS + g) * nq + qb
    count = cnt_ref[step]
    base = step * MAX_TILES

    def body(p, carry):
        second = 2 * p + 1
        sel_tile(list_ref[base + 2 * p], m_a, acc_a)
        sel_tile(list_ref[base + jnp.minimum(second, MAX_TILES - 1)], m_b, acc_b,
                 bias=jnp.where(second < count, 0.0, NEG))
        return carry

    lax.fori_loop(0, (count + 1) // 2, body, 0)
    m_new = jnp.maximum(m_a[...], m_b[...])
    acc_a[...] = jnp.exp2(m_a[...] - m_new) * acc_a[...] + jnp.exp2(m_b[...] - m_new) * acc_b[...]
    m_a[...] = m_new
    sel_tile(diag, m_a, acc_a, causal=True)
    o_sel = _normalized(acc_a[...])

    span = WINDOW + K_TILE
    wstart = pl.multiple_of(jnp.maximum(diag - WINDOW // K_TILE, 0) * K_TILE, K_TILE)
    dist = (qb * SW_TILE + r) - (wstart + lax.broadcasted_iota(jnp.int32, (1, span), 1))
    s = _dot_nt(q4, kw_ref[0, 0, pl.ds(wstart, span), :])
    s = jnp.where((dist >= 0) & (dist < WINDOW), s, NEG)
    o_win = _attend_once(s, vw_ref[0, 0, pl.ds(wstart, span), :])

    sm = sm_ref[0]
    y = _gate_rows(sm, g, 1) * o_sel + _gate_rows(sm, g, 2) * o_win
    o_ref[0] = (oc_ref[0] + _head_tile(y)).astype(BF16)


def _selwin_attention(tile_list, tile_count, qa, ksl, nkv, selb, ocg, sm):
    B, H, S, _ = qa.shape
    G = NSA_KV_GROUPS
    nq = S // SW_TILE
    rows = NSA_HPG * SW_TILE
    kv_spec = lambda piece: pl.BlockSpec((1, 1, S, LANES), lambda b, g, i, tl, tc: (b, piece + g, 0, 0))
    out_tile = pl.BlockSpec((1, SW_TILE, NSA_HPG * HEAD_DIM), lambda b, g, i, tl, tc: (b, i, g))
    grid_spec = pltpu.PrefetchScalarGridSpec(
        num_scalar_prefetch=2,
        grid=(B, G, nq),
        in_specs=[
            pl.BlockSpec((1, NSA_HPG, SW_TILE, LANES), lambda b, g, i, tl, tc: (b, g, i, 0)),
            pl.BlockSpec((1, 1, S, 2 * LANES), lambda b, g, i, tl, tc: (b, g, 0, 0)),
            kv_spec(0), kv_spec(2), kv_spec(4),
            pl.BlockSpec((1, 1, SW_TILE, LANES), lambda b, g, i, tl, tc: (b, g, i, 0)),
            out_tile,
            pl.BlockSpec((1, SW_TILE, LANES), lambda b, g, i, tl, tc: (b, i, 0)),
        ],
        out_specs=out_tile,
        scratch_shapes=[pltpu.VMEM((rows, LANES), F32)] * 4,
    )
    return pl.pallas_call(
        _selwin_kernel,
        grid_spec=grid_spec,
        out_shape=jax.ShapeDtypeStruct((B, S, NSA_W), BF16),
        compiler_params=_cparams(("parallel", "parallel", "arbitrary")),
        name="selwin_attention",
    )(tile_list, tile_count, qa, ksl, nkv, nkv, nkv, selb, ocg, sm)


def _fox_kernel(q_ref, k_ref, v_ref, o_ref, m_sc, acc_sc, *, tq):
    qi = pl.program_id(2)
    m_sc[...] = jnp.full(m_sc.shape, NEG, F32)
    acc_sc[...] = jnp.zeros(acc_sc.shape, F32)

    def tile(kt, width, causal):
        start = pl.multiple_of(kt * tq, tq)
        for hh in range(FOX_HPS):
            s = _dot_nt(q_ref[0, hh], k_ref[0, hh, pl.ds(start, width), :])
            if causal:
                r = lax.broadcasted_iota(jnp.int32, s.shape, 0)
                c = lax.broadcasted_iota(jnp.int32, s.shape, 1)
                s = jnp.where(r >= c, s, NEG)
            _online_update(s, v_ref[0, hh, pl.ds(start, width), :], m_sc.at[hh], acc_sc.at[hh])

    def body(kp, carry):
        tile(2 * kp, 2 * tq, False)
        return carry

    lax.fori_loop(0, qi // 2, body, 0)

    @pl.when(qi % 2 == 1)
    def _():
        tile(qi - 1, tq, False)

    tile(qi, tq, True)
    lane = lax.broadcasted_iota(jnp.int32, (tq, LANES), 1)
    o = [_normalized(acc_sc[hh]) for hh in range(FOX_HPS)]
    for pr in range(FOX_HPS // 2):
        o_ref[0, :, pr * LANES:(pr + 1) * LANES] = jnp.where(
            lane < HEAD_DIM, o[2 * pr], pltpu.roll(o[2 * pr + 1], HEAD_DIM, 1)).astype(BF16)


def _fox_attention(fq, fk, fv, tq=512):
    B, H, S, _ = fq.shape
    hps = FOX_HPS
    return pl.pallas_call(
        functools.partial(_fox_kernel, tq=tq),
        grid=(B, H // hps, S // tq),
        in_specs=[
            pl.BlockSpec((1, hps, tq, LANES), lambda b, h, i: (b, h, i, 0)),
            pl.BlockSpec((1, hps, S, LANES), lambda b, h, i: (b, h, 0, 0)),
            pl.BlockSpec((1, hps, S, LANES), lambda b, h, i: (b, h, 0, 0)),
        ],
        out_specs=pl.BlockSpec((1, tq, hps * HEAD_DIM), lambda b, h, i: (b, i, h)),
        out_shape=jax.ShapeDtypeStruct((B, S, FOX_W), BF16),
        scratch_shapes=[
            pltpu.VMEM((hps, tq, LANES), F32),
            pltpu.VMEM((hps, tq, LANES), F32),
        ],
        compiler_params=_cparams(("parallel", "parallel", "arbitrary")),
        name="fox_attention",
    )(fq, fk, fv)


def _merge_kernel(ya_ref, yb_ref, mg_ref, x_ref, mod_ref, gpost_ref, gpre_ref,
                  wa_ref, wb_ref, wo_ref, wrh_ref, wrl_ref, br_ref, stri_ref,
                  x1_ref, h2_ref, rt_ref, cnt_ref):
    D = D_MODEL

    @pl.when((pl.program_id(0) == 0) & (pl.program_id(1) == 0))
    def _():
        cnt_ref[...] = jnp.zeros(cnt_ref.shape, F32)

    a = _dot(ya_ref[0], wa_ref[...])
    bq = _dot(yb_ref[0], wb_ref[...])
    mg = mg_ref[0]
    u = mg[:, :D].astype(F32) * a + mg[:, D:].astype(F32) * bq
    mixed = _dot(u.astype(BF16), wo_ref[...])
    x1 = x_ref[0] + mod_ref[0, 2:3, :] * _rms(mixed, gpost_ref[...])
    x1_ref[0] = x1
    h2 = _rms(x1, gpre_ref[...]) * (1.0 + mod_ref[0, 4:5, :]) + mod_ref[0, 3:4, :]
    hi = h2.astype(BF16)
    lo = (h2 - hi.astype(F32)).astype(BF16)
    h2_ref[0] = h2
    lg = _dot(hi, wrh_ref[...]) + _dot(lo, wrh_ref[...]) + _dot(hi, wrl_ref[...]) + br_ref[...]

    lane = lax.broadcasted_iota(jnp.int32, lg.shape, 1)
    lanef = lane.astype(F32)
    no_lane = float(LANES)
    is_g = lane < N_EXPERT_GROUPS
    gl = jnp.where(is_g, lg, NEG)
    gmax = jnp.max(gl, axis=-1, keepdims=True)
    pg_top = 1.0 / jnp.sum(jnp.where(is_g, jnp.exp(gl - gmax), 0.0), axis=-1, keepdims=True)
    g_idx = jnp.min(jnp.where(is_g & (gl == gmax), lanef, no_lane), axis=-1, keepdims=True)
    in_grp = ((lane >= N_EXPERT_GROUPS) & (lane < N_EXPERT_GROUPS + N_EXPERTS)
              & (((lane - N_EXPERT_GROUPS) // EXPERTS_PER_GROUP).astype(F32) == g_idx))
    le = jnp.where(in_grp, lg, NEG)
    m1 = jnp.max(le, axis=-1, keepdims=True)
    i1 = jnp.min(jnp.where(in_grp & (le == m1), lanef, no_lane), axis=-1, keepdims=True)
    rest = in_grp & (lanef != i1)
    le2 = jnp.where(rest, lg, NEG)
    m2 = jnp.max(le2, axis=-1, keepdims=True)
    i2 = jnp.min(jnp.where(rest & (le2 == m2), lanef, no_lane), axis=-1, keepdims=True)
    e21 = jnp.exp(m2 - m1)
    w1 = pg_top / (1.0 + e21)
    w2 = w1 * e21
    pick1 = lanef == i1
    pick2 = lanef == i2
    onehot = jnp.where(pick1 | pick2, 1.0, 0.0)
    before = cnt_ref[...] + _dot(stri_ref[...], onehot.astype(BF16))
    rank1 = jnp.sum(jnp.where(pick1, before, 0.0), axis=-1, keepdims=True)
    rank2 = jnp.sum(jnp.where(pick2, before, 0.0), axis=-1, keepdims=True)
    cnt_ref[...] = cnt_ref[...] + jnp.sum(onehot, axis=0, keepdims=True)
    fields = {RT_EXPERT: i1 - N_EXPERT_GROUPS, RT_EXPERT + 1: i2 - N_EXPERT_GROUPS,
              RT_RANK: rank1, RT_RANK + 1: rank2, RT_WEIGHT: w1, RT_WEIGHT + 1: w2}
    rt = jnp.zeros(lg.shape, F32)
    for k, f in fields.items():
        rt = jnp.where(lane == k, f, rt)
    rt_ref[0] = rt


def _merge(ya, yb, mg, x, mod, gpost, gpre, wa, wb, wo, wrh, wrl, br, stri):
    B, S, D = x.shape
    tm = MERGE_TILE
    c2 = lambda b, i: (0, 0)
    row = lambda w: pl.BlockSpec((1, tm, w), lambda b, i: (b, i, 0))
    return pl.pallas_call(
        _merge_kernel,
        grid=(B, S // tm),
        in_specs=[
            row(NSA_W), row(FOX_W), row(2 * D), row(D),
            pl.BlockSpec((1, 6, D), lambda b, i: (b, 0, 0)),
            pl.BlockSpec((1, D), c2), pl.BlockSpec((1, D), c2),
            pl.BlockSpec((NSA_W, D), c2), pl.BlockSpec((FOX_W, D), c2), pl.BlockSpec((D, D), c2),
            pl.BlockSpec((D, LANES), c2), pl.BlockSpec((D, LANES), c2), pl.BlockSpec((1, LANES), c2),
            pl.BlockSpec((tm, tm), c2),
        ],
        out_specs=[row(D), row(D), row(LANES), pl.BlockSpec((1, LANES), c2)],
        out_shape=[
            jax.ShapeDtypeStruct((B, S, D), F32),
            jax.ShapeDtypeStruct((B, S, D), F32),
            jax.ShapeDtypeStruct((B, S, LANES), F32),
            jax.ShapeDtypeStruct((1, LANES), F32),
        ],
        compiler_params=_cparams(("arbitrary", "arbitrary")),
        name="merge",
    )(ya, yb, mg, x, mod, gpost, gpre, wa, wb, wo, wrh, wrl, br, stri)


def _expert_kernel(be_ref, na_ref, tok_ref, h_hbm, wg_ref, wu_ref, wd_ref, o_ref, *scratch):
    i = pl.program_id(0)
    n_active = na_ref[0]
    last_block = pl.num_programs(0) - 1
    ring = GATHER_AHEAD + 1
    bufs = scratch[:ring]
    wg_b, wu_b, wd_b, sem = scratch[ring:]

    def row_copy(blk, r, sl):
        tok = tok_ref[blk * MOE_TILE + r]
        return pltpu.make_async_copy(h_hbm.at[pl.ds(tok, 1)], bufs[sl].at[pl.ds(r, 1)], sem.at[sl])

    def wait_rows(sl):
        pltpu.make_async_copy(h_hbm.at[pl.ds(0, MOE_TILE)], bufs[sl], sem.at[sl]).wait()

    @pl.when(i == 0)
    def _():
        for ahead in range(GATHER_AHEAD):
            def body(r, carry, ahead=ahead):
                row_copy(jnp.minimum(ahead, last_block), r, ahead).start()
                return carry
            lax.fori_loop(0, MOE_TILE, body, 0, unroll=8)

    @pl.when((i == 0) | (be_ref[i] != be_ref[jnp.maximum(i - 1, 0)]))
    def _():
        wg_b[...] = wg_ref[0].astype(BF16)
        wu_b[...] = wu_ref[0].astype(BF16)
        wd_b[...] = wd_ref[0].astype(BF16)

    def step(sl):
        wait_rows(sl)
        nxt = jnp.minimum(i + GATHER_AHEAD, last_block)
        nxt_sl = (sl + GATHER_AHEAD) % ring
        for r in range(MOE_TILE):
            row_copy(nxt, r, nxt_sl).start(priority=r % 2)
        x = bufs[sl][...].astype(BF16)
        gate = _dot(x, wg_b[...])
        up = _dot(x, wu_b[...])
        mid = (gate * jax.nn.sigmoid(gate) * up).astype(BF16)
        o_ref[...] = _dot(mid, wd_b[...])

        @pl.when(i == n_active - 1)
        def _():
            for ahead in range(1, GATHER_AHEAD + 1):
                wait_rows((sl + ahead) % ring)

    for sl in range(ring):
        pl.when((i % ring == sl) & (i < n_active))(functools.partial(step, sl))

    @pl.when(i >= n_active)
    def _():
        o_ref[...] = jnp.zeros(o_ref.shape, o_ref.dtype)


def _experts(block_expert, n_active, buf_tok, h2, wg, wu, wd):
    cap = buf_tok.shape[0]
    D = D_MODEL
    nblk = cap // MOE_TILE
    grid_spec = pltpu.PrefetchScalarGridSpec(
        num_scalar_prefetch=3,
        grid=(nblk,),
        in_specs=[
            pl.BlockSpec(memory_space=pl.ANY),
            pl.BlockSpec((1, D, D_EXPERT), lambda i, be, na, tok: (be[i], 0, 0)),
            pl.BlockSpec((1, D, D_EXPERT), lambda i, be, na, tok: (be[i], 0, 0)),
            pl.BlockSpec((1, D_EXPERT, D), lambda i, be, na, tok: (be[i], 0, 0)),
        ],
        out_specs=pl.BlockSpec((MOE_TILE, D), lambda i, be, na, tok: (i, 0)),
        scratch_shapes=[
            *([pltpu.VMEM((MOE_TILE, D), F32)] * (GATHER_AHEAD + 1)),
            pltpu.VMEM((D, D_EXPERT), BF16),
            pltpu.VMEM((D, D_EXPERT), BF16),
            pltpu.VMEM((D_EXPERT, D), BF16),
            pltpu.SemaphoreType.DMA((GATHER_AHEAD + 1,)),
        ],
    )
    return pl.pallas_call(
        _expert_kernel,
        grid_spec=grid_spec,
        out_shape=jax.ShapeDtypeStruct((cap, D), F32),
        compiler_params=_cparams(("arbitrary",)),
        name="experts",
    )(block_expert, n_active, buf_tok, h2, wg, wu, wd)


def _final_kernel(dest_ref, x1_ref, rt_ref, mod_ref, g_ref, y_hbm, o_ref, *scratch):
    j = pl.program_id(0)
    last_tile = pl.num_programs(0) - 1
    tm = o_ref.shape[0]
    ring = GATHER_AHEAD + 1
    bufs = tuple(scratch[EXPERT_TOP_K * sl:EXPERT_TOP_K * (sl + 1)] for sl in range(ring))
    sem = scratch[-1]

    def row_copy(tile, r, k, sl):
        row = dest_ref[(tile * tm + r) * EXPERT_TOP_K + k]
        return pltpu.make_async_copy(y_hbm.at[pl.ds(row, 1)], bufs[sl][k].at[pl.ds(r, 1)], sem.at[sl])

    def wait_rows(sl):
        for k in range(EXPERT_TOP_K):
            pltpu.make_async_copy(y_hbm.at[pl.ds(0, tm)], bufs[sl][k], sem.at[sl]).wait()

    @pl.when(j == 0)
    def _():
        for ahead in range(GATHER_AHEAD):
            def body(r, carry, ahead=ahead):
                for k in range(EXPERT_TOP_K):
                    row_copy(jnp.minimum(ahead, last_tile), r, k, ahead).start()
                return carry
            lax.fori_loop(0, tm, body, 0, unroll=4)

    def step(sl):
        wait_rows(sl)
        nxt = jnp.minimum(j + GATHER_AHEAD, last_tile)
        nxt_sl = (sl + GATHER_AHEAD) % ring
        for r in range(tm):
            for k in range(EXPERT_TOP_K):
                row_copy(nxt, r, k, nxt_sl).start(priority=k)
        rt = rt_ref[...]
        lane = lax.broadcasted_iota(jnp.int32, rt.shape, 1)
        w0 = jnp.sum(jnp.where(lane == RT_WEIGHT, rt, 0.0), axis=-1, keepdims=True)
        w1 = jnp.sum(jnp.where(lane == RT_WEIGHT + 1, rt, 0.0), axis=-1, keepdims=True)
        y = w0 * bufs[sl][0][...] + w1 * bufs[sl][1][...]
        o_ref[...] = x1_ref[...] + mod_ref[0, 5:6, :] * _rms(y, g_ref[...])

        @pl.when(j == last_tile)
        def _():
            for ahead in range(1, GATHER_AHEAD + 1):
                wait_rows((sl + ahead) % ring)

    for sl in range(ring):
        pl.when(j % ring == sl)(functools.partial(step, sl))


def _final(dest, x1, rt, mod, g, yb, tiles_per_batch):
    T, D = x1.shape
    tm = FINAL_TILE
    grid_spec = pltpu.PrefetchScalarGridSpec(
        num_scalar_prefetch=1,
        grid=(T // tm,),
        in_specs=[
            pl.BlockSpec((tm, D), lambda j, d: (j, 0)),
            pl.BlockSpec((tm, LANES), lambda j, d: (j, 0)),
            pl.BlockSpec((1, 6, D), lambda j, d: (j // tiles_per_batch, 0, 0)),
            pl.BlockSpec((1, D), lambda j, d: (0, 0)),
            pl.BlockSpec(memory_space=pl.ANY),
        ],
        out_specs=pl.BlockSpec((tm, D), lambda j, d: (j, 0)),
        scratch_shapes=([pltpu.VMEM((tm, D), F32)] * (EXPERT_TOP_K * (GATHER_AHEAD + 1))
                        + [pltpu.SemaphoreType.DMA((GATHER_AHEAD + 1,))]),
    )
    return pl.pallas_call(
        _final_kernel,
        grid_spec=grid_spec,
        out_shape=jax.ShapeDtypeStruct((T, D), F32),
        compiler_params=_cparams(("arbitrary",)),
        name="final",
    )(dest, x1, rt, mod, g, yb)


def _overlap_matrix():
    n = np.arange(N_CMP_PAD)[:, None]
    j = np.arange(LANES)[None, :]
    start = n * CMP_STRIDE
    ov = (start < j * SEL_LEN + SEL_LEN) & (start + CMP_LEN - 1 >= j * SEL_LEN) & (n < N_CMP_PAD - 1)
    return jnp.asarray(ov.T.astype(np.float32), dtype=BF16)


def _pad_cols(w, width=LANES):
    return jnp.pad(w, ((0, 0), (0, width - w.shape[1])))


def _dispatch_plan(rt, cnt, T):
    expert = rt[:, RT_EXPERT:RT_EXPERT + EXPERT_TOP_K].astype(jnp.int32)
    rank = rt[:, RT_RANK:RT_RANK + EXPERT_TOP_K].astype(jnp.int32)
    weight = rt[:, RT_WEIGHT:RT_WEIGHT + EXPERT_TOP_K]
    counts = cnt[0, N_EXPERT_GROUPS:N_EXPERT_GROUPS + N_EXPERTS].astype(jnp.int32)
    padded = (counts + MOE_TILE - 1) // MOE_TILE * MOE_TILE
    pad_end = jnp.cumsum(padded)
    pad_start = pad_end - padded
    onehot = expert[:, :, None] == jnp.arange(N_EXPERTS)[None, None, :]
    dest = jnp.sum(jnp.where(onehot, pad_start[None, None, :], 0), axis=-1) + rank
    A = T * EXPERT_TOP_K
    cap = -(-(A + N_EXPERTS * (MOE_TILE - 1)) // MOE_TILE) * MOE_TILE
    nblk = cap // MOE_TILE
    n_active = (pad_end[-1] // MOE_TILE).astype(jnp.int32)
    blk = jnp.arange(nblk) * MOE_TILE
    block_expert = jnp.minimum(jnp.sum(pad_end[None, :] <= blk[:, None], axis=1), N_EXPERTS - 1)
    last = jnp.max(jnp.where(jnp.arange(nblk) < n_active, block_expert, 0))
    block_expert = jnp.where(jnp.arange(nblk) < n_active, block_expert, last).astype(jnp.int32)
    tok = jnp.arange(A, dtype=jnp.int32) // EXPERT_TOP_K
    buf_tok = jnp.zeros((cap,), jnp.int32).at[dest.reshape(A)].set(tok)
    return weight, dest, buf_tok, block_expert, n_active.reshape(1)


def kernel(x, c, w_ada, b_ada, g_pre_mix, g_post_mix, g_pre_ffn, g_post_ffn, w_in, b_forget,
           cmp_pe_k, cmp_w1_k, cmp_w2_k, cmp_pe_v, cmp_w1_v, cmp_w2_v,
           w_o_nsa, w_o_fox, w_out, w_router_group, b_router_group, w_router_expert, b_router_expert,
           w_exp_gate, w_exp_up, w_exp_down):
    B, S, D = x.shape
    T = B * S
    depth = w_ada.shape[0]
    ov = _overlap_matrix()
    tri = jnp.asarray(np.tril(np.ones((IN_TILE, IN_TILE), np.float32)), dtype=BF16)
    stri = jnp.asarray(np.tril(np.ones((MERGE_TILE, MERGE_TILE), np.float32), -1), dtype=BF16)
    row_feat = _row_features(S)
    placement = _placement()
    cmp_ext = _cmp_key_ext()
    for l in range(depth):
        mod = _adaln(c, w_ada[l], b_ada[l].reshape(1, 6 * D)).reshape(B, 6, D)
        w_qa, w_kva, w_gl, w_fox, w_f, w_mg = jnp.split(w_in[l], IN_SPLITS, axis=-1)
        w_big = jnp.concatenate([w_qa, w_kva, w_fox, w_mg], axis=1).astype(BF16)
        w_small = _pad_cols(jnp.concatenate([w_gl, w_f], axis=1)).astype(BF16)
        bf_pad = jnp.pad(b_forget[l], (F_LANE, LANES - F_LANE - FOX_HEADS)).reshape(1, LANES)
        qa, ckv, ksl, nkv, fq, fk, fv, mg, sm = _inproj(
            x, mod, g_pre_mix[l].reshape(1, D), w_big, w_small, bf_pad, tri, row_feat, placement)

        half = CMP_LEN // 2
        pe = jnp.stack([cmp_pe_k[l], cmp_pe_v[l]]).reshape(2, 2, 1, half * HEAD_DIM)
        w1 = jnp.stack([cmp_w1_k[l], cmp_w1_v[l]]).reshape(2, 2, half * HEAD_DIM, HEAD_DIM).astype(BF16)
        w2 = jnp.pad(jnp.stack([cmp_w2_k[l], cmp_w2_v[l]]), ((0, 0), (0, 0), (0, LANES - HEAD_DIM))).astype(BF16)
        kvc = _compress(ckv.reshape(B, 4, S // CMP_STRIDE, CMP_STRIDE * HEAD_DIM), pe, w1, w2, cmp_ext)
        ocg, selb, flags = _cmp_attention(qa, kvc, sm, ov)
        nq = S // SW_TILE
        per_tile = K_TILE // SEL_LEN
        tile_any = jnp.max(flags.reshape(B, NSA_KV_GROUPS, nq, SW_TILE // Q_TILE, MAX_TILES, per_tile), axis=(3, 5))
        tile_id = jnp.arange(MAX_TILES)
        diag = (jnp.arange(nq) // (K_TILE // SW_TILE))[:, None]
        active = (tile_any > 0) & (tile_id < diag)
        slot = jnp.cumsum(active, axis=-1) - 1
        hit = active[..., :, None] & (slot[..., :, None] == tile_id)
        tile_list = jnp.sum(jnp.where(hit, tile_id[:, None], 0), axis=-2).astype(jnp.int32).reshape(-1)
        tile_count = jnp.sum(active, axis=-1).astype(jnp.int32).reshape(-1)
        y_a = _selwin_attention(tile_list, tile_count, qa, ksl, nkv, selb, ocg, sm)

        y_b = _fox_attention(fq, fk, fv)

        w_r = _pad_cols(jnp.concatenate([w_router_group[l], w_router_expert[l]], axis=1))
        w_rh = w_r.astype(BF16)
        w_rl = (w_r - w_rh.astype(F32)).astype(BF16)
        b_r = _pad_cols(jnp.concatenate([b_router_group[l], b_router_expert[l]]).reshape(1, -1))
        x1, h2, rt, cnt = _merge(y_a, y_b, mg, x, mod, g_post_mix[l].reshape(1, D), g_pre_ffn[l].reshape(1, D),
                                 w_o_nsa[l].astype(BF16), w_o_fox[l].astype(BF16), w_out[l].astype(BF16),
                                 w_rh, w_rl, b_r, stri)

        weight, dest, buf_tok, block_expert, n_active = _dispatch_plan(rt.reshape(T, LANES), cnt, T)
        yb = _experts(block_expert, n_active, buf_tok, h2.reshape(T, D), w_exp_gate[l], w_exp_up[l], w_exp_down[l])
        x = _final(dest.reshape(T * EXPERT_TOP_K), x1.reshape(T, D), rt.reshape(T, LANES), mod,
                   g_post_ffn[l].reshape(1, D), yb, S // FINAL_TILE).reshape(B, S, D)
    return x
```

```python
import functools

import ml_dtypes
import numpy as np
import jax
import jax.numpy as jnp
from jax import lax
from jax.experimental import pallas as pl
from jax.experimental.pallas import tpu as pltpu

D_MODEL = 1024
HEAD_DIM = 64
NSA_HEADS = 8
NSA_KV_GROUPS = 2
NSA_HPG = NSA_HEADS // NSA_KV_GROUPS
FOX_HEADS = 8
CMP_LEN = 32
CMP_STRIDE = 16
SEL_LEN = 64
N_SEL = 16
WINDOW = 512
N_EXPERT_GROUPS = 4
EXPERTS_PER_GROUP = 8
N_EXPERTS = N_EXPERT_GROUPS * EXPERTS_PER_GROUP
EXPERT_TOP_K = 2
D_EXPERT = D_MODEL // 2
NORM_EPS = 1e-6
NEG = -1e30
FORCE = 1e9
LOG2E = 1.4426950408889634

NSA_W = NSA_HEADS * HEAD_DIM
NSA_KV_W = NSA_KV_GROUPS * HEAD_DIM
FOX_W = FOX_HEADS * HEAD_DIM
IN_SIZES = (NSA_W, 6 * NSA_KV_W, 3 * NSA_HEADS, 3 * FOX_W, FOX_HEADS, 2 * D_MODEL)
IN_SPLITS = tuple(int(v) for v in np.cumsum(IN_SIZES)[:-1])

LANES = 128
Q_TILE = 128
K_TILE = 256
SW_TILE = 256
N_CMP_PAD = 512
MOE_TILE = 256
IN_TILE = 512
MERGE_TILE = 512
FINAL_TILE = 256
FOX_HPS = 4
CMP_SUB = 8
MAX_TILES = 32
GATHER_AHEAD = 3
V7X_VMEM_BYTES = 64 * 1024 * 1024
VMEM_LIMIT = V7X_VMEM_BYTES - 8 * 1024 * 1024
RT_EXPERT, RT_RANK, RT_WEIGHT = 0, 2, 4
PICKED = -3e38

F_LANE = 3 * NSA_HEADS
U_LANE = 64
ONE_LANE = 88
A_LANE = 89
B_LANE = 90
EXT = HEAD_DIM
G_FQ, G_FK, G_NQ, G_NK, N_GROUPS = 0, 8, 16, 24, 25

F32 = jnp.float32
BF16 = jnp.bfloat16


def _dot(a, b):
    return jnp.dot(a, b, preferred_element_type=F32)


def _dot_nt(a, b):
    return lax.dot_general(a, b, (((1,), (1,)), ((), ())), preferred_element_type=F32)


def _rms(x, g):
    return x * lax.rsqrt(jnp.mean(x * x, axis=-1, keepdims=True) + NORM_EPS) * g


def _cparams(sem):
    return pltpu.CompilerParams(dimension_semantics=sem, vmem_limit_bytes=VMEM_LIMIT)


def _split3(x):
    hi = x.astype(BF16).astype(F32)
    r = x - hi
    mid = r.astype(BF16).astype(F32)
    lo = (r - mid).astype(BF16).astype(F32)
    return hi, mid, lo


def _np_split3(x):
    x = np.asarray(x, np.float32)
    hi = x.astype(ml_dtypes.bfloat16).astype(np.float32)
    r = x - hi
    mid = r.astype(ml_dtypes.bfloat16).astype(np.float32)
    lo = (r - mid).astype(ml_dtypes.bfloat16).astype(np.float32)
    return hi, mid, lo


def _alibi_c():
    slopes = np.exp2(-8.0 * np.arange(1, NSA_HEADS + 1, dtype=np.float32) / NSA_HEADS).astype(np.float32)
    return slopes * np.float32(LOG2E)


def _row_features(S):
    t = np.arange(S, dtype=np.float32)
    c = _alibi_c()
    rs = np.zeros((S, LANES), np.float32)
    for h in range(NSA_HEADS):
        for j, term in enumerate(_np_split3(c[h] * t)):
            rs[:, U_LANE + 8 * j + h] = -term
    rs[:, ONE_LANE] = 1.0
    rs[:, A_LANE] = np.floor(t / LANES)
    rs[:, B_LANE] = t % LANES
    return jnp.asarray(rs, dtype=BF16)


def _placement():
    c = _alibi_c()
    p = np.zeros((LANES, N_GROUPS * LANES), np.float32)
    for h in range(FOX_HEADS):
        q0 = (G_FQ + h) * LANES + EXT
        k0 = (G_FK + h) * LANES + EXT
        for j in range(3):
            p[ONE_LANE, q0 + j] = -1.0
            p[F_LANE + 8 * j + h, q0 + 3 + j] = 1.0
            p[F_LANE + 8 * j + h, k0 + j] = 1.0
            p[ONE_LANE, k0 + 3 + j] = 1.0
    for h in range(NSA_HEADS):
        q0 = (G_NQ + h) * LANES + EXT
        c128 = _np_split3(c[h] * np.float32(LANES))
        c1 = _np_split3(c[h])
        for j in range(3):
            p[U_LANE + 8 * j + h, q0 + j] = 1.0
            p[ONE_LANE, q0 + 3 + j] = c128[j]
            p[ONE_LANE, q0 + 6 + j] = c1[j]
    k0 = G_NK * LANES + EXT
    for j in range(3):
        p[ONE_LANE, k0 + j] = 1.0
        p[A_LANE, k0 + 3 + j] = 1.0
        p[B_LANE, k0 + 6 + j] = 1.0
    return jnp.asarray(p, dtype=BF16)


def _cmp_key_ext():
    pos = np.arange(N_CMP_PAD, dtype=np.float32) * CMP_STRIDE + (CMP_LEN - 1)
    e = np.zeros((2, N_CMP_PAD, LANES), np.float32)
    for j in range(3):
        e[0, :, EXT + j] = 1.0
        e[0, :, EXT + 3 + j] = np.floor(pos / LANES)
        e[0, :, EXT + 6 + j] = pos % LANES
    return jnp.asarray(e, dtype=BF16)


def _adaln_kernel(c_ref, w_ref, b_ref, o_ref):
    c = c_ref[...]
    act = (c * jax.nn.sigmoid(c)).astype(BF16)
    o_ref[...] = _dot(act, w_ref[...].astype(BF16)) + b_ref[...]


def _adaln(c, w, b):
    B, D = c.shape
    n = w.shape[1]
    return pl.pallas_call(
        _adaln_kernel,
        grid=(n // D,),
        in_specs=[
            pl.BlockSpec((B, D), lambda j: (0, 0)),
            pl.BlockSpec((D, D), lambda j: (0, j)),
            pl.BlockSpec((1, D), lambda j: (0, j)),
        ],
        out_specs=pl.BlockSpec((B, D), lambda j: (0, j)),
        out_shape=jax.ShapeDtypeStruct((B, n), F32),
        compiler_params=_cparams(("parallel",)),
        name="adaln",
    )(c, w, b)


def _inproj_kernel(x_ref, mod_ref, g_ref, wb_ref, ws_ref, bf_ref, tri_ref, rs_ref, p_ref,
                   qa_ref, ckv_ref, ksl_ref, nkv_ref, fq_ref, fk_ref, fv_ref, mg_ref, sm_ref, carry_sc):
    i = pl.program_id(1)
    tm = x_ref.shape[1]
    x = x_ref[0]
    h = _rms(x, g_ref[...]) * (1.0 + mod_ref[0, 1:2, :]) + mod_ref[0, 0:1, :]
    hb = h.astype(BF16)
    lane = lax.broadcasted_iota(jnp.int32, (tm, LANES), 1)
    lower = lane < HEAD_DIM
    ones_col = (lane == EXT).astype(F32)

    z = _dot(hb, ws_ref[...]) + bf_ref[...]
    logsig = jnp.minimum(z, 0.0) - jnp.log1p(jnp.exp(-jnp.abs(z)))
    sm_ref[0] = jnp.where(lane < F_LANE, jax.nn.sigmoid(z), logsig)

    @pl.when(i == 0)
    def _():
        carry_sc[...] = jnp.zeros(carry_sc.shape, F32)

    is_f = (lane >= F_LANE) & (lane < F_LANE + FOX_HEADS)
    l_hi, l_mid, l_lo = _split3(jnp.where(is_f, logsig, 0.0))
    tri = tri_ref[...]
    cum = carry_sc[...] + _dot(tri, l_hi.astype(BF16)) + _dot(tri, l_mid.astype(BF16)) + _dot(tri, l_lo.astype(BF16))
    carry_sc[...] = cum[tm - 1:tm, :]
    f_hi, f_mid, f_lo = _split3(cum * LOG2E)
    feat = (f_hi + pltpu.roll(f_mid, 8, 1) + pltpu.roll(f_lo, 16, 1) + rs_ref[...].astype(F32)).astype(BF16)

    ext_pairs = {}

    def ext(group):
        first = group - group % 2
        if first not in ext_pairs:
            width = min(2, N_GROUPS - first) * LANES
            ext_pairs[first] = _dot(feat, p_ref[:, first * LANES:first * LANES + width])
        off = (group - first) * LANES
        return ext_pairs[first][:, off:off + LANES]

    def piece(acc, idx, extra):
        pair = acc[:, (idx // 2) * LANES:(idx // 2 + 1) * LANES]
        if idx % 2:
            pair = pltpu.roll(pair, HEAD_DIM, 1)
        return jnp.where(lower, pair, extra).astype(BF16)

    qscale = (HEAD_DIM ** -0.5) * LOG2E
    acc = _dot(hb, wb_ref[:, 0:NSA_W]) * qscale
    for hd in range(NSA_HEADS):
        qa_ref[0, hd] = piece(acc, hd, ext(G_NQ + hd))
    off = NSA_W
    acc = _dot(hb, wb_ref[:, off:off + 6 * NSA_KV_W])
    for pc in range(4):
        ckv_ref[0, pc] = acc[:, pc * HEAD_DIM:(pc + 1) * HEAD_DIM].astype(BF16)
    ext_k = ext(G_NK)
    t = i * tm + lax.broadcasted_iota(jnp.int32, (tm, LANES), 0)
    block_onehot = (lane == t // SEL_LEN).astype(BF16)
    for g in range(NSA_KV_GROUPS):
        ksl_ref[0, g, :, 0:LANES] = piece(acc, 4 + g, ext_k)
        ksl_ref[0, g, :, LANES:2 * LANES] = block_onehot
        nkv_ref[0, g] = piece(acc, 6 + g, ones_col)
        nkv_ref[0, 2 + g] = piece(acc, 8 + g, ext_k)
        nkv_ref[0, 4 + g] = piece(acc, 10 + g, ones_col)
    off += 6 * NSA_KV_W
    acc = _dot(hb, wb_ref[:, off:off + FOX_W]) * qscale
    for hd in range(FOX_HEADS):
        fq_ref[0, hd] = piece(acc, hd, ext(G_FQ + hd))
    off += FOX_W
    acc = _dot(hb, wb_ref[:, off:off + FOX_W])
    for hd in range(FOX_HEADS):
        fk_ref[0, hd] = piece(acc, hd, ext(G_FK + hd))
    off += FOX_W
    acc = _dot(hb, wb_ref[:, off:off + FOX_W])
    for hd in range(FOX_HEADS):
        fv_ref[0, hd] = piece(acc, hd, ones_col)
    off += FOX_W
    for c in range(4):
        acc = _dot(hb, wb_ref[:, off + c * 512: off + (c + 1) * 512])
        mg_ref[0, :, c * 512:(c + 1) * 512] = jax.nn.sigmoid(acc).astype(BF16)


def _inproj(x, mod, g, wb, ws, bfp, tri, rs, pm):
    B, S, D = x.shape
    tm = IN_TILE
    nb = wb.shape[1]
    const2 = lambda b, i: (0, 0)
    heads = lambda n: pl.BlockSpec((1, n, tm, LANES), lambda b, i: (b, 0, i, 0))
    hshape = lambda n: jax.ShapeDtypeStruct((B, n, S, LANES), BF16)
    return pl.pallas_call(
        _inproj_kernel,
        grid=(B, S // tm),
        in_specs=[
            pl.BlockSpec((1, tm, D), lambda b, i: (b, i, 0)),
            pl.BlockSpec((1, 6, D), lambda b, i: (b, 0, 0)),
            pl.BlockSpec((1, D), const2),
            pl.BlockSpec((D, nb), const2),
            pl.BlockSpec((D, LANES), const2),
            pl.BlockSpec((1, LANES), const2),
            pl.BlockSpec((tm, tm), const2),
            pl.BlockSpec((tm, LANES), lambda b, i: (i, 0)),
            pl.BlockSpec((LANES, N_GROUPS * LANES), const2),
        ],
        out_specs=[
            heads(NSA_HEADS),
            pl.BlockSpec((1, 4, tm, HEAD_DIM), lambda b, i: (b, 0, i, 0)),
            pl.BlockSpec((1, NSA_KV_GROUPS, tm, 2 * LANES), lambda b, i: (b, 0, i, 0)),
            heads(6), heads(FOX_HEADS), heads(FOX_HEADS), heads(FOX_HEADS),
            pl.BlockSpec((1, tm, 2 * D), lambda b, i: (b, i, 0)),
            pl.BlockSpec((1, tm, LANES), lambda b, i: (b, i, 0)),
        ],
        out_shape=[
            hshape(NSA_HEADS),
            jax.ShapeDtypeStruct((B, 4, S, HEAD_DIM), BF16),
            jax.ShapeDtypeStruct((B, NSA_KV_GROUPS, S, 2 * LANES), BF16),
            hshape(6), hshape(FOX_HEADS), hshape(FOX_HEADS), hshape(FOX_HEADS),
            jax.ShapeDtypeStruct((B, S, 2 * D), BF16),
            jax.ShapeDtypeStruct((B, S, LANES), F32),
        ],
        scratch_shapes=[pltpu.VMEM((1, LANES), F32)],
        compiler_params=_cparams(("parallel", "arbitrary")),
        name="inproj",
    )(x, mod, g, wb, ws, bfp, tri, rs, pm)


def _compress_kernel(x_ref, pe_ref, w1_ref, w2_ref, e_ref, o_ref):
    x = x_ref[0, 0].astype(F32)
    x_lo = (x + pe_ref[0, 0]).astype(BF16)
    x_hi = (x + pe_ref[0, 1]).astype(BF16)
    y_lo = _dot(x_lo, w1_ref[0, 0])
    y_hi = _dot(x_hi, w1_ref[0, 1])
    n = y_hi.shape[0]
    hid = y_lo + pltpu.roll(y_hi, n - 1, 0)
    hid = jax.nn.gelu(hid)
    o_ref[0, 0] = (_dot(hid.astype(BF16), w2_ref[0]) + e_ref[0].astype(F32)).astype(BF16)


def _compress(kv_rows, pe, w1, w2, e):
    B = kv_rows.shape[0]
    R, C = kv_rows.shape[2], kv_rows.shape[3]
    return pl.pallas_call(
        _compress_kernel,
        grid=(B, 4),
        in_specs=[
            pl.BlockSpec((1, 1, R, C), lambda b, p: (b, p, 0, 0)),
            pl.BlockSpec((1, 2, 1, C), lambda b, p: (p // 2, 0, 0, 0)),
            pl.BlockSpec((1, 2, C, HEAD_DIM), lambda b, p: (p // 2, 0, 0, 0)),
            pl.BlockSpec((1, HEAD_DIM, LANES), lambda b, p: (p // 2, 0, 0)),
            pl.BlockSpec((1, R, LANES), lambda b, p: (p // 2, 0, 0)),
        ],
        out_specs=pl.BlockSpec((1, 1, R, LANES), lambda b, p: (b, p, 0, 0)),
        out_shape=jax.ShapeDtypeStruct((B, 4, R, LANES), BF16),
        compiler_params=_cparams(("parallel", "parallel")),
        name="compress",
    )(kv_rows, pe, w1, w2, e)


def _gate_rows(sm, g, branch):
    col = lax.broadcasted_iota(jnp.int32, sm.shape, 1)
    parts = []
    for hl in range(NSA_HPG):
        want = 3 * (NSA_HPG * g + hl) + branch
        parts.append(jnp.sum(jnp.where(col == want, sm, 0.0), axis=-1, keepdims=True))
    return jnp.concatenate(parts, axis=0)


def _head_tile(y):
    n = y.shape[0] // NSA_HPG
    lane = lax.broadcasted_iota(jnp.int32, (n, LANES), 1)
    hs = [y[i * n:(i + 1) * n] for i in range(NSA_HPG)]
    pairs = [jnp.where(lane < HEAD_DIM, hs[2 * i], pltpu.roll(hs[2 * i + 1], HEAD_DIM, 1)) for i in range(2)]
    return jnp.concatenate(pairs, axis=1)


def _cmp_kernel(q_ref, kc_ref, vc_ref, sm_ref, ovt_ref, oc_ref, selb_ref, flag_ref, imp_sc):
    g = pl.program_id(1)
    step_q0 = pl.program_id(2) * CMP_SUB * Q_TILE
    last_visible = (step_q0 + CMP_SUB * Q_TILE - CMP_LEN) // CMP_STRIDE
    chunks = last_visible // LANES + 1

    def attend(width):
        for sub in range(CMP_SUB):
            rows = pl.ds(sub * Q_TILE, Q_TILE)
            q = q_ref[0, :, rows, :].reshape(NSA_HPG * Q_TILE, LANES)
            oc, imp = _cmp_attend(q, kc_ref[0, 0, 0:width, :], vc_ref[0, 0, 0:width, :], sm_ref[0, rows, :],
                                  ovt_ref[:, 0:width], g, step_q0 + sub * Q_TILE)
            oc_ref[0, rows, :] = oc
            imp_sc[sub] = imp

    for v in range(1, N_CMP_PAD // LANES + 1):
        pl.when(chunks == v)(functools.partial(attend, v * LANES))

    for sub in range(CMP_SUB):
        selb, flag = _select_blocks(imp_sc[sub], step_q0 + sub * Q_TILE)
        selb_ref[0, 0, pl.ds(sub * Q_TILE, Q_TILE), :] = selb
        flag_ref[0, 0, sub] = flag


def _cmp_attend(q, kc, vc, sm, ovt, g, q0):
    width = kc.shape[0]
    s = _dot_nt(q, kc)
    r = lax.broadcasted_iota(jnp.int32, (NSA_HPG * Q_TILE, 1), 0) % Q_TILE
    n = lax.broadcasted_iota(jnp.int32, (1, width), 1)
    dc = (q0 + r) - (n * CMP_STRIDE + (CMP_LEN - 1))
    mask = (dc >= 0) & (n < N_CMP_PAD - 1)
    l = jnp.where(mask, s, NEG)
    m = jnp.max(l, axis=-1, keepdims=True)
    e = jnp.where(mask, jnp.exp2(l - m), 0.0)
    pc = e / jnp.maximum(jnp.sum(e, axis=-1, keepdims=True), 1e-30)
    oc = _dot(pc.astype(BF16), vc)
    oc = _head_tile(oc * _gate_rows(sm, g, 0))
    ps = pc[0:Q_TILE]
    for i in range(1, NSA_HPG):
        ps = ps + pc[i * Q_TILE:(i + 1) * Q_TILE]
    ps_hi = ps.astype(BF16)
    ps_lo = (ps - ps_hi.astype(F32)).astype(BF16)
    return oc, _dot_nt(ovt, ps_hi) + _dot_nt(ovt, ps_lo)


def _select_blocks(imp, q0):
    j = lax.broadcasted_iota(jnp.int32, imp.shape, 0)
    jf = j.astype(F32)
    t = q0 + lax.broadcasted_iota(jnp.int32, (1, Q_TILE), 1)
    cur = t // SEL_LEN
    forced = (j == 0) | (j == cur) | (j == cur - 1)
    v = jnp.where(j > cur, -FORCE, jnp.where(forced, FORCE, imp))
    sel = jnp.zeros(imp.shape, jnp.bool_)
    for _ in range(N_SEL):
        mx = jnp.max(v, axis=0, keepdims=True)
        idx = jnp.min(jnp.where(v == mx, jf, float(LANES)), axis=0, keepdims=True)
        pick = jf == idx
        sel = sel | pick
        v = jnp.where(pick, PICKED, v)
    live_t = jnp.where(sel & (j <= cur), 1.0, 0.0).astype(BF16)
    eye = (lax.broadcasted_iota(jnp.int32, imp.shape, 0) == lax.broadcasted_iota(jnp.int32, imp.shape, 1))
    live = _dot_nt(eye.astype(BF16), live_t)
    selb = jnp.where(live > 0.5, 0.0, NEG).astype(BF16)
    return selb, jnp.max(live, axis=0, keepdims=True).astype(jnp.int32)


def _cmp_attention(qa, kvc, sm, ov):
    B, H, S, _ = qa.shape
    G = NSA_KV_GROUPS
    nq = S // Q_TILE
    qt = CMP_SUB * Q_TILE
    return pl.pallas_call(
        _cmp_kernel,
        grid=(B, G, nq // CMP_SUB),
        in_specs=[
            pl.BlockSpec((1, NSA_HPG, qt, LANES), lambda b, g, i: (b, g, i, 0)),
            pl.BlockSpec((1, 1, N_CMP_PAD, LANES), lambda b, g, i: (b, g, 0, 0)),
            pl.BlockSpec((1, 1, N_CMP_PAD, LANES), lambda b, g, i: (b, 2 + g, 0, 0)),
            pl.BlockSpec((1, qt, LANES), lambda b, g, i: (b, i, 0)),
            pl.BlockSpec((LANES, N_CMP_PAD), lambda b, g, i: (0, 0)),
        ],
        out_specs=[
            pl.BlockSpec((1, qt, NSA_HPG * HEAD_DIM), lambda b, g, i: (b, i, g)),
            pl.BlockSpec((1, 1, qt, LANES), lambda b, g, i: (b, g, i, 0)),
            pl.BlockSpec((1, 1, CMP_SUB, 1, LANES), lambda b, g, i: (b, g, i, 0, 0)),
        ],
        out_shape=[
            jax.ShapeDtypeStruct((B, S, NSA_W), F32),
            jax.ShapeDtypeStruct((B, G, S, LANES), BF16),
            jax.ShapeDtypeStruct((B, G, nq, 1, LANES), jnp.int32),
        ],
        scratch_shapes=[pltpu.VMEM((CMP_SUB, LANES, Q_TILE), F32)],
        compiler_params=_cparams(("parallel", "parallel", "parallel")),
        name="cmp_attention",
    )(qa, kvc, kvc, sm, ov)


def _online_update(s, v, m_ref, acc_ref):
    m_old = m_ref[...]
    m_new = jnp.maximum(m_old, jnp.max(s, axis=-1, keepdims=True))
    chunks = [s[:, c * LANES:(c + 1) * LANES] - m_new for c in range(s.shape[1] // LANES)]
    p = jnp.exp2(jnp.concatenate(chunks, axis=1))
    acc_ref[...] = jnp.exp2(m_old - m_new) * acc_ref[...] + _dot(p.astype(BF16), v)
    m_ref[...] = m_new


def _normalized(acc):
    return acc / jnp.maximum(acc[:, EXT:EXT + 1], 1e-30)


def _attend_once(s, v):
    m = jnp.broadcast_to(jnp.max(s, axis=-1, keepdims=True), (s.shape[0], LANES))
    chunks = [s[:, c * LANES:(c + 1) * LANES] - m for c in range(s.shape[1] // LANES)]
    p = jnp.exp2(jnp.concatenate(chunks, axis=1))
    return _normalized(_dot(p.astype(BF16), v))


def _selwin_kernel(list_ref, cnt_ref, q_ref, ks_ref, vs_ref, kw_ref, vw_ref, selb_ref, oc_ref, sm_ref,
                   o_ref, m_a, acc_a, m_b, acc_b):
    b = pl.program_id(0)
    g = pl.program_id(1)
    qb = pl.program_id(2)
    nq = pl.num_programs(2)
    rows = NSA_HPG * SW_TILE
    q4 = q_ref[0].reshape(rows, LANES)
    q_aug = jnp.concatenate([q4, jnp.concatenate([selb_ref[0, 0]] * NSA_HPG, axis=0)], axis=1)
    r = lax.broadcasted_iota(jnp.int32, (rows, 1), 0) % SW_TILE
    c = lax.broadcasted_iota(jnp.int32, (1, K_TILE), 1)
    rel = r - c
    diag = qb * (SW_TILE // K_TILE)

    def sel_tile(kt, m_ref, acc_ref, causal=False, bias=None):
        start = pl.multiple_of(kt * K_TILE, K_TILE)
        s = _dot_nt(q_aug, ks_ref[0, 0, pl.ds(start, K_TILE), :])
        if bias is not None:
            s = s + bias
        if causal:
            s = jnp.where(rel + (qb * SW_TILE - kt * K_TILE) >= 0, s, NEG)
        _online_update(s, vs_ref[0, 0, pl.ds(start, K_TILE), :], m_ref, acc_ref)

    for m_ref, acc_ref in ((m_a, acc_a), (m_b, acc_b)):
        m_ref[...] = jnp.full(m_ref.shape, NEG, F32)
        acc_ref[...] = jnp.zeros(acc_ref.shape, F32)
    step = (b * NSA_KV_GROUPS + g) * nq + qb
    count = cnt_ref[step]
    base = step * MAX_TILES

    def body(p, carry):
        second = 2 * p + 1
        sel_tile(list_ref[base + 2 * p], m_a, acc_a)
        sel_tile(list_ref[base + jnp.minimum(second, MAX_TILES - 1)], m_b, acc_b,
                 bias=jnp.where(second < count, 0.0, NEG))
        return carry

    lax.fori_loop(0, (count + 1) // 2, body, 0)
    m_new = jnp.maximum(m_a[...], m_b[...])
    acc_a[...] = jnp.exp2(m_a[...] - m_new) * acc_a[...] + jnp.exp2(m_b[...] - m_new) * acc_b[...]
    m_a[...] = m_new
    for d in range(SW_TILE // K_TILE):
        sel_tile(diag + d, m_a, acc_a, causal=True)
    o_sel = _normalized(acc_a[...])

    span = WINDOW + SW_TILE
    wstart = pl.multiple_of(jnp.maximum(qb * SW_TILE - WINDOW, 0), K_TILE)
    dist = (qb * SW_TILE + r) - (wstart + lax.broadcasted_iota(jnp.int32, (1, span), 1))
    s = _dot_nt(q4, kw_ref[0, 0, pl.ds(wstart, span), :])
    s = jnp.where((dist >= 0) & (dist < WINDOW), s, NEG)
    o_win = _attend_once(s, vw_ref[0, 0, pl.ds(wstart, span), :])

    sm = sm_ref[0]
    y = _gate_rows(sm, g, 1) * o_sel + _gate_rows(sm, g, 2) * o_win
    o_ref[0] = (oc_ref[0] + _head_tile(y)).astype(BF16)


def _selwin_attention(tile_list, tile_count, qa, ksl, nkv, selb, ocg, sm):
    B, H, S, _ = qa.shape
    G = NSA_KV_GROUPS
    nq = S // SW_TILE
    rows = NSA_HPG * SW_TILE
    kv_spec = lambda piece: pl.BlockSpec((1, 1, S, LANES), lambda b, g, i, tl, tc: (b, piece + g, 0, 0))
    out_tile = pl.BlockSpec((1, SW_TILE, NSA_HPG * HEAD_DIM), lambda b, g, i, tl, tc: (b, i, g))
    grid_spec = pltpu.PrefetchScalarGridSpec(
        num_scalar_prefetch=2,
        grid=(B, G, nq),
        in_specs=[
            pl.BlockSpec((1, NSA_HPG, SW_TILE, LANES), lambda b, g, i, tl, tc: (b, g, i, 0)),
            pl.BlockSpec((1, 1, S, 2 * LANES), lambda b, g, i, tl, tc: (b, g, 0, 0)),
            kv_spec(0), kv_spec(2), kv_spec(4),
            pl.BlockSpec((1, 1, SW_TILE, LANES), lambda b, g, i, tl, tc: (b, g, i, 0)),
            out_tile,
            pl.BlockSpec((1, SW_TILE, LANES), lambda b, g, i, tl, tc: (b, i, 0)),
        ],
        out_specs=out_tile,
        scratch_shapes=[pltpu.VMEM((rows, LANES), F32)] * 4,
    )
    return pl.pallas_call(
        _selwin_kernel,
        grid_spec=grid_spec,
        out_shape=jax.ShapeDtypeStruct((B, S, NSA_W), BF16),
        compiler_params=_cparams(("parallel", "parallel", "arbitrary")),
        name="selwin_attention",
    )(tile_list, tile_count, qa, ksl, nkv, nkv, nkv, selb, ocg, sm)


def _fox_kernel(q_ref, k_ref, v_ref, o_ref, m_sc, acc_sc, *, tq):
    qi = pl.program_id(2)
    m_sc[...] = jnp.full(m_sc.shape, NEG, F32)
    acc_sc[...] = jnp.zeros(acc_sc.shape, F32)

    def tile(kt, width, causal):
        start = pl.multiple_of(kt * tq, tq)
        for hh in range(FOX_HPS):
            s = _dot_nt(q_ref[0, hh], k_ref[0, hh, pl.ds(start, width), :])
            if causal:
                r = lax.broadcasted_iota(jnp.int32, s.shape, 0)
                c = lax.broadcasted_iota(jnp.int32, s.shape, 1)
                s = jnp.where(r >= c, s, NEG)
            _online_update(s, v_ref[0, hh, pl.ds(start, width), :], m_sc.at[hh], acc_sc.at[hh])

    def body(kp, carry):
        tile(2 * kp, 2 * tq, False)
        return carry

    lax.fori_loop(0, qi // 2, body, 0)

    @pl.when(qi % 2 == 1)
    def _():
        tile(qi - 1, tq, False)

    tile(qi, tq, True)
    lane = lax.broadcasted_iota(jnp.int32, (tq, LANES), 1)
    o = [_normalized(acc_sc[hh]) for hh in range(FOX_HPS)]
    for pr in range(FOX_HPS // 2):
        o_ref[0, :, pr * LANES:(pr + 1) * LANES] = jnp.where(
            lane < HEAD_DIM, o[2 * pr], pltpu.roll(o[2 * pr + 1], HEAD_DIM, 1)).astype(BF16)


def _fox_attention(fq, fk, fv, tq=512):
    B, H, S, _ = fq.shape
    hps = FOX_HPS
    return pl.pallas_call(
        functools.partial(_fox_kernel, tq=tq),
        grid=(B, H // hps, S // tq),
        in_specs=[
            pl.BlockSpec((1, hps, tq, LANES), lambda b, h, i: (b, h, i, 0)),
            pl.BlockSpec((1, hps, S, LANES), lambda b, h, i: (b, h, 0, 0)),
            pl.BlockSpec((1, hps, S, LANES), lambda b, h, i: (b, h, 0, 0)),
        ],
        out_specs=pl.BlockSpec((1, tq, hps * HEAD_DIM), lambda b, h, i: (b, i, h)),
        out_shape=jax.ShapeDtypeStruct((B, S, FOX_W), BF16),
        scratch_shapes=[
            pltpu.VMEM((hps, tq, LANES), F32),
            pltpu.VMEM((hps, tq, LANES), F32),
        ],
        compiler_params=_cparams(("parallel", "parallel", "arbitrary")),
        name="fox_attention",
    )(fq, fk, fv)


def _merge_kernel(ya_ref, yb_ref, mg_ref, x_ref, mod_ref, gpost_ref, gpre_ref,
                  wa_ref, wb_ref, wo_ref, wrh_ref, wrl_ref, br_ref, stri_ref,
                  x1_ref, h2_ref, rt_ref, cnt_ref):
    D = D_MODEL

    @pl.when((pl.program_id(0) == 0) & (pl.program_id(1) == 0))
    def _():
        cnt_ref[...] = jnp.zeros(cnt_ref.shape, F32)

    a = _dot(ya_ref[0], wa_ref[...])
    bq = _dot(yb_ref[0], wb_ref[...])
    mg = mg_ref[0]
    u = mg[:, :D].astype(F32) * a + mg[:, D:].astype(F32) * bq
    mixed = _dot(u.astype(BF16), wo_ref[...])
    x1 = x_ref[0] + mod_ref[0, 2:3, :] * _rms(mixed, gpost_ref[...])
    x1_ref[0] = x1
    h2 = _rms(x1, gpre_ref[...]) * (1.0 + mod_ref[0, 4:5, :]) + mod_ref[0, 3:4, :]
    hi = h2.astype(BF16)
    lo = (h2 - hi.astype(F32)).astype(BF16)
    h2_ref[0] = h2
    lg = _dot(hi, wrh_ref[...]) + _dot(lo, wrh_ref[...]) + _dot(hi, wrl_ref[...]) + br_ref[...]

    lane = lax.broadcasted_iota(jnp.int32, lg.shape, 1)
    lanef = lane.astype(F32)
    no_lane = float(LANES)
    is_g = lane < N_EXPERT_GROUPS
    gl = jnp.where(is_g, lg, NEG)
    gmax = jnp.max(gl, axis=-1, keepdims=True)
    pg_top = 1.0 / jnp.sum(jnp.where(is_g, jnp.exp(gl - gmax), 0.0), axis=-1, keepdims=True)
    g_idx = jnp.min(jnp.where(is_g & (gl == gmax), lanef, no_lane), axis=-1, keepdims=True)
    in_grp = ((lane >= N_EXPERT_GROUPS) & (lane < N_EXPERT_GROUPS + N_EXPERTS)
              & (((lane - N_EXPERT_GROUPS) // EXPERTS_PER_GROUP).astype(F32) == g_idx))
    le = jnp.where(in_grp, lg, NEG)
    m1 = jnp.max(le, axis=-1, keepdims=True)
    i1 = jnp.min(jnp.where(in_grp & (le == m1), lanef, no_lane), axis=-1, keepdims=True)
    rest = in_grp & (lanef != i1)
    le2 = jnp.where(rest, lg, NEG)
    m2 = jnp.max(le2, axis=-1, keepdims=True)
    i2 = jnp.min(jnp.where(rest & (le2 == m2), lanef, no_lane), axis=-1, keepdims=True)
    e21 = jnp.exp(m2 - m1)
    w1 = pg_top / (1.0 + e21)
    w2 = w1 * e21
    pick1 = lanef == i1
    pick2 = lanef == i2
    onehot = jnp.where(pick1 | pick2, 1.0, 0.0)
    before = cnt_ref[...] + _dot(stri_ref[...], onehot.astype(BF16))
    rank1 = jnp.sum(jnp.where(pick1, before, 0.0), axis=-1, keepdims=True)
    rank2 = jnp.sum(jnp.where(pick2, before, 0.0), axis=-1, keepdims=True)
    cnt_ref[...] = cnt_ref[...] + jnp.sum(onehot, axis=0, keepdims=True)
    fields = {RT_EXPERT: i1 - N_EXPERT_GROUPS, RT_EXPERT + 1: i2 - N_EXPERT_GROUPS,
              RT_RANK: rank1, RT_RANK + 1: rank2, RT_WEIGHT: w1, RT_WEIGHT + 1: w2}
    rt = jnp.zeros(lg.shape, F32)
    for k, f in fields.items():
        rt = jnp.where(lane == k, f, rt)
    rt_ref[0] = rt


def _merge(ya, yb, mg, x, mod, gpost, gpre, wa, wb, wo, wrh, wrl, br, stri):
    B, S, D = x.shape
    tm = MERGE_TILE
    c2 = lambda b, i: (0, 0)
    row = lambda w: pl.BlockSpec((1, tm, w), lambda b, i: (b, i, 0))
    return pl.pallas_call(
        _merge_kernel,
        grid=(B, S // tm),
        in_specs=[
            row(NSA_W), row(FOX_W), row(2 * D), row(D),
            pl.BlockSpec((1, 6, D), lambda b, i: (b, 0, 0)),
            pl.BlockSpec((1, D), c2), pl.BlockSpec((1, D), c2),
            pl.BlockSpec((NSA_W, D), c2), pl.BlockSpec((FOX_W, D), c2), pl.BlockSpec((D, D), c2),
            pl.BlockSpec((D, LANES), c2), pl.BlockSpec((D, LANES), c2), pl.BlockSpec((1, LANES), c2),
            pl.BlockSpec((tm, tm), c2),
        ],
        out_specs=[row(D), row(D), row(LANES), pl.BlockSpec((1, LANES), c2)],
        out_shape=[
            jax.ShapeDtypeStruct((B, S, D), F32),
            jax.ShapeDtypeStruct((B, S, D), F32),
            jax.ShapeDtypeStruct((B, S, LANES), F32),
            jax.ShapeDtypeStruct((1, LANES), F32),
        ],
        compiler_params=_cparams(("arbitrary", "arbitrary")),
        name="merge",
    )(ya, yb, mg, x, mod, gpost, gpre, wa, wb, wo, wrh, wrl, br, stri)


def _expert_kernel(be_ref, na_ref, tok_ref, h_hbm, wg_ref, wu_ref, wd_ref, o_ref, *scratch):
    i = pl.program_id(0)
    n_active = na_ref[0]
    last_block = pl.num_programs(0) - 1
    ring = GATHER_AHEAD + 1
    bufs = scratch[:ring]
    wg_b, wu_b, wd_b, sem = scratch[ring:]

    def row_copy(blk, r, sl):
        tok = tok_ref[blk * MOE_TILE + r]
        return pltpu.make_async_copy(h_hbm.at[pl.ds(tok, 1)], bufs[sl].at[pl.ds(r, 1)], sem.at[sl])

    def wait_rows(sl):
        pltpu.make_async_copy(h_hbm.at[pl.ds(0, MOE_TILE)], bufs[sl], sem.at[sl]).wait()

    @pl.when(i == 0)
    def _():
        for ahead in range(GATHER_AHEAD):
            def body(r, carry, ahead=ahead):
                row_copy(jnp.minimum(ahead, last_block), r, ahead).start()
                return carry
            lax.fori_loop(0, MOE_TILE, body, 0, unroll=8)

    @pl.when((i == 0) | (be_ref[i] != be_ref[jnp.maximum(i - 1, 0)]))
    def _():
        wg_b[...] = wg_ref[0].astype(BF16)
        wu_b[...] = wu_ref[0].astype(BF16)
        wd_b[...] = wd_ref[0].astype(BF16)

    def step(sl):
        wait_rows(sl)
        nxt = jnp.minimum(i + GATHER_AHEAD, last_block)
        nxt_sl = (sl + GATHER_AHEAD) % ring
        for r in range(MOE_TILE):
            row_copy(nxt, r, nxt_sl).start(priority=r % 2)
        x = bufs[sl][...].astype(BF16)
        gate = _dot(x, wg_b[...])
        up = _dot(x, wu_b[...])
        mid = (gate * jax.nn.sigmoid(gate) * up).astype(BF16)
        o_ref[...] = _dot(mid, wd_b[...])

        @pl.when(i == n_active - 1)
        def _():
            for ahead in range(1, GATHER_AHEAD + 1):
                wait_rows((sl + ahead) % ring)

    for sl in range(ring):
        pl.when((i % ring == sl) & (i < n_active))(functools.partial(step, sl))

    @pl.when(i >= n_active)
    def _():
        o_ref[...] = jnp.zeros(o_ref.shape, o_ref.dtype)


def _experts(block_expert, n_active, buf_tok, h2, wg, wu, wd):
    cap = buf_tok.shape[0]
    D = D_MODEL
    nblk = cap // MOE_TILE
    grid_spec = pltpu.PrefetchScalarGridSpec(
        num_scalar_prefetch=3,
        grid=(nblk,),
        in_specs=[
            pl.BlockSpec(memory_space=pl.ANY),
            pl.BlockSpec((1, D, D_EXPERT), lambda i, be, na, tok: (be[i], 0, 0)),
            pl.BlockSpec((1, D, D_EXPERT), lambda i, be, na, tok: (be[i], 0, 0)),
            pl.BlockSpec((1, D_EXPERT, D), lambda i, be, na, tok: (be[i], 0, 0)),
        ],
        out_specs=pl.BlockSpec((MOE_TILE, D), lambda i, be, na, tok: (i, 0)),
        scratch_shapes=[
            *([pltpu.VMEM((MOE_TILE, D), F32)] * (GATHER_AHEAD + 1)),
            pltpu.VMEM((D, D_EXPERT), BF16),
            pltpu.VMEM((D, D_EXPERT), BF16),
            pltpu.VMEM((D_EXPERT, D), BF16),
            pltpu.SemaphoreType.DMA((GATHER_AHEAD + 1,)),
        ],
    )
    return pl.pallas_call(
        _expert_kernel,
        grid_spec=grid_spec,
        out_shape=jax.ShapeDtypeStruct((cap, D), F32),
        compiler_params=_cparams(("arbitrary",)),
        name="experts",
    )(block_expert, n_active, buf_tok, h2, wg, wu, wd)


def _final_kernel(dest_ref, x1_ref, rt_ref, mod_ref, g_ref, y_hbm, o_ref, *scratch):
    j = pl.program_id(0)
    last_tile = pl.num_programs(0) - 1
    tm = o_ref.shape[0]
    ring = GATHER_AHEAD + 1
    bufs = tuple(scratch[EXPERT_TOP_K * sl:EXPERT_TOP_K * (sl + 1)] for sl in range(ring))
    sem = scratch[-1]

    def row_copy(tile, r, k, sl):
        row = dest_ref[(tile * tm + r) * EXPERT_TOP_K + k]
        return pltpu.make_async_copy(y_hbm.at[pl.ds(row, 1)], bufs[sl][k].at[pl.ds(r, 1)], sem.at[sl])

    def wait_rows(sl):
        for k in range(EXPERT_TOP_K):
            pltpu.make_async_copy(y_hbm.at[pl.ds(0, tm)], bufs[sl][k], sem.at[sl]).wait()

    @pl.when(j == 0)
    def _():
        for ahead in range(GATHER_AHEAD):
            def body(r, carry, ahead=ahead):
                for k in range(EXPERT_TOP_K):
                    row_copy(jnp.minimum(ahead, last_tile), r, k, ahead).start()
                return carry
            lax.fori_loop(0, tm, body, 0, unroll=4)

    def step(sl):
        wait_rows(sl)
        nxt = jnp.minimum(j + GATHER_AHEAD, last_tile)
        nxt_sl = (sl + GATHER_AHEAD) % ring
        for r in range(tm):
            for k in range(EXPERT_TOP_K):
                row_copy(nxt, r, k, nxt_sl).start(priority=k)
        rt = rt_ref[...]
        lane = lax.broadcasted_iota(jnp.int32, rt.shape, 1)
        w0 = jnp.sum(jnp.where(lane == RT_WEIGHT, rt, 0.0), axis=-1, keepdims=True)
        w1 = jnp.sum(jnp.where(lane == RT_WEIGHT + 1, rt, 0.0), axis=-1, keepdims=True)
        y = w0 * bufs[sl][0][...] + w1 * bufs[sl][1][...]
        o_ref[...] = x1_ref[...] + mod_ref[0, 5:6, :] * _rms(y, g_ref[...])

        @pl.when(j == last_tile)
        def _():
            for ahead in range(1, GATHER_AHEAD + 1):
                wait_rows((sl + ahead) % ring)

    for sl in range(ring):
        pl.when(j % ring == sl)(functools.partial(step, sl))


def _final(dest, x1, rt, mod, g, yb, tiles_per_batch):
    T, D = x1.shape
    tm = FINAL_TILE
    grid_spec = pltpu.PrefetchScalarGridSpec(
        num_scalar_prefetch=1,
        grid=(T // tm,),
        in_specs=[
            pl.BlockSpec((tm, D), lambda j, d: (j, 0)),
            pl.BlockSpec((tm, LANES), lambda j, d: (j, 0)),
            pl.BlockSpec((1, 6, D), lambda j, d: (j // tiles_per_batch, 0, 0)),
            pl.BlockSpec((1, D), lambda j, d: (0, 0)),
            pl.BlockSpec(memory_space=pl.ANY),
        ],
        out_specs=pl.BlockSpec((tm, D), lambda j, d: (j, 0)),
        scratch_shapes=([pltpu.VMEM((tm, D), F32)] * (EXPERT_TOP_K * (GATHER_AHEAD + 1))
                        + [pltpu.SemaphoreType.DMA((GATHER_AHEAD + 1,))]),
    )
    return pl.pallas_call(
        _final_kernel,
        grid_spec=grid_spec,
        out_shape=jax.ShapeDtypeStruct((T, D), F32),
        compiler_params=_cparams(("arbitrary",)),
        name="final",
    )(dest, x1, rt, mod, g, yb)


def _overlap_matrix():
    n = np.arange(N_CMP_PAD)[:, None]
    j = np.arange(LANES)[None, :]
    start = n * CMP_STRIDE
    ov = (start < j * SEL_LEN + SEL_LEN) & (start + CMP_LEN - 1 >= j * SEL_LEN) & (n < N_CMP_PAD - 1)
    return jnp.asarray(ov.T.astype(np.float32), dtype=BF16)


def _pad_cols(w, width=LANES):
    return jnp.pad(w, ((0, 0), (0, width - w.shape[1])))


def _dispatch_plan(rt, cnt, T):
    expert = rt[:, RT_EXPERT:RT_EXPERT + EXPERT_TOP_K].astype(jnp.int32)
    rank = rt[:, RT_RANK:RT_RANK + EXPERT_TOP_K].astype(jnp.int32)
    weight = rt[:, RT_WEIGHT:RT_WEIGHT + EXPERT_TOP_K]
    counts = cnt[0, N_EXPERT_GROUPS:N_EXPERT_GROUPS + N_EXPERTS].astype(jnp.int32)
    padded = (counts + MOE_TILE - 1) // MOE_TILE * MOE_TILE
    pad_end = jnp.cumsum(padded)
    pad_start = pad_end - padded
    onehot = expert[:, :, None] == jnp.arange(N_EXPERTS)[None, None, :]
    dest = jnp.sum(jnp.where(onehot, pad_start[None, None, :], 0), axis=-1) + rank
    A = T * EXPERT_TOP_K
    cap = -(-(A + N_EXPERTS * (MOE_TILE - 1)) // MOE_TILE) * MOE_TILE
    nblk = cap // MOE_TILE
    n_active = (pad_end[-1] // MOE_TILE).astype(jnp.int32)
    blk = jnp.arange(nblk) * MOE_TILE
    block_expert = jnp.minimum(jnp.sum(pad_end[None, :] <= blk[:, None], axis=1), N_EXPERTS - 1)
    last = jnp.max(jnp.where(jnp.arange(nblk) < n_active, block_expert, 0))
    block_expert = jnp.where(jnp.arange(nblk) < n_active, block_expert, last).astype(jnp.int32)
    tok = jnp.arange(A, dtype=jnp.int32) // EXPERT_TOP_K
    buf_tok = jnp.zeros((cap,), jnp.int32).at[dest.reshape(A)].set(tok)
    return weight, dest, buf_tok, block_expert, n_active.reshape(1)


def kernel(x, c, w_ada, b_ada, g_pre_mix, g_post_mix, g_pre_ffn, g_post_ffn, w_in, b_forget,
           cmp_pe_k, cmp_w1_k, cmp_w2_k, cmp_pe_v, cmp_w1_v, cmp_w2_v,
           w_o_nsa, w_o_fox, w_out, w_router_group, b_router_group, w_router_expert, b_router_expert,
           w_exp_gate, w_exp_up, w_exp_down):
    B, S, D = x.shape
    T = B * S
    depth = w_ada.shape[0]
    ov = _overlap_matrix()
    tri = jnp.asarray(np.tril(np.ones((IN_TILE, IN_TILE), np.float32)), dtype=BF16)
    stri = jnp.asarray(np.tril(np.ones((MERGE_TILE, MERGE_TILE), np.float32), -1), dtype=BF16)
    row_feat = _row_features(S)
    placement = _placement()
    cmp_ext = _cmp_key_ext()
    for l in range(depth):
        mod = _adaln(c, w_ada[l], b_ada[l].reshape(1, 6 * D)).reshape(B, 6, D)
        w_qa, w_kva, w_gl, w_fox, w_f, w_mg = jnp.split(w_in[l], IN_SPLITS, axis=-1)
        w_big = jnp.concatenate([w_qa, w_kva, w_fox, w_mg], axis=1).astype(BF16)
        w_small = _pad_cols(jnp.concatenate([w_gl, w_f], axis=1)).astype(BF16)
        bf_pad = jnp.pad(b_forget[l], (F_LANE, LANES - F_LANE - FOX_HEADS)).reshape(1, LANES)
        qa, ckv, ksl, nkv, fq, fk, fv, mg, sm = _inproj(
            x, mod, g_pre_mix[l].reshape(1, D), w_big, w_small, bf_pad, tri, row_feat, placement)

        half = CMP_LEN // 2
        pe = jnp.stack([cmp_pe_k[l], cmp_pe_v[l]]).reshape(2, 2, 1, half * HEAD_DIM)
        w1 = jnp.stack([cmp_w1_k[l], cmp_w1_v[l]]).reshape(2, 2, half * HEAD_DIM, HEAD_DIM).astype(BF16)
        w2 = jnp.pad(jnp.stack([cmp_w2_k[l], cmp_w2_v[l]]), ((0, 0), (0, 0), (0, LANES - HEAD_DIM))).astype(BF16)
        kvc = _compress(ckv.reshape(B, 4, S // CMP_STRIDE, CMP_STRIDE * HEAD_DIM), pe, w1, w2, cmp_ext)
        ocg, selb, flags = _cmp_attention(qa, kvc, sm, ov)
        nq = S // SW_TILE
        per_tile = K_TILE // SEL_LEN
        tile_any = jnp.max(flags.reshape(B, NSA_KV_GROUPS, nq, SW_TILE // Q_TILE, MAX_TILES, per_tile), axis=(3, 5))
        tile_id = jnp.arange(MAX_TILES)
        diag = (jnp.arange(nq) * (SW_TILE // K_TILE))[:, None]
        active = (tile_any > 0) & (tile_id < diag)
        slot = jnp.cumsum(active, axis=-1) - 1
        hit = active[..., :, None] & (slot[..., :, None] == tile_id)
        tile_list = jnp.sum(jnp.where(hit, tile_id[:, None], 0), axis=-2).astype(jnp.int32).reshape(-1)
        tile_count = jnp.sum(active, axis=-1).astype(jnp.int32).reshape(-1)
        y_a = _selwin_attention(tile_list, tile_count, qa, ksl, nkv, selb, ocg, sm)

        y_b = _fox_attention(fq, fk, fv)

        w_r = _pad_cols(jnp.concatenate([w_router_group[l], w_router_expert[l]], axis=1))
        w_rh = w_r.astype(BF16)
        w_rl = (w_r - w_rh.astype(F32)).astype(BF16)
        b_r = _pad_cols(jnp.concatenate([b_router_group[l], b_router_expert[l]]).reshape(1, -1))
        x1, h2, rt, cnt = _merge(y_a, y_b, mg, x, mod, g_post_mix[l].reshape(1, D), g_pre_ffn[l].reshape(1, D),
                                 w_o_nsa[l].astype(BF16), w_o_fox[l].astype(BF16), w_out[l].astype(BF16),
                                 w_rh, w_rl, b_r, stri)

        weight, dest, buf_tok, block_expert, n_active = _dispatch_plan(rt.reshape(T, LANES), cnt, T)
        yb = _experts(block_expert, n_active, buf_tok, h2.reshape(T, D), w_exp_gate[l], w_exp_up[l], w_exp_down[l])
        x = _final(dest.reshape(T * EXPERT_TOP_K), x1.reshape(T, D), rt.reshape(T, LANES), mod,
                   g_post_ffn[l].reshape(1, D), yb, S // FINAL_TILE).reshape(B, S, D)
    return x
```

```python
import functools

import ml_dtypes
import numpy as np
import jax
import jax.numpy as jnp
from jax import lax
from jax.experimental import pallas as pl
from jax.experimental.pallas import tpu as pltpu

D_MODEL = 1024
HEAD_DIM = 64
NSA_HEADS = 8
NSA_KV_GROUPS = 2
NSA_HPG = NSA_HEADS // NSA_KV_GROUPS
FOX_HEADS = 8
CMP_LEN = 32
CMP_STRIDE = 16
SEL_LEN = 64
N_SEL = 16
WINDOW = 512
N_EXPERT_GROUPS = 4
EXPERTS_PER_GROUP = 8
N_EXPERTS = N_EXPERT_GROUPS * EXPERTS_PER_GROUP
EXPERT_TOP_K = 2
D_EXPERT = D_MODEL // 2
NORM_EPS = 1e-6
NEG = -1e30
FORCE = 1e9
LOG2E = 1.4426950408889634

NSA_W = NSA_HEADS * HEAD_DIM
NSA_KV_W = NSA_KV_GROUPS * HEAD_DIM
FOX_W = FOX_HEADS * HEAD_DIM
IN_SIZES = (NSA_W, 6 * NSA_KV_W, 3 * NSA_HEADS, 3 * FOX_W, FOX_HEADS, 2 * D_MODEL)
IN_SPLITS = tuple(int(v) for v in np.cumsum(IN_SIZES)[:-1])

LANES = 128
Q_TILE = 128
K_TILE = 256
SW_TILE = 256
N_CMP_PAD = 512
MOE_TILE = 256
IN_TILE = 512
MERGE_TILE = 512
FINAL_TILE = 256
FOX_HPS = 4
CMP_SUB = 16
MAX_TILES = 32
GATHER_AHEAD = 3
V7X_VMEM_BYTES = 64 * 1024 * 1024
VMEM_LIMIT = V7X_VMEM_BYTES - 8 * 1024 * 1024
RT_EXPERT, RT_RANK, RT_WEIGHT = 0, 2, 4
PICKED = -3e38

F_LANE = 3 * NSA_HEADS
U_LANE = 64
ONE_LANE = 88
A_LANE = 89
B_LANE = 90
EXT = HEAD_DIM
G_FQ, G_FK, G_NQ, G_NK, N_GROUPS = 0, 8, 16, 24, 25

F32 = jnp.float32
BF16 = jnp.bfloat16


def _dot(a, b):
    return jnp.dot(a, b, preferred_element_type=F32)


def _dot_nt(a, b):
    return lax.dot_general(a, b, (((1,), (1,)), ((), ())), preferred_element_type=F32)


def _rms(x, g):
    return x * lax.rsqrt(jnp.mean(x * x, axis=-1, keepdims=True) + NORM_EPS) * g


def _cparams(sem):
    return pltpu.CompilerParams(dimension_semantics=sem, vmem_limit_bytes=VMEM_LIMIT)


def _split3(x):
    hi = x.astype(BF16).astype(F32)
    r = x - hi
    mid = r.astype(BF16).astype(F32)
    lo = (r - mid).astype(BF16).astype(F32)
    return hi, mid, lo


def _np_split3(x):
    x = np.asarray(x, np.float32)
    hi = x.astype(ml_dtypes.bfloat16).astype(np.float32)
    r = x - hi
    mid = r.astype(ml_dtypes.bfloat16).astype(np.float32)
    lo = (r - mid).astype(ml_dtypes.bfloat16).astype(np.float32)
    return hi, mid, lo


def _alibi_c():
    slopes = np.exp2(-8.0 * np.arange(1, NSA_HEADS + 1, dtype=np.float32) / NSA_HEADS).astype(np.float32)
    return slopes * np.float32(LOG2E)


def _row_features(S):
    t = np.arange(S, dtype=np.float32)
    c = _alibi_c()
    rs = np.zeros((S, LANES), np.float32)
    for h in range(NSA_HEADS):
        for j, term in enumerate(_np_split3(c[h] * t)):
            rs[:, U_LANE + 8 * j + h] = -term
    rs[:, ONE_LANE] = 1.0
    rs[:, A_LANE] = np.floor(t / LANES)
    rs[:, B_LANE] = t % LANES
    return jnp.asarray(rs, dtype=BF16)


def _placement():
    c = _alibi_c()
    p = np.zeros((LANES, N_GROUPS * LANES), np.float32)
    for h in range(FOX_HEADS):
        q0 = (G_FQ + h) * LANES + EXT
        k0 = (G_FK + h) * LANES + EXT
        for j in range(3):
            p[ONE_LANE, q0 + j] = -1.0
            p[F_LANE + 8 * j + h, q0 + 3 + j] = 1.0
            p[F_LANE + 8 * j + h, k0 + j] = 1.0
            p[ONE_LANE, k0 + 3 + j] = 1.0
    for h in range(NSA_HEADS):
        q0 = (G_NQ + h) * LANES + EXT
        c128 = _np_split3(c[h] * np.float32(LANES))
        c1 = _np_split3(c[h])
        for j in range(3):
            p[U_LANE + 8 * j + h, q0 + j] = 1.0
            p[ONE_LANE, q0 + 3 + j] = c128[j]
            p[ONE_LANE, q0 + 6 + j] = c1[j]
    k0 = G_NK * LANES + EXT
    for j in range(3):
        p[ONE_LANE, k0 + j] = 1.0
        p[A_LANE, k0 + 3 + j] = 1.0
        p[B_LANE, k0 + 6 + j] = 1.0
    return jnp.asarray(p, dtype=BF16)


def _cmp_key_ext():
    pos = np.arange(N_CMP_PAD, dtype=np.float32) * CMP_STRIDE + (CMP_LEN - 1)
    e = np.zeros((2, N_CMP_PAD, LANES), np.float32)
    for j in range(3):
        e[0, :, EXT + j] = 1.0
        e[0, :, EXT + 3 + j] = np.floor(pos / LANES)
        e[0, :, EXT + 6 + j] = pos % LANES
    return jnp.asarray(e, dtype=BF16)


def _adaln_kernel(c_ref, w_ref, b_ref, o_ref):
    c = c_ref[...]
    act = (c * jax.nn.sigmoid(c)).astype(BF16)
    o_ref[...] = _dot(act, w_ref[...].astype(BF16)) + b_ref[...]


def _adaln(c, w, b):
    B, D = c.shape
    n = w.shape[1]
    return pl.pallas_call(
        _adaln_kernel,
        grid=(n // D,),
        in_specs=[
            pl.BlockSpec((B, D), lambda j: (0, 0)),
            pl.BlockSpec((D, D), lambda j: (0, j)),
            pl.BlockSpec((1, D), lambda j: (0, j)),
        ],
        out_specs=pl.BlockSpec((B, D), lambda j: (0, j)),
        out_shape=jax.ShapeDtypeStruct((B, n), F32),
        compiler_params=_cparams(("parallel",)),
        name="adaln",
    )(c, w, b)


def _inproj_kernel(x_ref, mod_ref, g_ref, wb_ref, ws_ref, bf_ref, tri_ref, rs_ref, p_ref,
                   qa_ref, ckv_ref, ksl_ref, nkv_ref, fq_ref, fk_ref, fv_ref, mg_ref, sm_ref, carry_sc):
    i = pl.program_id(1)
    tm = x_ref.shape[1]
    x = x_ref[0]
    h = _rms(x, g_ref[...]) * (1.0 + mod_ref[0, 1:2, :]) + mod_ref[0, 0:1, :]
    hb = h.astype(BF16)
    lane = lax.broadcasted_iota(jnp.int32, (tm, LANES), 1)
    lower = lane < HEAD_DIM
    ones_col = (lane == EXT).astype(F32)

    z = _dot(hb, ws_ref[...]) + bf_ref[...]
    logsig = jnp.minimum(z, 0.0) - jnp.log1p(jnp.exp(-jnp.abs(z)))
    sm_ref[0] = jnp.where(lane < F_LANE, jax.nn.sigmoid(z), logsig)

    @pl.when(i == 0)
    def _():
        carry_sc[...] = jnp.zeros(carry_sc.shape, F32)

    is_f = (lane >= F_LANE) & (lane < F_LANE + FOX_HEADS)
    l_hi, l_mid, l_lo = _split3(jnp.where(is_f, logsig, 0.0))
    tri = tri_ref[...]
    cum = carry_sc[...] + _dot(tri, l_hi.astype(BF16)) + _dot(tri, l_mid.astype(BF16)) + _dot(tri, l_lo.astype(BF16))
    carry_sc[...] = cum[tm - 1:tm, :]
    f_hi, f_mid, f_lo = _split3(cum * LOG2E)
    feat = (f_hi + pltpu.roll(f_mid, 8, 1) + pltpu.roll(f_lo, 16, 1) + rs_ref[...].astype(F32)).astype(BF16)

    ext_pairs = {}

    def ext(group):
        first = group - group % 2
        if first not in ext_pairs:
            width = min(2, N_GROUPS - first) * LANES
            ext_pairs[first] = _dot(feat, p_ref[:, first * LANES:first * LANES + width])
        off = (group - first) * LANES
        return ext_pairs[first][:, off:off + LANES]

    def piece(acc, idx, extra):
        pair = acc[:, (idx // 2) * LANES:(idx // 2 + 1) * LANES]
        if idx % 2:
            pair = pltpu.roll(pair, HEAD_DIM, 1)
        return jnp.where(lower, pair, extra).astype(BF16)

    qscale = (HEAD_DIM ** -0.5) * LOG2E
    acc = _dot(hb, wb_ref[:, 0:NSA_W]) * qscale
    for hd in range(NSA_HEADS):
        qa_ref[0, hd] = piece(acc, hd, ext(G_NQ + hd))
    off = NSA_W
    acc = _dot(hb, wb_ref[:, off:off + 6 * NSA_KV_W])
    for pc in range(4):
        ckv_ref[0, pc] = acc[:, pc * HEAD_DIM:(pc + 1) * HEAD_DIM].astype(BF16)
    ext_k = ext(G_NK)
    t = i * tm + lax.broadcasted_iota(jnp.int32, (tm, LANES), 0)
    block_onehot = (lane == t // SEL_LEN).astype(BF16)
    for g in range(NSA_KV_GROUPS):
        ksl_ref[0, g, :, 0:LANES] = piece(acc, 4 + g, ext_k)
        ksl_ref[0, g, :, LANES:2 * LANES] = block_onehot
        nkv_ref[0, g] = piece(acc, 6 + g, ones_col)
        nkv_ref[0, 2 + g] = piece(acc, 8 + g, ext_k)
        nkv_ref[0, 4 + g] = piece(acc, 10 + g, ones_col)
    off += 6 * NSA_KV_W
    acc = _dot(hb, wb_ref[:, off:off + FOX_W]) * qscale
    for hd in range(FOX_HEADS):
        fq_ref[0, hd] = piece(acc, hd, ext(G_FQ + hd))
    off += FOX_W
    acc = _dot(hb, wb_ref[:, off:off + FOX_W])
    for hd in range(FOX_HEADS):
        fk_ref[0, hd] = piece(acc, hd, ext(G_FK + hd))
    off += FOX_W
    acc = _dot(hb, wb_ref[:, off:off + FOX_W])
    for hd in range(FOX_HEADS):
        fv_ref[0, hd] = piece(acc, hd, ones_col)
    off += FOX_W
    for c in range(4):
        acc = _dot(hb, wb_ref[:, off + c * 512: off + (c + 1) * 512])
        mg_ref[0, :, c * 512:(c + 1) * 512] = jax.nn.sigmoid(acc).astype(BF16)


def _inproj(x, mod, g, wb, ws, bfp, tri, rs, pm):
    B, S, D = x.shape
    tm = IN_TILE
    nb = wb.shape[1]
    const2 = lambda b, i: (0, 0)
    heads = lambda n: pl.BlockSpec((1, n, tm, LANES), lambda b, i: (b, 0, i, 0))
    hshape = lambda n: jax.ShapeDtypeStruct((B, n, S, LANES), BF16)
    return pl.pallas_call(
        _inproj_kernel,
        grid=(B, S // tm),
        in_specs=[
            pl.BlockSpec((1, tm, D), lambda b, i: (b, i, 0)),
            pl.BlockSpec((1, 6, D), lambda b, i: (b, 0, 0)),
            pl.BlockSpec((1, D), const2),
            pl.BlockSpec((D, nb), const2),
            pl.BlockSpec((D, LANES), const2),
            pl.BlockSpec((1, LANES), const2),
            pl.BlockSpec((tm, tm), const2),
            pl.BlockSpec((tm, LANES), lambda b, i: (i, 0)),
            pl.BlockSpec((LANES, N_GROUPS * LANES), const2),
        ],
        out_specs=[
            heads(NSA_HEADS),
            pl.BlockSpec((1, 4, tm, HEAD_DIM), lambda b, i: (b, 0, i, 0)),
            pl.BlockSpec((1, NSA_KV_GROUPS, tm, 2 * LANES), lambda b, i: (b, 0, i, 0)),
            heads(6), heads(FOX_HEADS), heads(FOX_HEADS), heads(FOX_HEADS),
            pl.BlockSpec((1, tm, 2 * D), lambda b, i: (b, i, 0)),
            pl.BlockSpec((1, tm, LANES), lambda b, i: (b, i, 0)),
        ],
        out_shape=[
            hshape(NSA_HEADS),
            jax.ShapeDtypeStruct((B, 4, S, HEAD_DIM), BF16),
            jax.ShapeDtypeStruct((B, NSA_KV_GROUPS, S, 2 * LANES), BF16),
            hshape(6), hshape(FOX_HEADS), hshape(FOX_HEADS), hshape(FOX_HEADS),
            jax.ShapeDtypeStruct((B, S, 2 * D), BF16),
            jax.ShapeDtypeStruct((B, S, LANES), F32),
        ],
        scratch_shapes=[pltpu.VMEM((1, LANES), F32)],
        compiler_params=_cparams(("parallel", "arbitrary")),
        name="inproj",
    )(x, mod, g, wb, ws, bfp, tri, rs, pm)


def _compress_kernel(x_ref, pe_ref, w1_ref, w2_ref, e_ref, o_ref):
    x = x_ref[0, 0].astype(F32)
    x_lo = (x + pe_ref[0, 0]).astype(BF16)
    x_hi = (x + pe_ref[0, 1]).astype(BF16)
    y_lo = _dot(x_lo, w1_ref[0, 0])
    y_hi = _dot(x_hi, w1_ref[0, 1])
    n = y_hi.shape[0]
    hid = y_lo + pltpu.roll(y_hi, n - 1, 0)
    hid = jax.nn.gelu(hid)
    o_ref[0, 0] = (_dot(hid.astype(BF16), w2_ref[0]) + e_ref[0].astype(F32)).astype(BF16)


def _compress(kv_rows, pe, w1, w2, e):
    B = kv_rows.shape[0]
    R, C = kv_rows.shape[2], kv_rows.shape[3]
    return pl.pallas_call(
        _compress_kernel,
        grid=(B, 4),
        in_specs=[
            pl.BlockSpec((1, 1, R, C), lambda b, p: (b, p, 0, 0)),
            pl.BlockSpec((1, 2, 1, C), lambda b, p: (p // 2, 0, 0, 0)),
            pl.BlockSpec((1, 2, C, HEAD_DIM), lambda b, p: (p // 2, 0, 0, 0)),
            pl.BlockSpec((1, HEAD_DIM, LANES), lambda b, p: (p // 2, 0, 0)),
            pl.BlockSpec((1, R, LANES), lambda b, p: (p // 2, 0, 0)),
        ],
        out_specs=pl.BlockSpec((1, 1, R, LANES), lambda b, p: (b, p, 0, 0)),
        out_shape=jax.ShapeDtypeStruct((B, 4, R, LANES), BF16),
        compiler_params=_cparams(("parallel", "parallel")),
        name="compress",
    )(kv_rows, pe, w1, w2, e)


def _gate_rows(sm, g, branch):
    col = lax.broadcasted_iota(jnp.int32, sm.shape, 1)
    parts = []
    for hl in range(NSA_HPG):
        want = 3 * (NSA_HPG * g + hl) + branch
        parts.append(jnp.sum(jnp.where(col == want, sm, 0.0), axis=-1, keepdims=True))
    return jnp.concatenate(parts, axis=0)


def _head_tile(y):
    n = y.shape[0] // NSA_HPG
    lane = lax.broadcasted_iota(jnp.int32, (n, LANES), 1)
    hs = [y[i * n:(i + 1) * n] for i in range(NSA_HPG)]
    pairs = [jnp.where(lane < HEAD_DIM, hs[2 * i], pltpu.roll(hs[2 * i + 1], HEAD_DIM, 1)) for i in range(2)]
    return jnp.concatenate(pairs, axis=1)


def _cmp_kernel(q_ref, kc_ref, vc_ref, sm_ref, ovt_ref, oc_ref, selb_ref, flag_ref, imp_sc):
    g = pl.program_id(1)
    step_q0 = pl.program_id(2) * CMP_SUB * Q_TILE
    last_visible = (step_q0 + CMP_SUB * Q_TILE - CMP_LEN) // CMP_STRIDE
    chunks = last_visible // LANES + 1

    def attend(width):
        for sub in range(CMP_SUB):
            rows = pl.ds(sub * Q_TILE, Q_TILE)
            q = q_ref[0, :, rows, :].reshape(NSA_HPG * Q_TILE, LANES)
            oc, imp = _cmp_attend(q, kc_ref[0, 0, 0:width, :], vc_ref[0, 0, 0:width, :], sm_ref[0, rows, :],
                                  ovt_ref[:, 0:width], g, step_q0 + sub * Q_TILE)
            oc_ref[0, rows, :] = oc
            imp_sc[sub] = imp

    for v in range(1, N_CMP_PAD // LANES + 1):
        pl.when(chunks == v)(functools.partial(attend, v * LANES))

    for sub in range(CMP_SUB):
        selb, flag = _select_blocks(imp_sc[sub], step_q0 + sub * Q_TILE)
        selb_ref[0, 0, pl.ds(sub * Q_TILE, Q_TILE), :] = selb
        flag_ref[0, 0, sub] = flag


def _cmp_attend(q, kc, vc, sm, ovt, g, q0):
    width = kc.shape[0]
    s = _dot_nt(q, kc)
    r = lax.broadcasted_iota(jnp.int32, (NSA_HPG * Q_TILE, 1), 0) % Q_TILE
    n = lax.broadcasted_iota(jnp.int32, (1, width), 1)
    dc = (q0 + r) - (n * CMP_STRIDE + (CMP_LEN - 1))
    mask = (dc >= 0) & (n < N_CMP_PAD - 1)
    l = jnp.where(mask, s, NEG)
    m = jnp.max(l, axis=-1, keepdims=True)
    e = jnp.where(mask, jnp.exp2(l - m), 0.0)
    pc = e / jnp.maximum(jnp.sum(e, axis=-1, keepdims=True), 1e-30)
    oc = _dot(pc.astype(BF16), vc)
    oc = _head_tile(oc * _gate_rows(sm, g, 0))
    ps = pc[0:Q_TILE]
    for i in range(1, NSA_HPG):
        ps = ps + pc[i * Q_TILE:(i + 1) * Q_TILE]
    ps_hi = ps.astype(BF16)
    ps_lo = (ps - ps_hi.astype(F32)).astype(BF16)
    return oc, _dot_nt(ovt, ps_hi) + _dot_nt(ovt, ps_lo)


def _select_blocks(imp, q0):
    j = lax.broadcasted_iota(jnp.int32, imp.shape, 0)
    jf = j.astype(F32)
    t = q0 + lax.broadcasted_iota(jnp.int32, (1, Q_TILE), 1)
    cur = t // SEL_LEN
    forced = (j == 0) | (j == cur) | (j == cur - 1)
    v = jnp.where(j > cur, -FORCE, jnp.where(forced, FORCE, imp))
    sel = jnp.zeros(imp.shape, jnp.bool_)
    for _ in range(N_SEL):
        mx = jnp.max(v, axis=0, keepdims=True)
        idx = jnp.min(jnp.where(v == mx, jf, float(LANES)), axis=0, keepdims=True)
        pick = jf == idx
        sel = sel | pick
        v = jnp.where(pick, PICKED, v)
    live_t = jnp.where(sel & (j <= cur), 1.0, 0.0).astype(BF16)
    eye = (lax.broadcasted_iota(jnp.int32, imp.shape, 0) == lax.broadcasted_iota(jnp.int32, imp.shape, 1))
    live = _dot_nt(eye.astype(BF16), live_t)
    selb = jnp.where(live > 0.5, 0.0, NEG).astype(BF16)
    return selb, jnp.max(live, axis=0, keepdims=True).astype(jnp.int32)


def _cmp_attention(qa, kvc, sm, ov):
    B, H, S, _ = qa.shape
    G = NSA_KV_GROUPS
    nq = S // Q_TILE
    qt = CMP_SUB * Q_TILE
    return pl.pallas_call(
        _cmp_kernel,
        grid=(B, G, nq // CMP_SUB),
        in_specs=[
            pl.BlockSpec((1, NSA_HPG, qt, LANES), lambda b, g, i: (b, g, i, 0)),
            pl.BlockSpec((1, 1, N_CMP_PAD, LANES), lambda b, g, i: (b, g, 0, 0)),
            pl.BlockSpec((1, 1, N_CMP_PAD, LANES), lambda b, g, i: (b, 2 + g, 0, 0)),
            pl.BlockSpec((1, qt, LANES), lambda b, g, i: (b, i, 0)),
            pl.BlockSpec((LANES, N_CMP_PAD), lambda b, g, i: (0, 0)),
        ],
        out_specs=[
            pl.BlockSpec((1, qt, NSA_HPG * HEAD_DIM), lambda b, g, i: (b, i, g)),
            pl.BlockSpec((1, 1, qt, LANES), lambda b, g, i: (b, g, i, 0)),
            pl.BlockSpec((1, 1, CMP_SUB, 1, LANES), lambda b, g, i: (b, g, i, 0, 0)),
        ],
        out_shape=[
            jax.ShapeDtypeStruct((B, S, NSA_W), F32),
            jax.ShapeDtypeStruct((B, G, S, LANES), BF16),
            jax.ShapeDtypeStruct((B, G, nq, 1, LANES), jnp.int32),
        ],
        scratch_shapes=[pltpu.VMEM((CMP_SUB, LANES, Q_TILE), F32)],
        compiler_params=_cparams(("parallel", "parallel", "parallel")),
        name="cmp_attention",
    )(qa, kvc, kvc, sm, ov)


def _online_update(s, v, m_ref, acc_ref):
    m_old = m_ref[...]
    m_new = jnp.maximum(m_old, jnp.max(s, axis=-1, keepdims=True))
    chunks = [s[:, c * LANES:(c + 1) * LANES] - m_new for c in range(s.shape[1] // LANES)]
    p = jnp.exp2(jnp.concatenate(chunks, axis=1))
    acc_ref[...] = jnp.exp2(m_old - m_new) * acc_ref[...] + _dot(p.astype(BF16), v)
    m_ref[...] = m_new


def _normalized(acc):
    return acc / jnp.maximum(acc[:, EXT:EXT + 1], 1e-30)


def _attend_once(s, v):
    m = jnp.broadcast_to(jnp.max(s, axis=-1, keepdims=True), (s.shape[0], LANES))
    chunks = [s[:, c * LANES:(c + 1) * LANES] - m for c in range(s.shape[1] // LANES)]
    p = jnp.exp2(jnp.concatenate(chunks, axis=1))
    return _normalized(_dot(p.astype(BF16), v))


def _selwin_kernel(list_ref, cnt_ref, q_ref, ks_ref, vs_ref, kw_ref, vw_ref, selb_ref, oc_ref, sm_ref,
                   o_ref, m_a, acc_a, m_b, acc_b):
    b = pl.program_id(0)
    g = pl.program_id(1)
    qb = pl.program_id(2)
    nq = pl.num_programs(2)
    rows = NSA_HPG * SW_TILE
    q4 = q_ref[0].reshape(rows, LANES)
    q_aug = jnp.concatenate([q4, jnp.concatenate([selb_ref[0, 0]] * NSA_HPG, axis=0)], axis=1)
    r = lax.broadcasted_iota(jnp.int32, (rows, 1), 0) % SW_TILE
    c = lax.broadcasted_iota(jnp.int32, (1, K_TILE), 1)
    rel = r - c
    diag = qb * (SW_TILE // K_TILE)

    def sel_tile(kt, m_ref, acc_ref, causal=False, bias=None):
        start = pl.multiple_of(kt * K_TILE, K_TILE)
        s = _dot_nt(q_aug, ks_ref[0, 0, pl.ds(start, K_TILE), :])
        if bias is not None:
            s = s + bias
        if causal:
            s = jnp.where(rel + (qb * SW_TILE - kt * K_TILE) >= 0, s, NEG)
        _online_update(s, vs_ref[0, 0, pl.ds(start, K_TILE), :], m_ref, acc_ref)

    for m_ref, acc_ref in ((m_a, acc_a), (m_b, acc_b)):
        m_ref[...] = jnp.full(m_ref.shape, NEG, F32)
        acc_ref[...] = jnp.zeros(acc_ref.shape, F32)
    step = (b * NSA_KV_GROUPS + g) * nq + qb
    count = cnt_ref[step]
    base = step * MAX_TILES

    def body(p, carry):
        second = 2 * p + 1
        sel_tile(list_ref[base + 2 * p], m_a, acc_a)
        sel_tile(list_ref[base + jnp.minimum(second, MAX_TILES - 1)], m_b, acc_b,
                 bias=jnp.where(second < count, 0.0, NEG))
        return carry

    lax.fori_loop(0, (count + 1) // 2, body, 0)
    m_new = jnp.maximum(m_a[...], m_b[...])
    acc_a[...] = jnp.exp2(m_a[...] - m_new) * acc_a[...] + jnp.exp2(m_b[...] - m_new) * acc_b[...]
    m_a[...] = m_new
    for d in range(SW_TILE // K_TILE):
        sel_tile(diag + d, m_a, acc_a, causal=True)
    o_sel = _normalized(acc_a[...])

    span = WINDOW + SW_TILE
    wstart = pl.multiple_of(jnp.maximum(qb * SW_TILE - WINDOW, 0), K_TILE)
    dist = (qb * SW_TILE + r) - (wstart + lax.broadcasted_iota(jnp.int32, (1, span), 1))
    s = _dot_nt(q4, kw_ref[0, 0, pl.ds(wstart, span), :])
    s = jnp.where((dist >= 0) & (dist < WINDOW), s, NEG)
    o_win = _attend_once(s, vw_ref[0, 0, pl.ds(wstart, span), :])

    sm = sm_ref[0]
    y = _gate_rows(sm, g, 1) * o_sel + _gate_rows(sm, g, 2) * o_win
    o_ref[0] = (oc_ref[0] + _head_tile(y)).astype(BF16)


def _selwin_attention(tile_list, tile_count, qa, ksl, nkv, selb, ocg, sm):
    B, H, S, _ = qa.shape
    G = NSA_KV_GROUPS
    nq = S // SW_TILE
    rows = NSA_HPG * SW_TILE
    kv_spec = lambda piece: pl.BlockSpec((1, 1, S, LANES), lambda b, g, i, tl, tc: (b, piece + g, 0, 0))
    out_tile = pl.BlockSpec((1, SW_TILE, NSA_HPG * HEAD_DIM), lambda b, g, i, tl, tc: (b, i, g))
    grid_spec = pltpu.PrefetchScalarGridSpec(
        num_scalar_prefetch=2,
        grid=(B, G, nq),
        in_specs=[
            pl.BlockSpec((1, NSA_HPG, SW_TILE, LANES), lambda b, g, i, tl, tc: (b, g, i, 0)),
            pl.BlockSpec((1, 1, S, 2 * LANES), lambda b, g, i, tl, tc: (b, g, 0, 0)),
            kv_spec(0), kv_spec(2), kv_spec(4),
            pl.BlockSpec((1, 1, SW_TILE, LANES), lambda b, g, i, tl, tc: (b, g, i, 0)),
            out_tile,
            pl.BlockSpec((1, SW_TILE, LANES), lambda b, g, i, tl, tc: (b, i, 0)),
        ],
        out_specs=out_tile,
        scratch_shapes=[pltpu.VMEM((rows, LANES), F32)] * 4,
    )
    return pl.pallas_call(
        _selwin_kernel,
        grid_spec=grid_spec,
        out_shape=jax.ShapeDtypeStruct((B, S, NSA_W), BF16),
        compiler_params=_cparams(("parallel", "parallel", "arbitrary")),
        name="selwin_attention",
    )(tile_list, tile_count, qa, ksl, nkv, nkv, nkv, selb, ocg, sm)


def _fox_kernel(q_ref, k_ref, v_ref, o_ref, m_sc, acc_sc, *, tq):
    qi = pl.program_id(2)
    m_sc[...] = jnp.full(m_sc.shape, NEG, F32)
    acc_sc[...] = jnp.zeros(acc_sc.shape, F32)

    def tile(kt, width, causal):
        start = pl.multiple_of(kt * tq, tq)
        for hh in range(FOX_HPS):
            s = _dot_nt(q_ref[0, hh], k_ref[0, hh, pl.ds(start, width), :])
            if causal:
                r = lax.broadcasted_iota(jnp.int32, s.shape, 0)
                c = lax.broadcasted_iota(jnp.int32, s.shape, 1)
                s = jnp.where(r >= c, s, NEG)
            _online_update(s, v_ref[0, hh, pl.ds(start, width), :], m_sc.at[hh], acc_sc.at[hh])

    def body(kp, carry):
        tile(2 * kp, 2 * tq, False)
        return carry

    lax.fori_loop(0, qi // 2, body, 0)

    @pl.when(qi % 2 == 1)
    def _():
        tile(qi - 1, tq, False)

    tile(qi, tq, True)
    lane = lax.broadcasted_iota(jnp.int32, (tq, LANES), 1)
    o = [_normalized(acc_sc[hh]) for hh in range(FOX_HPS)]
    for pr in range(FOX_HPS // 2):
        o_ref[0, :, pr * LANES:(pr + 1) * LANES] = jnp.where(
            lane < HEAD_DIM, o[2 * pr], pltpu.roll(o[2 * pr + 1], HEAD_DIM, 1)).astype(BF16)


def _fox_attention(fq, fk, fv, tq=512):
    B, H, S, _ = fq.shape
    hps = FOX_HPS
    return pl.pallas_call(
        functools.partial(_fox_kernel, tq=tq),
        grid=(B, H // hps, S // tq),
        in_specs=[
            pl.BlockSpec((1, hps, tq, LANES), lambda b, h, i: (b, h, i, 0)),
            pl.BlockSpec((1, hps, S, LANES), lambda b, h, i: (b, h, 0, 0)),
            pl.BlockSpec((1, hps, S, LANES), lambda b, h, i: (b, h, 0, 0)),
        ],
        out_specs=pl.BlockSpec((1, tq, hps * HEAD_DIM), lambda b, h, i: (b, i, h)),
        out_shape=jax.ShapeDtypeStruct((B, S, FOX_W), BF16),
        scratch_shapes=[
            pltpu.VMEM((hps, tq, LANES), F32),
            pltpu.VMEM((hps, tq, LANES), F32),
        ],
        compiler_params=_cparams(("parallel", "parallel", "arbitrary")),
        name="fox_attention",
    )(fq, fk, fv)


def _merge_kernel(ya_ref, yb_ref, mg_ref, x_ref, mod_ref, gpost_ref, gpre_ref,
                  wa_ref, wb_ref, wo_ref, wrh_ref, wrl_ref, br_ref, stri_ref,
                  x1_ref, h2_ref, rt_ref, cnt_ref):
    D = D_MODEL

    @pl.when((pl.program_id(0) == 0) & (pl.program_id(1) == 0))
    def _():
        cnt_ref[...] = jnp.zeros(cnt_ref.shape, F32)

    a = _dot(ya_ref[0], wa_ref[...])
    bq = _dot(yb_ref[0], wb_ref[...])
    mg = mg_ref[0]
    u = mg[:, :D].astype(F32) * a + mg[:, D:].astype(F32) * bq
    mixed = _dot(u.astype(BF16), wo_ref[...])
    x1 = x_ref[0] + mod_ref[0, 2:3, :] * _rms(mixed, gpost_ref[...])
    x1_ref[0] = x1
    h2 = _rms(x1, gpre_ref[...]) * (1.0 + mod_ref[0, 4:5, :]) + mod_ref[0, 3:4, :]
    hi = h2.astype(BF16)
    lo = (h2 - hi.astype(F32)).astype(BF16)
    h2_ref[0] = h2
    lg = _dot(hi, wrh_ref[...]) + _dot(lo, wrh_ref[...]) + _dot(hi, wrl_ref[...]) + br_ref[...]

    lane = lax.broadcasted_iota(jnp.int32, lg.shape, 1)
    lanef = lane.astype(F32)
    no_lane = float(LANES)
    is_g = lane < N_EXPERT_GROUPS
    gl = jnp.where(is_g, lg, NEG)
    gmax = jnp.max(gl, axis=-1, keepdims=True)
    pg_top = 1.0 / jnp.sum(jnp.where(is_g, jnp.exp(gl - gmax), 0.0), axis=-1, keepdims=True)
    g_idx = jnp.min(jnp.where(is_g & (gl == gmax), lanef, no_lane), axis=-1, keepdims=True)
    in_grp = ((lane >= N_EXPERT_GROUPS) & (lane < N_EXPERT_GROUPS + N_EXPERTS)
              & (((lane - N_EXPERT_GROUPS) // EXPERTS_PER_GROUP).astype(F32) == g_idx))
    le = jnp.where(in_grp, lg, NEG)
    m1 = jnp.max(le, axis=-1, keepdims=True)
    i1 = jnp.min(jnp.where(in_grp & (le == m1), lanef, no_lane), axis=-1, keepdims=True)
    rest = in_grp & (lanef != i1)
    le2 = jnp.where(rest, lg, NEG)
    m2 = jnp.max(le2, axis=-1, keepdims=True)
    i2 = jnp.min(jnp.where(rest & (le2 == m2), lanef, no_lane), axis=-1, keepdims=True)
    e21 = jnp.exp(m2 - m1)
    w1 = pg_top / (1.0 + e21)
    w2 = w1 * e21
    pick1 = lanef == i1
    pick2 = lanef == i2
    onehot = jnp.where(pick1 | pick2, 1.0, 0.0)
    before = cnt_ref[...] + _dot(stri_ref[...], onehot.astype(BF16))
    rank1 = jnp.sum(jnp.where(pick1, before, 0.0), axis=-1, keepdims=True)
    rank2 = jnp.sum(jnp.where(pick2, before, 0.0), axis=-1, keepdims=True)
    cnt_ref[...] = cnt_ref[...] + jnp.sum(onehot, axis=0, keepdims=True)
    fields = {RT_EXPERT: i1 - N_EXPERT_GROUPS, RT_EXPERT + 1: i2 - N_EXPERT_GROUPS,
              RT_RANK: rank1, RT_RANK + 1: rank2, RT_WEIGHT: w1, RT_WEIGHT + 1: w2}
    rt = jnp.zeros(lg.shape, F32)
    for k, f in fields.items():
        rt = jnp.where(lane == k, f, rt)
    rt_ref[0] = rt


def _merge(ya, yb, mg, x, mod, gpost, gpre, wa, wb, wo, wrh, wrl, br, stri):
    B, S, D = x.shape
    tm = MERGE_TILE
    c2 = lambda b, i: (0, 0)
    row = lambda w: pl.BlockSpec((1, tm, w), lambda b, i: (b, i, 0))
    return pl.pallas_call(
        _merge_kernel,
        grid=(B, S // tm),
        in_specs=[
            row(NSA_W), row(FOX_W), row(2 * D), row(D),
            pl.BlockSpec((1, 6, D), lambda b, i: (b, 0, 0)),
            pl.BlockSpec((1, D), c2), pl.BlockSpec((1, D), c2),
            pl.BlockSpec((NSA_W, D), c2), pl.BlockSpec((FOX_W, D), c2), pl.BlockSpec((D, D), c2),
            pl.BlockSpec((D, LANES), c2), pl.BlockSpec((D, LANES), c2), pl.BlockSpec((1, LANES), c2),
            pl.BlockSpec((tm, tm), c2),
        ],
        out_specs=[row(D), row(D), row(LANES), pl.BlockSpec((1, LANES), c2)],
        out_shape=[
            jax.ShapeDtypeStruct((B, S, D), F32),
            jax.ShapeDtypeStruct((B, S, D), F32),
            jax.ShapeDtypeStruct((B, S, LANES), F32),
            jax.ShapeDtypeStruct((1, LANES), F32),
        ],
        compiler_params=_cparams(("arbitrary", "arbitrary")),
        name="merge",
    )(ya, yb, mg, x, mod, gpost, gpre, wa, wb, wo, wrh, wrl, br, stri)


def _expert_kernel(be_ref, na_ref, tok_ref, h_hbm, wg_ref, wu_ref, wd_ref, o_ref, *scratch):
    i = pl.program_id(0)
    n_active = na_ref[0]
    last_block = pl.num_programs(0) - 1
    ring = GATHER_AHEAD + 1
    bufs = scratch[:ring]
    wg_b, wu_b, wd_b, sem = scratch[ring:]

    def row_copy(blk, r, sl):
        tok = tok_ref[blk * MOE_TILE + r]
        return pltpu.make_async_copy(h_hbm.at[pl.ds(tok, 1)], bufs[sl].at[pl.ds(r, 1)], sem.at[sl])

    def wait_rows(sl):
        pltpu.make_async_copy(h_hbm.at[pl.ds(0, MOE_TILE)], bufs[sl], sem.at[sl]).wait()

    @pl.when(i == 0)
    def _():
        for ahead in range(GATHER_AHEAD):
            def body(r, carry, ahead=ahead):
                row_copy(jnp.minimum(ahead, last_block), r, ahead).start()
                return carry
            lax.fori_loop(0, MOE_TILE, body, 0, unroll=8)

    @pl.when((i == 0) | (be_ref[i] != be_ref[jnp.maximum(i - 1, 0)]))
    def _():
        wg_b[...] = wg_ref[0].astype(BF16)
        wu_b[...] = wu_ref[0].astype(BF16)
        wd_b[...] = wd_ref[0].astype(BF16)

    def step(sl):
        wait_rows(sl)
        nxt = jnp.minimum(i + GATHER_AHEAD, last_block)
        nxt_sl = (sl + GATHER_AHEAD) % ring
        for r in range(MOE_TILE):
            row_copy(nxt, r, nxt_sl).start(priority=r % 2)
        x = bufs[sl][...].astype(BF16)
        gate = _dot(x, wg_b[...])
        up = _dot(x, wu_b[...])
        mid = (gate * jax.nn.sigmoid(gate) * up).astype(BF16)
        o_ref[...] = _dot(mid, wd_b[...])

        @pl.when(i == n_active - 1)
        def _():
            for ahead in range(1, GATHER_AHEAD + 1):
                wait_rows((sl + ahead) % ring)

    for sl in range(ring):
        pl.when((i % ring == sl) & (i < n_active))(functools.partial(step, sl))

    @pl.when(i >= n_active)
    def _():
        o_ref[...] = jnp.zeros(o_ref.shape, o_ref.dtype)


def _experts(block_expert, n_active, buf_tok, h2, wg, wu, wd):
    cap = buf_tok.shape[0]
    D = D_MODEL
    nblk = cap // MOE_TILE
    grid_spec = pltpu.PrefetchScalarGridSpec(
        num_scalar_prefetch=3,
        grid=(nblk,),
        in_specs=[
            pl.BlockSpec(memory_space=pl.ANY),
            pl.BlockSpec((1, D, D_EXPERT), lambda i, be, na, tok: (be[i], 0, 0)),
            pl.BlockSpec((1, D, D_EXPERT), lambda i, be, na, tok: (be[i], 0, 0)),
            pl.BlockSpec((1, D_EXPERT, D), lambda i, be, na, tok: (be[i], 0, 0)),
        ],
        out_specs=pl.BlockSpec((MOE_TILE, D), lambda i, be, na, tok: (i, 0)),
        scratch_shapes=[
            *([pltpu.VMEM((MOE_TILE, D), F32)] * (GATHER_AHEAD + 1)),
            pltpu.VMEM((D, D_EXPERT), BF16),
            pltpu.VMEM((D, D_EXPERT), BF16),
            pltpu.VMEM((D_EXPERT, D), BF16),
            pltpu.SemaphoreType.DMA((GATHER_AHEAD + 1,)),
        ],
    )
    return pl.pallas_call(
        _expert_kernel,
        grid_spec=grid_spec,
        out_shape=jax.ShapeDtypeStruct((cap, D), F32),
        compiler_params=_cparams(("arbitrary",)),
        name="experts",
    )(block_expert, n_active, buf_tok, h2, wg, wu, wd)


def _final_kernel(dest_ref, x1_ref, rt_ref, mod_ref, g_ref, y_hbm, o_ref, *scratch):
    j = pl.program_id(0)
    last_tile = pl.num_programs(0) - 1
    tm = o_ref.shape[0]
    ring = GATHER_AHEAD + 1
    bufs = tuple(scratch[EXPERT_TOP_K * sl:EXPERT_TOP_K * (sl + 1)] for sl in range(ring))
    sem = scratch[-1]

    def row_copy(tile, r, k, sl):
        row = dest_ref[(tile * tm + r) * EXPERT_TOP_K + k]
        return pltpu.make_async_copy(y_hbm.at[pl.ds(row, 1)], bufs[sl][k].at[pl.ds(r, 1)], sem.at[sl])

    def wait_rows(sl):
        for k in range(EXPERT_TOP_K):
            pltpu.make_async_copy(y_hbm.at[pl.ds(0, tm)], bufs[sl][k], sem.at[sl]).wait()

    @pl.when(j == 0)
    def _():
        for ahead in range(GATHER_AHEAD):
            def body(r, carry, ahead=ahead):
                for k in range(EXPERT_TOP_K):
                    row_copy(jnp.minimum(ahead, last_tile), r, k, ahead).start()
                return carry
            lax.fori_loop(0, tm, body, 0, unroll=4)

    def step(sl):
        wait_rows(sl)
        nxt = jnp.minimum(j + GATHER_AHEAD, last_tile)
        nxt_sl = (sl + GATHER_AHEAD) % ring
        for r in range(tm):
            for k in range(EXPERT_TOP_K):
                row_copy(nxt, r, k, nxt_sl).start(priority=k)
        rt = rt_ref[...]
        lane = lax.broadcasted_iota(jnp.int32, rt.shape, 1)
        w0 = jnp.sum(jnp.where(lane == RT_WEIGHT, rt, 0.0), axis=-1, keepdims=True)
        w1 = jnp.sum(jnp.where(lane == RT_WEIGHT + 1, rt, 0.0), axis=-1, keepdims=True)
        y = w0 * bufs[sl][0][...] + w1 * bufs[sl][1][...]
        o_ref[...] = x1_ref[...] + mod_ref[0, 5:6, :] * _rms(y, g_ref[...])

        @pl.when(j == last_tile)
        def _():
            for ahead in range(1, GATHER_AHEAD + 1):
                wait_rows((sl + ahead) % ring)

    for sl in range(ring):
        pl.when(j % ring == sl)(functools.partial(step, sl))


def _final(dest, x1, rt, mod, g, yb, tiles_per_batch):
    T, D = x1.shape
    tm = FINAL_TILE
    grid_spec = pltpu.PrefetchScalarGridSpec(
        num_scalar_prefetch=1,
        grid=(T // tm,),
        in_specs=[
            pl.BlockSpec((tm, D), lambda j, d: (j, 0)),
            pl.BlockSpec((tm, LANES), lambda j, d: (j, 0)),
            pl.BlockSpec((1, 6, D), lambda j, d: (j // tiles_per_batch, 0, 0)),
            pl.BlockSpec((1, D), lambda j, d: (0, 0)),
            pl.BlockSpec(memory_space=pl.ANY),
        ],
        out_specs=pl.BlockSpec((tm, D), lambda j, d: (j, 0)),
        scratch_shapes=([pltpu.VMEM((tm, D), F32)] * (EXPERT_TOP_K * (GATHER_AHEAD + 1))
                        + [pltpu.SemaphoreType.DMA((GATHER_AHEAD + 1,))]),
    )
    return pl.pallas_call(
        _final_kernel,
        grid_spec=grid_spec,
        out_shape=jax.ShapeDtypeStruct((T, D), F32),
        compiler_params=_cparams(("arbitrary",)),
        name="final",
    )(dest, x1, rt, mod, g, yb)


def _overlap_matrix():
    n = np.arange(N_CMP_PAD)[:, None]
    j = np.arange(LANES)[None, :]
    start = n * CMP_STRIDE
    ov = (start < j * SEL_LEN + SEL_LEN) & (start + CMP_LEN - 1 >= j * SEL_LEN) & (n < N_CMP_PAD - 1)
    return jnp.asarray(ov.T.astype(np.float32), dtype=BF16)


def _pad_cols(w, width=LANES):
    return jnp.pad(w, ((0, 0), (0, width - w.shape[1])))


def _dispatch_plan(rt, cnt, T):
    expert = rt[:, RT_EXPERT:RT_EXPERT + EXPERT_TOP_K].astype(jnp.int32)
    rank = rt[:, RT_RANK:RT_RANK + EXPERT_TOP_K].astype(jnp.int32)
    weight = rt[:, RT_WEIGHT:RT_WEIGHT + EXPERT_TOP_K]
    counts = cnt[0, N_EXPERT_GROUPS:N_EXPERT_GROUPS + N_EXPERTS].astype(jnp.int32)
    padded = (counts + MOE_TILE - 1) // MOE_TILE * MOE_TILE
    pad_end = jnp.cumsum(padded)
    pad_start = pad_end - padded
    onehot = expert[:, :, None] == jnp.arange(N_EXPERTS)[None, None, :]
    dest = jnp.sum(jnp.where(onehot, pad_start[None, None, :], 0), axis=-1) + rank
    A = T * EXPERT_TOP_K
    cap = -(-(A + N_EXPERTS * (MOE_TILE - 1)) // MOE_TILE) * MOE_TILE
    nblk = cap // MOE_TILE
    n_active = (pad_end[-1] // MOE_TILE).astype(jnp.int32)
    blk = jnp.arange(nblk) * MOE_TILE
    block_expert = jnp.minimum(jnp.sum(pad_end[None, :] <= blk[:, None], axis=1), N_EXPERTS - 1)
    last = jnp.max(jnp.where(jnp.arange(nblk) < n_active, block_expert, 0))
    block_expert = jnp.where(jnp.arange(nblk) < n_active, block_expert, last).astype(jnp.int32)
    tok = jnp.arange(A, dtype=jnp.int32) // EXPERT_TOP_K
    buf_tok = jnp.zeros((cap,), jnp.int32).at[dest.reshape(A)].set(tok)
    return weight, dest, buf_tok, block_expert, n_active.reshape(1)


def kernel(x, c, w_ada, b_ada, g_pre_mix, g_post_mix, g_pre_ffn, g_post_ffn, w_in, b_forget,
           cmp_pe_k, cmp_w1_k, cmp_w2_k, cmp_pe_v, cmp_w1_v, cmp_w2_v,
           w_o_nsa, w_o_fox, w_out, w_router_group, b_router_group, w_router_expert, b_router_expert,
           w_exp_gate, w_exp_up, w_exp_down):
    B, S, D = x.shape
    T = B * S
    depth = w_ada.shape[0]
    ov = _overlap_matrix()
    tri = jnp.asarray(np.tril(np.ones((IN_TILE, IN_TILE), np.float32)), dtype=BF16)
    stri = jnp.asarray(np.tril(np.ones((MERGE_TILE, MERGE_TILE), np.float32), -1), dtype=BF16)
    row_feat = _row_features(S)
    placement = _placement()
    cmp_ext = _cmp_key_ext()
    for l in range(depth):
        mod = _adaln(c, w_ada[l], b_ada[l].reshape(1, 6 * D)).reshape(B, 6, D)
        w_qa, w_kva, w_gl, w_fox, w_f, w_mg = jnp.split(w_in[l], IN_SPLITS, axis=-1)
        w_big = jnp.concatenate([w_qa, w_kva, w_fox, w_mg], axis=1).astype(BF16)
        w_small = _pad_cols(jnp.concatenate([w_gl, w_f], axis=1)).astype(BF16)
        bf_pad = jnp.pad(b_forget[l], (F_LANE, LANES - F_LANE - FOX_HEADS)).reshape(1, LANES)
        qa, ckv, ksl, nkv, fq, fk, fv, mg, sm = _inproj(
            x, mod, g_pre_mix[l].reshape(1, D), w_big, w_small, bf_pad, tri, row_feat, placement)

        half = CMP_LEN // 2
        pe = jnp.stack([cmp_pe_k[l], cmp_pe_v[l]]).reshape(2, 2, 1, half * HEAD_DIM)
        w1 = jnp.stack([cmp_w1_k[l], cmp_w1_v[l]]).reshape(2, 2, half * HEAD_DIM, HEAD_DIM).astype(BF16)
        w2 = jnp.pad(jnp.stack([cmp_w2_k[l], cmp_w2_v[l]]), ((0, 0), (0, 0), (0, LANES - HEAD_DIM))).astype(BF16)
        kvc = _compress(ckv.reshape(B, 4, S // CMP_STRIDE, CMP_STRIDE * HEAD_DIM), pe, w1, w2, cmp_ext)
        ocg, selb, flags = _cmp_attention(qa, kvc, sm, ov)
        nq = S // SW_TILE
        per_tile = K_TILE // SEL_LEN
        tile_any = jnp.max(flags.reshape(B, NSA_KV_GROUPS, nq, SW_TILE // Q_TILE, MAX_TILES, per_tile), axis=(3, 5))
        tile_id = jnp.arange(MAX_TILES)
        diag = (jnp.arange(nq) * (SW_TILE // K_TILE))[:, None]
        active = (tile_any > 0) & (tile_id < diag)
        slot = jnp.cumsum(active, axis=-1) - 1
        hit = active[..., :, None] & (slot[..., :, None] == tile_id)
        tile_list = jnp.sum(jnp.where(hit, tile_id[:, None], 0), axis=-2).astype(jnp.int32).reshape(-1)
        tile_count = jnp.sum(active, axis=-1).astype(jnp.int32).reshape(-1)
        y_a = _selwin_attention(tile_list, tile_count, qa, ksl, nkv, selb, ocg, sm)

        y_b = _fox_attention(fq, fk, fv)

        w_r = _pad_cols(jnp.concatenate([w_router_group[l], w_router_expert[l]], axis=1))
        w_rh = w_r.astype(BF16)
        w_rl = (w_r - w_rh.astype(F32)).astype(BF16)
        b_r = _pad_cols(jnp.concatenate([b_router_group[l], b_router_expert[l]]).reshape(1, -1))
        x1, h2, rt, cnt = _merge(y_a, y_b, mg, x, mod, g_post_mix[l].reshape(1, D), g_pre_ffn[l].reshape(1, D),
                                 w_o_nsa[l].astype(BF16), w_o_fox[l].astype(BF16), w_out[l].astype(BF16),
                                 w_rh, w_rl, b_r, stri)

        weight, dest, buf_tok, block_expert, n_active = _dispatch_plan(rt.reshape(T, LANES), cnt, T)
        yb = _experts(block_expert, n_active, buf_tok, h2.reshape(T, D), w_exp_gate[l], w_exp_up[l], w_exp_down[l])
        x = _final(dest.reshape(T * EXPERT_TOP_K), x1.reshape(T, D), rt.reshape(T, LANES), mod,
                   g_post_ffn[l].reshape(1, D), yb, S // FINAL_TILE).reshape(B, S, D)
    return x
```

```python
import functools

import ml_dtypes
import numpy as np
import jax
import jax.numpy as jnp
from jax import lax
from jax.experimental import pallas as pl
from jax.experimental.pallas import tpu as pltpu

D_MODEL = 1024
HEAD_DIM = 64
NSA_HEADS = 8
NSA_KV_GROUPS = 2
NSA_HPG = NSA_HEADS // NSA_KV_GROUPS
FOX_HEADS = 8
CMP_LEN = 32
CMP_STRIDE = 16
SEL_LEN = 64
N_SEL = 16
WINDOW = 512
N_EXPERT_GROUPS = 4
EXPERTS_PER_GROUP = 8
N_EXPERTS = N_EXPERT_GROUPS * EXPERTS_PER_GROUP
EXPERT_TOP_K = 2
D_EXPERT = D_MODEL // 2
NORM_EPS = 1e-6
NEG = -1e30
FORCE = 1e9
LOG2E = 1.4426950408889634

NSA_W = NSA_HEADS * HEAD_DIM
NSA_KV_W = NSA_KV_GROUPS * HEAD_DIM
FOX_W = FOX_HEADS * HEAD_DIM
IN_SIZES = (NSA_W, 6 * NSA_KV_W, 3 * NSA_HEADS, 3 * FOX_W, FOX_HEADS, 2 * D_MODEL)
IN_SPLITS = tuple(int(v) for v in np.cumsum(IN_SIZES)[:-1])

LANES = 128
Q_TILE = 128
K_TILE = 256
SW_TILE = 256
N_CMP_PAD = 512
MOE_TILE = 256
IN_TILE = 512
MERGE_TILE = 512
FINAL_TILE = 256
FOX_HPS = 4
CMP_SUB = 8
MAX_TILES = 32
GATHER_AHEAD = 3
V7X_VMEM_BYTES = 64 * 1024 * 1024
VMEM_LIMIT = V7X_VMEM_BYTES - 8 * 1024 * 1024
RT_EXPERT, RT_RANK, RT_WEIGHT = 0, 2, 4
PICKED = -3e38

F_LANE = 3 * NSA_HEADS
U_LANE = 64
ONE_LANE = 88
A_LANE = 89
B_LANE = 90
EXT = HEAD_DIM
G_FQ, G_FK, G_NQ, G_NK, N_GROUPS = 0, 8, 16, 24, 25

F32 = jnp.float32
BF16 = jnp.bfloat16


def _dot(a, b):
    return jnp.dot(a, b, preferred_element_type=F32)


def _dot_nt(a, b):
    return lax.dot_general(a, b, (((1,), (1,)), ((), ())), preferred_element_type=F32)


def _rms(x, g):
    return x * lax.rsqrt(jnp.mean(x * x, axis=-1, keepdims=True) + NORM_EPS) * g


def _cparams(sem):
    return pltpu.CompilerParams(dimension_semantics=sem, vmem_limit_bytes=VMEM_LIMIT)


def _split3(x):
    hi = x.astype(BF16).astype(F32)
    r = x - hi
    mid = r.astype(BF16).astype(F32)
    lo = (r - mid).astype(BF16).astype(F32)
    return hi, mid, lo


def _np_split3(x):
    x = np.asarray(x, np.float32)
    hi = x.astype(ml_dtypes.bfloat16).astype(np.float32)
    r = x - hi
    mid = r.astype(ml_dtypes.bfloat16).astype(np.float32)
    lo = (r - mid).astype(ml_dtypes.bfloat16).astype(np.float32)
    return hi, mid, lo


def _alibi_c():
    slopes = np.exp2(-8.0 * np.arange(1, NSA_HEADS + 1, dtype=np.float32) / NSA_HEADS).astype(np.float32)
    return slopes * np.float32(LOG2E)


def _row_features(S):
    t = np.arange(S, dtype=np.float32)
    c = _alibi_c()
    rs = np.zeros((S, LANES), np.float32)
    for h in range(NSA_HEADS):
        for j, term in enumerate(_np_split3(c[h] * t)):
            rs[:, U_LANE + 8 * j + h] = -term
    rs[:, ONE_LANE] = 1.0
    rs[:, A_LANE] = np.floor(t / LANES)
    rs[:, B_LANE] = t % LANES
    return jnp.asarray(rs, dtype=BF16)


def _placement():
    c = _alibi_c()
    p = np.zeros((LANES, N_GROUPS * LANES), np.float32)
    for h in range(FOX_HEADS):
        q0 = (G_FQ + h) * LANES + EXT
        k0 = (G_FK + h) * LANES + EXT
        for j in range(3):
            p[ONE_LANE, q0 + j] = -1.0
            p[F_LANE + 8 * j + h, q0 + 3 + j] = 1.0
            p[F_LANE + 8 * j + h, k0 + j] = 1.0
            p[ONE_LANE, k0 + 3 + j] = 1.0
    for h in range(NSA_HEADS):
        q0 = (G_NQ + h) * LANES + EXT
        c128 = _np_split3(c[h] * np.float32(LANES))
        c1 = _np_split3(c[h])
        for j in range(3):
            p[U_LANE + 8 * j + h, q0 + j] = 1.0
            p[ONE_LANE, q0 + 3 + j] = c128[j]
            p[ONE_LANE, q0 + 6 + j] = c1[j]
    k0 = G_NK * LANES + EXT
    for j in range(3):
        p[ONE_LANE, k0 + j] = 1.0
        p[A_LANE, k0 + 3 + j] = 1.0
        p[B_LANE, k0 + 6 + j] = 1.0
    return jnp.asarray(p, dtype=BF16)


def _cmp_key_ext():
    pos = np.arange(N_CMP_PAD, dtype=np.float32) * CMP_STRIDE + (CMP_LEN - 1)
    e = np.zeros((2, N_CMP_PAD, LANES), np.float32)
    for j in range(3):
        e[0, :, EXT + j] = 1.0
        e[0, :, EXT + 3 + j] = np.floor(pos / LANES)
        e[0, :, EXT + 6 + j] = pos % LANES
    return jnp.asarray(e, dtype=BF16)


def _adaln_kernel(c_ref, w_ref, b_ref, o_ref):
    c = c_ref[...]
    act = (c * jax.nn.sigmoid(c)).astype(BF16)
    o_ref[...] = _dot(act, w_ref[...].astype(BF16)) + b_ref[...]


def _adaln(c, w, b):
    B, D = c.shape
    n = w.shape[1]
    return pl.pallas_call(
        _adaln_kernel,
        grid=(n // D,),
        in_specs=[
            pl.BlockSpec((B, D), lambda j: (0, 0)),
            pl.BlockSpec((D, D), lambda j: (0, j)),
            pl.BlockSpec((1, D), lambda j: (0, j)),
        ],
        out_specs=pl.BlockSpec((B, D), lambda j: (0, j)),
        out_shape=jax.ShapeDtypeStruct((B, n), F32),
        compiler_params=_cparams(("parallel",)),
        name="adaln",
    )(c, w, b)


def _inproj_kernel(x_ref, mod_ref, g_ref, wb_ref, ws_ref, bf_ref, tri_ref, rs_ref, p_ref,
                   qa_ref, ckv_ref, ksl_ref, nkv_ref, fq_ref, fk_ref, fv_ref, mg_ref, sm_ref, carry_sc):
    i = pl.program_id(1)
    tm = x_ref.shape[1]
    x = x_ref[0]
    h = _rms(x, g_ref[...]) * (1.0 + mod_ref[0, 1:2, :]) + mod_ref[0, 0:1, :]
    hb = h.astype(BF16)
    lane = lax.broadcasted_iota(jnp.int32, (tm, LANES), 1)
    lower = lane < HEAD_DIM
    ones_col = (lane == EXT).astype(F32)

    z = _dot(hb, ws_ref[...]) + bf_ref[...]
    logsig = jnp.minimum(z, 0.0) - jnp.log1p(jnp.exp(-jnp.abs(z)))
    sm_ref[0] = jnp.where(lane < F_LANE, jax.nn.sigmoid(z), logsig)

    @pl.when(i == 0)
    def _():
        carry_sc[...] = jnp.zeros(carry_sc.shape, F32)

    is_f = (lane >= F_LANE) & (lane < F_LANE + FOX_HEADS)
    l_hi, l_mid, l_lo = _split3(jnp.where(is_f, logsig, 0.0))
    tri = tri_ref[...]
    cum = carry_sc[...] + _dot(tri, l_hi.astype(BF16)) + _dot(tri, l_mid.astype(BF16)) + _dot(tri, l_lo.astype(BF16))
    carry_sc[...] = cum[tm - 1:tm, :]
    f_hi, f_mid, f_lo = _split3(cum * LOG2E)
    feat = (f_hi + pltpu.roll(f_mid, 8, 1) + pltpu.roll(f_lo, 16, 1) + rs_ref[...].astype(F32)).astype(BF16)

    ext_pairs = {}

    def ext(group):
        first = group - group % 2
        if first not in ext_pairs:
            width = min(2, N_GROUPS - first) * LANES
            ext_pairs[first] = _dot(feat, p_ref[:, first * LANES:first * LANES + width])
        off = (group - first) * LANES
        return ext_pairs[first][:, off:off + LANES]

    def piece(acc, idx, extra):
        pair = acc[:, (idx // 2) * LANES:(idx // 2 + 1) * LANES]
        if idx % 2:
            pair = pltpu.roll(pair, HEAD_DIM, 1)
        return jnp.where(lower, pair, extra).astype(BF16)

    qscale = (HEAD_DIM ** -0.5) * LOG2E
    acc = _dot(hb, wb_ref[:, 0:NSA_W]) * qscale
    for hd in range(NSA_HEADS):
        qa_ref[0, hd] = piece(acc, hd, ext(G_NQ + hd))
    off = NSA_W
    acc = _dot(hb, wb_ref[:, off:off + 6 * NSA_KV_W])
    for pc in range(4):
        ckv_ref[0, pc] = acc[:, pc * HEAD_DIM:(pc + 1) * HEAD_DIM].astype(BF16)
    ext_k = ext(G_NK)
    t = i * tm + lax.broadcasted_iota(jnp.int32, (tm, LANES), 0)
    block_onehot = (lane == t // SEL_LEN).astype(BF16)
    for g in range(NSA_KV_GROUPS):
        ksl_ref[0, g, :, 0:LANES] = piece(acc, 4 + g, ext_k)
        ksl_ref[0, g, :, LANES:2 * LANES] = block_onehot
        nkv_ref[0, g] = piece(acc, 6 + g, ones_col)
        nkv_ref[0, 2 + g] = piece(acc, 8 + g, ext_k)
        nkv_ref[0, 4 + g] = piece(acc, 10 + g, ones_col)
    off += 6 * NSA_KV_W
    acc = _dot(hb, wb_ref[:, off:off + FOX_W]) * qscale
    for hd in range(FOX_HEADS):
        fq_ref[0, hd] = piece(acc, hd, ext(G_FQ + hd))
    off += FOX_W
    acc = _dot(hb, wb_ref[:, off:off + FOX_W])
    for hd in range(FOX_HEADS):
        fk_ref[0, hd] = piece(acc, hd, ext(G_FK + hd))
    off += FOX_W
    acc = _dot(hb, wb_ref[:, off:off + FOX_W])
    for hd in range(FOX_HEADS):
        fv_ref[0, hd] = piece(acc, hd, ones_col)
    off += FOX_W
    for c in range(4):
        acc = _dot(hb, wb_ref[:, off + c * 512: off + (c + 1) * 512])
        mg_ref[0, :, c * 512:(c + 1) * 512] = jax.nn.sigmoid(acc).astype(BF16)


def _inproj(x, mod, g, wb, ws, bfp, tri, rs, pm):
    B, S, D = x.shape
    tm = IN_TILE
    nb = wb.shape[1]
    const2 = lambda b, i: (0, 0)
    heads = lambda n: pl.BlockSpec((1, n, tm, LANES), lambda b, i: (b, 0, i, 0))
    hshape = lambda n: jax.ShapeDtypeStruct((B, n, S, LANES), BF16)
    return pl.pallas_call(
        _inproj_kernel,
        grid=(B, S // tm),
        in_specs=[
            pl.BlockSpec((1, tm, D), lambda b, i: (b, i, 0)),
            pl.BlockSpec((1, 6, D), lambda b, i: (b, 0, 0)),
            pl.BlockSpec((1, D), const2),
            pl.BlockSpec((D, nb), const2),
            pl.BlockSpec((D, LANES), const2),
            pl.BlockSpec((1, LANES), const2),
            pl.BlockSpec((tm, tm), const2),
            pl.BlockSpec((tm, LANES), lambda b, i: (i, 0)),
            pl.BlockSpec((LANES, N_GROUPS * LANES), const2),
        ],
        out_specs=[
            heads(NSA_HEADS),
            pl.BlockSpec((1, 4, tm, HEAD_DIM), lambda b, i: (b, 0, i, 0)),
            pl.BlockSpec((1, NSA_KV_GROUPS, tm, 2 * LANES), lambda b, i: (b, 0, i, 0)),
            heads(6), heads(FOX_HEADS), heads(FOX_HEADS), heads(FOX_HEADS),
            pl.BlockSpec((1, tm, 2 * D), lambda b, i: (b, i, 0)),
            pl.BlockSpec((1, tm, LANES), lambda b, i: (b, i, 0)),
        ],
        out_shape=[
            hshape(NSA_HEADS),
            jax.ShapeDtypeStruct((B, 4, S, HEAD_DIM), BF16),
            jax.ShapeDtypeStruct((B, NSA_KV_GROUPS, S, 2 * LANES), BF16),
            hshape(6), hshape(FOX_HEADS), hshape(FOX_HEADS), hshape(FOX_HEADS),
            jax.ShapeDtypeStruct((B, S, 2 * D), BF16),
            jax.ShapeDtypeStruct((B, S, LANES), F32),
        ],
        scratch_shapes=[pltpu.VMEM((1, LANES), F32)],
        compiler_params=_cparams(("parallel", "arbitrary")),
        name="inproj",
    )(x, mod, g, wb, ws, bfp, tri, rs, pm)


def _compress_kernel(x_ref, pe_ref, w1_ref, w2_ref, e_ref, o_ref):
    x = x_ref[0, 0].astype(F32)
    x_lo = (x + pe_ref[0, 0]).astype(BF16)
    x_hi = (x + pe_ref[0, 1]).astype(BF16)
    y_lo = _dot(x_lo, w1_ref[0, 0])
    y_hi = _dot(x_hi, w1_ref[0, 1])
    n = y_hi.shape[0]
    hid = y_lo + pltpu.roll(y_hi, n - 1, 0)
    hid = jax.nn.gelu(hid)
    o_ref[0, 0] = (_dot(hid.astype(BF16), w2_ref[0]) + e_ref[0].astype(F32)).astype(BF16)


def _compress(kv_rows, pe, w1, w2, e):
    B = kv_rows.shape[0]
    R, C = kv_rows.shape[2], kv_rows.shape[3]
    return pl.pallas_call(
        _compress_kernel,
        grid=(B, 4),
        in_specs=[
            pl.BlockSpec((1, 1, R, C), lambda b, p: (b, p, 0, 0)),
            pl.BlockSpec((1, 2, 1, C), lambda b, p: (p // 2, 0, 0, 0)),
            pl.BlockSpec((1, 2, C, HEAD_DIM), lambda b, p: (p // 2, 0, 0, 0)),
            pl.BlockSpec((1, HEAD_DIM, LANES), lambda b, p: (p // 2, 0, 0)),
            pl.BlockSpec((1, R, LANES), lambda b, p: (p // 2, 0, 0)),
        ],
        out_specs=pl.BlockSpec((1, 1, R, LANES), lambda b, p: (b, p, 0, 0)),
        out_shape=jax.ShapeDtypeStruct((B, 4, R, LANES), BF16),
        compiler_params=_cparams(("parallel", "parallel")),
        name="compress",
    )(kv_rows, pe, w1, w2, e)


def _gate_rows(sm, g, branch):
    col = lax.broadcasted_iota(jnp.int32, sm.shape, 1)
    parts = []
    for hl in range(NSA_HPG):
        want = 3 * (NSA_HPG * g + hl) + branch
        parts.append(jnp.sum(jnp.where(col == want, sm, 0.0), axis=-1, keepdims=True))
    return jnp.concatenate(parts, axis=0)


def _head_tile(y):
    n = y.shape[0] // NSA_HPG
    lane = lax.broadcasted_iota(jnp.int32, (n, LANES), 1)
    hs = [y[i * n:(i + 1) * n] for i in range(NSA_HPG)]
    pairs = [jnp.where(lane < HEAD_DIM, hs[2 * i], pltpu.roll(hs[2 * i + 1], HEAD_DIM, 1)) for i in range(2)]
    return jnp.concatenate(pairs, axis=1)


def _cmp_kernel(q_ref, kc_ref, vc_ref, sm_ref, ovt_ref, oc_ref, selb_ref, flag_ref, imp_sc):
    g = pl.program_id(1)
    step_q0 = pl.program_id(2) * CMP_SUB * Q_TILE
    last_visible = (step_q0 + CMP_SUB * Q_TILE - CMP_LEN) // CMP_STRIDE
    chunks = last_visible // LANES + 1

    def attend(width):
        for sub in range(CMP_SUB):
            rows = pl.ds(sub * Q_TILE, Q_TILE)
            q = q_ref[0, :, rows, :].reshape(NSA_HPG * Q_TILE, LANES)
            oc, imp = _cmp_attend(q, kc_ref[0, 0, 0:width, :], vc_ref[0, 0, 0:width, :], sm_ref[0, rows, :],
                                  ovt_ref[:, 0:width], g, step_q0 + sub * Q_TILE)
            oc_ref[0, rows, :] = oc
            imp_sc[sub] = imp

    for v in range(1, N_CMP_PAD // LANES + 1):
        pl.when(chunks == v)(functools.partial(attend, v * LANES))

    for sub in range(CMP_SUB):
        selb, flag = _select_blocks(imp_sc[sub], step_q0 + sub * Q_TILE)
        selb_ref[0, 0, pl.ds(sub * Q_TILE, Q_TILE), :] = selb
        flag_ref[0, 0, sub] = flag


def _cmp_attend(q, kc, vc, sm, ovt, g, q0):
    width = kc.shape[0]
    s = _dot_nt(q, kc)
    r = lax.broadcasted_iota(jnp.int32, (NSA_HPG * Q_TILE, 1), 0) % Q_TILE
    n = lax.broadcasted_iota(jnp.int32, (1, width), 1)
    dc = (q0 + r) - (n * CMP_STRIDE + (CMP_LEN - 1))
    mask = (dc >= 0) & (n < N_CMP_PAD - 1)
    l = jnp.where(mask, s, NEG)
    m = jnp.max(l, axis=-1, keepdims=True)
    e = jnp.where(mask, jnp.exp2(l - m), 0.0)
    pc = e / jnp.maximum(jnp.sum(e, axis=-1, keepdims=True), 1e-30)
    oc = _dot(pc.astype(BF16), vc)
    oc = _head_tile(oc * _gate_rows(sm, g, 0))
    ps = pc[0:Q_TILE]
    for i in range(1, NSA_HPG):
        ps = ps + pc[i * Q_TILE:(i + 1) * Q_TILE]
    ps_hi = ps.astype(BF16)
    ps_lo = (ps - ps_hi.astype(F32)).astype(BF16)
    return oc, _dot_nt(ovt, ps_hi) + _dot_nt(ovt, ps_lo)


def _select_blocks(imp, q0):
    j = lax.broadcasted_iota(jnp.int32, imp.shape, 0)
    jf = j.astype(F32)
    t = q0 + lax.broadcasted_iota(jnp.int32, (1, Q_TILE), 1)
    cur = t // SEL_LEN
    forced = (j == 0) | (j == cur) | (j == cur - 1)
    v = jnp.where(j > cur, -FORCE, jnp.where(forced, FORCE, imp))
    sel = jnp.zeros(imp.shape, jnp.bool_)
    for _ in range(N_SEL):
        mx = jnp.max(v, axis=0, keepdims=True)
        idx = jnp.min(jnp.where(v == mx, jf, float(LANES)), axis=0, keepdims=True)
        pick = jf == idx
        sel = sel | pick
        v = jnp.where(pick, PICKED, v)
    live_t = jnp.where(sel & (j <= cur), 1.0, 0.0).astype(BF16)
    eye = (lax.broadcasted_iota(jnp.int32, imp.shape, 0) == lax.broadcasted_iota(jnp.int32, imp.shape, 1))
    live = _dot_nt(eye.astype(BF16), live_t)
    selb = jnp.where(live > 0.5, 0.0, NEG).astype(BF16)
    return selb, jnp.max(live, axis=0, keepdims=True).astype(jnp.int32)


def _cmp_attention(qa, kvc, sm, ov):
    B, H, S, _ = qa.shape
    G = NSA_KV_GROUPS
    nq = S // Q_TILE
    qt = CMP_SUB * Q_TILE
    return pl.pallas_call(
        _cmp_kernel,
        grid=(B, G, nq // CMP_SUB),
        in_specs=[
            pl.BlockSpec((1, NSA_HPG, qt, LANES), lambda b, g, i: (b, g, i, 0)),
            pl.BlockSpec((1, 1, N_CMP_PAD, LANES), lambda b, g, i: (b, g, 0, 0)),
            pl.BlockSpec((1, 1, N_CMP_PAD, LANES), lambda b, g, i: (b, 2 + g, 0, 0)),
            pl.BlockSpec((1, qt, LANES), lambda b, g, i: (b, i, 0)),
            pl.BlockSpec((LANES, N_CMP_PAD), lambda b, g, i: (0, 0)),
        ],
        out_specs=[
            pl.BlockSpec((1, qt, NSA_HPG * HEAD_DIM), lambda b, g, i: (b, i, g)),
            pl.BlockSpec((1, 1, qt, LANES), lambda b, g, i: (b, g, i, 0)),
            pl.BlockSpec((1, 1, CMP_SUB, 1, LANES), lambda b, g, i: (b, g, i, 0, 0)),
        ],
        out_shape=[
            jax.ShapeDtypeStruct((B, S, NSA_W), F32),
            jax.ShapeDtypeStruct((B, G, S, LANES), BF16),
            jax.ShapeDtypeStruct((B, G, nq, 1, LANES), jnp.int32),
        ],
        scratch_shapes=[pltpu.VMEM((CMP_SUB, LANES, Q_TILE), F32)],
        compiler_params=_cparams(("parallel", "parallel", "parallel")),
        name="cmp_attention",
    )(qa, kvc, kvc, sm, ov)


def _online_update(s, v, m_ref, acc_ref):
    m_old = m_ref[...]
    m_new = jnp.maximum(m_old, jnp.max(s, axis=-1, keepdims=True))
    chunks = [s[:, c * LANES:(c + 1) * LANES] - m_new for c in range(s.shape[1] // LANES)]
    p = jnp.exp2(jnp.concatenate(chunks, axis=1))
    acc_ref[...] = jnp.exp2(m_old - m_new) * acc_ref[...] + _dot(p.astype(BF16), v)
    m_ref[...] = m_new


def _normalized(acc):
    return acc / jnp.maximum(acc[:, EXT:EXT + 1], 1e-30)


def _attend_once(s, v):
    m = jnp.broadcast_to(jnp.max(s, axis=-1, keepdims=True), (s.shape[0], LANES))
    chunks = [s[:, c * LANES:(c + 1) * LANES] - m for c in range(s.shape[1] // LANES)]
    p = jnp.exp2(jnp.concatenate(chunks, axis=1))
    return _normalized(_dot(p.astype(BF16), v))


def _selwin_kernel(list_ref, cnt_ref, q_ref, ks_ref, vs_ref, kw_ref, vw_ref, selb_ref, oc_ref, sm_ref,
                   o_ref, m_a, acc_a, m_b, acc_b):
    b = pl.program_id(0)
    g = pl.program_id(1)
    qb = pl.program_id(2)
    nq = pl.num_programs(2)
    rows = NSA_HPG * SW_TILE
    q4 = q_ref[0].reshape(rows, LANES)
    q_aug = jnp.concatenate([q4, jnp.concatenate([selb_ref[0, 0]] * NSA_HPG, axis=0)], axis=1)
    r = lax.broadcasted_iota(jnp.int32, (rows, 1), 0) % SW_TILE
    c = lax.broadcasted_iota(jnp.int32, (1, K_TILE), 1)
    rel = r - c
    diag = qb * (SW_TILE // K_TILE)

    def sel_tile(kt, m_ref, acc_ref, causal=False, bias=None):
        start = pl.multiple_of(kt * K_TILE, K_TILE)
        s = _dot_nt(q_aug, ks_ref[0, 0, pl.ds(start, K_TILE), :])
        if bias is not None:
            s = s + bias
        if causal:
            s = jnp.where(rel + (qb * SW_TILE - kt * K_TILE) >= 0, s, NEG)
        _online_update(s, vs_ref[0, 0, pl.ds(start, K_TILE), :], m_ref, acc_ref)

    for m_ref, acc_ref in ((m_a, acc_a), (m_b, acc_b)):
        m_ref[...] = jnp.full(m_ref.shape, NEG, F32)
        acc_ref[...] = jnp.zeros(acc_ref.shape, F32)
    step = (b * NSA_KV_GROUPS + g) * nq + qb
    count = cnt_ref[step]
    base = step * MAX_TILES

    def body(p, carry):
        second = 2 * p + 1
        sel_tile(list_ref[base + 2 * p], m_a, acc_a)
        sel_tile(list_ref[base + jnp.minimum(second, MAX_TILES - 1)], m_b, acc_b,
                 bias=jnp.where(second < count, 0.0, NEG))
        return carry

    lax.fori_loop(0, (count + 1) // 2, body, 0)
    m_new = jnp.maximum(m_a[...], m_b[...])
    acc_a[...] = jnp.exp2(m_a[...] - m_new) * acc_a[...] + jnp.exp2(m_b[...] - m_new) * acc_b[...]
    m_a[...] = m_new
    for d in range(SW_TILE // K_TILE):
        sel_tile(diag + d, m_a, acc_a, causal=True)
    o_sel = _normalized(acc_a[...])

    span = WINDOW + SW_TILE
    wstart = pl.multiple_of(jnp.maximum(qb * SW_TILE - WINDOW, 0), K_TILE)
    dist = (qb * SW_TILE + r) - (wstart + lax.broadcasted_iota(jnp.int32, (1, span), 1))
    s = _dot_nt(q4, kw_ref[0, 0, pl.ds(wstart, span), :])
    s = jnp.where((dist >= 0) & (dist < WINDOW), s, NEG)
    o_win = _attend_once(s, vw_ref[0, 0, pl.ds(wstart, span), :])

    sm = sm_ref[0]
    y = _gate_rows(sm, g, 1) * o_sel + _gate_rows(sm, g, 2) * o_win
    o_ref[0] = (oc_ref[0] + _head_tile(y)).astype(BF16)


def _selwin_attention(tile_list, tile_count, qa, ksl, nkv, selb, ocg, sm):
    B, H, S, _ = qa.shape
    G = NSA_KV_GROUPS
    nq = S // SW_TILE
    rows = NSA_HPG * SW_TILE
    kv_spec = lambda piece: pl.BlockSpec((1, 1, S, LANES), lambda b, g, i, tl, tc: (b, piece + g, 0, 0))
    out_tile = pl.BlockSpec((1, SW_TILE, NSA_HPG * HEAD_DIM), lambda b, g, i, tl, tc: (b, i, g))
    grid_spec = pltpu.PrefetchScalarGridSpec(
        num_scalar_prefetch=2,
        grid=(B, G, nq),
        in_specs=[
            pl.BlockSpec((1, NSA_HPG, SW_TILE, LANES), lambda b, g, i, tl, tc: (b, g, i, 0)),
            pl.BlockSpec((1, 1, S, 2 * LANES), lambda b, g, i, tl, tc: (b, g, 0, 0)),
            kv_spec(0), kv_spec(2), kv_spec(4),
            pl.BlockSpec((1, 1, SW_TILE, LANES), lambda b, g, i, tl, tc: (b, g, i, 0)),
            out_tile,
            pl.BlockSpec((1, SW_TILE, LANES), lambda b, g, i, tl, tc: (b, i, 0)),
        ],
        out_specs=out_tile,
        scratch_shapes=[pltpu.VMEM((rows, LANES), F32)] * 4,
    )
    return pl.pallas_call(
        _selwin_kernel,
        grid_spec=grid_spec,
        out_shape=jax.ShapeDtypeStruct((B, S, NSA_W), BF16),
        compiler_params=_cparams(("parallel", "parallel", "arbitrary")),
        name="selwin_attention",
    )(tile_list, tile_count, qa, ksl, nkv, nkv, nkv, selb, ocg, sm)


def _fox_kernel(q_ref, k_ref, v_ref, o_ref, m_sc, acc_sc, *, tq):
    qi = pl.program_id(2)
    m_sc[...] = jnp.full(m_sc.shape, NEG, F32)
    acc_sc[...] = jnp.zeros(acc_sc.shape, F32)

    def tile(kt, width, causal):
        start = pl.multiple_of(kt * tq, tq)
        for hh in range(FOX_HPS):
            s = _dot_nt(q_ref[0, hh], k_ref[0, hh, pl.ds(start, width), :])
            if causal:
                r = lax.broadcasted_iota(jnp.int32, s.shape, 0)
                c = lax.broadcasted_iota(jnp.int32, s.shape, 1)
                s = jnp.where(r + (width - tq) >= c, s, NEG)
            _online_update(s, v_ref[0, hh, pl.ds(start, width), :], m_sc.at[hh], acc_sc.at[hh])

    def body(kp, carry):
        tile(2 * kp, 2 * tq, False)
        return carry

    lax.fori_loop(0, qi // 2, body, 0)

    @pl.when(qi % 2 == 1)
    def _():
        tile(qi - 1, 2 * tq, True)

    @pl.when(qi % 2 == 0)
    def _():
        tile(qi, tq, True)

    lane = lax.broadcasted_iota(jnp.int32, (tq, LANES), 1)
    o = [_normalized(acc_sc[hh]) for hh in range(FOX_HPS)]
    for pr in range(FOX_HPS // 2):
        o_ref[0, :, pr * LANES:(pr + 1) * LANES] = jnp.where(
            lane < HEAD_DIM, o[2 * pr], pltpu.roll(o[2 * pr + 1], HEAD_DIM, 1)).astype(BF16)


def _fox_attention(fq, fk, fv, tq=512):
    B, H, S, _ = fq.shape
    hps = FOX_HPS
    return pl.pallas_call(
        functools.partial(_fox_kernel, tq=tq),
        grid=(B, H // hps, S // tq),
        in_specs=[
            pl.BlockSpec((1, hps, tq, LANES), lambda b, h, i: (b, h, i, 0)),
            pl.BlockSpec((1, hps, S, LANES), lambda b, h, i: (b, h, 0, 0)),
            pl.BlockSpec((1, hps, S, LANES), lambda b, h, i: (b, h, 0, 0)),
        ],
        out_specs=pl.BlockSpec((1, tq, hps * HEAD_DIM), lambda b, h, i: (b, i, h)),
        out_shape=jax.ShapeDtypeStruct((B, S, FOX_W), BF16),
        scratch_shapes=[
            pltpu.VMEM((hps, tq, LANES), F32),
            pltpu.VMEM((hps, tq, LANES), F32),
        ],
        compiler_params=_cparams(("parallel", "parallel", "arbitrary")),
        name="fox_attention",
    )(fq, fk, fv)


def _merge_kernel(ya_ref, yb_ref, mg_ref, x_ref, mod_ref, gpost_ref, gpre_ref,
                  wa_ref, wb_ref, wo_ref, wrh_ref, wrl_ref, br_ref, stri_ref,
                  x1_ref, h2_ref, rt_ref, cnt_ref):
    D = D_MODEL

    @pl.when((pl.program_id(0) == 0) & (pl.program_id(1) == 0))
    def _():
        cnt_ref[...] = jnp.zeros(cnt_ref.shape, F32)

    a = _dot(ya_ref[0], wa_ref[...])
    bq = _dot(yb_ref[0], wb_ref[...])
    mg = mg_ref[0]
    u = mg[:, :D].astype(F32) * a + mg[:, D:].astype(F32) * bq
    mixed = _dot(u.astype(BF16), wo_ref[...])
    x1 = x_ref[0] + mod_ref[0, 2:3, :] * _rms(mixed, gpost_ref[...])
    x1_ref[0] = x1
    h2 = _rms(x1, gpre_ref[...]) * (1.0 + mod_ref[0, 4:5, :]) + mod_ref[0, 3:4, :]
    hi = h2.astype(BF16)
    lo = (h2 - hi.astype(F32)).astype(BF16)
    h2_ref[0] = h2
    lg = _dot(hi, wrh_ref[...]) + _dot(lo, wrh_ref[...]) + _dot(hi, wrl_ref[...]) + br_ref[...]

    lane = lax.broadcasted_iota(jnp.int32, lg.shape, 1)
    lanef = lane.astype(F32)
    no_lane = float(LANES)
    is_g = lane < N_EXPERT_GROUPS
    gl = jnp.where(is_g, lg, NEG)
    gmax = jnp.max(gl, axis=-1, keepdims=True)
    pg_top = 1.0 / jnp.sum(jnp.where(is_g, jnp.exp(gl - gmax), 0.0), axis=-1, keepdims=True)
    g_idx = jnp.min(jnp.where(is_g & (gl == gmax), lanef, no_lane), axis=-1, keepdims=True)
    in_grp = ((lane >= N_EXPERT_GROUPS) & (lane < N_EXPERT_GROUPS + N_EXPERTS)
              & (((lane - N_EXPERT_GROUPS) // EXPERTS_PER_GROUP).astype(F32) == g_idx))
    le = jnp.where(in_grp, lg, NEG)
    m1 = jnp.max(le, axis=-1, keepdims=True)
    i1 = jnp.min(jnp.where(in_grp & (le == m1), lanef, no_lane), axis=-1, keepdims=True)
    rest = in_grp & (lanef != i1)
    le2 = jnp.where(rest, lg, NEG)
    m2 = jnp.max(le2, axis=-1, keepdims=True)
    i2 = jnp.min(jnp.where(rest & (le2 == m2), lanef, no_lane), axis=-1, keepdims=True)
    e21 = jnp.exp(m2 - m1)
    w1 = pg_top / (1.0 + e21)
    w2 = w1 * e21
    pick1 = lanef == i1
    pick2 = lanef == i2
    onehot = jnp.where(pick1 | pick2, 1.0, 0.0)
    before = cnt_ref[...] + _dot(stri_ref[...], onehot.astype(BF16))
    rank1 = jnp.sum(jnp.where(pick1, before, 0.0), axis=-1, keepdims=True)
    rank2 = jnp.sum(jnp.where(pick2, before, 0.0), axis=-1, keepdims=True)
    cnt_ref[...] = cnt_ref[...] + jnp.sum(onehot, axis=0, keepdims=True)
    fields = {RT_EXPERT: i1 - N_EXPERT_GROUPS, RT_EXPERT + 1: i2 - N_EXPERT_GROUPS,
              RT_RANK: rank1, RT_RANK + 1: rank2, RT_WEIGHT: w1, RT_WEIGHT + 1: w2}
    rt = jnp.zeros(lg.shape, F32)
    for k, f in fields.items():
        rt = jnp.where(lane == k, f, rt)
    rt_ref[0] = rt


def _merge(ya, yb, mg, x, mod, gpost, gpre, wa, wb, wo, wrh, wrl, br, stri):
    B, S, D = x.shape
    tm = MERGE_TILE
    c2 = lambda b, i: (0, 0)
    row = lambda w: pl.BlockSpec((1, tm, w), lambda b, i: (b, i, 0))
    return pl.pallas_call(
        _merge_kernel,
        grid=(B, S // tm),
        in_specs=[
            row(NSA_W), row(FOX_W), row(2 * D), row(D),
            pl.BlockSpec((1, 6, D), lambda b, i: (b, 0, 0)),
            pl.BlockSpec((1, D), c2), pl.BlockSpec((1, D), c2),
            pl.BlockSpec((NSA_W, D), c2), pl.BlockSpec((FOX_W, D), c2), pl.BlockSpec((D, D), c2),
            pl.BlockSpec((D, LANES), c2), pl.BlockSpec((D, LANES), c2), pl.BlockSpec((1, LANES), c2),
            pl.BlockSpec((tm, tm), c2),
        ],
        out_specs=[row(D), row(D), row(LANES), pl.BlockSpec((1, LANES), c2)],
        out_shape=[
            jax.ShapeDtypeStruct((B, S, D), F32),
            jax.ShapeDtypeStruct((B, S, D), F32),
            jax.ShapeDtypeStruct((B, S, LANES), F32),
            jax.ShapeDtypeStruct((1, LANES), F32),
        ],
        compiler_params=_cparams(("arbitrary", "arbitrary")),
        name="merge",
    )(ya, yb, mg, x, mod, gpost, gpre, wa, wb, wo, wrh, wrl, br, stri)


def _expert_kernel(be_ref, na_ref, tok_ref, h_hbm, wg_ref, wu_ref, wd_ref, o_ref, *scratch):
    i = pl.program_id(0)
    n_active = na_ref[0]
    last_block = pl.num_programs(0) - 1
    ring = GATHER_AHEAD + 1
    bufs = scratch[:ring]
    wg_b, wu_b, wd_b, sem = scratch[ring:]

    def row_copy(blk, r, sl):
        tok = tok_ref[blk * MOE_TILE + r]
        return pltpu.make_async_copy(h_hbm.at[pl.ds(tok, 1)], bufs[sl].at[pl.ds(r, 1)], sem.at[sl])

    def wait_rows(sl):
        pltpu.make_async_copy(h_hbm.at[pl.ds(0, MOE_TILE)], bufs[sl], sem.at[sl]).wait()

    @pl.when(i == 0)
    def _():
        for ahead in range(GATHER_AHEAD):
            def body(r, carry, ahead=ahead):
                row_copy(jnp.minimum(ahead, last_block), r, ahead).start()
                return carry
            lax.fori_loop(0, MOE_TILE, body, 0, unroll=8)

    @pl.when((i == 0) | (be_ref[i] != be_ref[jnp.maximum(i - 1, 0)]))
    def _():
        wg_b[...] = wg_ref[0].astype(BF16)
        wu_b[...] = wu_ref[0].astype(BF16)
        wd_b[...] = wd_ref[0].astype(BF16)

    def step(sl):
        wait_rows(sl)
        nxt = jnp.minimum(i + GATHER_AHEAD, last_block)
        nxt_sl = (sl + GATHER_AHEAD) % ring
        for r in range(MOE_TILE):
            row_copy(nxt, r, nxt_sl).start(priority=r % 2)
        x = bufs[sl][...].astype(BF16)
        gate = _dot(x, wg_b[...])
        up = _dot(x, wu_b[...])
        mid = (gate * jax.nn.sigmoid(gate) * up).astype(BF16)
        o_ref[...] = _dot(mid, wd_b[...])

        @pl.when(i == n_active - 1)
        def _():
            for ahead in range(1, GATHER_AHEAD + 1):
                wait_rows((sl + ahead) % ring)

    for sl in range(ring):
        pl.when((i % ring == sl) & (i < n_active))(functools.partial(step, sl))

    @pl.when(i >= n_active)
    def _():
        o_ref[...] = jnp.zeros(o_ref.shape, o_ref.dtype)


def _experts(block_expert, n_active, buf_tok, h2, wg, wu, wd):
    cap = buf_tok.shape[0]
    D = D_MODEL
    nblk = cap // MOE_TILE
    grid_spec = pltpu.PrefetchScalarGridSpec(
        num_scalar_prefetch=3,
        grid=(nblk,),
        in_specs=[
            pl.BlockSpec(memory_space=pl.ANY),
            pl.BlockSpec((1, D, D_EXPERT), lambda i, be, na, tok: (be[i], 0, 0)),
            pl.BlockSpec((1, D, D_EXPERT), lambda i, be, na, tok: (be[i], 0, 0)),
            pl.BlockSpec((1, D_EXPERT, D), lambda i, be, na, tok: (be[i], 0, 0)),
        ],
        out_specs=pl.BlockSpec((MOE_TILE, D), lambda i, be, na, tok: (i, 0)),
        scratch_shapes=[
            *([pltpu.VMEM((MOE_TILE, D), F32)] * (GATHER_AHEAD + 1)),
            pltpu.VMEM((D, D_EXPERT), BF16),
            pltpu.VMEM((D, D_EXPERT), BF16),
            pltpu.VMEM((D_EXPERT, D), BF16),
            pltpu.SemaphoreType.DMA((GATHER_AHEAD + 1,)),
        ],
    )
    return pl.pallas_call(
        _expert_kernel,
        grid_spec=grid_spec,
        out_shape=jax.ShapeDtypeStruct((cap, D), F32),
        compiler_params=_cparams(("arbitrary",)),
        name="experts",
    )(block_expert, n_active, buf_tok, h2, wg, wu, wd)


def _final_kernel(dest_ref, x1_ref, rt_ref, mod_ref, g_ref, y_hbm, o_ref, *scratch):
    j = pl.program_id(0)
    last_tile = pl.num_programs(0) - 1
    tm = o_ref.shape[0]
    ring = GATHER_AHEAD + 1
    bufs = tuple(scratch[EXPERT_TOP_K * sl:EXPERT_TOP_K * (sl + 1)] for sl in range(ring))
    sem = scratch[-1]

    def row_copy(tile, r, k, sl):
        row = dest_ref[(tile * tm + r) * EXPERT_TOP_K + k]
        return pltpu.make_async_copy(y_hbm.at[pl.ds(row, 1)], bufs[sl][k].at[pl.ds(r, 1)], sem.at[sl])

    def wait_rows(sl):
        for k in range(EXPERT_TOP_K):
            pltpu.make_async_copy(y_hbm.at[pl.ds(0, tm)], bufs[sl][k], sem.at[sl]).wait()

    @pl.when(j == 0)
    def _():
        for ahead in range(GATHER_AHEAD):
            def body(r, carry, ahead=ahead):
                for k in range(EXPERT_TOP_K):
                    row_copy(jnp.minimum(ahead, last_tile), r, k, ahead).start()
                return carry
            lax.fori_loop(0, tm, body, 0, unroll=4)

    def step(sl):
        wait_rows(sl)
        nxt = jnp.minimum(j + GATHER_AHEAD, last_tile)
        nxt_sl = (sl + GATHER_AHEAD) % ring
        for r in range(tm):
            for k in range(EXPERT_TOP_K):
                row_copy(nxt, r, k, nxt_sl).start(priority=k)
        rt = rt_ref[...]
        lane = lax.broadcasted_iota(jnp.int32, rt.shape, 1)
        w0 = jnp.sum(jnp.where(lane == RT_WEIGHT, rt, 0.0), axis=-1, keepdims=True)
        w1 = jnp.sum(jnp.where(lane == RT_WEIGHT + 1, rt, 0.0), axis=-1, keepdims=True)
        y = w0 * bufs[sl][0][...] + w1 * bufs[sl][1][...]
        o_ref[...] = x1_ref[...] + mod_ref[0, 5:6, :] * _rms(y, g_ref[...])

        @pl.when(j == last_tile)
        def _():
            for ahead in range(1, GATHER_AHEAD + 1):
                wait_rows((sl + ahead) % ring)

    for sl in range(ring):
        pl.when(j % ring == sl)(functools.partial(step, sl))


def _final(dest, x1, rt, mod, g, yb, tiles_per_batch):
    T, D = x1.shape
    tm = FINAL_TILE
    grid_spec = pltpu.PrefetchScalarGridSpec(
        num_scalar_prefetch=1,
        grid=(T // tm,),
        in_specs=[
            pl.BlockSpec((tm, D), lambda j, d: (j, 0)),
            pl.BlockSpec((tm, LANES), lambda j, d: (j, 0)),
            pl.BlockSpec((1, 6, D), lambda j, d: (j // tiles_per_batch, 0, 0)),
            pl.BlockSpec((1, D), lambda j, d: (0, 0)),
            pl.BlockSpec(memory_space=pl.ANY),
        ],
        out_specs=pl.BlockSpec((tm, D), lambda j, d: (j, 0)),
        scratch_shapes=([pltpu.VMEM((tm, D), F32)] * (EXPERT_TOP_K * (GATHER_AHEAD + 1))
                        + [pltpu.SemaphoreType.DMA((GATHER_AHEAD + 1,))]),
    )
    return pl.pallas_call(
        _final_kernel,
        grid_spec=grid_spec,
        out_shape=jax.ShapeDtypeStruct((T, D), F32),
        compiler_params=_cparams(("arbitrary",)),
        name="final",
    )(dest, x1, rt, mod, g, yb)


def _overlap_matrix():
    n = np.arange(N_CMP_PAD)[:, None]
    j = np.arange(LANES)[None, :]
    start = n * CMP_STRIDE
    ov = (start < j * SEL_LEN + SEL_LEN) & (start + CMP_LEN - 1 >= j * SEL_LEN) & (n < N_CMP_PAD - 1)
    return jnp.asarray(ov.T.astype(np.float32), dtype=BF16)


def _pad_cols(w, width=LANES):
    return jnp.pad(w, ((0, 0), (0, width - w.shape[1])))


def _dispatch_plan(rt, cnt, T):
    expert = rt[:, RT_EXPERT:RT_EXPERT + EXPERT_TOP_K].astype(jnp.int32)
    rank = rt[:, RT_RANK:RT_RANK + EXPERT_TOP_K].astype(jnp.int32)
    weight = rt[:, RT_WEIGHT:RT_WEIGHT + EXPERT_TOP_K]
    counts = cnt[0, N_EXPERT_GROUPS:N_EXPERT_GROUPS + N_EXPERTS].astype(jnp.int32)
    padded = (counts + MOE_TILE - 1) // MOE_TILE * MOE_TILE
    pad_end = jnp.cumsum(padded)
    pad_start = pad_end - padded
    onehot = expert[:, :, None] == jnp.arange(N_EXPERTS)[None, None, :]
    dest = jnp.sum(jnp.where(onehot, pad_start[None, None, :], 0), axis=-1) + rank
    A = T * EXPERT_TOP_K
    cap = -(-(A + N_EXPERTS * (MOE_TILE - 1)) // MOE_TILE) * MOE_TILE
    nblk = cap // MOE_TILE
    n_active = (pad_end[-1] // MOE_TILE).astype(jnp.int32)
    blk = jnp.arange(nblk) * MOE_TILE
    block_expert = jnp.minimum(jnp.sum(pad_end[None, :] <= blk[:, None], axis=1), N_EXPERTS - 1)
    last = jnp.max(jnp.where(jnp.arange(nblk) < n_active, block_expert, 0))
    block_expert = jnp.where(jnp.arange(nblk) < n_active, block_expert, last).astype(jnp.int32)
    tok = jnp.arange(A, dtype=jnp.int32) // EXPERT_TOP_K
    buf_tok = jnp.zeros((cap,), jnp.int32).at[dest.reshape(A)].set(tok)
    return weight, dest, buf_tok, block_expert, n_active.reshape(1)


def kernel(x, c, w_ada, b_ada, g_pre_mix, g_post_mix, g_pre_ffn, g_post_ffn, w_in, b_forget,
           cmp_pe_k, cmp_w1_k, cmp_w2_k, cmp_pe_v, cmp_w1_v, cmp_w2_v,
           w_o_nsa, w_o_fox, w_out, w_router_group, b_router_group, w_router_expert, b_router_expert,
           w_exp_gate, w_exp_up, w_exp_down):
    B, S, D = x.shape
    T = B * S
    depth = w_ada.shape[0]
    ov = _overlap_matrix()
    tri = jnp.asarray(np.tril(np.ones((IN_TILE, IN_TILE), np.float32)), dtype=BF16)
    stri = jnp.asarray(np.tril(np.ones((MERGE_TILE, MERGE_TILE), np.float32), -1), dtype=BF16)
    row_feat = _row_features(S)
    placement = _placement()
    cmp_ext = _cmp_key_ext()
    for l in range(depth):
        mod = _adaln(c, w_ada[l], b_ada[l].reshape(1, 6 * D)).reshape(B, 6, D)
        w_qa, w_kva, w_gl, w_fox, w_f, w_mg = jnp.split(w_in[l], IN_SPLITS, axis=-1)
        w_big = jnp.concatenate([w_qa, w_kva, w_fox, w_mg], axis=1).astype(BF16)
        w_small = _pad_cols(jnp.concatenate([w_gl, w_f], axis=1)).astype(BF16)
        bf_pad = jnp.pad(b_forget[l], (F_LANE, LANES - F_LANE - FOX_HEADS)).reshape(1, LANES)
        qa, ckv, ksl, nkv, fq, fk, fv, mg, sm = _inproj(
            x, mod, g_pre_mix[l].reshape(1, D), w_big, w_small, bf_pad, tri, row_feat, placement)

        half = CMP_LEN // 2
        pe = jnp.stack([cmp_pe_k[l], cmp_pe_v[l]]).reshape(2, 2, 1, half * HEAD_DIM)
        w1 = jnp.stack([cmp_w1_k[l], cmp_w1_v[l]]).reshape(2, 2, half * HEAD_DIM, HEAD_DIM).astype(BF16)
        w2 = jnp.pad(jnp.stack([cmp_w2_k[l], cmp_w2_v[l]]), ((0, 0), (0, 0), (0, LANES - HEAD_DIM))).astype(BF16)
        kvc = _compress(ckv.reshape(B, 4, S // CMP_STRIDE, CMP_STRIDE * HEAD_DIM), pe, w1, w2, cmp_ext)
        ocg, selb, flags = _cmp_attention(qa, kvc, sm, ov)
        nq = S // SW_TILE
        per_tile = K_TILE // SEL_LEN
        tile_any = jnp.max(flags.reshape(B, NSA_KV_GROUPS, nq, SW_TILE // Q_TILE, MAX_TILES, per_tile), axis=(3, 5))
        tile_id = jnp.arange(MAX_TILES)
        diag = (jnp.arange(nq) * (SW_TILE // K_TILE))[:, None]
        active = (tile_any > 0) & (tile_id < diag)
        slot = jnp.cumsum(active, axis=-1) - 1
        hit = active[..., :, None] & (slot[..., :, None] == tile_id)
        tile_list = jnp.sum(jnp.where(hit, tile_id[:, None], 0), axis=-2).astype(jnp.int32).reshape(-1)
        tile_count = jnp.sum(active, axis=-1).astype(jnp.int32).reshape(-1)
        y_a = _selwin_attention(tile_list, tile_count, qa, ksl, nkv, selb, ocg, sm)

        y_b = _fox_attention(fq, fk, fv)

        w_r = _pad_cols(jnp.concatenate([w_router_group[l], w_router_expert[l]], axis=1))
        w_rh = w_r.astype(BF16)
        w_rl = (w_r - w_rh.astype(F32)).astype(BF16)
        b_r = _pad_cols(jnp.concatenate([b_router_group[l], b_router_expert[l]]).reshape(1, -1))
        x1, h2, rt, cnt = _merge(y_a, y_b, mg, x, mod, g_post_mix[l].reshape(1, D), g_pre_ffn[l].reshape(1, D),
                                 w_o_nsa[l].astype(BF16), w_o_fox[l].astype(BF16), w_out[l].astype(BF16),
                                 w_rh, w_rl, b_r, stri)

        weight, dest, buf_tok, block_expert, n_active = _dispatch_plan(rt.reshape(T, LANES), cnt, T)
        yb = _experts(block_expert, n_active, buf_tok, h2.reshape(T, D), w_exp_gate[l], w_exp_up[l], w_exp_down[l])
        x = _final(dest.reshape(T * EXPERT_TOP_K), x1.reshape(T, D), rt.reshape(T, LANES), mod,
                   g_post_ffn[l].reshape(1, D), yb, S // FINAL_TILE).reshape(B, S, D)
    return x
```

```python
import functools

import ml_dtypes
import numpy as np
import jax
import jax.numpy as jnp
from jax import lax
from jax.experimental import pallas as pl
from jax.experimental.pallas import tpu as pltpu

D_MODEL = 1024
HEAD_DIM = 64
NSA_HEADS = 8
NSA_KV_GROUPS = 2
NSA_HPG = NSA_HEADS // NSA_KV_GROUPS
FOX_HEADS = 8
CMP_LEN = 32
CMP_STRIDE = 16
SEL_LEN = 64
N_SEL = 16
WINDOW = 512
N_EXPERT_GROUPS = 4
EXPERTS_PER_GROUP = 8
N_EXPERTS = N_EXPERT_GROUPS * EXPERTS_PER_GROUP
EXPERT_TOP_K = 2
D_EXPERT = D_MODEL // 2
NORM_EPS = 1e-6
NEG = -1e30
FORCE = 1e9
LOG2E = 1.4426950408889634

NSA_W = NSA_HEADS * HEAD_DIM
NSA_KV_W = NSA_KV_GROUPS * HEAD_DIM
FOX_W = FOX_HEADS * HEAD_DIM
IN_SIZES = (NSA_W, 6 * NSA_KV_W, 3 * NSA_HEADS, 3 * FOX_W, FOX_HEADS, 2 * D_MODEL)
IN_SPLITS = tuple(int(v) for v in np.cumsum(IN_SIZES)[:-1])

LANES = 128
Q_TILE = 128
K_TILE = 256
SW_TILE = 256
N_CMP_PAD = 512
MOE_TILE = 256
IN_TILE = 512
MERGE_TILE = 512
FINAL_TILE = 256
FOX_HPS = 4
CMP_SUB = 8
MAX_TILES = 32
GATHER_AHEAD = 3
V7X_VMEM_BYTES = 64 * 1024 * 1024
VMEM_LIMIT = V7X_VMEM_BYTES - 8 * 1024 * 1024
RT_EXPERT, RT_RANK, RT_WEIGHT = 0, 2, 4
PICKED = -3e38

F_LANE = 3 * NSA_HEADS
U_LANE = 64
ONE_LANE = 88
A_LANE = 89
B_LANE = 90
EXT = HEAD_DIM
G_FQ, G_FK, G_NQ, G_NK, N_GROUPS = 0, 8, 16, 24, 25

F32 = jnp.float32
BF16 = jnp.bfloat16


def _dot(a, b):
    return jnp.dot(a, b, preferred_element_type=F32)


def _dot_nt(a, b):
    return lax.dot_general(a, b, (((1,), (1,)), ((), ())), preferred_element_type=F32)


def _rms(x, g):
    return x * lax.rsqrt(jnp.mean(x * x, axis=-1, keepdims=True) + NORM_EPS) * g


def _cparams(sem):
    return pltpu.CompilerParams(dimension_semantics=sem, vmem_limit_bytes=VMEM_LIMIT)


def _split3(x):
    hi = x.astype(BF16).astype(F32)
    r = x - hi
    mid = r.astype(BF16).astype(F32)
    lo = (r - mid).astype(BF16).astype(F32)
    return hi, mid, lo


def _np_split3(x):
    x = np.asarray(x, np.float32)
    hi = x.astype(ml_dtypes.bfloat16).astype(np.float32)
    r = x - hi
    mid = r.astype(ml_dtypes.bfloat16).astype(np.float32)
    lo = (r - mid).astype(ml_dtypes.bfloat16).astype(np.float32)
    return hi, mid, lo


def _alibi_c():
    slopes = np.exp2(-8.0 * np.arange(1, NSA_HEADS + 1, dtype=np.float32) / NSA_HEADS).astype(np.float32)
    return slopes * np.float32(LOG2E)


def _row_features(S):
    t = np.arange(S, dtype=np.float32)
    c = _alibi_c()
    rs = np.zeros((S, LANES), np.float32)
    for h in range(NSA_HEADS):
        for j, term in enumerate(_np_split3(c[h] * t)):
            rs[:, U_LANE + 8 * j + h] = -term
    rs[:, ONE_LANE] = 1.0
    rs[:, A_LANE] = np.floor(t / LANES)
    rs[:, B_LANE] = t % LANES
    return jnp.asarray(rs, dtype=BF16)


def _placement():
    c = _alibi_c()
    p = np.zeros((LANES, N_GROUPS * LANES), np.float32)
    for h in range(FOX_HEADS):
        q0 = (G_FQ + h) * LANES + EXT
        k0 = (G_FK + h) * LANES + EXT
        for j in range(3):
            p[ONE_LANE, q0 + j] = -1.0
            p[F_LANE + 8 * j + h, q0 + 3 + j] = 1.0
            p[F_LANE + 8 * j + h, k0 + j] = 1.0
            p[ONE_LANE, k0 + 3 + j] = 1.0
    for h in range(NSA_HEADS):
        q0 = (G_NQ + h) * LANES + EXT
        c128 = _np_split3(c[h] * np.float32(LANES))
        c1 = _np_split3(c[h])
        for j in range(3):
            p[U_LANE + 8 * j + h, q0 + j] = 1.0
            p[ONE_LANE, q0 + 3 + j] = c128[j]
            p[ONE_LANE, q0 + 6 + j] = c1[j]
    k0 = G_NK * LANES + EXT
    for j in range(3):
        p[ONE_LANE, k0 + j] = 1.0
        p[A_LANE, k0 + 3 + j] = 1.0
        p[B_LANE, k0 + 6 + j] = 1.0
    return jnp.asarray(p, dtype=BF16)


def _cmp_key_ext():
    pos = np.arange(N_CMP_PAD, dtype=np.float32) * CMP_STRIDE + (CMP_LEN - 1)
    e = np.zeros((2, N_CMP_PAD, LANES), np.float32)
    for j in range(3):
        e[0, :, EXT + j] = 1.0
        e[0, :, EXT + 3 + j] = np.floor(pos / LANES)
        e[0, :, EXT + 6 + j] = pos % LANES
    return jnp.asarray(e, dtype=BF16)


def _adaln_kernel(c_ref, w_ref, b_ref, o_ref):
    c = c_ref[...]
    act = (c * jax.nn.sigmoid(c)).astype(BF16)
    o_ref[...] = _dot(act, w_ref[...].astype(BF16)) + b_ref[...]


def _adaln(c, w, b):
    B, D = c.shape
    n = w.shape[1]
    return pl.pallas_call(
        _adaln_kernel,
        grid=(n // D,),
        in_specs=[
            pl.BlockSpec((B, D), lambda j: (0, 0)),
            pl.BlockSpec((D, D), lambda j: (0, j)),
            pl.BlockSpec((1, D), lambda j: (0, j)),
        ],
        out_specs=pl.BlockSpec((B, D), lambda j: (0, j)),
        out_shape=jax.ShapeDtypeStruct((B, n), F32),
        compiler_params=_cparams(("parallel",)),
        name="adaln",
    )(c, w, b)


def _inproj_kernel(x_ref, mod_ref, g_ref, wb_ref, ws_ref, bf_ref, tri_ref, rs_ref, p_ref,
                   qa_ref, ckv_ref, ksl_ref, nkv_ref, fq_ref, fk_ref, fv_ref, mg_ref, sm_ref, carry_sc):
    i = pl.program_id(1)
    tm = x_ref.shape[1]
    x = x_ref[0]
    h = _rms(x, g_ref[...]) * (1.0 + mod_ref[0, 1:2, :]) + mod_ref[0, 0:1, :]
    hb = h.astype(BF16)
    lane = lax.broadcasted_iota(jnp.int32, (tm, LANES), 1)
    lower = lane < HEAD_DIM
    ones_col = (lane == EXT).astype(F32)

    z = _dot(hb, ws_ref[...]) + bf_ref[...]
    logsig = jnp.minimum(z, 0.0) - jnp.log1p(jnp.exp(-jnp.abs(z)))
    sm_ref[0] = jnp.where(lane < F_LANE, jax.nn.sigmoid(z), logsig)

    @pl.when(i == 0)
    def _():
        carry_sc[...] = jnp.zeros(carry_sc.shape, F32)

    is_f = (lane >= F_LANE) & (lane < F_LANE + FOX_HEADS)
    l_hi, l_mid, l_lo = _split3(jnp.where(is_f, logsig, 0.0))
    tri = tri_ref[...]
    cum = carry_sc[...] + _dot(tri, l_hi.astype(BF16)) + _dot(tri, l_mid.astype(BF16)) + _dot(tri, l_lo.astype(BF16))
    carry_sc[...] = cum[tm - 1:tm, :]
    f_hi, f_mid, f_lo = _split3(cum * LOG2E)
    feat = (f_hi + pltpu.roll(f_mid, 8, 1) + pltpu.roll(f_lo, 16, 1) + rs_ref[...].astype(F32)).astype(BF16)

    ext_pairs = {}

    def ext(group):
        first = group - group % 2
        if first not in ext_pairs:
            width = min(2, N_GROUPS - first) * LANES
            ext_pairs[first] = _dot(feat, p_ref[:, first * LANES:first * LANES + width])
        off = (group - first) * LANES
        return ext_pairs[first][:, off:off + LANES]

    def piece(acc, idx, extra):
        pair = acc[:, (idx // 2) * LANES:(idx // 2 + 1) * LANES]
        if idx % 2:
            pair = pltpu.roll(pair, HEAD_DIM, 1)
        return jnp.where(lower, pair, extra).astype(BF16)

    qscale = (HEAD_DIM ** -0.5) * LOG2E
    acc = _dot(hb, wb_ref[:, 0:NSA_W]) * qscale
    for hd in range(NSA_HEADS):
        qa_ref[0, hd] = piece(acc, hd, ext(G_NQ + hd))
    off = NSA_W
    acc = _dot(hb, wb_ref[:, off:off + 6 * NSA_KV_W])
    for pc in range(4):
        ckv_ref[0, pc] = acc[:, pc * HEAD_DIM:(pc + 1) * HEAD_DIM].astype(BF16)
    ext_k = ext(G_NK)
    t = i * tm + lax.broadcasted_iota(jnp.int32, (tm, LANES), 0)
    block_onehot = (lane == t // SEL_LEN).astype(BF16)
    for g in range(NSA_KV_GROUPS):
        ksl_ref[0, g, :, 0:LANES] = piece(acc, 4 + g, ext_k)
        ksl_ref[0, g, :, LANES:2 * LANES] = block_onehot
        nkv_ref[0, g] = piece(acc, 6 + g, ones_col)
        nkv_ref[0, 2 + g] = piece(acc, 8 + g, ext_k)
        nkv_ref[0, 4 + g] = piece(acc, 10 + g, ones_col)
    off += 6 * NSA_KV_W
    acc = _dot(hb, wb_ref[:, off:off + FOX_W]) * qscale
    for hd in range(FOX_HEADS):
        fq_ref[0, hd] = piece(acc, hd, ext(G_FQ + hd))
    off += FOX_W
    acc = _dot(hb, wb_ref[:, off:off + FOX_W])
    for hd in range(FOX_HEADS):
        fk_ref[0, hd] = piece(acc, hd, ext(G_FK + hd))
    off += FOX_W
    acc = _dot(hb, wb_ref[:, off:off + FOX_W])
    for hd in range(FOX_HEADS):
        fv_ref[0, hd] = piece(acc, hd, ones_col)
    off += FOX_W
    for c in range(4):
        acc = _dot(hb, wb_ref[:, off + c * 512: off + (c + 1) * 512])
        mg_ref[0, :, c * 512:(c + 1) * 512] = jax.nn.sigmoid(acc).astype(BF16)


def _inproj(x, mod, g, wb, ws, bfp, tri, rs, pm):
    B, S, D = x.shape
    tm = IN_TILE
    nb = wb.shape[1]
    const2 = lambda b, i: (0, 0)
    heads = lambda n: pl.BlockSpec((1, n, tm, LANES), lambda b, i: (b, 0, i, 0))
    hshape = lambda n: jax.ShapeDtypeStruct((B, n, S, LANES), BF16)
    return pl.pallas_call(
        _inproj_kernel,
        grid=(B, S // tm),
        in_specs=[
            pl.BlockSpec((1, tm, D), lambda b, i: (b, i, 0)),
            pl.BlockSpec((1, 6, D), lambda b, i: (b, 0, 0)),
            pl.BlockSpec((1, D), const2),
            pl.BlockSpec((D, nb), const2),
            pl.BlockSpec((D, LANES), const2),
            pl.BlockSpec((1, LANES), const2),
            pl.BlockSpec((tm, tm), const2),
            pl.BlockSpec((tm, LANES), lambda b, i: (i, 0)),
            pl.BlockSpec((LANES, N_GROUPS * LANES), const2),
        ],
        out_specs=[
            heads(NSA_HEADS),
            pl.BlockSpec((1, 4, tm, HEAD_DIM), lambda b, i: (b, 0, i, 0)),
            pl.BlockSpec((1, NSA_KV_GROUPS, tm, 2 * LANES), lambda b, i: (b, 0, i, 0)),
            heads(6), heads(FOX_HEADS), heads(FOX_HEADS), heads(FOX_HEADS),
            pl.BlockSpec((1, tm, 2 * D), lambda b, i: (b, i, 0)),
            pl.BlockSpec((1, tm, LANES), lambda b, i: (b, i, 0)),
        ],
        out_shape=[
            hshape(NSA_HEADS),
            jax.ShapeDtypeStruct((B, 4, S, HEAD_DIM), BF16),
            jax.ShapeDtypeStruct((B, NSA_KV_GROUPS, S, 2 * LANES), BF16),
            hshape(6), hshape(FOX_HEADS), hshape(FOX_HEADS), hshape(FOX_HEADS),
            jax.ShapeDtypeStruct((B, S, 2 * D), BF16),
            jax.ShapeDtypeStruct((B, S, LANES), F32),
        ],
        scratch_shapes=[pltpu.VMEM((1, LANES), F32)],
        compiler_params=_cparams(("parallel", "arbitrary")),
        name="inproj",
    )(x, mod, g, wb, ws, bfp, tri, rs, pm)


def _compress_kernel(x_ref, pe_ref, w1_ref, w2_ref, e_ref, o_ref):
    x = x_ref[0, 0].astype(F32)
    x_lo = (x + pe_ref[0, 0]).astype(BF16)
    x_hi = (x + pe_ref[0, 1]).astype(BF16)
    y_lo = _dot(x_lo, w1_ref[0, 0])
    y_hi = _dot(x_hi, w1_ref[0, 1])
    n = y_hi.shape[0]
    hid = y_lo + pltpu.roll(y_hi, n - 1, 0)
    hid = jax.nn.gelu(hid)
    o_ref[0, 0] = (_dot(hid.astype(BF16), w2_ref[0]) + e_ref[0].astype(F32)).astype(BF16)


def _compress(kv_rows, pe, w1, w2, e):
    B = kv_rows.shape[0]
    R, C = kv_rows.shape[2], kv_rows.shape[3]
    return pl.pallas_call(
        _compress_kernel,
        grid=(B, 4),
        in_specs=[
            pl.BlockSpec((1, 1, R, C), lambda b, p: (b, p, 0, 0)),
            pl.BlockSpec((1, 2, 1, C), lambda b, p: (p // 2, 0, 0, 0)),
            pl.BlockSpec((1, 2, C, HEAD_DIM), lambda b, p: (p // 2, 0, 0, 0)),
            pl.BlockSpec((1, HEAD_DIM, LANES), lambda b, p: (p // 2, 0, 0)),
            pl.BlockSpec((1, R, LANES), lambda b, p: (p // 2, 0, 0)),
        ],
        out_specs=pl.BlockSpec((1, 1, R, LANES), lambda b, p: (b, p, 0, 0)),
        out_shape=jax.ShapeDtypeStruct((B, 4, R, LANES), BF16),
        compiler_params=_cparams(("parallel", "parallel")),
        name="compress",
    )(kv_rows, pe, w1, w2, e)


def _gate_rows(sm, g, branch):
    col = lax.broadcasted_iota(jnp.int32, sm.shape, 1)
    parts = []
    for hl in range(NSA_HPG):
        want = 3 * (NSA_HPG * g + hl) + branch
        parts.append(jnp.sum(jnp.where(col == want, sm, 0.0), axis=-1, keepdims=True))
    return jnp.concatenate(parts, axis=0)


def _head_tile(y):
    n = y.shape[0] // NSA_HPG
    lane = lax.broadcasted_iota(jnp.int32, (n, LANES), 1)
    hs = [y[i * n:(i + 1) * n] for i in range(NSA_HPG)]
    pairs = [jnp.where(lane < HEAD_DIM, hs[2 * i], pltpu.roll(hs[2 * i + 1], HEAD_DIM, 1)) for i in range(2)]
    return jnp.concatenate(pairs, axis=1)


def _cmp_kernel(q_ref, kc_ref, vc_ref, sm_ref, ovt_ref, oc_ref, selb_ref, flag_ref, imp_sc):
    g = pl.program_id(1)
    step_q0 = pl.program_id(2) * CMP_SUB * Q_TILE
    last_visible = (step_q0 + CMP_SUB * Q_TILE - CMP_LEN) // CMP_STRIDE
    chunks = last_visible // LANES + 1

    def attend(width):
        for sub in range(CMP_SUB):
            rows = pl.ds(sub * Q_TILE, Q_TILE)
            q = q_ref[0, :, rows, :].reshape(NSA_HPG * Q_TILE, LANES)
            oc, imp = _cmp_attend(q, kc_ref[0, 0, 0:width, :], vc_ref[0, 0, 0:width, :], sm_ref[0, rows, :],
                                  ovt_ref[:, 0:width], g, step_q0 + sub * Q_TILE)
            oc_ref[0, rows, :] = oc
            imp_sc[sub] = imp

    for v in range(1, N_CMP_PAD // LANES + 1):
        pl.when(chunks == v)(functools.partial(attend, v * LANES))

    for sub in range(CMP_SUB):
        selb, flag = _select_blocks(imp_sc[sub], step_q0 + sub * Q_TILE)
        selb_ref[0, 0, pl.ds(sub * Q_TILE, Q_TILE), :] = selb
        flag_ref[0, 0, sub] = flag


def _cmp_attend(q, kc, vc, sm, ovt, g, q0):
    width = kc.shape[0]
    s = _dot_nt(q, kc)
    r = lax.broadcasted_iota(jnp.int32, (NSA_HPG * Q_TILE, 1), 0) % Q_TILE
    n = lax.broadcasted_iota(jnp.int32, (1, width), 1)
    dc = (q0 + r) - (n * CMP_STRIDE + (CMP_LEN - 1))
    mask = (dc >= 0) & (n < N_CMP_PAD - 1)
    l = jnp.where(mask, s, NEG)
    m = jnp.max(l, axis=-1, keepdims=True)
    e = jnp.where(mask, jnp.exp2(l - m), 0.0)
    pc = e / jnp.maximum(jnp.sum(e, axis=-1, keepdims=True), 1e-30)
    oc = _dot(pc.astype(BF16), vc)
    oc = _head_tile(oc * _gate_rows(sm, g, 0))
    ps = pc[0:Q_TILE]
    for i in range(1, NSA_HPG):
        ps = ps + pc[i * Q_TILE:(i + 1) * Q_TILE]
    ps_hi = ps.astype(BF16)
    ps_lo = (ps - ps_hi.astype(F32)).astype(BF16)
    return oc, _dot_nt(ovt, ps_hi) + _dot_nt(ovt, ps_lo)


def _select_blocks(imp, q0):
    j = lax.broadcasted_iota(jnp.int32, imp.shape, 0)
    jf = j.astype(F32)
    t = q0 + lax.broadcasted_iota(jnp.int32, (1, Q_TILE), 1)
    cur = t // SEL_LEN
    forced = (j == 0) | (j == cur) | (j == cur - 1)
    v = jnp.where(j > cur, -FORCE, jnp.where(forced, FORCE, imp))
    sel = jnp.zeros(imp.shape, jnp.bool_)
    for _ in range(N_SEL):
        mx = jnp.max(v, axis=0, keepdims=True)
        idx = jnp.min(jnp.where(v == mx, jf, float(LANES)), axis=0, keepdims=True)
        pick = jf == idx
        sel = sel | pick
        v = jnp.where(pick, PICKED, v)
    live_t = jnp.where(sel & (j <= cur), 1.0, 0.0).astype(BF16)
    eye = (lax.broadcasted_iota(jnp.int32, imp.shape, 0) == lax.broadcasted_iota(jnp.int32, imp.shape, 1))
    live = _dot_nt(eye.astype(BF16), live_t)
    selb = jnp.where(live > 0.5, 0.0, NEG).astype(BF16)
    return selb, jnp.max(live, axis=0, keepdims=True).astype(jnp.int32)


def _cmp_attention(qa, kvc, sm, ov):
    B, H, S, _ = qa.shape
    G = NSA_KV_GROUPS
    nq = S // Q_TILE
    qt = CMP_SUB * Q_TILE
    return pl.pallas_call(
        _cmp_kernel,
        grid=(B, G, nq // CMP_SUB),
        in_specs=[
            pl.BlockSpec((1, NSA_HPG, qt, LANES), lambda b, g, i: (b, g, i, 0)),
            pl.BlockSpec((1, 1, N_CMP_PAD, LANES), lambda b, g, i: (b, g, 0, 0)),
            pl.BlockSpec((1, 1, N_CMP_PAD, LANES), lambda b, g, i: (b, 2 + g, 0, 0)),
            pl.BlockSpec((1, qt, LANES), lambda b, g, i: (b, i, 0)),
            pl.BlockSpec((LANES, N_CMP_PAD), lambda b, g, i: (0, 0)),
        ],
        out_specs=[
            pl.BlockSpec((1, qt, NSA_HPG * HEAD_DIM), lambda b, g, i: (b, i, g)),
            pl.BlockSpec((1, 1, qt, LANES), lambda b, g, i: (b, g, i, 0)),
            pl.BlockSpec((1, 1, CMP_SUB, 1, LANES), lambda b, g, i: (b, g, i, 0, 0)),
        ],
        out_shape=[
            jax.ShapeDtypeStruct((B, S, NSA_W), F32),
            jax.ShapeDtypeStruct((B, G, S, LANES), BF16),
            jax.ShapeDtypeStruct((B, G, nq, 1, LANES), jnp.int32),
        ],
        scratch_shapes=[pltpu.VMEM((CMP_SUB, LANES, Q_TILE), F32)],
        compiler_params=_cparams(("parallel", "parallel", "parallel")),
        name="cmp_attention",
    )(qa, kvc, kvc, sm, ov)


def _online_update(s, v, m_ref, acc_ref):
    m_old = m_ref[...]
    m_new = jnp.maximum(m_old, jnp.max(s, axis=-1, keepdims=True))
    chunks = [s[:, c * LANES:(c + 1) * LANES] - m_new for c in range(s.shape[1] // LANES)]
    p = jnp.exp2(jnp.concatenate(chunks, axis=1))
    acc_ref[...] = jnp.exp2(m_old - m_new) * acc_ref[...] + _dot(p.astype(BF16), v)
    m_ref[...] = m_new


def _normalized(acc):
    return acc / jnp.maximum(acc[:, EXT:EXT + 1], 1e-30)


def _attend_once(s, v):
    m = jnp.broadcast_to(jnp.max(s, axis=-1, keepdims=True), (s.shape[0], LANES))
    chunks = [s[:, c * LANES:(c + 1) * LANES] - m for c in range(s.shape[1] // LANES)]
    p = jnp.exp2(jnp.concatenate(chunks, axis=1))
    return _normalized(_dot(p.astype(BF16), v))


def _selwin_kernel(list_ref, cnt_ref, q_ref, ks_ref, vs_ref, kw_ref, vw_ref, selb_ref, oc_ref, sm_ref,
                   o_ref, m_a, acc_a, m_b, acc_b):
    b = pl.program_id(0)
    g = pl.program_id(1)
    qb = pl.program_id(2)
    nq = pl.num_programs(2)
    rows = NSA_HPG * SW_TILE
    q4 = q_ref[0].reshape(rows, LANES)
    q_aug = jnp.concatenate([q4, jnp.concatenate([selb_ref[0, 0]] * NSA_HPG, axis=0)], axis=1)
    r = lax.broadcasted_iota(jnp.int32, (rows, 1), 0) % SW_TILE
    c = lax.broadcasted_iota(jnp.int32, (1, K_TILE), 1)
    rel = r - c
    diag = qb * (SW_TILE // K_TILE)

    def sel_tile(kt, m_ref, acc_ref, causal=False, bias=None):
        start = pl.multiple_of(kt * K_TILE, K_TILE)
        s = _dot_nt(q_aug, ks_ref[0, 0, pl.ds(start, K_TILE), :])
        if bias is not None:
            s = s + bias
        if causal:
            s = jnp.where(rel + (qb * SW_TILE - kt * K_TILE) >= 0, s, NEG)
        _online_update(s, vs_ref[0, 0, pl.ds(start, K_TILE), :], m_ref, acc_ref)

    for m_ref, acc_ref in ((m_a, acc_a), (m_b, acc_b)):
        m_ref[...] = jnp.full(m_ref.shape, NEG, F32)
        acc_ref[...] = jnp.zeros(acc_ref.shape, F32)
    step = (b * NSA_KV_GROUPS + g) * nq + qb
    count = cnt_ref[step]
    base = step * MAX_TILES

    def body(p, carry):
        sel_tile(list_ref[base + 2 * p], m_a, acc_a)
        sel_tile(list_ref[base + 2 * p + 1], m_b, acc_b)
        return carry

    lax.fori_loop(0, count // 2, body, 0)

    @pl.when(count % 2 == 1)
    def _():
        sel_tile(list_ref[base + count - 1], m_a, acc_a)

    m_new = jnp.maximum(m_a[...], m_b[...])
    acc_a[...] = jnp.exp2(m_a[...] - m_new) * acc_a[...] + jnp.exp2(m_b[...] - m_new) * acc_b[...]
    m_a[...] = m_new
    for d in range(SW_TILE // K_TILE):
        sel_tile(diag + d, m_a, acc_a, causal=True)
    o_sel = _normalized(acc_a[...])

    span = WINDOW + SW_TILE
    wstart = pl.multiple_of(jnp.maximum(qb * SW_TILE - WINDOW, 0), K_TILE)
    dist = (qb * SW_TILE + r) - (wstart + lax.broadcasted_iota(jnp.int32, (1, span), 1))
    s = _dot_nt(q4, kw_ref[0, 0, pl.ds(wstart, span), :])
    s = jnp.where((dist >= 0) & (dist < WINDOW), s, NEG)
    o_win = _attend_once(s, vw_ref[0, 0, pl.ds(wstart, span), :])

    sm = sm_ref[0]
    y = _gate_rows(sm, g, 1) * o_sel + _gate_rows(sm, g, 2) * o_win
    o_ref[0] = (oc_ref[0] + _head_tile(y)).astype(BF16)


def _selwin_attention(tile_list, tile_count, qa, ksl, nkv, selb, ocg, sm):
    B, H, S, _ = qa.shape
    G = NSA_KV_GROUPS
    nq = S // SW_TILE
    rows = NSA_HPG * SW_TILE
    kv_spec = lambda piece: pl.BlockSpec((1, 1, S, LANES), lambda b, g, i, tl, tc: (b, piece + g, 0, 0))
    out_tile = pl.BlockSpec((1, SW_TILE, NSA_HPG * HEAD_DIM), lambda b, g, i, tl, tc: (b, i, g))
    grid_spec = pltpu.PrefetchScalarGridSpec(
        num_scalar_prefetch=2,
        grid=(B, G, nq),
        in_specs=[
            pl.BlockSpec((1, NSA_HPG, SW_TILE, LANES), lambda b, g, i, tl, tc: (b, g, i, 0)),
            pl.BlockSpec((1, 1, S, 2 * LANES), lambda b, g, i, tl, tc: (b, g, 0, 0)),
            kv_spec(0), kv_spec(2), kv_spec(4),
            pl.BlockSpec((1, 1, SW_TILE, LANES), lambda b, g, i, tl, tc: (b, g, i, 0)),
            out_tile,
            pl.BlockSpec((1, SW_TILE, LANES), lambda b, g, i, tl, tc: (b, i, 0)),
        ],
        out_specs=out_tile,
        scratch_shapes=[pltpu.VMEM((rows, LANES), F32)] * 4,
    )
    return pl.pallas_call(
        _selwin_kernel,
        grid_spec=grid_spec,
        out_shape=jax.ShapeDtypeStruct((B, S, NSA_W), BF16),
        compiler_params=_cparams(("parallel", "parallel", "arbitrary")),
        name="selwin_attention",
    )(tile_list, tile_count, qa, ksl, nkv, nkv, nkv, selb, ocg, sm)


def _fox_kernel(q_ref, k_ref, v_ref, o_ref, m_sc, acc_sc, *, tq):
    qi = pl.program_id(2)
    m_sc[...] = jnp.full(m_sc.shape, NEG, F32)
    acc_sc[...] = jnp.zeros(acc_sc.shape, F32)

    def tile(kt, width, causal):
        start = pl.multiple_of(kt * tq, tq)
        for hh in range(FOX_HPS):
            s = _dot_nt(q_ref[0, hh], k_ref[0, hh, pl.ds(start, width), :])
            if causal:
                r = lax.broadcasted_iota(jnp.int32, s.shape, 0)
                c = lax.broadcasted_iota(jnp.int32, s.shape, 1)
                s = jnp.where(r + (width - tq) >= c, s, NEG)
            _online_update(s, v_ref[0, hh, pl.ds(start, width), :], m_sc.at[hh], acc_sc.at[hh])

    def body(kp, carry):
        tile(2 * kp, 2 * tq, False)
        return carry

    lax.fori_loop(0, qi // 2, body, 0)

    @pl.when(qi % 2 == 1)
    def _():
        tile(qi - 1, 2 * tq, True)

    @pl.when(qi % 2 == 0)
    def _():
        tile(qi, tq, True)

    lane = lax.broadcasted_iota(jnp.int32, (tq, LANES), 1)
    o = [_normalized(acc_sc[hh]) for hh in range(FOX_HPS)]
    for pr in range(FOX_HPS // 2):
        o_ref[0, :, pr * LANES:(pr + 1) * LANES] = jnp.where(
            lane < HEAD_DIM, o[2 * pr], pltpu.roll(o[2 * pr + 1], HEAD_DIM, 1)).astype(BF16)


def _fox_attention(fq, fk, fv, tq=512):
    B, H, S, _ = fq.shape
    hps = FOX_HPS
    return pl.pallas_call(
        functools.partial(_fox_kernel, tq=tq),
        grid=(B, H // hps, S // tq),
        in_specs=[
            pl.BlockSpec((1, hps, tq, LANES), lambda b, h, i: (b, h, i, 0)),
            pl.BlockSpec((1, hps, S, LANES), lambda b, h, i: (b, h, 0, 0)),
            pl.BlockSpec((1, hps, S, LANES), lambda b, h, i: (b, h, 0, 0)),
        ],
        out_specs=pl.BlockSpec((1, tq, hps * HEAD_DIM), lambda b, h, i: (b, i, h)),
        out_shape=jax.ShapeDtypeStruct((B, S, FOX_W), BF16),
        scratch_shapes=[
            pltpu.VMEM((hps, tq, LANES), F32),
            pltpu.VMEM((hps, tq, LANES), F32),
        ],
        compiler_params=_cparams(("parallel", "parallel", "arbitrary")),
        name="fox_attention",
    )(fq, fk, fv)


def _merge_kernel(ya_ref, yb_ref, mg_ref, x_ref, mod_ref, gpost_ref, gpre_ref,
                  wa_ref, wb_ref, wo_ref, wrh_ref, wrl_ref, br_ref, stri_ref,
                  x1_ref, h2_ref, rt_ref, cnt_ref):
    D = D_MODEL

    @pl.when((pl.program_id(0) == 0) & (pl.program_id(1) == 0))
    def _():
        cnt_ref[...] = jnp.zeros(cnt_ref.shape, F32)

    a = _dot(ya_ref[0], wa_ref[...])
    bq = _dot(yb_ref[0], wb_ref[...])
    mg = mg_ref[0]
    u = mg[:, :D].astype(F32) * a + mg[:, D:].astype(F32) * bq
    mixed = _dot(u.astype(BF16), wo_ref[...])
    x1 = x_ref[0] + mod_ref[0, 2:3, :] * _rms(mixed, gpost_ref[...])
    x1_ref[0] = x1
    h2 = _rms(x1, gpre_ref[...]) * (1.0 + mod_ref[0, 4:5, :]) + mod_ref[0, 3:4, :]
    hi = h2.astype(BF16)
    lo = (h2 - hi.astype(F32)).astype(BF16)
    h2_ref[0] = h2
    lg = _dot(hi, wrh_ref[...]) + _dot(lo, wrh_ref[...]) + _dot(hi, wrl_ref[...]) + br_ref[...]

    lane = lax.broadcasted_iota(jnp.int32, lg.shape, 1)
    lanef = lane.astype(F32)
    no_lane = float(LANES)
    is_g = lane < N_EXPERT_GROUPS
    gl = jnp.where(is_g, lg, NEG)
    gmax = jnp.max(gl, axis=-1, keepdims=True)
    pg_top = 1.0 / jnp.sum(jnp.where(is_g, jnp.exp(gl - gmax), 0.0), axis=-1, keepdims=True)
    g_idx = jnp.min(jnp.where(is_g & (gl == gmax), lanef, no_lane), axis=-1, keepdims=True)
    in_grp = ((lane >= N_EXPERT_GROUPS) & (lane < N_EXPERT_GROUPS + N_EXPERTS)
              & (((lane - N_EXPERT_GROUPS) // EXPERTS_PER_GROUP).astype(F32) == g_idx))
    le = jnp.where(in_grp, lg, NEG)
    m1 = jnp.max(le, axis=-1, keepdims=True)
    i1 = jnp.min(jnp.where(in_grp & (le == m1), lanef, no_lane), axis=-1, keepdims=True)
    rest = in_grp & (lanef != i1)
    le2 = jnp.where(rest, lg, NEG)
    m2 = jnp.max(le2, axis=-1, keepdims=True)
    i2 = jnp.min(jnp.where(rest & (le2 == m2), lanef, no_lane), axis=-1, keepdims=True)
    e21 = jnp.exp(m2 - m1)
    w1 = pg_top / (1.0 + e21)
    w2 = w1 * e21
    pick1 = lanef == i1
    pick2 = lanef == i2
    onehot = jnp.where(pick1 | pick2, 1.0, 0.0)
    before = cnt_ref[...] + _dot(stri_ref[...], onehot.astype(BF16))
    rank1 = jnp.sum(jnp.where(pick1, before, 0.0), axis=-1, keepdims=True)
    rank2 = jnp.sum(jnp.where(pick2, before, 0.0), axis=-1, keepdims=True)
    cnt_ref[...] = cnt_ref[...] + jnp.sum(onehot, axis=0, keepdims=True)
    fields = {RT_EXPERT: i1 - N_EXPERT_GROUPS, RT_EXPERT + 1: i2 - N_EXPERT_GROUPS,
              RT_RANK: rank1, RT_RANK + 1: rank2, RT_WEIGHT: w1, RT_WEIGHT + 1: w2}
    rt = jnp.zeros(lg.shape, F32)
    for k, f in fields.items():
        rt = jnp.where(lane == k, f, rt)
    rt_ref[0] = rt


def _merge(ya, yb, mg, x, mod, gpost, gpre, wa, wb, wo, wrh, wrl, br, stri):
    B, S, D = x.shape
    tm = MERGE_TILE
    c2 = lambda b, i: (0, 0)
    row = lambda w: pl.BlockSpec((1, tm, w), lambda b, i: (b, i, 0))
    return pl.pallas_call(
        _merge_kernel,
        grid=(B, S // tm),
        in_specs=[
            row(NSA_W), row(FOX_W), row(2 * D), row(D),
            pl.BlockSpec((1, 6, D), lambda b, i: (b, 0, 0)),
            pl.BlockSpec((1, D), c2), pl.BlockSpec((1, D), c2),
            pl.BlockSpec((NSA_W, D), c2), pl.BlockSpec((FOX_W, D), c2), pl.BlockSpec((D, D), c2),
            pl.BlockSpec((D, LANES), c2), pl.BlockSpec((D, LANES), c2), pl.BlockSpec((1, LANES), c2),
            pl.BlockSpec((tm, tm), c2),
        ],
        out_specs=[row(D), row(D), row(LANES), pl.BlockSpec((1, LANES), c2)],
        out_shape=[
            jax.ShapeDtypeStruct((B, S, D), F32),
            jax.ShapeDtypeStruct((B, S, D), F32),
            jax.ShapeDtypeStruct((B, S, LANES), F32),
            jax.ShapeDtypeStruct((1, LANES), F32),
        ],
        compiler_params=_cparams(("arbitrary", "arbitrary")),
        name="merge",
    )(ya, yb, mg, x, mod, gpost, gpre, wa, wb, wo, wrh, wrl, br, stri)


def _expert_kernel(be_ref, na_ref, tok_ref, h_hbm, wg_ref, wu_ref, wd_ref, o_ref, *scratch):
    i = pl.program_id(0)
    n_active = na_ref[0]
    last_block = pl.num_programs(0) - 1
    ring = GATHER_AHEAD + 1
    bufs = scratch[:ring]
    wg_b, wu_b, wd_b, sem = scratch[ring:]

    def row_copy(blk, r, sl):
        tok = tok_ref[blk * MOE_TILE + r]
        return pltpu.make_async_copy(h_hbm.at[pl.ds(tok, 1)], bufs[sl].at[pl.ds(r, 1)], sem.at[sl])

    def wait_rows(sl):
        pltpu.make_async_copy(h_hbm.at[pl.ds(0, MOE_TILE)], bufs[sl], sem.at[sl]).wait()

    @pl.when(i == 0)
    def _():
        for ahead in range(GATHER_AHEAD):
            def body(r, carry, ahead=ahead):
                row_copy(jnp.minimum(ahead, last_block), r, ahead).start()
                return carry
            lax.fori_loop(0, MOE_TILE, body, 0, unroll=8)

    @pl.when((i == 0) | (be_ref[i] != be_ref[jnp.maximum(i - 1, 0)]))
    def _():
        wg_b[...] = wg_ref[0].astype(BF16)
        wu_b[...] = wu_ref[0].astype(BF16)
        wd_b[...] = wd_ref[0].astype(BF16)

    def step(sl):
        wait_rows(sl)
        nxt = jnp.minimum(i + GATHER_AHEAD, last_block)
        nxt_sl = (sl + GATHER_AHEAD) % ring
        for r in range(MOE_TILE):
            row_copy(nxt, r, nxt_sl).start(priority=r % 2)
        x = bufs[sl][...].astype(BF16)
        gate = _dot(x, wg_b[...])
        up = _dot(x, wu_b[...])
        mid = (gate * jax.nn.sigmoid(gate) * up).astype(BF16)
        o_ref[...] = _dot(mid, wd_b[...])

        @pl.when(i == n_active - 1)
        def _():
            for ahead in range(1, GATHER_AHEAD + 1):
                wait_rows((sl + ahead) % ring)

    for sl in range(ring):
        pl.when((i % ring == sl) & (i < n_active))(functools.partial(step, sl))

    @pl.when(i >= n_active)
    def _():
        o_ref[...] = jnp.zeros(o_ref.shape, o_ref.dtype)


def _experts(block_expert, n_active, buf_tok, h2, wg, wu, wd):
    cap = buf_tok.shape[0]
    D = D_MODEL
    nblk = cap // MOE_TILE
    grid_spec = pltpu.PrefetchScalarGridSpec(
        num_scalar_prefetch=3,
        grid=(nblk,),
        in_specs=[
            pl.BlockSpec(memory_space=pl.ANY),
            pl.BlockSpec((1, D, D_EXPERT), lambda i, be, na, tok: (be[i], 0, 0)),
            pl.BlockSpec((1, D, D_EXPERT), lambda i, be, na, tok: (be[i], 0, 0)),
            pl.BlockSpec((1, D_EXPERT, D), lambda i, be, na, tok: (be[i], 0, 0)),
        ],
        out_specs=pl.BlockSpec((MOE_TILE, D), lambda i, be, na, tok: (i, 0)),
        scratch_shapes=[
            *([pltpu.VMEM((MOE_TILE, D), F32)] * (GATHER_AHEAD + 1)),
            pltpu.VMEM((D, D_EXPERT), BF16),
            pltpu.VMEM((D, D_EXPERT), BF16),
            pltpu.VMEM((D_EXPERT, D), BF16),
            pltpu.SemaphoreType.DMA((GATHER_AHEAD + 1,)),
        ],
    )
    return pl.pallas_call(
        _expert_kernel,
        grid_spec=grid_spec,
        out_shape=jax.ShapeDtypeStruct((cap, D), F32),
        compiler_params=_cparams(("arbitrary",)),
        name="experts",
    )(block_expert, n_active, buf_tok, h2, wg, wu, wd)


def _final_kernel(dest_ref, x1_ref, rt_ref, mod_ref, g_ref, y_hbm, o_ref, *scratch):
    j = pl.program_id(0)
    last_tile = pl.num_programs(0) - 1
    tm = o_ref.shape[0]
    ring = GATHER_AHEAD + 1
    bufs = tuple(scratch[EXPERT_TOP_K * sl:EXPERT_TOP_K * (sl + 1)] for sl in range(ring))
    sem = scratch[-1]

    def row_copy(tile, r, k, sl):
        row = dest_ref[(tile * tm + r) * EXPERT_TOP_K + k]
        return pltpu.make_async_copy(y_hbm.at[pl.ds(row, 1)], bufs[sl][k].at[pl.ds(r, 1)], sem.at[sl])

    def wait_rows(sl):
        for k in range(EXPERT_TOP_K):
            pltpu.make_async_copy(y_hbm.at[pl.ds(0, tm)], bufs[sl][k], sem.at[sl]).wait()

    @pl.when(j == 0)
    def _():
        for ahead in range(GATHER_AHEAD):
            def body(r, carry, ahead=ahead):
                for k in range(EXPERT_TOP_K):
                    row_copy(jnp.minimum(ahead, last_tile), r, k, ahead).start()
                return carry
            lax.fori_loop(0, tm, body, 0, unroll=4)

    def step(sl):
        wait_rows(sl)
        nxt = jnp.minimum(j + GATHER_AHEAD, last_tile)
        nxt_sl = (sl + GATHER_AHEAD) % ring
        for r in range(tm):
            for k in range(EXPERT_TOP_K):
                row_copy(nxt, r, k, nxt_sl).start(priority=k)
        rt = rt_ref[...]
        lane = lax.broadcasted_iota(jnp.int32, rt.shape, 1)
        w0 = jnp.sum(jnp.where(lane == RT_WEIGHT, rt, 0.0), axis=-1, keepdims=True)
        w1 = jnp.sum(jnp.where(lane == RT_WEIGHT + 1, rt, 0.0), axis=-1, keepdims=True)
        y = w0 * bufs[sl][0][...] + w1 * bufs[sl][1][...]
        o_ref[...] = x1_ref[...] + mod_ref[0, 5:6, :] * _rms(y, g_ref[...])

        @pl.when(j == last_tile)
        def _():
            for ahead in range(1, GATHER_AHEAD + 1):
                wait_rows((sl + ahead) % ring)

    for sl in range(ring):
        pl.when(j % ring == sl)(functools.partial(step, sl))


def _final(dest, x1, rt, mod, g, yb, tiles_per_batch):
    T, D = x1.shape
    tm = FINAL_TILE
    grid_spec = pltpu.PrefetchScalarGridSpec(
        num_scalar_prefetch=1,
        grid=(T // tm,),
        in_specs=[
            pl.BlockSpec((tm, D), lambda j, d: (j, 0)),
            pl.BlockSpec((tm, LANES), lambda j, d: (j, 0)),
            pl.BlockSpec((1, 6, D), lambda j, d: (j // tiles_per_batch, 0, 0)),
            pl.BlockSpec((1, D), lambda j, d: (0, 0)),
            pl.BlockSpec(memory_space=pl.ANY),
        ],
        out_specs=pl.BlockSpec((tm, D), lambda j, d: (j, 0)),
        scratch_shapes=([pltpu.VMEM((tm, D), F32)] * (EXPERT_TOP_K * (GATHER_AHEAD + 1))
                        + [pltpu.SemaphoreType.DMA((GATHER_AHEAD + 1,))]),
    )
    return pl.pallas_call(
        _final_kernel,
        grid_spec=grid_spec,
        out_shape=jax.ShapeDtypeStruct((T, D), F32),
        compiler_params=_cparams(("arbitrary",)),
        name="final",
    )(dest, x1, rt, mod, g, yb)


def _overlap_matrix():
    n = np.arange(N_CMP_PAD)[:, None]
    j = np.arange(LANES)[None, :]
    start = n * CMP_STRIDE
    ov = (start < j * SEL_LEN + SEL_LEN) & (start + CMP_LEN - 1 >= j * SEL_LEN) & (n < N_CMP_PAD - 1)
    return jnp.asarray(ov.T.astype(np.float32), dtype=BF16)


def _pad_cols(w, width=LANES):
    return jnp.pad(w, ((0, 0), (0, width - w.shape[1])))


def _dispatch_plan(rt, cnt, T):
    expert = rt[:, RT_EXPERT:RT_EXPERT + EXPERT_TOP_K].astype(jnp.int32)
    rank = rt[:, RT_RANK:RT_RANK + EXPERT_TOP_K].astype(jnp.int32)
    weight = rt[:, RT_WEIGHT:RT_WEIGHT + EXPERT_TOP_K]
    counts = cnt[0, N_EXPERT_GROUPS:N_EXPERT_GROUPS + N_EXPERTS].astype(jnp.int32)
    padded = (counts + MOE_TILE - 1) // MOE_TILE * MOE_TILE
    pad_end = jnp.cumsum(padded)
    pad_start = pad_end - padded
    onehot = expert[:, :, None] == jnp.arange(N_EXPERTS)[None, None, :]
    dest = jnp.sum(jnp.where(onehot, pad_start[None, None, :], 0), axis=-1) + rank
    A = T * EXPERT_TOP_K
    cap = -(-(A + N_EXPERTS * (MOE_TILE - 1)) // MOE_TILE) * MOE_TILE
    nblk = cap // MOE_TILE
    n_active = (pad_end[-1] // MOE_TILE).astype(jnp.int32)
    blk = jnp.arange(nblk) * MOE_TILE
    block_expert = jnp.minimum(jnp.sum(pad_end[None, :] <= blk[:, None], axis=1), N_EXPERTS - 1)
    last = jnp.max(jnp.where(jnp.arange(nblk) < n_active, block_expert, 0))
    block_expert = jnp.where(jnp.arange(nblk) < n_active, block_expert, last).astype(jnp.int32)
    tok = jnp.arange(A, dtype=jnp.int32) // EXPERT_TOP_K
    buf_tok = jnp.zeros((cap,), jnp.int32).at[dest.reshape(A)].set(tok)
    return weight, dest, buf_tok, block_expert, n_active.reshape(1)


def kernel(x, c, w_ada, b_ada, g_pre_mix, g_post_mix, g_pre_ffn, g_post_ffn, w_in, b_forget,
           cmp_pe_k, cmp_w1_k, cmp_w2_k, cmp_pe_v, cmp_w1_v, cmp_w2_v,
           w_o_nsa, w_o_fox, w_out, w_router_group, b_router_group, w_router_expert, b_router_expert,
           w_exp_gate, w_exp_up, w_exp_down):
    B, S, D = x.shape
    T = B * S
    depth = w_ada.shape[0]
    ov = _overlap_matrix()
    tri = jnp.asarray(np.tril(np.ones((IN_TILE, IN_TILE), np.float32)), dtype=BF16)
    stri = jnp.asarray(np.tril(np.ones((MERGE_TILE, MERGE_TILE), np.float32), -1), dtype=BF16)
    row_feat = _row_features(S)
    placement = _placement()
    cmp_ext = _cmp_key_ext()
    for l in range(depth):
        mod = _adaln(c, w_ada[l], b_ada[l].reshape(1, 6 * D)).reshape(B, 6, D)
        w_qa, w_kva, w_gl, w_fox, w_f, w_mg = jnp.split(w_in[l], IN_SPLITS, axis=-1)
        w_big = jnp.concatenate([w_qa, w_kva, w_fox, w_mg], axis=1).astype(BF16)
        w_small = _pad_cols(jnp.concatenate([w_gl, w_f], axis=1)).astype(BF16)
        bf_pad = jnp.pad(b_forget[l], (F_LANE, LANES - F_LANE - FOX_HEADS)).reshape(1, LANES)
        qa, ckv, ksl, nkv, fq, fk, fv, mg, sm = _inproj(
            x, mod, g_pre_mix[l].reshape(1, D), w_big, w_small, bf_pad, tri, row_feat, placement)

        half = CMP_LEN // 2
        pe = jnp.stack([cmp_pe_k[l], cmp_pe_v[l]]).reshape(2, 2, 1, half * HEAD_DIM)
        w1 = jnp.stack([cmp_w1_k[l], cmp_w1_v[l]]).reshape(2, 2, half * HEAD_DIM, HEAD_DIM).astype(BF16)
        w2 = jnp.pad(jnp.stack([cmp_w2_k[l], cmp_w2_v[l]]), ((0, 0), (0, 0), (0, LANES - HEAD_DIM))).astype(BF16)
        kvc = _compress(ckv.reshape(B, 4, S // CMP_STRIDE, CMP_STRIDE * HEAD_DIM), pe, w1, w2, cmp_ext)
        ocg, selb, flags = _cmp_attention(qa, kvc, sm, ov)
        nq = S // SW_TILE
        per_tile = K_TILE // SEL_LEN
        tile_any = jnp.max(flags.reshape(B, NSA_KV_GROUPS, nq, SW_TILE // Q_TILE, MAX_TILES, per_tile), axis=(3, 5))
        tile_id = jnp.arange(MAX_TILES)
        diag = (jnp.arange(nq) * (SW_TILE // K_TILE))[:, None]
        active = (tile_any > 0) & (tile_id < diag)
        slot = jnp.cumsum(active, axis=-1) - 1
        hit = active[..., :, None] & (slot[..., :, None] == tile_id)
        tile_list = jnp.sum(jnp.where(hit, tile_id[:, None], 0), axis=-2).astype(jnp.int32).reshape(-1)
        tile_count = jnp.sum(active, axis=-1).astype(jnp.int32).reshape(-1)
        y_a = _selwin_attention(tile_list, tile_count, qa, ksl, nkv, selb, ocg, sm)

        y_b = _fox_attention(fq, fk, fv)

        w_r = _pad_cols(jnp.concatenate([w_router_group[l], w_router_expert[l]], axis=1))
        w_rh = w_r.astype(BF16)
        w_rl = (w_r - w_rh.astype(F32)).astype(BF16)
        b_r = _pad_cols(jnp.concatenate([b_router_group[l], b_router_expert[l]]).reshape(1, -1))
        x1, h2, rt, cnt = _merge(y_a, y_b, mg, x, mod, g_post_mix[l].reshape(1, D), g_pre_ffn[l].reshape(1, D),
                                 w_o_nsa[l].astype(BF16), w_o_fox[l].astype(BF16), w_out[l].astype(BF16),
                                 w_rh, w_rl, b_r, stri)

        weight, dest, buf_tok, block_expert, n_active = _dispatch_plan(rt.reshape(T, LANES), cnt, T)
        yb = _experts(block_expert, n_active, buf_tok, h2.reshape(T, D), w_exp_gate[l], w_exp_up[l], w_exp_down[l])
        x = _final(dest.reshape(T * EXPERT_TOP_K), x1.reshape(T, D), rt.reshape(T, LANES), mod,
                   g_post_ffn[l].reshape(1, D), yb, S // FINAL_TILE).reshape(B, S, D)
    return x
```
